```python
import jax, jax.numpy as jnp
from jax import lax
import numpy as np

D_MODEL = 1024
BATCH = 8
SEQ = 2048
DEPTH = 1

M_HEADS = 4
M_DQK = 128
M_DV = 128
M_CHUNK = 64
CONV_W = 4
F_HEADS = 8
F_DH = 64
Q_BLOCK = 128
N_EXPERTS = 32
TOP_K = 4
D_FF = D_MODEL
SWIGLU_ALPHA = 1.702
SWIGLU_LIMIT = 7.0
MOE_BLOCK = 256
DN_ALPHA = (2.0 * DEPTH) ** 0.25
DN_BETA = (8.0 * DEPTH) ** -0.25
LN_EPS = 1e-5

M_W = M_HEADS * M_DV
F_W = F_HEADS * F_DH
MQK_W = 2 * M_HEADS * M_DQK
MQK_OFF = 0
MV_OFF = MQK_OFF + MQK_W
MO_OFF = MV_OFF + M_W
MI_OFF = MO_OFF + M_W
MF_OFF = MI_OFF + M_HEADS
FQ_OFF = MF_OFF + M_HEADS
FK_OFF = FQ_OFF + F_W
FV_OFF = FK_OFF + F_W
FF_OFF = FV_OFF + F_W
GM_OFF = FF_OFF + F_HEADS
GF_OFF = GM_OFF + D_MODEL
IN_COLS = GF_OFF + D_MODEL
SPLIT_IDX = (MV_OFF, MO_OFF, MI_OFF, MF_OFF, FQ_OFF, FK_OFF, FV_OFF, FF_OFF, GM_OFF, GF_OFF)

kernel_name = "hybrid_mlstm_fox_moe_deepnorm"


def layer_norm(x, g, b):
    xf = x.astype(jnp.float32)
    mu = jnp.mean(xf, -1, keepdims=True)
    var = jnp.mean(jnp.square(xf - mu), -1, keepdims=True)
    return ((xf - mu) * lax.rsqrt(var + LN_EPS)).astype(x.dtype) * g + b


def head_layer_norm(h):
    mu = jnp.mean(h, -1, keepdims=True)
    var = jnp.mean(jnp.square(h - mu), -1, keepdims=True)
    return (h - mu) * lax.rsqrt(var + LN_EPS)


def causal_depthwise_conv(u, w, b):
    c = u.shape[-1]
    y = lax.conv_general_dilated(u, w[:, None, :].astype(u.dtype), window_strides=(1,),
                                 padding=[(CONV_W - 1, 0)],
                                 dimension_numbers=("NWC", "WIO", "NWC"),
                                 feature_group_count=c)
    return y + b


def to_heads(t, n_heads):
    bsz, s, _ = t.shape
    return t.reshape(bsz, s, n_heads, -1).transpose(0, 2, 1, 3)


def mlstm_chunkwise(q, k, v, i_pre, logf):
    f32 = jnp.float32
    q, k, v = q.astype(f32), k.astype(f32), v.astype(f32)
    i_pre, logf = i_pre.astype(f32), logf.astype(f32)
    bsz, nh, s, dk = q.shape
    dv = v.shape[-1]
    L = M_CHUNK
    n = s // L
    q = q.reshape(bsz, nh, n, L, dk)
    k = k.reshape(bsz, nh, n, L, dk)
    v = v.reshape(bsz, nh, n, L, dv)
    i_pre = i_pre.reshape(bsz, nh, n, L)
    logf = logf.reshape(bsz, nh, n, L)

    b = jnp.cumsum(logf, axis=-1)
    g = b[..., -1]
    a = g[..., None] - b + i_pre
    m_loc = jnp.max(a, axis=-1)
    causal = jnp.tril(jnp.ones((L, L), dtype=bool))
    dlog = jnp.where(causal, b[..., :, None] - b[..., None, :] + i_pre[..., None, :], -jnp.inf)

    def step(carry, xs):
        c_st, n_st, m_st = carry
        k_c, v_c, a_c, g_c, ml_c = xs
        m_new = jnp.maximum(g_c + m_st, ml_c)
        decay = jnp.exp(g_c + m_st - m_new)
        w = jnp.exp(a_c - m_new[..., None])
        c_new = decay[..., None, None] * c_st + jnp.einsum('bhl,bhlk,bhlv->bhkv', w, k_c, v_c)
        n_new = decay[..., None] * n_st + jnp.einsum('bhl,bhlk->bhk', w, k_c)
        return (c_new, n_new, m_new), (c_st, n_st, m_st)

    init = (jnp.zeros((bsz, nh, dk, dv), f32), jnp.zeros((bsz, nh, dk), f32), jnp.zeros((bsz, nh), f32))
    xs = (jnp.moveaxis(k, 2, 0), jnp.moveaxis(v, 2, 0), jnp.moveaxis(a, 2, 0),
          jnp.moveaxis(g, 2, 0), jnp.moveaxis(m_loc, 2, 0))
    _, (c_prev, n_prev, m_prev) = lax.scan(step, init, xs)
    c_prev = jnp.moveaxis(c_prev, 0, 2)
    n_prev = jnp.moveaxis(n_prev, 0, 2)
    m_prev = jnp.moveaxis(m_prev, 0, 2)

    inter_log = b + m_prev[..., None]
    m_t = jnp.maximum(inter_log, jnp.max(dlog, axis=-1))
    w_inter = jnp.exp(inter_log - m_t)
    w_intra = jnp.exp(dlog - m_t[..., None])
    qk = jnp.einsum('bhntd,bhnsd->bhnts', q, k) * w_intra
    num = (w_inter[..., None] * jnp.einsum('bhntk,bhnkv->bhntv', q, c_prev)
           + jnp.einsum('bhnts,bhnsv->bhntv', qk, v))
    den = w_inter * jnp.einsum('bhntk,bhnk->bhnt', q, n_prev) + jnp.sum(qk, axis=-1)
    h = num / jnp.maximum(jnp.abs(den), jnp.exp(-m_t))[..., None]
    return h.reshape(bsz, nh, s, dv)


def forgetting_attention(q, k, v, logf):
    f32 = jnp.float32
    s_len = q.shape[2]
    scale = q.shape[-1] ** -0.5
    c = jnp.cumsum(logf.astype(f32), axis=-1)
    outs = []
    for blk in range(s_len // Q_BLOCK):
        q0 = blk * Q_BLOCK
        q1 = q0 + Q_BLOCK
        sc = jnp.einsum('bhtd,bhsd->bhts', q[:, :, q0:q1], k[:, :, :q1],
                        preferred_element_type=f32) * scale
        sc = sc + c[:, :, q0:q1, None] - c[:, :, None, :q1]
        mask = (q0 + jnp.arange(Q_BLOCK))[:, None] >= jnp.arange(q1)[None, :]
        p = jax.nn.softmax(jnp.where(mask, sc, -jnp.inf), axis=-1)
        outs.append(jnp.einsum('bhts,bhsd->bhtd', p.astype(v.dtype), v[:, :, :q1]))
    return jnp.concatenate(outs, axis=2)


def hybrid_mixer(h, w_in, b_in, m_conv_w, m_conv_b, m_norm_g, w_bm, w_bf, w_o):
    bsz, s, _ = h.shape
    proj = jnp.einsum('bsd,dc->bsc', h, w_in) + b_in
    mqk, mv, mo, mi, mf, fq, fk, fv, ff, gm, gf = jnp.split(proj, SPLIT_IDX, axis=-1)

    mqk = jax.nn.silu(causal_depthwise_conv(mqk, m_conv_w, m_conv_b))
    mq, mk = jnp.split(mqk, 2, axis=-1)
    hm = mlstm_chunkwise(to_heads(mq, M_HEADS), to_heads(mk, M_HEADS) * (M_DQK ** -0.5),
                         to_heads(mv, M_HEADS),
                         mi.astype(jnp.float32).transpose(0, 2, 1),
                         jax.nn.log_sigmoid(mf.astype(jnp.float32)).transpose(0, 2, 1))
    hm = head_layer_norm(hm).transpose(0, 2, 1, 3).reshape(bsz, s, M_W).astype(h.dtype) * m_norm_g
    hm = jax.nn.sigmoid(mo) * hm

    hf = forgetting_attention(to_heads(fq, F_HEADS), to_heads(fk, F_HEADS), to_heads(fv, F_HEADS),
                              jax.nn.log_sigmoid(ff.astype(jnp.float32)).transpose(0, 2, 1))
    hf = hf.transpose(0, 2, 1, 3).reshape(bsz, s, F_W)

    y = jax.nn.sigmoid(gm) * (hm @ w_bm) + jax.nn.sigmoid(gf) * (hf @ w_bf)
    return y @ w_o


def moe_ffn(h, w_router, b_router, w_gu, b_gu, w_dn, b_dn):
    bsz, s, d = h.shape
    t = bsz * s
    xt = h.reshape(t, d)
    logits = (xt @ w_router + b_router).astype(jnp.float32)
    top_v, top_e = lax.top_k(logits, TOP_K)
    gate = jax.nn.softmax(top_v, axis=-1)

    n_assign = t * TOP_K
    e_flat = top_e.reshape(n_assign).astype(jnp.int32)
    tok_flat = jnp.repeat(jnp.arange(t, dtype=jnp.int32), TOP_K)
    g_flat = gate.reshape(n_assign)
    order = jnp.argsort(e_flat)
    e_sorted = e_flat[order]
    counts = jnp.bincount(e_flat, length=N_EXPERTS).astype(jnp.int32)
    start = jnp.cumsum(counts) - counts
    padded = ((counts + MOE_BLOCK - 1) // MOE_BLOCK) * MOE_BLOCK
    pad_end = jnp.cumsum(padded)
    pad_start = pad_end - padded
    dest = pad_start[e_sorted] + (jnp.arange(n_assign, dtype=jnp.int32) - start[e_sorted])
    n_blocks = -(-n_assign // MOE_BLOCK) + N_EXPERTS
    n_rows = n_blocks * MOE_BLOCK
    row_tok = jnp.full((n_rows,), t, jnp.int32).at[dest].set(tok_flat[order])
    row_gate = jnp.zeros((n_rows,), jnp.float32).at[dest].set(g_flat[order])
    block_e = jnp.minimum(jnp.searchsorted(pad_end, jnp.arange(n_blocks, dtype=jnp.int32) * MOE_BLOCK,
                                           side='right'), N_EXPERTS - 1)

    def expert_block(args):
        tok_b, e = args
        xb = xt[jnp.minimum(tok_b, t - 1)]
        gu = xb @ w_gu[e] + b_gu[e]
        x_glu, x_lin = jnp.split(gu, 2, axis=-1)
        x_glu = jnp.minimum(x_glu, SWIGLU_LIMIT)
        x_lin = jnp.clip(x_lin, -SWIGLU_LIMIT, SWIGLU_LIMIT)
        act = x_glu * jax.nn.sigmoid(SWIGLU_ALPHA * x_glu) * (x_lin + 1.0)
        return act @ w_dn[e] + b_dn[e]

    y_rows = lax.map(expert_block, (row_tok.reshape(n_blocks, MOE_BLOCK), block_e)).reshape(n_rows, d)
    y = jax.ops.segment_sum(y_rows * row_gate[:, None].astype(y_rows.dtype), row_tok, num_segments=t)
    return y.reshape(bsz, s, d)


def setup_inputs(seed: int = 0) -> dict:
    key = jax.random.key(seed)
    ks = jax.random.split(key, 20)
    L, D, E, F = DEPTH, D_MODEL, N_EXPERTS, D_FF

    def nrm(k, shape, scale):
        return jax.random.normal(k, shape, jnp.float32) * scale

    col_scale = np.ones((IN_COLS,), np.float32)
    col_scale[MV_OFF:MV_OFF + M_W] = DN_BETA
    col_scale[FV_OFF:FV_OFF + F_W] = DN_BETA
    b_off = np.zeros((IN_COLS,), np.float32)
    b_off[MF_OFF:MF_OFF + M_HEADS] = np.linspace(3.0, 6.0, M_HEADS)
    b_off[FF_OFF:FF_OFF + F_HEADS] = np.linspace(1.0, 5.0, F_HEADS)

    x = nrm(ks[0], (BATCH, SEQ, D), 1.0)
    w_in = nrm(ks[1], (L, D, IN_COLS), D ** -0.5) * jnp.asarray(col_scale)
    b_in = nrm(ks[2], (L, IN_COLS), 0.02) + jnp.asarray(b_off)
    m_conv_w = nrm(ks[3], (L, CONV_W, MQK_W), CONV_W ** -0.5)
    m_conv_b = nrm(ks[4], (L, MQK_W), 0.02)
    m_norm_g = 1.0 + nrm(ks[5], (L, M_W), 0.02)
    w_bm = nrm(ks[6], (L, M_W, D), DN_BETA * M_W ** -0.5)
    w_bf = nrm(ks[7], (L, F_W, D), DN_BETA * F_W ** -0.5)
    w_o = nrm(ks[8], (L, D, D), DN_BETA * D ** -0.5)
    ln1_g = 1.0 + nrm(ks[9], (L, D), 0.02)
    ln1_b = nrm(ks[10], (L, D), 0.02)
    w_router = nrm(ks[11], (L, D, E), D ** -0.5)
    b_router = nrm(ks[12], (L, E), 0.01)
    w_gu = nrm(ks[13], (L, E, D, 2 * F), DN_BETA * D ** -0.5)
    b_gu = nrm(ks[14], (L, E, 2 * F), 0.02)
    w_dn = nrm(ks[15], (L, E, F, D), DN_BETA * F ** -0.5)
    b_dn = nrm(ks[16], (L, E, D), 0.02)
    ln2_g = 1.0 + nrm(ks[17], (L, D), 0.02)
    ln2_b = nrm(ks[18], (L, D), 0.02)
    return {"x": x, "w_in": w_in, "b_in": b_in, "m_conv_w": m_conv_w, "m_conv_b": m_conv_b,
            "m_norm_g": m_norm_g, "w_bm": w_bm, "w_bf": w_bf, "w_o": w_o,
            "ln1_g": ln1_g, "ln1_b": ln1_b, "w_router": w_router, "b_router": b_router,
            "w_gu": w_gu, "b_gu": b_gu, "w_dn": w_dn, "b_dn": b_dn,
            "ln2_g": ln2_g, "ln2_b": ln2_b}


def reference(x, w_in, b_in, m_conv_w, m_conv_b, m_norm_g, w_bm, w_bf, w_o, ln1_g, ln1_b,
              w_router, b_router, w_gu, b_gu, w_dn, b_dn, ln2_g, ln2_b):
    h = x
    for l in range(DEPTH):
        mix = hybrid_mixer(h, w_in[l], b_in[l], m_conv_w[l], m_conv_b[l], m_norm_g[l],
                           w_bm[l], w_bf[l], w_o[l])
        h = layer_norm(DN_ALPHA * h + mix, ln1_g[l], ln1_b[l])
        ffn = moe_ffn(h, w_router[l], b_router[l], w_gu[l], b_gu[l], w_dn[l], b_dn[l])
        h = layer_norm(DN_ALPHA * h + ffn, ln2_g[l], ln2_b[l])
    return h
```

```python
import functools
import math

import jax
import jax.numpy as jnp
from jax import lax
from jax.experimental import pallas as pl
from jax.experimental.pallas import tpu as pltpu

F32 = jnp.float32
BF16 = jnp.bfloat16
I32 = jnp.int32
U32 = jnp.uint32

M_HEADS = 4
M_DQK = 128
M_DV = 128
CONV_W = 4
F_HEADS = 8
F_DH = 64
N_EXPERTS = 32
TOP_K = 4
SWIGLU_ALPHA = 1.702
SWIGLU_LIMIT = 7.0
LN_EPS = 1e-5

M_W = M_HEADS * M_DV
F_W = F_HEADS * F_DH
MQK_W = 2 * M_HEADS * M_DQK

V7X_LANES = 128
V7X_SUBLANES = 8
V7X_VMEM_BYTES = 64 * 1024 * 1024
VMEM_LIMIT_BYTES = (V7X_VMEM_BYTES * 3) // 4

INPROJ_ROWS = 512
MLSTM_CHUNK = 256
FOX_BLOCK = 256
CUMSUM_BLOCK = 256
MERGE_ROWS = 256
ROUTE_ROWS = 256
MOE_BLOCK = 256
DISPATCH_ROWS = 256
COMBINE_ROWS = 128

GATE_COLS = V7X_LANES
MI_LANE = 0
MF_LANE = M_HEADS
FF_LANE = 2 * M_HEADS


def _params(semantics):
    return pltpu.CompilerParams(dimension_semantics=semantics, vmem_limit_bytes=VMEM_LIMIT_BYTES)


def _log_sigmoid(x):
    return jnp.minimum(x, 0.0) - jnp.log1p(jnp.exp(-jnp.abs(x)))


def _sigmoid(x):
    return 1.0 / (1.0 + jnp.exp(-x))


def _dot(a, b):
    return jnp.dot(a, b, preferred_element_type=F32)


def _dot_nt(a, b):
    return lax.dot_general(a, b, (((1,), (1,)), ((), ())), preferred_element_type=F32)


def _dot_tn(a, b):
    return lax.dot_general(a, b, (((0,), (0,)), ((), ())), preferred_element_type=F32)


def _split3(x):
    hi = x.astype(BF16)
    r1 = x - hi.astype(F32)
    mid = r1.astype(BF16)
    lo = (r1 - mid.astype(F32)).astype(BF16)
    return hi, mid, lo


def _dot_mask_f32(mask_bf16, x):
    hi, mid, lo = _split3(x)
    return (_dot(mask_bf16, lo) + _dot(mask_bf16, mid)) + _dot(mask_bf16, hi)


def _tril_mask(n, strict=False):
    r = lax.broadcasted_iota(I32, (n, n), 0)
    c = lax.broadcasted_iota(I32, (n, n), 1)
    return (r > c) if strict else (r >= c)


_OFF_MQK = 0
_OFF_MV = _OFF_MQK + MQK_W
_OFF_MO = _OFF_MV + M_W
_OFF_FQ = _OFF_MO + M_W
_OFF_FK = _OFF_FQ + F_W
_OFF_FV = _OFF_FK + F_W
_OFF_GATES = _OFF_FV + F_W
_PACKED_COLS = _OFF_GATES + GATE_COLS


def _inproj_kernel(x_ref, w_ref, b_ref, mqk_ref, mv_ref, mo_ref, fq_ref, fk_ref, fv_ref, gates_ref):
    xb = x_ref[...].astype(BF16)

    def seg(lo, width):
        return _dot(xb, w_ref[:, lo:lo + width]) + b_ref[:, lo:lo + width]

    mqk_ref[...] = seg(_OFF_MQK, MQK_W)
    mv_ref[...] = seg(_OFF_MV, M_W).astype(BF16)
    mo_ref[...] = seg(_OFF_MO, M_W)
    fq_ref[...] = (seg(_OFF_FQ, F_W) * (F_DH ** -0.5)).astype(BF16)
    fk_ref[...] = seg(_OFF_FK, F_W).astype(BF16)
    fv_ref[...] = seg(_OFF_FV, F_W).astype(BF16)
    gates_ref[...] = seg(_OFF_GATES, GATE_COLS)


def _inproj(x2, w_packed, b_packed):
    t, d = x2.shape
    tm = INPROJ_ROWS
    row = lambda i: (i, 0)
    const = lambda i: (0, 0)
    out_shapes = (
        jax.ShapeDtypeStruct((t, MQK_W), F32),
        jax.ShapeDtypeStruct((t, M_W), BF16),
        jax.ShapeDtypeStruct((t, M_W), F32),
        jax.ShapeDtypeStruct((t, F_W), BF16),
        jax.ShapeDtypeStruct((t, F_W), BF16),
        jax.ShapeDtypeStruct((t, F_W), BF16),
        jax.ShapeDtypeStruct((t, GATE_COLS), F32),
    )
    return pl.pallas_call(
        _inproj_kernel,
        grid=(t // tm,),
        in_specs=[
            pl.BlockSpec((tm, d), row),
            pl.BlockSpec((d, _PACKED_COLS), const),
            pl.BlockSpec((1, _PACKED_COLS), const),
        ],
        out_specs=tuple(pl.BlockSpec((tm, s.shape[1]), row) for s in out_shapes),
        out_shape=out_shapes,
        compiler_params=_params(("parallel",)),
        name="inproj",
    )(x2, w_packed, b_packed)


def _fox_cumsum_kernel(g_ref, ccol_ref, crow_ref):
    s = g_ref.shape[0]
    cb = CUMSUM_BLOCK
    tri = _tril_mask(cb).astype(BF16)
    carry = jnp.zeros((1, GATE_COLS), F32)
    for j in range(s // cb):
        lsg = _log_sigmoid(g_ref[j * cb:(j + 1) * cb, :])
        cs = _dot_mask_f32(tri, lsg) + carry
        ccol_ref[j * cb:(j + 1) * cb, :] = cs
        carry = cs[cb - 1:cb, :]
    ct = ccol_ref[...].T
    rows = ct[FF_LANE:FF_LANE + F_HEADS, :]
    sub = lax.broadcasted_iota(I32, rows.shape, 0)
    for p in range(F_HEADS // 2):
        shifted = rows if p == 0 else pltpu.roll(rows, shift=F_HEADS - 2 * p, axis=0)
        crow_ref[p] = jnp.where(sub < 2, shifted, 0.0)


def _fox_cumsum(gates, bsz, s):
    t = gates.shape[0]
    return pl.pallas_call(
        _fox_cumsum_kernel,
        grid=(bsz,),
        in_specs=[pl.BlockSpec((s, GATE_COLS), lambda b: (b, 0))],
        out_specs=(
            pl.BlockSpec((s, GATE_COLS), lambda b: (b, 0)),
            pl.BlockSpec((None, F_HEADS // 2, V7X_SUBLANES, s), lambda b: (b, 0, 0, 0)),
        ),
        out_shape=(
            jax.ShapeDtypeStruct((t, GATE_COLS), F32),
            jax.ShapeDtypeStruct((bsz, F_HEADS // 2, V7X_SUBLANES, s), F32),
        ),
        compiler_params=_params(("parallel",)),
        name="fox_cumsum",
    )(gates)


def _mlstm_kernel(mqk_ref, mv_ref, mo_ref, gates_ref, cw_ref, cb_ref, ng_ref, hm_ref,
                  ext_ref, state_ref, m_ref):
    L = MLSTM_CHUNK
    pad = V7X_SUBLANES
    n = pl.program_id(1)

    @pl.when(n == 0)
    def _():
        ext_ref[0:pad, :] = jnp.zeros((pad, MQK_W), F32)
        state_ref[...] = jnp.zeros(state_ref.shape, F32)
        m_ref[...] = jnp.zeros(m_ref.shape, F32)

    ext_ref[pad:pad + L, :] = mqk_ref[...]
    y = cb_ref[...] + cw_ref[CONV_W - 1:CONV_W, :] * ext_ref[pad:pad + L, :]
    for k in range(CONV_W - 1):
        shift = CONV_W - 1 - k
        y = y + cw_ref[k:k + 1, :] * ext_ref[pad - shift:pad - shift + L, :]
    ext_ref[0:pad, :] = ext_ref[L:L + pad, :]
    qk = y * _sigmoid(y)

    gates = gates_ref[...]
    lsg = _log_sigmoid(gates)
    tri = _tril_mask(L)
    bfull = _dot_mask_f32(tri.astype(BF16), lsg)
    gates_t = gates.T
    bfull_t = bfull.T
    lane = lax.broadcasted_iota(I32, (L, M_DV), 1)
    ones_col = (lane == 0).astype(BF16)

    for h in range(M_HEADS):
        i_col = gates[:, MI_LANE + h:MI_LANE + h + 1]
        b_col = bfull[:, MF_LANE + h:MF_LANE + h + 1]
        i_row = gates_t[MI_LANE + h:MI_LANE + h + 1, :]
        b_row = bfull_t[MF_LANE + h:MF_LANE + h + 1, :]
        g_tot = b_col[L - 1:L, :]
        m_prev = m_ref[h][0:1, 0:1]

        q_h = qk[:, h * M_DQK:(h + 1) * M_DQK].astype(BF16)
        k_f = qk[:, MQK_W // 2 + h * M_DQK:MQK_W // 2 + (h + 1) * M_DQK] * (M_DQK ** -0.5)
        k_h = k_f.astype(BF16)
        v_h = mv_ref[:, h * M_DV:(h + 1) * M_DV]
        cn = state_ref[h]

        dlog = jnp.where(tri, (b_col - b_row) + i_row, -jnp.inf)
        inter_log = b_col + m_prev
        m_t = jnp.maximum(inter_log, jnp.max(dlog, axis=-1, keepdims=True))
        w_inter = jnp.exp(inter_log - m_t)
        w_intra = jnp.exp(dlog - m_t)
        qkw = _dot_nt(q_h, k_h) * w_intra
        qc = _dot(q_h, cn.astype(BF16))
        num = w_inter * qc[:, :M_DV] + _dot(qkw.astype(BF16), v_h)
        den = w_inter * qc[:, M_DV:M_DV + 1] + jnp.sum(qkw, axis=-1, keepdims=True)
        hh = num / jnp.maximum(jnp.abs(den), jnp.exp(-m_t))

        mu = jnp.mean(hh, axis=-1, keepdims=True)
        dv = hh - mu
        var = jnp.mean(dv * dv, axis=-1, keepdims=True)
        hn = (dv * lax.rsqrt(var + LN_EPS)) * ng_ref[:, h * M_DV:(h + 1) * M_DV]
        hm_ref[:, h * M_DV:(h + 1) * M_DV] = (
            _sigmoid(mo_ref[:, h * M_DV:(h + 1) * M_DV]) * hn).astype(BF16)

        a_col = (g_tot - b_col) + i_col
        m_new = jnp.maximum(g_tot + m_prev, jnp.max(a_col, axis=0, keepdims=True))
        decay = jnp.exp(g_tot + m_prev - m_new)
        w_col = jnp.exp(a_col - m_new)
        kw = (k_f * w_col).astype(BF16)
        v_aug = jnp.concatenate([v_h, ones_col], axis=1)
        state_ref[h] = decay * cn + _dot_tn(kw, v_aug)
        m_ref[h] = jnp.broadcast_to(m_new, m_ref.shape[1:])


def _mlstm(mqk, mv, mo, gates, conv_w, conv_b, norm_g, bsz, s):
    t = mqk.shape[0]
    L = MLSTM_CHUNK
    nc = s // L
    row = lambda b, n: (b * nc + n, 0)
    const = lambda b, n: (0, 0)
    return pl.pallas_call(
        _mlstm_kernel,
        grid=(bsz, nc),
        in_specs=[
            pl.BlockSpec((L, MQK_W), row),
            pl.BlockSpec((L, M_W), row),
            pl.BlockSpec((L, M_W), row),
            pl.BlockSpec((L, GATE_COLS), row),
            pl.BlockSpec((CONV_W, MQK_W), const),
            pl.BlockSpec((1, MQK_W), const),
            pl.BlockSpec((1, M_W), const),
        ],
        out_specs=pl.BlockSpec((L, M_W), row),
        out_shape=jax.ShapeDtypeStruct((t, M_W), BF16),
        scratch_shapes=[
            pltpu.VMEM((L + V7X_SUBLANES, MQK_W), F32),
            pltpu.VMEM((M_HEADS, M_DQK, 2 * M_DV), F32),
            pltpu.VMEM((M_HEADS, V7X_SUBLANES, V7X_LANES), F32),
        ],
        compiler_params=_params(("parallel", "arbitrary")),
        name="mlstm",
    )(mqk, mv, mo, gates, conv_w, conv_b, norm_g)


def _fox_kernel(q_ref, k_ref, v_ref, ccol_ref, crow_ref, o_ref):
    tq = FOX_BLOCK
    tk = FOX_BLOCK
    p = pl.program_id(1)
    i = pl.program_id(2)
    q = q_ref[...]
    lane = lax.broadcasted_iota(I32, (tq, V7X_LANES), 1)
    first = lane < F_DH
    q_heads = (jnp.where(first, q, jnp.zeros_like(q)), jnp.where(first, jnp.zeros_like(q), q))
    ccol = ccol_ref[...]
    c_q = tuple(
        jnp.sum(jnp.where(lane == FF_LANE + 2 * p + hh, ccol, 0.0), axis=-1, keepdims=True)
        for hh in range(2))
    causal = _tril_mask(tq)

    def chunk(j, carry, masked):
        start = pl.multiple_of(j * tk, tk)
        k_c = k_ref[pl.ds(start, tk), :]
        v_c = v_ref[pl.ds(start, tk), :]
        c_k = crow_ref[:, pl.ds(start, tk)]
        new = []
        for hh in range(2):
            m_old, l_old, acc = carry[hh]
            sc = _dot_nt(q_heads[hh], k_c) + (c_q[hh] - c_k[hh:hh + 1, :])
            if masked:
                sc = jnp.where(causal, sc, -jnp.inf)
            m_new = jnp.maximum(m_old, jnp.max(sc, axis=-1, keepdims=True))
            alpha = jnp.exp(m_old - m_new)
            pe = jnp.exp(sc - m_new)
            l_new = alpha * l_old + jnp.sum(pe, axis=-1, keepdims=True)
            acc_new = alpha * acc + _dot(pe.astype(BF16), v_c)
            new.append((m_new, l_new, acc_new))
        return tuple(new)

    init = tuple((jnp.full((tq, 1), -jnp.inf, F32), jnp.zeros((tq, 1), F32),
                  jnp.zeros((tq, V7X_LANES), F32)) for _ in range(2))
    carry = lax.fori_loop(0, i, lambda j, c: chunk(j, c, False), init)
    (_, l0, a0), (_, l1, a1) = chunk(i, carry, True)
    o_ref[...] = jnp.where(first, a0 / l0, a1 / l1).astype(BF16)


def _fox_attention(fq, fk, fv, ccol, crow, bsz, s):
    t = fq.shape[0]
    tq = FOX_BLOCK
    nq = s // tq
    pairs = F_HEADS // 2
    qmap = lambda b, p, i: (b * nq + i, p)
    kvmap = lambda b, p, i: (b, p)
    return pl.pallas_call(
        _fox_kernel,
        grid=(bsz, pairs, nq),
        in_specs=[
            pl.BlockSpec((tq, V7X_LANES), qmap),
            pl.BlockSpec((s, V7X_LANES), kvmap),
            pl.BlockSpec((s, V7X_LANES), kvmap),
            pl.BlockSpec((tq, GATE_COLS), lambda b, p, i: (b * nq + i, 0)),
            pl.BlockSpec((None, None, V7X_SUBLANES, s), lambda b, p, i: (b, p, 0, 0)),
        ],
        out_specs=pl.BlockSpec((tq, V7X_LANES), qmap),
        out_shape=jax.ShapeDtypeStruct((t, F_W), BF16),
        compiler_params=_params(("parallel", "parallel", "arbitrary")),
        name="fox_attention",
    )(fq, fk, fv, ccol, crow)


def _layer_norm_rows(r, g, b):
    mu = jnp.mean(r, axis=-1, keepdims=True)
    d = r - mu
    var = jnp.mean(d * d, axis=-1, keepdims=True)
    return (d * lax.rsqrt(var + LN_EPS)) * g + b


def _merge_kernel(dn_alpha, hm_ref, hf_ref, x_ref, wg_ref, bg_ref, wbm_ref, wbf_ref, wo_ref,
                  g_ref, b_ref, wrh_ref, wrl_ref, br_ref, h1_ref, h1p_ref, gate_ref, tope_ref):
    x = x_ref[...]
    d = x.shape[1]
    gmf = _dot(x.astype(BF16), wg_ref[...]) + bg_ref[...]
    ym = _dot(hm_ref[...], wbm_ref[...])
    yf = _dot(hf_ref[...], wbf_ref[...])
    y = _sigmoid(gmf[:, :d]) * ym + _sigmoid(gmf[:, d:]) * yf
    mix = _dot(y.astype(BF16), wo_ref[...])
    h1 = _layer_norm_rows(dn_alpha * x + mix, g_ref[...], b_ref[...])
    h1_ref[...] = h1

    hb = h1.astype(BF16)
    bits = lax.bitcast_convert_type(hb.astype(F32), U32)
    h1p_ref[...] = (bits[:, :d // 2] >> 16) | bits[:, d // 2:]

    lo = (h1 - hb.astype(F32)).astype(BF16)
    logits = (_dot(lo, wrh_ref[...]) + _dot(hb, wrl_ref[...])) + _dot(hb, wrh_ref[...]) + br_ref[...]
    tm = logits.shape[0]
    lane = lax.broadcasted_iota(I32, (tm, V7X_LANES), 1)
    vals = jnp.where(lane < N_EXPERTS, logits, -jnp.inf)
    top_v, top_i = [], []
    for _ in range(TOP_K):
        mx = jnp.max(vals, axis=-1, keepdims=True)
        idx = jnp.min(jnp.where(vals == mx, lane, V7X_LANES), axis=-1, keepdims=True)
        top_v.append(mx)
        top_i.append(idx)
        vals = jnp.where(lane == idx, -jnp.inf, vals)
    ex = [jnp.exp(v - top_v[0]) for v in top_v]
    tot = ex[0]
    for e in ex[1:]:
        tot = tot + e
    gate = jnp.zeros((tm, V7X_LANES), F32)
    tope = jnp.zeros((tm, V7X_LANES), I32)
    for k in range(TOP_K):
        gate = jnp.where(lane == k, ex[k] / tot, gate)
        tope = jnp.where(lane == k, top_i[k], tope)
    gate_ref[...] = gate
    tope_ref[...] = tope


def _merge(dn_alpha, hm, hf, x2, wg, bg, wbm, wbf, wo, ln_g, ln_b, wr_hi, wr_lo, br):
    t, d = x2.shape
    tm = MERGE_ROWS
    row = lambda i: (i, 0)
    const = lambda i: (0, 0)
    full = lambda a: pl.BlockSpec(a.shape, const)
    return pl.pallas_call(
        functools.partial(_merge_kernel, dn_alpha),
        grid=(t // tm,),
        in_specs=[
            pl.BlockSpec((tm, M_W), row),
            pl.BlockSpec((tm, F_W), row),
            pl.BlockSpec((tm, d), row),
            full(wg), full(bg), full(wbm), full(wbf), full(wo), full(ln_g), full(ln_b),
            full(wr_hi), full(wr_lo), full(br),
        ],
        out_specs=(
            pl.BlockSpec((tm, d), row),
            pl.BlockSpec((tm, d // 2), row),
            pl.BlockSpec((tm, V7X_LANES), row),
            pl.BlockSpec((tm, V7X_LANES), row),
        ),
        out_shape=(
            jax.ShapeDtypeStruct((t, d), F32),
            jax.ShapeDtypeStruct((t, d // 2), U32),
            jax.ShapeDtypeStruct((t, V7X_LANES), F32),
            jax.ShapeDtypeStruct((t, V7X_LANES), I32),
        ),
        compiler_params=_params(("parallel",)),
        name="merge_ln1_router",
    )(hm, hf, x2, wg, bg, wbm, wbf, wo, ln_g, ln_b, wr_hi, wr_lo, br)


def _lane_cumsum(x):
    lane = lax.broadcasted_iota(I32, x.shape, 1)
    d = 1
    while d < V7X_LANES:
        x = x + jnp.where(lane >= d, pltpu.roll(x, shift=d, axis=1), 0.0)
        d *= 2
    return x


def _routing_kernel(tope_ref, dest_ref, table_ref, cnt_ref, start_ref):
    ph = pl.program_id(0)
    i = pl.program_id(1)
    tr = tope_ref.shape[0]
    tope = tope_ref[...]
    lane = lax.broadcasted_iota(I32, (tr, V7X_LANES), 1)
    hit = [lane == tope[:, k:k + 1] for k in range(TOP_K)]
    member = jnp.zeros((tr, V7X_LANES), F32)
    for k in range(TOP_K):
        member = member + hit[k].astype(F32)
    tile_count = jnp.sum(member, axis=0, keepdims=True)

    @pl.when(jnp.logical_and(ph == 0, i == 0))
    def _():
        cnt_ref[...] = jnp.zeros(cnt_ref.shape, F32)
        table_ref[...] = jnp.zeros(table_ref.shape, I32)

    @pl.when(ph == 0)
    def _():
        cnt_ref[...] = cnt_ref[...] + tile_count
        dest_ref[...] = jnp.zeros(dest_ref.shape, I32)

    @pl.when(jnp.logical_and(ph == 1, i == 0))
    def _():
        counts = cnt_ref[...]
        padded = jnp.ceil(counts * (1.0 / MOE_BLOCK)) * MOE_BLOCK
        pad_end = _lane_cumsum(padded)
        pad_start = pad_end - padded
        start_ref[...] = pad_start
        cnt_ref[...] = jnp.zeros(cnt_ref.shape, F32)
        nb = table_ref.shape[0]
        blk = lax.broadcasted_iota(I32, (nb, V7X_LANES), 0).astype(F32) * MOE_BLOCK
        ln = lax.broadcasted_iota(I32, (nb, V7X_LANES), 1)
        done = jnp.logical_and(pad_end[0:1, :] <= blk, ln < N_EXPERTS)
        be = jnp.minimum(jnp.sum(done.astype(F32), axis=-1, keepdims=True), N_EXPERTS - 1.0)
        onehot = ln == be.astype(I32)
        cnt_e = jnp.sum(jnp.where(onehot, counts[0:1, :], 0.0), axis=-1, keepdims=True)
        start_e = jnp.sum(jnp.where(onehot, pad_start[0:1, :], 0.0), axis=-1, keepdims=True)
        valid = jnp.clip(cnt_e - (blk[:, 0:1] - start_e), 0.0, float(MOE_BLOCK))
        table_ref[...] = jnp.where(ln == 0, be.astype(I32),
                                   jnp.where(ln == 1, valid.astype(I32), 0))

    @pl.when(ph == 1)
    def _():
        before = _dot(_tril_mask(tr, strict=True).astype(BF16), member.astype(BF16))
        base = before + cnt_ref[0:1, :] + start_ref[0:1, :]
        dest = jnp.zeros((tr, V7X_LANES), I32)
        for k in range(TOP_K):
            dk = jnp.sum(jnp.where(hit[k], base, 0.0), axis=-1, keepdims=True)
            dest = jnp.where(lane == k, dk.astype(I32), dest)
        dest_ref[...] = dest
        cnt_ref[...] = cnt_ref[...] + tile_count


def _routing(tope, n_blocks):
    t = tope.shape[0]
    tr = ROUTE_ROWS
    return pl.pallas_call(
        _routing_kernel,
        grid=(2, t // tr),
        in_specs=[pl.BlockSpec((tr, V7X_LANES), lambda ph, i: (i, 0))],
        out_specs=(
            pl.BlockSpec((tr, V7X_LANES), lambda ph, i: (i * ph, 0)),
            pl.BlockSpec((n_blocks, V7X_LANES), lambda ph, i: (0, 0)),
        ),
        out_shape=(
            jax.ShapeDtypeStruct((t, V7X_LANES), I32),
            jax.ShapeDtypeStruct((n_blocks, V7X_LANES), I32),
        ),
        scratch_shapes=[
            pltpu.VMEM((V7X_SUBLANES, V7X_LANES), F32),
            pltpu.VMEM((V7X_SUBLANES, V7X_LANES), F32),
        ],
        compiler_params=_params(("arbitrary", "arbitrary")),
        name="routing",
    )(tope)


def _row_copy(src_ref, src_row, dst_ref, dst_row, sem):
    return pltpu.make_async_copy(src_ref.at[pl.ds(src_row, 1), :], dst_ref.at[pl.ds(dst_row, 1), :], sem)


def _dispatch_kernel(nv_ref, dest_ref, h1p_ref, xs_ref, zero_ref, sem, zsem):
    rows = h1p_ref.shape[0]

    @pl.when(pl.program_id(0) == 0)
    def _():
        zero_ref[...] = jnp.zeros(zero_ref.shape, U32)

        def zero_copy(b):
            start = pl.multiple_of(b * MOE_BLOCK, MOE_BLOCK)
            return pltpu.make_async_copy(zero_ref, xs_ref.at[pl.ds(start, MOE_BLOCK), :], zsem)

        def zero_start(b, c):
            @pl.when(nv_ref[b] < MOE_BLOCK)
            def _():
                zero_copy(b).start()
            return c

        def zero_wait(b, c):
            @pl.when(nv_ref[b] < MOE_BLOCK)
            def _():
                zero_copy(b).wait()
            return c

        lax.fori_loop(0, nv_ref.shape[0], zero_start, 0)
        lax.fori_loop(0, nv_ref.shape[0], zero_wait, 0)

    def issue(r, c):
        for k in range(TOP_K):
            _row_copy(h1p_ref, r, xs_ref, dest_ref[r * TOP_K + k], sem).start()
        return c

    lax.fori_loop(0, rows, issue, 0)

    def drain(n, c):
        _row_copy(h1p_ref, 0, xs_ref, 0, sem).wait()
        return c

    lax.fori_loop(0, rows * TOP_K, drain, 0)


def _dispatch(block_valid, dest_flat, h1p, n_rows):
    t, w = h1p.shape
    td = DISPATCH_ROWS
    grid_spec = pltpu.PrefetchScalarGridSpec(
        num_scalar_prefetch=1,
        grid=(t // td,),
        in_specs=[
            pl.BlockSpec((td * TOP_K,), lambda i, nv: (i,), memory_space=pltpu.SMEM),
            pl.BlockSpec((td, w), lambda i, nv: (i, 0)),
        ],
        out_specs=pl.BlockSpec(memory_space=pl.ANY),
        scratch_shapes=[pltpu.VMEM((MOE_BLOCK, w), U32), pltpu.SemaphoreType.DMA(()),
                        pltpu.SemaphoreType.DMA(())],
    )
    return pl.pallas_call(
        _dispatch_kernel,
        grid_spec=grid_spec,
        out_shape=jax.ShapeDtypeStruct((n_rows, w), U32),
        compiler_params=_params(("arbitrary",)),
        name="dispatch",
    )(block_valid, dest_flat, h1p)


def _expert_kernel(be_ref, nv_ref, xs_ref, wgu_ref, bgu_ref, wdn_ref, bdn_ref, y_ref):
    i = pl.program_id(0)
    nv = nv_ref[i]

    @pl.when(nv == 0)
    def _():
        y_ref[...] = jnp.zeros(y_ref.shape, F32)

    @pl.when(nv > 0)
    def _():
        words = xs_ref[...]
        lo = lax.bitcast_convert_type(words << 16, F32)
        hi = lax.bitcast_convert_type(words & jnp.uint32(0xFFFF0000), F32)
        x = jnp.concatenate([lo, hi], axis=1).astype(BF16)
        gu =_dot(x, wgu_ref[...]) + bgu_ref[...]
        f = gu.shape[1] // 2
        glu = jnp.minimum(gu[:, :f], SWIGLU_LIMIT)
        lin = jnp.clip(gu[:, f:], -SWIGLU_LIMIT, SWIGLU_LIMIT)
        act = glu * _sigmoid(SWIGLU_ALPHA * glu) * (lin + 1.0)
        y_ref[...] = _dot(act.astype(BF16), wdn_ref[...]) + bdn_ref[...]


def _experts(block_e, block_valid, xs, wgu, bgu, wdn, bdn):
    n_rows, w = xs.shape
    e, d, f2 = wgu.shape
    n_blocks = n_rows // MOE_BLOCK
    grid_spec = pltpu.PrefetchScalarGridSpec(
        num_scalar_prefetch=2,
        grid=(n_blocks,),
        in_specs=[
            pl.BlockSpec((MOE_BLOCK, w), lambda i, be, nv: (i, 0)),
            pl.BlockSpec((None, d, f2), lambda i, be, nv: (be[i], 0, 0)),
            pl.BlockSpec((None, 1, f2), lambda i, be, nv: (be[i], 0, 0)),
            pl.BlockSpec((None, f2 // 2, d), lambda i, be, nv: (be[i], 0, 0)),
            pl.BlockSpec((None, 1, d), lambda i, be, nv: (be[i], 0, 0)),
        ],
        out_specs=pl.BlockSpec((MOE_BLOCK, d), lambda i, be, nv: (i, 0)),
    )
    return pl.pallas_call(
        _expert_kernel,
        grid_spec=grid_spec,
        out_shape=jax.ShapeDtypeStruct((n_rows, d), F32),
        compiler_params=_params(("arbitrary",)),
        name="experts",
    )(block_e, block_valid, xs, wgu, bgu, wdn, bdn)


def _combine_kernel(dn_alpha, dest_ref, h1_ref, gate_ref, g_ref, b_ref, y_ref, o_ref, buf_ref, sem):
    rows = h1_ref.shape[0]

    def issue(r, c):
        for k in range(TOP_K):
            _row_copy(y_ref, dest_ref[r * TOP_K + k], buf_ref.at[k], r, sem).start()
        return c

    lax.fori_loop(0, rows, issue, 0)

    def drain(n, c):
        _row_copy(y_ref, 0, buf_ref.at[0], 0, sem).wait()
        return c

    lax.fori_loop(0, rows * TOP_K, drain, 0)

    gate = gate_ref[...]
    ffn = gate[:, 0:1] * buf_ref[0]
    for k in range(1, TOP_K):
        ffn = ffn + gate[:, k:k + 1] * buf_ref[k]
    o_ref[...] = _layer_norm_rows(dn_alpha * h1_ref[...] + ffn, g_ref[...], b_ref[...])


def _combine(dn_alpha, dest_flat, h1, gate, ln_g, ln_b, y_rows):
    t, d = h1.shape
    tc = COMBINE_ROWS
    row = lambda i: (i, 0)
    const = lambda i: (0, 0)
    return pl.pallas_call(
        functools.partial(_combine_kernel, dn_alpha),
        grid=(t // tc,),
        in_specs=[
            pl.BlockSpec((tc * TOP_K,), lambda i: (i,), memory_space=pltpu.SMEM),
            pl.BlockSpec((tc, d), row),
            pl.BlockSpec((tc, V7X_LANES), row),
            pl.BlockSpec((1, d), const),
            pl.BlockSpec((1, d), const),
            pl.BlockSpec(memory_space=pl.ANY),
        ],
        out_specs=pl.BlockSpec((tc, d), row),
        out_shape=jax.ShapeDtypeStruct((t, d), F32),
        scratch_shapes=[pltpu.VMEM((TOP_K, tc, d), F32), pltpu.SemaphoreType.DMA(())],
        compiler_params=_params(("arbitrary",)),
        name="combine_ln2",
    )(dest_flat, h1, gate, ln_g, ln_b, y_rows)


def _pack_in_proj(w_in, b_in):
    d = w_in.shape[0]
    o = 0
    cols = {}
    for name, width in (("mqk", MQK_W), ("mv", M_W), ("mo", M_W), ("mi", M_HEADS), ("mf", M_HEADS),
                        ("fq", F_W), ("fk", F_W), ("fv", F_W), ("ff", F_HEADS), ("gm", d), ("gf", d)):
        cols[name] = (o, o + width)
        o += width

    def take(a, names):
        return [a[..., cols[n][0]:cols[n][1]] for n in names]

    n_gate = 2 * M_HEADS + F_HEADS
    main = ("mqk", "mv", "mo", "fq", "fk", "fv", "mi", "mf", "ff")
    w_main = jnp.concatenate(take(w_in, main) + [jnp.zeros((d, GATE_COLS - n_gate), w_in.dtype)], axis=1)
    b_main = jnp.concatenate(take(b_in, main) + [jnp.zeros((GATE_COLS - n_gate,), b_in.dtype)])
    w_gate = jnp.concatenate(take(w_in, ("gm", "gf")), axis=1)
    b_gate = jnp.concatenate(take(b_in, ("gm", "gf")))
    return w_main.astype(BF16), b_main[None, :], w_gate.astype(BF16), b_gate[None, :]


def _layer(h, depth, w_in, b_in, m_conv_w, m_conv_b, m_norm_g, w_bm, w_bf, w_o, ln1_g, ln1_b,
           w_router, b_router, w_gu, b_gu, w_dn, b_dn, ln2_g, ln2_b):
    bsz, s, d = h.shape
    t = bsz * s
    dn_alpha = (2.0 * depth) ** 0.25
    x2 = h.reshape(t, d)

    w_main, b_main, w_gate, b_gate = _pack_in_proj(w_in, b_in)
    mqk, mv, mo, fq, fk, fv, gates = _inproj(x2, w_main, b_main)
    ccol, crow = _fox_cumsum(gates, bsz, s)
    hm = _mlstm(mqk, mv, mo, gates, m_conv_w, m_conv_b[None, :], m_norm_g[None, :], bsz, s)
    hf = _fox_attention(fq, fk, fv, ccol, crow, bsz, s)

    n_exp = w_router.shape[1]
    wr = jnp.zeros((d, V7X_LANES), F32).at[:, :n_exp].set(w_router)
    wr_hi = wr.astype(BF16)
    wr_lo = (wr - wr_hi.astype(F32)).astype(BF16)
    br = jnp.zeros((1, V7X_LANES), F32).at[0, :n_exp].set(b_router)
    h1, h1p, gate, tope = _merge(
        dn_alpha, hm, hf, x2, w_gate, b_gate, w_bm.astype(BF16), w_bf.astype(BF16), w_o.astype(BF16),
        ln1_g[None, :], ln1_b[None, :], wr_hi, wr_lo, br)

    n_blocks = -(-(t * TOP_K) // MOE_BLOCK) + N_EXPERTS
    dest, table = _routing(tope, n_blocks)
    dest_flat = dest[:, :TOP_K].reshape(t * TOP_K)
    block_e, block_valid = table[:, 0], table[:, 1]
    xs = _dispatch(block_valid, dest_flat, h1p, n_blocks * MOE_BLOCK)
    y_rows = _experts(block_e, block_valid, xs, w_gu.astype(BF16), b_gu[:, None, :],
                      w_dn.astype(BF16), b_dn[:, None, :])
    out = _combine(dn_alpha, dest_flat, h1, gate, ln2_g[None, :], ln2_b[None, :], y_rows)
    return out.reshape(bsz, s, d)


def kernel(x, w_in, b_in, m_conv_w, m_conv_b, m_norm_g, w_bm, w_bf, w_o, ln1_g, ln1_b,
           w_router, b_router, w_gu, b_gu, w_dn, b_dn, ln2_g, ln2_b):
    depth = w_in.shape[0]
    h = x
    for l in range(depth):
        h = _layer(h, depth, w_in[l], b_in[l], m_conv_w[l], m_conv_b[l], m_norm_g[l], w_bm[l], w_bf[l],
                   w_o[l], ln1_g[l], ln1_b[l], w_router[l], b_router[l], w_gu[l], b_gu[l], w_dn[l],
                   b_dn[l], ln2_g[l], ln2_b[l])
    return h
```

```python
import functools
import math

import jax
import jax.numpy as jnp
from jax import lax
from jax.experimental import pallas as pl
from jax.experimental.pallas import tpu as pltpu
from jax.experimental.pallas import tpu_sc as plsc

F32 = jnp.float32
BF16 = jnp.bfloat16
I32 = jnp.int32
U32 = jnp.uint32

M_HEADS = 4
M_DQK = 128
M_DV = 128
CONV_W = 4
F_HEADS = 8
F_DH = 64
N_EXPERTS = 32
TOP_K = 4
SWIGLU_ALPHA = 1.702
SWIGLU_LIMIT = 7.0
LN_EPS = 1e-5

M_W = M_HEADS * M_DV
F_W = F_HEADS * F_DH
MQK_W = 2 * M_HEADS * M_DQK

V7X_LANES = 128
V7X_SUBLANES = 8
V7X_VMEM_BYTES = 64 * 1024 * 1024
VMEM_LIMIT_BYTES = (V7X_VMEM_BYTES * 3) // 4
V7X_SC_CORES = 2
V7X_SC_SUBCORES = 16
V7X_SC_WORKERS = V7X_SC_CORES * V7X_SC_SUBCORES

INPROJ_ROWS = 512
MLSTM_CHUNK = 256
FOX_Q_BLOCK = 256
FOX_K_BLOCK = 512
CUMSUM_BLOCK = 256
MERGE_ROWS = 256
ROUTE_ROWS = 256
MOE_BLOCK = 256
SC_ROWS_PER_DMA = 64
COMBINE_ROWS = 128

GATE_COLS = V7X_LANES
MI_LANE = 0
MF_LANE = M_HEADS
FF_LANE = 2 * M_HEADS


def _params(semantics):
    return pltpu.CompilerParams(dimension_semantics=semantics, vmem_limit_bytes=VMEM_LIMIT_BYTES)


def _log_sigmoid(x):
    return jnp.minimum(x, 0.0) - jnp.log1p(jnp.exp(-jnp.abs(x)))


def _sigmoid(x):
    return 1.0 / (1.0 + jnp.exp(-x))


def _dot(a, b):
    return jnp.dot(a, b, preferred_element_type=F32)


def _dot_nt(a, b):
    return lax.dot_general(a, b, (((1,), (1,)), ((), ())), preferred_element_type=F32)


def _dot_tn(a, b):
    return lax.dot_general(a, b, (((0,), (0,)), ((), ())), preferred_element_type=F32)


def _split3(x):
    hi = x.astype(BF16)
    r1 = x - hi.astype(F32)
    mid = r1.astype(BF16)
    lo = (r1 - mid.astype(F32)).astype(BF16)
    return hi, mid, lo


def _dot_mask_f32(mask_bf16, x):
    hi, mid, lo = _split3(x)
    return (_dot(mask_bf16, lo) + _dot(mask_bf16, mid)) + _dot(mask_bf16, hi)


def _tril_mask(n, strict=False):
    r = lax.broadcasted_iota(I32, (n, n), 0)
    c = lax.broadcasted_iota(I32, (n, n), 1)
    return (r > c) if strict else (r >= c)


_OFF_MQK = 0
_OFF_MV = _OFF_MQK + MQK_W
_OFF_MO = _OFF_MV + M_W
_OFF_FQ = _OFF_MO + M_W
_OFF_FK = _OFF_FQ + F_W
_OFF_FV = _OFF_FK + F_W
_OFF_GATES = _OFF_FV + F_W
_PACKED_COLS = _OFF_GATES + GATE_COLS


def _inproj_kernel(x_ref, w_ref, b_ref, mqk_ref, mv_ref, mo_ref, fq_ref, fk_ref, fv_ref, gates_ref):
    xb = x_ref[...].astype(BF16)

    def seg(lo, width):
        return _dot(xb, w_ref[:, lo:lo + width]) + b_ref[:, lo:lo + width]

    mqk_ref[...] = seg(_OFF_MQK, MQK_W)
    mv_ref[...] = seg(_OFF_MV, M_W).astype(BF16)
    mo_ref[...] = seg(_OFF_MO, M_W)
    fq_ref[...] = (seg(_OFF_FQ, F_W) * (F_DH ** -0.5)).astype(BF16)
    fk_ref[...] = seg(_OFF_FK, F_W).astype(BF16)
    fv_ref[...] = seg(_OFF_FV, F_W).astype(BF16)
    gates_ref[...] = seg(_OFF_GATES, GATE_COLS)


def _inproj(x2, w_packed, b_packed):
    t, d = x2.shape
    tm = INPROJ_ROWS
    row = lambda i: (i, 0)
    const = lambda i: (0, 0)
    out_shapes = (
        jax.ShapeDtypeStruct((t, MQK_W), F32),
        jax.ShapeDtypeStruct((t, M_W), BF16),
        jax.ShapeDtypeStruct((t, M_W), F32),
        jax.ShapeDtypeStruct((t, F_W), BF16),
        jax.ShapeDtypeStruct((t, F_W), BF16),
        jax.ShapeDtypeStruct((t, F_W), BF16),
        jax.ShapeDtypeStruct((t, GATE_COLS), F32),
    )
    return pl.pallas_call(
        _inproj_kernel,
        grid=(t // tm,),
        in_specs=[
            pl.BlockSpec((tm, d), row),
            pl.BlockSpec((d, _PACKED_COLS), const),
            pl.BlockSpec((1, _PACKED_COLS), const),
        ],
        out_specs=tuple(pl.BlockSpec((tm, s.shape[1]), row) for s in out_shapes),
        out_shape=out_shapes,
        compiler_params=_params(("parallel",)),
        name="inproj",
    )(x2, w_packed, b_packed)


def _fox_cumsum_kernel(g_ref, ccol_ref, crow_ref):
    s = g_ref.shape[0]
    cb = CUMSUM_BLOCK
    tri = _tril_mask(cb).astype(BF16)
    carry = jnp.zeros((1, GATE_COLS), F32)
    for j in range(s // cb):
        lsg = _log_sigmoid(g_ref[j * cb:(j + 1) * cb, :])
        cs = _dot_mask_f32(tri, lsg) + carry
        ccol_ref[j * cb:(j + 1) * cb, :] = cs
        carry = cs[cb - 1:cb, :]
    ct = ccol_ref[...].T
    rows = ct[FF_LANE:FF_LANE + F_HEADS, :]
    sub = lax.broadcasted_iota(I32, rows.shape, 0)
    for p in range(F_HEADS // 2):
        shifted = rows if p == 0 else pltpu.roll(rows, shift=F_HEADS - 2 * p, axis=0)
        crow_ref[p] = jnp.where(sub < 2, shifted, 0.0)


def _fox_cumsum(gates, bsz, s):
    t = gates.shape[0]
    return pl.pallas_call(
        _fox_cumsum_kernel,
        grid=(bsz,),
        in_specs=[pl.BlockSpec((s, GATE_COLS), lambda b: (b, 0))],
        out_specs=(
            pl.BlockSpec((s, GATE_COLS), lambda b: (b, 0)),
            pl.BlockSpec((None, F_HEADS // 2, V7X_SUBLANES, s), lambda b: (b, 0, 0, 0)),
        ),
        out_shape=(
            jax.ShapeDtypeStruct((t, GATE_COLS), F32),
            jax.ShapeDtypeStruct((bsz, F_HEADS // 2, V7X_SUBLANES, s), F32),
        ),
        compiler_params=_params(("parallel",)),
        name="fox_cumsum",
    )(gates)


def _mlstm_kernel(mqk_ref, mv_ref, mo_ref, gates_ref, cw_ref, cb_ref, ng_ref, hm_ref,
                  ext_ref, state_ref, m_ref):
    L = MLSTM_CHUNK
    pad = V7X_SUBLANES
    n = pl.program_id(1)

    @pl.when(n == 0)
    def _():
        ext_ref[0:pad, :] = jnp.zeros((pad, MQK_W), F32)
        state_ref[...] = jnp.zeros(state_ref.shape, F32)
        m_ref[...] = jnp.zeros(m_ref.shape, F32)

    ext_ref[pad:pad + L, :] = mqk_ref[...]
    y = cb_ref[...] + cw_ref[CONV_W - 1:CONV_W, :] * ext_ref[pad:pad + L, :]
    for k in range(CONV_W - 1):
        shift = CONV_W - 1 - k
        y = y + cw_ref[k:k + 1, :] * ext_ref[pad - shift:pad - shift + L, :]
    ext_ref[0:pad, :] = ext_ref[L:L + pad, :]
    qk = y * _sigmoid(y)

    gates = gates_ref[...]
    lsg = _log_sigmoid(gates)
    tri = _tril_mask(L)
    bfull = _dot_mask_f32(tri.astype(BF16), lsg)
    gates_t = gates.T
    bfull_t = bfull.T
    lane = lax.broadcasted_iota(I32, (L, M_DV), 1)
    ones_col = (lane == 0).astype(BF16)

    for h in range(M_HEADS):
        i_col = gates[:, MI_LANE + h:MI_LANE + h + 1]
        b_col = bfull[:, MF_LANE + h:MF_LANE + h + 1]
        i_row = gates_t[MI_LANE + h:MI_LANE + h + 1, :]
        b_row = bfull_t[MF_LANE + h:MF_LANE + h + 1, :]
        g_tot = b_col[L - 1:L, :]
        m_prev = m_ref[h][0:1, 0:1]

        q_h = qk[:, h * M_DQK:(h + 1) * M_DQK].astype(BF16)
        k_f = qk[:, MQK_W // 2 + h * M_DQK:MQK_W // 2 + (h + 1) * M_DQK] * (M_DQK ** -0.5)
        k_h = k_f.astype(BF16)
        v_h = mv_ref[:, h * M_DV:(h + 1) * M_DV]
        cn = state_ref[h]

        dlog = jnp.where(tri, (b_col - b_row) + i_row, -jnp.inf)
        inter_log = b_col + m_prev
        m_t = jnp.maximum(inter_log, jnp.max(dlog, axis=-1, keepdims=True))
        w_inter = jnp.exp(inter_log - m_t)
        w_intra = jnp.exp(dlog - m_t)
        qkw = _dot_nt(q_h, k_h) * w_intra
        qc = _dot(q_h, cn.astype(BF16))
        num = w_inter * qc[:, :M_DV] + _dot(qkw.astype(BF16), v_h)
        den = w_inter * qc[:, M_DV:M_DV + 1] + jnp.sum(qkw, axis=-1, keepdims=True)
        hh = num / jnp.maximum(jnp.abs(den), jnp.exp(-m_t))

        mu = jnp.mean(hh, axis=-1, keepdims=True)
        dv = hh - mu
        var = jnp.mean(dv * dv, axis=-1, keepdims=True)
        hn = (dv * lax.rsqrt(var + LN_EPS)) * ng_ref[:, h * M_DV:(h + 1) * M_DV]
        hm_ref[:, h * M_DV:(h + 1) * M_DV] = (
            _sigmoid(mo_ref[:, h * M_DV:(h + 1) * M_DV]) * hn).astype(BF16)

        a_col = (g_tot - b_col) + i_col
        m_new = jnp.maximum(g_tot + m_prev, jnp.max(a_col, axis=0, keepdims=True))
        decay = jnp.exp(g_tot + m_prev - m_new)
        w_col = jnp.exp(a_col - m_new)
        kw = (k_f * w_col).astype(BF16)
        v_aug = jnp.concatenate([v_h, ones_col], axis=1)
        state_ref[h] = decay * cn + _dot_tn(kw, v_aug)
        m_ref[h] = jnp.broadcast_to(m_new, m_ref.shape[1:])


def _mlstm(mqk, mv, mo, gates, conv_w, conv_b, norm_g, bsz, s):
    t = mqk.shape[0]
    L = MLSTM_CHUNK
    nc = s // L
    row = lambda b, n: (b * nc + n, 0)
    const = lambda b, n: (0, 0)
    return pl.pallas_call(
        _mlstm_kernel,
        grid=(bsz, nc),
        in_specs=[
            pl.BlockSpec((L, MQK_W), row),
            pl.BlockSpec((L, M_W), row),
            pl.BlockSpec((L, M_W), row),
            pl.BlockSpec((L, GATE_COLS), row),
            pl.BlockSpec((CONV_W, MQK_W), const),
            pl.BlockSpec((1, MQK_W), const),
            pl.BlockSpec((1, M_W), const),
        ],
        out_specs=pl.BlockSpec((L, M_W), row),
        out_shape=jax.ShapeDtypeStruct((t, M_W), BF16),
        scratch_shapes=[
            pltpu.VMEM((L + V7X_SUBLANES, MQK_W), F32),
            pltpu.VMEM((M_HEADS, M_DQK, 2 * M_DV), F32),
            pltpu.VMEM((M_HEADS, V7X_SUBLANES, V7X_LANES), F32),
        ],
        compiler_params=_params(("parallel", "arbitrary")),
        name="mlstm",
    )(mqk, mv, mo, gates, conv_w, conv_b, norm_g)


def _fox_kernel(q_ref, k_ref, v_ref, ccol_ref, crow_ref, o_ref):
    tq = FOX_Q_BLOCK
    tk = FOX_K_BLOCK
    p = pl.program_id(1)
    i = pl.program_id(2)
    q = q_ref[...]
    lane = lax.broadcasted_iota(I32, (tq, V7X_LANES), 1)
    first = lane < F_DH
    ccol = ccol_ref[...]
    n_full = (i * tq) // tk
    qpos = i * tq + lax.broadcasted_iota(I32, (tq, tk), 0)
    kpos = n_full * tk + lax.broadcasted_iota(I32, (tq, tk), 1)
    causal = kpos <= qpos

    def head(hh):
        q_h = jnp.where(first, q, jnp.zeros_like(q)) if hh == 0 else jnp.where(first, jnp.zeros_like(q), q)
        c_q = jnp.sum(jnp.where(lane == FF_LANE + 2 * p + hh, ccol, 0.0), axis=-1, keepdims=True)

        def chunk(j, carry, masked):
            m_old, l_old, acc = carry
            start = pl.multiple_of(j * tk, tk)
            k_c = k_ref[pl.ds(start, tk), :]
            v_c = v_ref[pl.ds(start, tk), :]
            c_k = crow_ref[hh:hh + 1, pl.ds(start, tk)]
            sc = _dot_nt(q_h, k_c) + (c_q - c_k)
            if masked:
                sc = jnp.where(causal, sc, -jnp.inf)
            m_new = jnp.maximum(m_old, jnp.max(sc, axis=-1, keepdims=True))
            alpha = jnp.exp(m_old - m_new)
            pe = jnp.exp(sc - m_new)
            l_new = alpha * l_old + jnp.sum(pe, axis=-1, keepdims=True)
            acc_new = alpha * acc + _dot(pe.astype(BF16), v_c)
            return m_new, l_new, acc_new

        init = (jnp.full((tq, 1), -jnp.inf, F32), jnp.zeros((tq, 1), F32),
                jnp.zeros((tq, V7X_LANES), F32))
        carry = lax.fori_loop(0, n_full, lambda j, c: chunk(j, c, False), init)
        _, l_fin, acc = chunk(n_full, carry, True)
        return acc / l_fin

    o_ref[...] = jnp.where(first, head(0), head(1)).astype(BF16)


def _fox_attention(fq, fk, fv, ccol, crow, bsz, s):
    t = fq.shape[0]
    tq = FOX_Q_BLOCK
    nq = s // tq
    pairs = F_HEADS // 2
    qmap = lambda b, p, i: (b * nq + i, p)
    kvmap = lambda b, p, i: (b, p)
    return pl.pallas_call(
        _fox_kernel,
        grid=(bsz, pairs, nq),
        in_specs=[
            pl.BlockSpec((tq, V7X_LANES), qmap),
            pl.BlockSpec((s, V7X_LANES), kvmap),
            pl.BlockSpec((s, V7X_LANES), kvmap),
            pl.BlockSpec((tq, GATE_COLS), lambda b, p, i: (b * nq + i, 0)),
            pl.BlockSpec((None, None, V7X_SUBLANES, s), lambda b, p, i: (b, p, 0, 0)),
        ],
        out_specs=pl.BlockSpec((tq, V7X_LANES), qmap),
        out_shape=jax.ShapeDtypeStruct((t, F_W), BF16),
        compiler_params=_params(("parallel", "parallel", "arbitrary")),
        name="fox_attention",
    )(fq, fk, fv, ccol, crow)


def _layer_norm_rows(r, g, b):
    mu = jnp.mean(r, axis=-1, keepdims=True)
    d = r - mu
    var = jnp.mean(d * d, axis=-1, keepdims=True)
    return (d * lax.rsqrt(var + LN_EPS)) * g + b


def _merge_kernel(dn_alpha, hm_ref, hf_ref, x_ref, wg_ref, bg_ref, wbm_ref, wbf_ref, wo_ref,
                  g_ref, b_ref, wrh_ref, wrl_ref, br_ref, h1_ref, h1p_ref, gate_ref, tope_ref):
    x = x_ref[...]
    d = x.shape[1]
    gmf = _dot(x.astype(BF16), wg_ref[...]) + bg_ref[...]
    ym = _dot(hm_ref[...], wbm_ref[...])
    yf = _dot(hf_ref[...], wbf_ref[...])
    y = _sigmoid(gmf[:, :d]) * ym + _sigmoid(gmf[:, d:]) * yf
    mix = _dot(y.astype(BF16), wo_ref[...])
    h1 = _layer_norm_rows(dn_alpha * x + mix, g_ref[...], b_ref[...])
    h1_ref[...] = h1

    hb = h1.astype(BF16)
    bits = lax.bitcast_convert_type(hb.astype(F32), U32)
    h1p_ref[...] = (bits[:, :d // 2] >> 16) | bits[:, d // 2:]

    lo = (h1 - hb.astype(F32)).astype(BF16)
    logits = (_dot(lo, wrh_ref[...]) + _dot(hb, wrl_ref[...])) + _dot(hb, wrh_ref[...]) + br_ref[...]
    tm = logits.shape[0]
    lane = lax.broadcasted_iota(I32, (tm, V7X_LANES), 1)
    vals = jnp.where(lane < N_EXPERTS, logits, -jnp.inf)
    top_v, top_i = [], []
    for _ in range(TOP_K):
        mx = jnp.max(vals, axis=-1, keepdims=True)
        idx = jnp.min(jnp.where(vals == mx, lane, V7X_LANES), axis=-1, keepdims=True)
        top_v.append(mx)
        top_i.append(idx)
        vals = jnp.where(lane == idx, -jnp.inf, vals)
    ex = [jnp.exp(v - top_v[0]) for v in top_v]
    tot = ex[0]
    for e in ex[1:]:
        tot = tot + e
    gate = jnp.zeros((tm, V7X_LANES), F32)
    tope = jnp.zeros((tm, V7X_LANES), I32)
    for k in range(TOP_K):
        gate = jnp.where(lane == k, ex[k] / tot, gate)
        tope = jnp.where(lane == k, top_i[k], tope)
    gate_ref[...] = gate
    tope_ref[...] = tope


def _merge(dn_alpha, hm, hf, x2, wg, bg, wbm, wbf, wo, ln_g, ln_b, wr_hi, wr_lo, br):
    t, d = x2.shape
    tm = MERGE_ROWS
    row = lambda i: (i, 0)
    const = lambda i: (0, 0)
    full = lambda a: pl.BlockSpec(a.shape, const)
    return pl.pallas_call(
        functools.partial(_merge_kernel, dn_alpha),
        grid=(t // tm,),
        in_specs=[
            pl.BlockSpec((tm, M_W), row),
            pl.BlockSpec((tm, F_W), row),
            pl.BlockSpec((tm, d), row),
            full(wg), full(bg), full(wbm), full(wbf), full(wo), full(ln_g), full(ln_b),
            full(wr_hi), full(wr_lo), full(br),
        ],
        out_specs=(
            pl.BlockSpec((tm, d), row),
            pl.BlockSpec((tm, d // 2), row),
            pl.BlockSpec((tm, V7X_LANES), row),
            pl.BlockSpec((tm, V7X_LANES), row),
        ),
        out_shape=(
            jax.ShapeDtypeStruct((t, d), F32),
            jax.ShapeDtypeStruct((t, d // 2), U32),
            jax.ShapeDtypeStruct((t, V7X_LANES), F32),
            jax.ShapeDtypeStruct((t, V7X_LANES), I32),
        ),
        compiler_params=_params(("parallel",)),
        name="merge_ln1_router",
    )(hm, hf, x2, wg, bg, wbm, wbf, wo, ln_g, ln_b, wr_hi, wr_lo, br)


def _lane_cumsum(x):
    lane = lax.broadcasted_iota(I32, x.shape, 1)
    d = 1
    while d < V7X_LANES:
        x = x + jnp.where(lane >= d, pltpu.roll(x, shift=d, axis=1), 0.0)
        d *= 2
    return x


def _routing_kernel(tope_ref, dest_ref, table_ref, cnt_ref, start_ref):
    ph = pl.program_id(0)
    i = pl.program_id(1)
    tr = tope_ref.shape[0]
    tope = tope_ref[...]
    lane = lax.broadcasted_iota(I32, (tr, V7X_LANES), 1)
    hit = [lane == tope[:, k:k + 1] for k in range(TOP_K)]
    member = jnp.zeros((tr, V7X_LANES), F32)
    for k in range(TOP_K):
        member = member + hit[k].astype(F32)
    tile_count = jnp.sum(member, axis=0, keepdims=True)

    @pl.when(jnp.logical_and(ph == 0, i == 0))
    def _():
        cnt_ref[...] = jnp.zeros(cnt_ref.shape, F32)
        table_ref[...] = jnp.zeros(table_ref.shape, I32)

    @pl.when(ph == 0)
    def _():
        cnt_ref[...] = cnt_ref[...] + tile_count
        dest_ref[...] = jnp.zeros(dest_ref.shape, I32)

    @pl.when(jnp.logical_and(ph == 1, i == 0))
    def _():
        counts = cnt_ref[...]
        padded = jnp.ceil(counts * (1.0 / MOE_BLOCK)) * MOE_BLOCK
        pad_end = _lane_cumsum(padded)
        pad_start = pad_end - padded
        start_ref[...] = pad_start
        cnt_ref[...] = jnp.zeros(cnt_ref.shape, F32)
        nb = table_ref.shape[0]
        blk = lax.broadcasted_iota(I32, (nb, V7X_LANES), 0).astype(F32) * MOE_BLOCK
        ln = lax.broadcasted_iota(I32, (nb, V7X_LANES), 1)
        done = jnp.logical_and(pad_end[0:1, :] <= blk, ln < N_EXPERTS)
        be = jnp.minimum(jnp.sum(done.astype(F32), axis=-1, keepdims=True), N_EXPERTS - 1.0)
        onehot = ln == be.astype(I32)
        cnt_e = jnp.sum(jnp.where(onehot, counts[0:1, :], 0.0), axis=-1, keepdims=True)
        start_e = jnp.sum(jnp.where(onehot, pad_start[0:1, :], 0.0), axis=-1, keepdims=True)
        valid = jnp.clip(cnt_e - (blk[:, 0:1] - start_e), 0.0, float(MOE_BLOCK))
        table_ref[...] = jnp.where(ln == 0, be.astype(I32),
                                   jnp.where(ln == 1, valid.astype(I32), 0))

    @pl.when(ph == 1)
    def _():
        before = _dot(_tril_mask(tr, strict=True).astype(BF16), member.astype(BF16))
        base = before + cnt_ref[0:1, :] + start_ref[0:1, :]
        dest = jnp.zeros((tr, V7X_LANES), I32)
        for k in range(TOP_K):
            dk = jnp.sum(jnp.where(hit[k], base, 0.0), axis=-1, keepdims=True)
            dest = jnp.where(lane == k, dk.astype(I32), dest)
        dest_ref[...] = dest
        cnt_ref[...] = cnt_ref[...] + tile_count


def _routing(tope, n_blocks):
    t = tope.shape[0]
    tr = ROUTE_ROWS
    return pl.pallas_call(
        _routing_kernel,
        grid=(2, t // tr),
        in_specs=[pl.BlockSpec((tr, V7X_LANES), lambda ph, i: (i, 0))],
        out_specs=(
            pl.BlockSpec((tr, V7X_LANES), lambda ph, i: (i * ph, 0)),
            pl.BlockSpec((n_blocks, V7X_LANES), lambda ph, i: (0, 0)),
        ),
        out_shape=(
            jax.ShapeDtypeStruct((t, V7X_LANES), I32),
            jax.ShapeDtypeStruct((n_blocks, V7X_LANES), I32),
        ),
        scratch_shapes=[
            pltpu.VMEM((V7X_SUBLANES, V7X_LANES), F32),
            pltpu.VMEM((V7X_SUBLANES, V7X_LANES), F32),
        ],
        compiler_params=_params(("arbitrary", "arbitrary")),
        name="routing",
    )(tope)


def _sc_worker_id():
    return lax.axis_index("s") * V7X_SC_CORES + lax.axis_index("c")


def _sc_mesh():
    return plsc.VectorSubcoreMesh(core_axis_name="c", subcore_axis_name="s",
                                  num_cores=V7X_SC_CORES, num_subcores=V7X_SC_SUBCORES)


def _sc_dispatch(dest_km, h1p, n_rows):
    t, w = h1p.shape
    per_worker = t // V7X_SC_WORKERS
    ch = SC_ROWS_PER_DMA

    @functools.partial(
        pl.kernel, mesh=_sc_mesh(),
        out_type=jax.ShapeDtypeStruct((n_rows, w), h1p.dtype),
        scratch_types=[pltpu.VMEM((ch,), I32), pltpu.VMEM((ch, w), h1p.dtype), pltpu.SemaphoreType.DMA],
        name="sc_dispatch",
    )
    def scatter_rows(dest_hbm, h1p_hbm, xs_hbm, idx_v, rows_v, sem):
        first = _sc_worker_id() * per_worker

        @pl.loop(0, per_worker // ch)
        def _(j):
            base = first + j * ch
            pltpu.sync_copy(h1p_hbm.at[pl.ds(base, ch)], rows_v)
            for k in range(TOP_K):
                pltpu.sync_copy(dest_hbm.at[pl.ds(k * t + base, ch)], idx_v)
                pltpu.async_copy(rows_v, xs_hbm.at[idx_v], sem).wait()

    return scatter_rows(dest_km, h1p)


def _sc_gather(dest_km, y_rows):
    n = dest_km.shape[0]
    w = y_rows.shape[1]
    per_worker = n // V7X_SC_WORKERS
    ch = SC_ROWS_PER_DMA

    @functools.partial(
        pl.kernel, mesh=_sc_mesh(),
        out_type=jax.ShapeDtypeStruct((n, w), y_rows.dtype),
        scratch_types=[pltpu.VMEM((ch,), I32), pltpu.VMEM((ch, w), y_rows.dtype), pltpu.SemaphoreType.DMA],
        name="sc_gather",
    )
    def gather_rows(dest_hbm, y_hbm, out_hbm, idx_v, rows_v, sem):
        first = _sc_worker_id() * per_worker

        @pl.loop(0, per_worker // ch)
        def _(j):
            base = first + j * ch
            pltpu.sync_copy(dest_hbm.at[pl.ds(base, ch)], idx_v)
            pltpu.async_copy(y_hbm.at[idx_v], rows_v, sem).wait()
            pltpu.sync_copy(rows_v, out_hbm.at[pl.ds(base, ch)])

    return gather_rows(dest_km, y_rows)


def _expert_kernel(be_ref, nv_ref, xs_ref, wgu_ref, bgu_ref, wdn_ref, bdn_ref, y_ref):
    i = pl.program_id(0)
    nv = nv_ref[i]

    @pl.when(nv == 0)
    def _():
        y_ref[...] = jnp.zeros(y_ref.shape, F32)

    @pl.when(nv > 0)
    def _():
        words = xs_ref[...]
        lo = lax.bitcast_convert_type(words << 16, F32)
        hi = lax.bitcast_convert_type(words & jnp.uint32(0xFFFF0000), F32)
        x = jnp.concatenate([lo, hi], axis=1)
        rowid = lax.broadcasted_iota(I32, x.shape, 0)
        x = jnp.where(rowid < nv, x, 0.0).astype(BF16)
        gu = _dot(x, wgu_ref[...]) + bgu_ref[...]
        f = gu.shape[1] // 2
        glu = jnp.minimum(gu[:, :f], SWIGLU_LIMIT)
        lin = jnp.clip(gu[:, f:], -SWIGLU_LIMIT, SWIGLU_LIMIT)
        act = glu * _sigmoid(SWIGLU_ALPHA * glu) * (lin + 1.0)
        y_ref[...] = _dot(act.astype(BF16), wdn_ref[...]) + bdn_ref[...]


def _experts(block_e, block_valid, xs, wgu, bgu, wdn, bdn):
    n_rows, w = xs.shape
    e, d, f2 = wgu.shape
    n_blocks = n_rows // MOE_BLOCK
    grid_spec = pltpu.PrefetchScalarGridSpec(
        num_scalar_prefetch=2,
        grid=(n_blocks,),
        in_specs=[
            pl.BlockSpec((MOE_BLOCK, w), lambda i, be, nv: (i, 0)),
            pl.BlockSpec((None, d, f2), lambda i, be, nv: (be[i], 0, 0)),
            pl.BlockSpec((None, 1, f2), lambda i, be, nv: (be[i], 0, 0)),
            pl.BlockSpec((None, f2 // 2, d), lambda i, be, nv: (be[i], 0, 0)),
            pl.BlockSpec((None, 1, d), lambda i, be, nv: (be[i], 0, 0)),
        ],
        out_specs=pl.BlockSpec((MOE_BLOCK, d), lambda i, be, nv: (i, 0)),
    )
    return pl.pallas_call(
        _expert_kernel,
        grid_spec=grid_spec,
        out_shape=jax.ShapeDtypeStruct((n_rows, d), F32),
        compiler_params=_params(("arbitrary",)),
        name="experts",
    )(block_e, block_valid, xs, wgu, bgu, wdn, bdn)


def _combine_kernel(dn_alpha, h1_ref, gate_ref, g_ref, b_ref, yg_ref, o_ref):
    gate = gate_ref[...]
    ffn = gate[:, 0:1] * yg_ref[0]
    for k in range(1, TOP_K):
        ffn = ffn + gate[:, k:k + 1] * yg_ref[k]
    o_ref[...] = _layer_norm_rows(dn_alpha * h1_ref[...] + ffn, g_ref[...], b_ref[...])


def _combine(dn_alpha, h1, gate, ln_g, ln_b, yg):
    t, d = h1.shape
    tc = COMBINE_ROWS
    row = lambda i: (i, 0)
    const = lambda i: (0, 0)
    return pl.pallas_call(
        functools.partial(_combine_kernel, dn_alpha),
        grid=(t // tc,),
        in_specs=[
            pl.BlockSpec((tc, d), row),
            pl.BlockSpec((tc, V7X_LANES), row),
            pl.BlockSpec((1, d), const),
            pl.BlockSpec((1, d), const),
            pl.BlockSpec((TOP_K, tc, d), lambda i: (0, i, 0)),
        ],
        out_specs=pl.BlockSpec((tc, d), row),
        out_shape=jax.ShapeDtypeStruct((t, d), F32),
        compiler_params=_params(("parallel",)),
        name="combine_ln2",
    )(h1, gate, ln_g, ln_b, yg)


def _pack_in_proj(w_in, b_in):
    d = w_in.shape[0]
    o = 0
    cols = {}
    for name, width in (("mqk", MQK_W), ("mv", M_W), ("mo", M_W), ("mi", M_HEADS), ("mf", M_HEADS),
                        ("fq", F_W), ("fk", F_W), ("fv", F_W), ("ff", F_HEADS), ("gm", d), ("gf", d)):
        cols[name] = (o, o + width)
        o += width

    def take(a, names):
        return [a[..., cols[n][0]:cols[n][1]] for n in names]

    n_gate = 2 * M_HEADS + F_HEADS
    main = ("mqk", "mv", "mo", "fq", "fk", "fv", "mi", "mf", "ff")
    w_main = jnp.concatenate(take(w_in, main) + [jnp.zeros((d, GATE_COLS - n_gate), w_in.dtype)], axis=1)
    b_main = jnp.concatenate(take(b_in, main) + [jnp.zeros((GATE_COLS - n_gate,), b_in.dtype)])
    w_gate = jnp.concatenate(take(w_in, ("gm", "gf")), axis=1)
    b_gate = jnp.concatenate(take(b_in, ("gm", "gf")))
    return w_main.astype(BF16), b_main[None, :], w_gate.astype(BF16), b_gate[None, :]


def _layer(h, depth, w_in, b_in, m_conv_w, m_conv_b, m_norm_g, w_bm, w_bf, w_o, ln1_g, ln1_b,
           w_router, b_router, w_gu, b_gu, w_dn, b_dn, ln2_g, ln2_b):
    bsz, s, d = h.shape
    t = bsz * s
    dn_alpha = (2.0 * depth) ** 0.25
    x2 = h.reshape(t, d)

    w_main, b_main, w_gate, b_gate = _pack_in_proj(w_in, b_in)
    mqk, mv, mo, fq, fk, fv, gates = _inproj(x2, w_main, b_main)
    ccol, crow = _fox_cumsum(gates, bsz, s)
    hm = _mlstm(mqk, mv, mo, gates, m_conv_w, m_conv_b[None, :], m_norm_g[None, :], bsz, s)
    hf = _fox_attention(fq, fk, fv, ccol, crow, bsz, s)

    n_exp = w_router.shape[1]
    wr = jnp.zeros((d, V7X_LANES), F32).at[:, :n_exp].set(w_router)
    wr_hi = wr.astype(BF16)
    wr_lo = (wr - wr_hi.astype(F32)).astype(BF16)
    br = jnp.zeros((1, V7X_LANES), F32).at[0, :n_exp].set(b_router)
    h1, h1p, gate, tope = _merge(
        dn_alpha, hm, hf, x2, w_gate, b_gate, w_bm.astype(BF16), w_bf.astype(BF16), w_o.astype(BF16),
        ln1_g[None, :], ln1_b[None, :], wr_hi, wr_lo, br)

    n_blocks = -(-(t * TOP_K) // MOE_BLOCK) + N_EXPERTS
    dest, table = _routing(tope, n_blocks)
    dest_km = dest[:, :TOP_K].T.reshape(TOP_K * t)
    block_e, block_valid = table[:, 0], table[:, 1]
    xs = _sc_dispatch(dest_km, h1p, n_blocks * MOE_BLOCK)
    y_rows = _experts(block_e, block_valid, xs, w_gu.astype(BF16), b_gu[:, None, :],
                      w_dn.astype(BF16), b_dn[:, None, :])
    yg = _sc_gather(dest_km, y_rows).reshape(TOP_K, t, d)
    out = _combine(dn_alpha, h1, gate, ln2_g[None, :], ln2_b[None, :], yg)
    return out.reshape(bsz, s, d)


def kernel(x, w_in, b_in, m_conv_w, m_conv_b, m_norm_g, w_bm, w_bf, w_o, ln1_g, ln1_b,
           w_router, b_router, w_gu, b_gu, w_dn, b_dn, ln2_g, ln2_b):
    depth = w_in.shape[0]
    h = x
    for l in range(depth):
        h = _layer(h, depth, w_in[l], b_in[l], m_conv_w[l], m_conv_b[l], m_norm_g[l], w_bm[l], w_bf[l],
                   w_o[l], ln1_g[l], ln1_b[l], w_router[l], b_router[l], w_gu[l], b_gu[l], w_dn[l],
                   b_dn[l], ln2_g[l], ln2_b[l])
    return h
```

```python
import functools
import math

import jax
import jax.numpy as jnp
from jax import lax
from jax.experimental import pallas as pl
from jax.experimental.pallas import tpu as pltpu
from jax.experimental.pallas import tpu_sc as plsc

F32 = jnp.float32
BF16 = jnp.bfloat16
I32 = jnp.int32
U32 = jnp.uint32

M_HEADS = 4
M_DQK = 128
M_DV = 128
CONV_W = 4
F_HEADS = 8
F_DH = 64
N_EXPERTS = 32
TOP_K = 4
SWIGLU_ALPHA = 1.702
SWIGLU_LIMIT = 7.0
LN_EPS = 1e-5

M_W = M_HEADS * M_DV
F_W = F_HEADS * F_DH
MQK_W = 2 * M_HEADS * M_DQK

V7X_LANES = 128
V7X_SUBLANES = 8
V7X_VMEM_BYTES = 64 * 1024 * 1024
VMEM_LIMIT_BYTES = (V7X_VMEM_BYTES * 3) // 4
V7X_SC_CORES = 2
V7X_SC_SUBCORES = 16
V7X_SC_WORKERS = V7X_SC_CORES * V7X_SC_SUBCORES

INPROJ_ROWS = 512
MLSTM_CHUNK = 256
FOX_Q_BLOCK = 256
FOX_K_BLOCK = 512
CUMSUM_BLOCK = 256
MERGE_ROWS = 256
ROUTE_ROWS = 1024
ROUTE_SUB_ROWS = 256
MOE_BLOCK = 256
SC_ROWS_PER_DMA = 128
COMBINE_ROWS = 128

GATE_COLS = V7X_LANES
MI_LANE = 0
MF_LANE = M_HEADS
FF_LANE = 2 * M_HEADS


def _params(semantics):
    return pltpu.CompilerParams(dimension_semantics=semantics, vmem_limit_bytes=VMEM_LIMIT_BYTES)


def _log_sigmoid(x):
    return jnp.minimum(x, 0.0) - jnp.log1p(jnp.exp(-jnp.abs(x)))


def _sigmoid(x):
    return 1.0 / (1.0 + jnp.exp(-x))


def _dot(a, b):
    return jnp.dot(a, b, preferred_element_type=F32)


def _dot_nt(a, b):
    return lax.dot_general(a, b, (((1,), (1,)), ((), ())), preferred_element_type=F32)


def _dot_tn(a, b):
    return lax.dot_general(a, b, (((0,), (0,)), ((), ())), preferred_element_type=F32)


def _split3(x):
    hi = x.astype(BF16)
    r1 = x - hi.astype(F32)
    mid = r1.astype(BF16)
    lo = (r1 - mid.astype(F32)).astype(BF16)
    return hi, mid, lo


def _dot_mask_f32(mask_bf16, x):
    hi, mid, lo = _split3(x)
    return (_dot(mask_bf16, lo) + _dot(mask_bf16, mid)) + _dot(mask_bf16, hi)


def _pack_bf16_pairs(x):
    half = x.shape[1] // 2
    bits = lax.bitcast_convert_type(x.astype(BF16).astype(F32), U32)
    return (bits[:, :half] >> 16) | bits[:, half:]


def _unpack_bf16_pairs(words):
    lo = lax.bitcast_convert_type(words << 16, F32)
    hi = lax.bitcast_convert_type(words & jnp.uint32(0xFFFF0000), F32)
    return jnp.concatenate([lo, hi], axis=1)


def _tril_mask(n, strict=False):
    r = lax.broadcasted_iota(I32, (n, n), 0)
    c = lax.broadcasted_iota(I32, (n, n), 1)
    return (r > c) if strict else (r >= c)


_OFF_MQK = 0
_OFF_MV = _OFF_MQK + MQK_W
_OFF_MO = _OFF_MV + M_W
_OFF_FQ = _OFF_MO + M_W
_OFF_FK = _OFF_FQ + F_W
_OFF_FV = _OFF_FK + F_W
_OFF_GATES = _OFF_FV + F_W
_PACKED_COLS = _OFF_GATES + GATE_COLS


def _inproj_kernel(x_ref, w_ref, b_ref, mqk_ref, mv_ref, mo_ref, fq_ref, fk_ref, fv_ref, gates_ref):
    xb = x_ref[...].astype(BF16)

    def seg(lo, width):
        return _dot(xb, w_ref[:, lo:lo + width]) + b_ref[:, lo:lo + width]

    mqk_ref[...] = seg(_OFF_MQK, MQK_W)
    mv_ref[...] = seg(_OFF_MV, M_W).astype(BF16)
    mo_ref[...] = seg(_OFF_MO, M_W)
    fq_ref[...] = (seg(_OFF_FQ, F_W) * (F_DH ** -0.5)).astype(BF16)
    fk_ref[...] = seg(_OFF_FK, F_W).astype(BF16)
    fv_ref[...] = seg(_OFF_FV, F_W).astype(BF16)
    gates_ref[...] = seg(_OFF_GATES, GATE_COLS)


def _inproj(x2, w_packed, b_packed):
    t, d = x2.shape
    tm = INPROJ_ROWS
    row = lambda i: (i, 0)
    const = lambda i: (0, 0)
    out_shapes = (
        jax.ShapeDtypeStruct((t, MQK_W), F32),
        jax.ShapeDtypeStruct((t, M_W), BF16),
        jax.ShapeDtypeStruct((t, M_W), F32),
        jax.ShapeDtypeStruct((t, F_W), BF16),
        jax.ShapeDtypeStruct((t, F_W), BF16),
        jax.ShapeDtypeStruct((t, F_W), BF16),
        jax.ShapeDtypeStruct((t, GATE_COLS), F32),
    )
    return pl.pallas_call(
        _inproj_kernel,
        grid=(t // tm,),
        in_specs=[
            pl.BlockSpec((tm, d), row),
            pl.BlockSpec((d, _PACKED_COLS), const),
            pl.BlockSpec((1, _PACKED_COLS), const),
        ],
        out_specs=tuple(pl.BlockSpec((tm, s.shape[1]), row) for s in out_shapes),
        out_shape=out_shapes,
        compiler_params=_params(("parallel",)),
        name="inproj",
    )(x2, w_packed, b_packed)


def _fox_cumsum_kernel(g_ref, ccol_ref, crow_ref):
    s = g_ref.shape[0]
    cb = CUMSUM_BLOCK
    tri = _tril_mask(cb).astype(BF16)
    carry = jnp.zeros((1, GATE_COLS), F32)
    for j in range(s // cb):
        lsg = _log_sigmoid(g_ref[j * cb:(j + 1) * cb, :])
        cs = _dot_mask_f32(tri, lsg) + carry
        ccol_ref[j * cb:(j + 1) * cb, :] = cs
        carry = cs[cb - 1:cb, :]
    ct = ccol_ref[...].T
    rows = ct[FF_LANE:FF_LANE + F_HEADS, :]
    sub = lax.broadcasted_iota(I32, rows.shape, 0)
    for p in range(F_HEADS // 2):
        shifted = rows if p == 0 else pltpu.roll(rows, shift=F_HEADS - 2 * p, axis=0)
        crow_ref[p] = jnp.where(sub < 2, shifted, 0.0)


def _fox_cumsum(gates, bsz, s):
    t = gates.shape[0]
    return pl.pallas_call(
        _fox_cumsum_kernel,
        grid=(bsz,),
        in_specs=[pl.BlockSpec((s, GATE_COLS), lambda b: (b, 0))],
        out_specs=(
            pl.BlockSpec((s, GATE_COLS), lambda b: (b, 0)),
            pl.BlockSpec((None, F_HEADS // 2, V7X_SUBLANES, s), lambda b: (b, 0, 0, 0)),
        ),
        out_shape=(
            jax.ShapeDtypeStruct((t, GATE_COLS), F32),
            jax.ShapeDtypeStruct((bsz, F_HEADS // 2, V7X_SUBLANES, s), F32),
        ),
        compiler_params=_params(("parallel",)),
        name="fox_cumsum",
    )(gates)


def _mlstm_kernel(mqk_ref, mv_ref, mo_ref, gates_ref, cw_ref, cb_ref, ng_ref, hm_ref,
                  ext_ref, state_ref, m_ref):
    L = MLSTM_CHUNK
    pad = V7X_SUBLANES
    n = pl.program_id(1)

    @pl.when(n == 0)
    def _():
        ext_ref[0:pad, :] = jnp.zeros((pad, MQK_W), F32)
        state_ref[...] = jnp.zeros(state_ref.shape, F32)
        m_ref[...] = jnp.zeros(m_ref.shape, F32)

    ext_ref[pad:pad + L, :] = mqk_ref[...]
    y = cb_ref[...] + cw_ref[CONV_W - 1:CONV_W, :] * ext_ref[pad:pad + L, :]
    for k in range(CONV_W - 1):
        shift = CONV_W - 1 - k
        y = y + cw_ref[k:k + 1, :] * ext_ref[pad - shift:pad - shift + L, :]
    ext_ref[0:pad, :] = ext_ref[L:L + pad, :]
    qk = y * _sigmoid(y)

    gates = gates_ref[...]
    lsg = _log_sigmoid(gates)
    tri = _tril_mask(L)
    bfull = _dot_mask_f32(tri.astype(BF16), lsg)
    gates_t = gates.T
    bfull_t = bfull.T
    lane = lax.broadcasted_iota(I32, (L, M_DV), 1)
    ones_col = (lane == 0).astype(BF16)

    for h in range(M_HEADS):
        i_col = gates[:, MI_LANE + h:MI_LANE + h + 1]
        b_col = bfull[:, MF_LANE + h:MF_LANE + h + 1]
        i_row = gates_t[MI_LANE + h:MI_LANE + h + 1, :]
        b_row = bfull_t[MF_LANE + h:MF_LANE + h + 1, :]
        g_tot = b_col[L - 1:L, :]
        m_prev = m_ref[h][0:1, 0:1]

        q_h = qk[:, h * M_DQK:(h + 1) * M_DQK].astype(BF16)
        k_f = qk[:, MQK_W // 2 + h * M_DQK:MQK_W // 2 + (h + 1) * M_DQK] * (M_DQK ** -0.5)
        k_h = k_f.astype(BF16)
        v_h = mv_ref[:, h * M_DV:(h + 1) * M_DV]
        cn = state_ref[h]

        dlog = jnp.where(tri, (b_col - b_row) + i_row, -jnp.inf)
        inter_log = b_col + m_prev
        m_t = jnp.maximum(inter_log, jnp.max(dlog, axis=-1, keepdims=True))
        w_inter = jnp.exp(inter_log - m_t)
        w_intra = jnp.exp(dlog - m_t)
        qkw = _dot_nt(q_h, k_h) * w_intra
        qc = _dot(q_h, cn.astype(BF16))
        num = w_inter * qc[:, :M_DV] + _dot(qkw.astype(BF16), v_h)
        den = w_inter * qc[:, M_DV:M_DV + 1] + jnp.sum(qkw, axis=-1, keepdims=True)
        hh = num / jnp.maximum(jnp.abs(den), jnp.exp(-m_t))

        mu = jnp.mean(hh, axis=-1, keepdims=True)
        dv = hh - mu
        var = jnp.mean(dv * dv, axis=-1, keepdims=True)
        hn = (dv * lax.rsqrt(var + LN_EPS)) * ng_ref[:, h * M_DV:(h + 1) * M_DV]
        hm_ref[:, h * M_DV:(h + 1) * M_DV] = (
            _sigmoid(mo_ref[:, h * M_DV:(h + 1) * M_DV]) * hn).astype(BF16)

        a_col = (g_tot - b_col) + i_col
        m_new = jnp.maximum(g_tot + m_prev, jnp.max(a_col, axis=0, keepdims=True))
        decay = jnp.exp(g_tot + m_prev - m_new)
        w_col = jnp.exp(a_col - m_new)
        kw = (k_f * w_col).astype(BF16)
        v_aug = jnp.concatenate([v_h, ones_col], axis=1)
        state_ref[h] = decay * cn + _dot_tn(kw, v_aug)
        m_ref[h] = jnp.broadcast_to(m_new, m_ref.shape[1:])


def _mlstm(mqk, mv, mo, gates, conv_w, conv_b, norm_g, bsz, s):
    t = mqk.shape[0]
    L = MLSTM_CHUNK
    nc = s // L
    row = lambda b, n: (b * nc + n, 0)
    const = lambda b, n: (0, 0)
    return pl.pallas_call(
        _mlstm_kernel,
        grid=(bsz, nc),
        in_specs=[
            pl.BlockSpec((L, MQK_W), row),
            pl.BlockSpec((L, M_W), row),
            pl.BlockSpec((L, M_W), row),
            pl.BlockSpec((L, GATE_COLS), row),
            pl.BlockSpec((CONV_W, MQK_W), const),
            pl.BlockSpec((1, MQK_W), const),
            pl.BlockSpec((1, M_W), const),
        ],
        out_specs=pl.BlockSpec((L, M_W), row),
        out_shape=jax.ShapeDtypeStruct((t, M_W), BF16),
        scratch_shapes=[
            pltpu.VMEM((L + V7X_SUBLANES, MQK_W), F32),
            pltpu.VMEM((M_HEADS, M_DQK, 2 * M_DV), F32),
            pltpu.VMEM((M_HEADS, V7X_SUBLANES, V7X_LANES), F32),
        ],
        compiler_params=_params(("parallel", "arbitrary")),
        name="mlstm",
    )(mqk, mv, mo, gates, conv_w, conv_b, norm_g)


def _fox_kernel(q_ref, k_ref, v_ref, ccol_ref, crow_ref, o_ref):
    tq = FOX_Q_BLOCK
    tk = FOX_K_BLOCK
    p = pl.program_id(1)
    i = pl.program_id(2)
    q = q_ref[...]
    lane = lax.broadcasted_iota(I32, (tq, V7X_LANES), 1)
    first = lane < F_DH
    ccol = ccol_ref[...]
    n_full = (i * tq) // tk
    qpos = i * tq + lax.broadcasted_iota(I32, (tq, tk), 0)
    kpos = n_full * tk + lax.broadcasted_iota(I32, (tq, tk), 1)
    causal = kpos <= qpos

    def head(hh):
        q_h = jnp.where(first, q, jnp.zeros_like(q)) if hh == 0 else jnp.where(first, jnp.zeros_like(q), q)
        c_q = jnp.sum(jnp.where(lane == FF_LANE + 2 * p + hh, ccol, 0.0), axis=-1, keepdims=True)

        def chunk(j, carry, masked):
            m_old, l_old, acc = carry
            start = pl.multiple_of(j * tk, tk)
            k_c = k_ref[pl.ds(start, tk), :]
            v_c = v_ref[pl.ds(start, tk), :]
            c_k = crow_ref[hh:hh + 1, pl.ds(start, tk)]
            sc = _dot_nt(q_h, k_c) + (c_q - c_k)
            if masked:
                sc = jnp.where(causal, sc, -jnp.inf)
            m_new = jnp.maximum(m_old, jnp.max(sc, axis=-1, keepdims=True))
            alpha = jnp.exp(m_old - m_new)
            pe = jnp.exp(sc - m_new)
            l_new = alpha * l_old + jnp.sum(pe, axis=-1, keepdims=True)
            acc_new = alpha * acc + _dot(pe.astype(BF16), v_c)
            return m_new, l_new, acc_new

        init = (jnp.full((tq, 1), -jnp.inf, F32), jnp.zeros((tq, 1), F32),
                jnp.zeros((tq, V7X_LANES), F32))
        carry = lax.fori_loop(0, n_full, lambda j, c: chunk(j, c, False), init)
        _, l_fin, acc = chunk(n_full, carry, True)
        return acc / l_fin

    o_ref[...] = jnp.where(first, head(0), head(1)).astype(BF16)


def _fox_attention(fq, fk, fv, ccol, crow, bsz, s):
    t = fq.shape[0]
    tq = FOX_Q_BLOCK
    nq = s // tq
    pairs = F_HEADS // 2
    qmap = lambda b, p, i: (b * nq + i, p)
    kvmap = lambda b, p, i: (b, p)
    return pl.pallas_call(
        _fox_kernel,
        grid=(bsz, pairs, nq),
        in_specs=[
            pl.BlockSpec((tq, V7X_LANES), qmap),
            pl.BlockSpec((s, V7X_LANES), kvmap),
            pl.BlockSpec((s, V7X_LANES), kvmap),
            pl.BlockSpec((tq, GATE_COLS), lambda b, p, i: (b * nq + i, 0)),
            pl.BlockSpec((None, None, V7X_SUBLANES, s), lambda b, p, i: (b, p, 0, 0)),
        ],
        out_specs=pl.BlockSpec((tq, V7X_LANES), qmap),
        out_shape=jax.ShapeDtypeStruct((t, F_W), BF16),
        compiler_params=_params(("parallel", "parallel", "arbitrary")),
        name="fox_attention",
    )(fq, fk, fv, ccol, crow)


def _layer_norm_rows(r, g, b):
    mu = jnp.mean(r, axis=-1, keepdims=True)
    d = r - mu
    var = jnp.mean(d * d, axis=-1, keepdims=True)
    return (d * lax.rsqrt(var + LN_EPS)) * g + b


def _merge_kernel(dn_alpha, hm_ref, hf_ref, x_ref, wg_ref, bg_ref, wbm_ref, wbf_ref, wo_ref,
                  g_ref, b_ref, wrh_ref, wrl_ref, br_ref, h1_ref, h1p_ref, gate_ref, tope_ref, cnt_ref):
    x = x_ref[...]
    d = x.shape[1]
    gmf = _dot(x.astype(BF16), wg_ref[...]) + bg_ref[...]
    ym = _dot(hm_ref[...], wbm_ref[...])
    yf = _dot(hf_ref[...], wbf_ref[...])
    y = _sigmoid(gmf[:, :d]) * ym + _sigmoid(gmf[:, d:]) * yf
    mix = _dot(y.astype(BF16), wo_ref[...])
    h1 = _layer_norm_rows(dn_alpha * x + mix, g_ref[...], b_ref[...])
    h1_ref[...] = h1

    h1p_ref[...] = _pack_bf16_pairs(h1)
    hb = h1.astype(BF16)

    lo = (h1 - hb.astype(F32)).astype(BF16)
    logits = (_dot(lo, wrh_ref[...]) + _dot(hb, wrl_ref[...])) + _dot(hb, wrh_ref[...]) + br_ref[...]
    tm = logits.shape[0]
    lane = lax.broadcasted_iota(I32, (tm, V7X_LANES), 1)
    vals = jnp.where(lane < N_EXPERTS, logits, -jnp.inf)
    top_v, top_i = [], []
    for _ in range(TOP_K):
        mx = jnp.max(vals, axis=-1, keepdims=True)
        idx = jnp.min(jnp.where(vals == mx, lane, V7X_LANES), axis=-1, keepdims=True)
        top_v.append(mx)
        top_i.append(idx)
        vals = jnp.where(lane == idx, -jnp.inf, vals)
    ex = [jnp.exp(v - top_v[0]) for v in top_v]
    tot = ex[0]
    for e in ex[1:]:
        tot = tot + e
    gate = jnp.zeros((tm, V7X_LANES), F32)
    tope = jnp.zeros((tm, V7X_LANES), I32)
    member = jnp.zeros((tm, V7X_LANES), F32)
    for k in range(TOP_K):
        gate = jnp.where(lane == k, ex[k] / tot, gate)
        tope = jnp.where(lane == k, top_i[k], tope)
        member = member + (lane == top_i[k]).astype(F32)
    gate_ref[...] = gate
    tope_ref[...] = tope
    sub = lax.broadcasted_iota(I32, cnt_ref.shape, 0)
    cnt_ref[...] = jnp.where(sub == 0, jnp.sum(member, axis=0, keepdims=True), 0.0)


def _merge(dn_alpha, hm, hf, x2, wg, bg, wbm, wbf, wo, ln_g, ln_b, wr_hi, wr_lo, br):
    t, d = x2.shape
    tm = MERGE_ROWS
    row = lambda i: (i, 0)
    const = lambda i: (0, 0)
    full = lambda a: pl.BlockSpec(a.shape, const)
    return pl.pallas_call(
        functools.partial(_merge_kernel, dn_alpha),
        grid=(t // tm,),
        in_specs=[
            pl.BlockSpec((tm, M_W), row),
            pl.BlockSpec((tm, F_W), row),
            pl.BlockSpec((tm, d), row),
            full(wg), full(bg), full(wbm), full(wbf), full(wo), full(ln_g), full(ln_b),
            full(wr_hi), full(wr_lo), full(br),
        ],
        out_specs=(
            pl.BlockSpec((tm, d), row),
            pl.BlockSpec((tm, d // 2), row),
            pl.BlockSpec((tm, V7X_LANES), row),
            pl.BlockSpec((tm, V7X_LANES), row),
            pl.BlockSpec((V7X_SUBLANES, V7X_LANES), row),
        ),
        out_shape=(
            jax.ShapeDtypeStruct((t, d), F32),
            jax.ShapeDtypeStruct((t, d // 2), U32),
            jax.ShapeDtypeStruct((t, V7X_LANES), F32),
            jax.ShapeDtypeStruct((t, V7X_LANES), I32),
            jax.ShapeDtypeStruct((t // tm * V7X_SUBLANES, V7X_LANES), F32),
        ),
        compiler_params=_params(("parallel",)),
        name="merge_ln1_router",
    )(hm, hf, x2, wg, bg, wbm, wbf, wo, ln_g, ln_b, wr_hi, wr_lo, br)


def _lane_cumsum(x):
    lane = lax.broadcasted_iota(I32, x.shape, 1)
    d = 1
    while d < V7X_LANES:
        x = x + jnp.where(lane >= d, pltpu.roll(x, shift=d, axis=1), 0.0)
        d *= 2
    return x


def _routing_kernel(cnt_ref, tope_ref, dest_ref, table_ref, run_ref, start_ref):
    sb = ROUTE_SUB_ROWS

    @pl.when(pl.program_id(0) == 0)
    def _():
        total = jnp.sum(cnt_ref[...], axis=0, keepdims=True)
        counts = jnp.broadcast_to(total, (V7X_SUBLANES, V7X_LANES))
        padded = jnp.ceil(counts * (1.0 / MOE_BLOCK)) * MOE_BLOCK
        pad_end = _lane_cumsum(padded)
        pad_start = pad_end - padded
        start_ref[...] = pad_start
        run_ref[...] = jnp.zeros(run_ref.shape, F32)
        nb = table_ref.shape[0]
        blk = lax.broadcasted_iota(I32, (nb, V7X_LANES), 0).astype(F32) * MOE_BLOCK
        ln = lax.broadcasted_iota(I32, (nb, V7X_LANES), 1)
        done = jnp.logical_and(pad_end[0:1, :] <= blk, ln < N_EXPERTS)
        be = jnp.minimum(jnp.sum(done.astype(F32), axis=-1, keepdims=True), N_EXPERTS - 1.0)
        onehot = ln == be.astype(I32)
        cnt_e = jnp.sum(jnp.where(onehot, counts[0:1, :], 0.0), axis=-1, keepdims=True)
        start_e = jnp.sum(jnp.where(onehot, pad_start[0:1, :], 0.0), axis=-1, keepdims=True)
        valid = jnp.clip(cnt_e - (blk[:, 0:1] - start_e), 0.0, float(MOE_BLOCK))
        table_ref[...] = jnp.where(ln == 0, be.astype(I32),
                                   jnp.where(ln == 1, valid.astype(I32), 0))

    earlier = _tril_mask(sb, strict=True).astype(BF16)
    lane = lax.broadcasted_iota(I32, (sb, V7X_LANES), 1)
    for j in range(tope_ref.shape[0] // sb):
        tope = tope_ref[j * sb:(j + 1) * sb, :]
        hit = [lane == tope[:, k:k + 1] for k in range(TOP_K)]
        member = jnp.zeros((sb, V7X_LANES), F32)
        for k in range(TOP_K):
            member = member + hit[k].astype(F32)
        base = _dot(earlier, member.astype(BF16)) + (run_ref[0:1, :] + start_ref[0:1, :])
        dest = jnp.zeros((sb, V7X_LANES), I32)
        for k in range(TOP_K):
            dk = jnp.sum(jnp.where(hit[k], base, 0.0), axis=-1, keepdims=True)
            dest = jnp.where(lane == k, dk.astype(I32), dest)
        dest_ref[j * sb:(j + 1) * sb, :] = dest
        run_ref[...] = run_ref[...] + jnp.sum(member, axis=0, keepdims=True)


def _routing(tile_counts, tope, n_blocks):
    t = tope.shape[0]
    tr = ROUTE_ROWS
    return pl.pallas_call(
        _routing_kernel,
        grid=(t // tr,),
        in_specs=[pl.BlockSpec(tile_counts.shape, lambda i: (0, 0)),
                  pl.BlockSpec((tr, V7X_LANES), lambda i: (i, 0))],
        out_specs=(
            pl.BlockSpec((tr, V7X_LANES), lambda i: (i, 0)),
            pl.BlockSpec((n_blocks, V7X_LANES), lambda i: (0, 0)),
        ),
        out_shape=(
            jax.ShapeDtypeStruct((t, V7X_LANES), I32),
            jax.ShapeDtypeStruct((n_blocks, V7X_LANES), I32),
        ),
        scratch_shapes=[
            pltpu.VMEM((V7X_SUBLANES, V7X_LANES), F32),
            pltpu.VMEM((V7X_SUBLANES, V7X_LANES), F32),
        ],
        compiler_params=_params(("arbitrary",)),
        name="routing",
    )(tile_counts, tope)


def _sc_worker_id():
    return lax.axis_index("s") * V7X_SC_CORES + lax.axis_index("c")


def _sc_mesh():
    return plsc.VectorSubcoreMesh(core_axis_name="c", subcore_axis_name="s",
                                  num_cores=V7X_SC_CORES, num_subcores=V7X_SC_SUBCORES)


def _sc_dispatch(dest_km, h1p, n_rows):
    t, w = h1p.shape
    per_worker = t // V7X_SC_WORKERS
    ch = SC_ROWS_PER_DMA

    @functools.partial(
        pl.kernel, mesh=_sc_mesh(),
        out_type=jax.ShapeDtypeStruct((n_rows, w), h1p.dtype),
        scratch_types=[pltpu.VMEM((ch,), I32), pltpu.VMEM((ch, w), h1p.dtype), pltpu.SemaphoreType.DMA],
        name="sc_dispatch",
    )
    def scatter_rows(dest_hbm, h1p_hbm, xs_hbm, idx_v, rows_v, sem):
        first = _sc_worker_id() * per_worker

        @pl.loop(0, per_worker // ch)
        def _(j):
            base = first + j * ch
            pltpu.sync_copy(h1p_hbm.at[pl.ds(base, ch)], rows_v)
            for k in range(TOP_K):
                pltpu.sync_copy(dest_hbm.at[pl.ds(k * t + base, ch)], idx_v)
                pltpu.async_copy(rows_v, xs_hbm.at[idx_v], sem).wait()

    return scatter_rows(dest_km, h1p)


def _sc_gather(dest_km, y_rows):
    n = dest_km.shape[0]
    w = y_rows.shape[1]
    per_worker = n // V7X_SC_WORKERS
    ch = SC_ROWS_PER_DMA

    @functools.partial(
        pl.kernel, mesh=_sc_mesh(),
        out_type=jax.ShapeDtypeStruct((n, w), y_rows.dtype),
        scratch_types=[pltpu.VMEM((ch,), I32), pltpu.VMEM((ch, w), y_rows.dtype), pltpu.SemaphoreType.DMA],
        name="sc_gather",
    )
    def gather_rows(dest_hbm, y_hbm, out_hbm, idx_v, rows_v, sem):
        first = _sc_worker_id() * per_worker

        @pl.loop(0, per_worker // ch)
        def _(j):
            base = first + j * ch
            pltpu.sync_copy(dest_hbm.at[pl.ds(base, ch)], idx_v)
            pltpu.async_copy(y_hbm.at[idx_v], rows_v, sem).wait()
            pltpu.sync_copy(rows_v, out_hbm.at[pl.ds(base, ch)])

    return gather_rows(dest_km, y_rows)


def _expert_kernel(be_ref, nv_ref, xs_ref, wgu_ref, bgu_ref, wdn_ref, bdn_ref, y_ref):
    i = pl.program_id(0)
    nv = nv_ref[i]

    @pl.when(nv == 0)
    def _():
        y_ref[...] = jnp.zeros(y_ref.shape, U32)

    @pl.when(nv > 0)
    def _():
        x = _unpack_bf16_pairs(xs_ref[...])
        rowid = lax.broadcasted_iota(I32, x.shape, 0)
        x = jnp.where(rowid < nv, x, 0.0).astype(BF16)
        gu = _dot(x, wgu_ref[...]) + bgu_ref[...]
        f = gu.shape[1] // 2
        glu = jnp.minimum(gu[:, :f], SWIGLU_LIMIT)
        lin = jnp.clip(gu[:, f:], -SWIGLU_LIMIT, SWIGLU_LIMIT)
        act = glu * _sigmoid(SWIGLU_ALPHA * glu) * (lin + 1.0)
        y_ref[...] = _pack_bf16_pairs(_dot(act.astype(BF16), wdn_ref[...]) + bdn_ref[...])


def _experts(block_e, block_valid, xs, wgu, bgu, wdn, bdn):
    n_rows, w = xs.shape
    e, d, f2 = wgu.shape
    n_blocks = n_rows // MOE_BLOCK
    grid_spec = pltpu.PrefetchScalarGridSpec(
        num_scalar_prefetch=2,
        grid=(n_blocks,),
        in_specs=[
            pl.BlockSpec((MOE_BLOCK, w), lambda i, be, nv: (i, 0)),
            pl.BlockSpec((None, d, f2), lambda i, be, nv: (be[i], 0, 0)),
            pl.BlockSpec((None, 1, f2), lambda i, be, nv: (be[i], 0, 0)),
            pl.BlockSpec((None, f2 // 2, d), lambda i, be, nv: (be[i], 0, 0)),
            pl.BlockSpec((None, 1, d), lambda i, be, nv: (be[i], 0, 0)),
        ],
        out_specs=pl.BlockSpec((MOE_BLOCK, d // 2), lambda i, be, nv: (i, 0)),
    )
    return pl.pallas_call(
        _expert_kernel,
        grid_spec=grid_spec,
        out_shape=jax.ShapeDtypeStruct((n_rows, d // 2), U32),
        compiler_params=_params(("arbitrary",)),
        name="experts",
    )(block_e, block_valid, xs, wgu, bgu, wdn, bdn)


def _combine_kernel(dn_alpha, h1_ref, gate_ref, g_ref, b_ref, yg_ref, o_ref):
    gate = gate_ref[...]
    ffn = gate[:, 0:1] * _unpack_bf16_pairs(yg_ref[0])
    for k in range(1, TOP_K):
        ffn = ffn + gate[:, k:k + 1] * _unpack_bf16_pairs(yg_ref[k])
    o_ref[...] = _layer_norm_rows(dn_alpha * h1_ref[...] + ffn, g_ref[...], b_ref[...])


def _combine(dn_alpha, h1, gate, ln_g, ln_b, yg):
    t, d = h1.shape
    tc = COMBINE_ROWS
    row = lambda i: (i, 0)
    const = lambda i: (0, 0)
    return pl.pallas_call(
        functools.partial(_combine_kernel, dn_alpha),
        grid=(t // tc,),
        in_specs=[
            pl.BlockSpec((tc, d), row),
            pl.BlockSpec((tc, V7X_LANES), row),
            pl.BlockSpec((1, d), const),
            pl.BlockSpec((1, d), const),
            pl.BlockSpec((TOP_K, tc, d // 2), lambda i: (0, i, 0)),
        ],
        out_specs=pl.BlockSpec((tc, d), row),
        out_shape=jax.ShapeDtypeStruct((t, d), F32),
        compiler_params=_params(("parallel",)),
        name="combine_ln2",
    )(h1, gate, ln_g, ln_b, yg)


def _pack_in_proj(w_in, b_in):
    d = w_in.shape[0]
    o = 0
    cols = {}
    for name, width in (("mqk", MQK_W), ("mv", M_W), ("mo", M_W), ("mi", M_HEADS), ("mf", M_HEADS),
                        ("fq", F_W), ("fk", F_W), ("fv", F_W), ("ff", F_HEADS), ("gm", d), ("gf", d)):
        cols[name] = (o, o + width)
        o += width

    def take(a, names):
        return [a[..., cols[n][0]:cols[n][1]] for n in names]

    n_gate = 2 * M_HEADS + F_HEADS
    main = ("mqk", "mv", "mo", "fq", "fk", "fv", "mi", "mf", "ff")
    w_main = jnp.concatenate(take(w_in, main) + [jnp.zeros((d, GATE_COLS - n_gate), w_in.dtype)], axis=1)
    b_main = jnp.concatenate(take(b_in, main) + [jnp.zeros((GATE_COLS - n_gate,), b_in.dtype)])
    w_gate = jnp.concatenate(take(w_in, ("gm", "gf")), axis=1)
    b_gate = jnp.concatenate(take(b_in, ("gm", "gf")))
    return w_main.astype(BF16), b_main[None, :], w_gate.astype(BF16), b_gate[None, :]


def _layer(h, depth, w_in, b_in, m_conv_w, m_conv_b, m_norm_g, w_bm, w_bf, w_o, ln1_g, ln1_b,
           w_router, b_router, w_gu, b_gu, w_dn, b_dn, ln2_g, ln2_b):
    bsz, s, d = h.shape
    t = bsz * s
    dn_alpha = (2.0 * depth) ** 0.25
    x2 = h.reshape(t, d)

    w_main, b_main, w_gate, b_gate = _pack_in_proj(w_in, b_in)
    mqk, mv, mo, fq, fk, fv, gates = _inproj(x2, w_main, b_main)
    ccol, crow = _fox_cumsum(gates, bsz, s)
    hm = _mlstm(mqk, mv, mo, gates, m_conv_w, m_conv_b[None, :], m_norm_g[None, :], bsz, s)
    hf = _fox_attention(fq, fk, fv, ccol, crow, bsz, s)

    n_exp = w_router.shape[1]
    wr = jnp.zeros((d, V7X_LANES), F32).at[:, :n_exp].set(w_router)
    wr_hi = wr.astype(BF16)
    wr_lo = (wr - wr_hi.astype(F32)).astype(BF16)
    br = jnp.zeros((1, V7X_LANES), F32).at[0, :n_exp].set(b_router)
    h1, h1p, gate, tope, tile_counts = _merge(
        dn_alpha, hm, hf, x2, w_gate, b_gate, w_bm.astype(BF16), w_bf.astype(BF16), w_o.astype(BF16),
        ln1_g[None, :], ln1_b[None, :], wr_hi, wr_lo, br)

    n_blocks = -(-(t * TOP_K) // MOE_BLOCK) + N_EXPERTS
    dest, table = _routing(tile_counts, tope, n_blocks)
    dest_km = dest[:, :TOP_K].T.reshape(TOP_K * t)
    block_e, block_valid = table[:, 0], table[:, 1]
    xs = _sc_dispatch(dest_km, h1p, n_blocks * MOE_BLOCK)
    y_rows = _experts(block_e, block_valid, xs, w_gu.astype(BF16), b_gu[:, None, :],
                      w_dn.astype(BF16), b_dn[:, None, :])
    yg = _sc_gather(dest_km, y_rows).reshape(TOP_K, t, d // 2)
    out = _combine(dn_alpha, h1, gate, ln2_g[None, :], ln2_b[None, :], yg)
    return out.reshape(bsz, s, d)


def kernel(x, w_in, b_in, m_conv_w, m_conv_b, m_norm_g, w_bm, w_bf, w_o, ln1_g, ln1_b,
           w_router, b_router, w_gu, b_gu, w_dn, b_dn, ln2_g, ln2_b):
    depth = w_in.shape[0]
    h = x
    for l in range(depth):
        h = _layer(h, depth, w_in[l], b_in[l], m_conv_w[l], m_conv_b[l], m_norm_g[l], w_bm[l], w_bf[l],
                   w_o[l], ln1_g[l], ln1_b[l], w_router[l], b_router[l], w_gu[l], b_gu[l], w_dn[l],
                   b_dn[l], ln2_g[l], ln2_b[l])
    return h
```

```python
import functools
import math

import jax
import jax.numpy as jnp
from jax import lax
from jax.experimental import pallas as pl
from jax.experimental.pallas import tpu as pltpu
from jax.experimental.pallas import tpu_sc as plsc

F32 = jnp.float32
BF16 = jnp.bfloat16
I32 = jnp.int32
U32 = jnp.uint32

M_HEADS = 4
M_DQK = 128
M_DV = 128
CONV_W = 4
F_HEADS = 8
F_DH = 64
N_EXPERTS = 32
TOP_K = 4
SWIGLU_ALPHA = 1.702
SWIGLU_LIMIT = 7.0
LN_EPS = 1e-5

M_W = M_HEADS * M_DV
F_W = F_HEADS * F_DH
MQK_W = 2 * M_HEADS * M_DQK

V7X_LANES = 128
V7X_SUBLANES = 8
V7X_VMEM_BYTES = 64 * 1024 * 1024
VMEM_LIMIT_BYTES = (V7X_VMEM_BYTES * 3) // 4
V7X_SC_CORES = 2
V7X_SC_SUBCORES = 16
V7X_SC_WORKERS = V7X_SC_CORES * V7X_SC_SUBCORES

INPROJ_ROWS = 512
MLSTM_CHUNK = 256
FOX_Q_BLOCK = 512
FOX_K_BLOCK = 256
CUMSUM_BLOCK = 256
MERGE_ROWS = 256
ROUTE_ROWS = 1024
ROUTE_SUB_ROWS = 256
MOE_BLOCK = 256
SC_ROWS_PER_DMA = 128
COMBINE_ROWS = 512

GATE_COLS = V7X_LANES
MI_LANE = 0
MF_LANE = M_HEADS
FF_LANE = 2 * M_HEADS


def _params(semantics):
    return pltpu.CompilerParams(dimension_semantics=semantics, vmem_limit_bytes=VMEM_LIMIT_BYTES)


def _log_sigmoid(x):
    return jnp.minimum(x, 0.0) - jnp.log1p(jnp.exp(-jnp.abs(x)))


def _sigmoid(x):
    return 1.0 / (1.0 + jnp.exp(-x))


def _dot(a, b):
    return jnp.dot(a, b, preferred_element_type=F32)


def _dot_nt(a, b):
    return lax.dot_general(a, b, (((1,), (1,)), ((), ())), preferred_element_type=F32)


def _dot_tn(a, b):
    return lax.dot_general(a, b, (((0,), (0,)), ((), ())), preferred_element_type=F32)


def _split3(x):
    hi = x.astype(BF16)
    r1 = x - hi.astype(F32)
    mid = r1.astype(BF16)
    lo = (r1 - mid.astype(F32)).astype(BF16)
    return hi, mid, lo


def _dot_mask_f32(mask_bf16, x):
    hi, mid, lo = _split3(x)
    return (_dot(mask_bf16, lo) + _dot(mask_bf16, mid)) + _dot(mask_bf16, hi)


def _pack_bf16_pairs(x):
    half = x.shape[1] // 2
    bits = lax.bitcast_convert_type(x.astype(BF16).astype(F32), U32)
    return (bits[:, :half] >> 16) | bits[:, half:]


def _unpack_bf16_pairs(words):
    lo = lax.bitcast_convert_type(words << 16, F32)
    hi = lax.bitcast_convert_type(words & jnp.uint32(0xFFFF0000), F32)
    return jnp.concatenate([lo, hi], axis=1)


def _tril_mask(n, strict=False):
    r = lax.broadcasted_iota(I32, (n, n), 0)
    c = lax.broadcasted_iota(I32, (n, n), 1)
    return (r > c) if strict else (r >= c)


_OFF_MQK = 0
_OFF_MV = _OFF_MQK + MQK_W
_OFF_MO = _OFF_MV + M_W
_OFF_FQ = _OFF_MO + M_W
_OFF_FK = _OFF_FQ + F_W
_OFF_FV = _OFF_FK + F_W
_OFF_GATES = _OFF_FV + F_W
_PACKED_COLS = _OFF_GATES + GATE_COLS


def _inproj_kernel(x_ref, w_ref, b_ref, mqk_ref, mv_ref, mo_ref, fq_ref, fk_ref, fv_ref, gates_ref):
    xb = x_ref[...].astype(BF16)

    def seg(lo, width):
        return _dot(xb, w_ref[:, lo:lo + width]) + b_ref[:, lo:lo + width]

    mqk_ref[...] = seg(_OFF_MQK, MQK_W)
    mv_ref[...] = seg(_OFF_MV, M_W).astype(BF16)
    mo_ref[...] = seg(_OFF_MO, M_W)
    fq_ref[...] = (seg(_OFF_FQ, F_W) * (F_DH ** -0.5)).astype(BF16)
    fk_ref[...] = seg(_OFF_FK, F_W).astype(BF16)
    fv_ref[...] = seg(_OFF_FV, F_W).astype(BF16)
    gates_ref[...] = seg(_OFF_GATES, GATE_COLS)


def _inproj(x2, w_packed, b_packed):
    t, d = x2.shape
    tm = INPROJ_ROWS
    row = lambda i: (i, 0)
    const = lambda i: (0, 0)
    out_shapes = (
        jax.ShapeDtypeStruct((t, MQK_W), F32),
        jax.ShapeDtypeStruct((t, M_W), BF16),
        jax.ShapeDtypeStruct((t, M_W), F32),
        jax.ShapeDtypeStruct((t, F_W), BF16),
        jax.ShapeDtypeStruct((t, F_W), BF16),
        jax.ShapeDtypeStruct((t, F_W), BF16),
        jax.ShapeDtypeStruct((t, GATE_COLS), F32),
    )
    return pl.pallas_call(
        _inproj_kernel,
        grid=(t // tm,),
        in_specs=[
            pl.BlockSpec((tm, d), row),
            pl.BlockSpec((d, _PACKED_COLS), const),
            pl.BlockSpec((1, _PACKED_COLS), const),
        ],
        out_specs=tuple(pl.BlockSpec((tm, s.shape[1]), row) for s in out_shapes),
        out_shape=out_shapes,
        compiler_params=_params(("parallel",)),
        name="inproj",
    )(x2, w_packed, b_packed)


def _fox_cumsum_kernel(g_ref, ccol_ref):
    s = g_ref.shape[0]
    cb = CUMSUM_BLOCK
    tri = _tril_mask(cb).astype(BF16)
    carry = jnp.zeros((1, GATE_COLS), F32)
    for j in range(s // cb):
        lsg = _log_sigmoid(g_ref[j * cb:(j + 1) * cb, :])
        cs = _dot_mask_f32(tri, lsg) + carry
        ccol_ref[j * cb:(j + 1) * cb, :] = cs
        carry = cs[cb - 1:cb, :]


def _fox_cumsum(gates, bsz, s):
    t = gates.shape[0]
    return pl.pallas_call(
        _fox_cumsum_kernel,
        grid=(bsz,),
        in_specs=[pl.BlockSpec((s, GATE_COLS), lambda b: (b, 0))],
        out_specs=pl.BlockSpec((s, GATE_COLS), lambda b: (b, 0)),
        out_shape=jax.ShapeDtypeStruct((t, GATE_COLS), F32),
        compiler_params=_params(("parallel",)),
        name="fox_cumsum",
    )(gates)


def _mlstm_kernel(mqk_ref, mv_ref, mo_ref, gates_ref, cw_ref, cb_ref, ng_ref, hm_ref,
                  ext_ref, state_ref, m_ref):
    L = MLSTM_CHUNK
    pad = V7X_SUBLANES
    n = pl.program_id(1)

    @pl.when(n == 0)
    def _():
        ext_ref[0:pad, :] = jnp.zeros((pad, MQK_W), F32)
        state_ref[...] = jnp.zeros(state_ref.shape, F32)
        m_ref[...] = jnp.zeros(m_ref.shape, F32)

    ext_ref[pad:pad + L, :] = mqk_ref[...]
    y = cb_ref[...] + cw_ref[CONV_W - 1:CONV_W, :] * ext_ref[pad:pad + L, :]
    for k in range(CONV_W - 1):
        shift = CONV_W - 1 - k
        y = y + cw_ref[k:k + 1, :] * ext_ref[pad - shift:pad - shift + L, :]
    ext_ref[0:pad, :] = ext_ref[L:L + pad, :]
    qk = y * _sigmoid(y)

    gates = gates_ref[...]
    lsg = _log_sigmoid(gates)
    tri = _tril_mask(L)
    bfull = _dot_mask_f32(tri.astype(BF16), lsg)
    gates_t = gates.T
    bfull_t = bfull.T
    lane = lax.broadcasted_iota(I32, (L, M_DV), 1)
    ones_col = (lane == 0).astype(BF16)

    for h in range(M_HEADS):
        i_col = gates[:, MI_LANE + h:MI_LANE + h + 1]
        b_col = bfull[:, MF_LANE + h:MF_LANE + h + 1]
        i_row = gates_t[MI_LANE + h:MI_LANE + h + 1, :]
        b_row = bfull_t[MF_LANE + h:MF_LANE + h + 1, :]
        g_tot = b_col[L - 1:L, :]
        m_prev = m_ref[h][0:1, 0:1]

        q_h = qk[:, h * M_DQK:(h + 1) * M_DQK].astype(BF16)
        k_f = qk[:, MQK_W // 2 + h * M_DQK:MQK_W // 2 + (h + 1) * M_DQK] * (M_DQK ** -0.5)
        k_h = k_f.astype(BF16)
        v_h = mv_ref[:, h * M_DV:(h + 1) * M_DV]
        cn = state_ref[h]

        dlog = jnp.where(tri, (b_col - b_row) + i_row, -jnp.inf)
        inter_log = b_col + m_prev
        m_t = jnp.maximum(inter_log, jnp.max(dlog, axis=-1, keepdims=True))
        w_inter = jnp.exp(inter_log - m_t)
        w_intra = jnp.exp(dlog - m_t)
        qkw = _dot_nt(q_h, k_h) * w_intra
        qc = _dot(q_h, cn.astype(BF16))
        num = w_inter * qc[:, :M_DV] + _dot(qkw.astype(BF16), v_h)
        den = w_inter * qc[:, M_DV:M_DV + 1] + jnp.sum(qkw, axis=-1, keepdims=True)
        hh = num / jnp.maximum(jnp.abs(den), jnp.exp(-m_t))

        mu = jnp.mean(hh, axis=-1, keepdims=True)
        dv = hh - mu
        var = jnp.mean(dv * dv, axis=-1, keepdims=True)
        hn = (dv * lax.rsqrt(var + LN_EPS)) * ng_ref[:, h * M_DV:(h + 1) * M_DV]
        hm_ref[:, h * M_DV:(h + 1) * M_DV] = (
            _sigmoid(mo_ref[:, h * M_DV:(h + 1) * M_DV]) * hn).astype(BF16)

        a_col = (g_tot - b_col) + i_col
        m_new = jnp.maximum(g_tot + m_prev, jnp.max(a_col, axis=0, keepdims=True))
        decay = jnp.exp(g_tot + m_prev - m_new)
        w_col = jnp.exp(a_col - m_new)
        kw = (k_f * w_col).astype(BF16)
        v_aug = jnp.concatenate([v_h, ones_col], axis=1)
        state_ref[h] = decay * cn + _dot_tn(kw, v_aug)
        m_ref[h] = jnp.broadcast_to(m_new, m_ref.shape[1:])


def _mlstm(mqk, mv, mo, gates, conv_w, conv_b, norm_g, bsz, s):
    t = mqk.shape[0]
    L = MLSTM_CHUNK
    nc = s // L
    row = lambda b, n: (b * nc + n, 0)
    const = lambda b, n: (0, 0)
    return pl.pallas_call(
        _mlstm_kernel,
        grid=(bsz, nc),
        in_specs=[
            pl.BlockSpec((L, MQK_W), row),
            pl.BlockSpec((L, M_W), row),
            pl.BlockSpec((L, M_W), row),
            pl.BlockSpec((L, GATE_COLS), row),
            pl.BlockSpec((CONV_W, MQK_W), const),
            pl.BlockSpec((1, MQK_W), const),
            pl.BlockSpec((1, M_W), const),
        ],
        out_specs=pl.BlockSpec((L, M_W), row),
        out_shape=jax.ShapeDtypeStruct((t, M_W), BF16),
        scratch_shapes=[
            pltpu.VMEM((L + V7X_SUBLANES, MQK_W), F32),
            pltpu.VMEM((M_HEADS, M_DQK, 2 * M_DV), F32),
            pltpu.VMEM((M_HEADS, V7X_SUBLANES, V7X_LANES), F32),
        ],
        compiler_params=_params(("parallel", "arbitrary")),
        name="mlstm",
    )(mqk, mv, mo, gates, conv_w, conv_b, norm_g)


def _fox_operand(x, c_col, hh, key_side):
    rows = x.shape[0]
    lane = lax.broadcasted_iota(I32, (rows, V7X_LANES), 1)
    own = (lane < F_DH) if hh == 0 else (lane >= F_DH)
    base = F_DH if hh == 0 else 0
    parts = [part.astype(F32) for part in _split3(c_col)]
    if key_side:
        feats = parts + [1.0, 1.0, 1.0]
    else:
        feats = [-1.0, -1.0, -1.0] + parts
    out = jnp.where(own, x, 0.0)
    for n, feat in enumerate(feats):
        out = jnp.where(lane == base + n, feat, out)
    return out.astype(BF16)


def _fox_kernel(q_ref, k_ref, v_ref, ccol_ref, o_ref, qaug_ref, kaug_ref, vt_ref,
                st_a0, st_a1, st_b0, st_b1, pe_a0, pe_a1, pe_b0, pe_b1, acc0, acc1):
    blk = FOX_K_BLOCK
    tq = FOX_Q_BLOCK
    assert tq == 2 * blk
    strips = tq // V7X_LANES
    st_a_refs, st_b_refs = (st_a0, st_a1), (st_b0, st_b1)
    pe_a_refs, pe_b_refs = (pe_a0, pe_a1), (pe_b0, pe_b1)
    acc_refs = (acc0, acc1)
    p = pl.program_id(1)
    i = pl.program_id(2)

    @pl.when(i == 0)
    def _():
        c_all = ccol_ref[...]
        lane = lax.broadcasted_iota(I32, c_all.shape, 1)
        q_all = q_ref[...].astype(F32)
        k_all = k_ref[...].astype(F32)
        for hh in range(2):
            c_h = jnp.sum(jnp.where(lane == FF_LANE + 2 * p + hh, c_all, 0.0), axis=-1, keepdims=True)
            qaug_ref[hh] = _fox_operand(q_all, c_h, hh, False)
            kaug_ref[hh] = _fox_operand(k_all, c_h, hh, True)
        v_t = v_ref[...].astype(F32).T
        for j in range(vt_ref.shape[0]):
            vt_ref[j] = v_t[:, j * blk:(j + 1) * blk].astype(BF16)

    q_start = pl.multiple_of(i * tq, tq)
    q_heads = [qaug_ref[hh, pl.ds(q_start, tq), :] for hh in range(2)]
    key_row = lax.broadcasted_iota(I32, (blk, V7X_LANES), 0)
    query_col = lax.broadcasted_iota(I32, (blk, V7X_LANES), 1)
    last_chunk = 2 * i + 1

    def scores(j, hh):
        start = pl.multiple_of(j * blk, blk)
        return _dot_nt(kaug_ref[hh, pl.ds(start, blk), :], q_heads[hh])

    def softmax_update(st_ref, pe_ref, m_old, l_old, key_minus_query=None):
        alphas, ms, ls = [], [], []
        for c in range(strips):
            cols = slice(c * V7X_LANES, (c + 1) * V7X_LANES)
            gap = None if key_minus_query is None else key_minus_query - c * V7X_LANES
            if gap is not None and gap - (V7X_LANES - 1) > 0:
                pe_ref[:, cols] = jnp.zeros((blk, V7X_LANES), BF16)
                alphas.append(jnp.ones((1, V7X_LANES), F32))
                ms.append(m_old[:, cols])
                ls.append(l_old[:, cols])
                continue
            st = st_ref[:, cols]
            if gap is not None and gap + (blk - 1) > 0:
                st = jnp.where(key_row + gap <= query_col, st, -jnp.inf)
            m_new = jnp.maximum(m_old[:, cols], jnp.max(st, axis=0, keepdims=True))
            alpha = jnp.exp(m_old[:, cols] - m_new)
            pe = jnp.exp(st - m_new)
            pe_ref[:, cols] = pe.astype(BF16)
            alphas.append(alpha)
            ms.append(m_new)
            ls.append(alpha * l_old[:, cols] + jnp.sum(pe, axis=0, keepdims=True))
        cat = lambda parts: jnp.concatenate(parts, axis=1)
        return cat(alphas), cat(ms), cat(ls)

    def pair(mi, carry, diagonal=False):
        a = 2 * mi
        b = a + 1
        v_prev = vt_ref[jnp.maximum(a - 1, 0)]
        v_a = vt_ref[a]
        partial = []
        for hh in range(2):
            alpha_prev = carry[hh][0]
            partial.append(alpha_prev * acc_refs[hh][...] + _dot(v_prev, pe_b_refs[hh][...]))
            st_b_refs[hh][...] = scores(b, hh)
        stats = []
        for hh in range(2):
            _, m_old, l_old = carry[hh]
            stats.append(softmax_update(st_a_refs[hh], pe_a_refs[hh], m_old, l_old,
                                        0 if diagonal else None))
        for hh in range(2):
            alpha_a = stats[hh][0]
            acc_refs[hh][...] = alpha_a * partial[hh] + _dot(v_a, pe_a_refs[hh][...])
            if not diagonal:
                st_a_refs[hh][...] = scores(a + 2, hh)
        return tuple(softmax_update(st_b_refs[hh], pe_b_refs[hh], stats[hh][1], stats[hh][2],
                                    blk if diagonal else None) for hh in range(2))

    for hh in range(2):
        st_a_refs[hh][...] = scores(0, hh)
        pe_b_refs[hh][...] = jnp.zeros((blk, tq), BF16)
        acc_refs[hh][...] = jnp.zeros((V7X_LANES, tq), F32)
    init = tuple((jnp.ones((1, tq), F32), jnp.full((1, tq), -jnp.inf, F32), jnp.zeros((1, tq), F32))
                 for _ in range(2))
    final = pair(i, lax.fori_loop(0, i, pair, init), diagonal=True)
    v_last = vt_ref[last_chunk]
    outs = []
    for hh in range(2):
        alpha, _, l_fin = final[hh]
        outs.append((alpha * acc_refs[hh][...] + _dot(v_last, pe_b_refs[hh][...])) / l_fin)
    row = lax.broadcasted_iota(I32, (V7X_LANES, tq), 0)
    o_t = jnp.where(row < F_DH, outs[0], outs[1])
    o_ref[...] = o_t.T.astype(BF16)


def _fox_attention(fq, fk, fv, ccol, bsz, s):
    t = fq.shape[0]
    blk = FOX_K_BLOCK
    tq = FOX_Q_BLOCK
    nq = s // tq
    pairs = F_HEADS // 2
    qmap = lambda b, p, i: (b * nq + i, p)
    kvmap = lambda b, p, i: (b, p)
    return pl.pallas_call(
        _fox_kernel,
        grid=(bsz, pairs, nq),
        in_specs=[
            pl.BlockSpec((s, V7X_LANES), kvmap),
            pl.BlockSpec((s, V7X_LANES), kvmap),
            pl.BlockSpec((s, V7X_LANES), kvmap),
            pl.BlockSpec((s, GATE_COLS), lambda b, p, i: (b, 0)),
        ],
        out_specs=pl.BlockSpec((tq, V7X_LANES), qmap),
        out_shape=jax.ShapeDtypeStruct((t, F_W), BF16),
        scratch_shapes=[
            pltpu.VMEM((2, s, V7X_LANES), BF16),
            pltpu.VMEM((2, s, V7X_LANES), BF16),
            pltpu.VMEM((s // blk, V7X_LANES, blk), BF16),
        ] + [pltpu.VMEM((blk, tq), F32)] * 4 + [pltpu.VMEM((blk, tq), BF16)] * 4
          + [pltpu.VMEM((V7X_LANES, tq), F32)] * 2,
        compiler_params=_params(("parallel", "parallel", "arbitrary")),
        name="fox_attention",
    )(fq, fk, fv, ccol)


def _layer_norm_rows(r, g, b):
    mu = jnp.mean(r, axis=-1, keepdims=True)
    d = r - mu
    var = jnp.mean(d * d, axis=-1, keepdims=True)
    return (d * lax.rsqrt(var + LN_EPS)) * g + b


def _merge_kernel(dn_alpha, hm_ref, hf_ref, x_ref, wg_ref, bg_ref, wbm_ref, wbf_ref, wo_ref,
                  g_ref, b_ref, wrh_ref, wrl_ref, br_ref, h1_ref, h1p_ref, gate_ref, tope_ref, cnt_ref):
    x = x_ref[...]
    d = x.shape[1]
    gmf = _dot(x.astype(BF16), wg_ref[...]) + bg_ref[...]
    ym = _dot(hm_ref[...], wbm_ref[...])
    yf = _dot(hf_ref[...], wbf_ref[...])
    y = _sigmoid(gmf[:, :d]) * ym + _sigmoid(gmf[:, d:]) * yf
    mix = _dot(y.astype(BF16), wo_ref[...])
    h1 = _layer_norm_rows(dn_alpha * x + mix, g_ref[...], b_ref[...])
    h1_ref[...] = h1

    h1p_ref[...] = _pack_bf16_pairs(h1)
    hb = h1.astype(BF16)

    lo = (h1 - hb.astype(F32)).astype(BF16)
    logits = (_dot(lo, wrh_ref[...]) + _dot(hb, wrl_ref[...])) + _dot(hb, wrh_ref[...]) + br_ref[...]
    tm = logits.shape[0]
    lane = lax.broadcasted_iota(I32, (tm, V7X_LANES), 1)
    vals = jnp.where(lane < N_EXPERTS, logits, -jnp.inf)
    top_v, top_i = [], []
    for _ in range(TOP_K):
        mx = jnp.max(vals, axis=-1, keepdims=True)
        idx = jnp.min(jnp.where(vals == mx, lane, V7X_LANES), axis=-1, keepdims=True)
        top_v.append(mx)
        top_i.append(idx)
        vals = jnp.where(lane == idx, -jnp.inf, vals)
    ex = [jnp.exp(v - top_v[0]) for v in top_v]
    tot = ex[0]
    for e in ex[1:]:
        tot = tot + e
    gate = jnp.zeros((tm, V7X_LANES), F32)
    tope = jnp.zeros((tm, V7X_LANES), I32)
    member = jnp.zeros((tm, V7X_LANES), F32)
    for k in range(TOP_K):
        gate = jnp.where(lane == k, ex[k] / tot, gate)
        tope = jnp.where(lane == k, top_i[k], tope)
        member = member + (lane == top_i[k]).astype(F32)
    gate_ref[...] = gate
    tope_ref[...] = tope
    sub = lax.broadcasted_iota(I32, cnt_ref.shape, 0)
    cnt_ref[...] = jnp.where(sub == 0, jnp.sum(member, axis=0, keepdims=True), 0.0)


def _merge(dn_alpha, hm, hf, x2, wg, bg, wbm, wbf, wo, ln_g, ln_b, wr_hi, wr_lo, br):
    t, d = x2.shape
    tm = MERGE_ROWS
    row = lambda i: (i, 0)
    const = lambda i: (0, 0)
    full = lambda a: pl.BlockSpec(a.shape, const)
    return pl.pallas_call(
        functools.partial(_merge_kernel, dn_alpha),
        grid=(t // tm,),
        in_specs=[
            pl.BlockSpec((tm, M_W), row),
            pl.BlockSpec((tm, F_W), row),
            pl.BlockSpec((tm, d), row),
            full(wg), full(bg), full(wbm), full(wbf), full(wo), full(ln_g), full(ln_b),
            full(wr_hi), full(wr_lo), full(br),
        ],
        out_specs=(
            pl.BlockSpec((tm, d), row),
            pl.BlockSpec((tm, d // 2), row),
            pl.BlockSpec((tm, V7X_LANES), row),
            pl.BlockSpec((tm, V7X_LANES), row),
            pl.BlockSpec((V7X_SUBLANES, V7X_LANES), row),
        ),
        out_shape=(
            jax.ShapeDtypeStruct((t, d), F32),
            jax.ShapeDtypeStruct((t, d // 2), U32),
            jax.ShapeDtypeStruct((t, V7X_LANES), F32),
            jax.ShapeDtypeStruct((t, V7X_LANES), I32),
            jax.ShapeDtypeStruct((t // tm * V7X_SUBLANES, V7X_LANES), F32),
        ),
        compiler_params=_params(("parallel",)),
        name="merge_ln1_router",
    )(hm, hf, x2, wg, bg, wbm, wbf, wo, ln_g, ln_b, wr_hi, wr_lo, br)


def _lane_cumsum(x):
    lane = lax.broadcasted_iota(I32, x.shape, 1)
    d = 1
    while d < V7X_LANES:
        x = x + jnp.where(lane >= d, pltpu.roll(x, shift=d, axis=1), 0.0)
        d *= 2
    return x


def _routing_kernel(cnt_ref, tope_ref, dest_ref, table_ref, run_ref, start_ref):
    sb = ROUTE_SUB_ROWS

    @pl.when(pl.program_id(0) == 0)
    def _():
        total = jnp.sum(cnt_ref[...], axis=0, keepdims=True)
        counts = jnp.broadcast_to(total, (V7X_SUBLANES, V7X_LANES))
        padded = jnp.ceil(counts * (1.0 / MOE_BLOCK)) * MOE_BLOCK
        pad_end = _lane_cumsum(padded)
        pad_start = pad_end - padded
        start_ref[...] = pad_start
        run_ref[...] = jnp.zeros(run_ref.shape, F32)
        nb = table_ref.shape[0]
        blk = lax.broadcasted_iota(I32, (nb, V7X_LANES), 0).astype(F32) * MOE_BLOCK
        ln = lax.broadcasted_iota(I32, (nb, V7X_LANES), 1)
        done = jnp.logical_and(pad_end[0:1, :] <= blk, ln < N_EXPERTS)
        be = jnp.minimum(jnp.sum(done.astype(F32), axis=-1, keepdims=True), N_EXPERTS - 1.0)
        onehot = ln == be.astype(I32)
        cnt_e = jnp.sum(jnp.where(onehot, counts[0:1, :], 0.0), axis=-1, keepdims=True)
        start_e = jnp.sum(jnp.where(onehot, pad_start[0:1, :], 0.0), axis=-1, keepdims=True)
        valid = jnp.clip(cnt_e - (blk[:, 0:1] - start_e), 0.0, float(MOE_BLOCK))
        table_ref[...] = jnp.where(ln == 0, be.astype(I32),
                                   jnp.where(ln == 1, valid.astype(I32), 0))

    earlier = _tril_mask(sb, strict=True).astype(BF16)
    lane = lax.broadcasted_iota(I32, (sb, V7X_LANES), 1)
    for j in range(tope_ref.shape[0] // sb):
        tope = tope_ref[j * sb:(j + 1) * sb, :]
        hit = [lane == tope[:, k:k + 1] for k in range(TOP_K)]
        member = jnp.zeros((sb, V7X_LANES), F32)
        for k in range(TOP_K):
            member = member + hit[k].astype(F32)
        base = _dot(earlier, member.astype(BF16)) + (run_ref[0:1, :] + start_ref[0:1, :])
        dest = jnp.zeros((sb, V7X_LANES), I32)
        for k in range(TOP_K):
            dk = jnp.sum(jnp.where(hit[k], base, 0.0), axis=-1, keepdims=True)
            dest = jnp.where(lane == k, dk.astype(I32), dest)
        dest_ref[j * sb:(j + 1) * sb, :] = dest
        run_ref[...] = run_ref[...] + jnp.sum(member, axis=0, keepdims=True)


def _routing(tile_counts, tope, n_blocks):
    t = tope.shape[0]
    tr = ROUTE_ROWS
    return pl.pallas_call(
        _routing_kernel,
        grid=(t // tr,),
        in_specs=[pl.BlockSpec(tile_counts.shape, lambda i: (0, 0)),
                  pl.BlockSpec((tr, V7X_LANES), lambda i: (i, 0))],
        out_specs=(
            pl.BlockSpec((tr, V7X_LANES), lambda i: (i, 0)),
            pl.BlockSpec((n_blocks, V7X_LANES), lambda i: (0, 0)),
        ),
        out_shape=(
            jax.ShapeDtypeStruct((t, V7X_LANES), I32),
            jax.ShapeDtypeStruct((n_blocks, V7X_LANES), I32),
        ),
        scratch_shapes=[
            pltpu.VMEM((V7X_SUBLANES, V7X_LANES), F32),
            pltpu.VMEM((V7X_SUBLANES, V7X_LANES), F32),
        ],
        compiler_params=_params(("arbitrary",)),
        name="routing",
    )(tile_counts, tope)


def _sc_worker_id():
    return lax.axis_index("s") * V7X_SC_CORES + lax.axis_index("c")


def _sc_mesh():
    return plsc.VectorSubcoreMesh(core_axis_name="c", subcore_axis_name="s",
                                  num_cores=V7X_SC_CORES, num_subcores=V7X_SC_SUBCORES)


def _sc_dispatch(dest_km, h1p, n_rows):
    t, w = h1p.shape
    per_worker = t // V7X_SC_WORKERS
    ch = SC_ROWS_PER_DMA

    @functools.partial(
        pl.kernel, mesh=_sc_mesh(),
        out_type=jax.ShapeDtypeStruct((n_rows, w), h1p.dtype),
        scratch_types=[pltpu.VMEM((ch,), I32), pltpu.VMEM((ch, w), h1p.dtype), pltpu.SemaphoreType.DMA],
        name="sc_dispatch",
    )
    def scatter_rows(dest_hbm, h1p_hbm, xs_hbm, idx_v, rows_v, sem):
        first = _sc_worker_id() * per_worker

        @pl.loop(0, per_worker // ch)
        def _(j):
            base = first + j * ch
            pltpu.sync_copy(h1p_hbm.at[pl.ds(base, ch)], rows_v)
            for k in range(TOP_K):
                pltpu.sync_copy(dest_hbm.at[pl.ds(k * t + base, ch)], idx_v)
                pltpu.async_copy(rows_v, xs_hbm.at[idx_v], sem).wait()

    return scatter_rows(dest_km, h1p)


def _sc_gather(dest_km, y_rows):
    n = dest_km.shape[0]
    w = y_rows.shape[1]
    per_worker = n // V7X_SC_WORKERS
    ch = SC_ROWS_PER_DMA

    @functools.partial(
        pl.kernel, mesh=_sc_mesh(),
        out_type=jax.ShapeDtypeStruct((n, w), y_rows.dtype),
        scratch_types=[pltpu.VMEM((ch,), I32), pltpu.VMEM((ch, w), y_rows.dtype), pltpu.SemaphoreType.DMA],
        name="sc_gather",
    )
    def gather_rows(dest_hbm, y_hbm, out_hbm, idx_v, rows_v, sem):
        first = _sc_worker_id() * per_worker

        @pl.loop(0, per_worker // ch)
        def _(j):
            base = first + j * ch
            pltpu.sync_copy(dest_hbm.at[pl.ds(base, ch)], idx_v)
            pltpu.async_copy(y_hbm.at[idx_v], rows_v, sem).wait()
            pltpu.sync_copy(rows_v, out_hbm.at[pl.ds(base, ch)])

    return gather_rows(dest_km, y_rows)


def _expert_kernel(be_ref, nv_ref, xs_ref, wgu_ref, bgu_ref, wdn_ref, bdn_ref, y_ref):
    i = pl.program_id(0)
    nv = nv_ref[i]

    @pl.when(nv == 0)
    def _():
        y_ref[...] = jnp.zeros(y_ref.shape, U32)

    @pl.when(nv > 0)
    def _():
        x = _unpack_bf16_pairs(xs_ref[...])
        rowid = lax.broadcasted_iota(I32, x.shape, 0)
        x = jnp.where(rowid < nv, x, 0.0).astype(BF16)
        gu = _dot(x, wgu_ref[...]) + bgu_ref[...]
        f = gu.shape[1] // 2
        glu = jnp.minimum(gu[:, :f], SWIGLU_LIMIT)
        lin = jnp.clip(gu[:, f:], -SWIGLU_LIMIT, SWIGLU_LIMIT)
        act = glu * _sigmoid(SWIGLU_ALPHA * glu) * (lin + 1.0)
        y_ref[...] = _pack_bf16_pairs(_dot(act.astype(BF16), wdn_ref[...]) + bdn_ref[...])


def _experts(block_e, block_valid, xs, wgu, bgu, wdn, bdn):
    n_rows, w = xs.shape
    e, d, f2 = wgu.shape
    n_blocks = n_rows // MOE_BLOCK
    grid_spec = pltpu.PrefetchScalarGridSpec(
        num_scalar_prefetch=2,
        grid=(n_blocks,),
        in_specs=[
            pl.BlockSpec((MOE_BLOCK, w), lambda i, be, nv: (i, 0)),
            pl.BlockSpec((None, d, f2), lambda i, be, nv: (be[i], 0, 0)),
            pl.BlockSpec((None, 1, f2), lambda i, be, nv: (be[i], 0, 0)),
            pl.BlockSpec((None, f2 // 2, d), lambda i, be, nv: (be[i], 0, 0)),
            pl.BlockSpec((None, 1, d), lambda i, be, nv: (be[i], 0, 0)),
        ],
        out_specs=pl.BlockSpec((MOE_BLOCK, d // 2), lambda i, be, nv: (i, 0)),
    )
    return pl.pallas_call(
        _expert_kernel,
        grid_spec=grid_spec,
        out_shape=jax.ShapeDtypeStruct((n_rows, d // 2), U32),
        compiler_params=_params(("arbitrary",)),
        name="experts",
    )(block_e, block_valid, xs, wgu, bgu, wdn, bdn)


def _combine_kernel(dn_alpha, h1_ref, gate_ref, g_ref, b_ref, yg_ref, o_ref):
    gate = gate_ref[...]
    ffn = gate[:, 0:1] * _unpack_bf16_pairs(yg_ref[0])
    for k in range(1, TOP_K):
        ffn = ffn + gate[:, k:k + 1] * _unpack_bf16_pairs(yg_ref[k])
    o_ref[...] = _layer_norm_rows(dn_alpha * h1_ref[...] + ffn, g_ref[...], b_ref[...])


def _combine(dn_alpha, h1, gate, ln_g, ln_b, yg):
    t, d = h1.shape
    tc = COMBINE_ROWS
    row = lambda i: (i, 0)
    const = lambda i: (0, 0)
    return pl.pallas_call(
        functools.partial(_combine_kernel, dn_alpha),
        grid=(t // tc,),
        in_specs=[
            pl.BlockSpec((tc, d), row),
            pl.BlockSpec((tc, V7X_LANES), row),
            pl.BlockSpec((1, d), const),
            pl.BlockSpec((1, d), const),
            pl.BlockSpec((TOP_K, tc, d // 2), lambda i: (0, i, 0)),
        ],
        out_specs=pl.BlockSpec((tc, d), row),
        out_shape=jax.ShapeDtypeStruct((t, d), F32),
        compiler_params=_params(("parallel",)),
        name="combine_ln2",
    )(h1, gate, ln_g, ln_b, yg)


def _pack_in_proj(w_in, b_in):
    d = w_in.shape[0]
    o = 0
    cols = {}
    for name, width in (("mqk", MQK_W), ("mv", M_W), ("mo", M_W), ("mi", M_HEADS), ("mf", M_HEADS),
                        ("fq", F_W), ("fk", F_W), ("fv", F_W), ("ff", F_HEADS), ("gm", d), ("gf", d)):
        cols[name] = (o, o + width)
        o += width

    def take(a, names):
        return [a[..., cols[n][0]:cols[n][1]] for n in names]

    n_gate = 2 * M_HEADS + F_HEADS
    main = ("mqk", "mv", "mo", "fq", "fk", "fv", "mi", "mf", "ff")
    w_main = jnp.concatenate(take(w_in, main) + [jnp.zeros((d, GATE_COLS - n_gate), w_in.dtype)], axis=1)
    b_main = jnp.concatenate(take(b_in, main) + [jnp.zeros((GATE_COLS - n_gate,), b_in.dtype)])
    w_gate = jnp.concatenate(take(w_in, ("gm", "gf")), axis=1)
    b_gate = jnp.concatenate(take(b_in, ("gm", "gf")))
    return w_main.astype(BF16), b_main[None, :], w_gate.astype(BF16), b_gate[None, :]


def _layer(h, depth, w_in, b_in, m_conv_w, m_conv_b, m_norm_g, w_bm, w_bf, w_o, ln1_g, ln1_b,
           w_router, b_router, w_gu, b_gu, w_dn, b_dn, ln2_g, ln2_b):
    bsz, s, d = h.shape
    t = bsz * s
    dn_alpha = (2.0 * depth) ** 0.25
    x2 = h.reshape(t, d)

    w_main, b_main, w_gate, b_gate = _pack_in_proj(w_in, b_in)
    mqk, mv, mo, fq, fk, fv, gates = _inproj(x2, w_main, b_main)
    ccol = _fox_cumsum(gates, bsz, s)
    hm = _mlstm(mqk, mv, mo, gates, m_conv_w, m_conv_b[None, :], m_norm_g[None, :], bsz, s)
    hf = _fox_attention(fq, fk, fv, ccol, bsz, s)

    n_exp = w_router.shape[1]
    wr = jnp.zeros((d, V7X_LANES), F32).at[:, :n_exp].set(w_router)
    wr_hi = wr.astype(BF16)
    wr_lo = (wr - wr_hi.astype(F32)).astype(BF16)
    br = jnp.zeros((1, V7X_LANES), F32).at[0, :n_exp].set(b_router)
    h1, h1p, gate, tope, tile_counts = _merge(
        dn_alpha, hm, hf, x2, w_gate, b_gate, w_bm.astype(BF16), w_bf.astype(BF16), w_o.astype(BF16),
        ln1_g[None, :], ln1_b[None, :], wr_hi, wr_lo, br)

    n_blocks = -(-(t * TOP_K) // MOE_BLOCK) + N_EXPERTS
    dest, table = _routing(tile_counts, tope, n_blocks)
    dest_km = dest[:, :TOP_K].T.reshape(TOP_K * t)
    block_e, block_valid = table[:, 0], table[:, 1]
    xs = _sc_dispatch(dest_km, h1p, n_blocks * MOE_BLOCK)
    y_rows = _experts(block_e, block_valid, xs, w_gu.astype(BF16), b_gu[:, None, :],
                      w_dn.astype(BF16), b_dn[:, None, :])
    yg = _sc_gather(dest_km, y_rows).reshape(TOP_K, t, d // 2)
    out = _combine(dn_alpha, h1, gate, ln2_g[None, :], ln2_b[None, :], yg)
    return out.reshape(bsz, s, d)


def kernel(x, w_in, b_in, m_conv_w, m_conv_b, m_norm_g, w_bm, w_bf, w_o, ln1_g, ln1_b,
           w_router, b_router, w_gu, b_gu, w_dn, b_dn, ln2_g, ln2_b):
    depth = w_in.shape[0]
    h = x
    for l in range(depth):
        h = _layer(h, depth, w_in[l], b_in[l], m_conv_w[l], m_conv_b[l], m_norm_g[l], w_bm[l], w_bf[l],
                   w_o[l], ln1_g[l], ln1_b[l], w_router[l], b_router[l], w_gu[l], b_gu[l], w_dn[l],
                   b_dn[l], ln2_g[l], ln2_b[l])
    return h
```

```python
import functools
import math

import jax
import jax.numpy as jnp
from jax import lax
from jax.experimental import pallas as pl
from jax.experimental.pallas import tpu as pltpu
from jax.experimental.pallas import tpu_sc as plsc

F32 = jnp.float32
BF16 = jnp.bfloat16
I32 = jnp.int32
U32 = jnp.uint32

M_HEADS = 4
M_DQK = 128
M_DV = 128
CONV_W = 4
F_HEADS = 8
F_DH = 64
N_EXPERTS = 32
TOP_K = 4
SWIGLU_ALPHA = 1.702
SWIGLU_LIMIT = 7.0
LN_EPS = 1e-5

M_W = M_HEADS * M_DV
F_W = F_HEADS * F_DH
MQK_W = 2 * M_HEADS * M_DQK

V7X_LANES = 128
V7X_SUBLANES = 8
V7X_VMEM_BYTES = 64 * 1024 * 1024
VMEM_LIMIT_BYTES = (V7X_VMEM_BYTES * 3) // 4
V7X_SC_CORES = 2
V7X_SC_SUBCORES = 16
V7X_SC_WORKERS = V7X_SC_CORES * V7X_SC_SUBCORES

INPROJ_ROWS = 512
MLSTM_CHUNK = 256
FOX_Q_BLOCK = 512
FOX_K_BLOCK = 256
MERGE_ROWS = 512
MERGE_SUB_ROWS = 256
ROUTE_ROWS = 1024
ROUTE_SUB_ROWS = 256
MOE_BLOCK = 512
SC_ROWS_PER_DMA = 128
COMBINE_ROWS = 512

GATE_COLS = V7X_LANES
MI_LANE = 0
MF_LANE = M_HEADS
FF_LANE = 2 * M_HEADS


def _params(semantics):
    return pltpu.CompilerParams(dimension_semantics=semantics, vmem_limit_bytes=VMEM_LIMIT_BYTES)


def _log_sigmoid(x):
    return jnp.minimum(x, 0.0) - jnp.log1p(jnp.exp(-jnp.abs(x)))


def _sigmoid(x):
    return 1.0 / (1.0 + jnp.exp(-x))


def _dot(a, b):
    return jnp.dot(a, b, preferred_element_type=F32)


def _dot_nt(a, b):
    return lax.dot_general(a, b, (((1,), (1,)), ((), ())), preferred_element_type=F32)


def _dot_tn(a, b):
    return lax.dot_general(a, b, (((0,), (0,)), ((), ())), preferred_element_type=F32)


def _split3(x):
    hi = x.astype(BF16)
    r1 = x - hi.astype(F32)
    mid = r1.astype(BF16)
    lo = (r1 - mid.astype(F32)).astype(BF16)
    return hi, mid, lo


def _dot_mask_f32(mask_bf16, x):
    hi, mid, lo = _split3(x)
    return (_dot(mask_bf16, lo) + _dot(mask_bf16, mid)) + _dot(mask_bf16, hi)


def _pack_bf16_pairs(x):
    half = x.shape[1] // 2
    bits = lax.bitcast_convert_type(x.astype(BF16).astype(F32), U32)
    return (bits[:, :half] >> 16) | bits[:, half:]


def _unpack_bf16_pairs(words):
    lo = lax.bitcast_convert_type(words << 16, F32)
    hi = lax.bitcast_convert_type(words & jnp.uint32(0xFFFF0000), F32)
    return jnp.concatenate([lo, hi], axis=1)


def _tril_mask(n, strict=False):
    r = lax.broadcasted_iota(I32, (n, n), 0)
    c = lax.broadcasted_iota(I32, (n, n), 1)
    return (r > c) if strict else (r >= c)


_OFF_MQK = 0
_OFF_MV = _OFF_MQK + MQK_W
_OFF_MO = _OFF_MV + M_W
_OFF_FQ = _OFF_MO + M_W
_OFF_FK = _OFF_FQ + F_W
_OFF_FV = _OFF_FK + F_W
_OFF_GATES = _OFF_FV + F_W
_PACKED_COLS = _OFF_GATES + GATE_COLS


def _inproj_kernel(x_ref, w_ref, b_ref, mqk_ref, mv_ref, mo_ref, fq_ref, fk_ref, fv_ref, gates_ref):
    xb = x_ref[...].astype(BF16)

    def seg(lo, width):
        return _dot(xb, w_ref[:, lo:lo + width]) + b_ref[:, lo:lo + width]

    mqk_ref[...] = seg(_OFF_MQK, MQK_W)
    mv_ref[...] = seg(_OFF_MV, M_W).astype(BF16)
    mo_ref[...] = seg(_OFF_MO, M_W)
    fq_ref[...] = (seg(_OFF_FQ, F_W) * (F_DH ** -0.5)).astype(BF16)
    fk_ref[...] = seg(_OFF_FK, F_W).astype(BF16)
    fv_ref[...] = seg(_OFF_FV, F_W).astype(BF16)
    gates_ref[...] = seg(_OFF_GATES, GATE_COLS)


def _inproj(x2, w_packed, b_packed):
    t, d = x2.shape
    tm = INPROJ_ROWS
    row = lambda i: (i, 0)
    const = lambda i: (0, 0)
    out_shapes = (
        jax.ShapeDtypeStruct((t, MQK_W), F32),
        jax.ShapeDtypeStruct((t, M_W), BF16),
        jax.ShapeDtypeStruct((t, M_W), F32),
        jax.ShapeDtypeStruct((t, F_W), BF16),
        jax.ShapeDtypeStruct((t, F_W), BF16),
        jax.ShapeDtypeStruct((t, F_W), BF16),
        jax.ShapeDtypeStruct((t, GATE_COLS), F32),
    )
    return pl.pallas_call(
        _inproj_kernel,
        grid=(t // tm,),
        in_specs=[
            pl.BlockSpec((tm, d), row),
            pl.BlockSpec((d, _PACKED_COLS), const),
            pl.BlockSpec((1, _PACKED_COLS), const),
        ],
        out_specs=tuple(pl.BlockSpec((tm, s.shape[1]), row) for s in out_shapes),
        out_shape=out_shapes,
        compiler_params=_params(("parallel",)),
        name="inproj",
    )(x2, w_packed, b_packed)


def _fox_cumsum_kernel(g_ref, ccol_ref, crel_k_ref, crel_q_ref):
    s = g_ref.shape[0]
    cb = FOX_K_BLOCK
    per_q = FOX_Q_BLOCK // cb
    tri = _tril_mask(cb).astype(BF16)
    carry = jnp.zeros((1, GATE_COLS), F32)
    for j in range(s // cb):
        rows = slice(j * cb, (j + 1) * cb)
        if j % per_q == 0:
            q_carry = carry
        within = _dot_mask_f32(tri, _log_sigmoid(g_ref[rows, :]))
        crel_k_ref[rows, :] = within
        crel_q_ref[rows, :] = within + (carry - q_carry)
        ccol_ref[rows, :] = within + carry
        carry = carry + within[cb - 1:cb, :]


def _fox_cumsum(gates, bsz, s):
    t = gates.shape[0]
    spec = pl.BlockSpec((s, GATE_COLS), lambda b: (b, 0))
    shape = jax.ShapeDtypeStruct((t, GATE_COLS), F32)
    return pl.pallas_call(
        _fox_cumsum_kernel,
        grid=(bsz,),
        in_specs=[spec],
        out_specs=(spec, spec, spec),
        out_shape=(shape, shape, shape),
        compiler_params=_params(("parallel",)),
        name="fox_cumsum",
    )(gates)


def _mlstm_kernel(mqk_ref, mv_ref, mo_ref, gates_ref, cw_ref, cb_ref, ng_ref, hm_ref,
                  ext_ref, state_ref, m_ref):
    L = MLSTM_CHUNK
    pad = V7X_SUBLANES
    n = pl.program_id(1)

    @pl.when(n == 0)
    def _():
        ext_ref[0:pad, :] = jnp.zeros((pad, MQK_W), F32)
        state_ref[...] = jnp.zeros(state_ref.shape, F32)
        m_ref[...] = jnp.zeros(m_ref.shape, F32)

    ext_ref[pad:pad + L, :] = mqk_ref[...]
    y = cb_ref[...] + cw_ref[CONV_W - 1:CONV_W, :] * ext_ref[pad:pad + L, :]
    for k in range(CONV_W - 1):
        shift = CONV_W - 1 - k
        y = y + cw_ref[k:k + 1, :] * ext_ref[pad - shift:pad - shift + L, :]
    ext_ref[0:pad, :] = ext_ref[L:L + pad, :]
    qk = y * _sigmoid(y)

    gates = gates_ref[...]
    lsg = _log_sigmoid(gates)
    tri = _tril_mask(L)
    bfull = _dot_mask_f32(tri.astype(BF16), lsg)
    gates_t = gates.T
    bfull_t = bfull.T
    lane = lax.broadcasted_iota(I32, (L, M_DV), 1)
    ones_col = (lane == 0).astype(BF16)

    for h in range(M_HEADS):
        i_col = gates[:, MI_LANE + h:MI_LANE + h + 1]
        b_col = bfull[:, MF_LANE + h:MF_LANE + h + 1]
        i_row = gates_t[MI_LANE + h:MI_LANE + h + 1, :]
        b_row = bfull_t[MF_LANE + h:MF_LANE + h + 1, :]
        g_tot = b_col[L - 1:L, :]
        m_prev = m_ref[h][0:1, 0:1]

        q_h = qk[:, h * M_DQK:(h + 1) * M_DQK].astype(BF16)
        k_f = qk[:, MQK_W // 2 + h * M_DQK:MQK_W // 2 + (h + 1) * M_DQK] * (M_DQK ** -0.5)
        k_h = k_f.astype(BF16)
        v_h = mv_ref[:, h * M_DV:(h + 1) * M_DV]
        cn = state_ref[h]

        dlog = jnp.where(tri, (b_col - b_row) + i_row, -jnp.inf)
        inter_log = b_col + m_prev
        m_t = jnp.maximum(inter_log, jnp.max(dlog, axis=-1, keepdims=True))
        w_inter = jnp.exp(inter_log - m_t)
        w_intra = jnp.exp(dlog - m_t)
        qkw = _dot_nt(q_h, k_h) * w_intra
        qc = _dot(q_h, cn.astype(BF16))
        num = w_inter * qc[:, :M_DV] + _dot(qkw.astype(BF16), v_h)
        den = w_inter * qc[:, M_DV:M_DV + 1] + jnp.sum(qkw, axis=-1, keepdims=True)
        hh = num / jnp.maximum(jnp.abs(den), jnp.exp(-m_t))

        mu = jnp.mean(hh, axis=-1, keepdims=True)
        dv = hh - mu
        var = jnp.mean(dv * dv, axis=-1, keepdims=True)
        hn = (dv * lax.rsqrt(var + LN_EPS)) * ng_ref[:, h * M_DV:(h + 1) * M_DV]
        hm_ref[:, h * M_DV:(h + 1) * M_DV] = (
            _sigmoid(mo_ref[:, h * M_DV:(h + 1) * M_DV]) * hn).astype(BF16)

        a_col = (g_tot - b_col) + i_col
        m_new = jnp.maximum(g_tot + m_prev, jnp.max(a_col, axis=0, keepdims=True))
        decay = jnp.exp(g_tot + m_prev - m_new)
        w_col = jnp.exp(a_col - m_new)
        kw = (k_f * w_col).astype(BF16)
        v_aug = jnp.concatenate([v_h, ones_col], axis=1)
        state_ref[h] = decay * cn + _dot_tn(kw, v_aug)
        m_ref[h] = jnp.broadcast_to(m_new, m_ref.shape[1:])


def _mlstm(mqk, mv, mo, gates, conv_w, conv_b, norm_g, bsz, s):
    t = mqk.shape[0]
    L = MLSTM_CHUNK
    nc = s // L
    row = lambda b, n: (b * nc + n, 0)
    const = lambda b, n: (0, 0)
    return pl.pallas_call(
        _mlstm_kernel,
        grid=(bsz, nc),
        in_specs=[
            pl.BlockSpec((L, MQK_W), row),
            pl.BlockSpec((L, M_W), row),
            pl.BlockSpec((L, M_W), row),
            pl.BlockSpec((L, GATE_COLS), row),
            pl.BlockSpec((CONV_W, MQK_W), const),
            pl.BlockSpec((1, MQK_W), const),
            pl.BlockSpec((1, M_W), const),
        ],
        out_specs=pl.BlockSpec((L, M_W), row),
        out_shape=jax.ShapeDtypeStruct((t, M_W), BF16),
        scratch_shapes=[
            pltpu.VMEM((L + V7X_SUBLANES, MQK_W), F32),
            pltpu.VMEM((M_HEADS, M_DQK, 2 * M_DV), F32),
            pltpu.VMEM((M_HEADS, V7X_SUBLANES, V7X_LANES), F32),
        ],
        compiler_params=_params(("parallel", "arbitrary")),
        name="mlstm",
    )(mqk, mv, mo, gates, conv_w, conv_b, norm_g)


def _fox_operand(x, c_col, hh, key_side):
    rows = x.shape[0]
    lane = lax.broadcasted_iota(I32, (rows, V7X_LANES), 1)
    own = (lane < F_DH) if hh == 0 else (lane >= F_DH)
    base = F_DH if hh == 0 else 0
    parts = [part.astype(F32) for part in _split3(c_col)]
    if key_side:
        feats = parts + [1.0, 1.0, 1.0]
    else:
        feats = [-1.0, -1.0, -1.0] + parts
    out = jnp.where(own, x, 0.0)
    for n, feat in enumerate(feats):
        out = jnp.where(lane == base + n, feat, out)
    return out.astype(BF16)


def _fox_kernel(q_ref, k_ref, v_ref, ccol_ref, crel_k_ref, crel_q_ref, o_ref, qaug_ref, kaug_ref, vt_ref,
                st_a0, st_a1, st_b0, st_b1, pe_a0, pe_a1, pe_b0, pe_b1, acc0, acc1):
    blk = FOX_K_BLOCK
    tq = FOX_Q_BLOCK
    assert tq == 2 * blk
    strips = tq // V7X_LANES
    st_a_refs, st_b_refs = (st_a0, st_a1), (st_b0, st_b1)
    pe_a_refs, pe_b_refs = (pe_a0, pe_a1), (pe_b0, pe_b1)
    acc_refs = (acc0, acc1)
    p = pl.program_id(1)
    i = pl.program_id(2)

    @pl.when(i == 0)
    def _():
        lane = lax.broadcasted_iota(I32, crel_q_ref.shape, 1)
        q_all = q_ref[...].astype(F32)
        k_all = k_ref[...].astype(F32)
        for hh in range(2):
            mine = lane == FF_LANE + 2 * p + hh
            c_q = jnp.sum(jnp.where(mine, crel_q_ref[...], 0.0), axis=-1, keepdims=True)
            c_k = jnp.sum(jnp.where(mine, crel_k_ref[...], 0.0), axis=-1, keepdims=True)
            qaug_ref[hh] = _fox_operand(q_all, c_q, hh, False)
            kaug_ref[hh] = _fox_operand(k_all, c_k, hh, True)
        v_t = v_ref[...].astype(F32).T
        for j in range(vt_ref.shape[0]):
            vt_ref[j] = v_t[:, j * blk:(j + 1) * blk].astype(BF16)

    q_start = pl.multiple_of(i * tq, tq)
    q_heads = [qaug_ref[hh, pl.ds(q_start, tq), :] for hh in range(2)]
    key_row = lax.broadcasted_iota(I32, (blk, V7X_LANES), 0)
    query_col = lax.broadcasted_iota(I32, (blk, V7X_LANES), 1)
    last_chunk = 2 * i + 1
    head_lane = lax.broadcasted_iota(I32, (1, GATE_COLS), 1) - (FF_LANE + 2 * p)

    def c_before(position, hh):
        row = ccol_ref[pl.ds(jnp.maximum(position - 1, 0), 1), :]
        keep = jnp.logical_and(head_lane == hh, position > 0)
        return jnp.sum(jnp.where(keep, row, 0.0), axis=-1, keepdims=True)

    c_query0 = [c_before(q_start, hh) for hh in range(2)]

    def scores(j, hh):
        start = pl.multiple_of(j * blk, blk)
        return _dot_nt(kaug_ref[hh, pl.ds(start, blk), :], q_heads[hh])

    def softmax_update(st_ref, pe_ref, m_old, l_old, base, key_minus_query=None):
        alphas, ms, ls = [], [], []
        for c in range(strips):
            cols = slice(c * V7X_LANES, (c + 1) * V7X_LANES)
            gap = None if key_minus_query is None else key_minus_query - c * V7X_LANES
            if gap is not None and gap - (V7X_LANES - 1) > 0:
                pe_ref[:, cols] = jnp.zeros((blk, V7X_LANES), BF16)
                alphas.append(jnp.ones((1, V7X_LANES), F32))
                ms.append(m_old[:, cols])
                ls.append(l_old[:, cols])
                continue
            st = st_ref[:, cols]
            if gap is not None and gap + (blk - 1) > 0:
                st = jnp.where(key_row + gap <= query_col, st, -jnp.inf)
            m_new = jnp.maximum(m_old[:, cols], jnp.max(st, axis=0, keepdims=True) + base)
            alpha = jnp.exp(m_old[:, cols] - m_new)
            pe = jnp.exp(st - (m_new - base))
            pe_ref[:, cols] = pe.astype(BF16)
            alphas.append(alpha)
            ms.append(m_new)
            ls.append(alpha * l_old[:, cols] + jnp.sum(pe, axis=0, keepdims=True))
        cat = lambda parts: jnp.concatenate(parts, axis=1)
        return cat(alphas), cat(ms), cat(ls)

    def pair(mi, carry, diagonal=False):
        a = 2 * mi
        b = a + 1
        v_prev = vt_ref[jnp.maximum(a - 1, 0)]
        v_a = vt_ref[a]
        partial = []
        for hh in range(2):
            alpha_prev = carry[hh][0]
            partial.append(alpha_prev * acc_refs[hh][...] + _dot(v_prev, pe_b_refs[hh][...]))
            st_b_refs[hh][...] = scores(b, hh)
        stats = []
        for hh in range(2):
            _, m_old, l_old = carry[hh]
            stats.append(softmax_update(st_a_refs[hh], pe_a_refs[hh], m_old, l_old,
                                        c_query0[hh] - c_before(a * blk, hh), 0 if diagonal else None))
        for hh in range(2):
            alpha_a = stats[hh][0]
            acc_refs[hh][...] = alpha_a * partial[hh] + _dot(v_a, pe_a_refs[hh][...])
            if not diagonal:
                st_a_refs[hh][...] = scores(a + 2, hh)
        return tuple(softmax_update(st_b_refs[hh], pe_b_refs[hh], stats[hh][1], stats[hh][2],
                                    c_query0[hh] - c_before(b * blk, hh), blk if diagonal else None)
                     for hh in range(2))

    for hh in range(2):
        st_a_refs[hh][...] = scores(0, hh)
        pe_b_refs[hh][...] = jnp.zeros((blk, tq), BF16)
        acc_refs[hh][...] = jnp.zeros((V7X_LANES, tq), F32)
    init = tuple((jnp.ones((1, tq), F32), jnp.full((1, tq), -jnp.inf, F32), jnp.zeros((1, tq), F32))
                 for _ in range(2))
    final = pair(i, lax.fori_loop(0, i, pair, init), diagonal=True)
    v_last = vt_ref[last_chunk]
    outs = []
    for hh in range(2):
        alpha, _, l_fin = final[hh]
        outs.append((alpha * acc_refs[hh][...] + _dot(v_last, pe_b_refs[hh][...])) / l_fin)
    row = lax.broadcasted_iota(I32, (V7X_LANES, tq), 0)
    o_t = jnp.where(row < F_DH, outs[0], outs[1])
    o_ref[...] = o_t.T.astype(BF16)


def _fox_attention(fq, fk, fv, ccol, crel_k, crel_q, bsz, s):
    t = fq.shape[0]
    blk = FOX_K_BLOCK
    tq = FOX_Q_BLOCK
    nq = s // tq
    pairs = F_HEADS // 2
    qmap = lambda b, p, i: (b * nq + i, p)
    kvmap = lambda b, p, i: (b, p)
    return pl.pallas_call(
        _fox_kernel,
        grid=(bsz, pairs, nq),
        in_specs=[
            pl.BlockSpec((s, V7X_LANES), kvmap),
            pl.BlockSpec((s, V7X_LANES), kvmap),
            pl.BlockSpec((s, V7X_LANES), kvmap),
            pl.BlockSpec((s, GATE_COLS), lambda b, p, i: (b, 0)),
            pl.BlockSpec((s, GATE_COLS), lambda b, p, i: (b, 0)),
            pl.BlockSpec((s, GATE_COLS), lambda b, p, i: (b, 0)),
        ],
        out_specs=pl.BlockSpec((tq, V7X_LANES), qmap),
        out_shape=jax.ShapeDtypeStruct((t, F_W), BF16),
        scratch_shapes=[
            pltpu.VMEM((2, s, V7X_LANES), BF16),
            pltpu.VMEM((2, s, V7X_LANES), BF16),
            pltpu.VMEM((s // blk, V7X_LANES, blk), BF16),
        ] + [pltpu.VMEM((blk, tq), F32)] * 4 + [pltpu.VMEM((blk, tq), BF16)] * 4
          + [pltpu.VMEM((V7X_LANES, tq), F32)] * 2,
        compiler_params=_params(("parallel", "parallel", "arbitrary")),
        name="fox_attention",
    )(fq, fk, fv, ccol, crel_k, crel_q)


def _layer_norm_rows(r, g, b):
    mu = jnp.mean(r, axis=-1, keepdims=True)
    d = r - mu
    var = jnp.mean(d * d, axis=-1, keepdims=True)
    return (d * lax.rsqrt(var + LN_EPS)) * g + b


def _merge_kernel(dn_alpha, hm_ref, hf_ref, x_ref, wg_ref, bg_ref, wbm_ref, wbf_ref, wo_ref,
                  g_ref, b_ref, wrh_ref, wrl_ref, br_ref, h1_ref, h1p_ref, gate_ref, tope_ref, cnt_ref):
    counts = jnp.zeros((1, V7X_LANES), F32)
    for n in range(x_ref.shape[0] // MERGE_SUB_ROWS):
        rows = slice(n * MERGE_SUB_ROWS, (n + 1) * MERGE_SUB_ROWS)
        counts = counts + _merge_rows(dn_alpha, rows, hm_ref, hf_ref, x_ref, wg_ref, bg_ref, wbm_ref,
                                      wbf_ref, wo_ref, g_ref, b_ref, wrh_ref, wrl_ref, br_ref,
                                      h1_ref, h1p_ref, gate_ref, tope_ref)
    sub = lax.broadcasted_iota(I32, cnt_ref.shape, 0)
    cnt_ref[...] = jnp.where(sub == 0, counts, 0.0)


def _merge_rows(dn_alpha, rows, hm_ref, hf_ref, x_ref, wg_ref, bg_ref, wbm_ref, wbf_ref, wo_ref,
                g_ref, b_ref, wrh_ref, wrl_ref, br_ref, h1_ref, h1p_ref, gate_ref, tope_ref):
    x = x_ref[rows, :]
    d = x.shape[1]
    gmf = _dot(x.astype(BF16), wg_ref[...]) + bg_ref[...]
    ym = _dot(hm_ref[rows, :], wbm_ref[...])
    yf = _dot(hf_ref[rows, :], wbf_ref[...])
    y = _sigmoid(gmf[:, :d]) * ym + _sigmoid(gmf[:, d:]) * yf
    mix = _dot(y.astype(BF16), wo_ref[...])
    h1 = _layer_norm_rows(dn_alpha * x + mix, g_ref[...], b_ref[...])
    h1_ref[rows, :] = h1

    h1p_ref[rows, :] = _pack_bf16_pairs(h1)
    hb = h1.astype(BF16)

    lo = (h1 - hb.astype(F32)).astype(BF16)
    logits = (_dot(lo, wrh_ref[...]) + _dot(hb, wrl_ref[...])) + _dot(hb, wrh_ref[...]) + br_ref[...]
    tm = logits.shape[0]
    lane = lax.broadcasted_iota(I32, (tm, V7X_LANES), 1)
    vals = jnp.where(lane < N_EXPERTS, logits, -jnp.inf)
    top_v, top_i = [], []
    for _ in range(TOP_K):
        mx = jnp.max(vals, axis=-1, keepdims=True)
        idx = jnp.min(jnp.where(vals == mx, lane, V7X_LANES), axis=-1, keepdims=True)
        top_v.append(mx)
        top_i.append(idx)
        vals = jnp.where(lane == idx, -jnp.inf, vals)
    ex = [jnp.exp(v - top_v[0]) for v in top_v]
    tot = ex[0]
    for e in ex[1:]:
        tot = tot + e
    gate = jnp.zeros((tm, V7X_LANES), F32)
    tope = jnp.zeros((tm, V7X_LANES), I32)
    member = jnp.zeros((tm, V7X_LANES), F32)
    for k in range(TOP_K):
        gate = jnp.where(lane == k, ex[k] / tot, gate)
        tope = jnp.where(lane == k, top_i[k], tope)
        member = member + (lane == top_i[k]).astype(F32)
    gate_ref[rows, :] = gate
    tope_ref[rows, :] = tope
    return jnp.sum(member, axis=0, keepdims=True)


def _merge(dn_alpha, hm, hf, x2, wg, bg, wbm, wbf, wo, ln_g, ln_b, wr_hi, wr_lo, br):
    t, d = x2.shape
    tm = MERGE_ROWS
    row = lambda i: (i, 0)
    const = lambda i: (0, 0)
    full = lambda a: pl.BlockSpec(a.shape, const)
    return pl.pallas_call(
        functools.partial(_merge_kernel, dn_alpha),
        grid=(t // tm,),
        in_specs=[
            pl.BlockSpec((tm, M_W), row),
            pl.BlockSpec((tm, F_W), row),
            pl.BlockSpec((tm, d), row),
            full(wg), full(bg), full(wbm), full(wbf), full(wo), full(ln_g), full(ln_b),
            full(wr_hi), full(wr_lo), full(br),
        ],
        out_specs=(
            pl.BlockSpec((tm, d), row),
            pl.BlockSpec((tm, d // 2), row),
            pl.BlockSpec((tm, V7X_LANES), row),
            pl.BlockSpec((tm, V7X_LANES), row),
            pl.BlockSpec((V7X_SUBLANES, V7X_LANES), row),
        ),
        out_shape=(
            jax.ShapeDtypeStruct((t, d), F32),
            jax.ShapeDtypeStruct((t, d // 2), U32),
            jax.ShapeDtypeStruct((t, V7X_LANES), F32),
            jax.ShapeDtypeStruct((t, V7X_LANES), I32),
            jax.ShapeDtypeStruct((t // tm * V7X_SUBLANES, V7X_LANES), F32),
        ),
        compiler_params=_params(("parallel",)),
        name="merge_ln1_router",
    )(hm, hf, x2, wg, bg, wbm, wbf, wo, ln_g, ln_b, wr_hi, wr_lo, br)


def _lane_cumsum(x):
    lane = lax.broadcasted_iota(I32, x.shape, 1)
    d = 1
    while d < V7X_LANES:
        x = x + jnp.where(lane >= d, pltpu.roll(x, shift=d, axis=1), 0.0)
        d *= 2
    return x


def _routing_kernel(cnt_ref, tope_ref, dest_ref, table_ref, run_ref, start_ref):
    sb = ROUTE_SUB_ROWS

    @pl.when(pl.program_id(0) == 0)
    def _():
        total = jnp.sum(cnt_ref[...], axis=0, keepdims=True)
        counts = jnp.broadcast_to(total, (V7X_SUBLANES, V7X_LANES))
        padded = jnp.ceil(counts * (1.0 / MOE_BLOCK)) * MOE_BLOCK
        pad_end = _lane_cumsum(padded)
        pad_start = pad_end - padded
        start_ref[...] = pad_start
        run_ref[...] = jnp.zeros(run_ref.shape, F32)
        nb = table_ref.shape[0]
        blk = lax.broadcasted_iota(I32, (nb, V7X_LANES), 0).astype(F32) * MOE_BLOCK
        ln = lax.broadcasted_iota(I32, (nb, V7X_LANES), 1)
        done = jnp.logical_and(pad_end[0:1, :] <= blk, ln < N_EXPERTS)
        be = jnp.minimum(jnp.sum(done.astype(F32), axis=-1, keepdims=True), N_EXPERTS - 1.0)
        onehot = ln == be.astype(I32)
        cnt_e = jnp.sum(jnp.where(onehot, counts[0:1, :], 0.0), axis=-1, keepdims=True)
        start_e = jnp.sum(jnp.where(onehot, pad_start[0:1, :], 0.0), axis=-1, keepdims=True)
        valid = jnp.clip(cnt_e - (blk[:, 0:1] - start_e), 0.0, float(MOE_BLOCK))
        table_ref[...] = jnp.where(ln == 0, be.astype(I32),
                                   jnp.where(ln == 1, valid.astype(I32), 0))

    earlier = _tril_mask(sb, strict=True).astype(BF16)
    lane = lax.broadcasted_iota(I32, (sb, V7X_LANES), 1)
    for j in range(tope_ref.shape[0] // sb):
        tope = tope_ref[j * sb:(j + 1) * sb, :]
        hit = [lane == tope[:, k:k + 1] for k in range(TOP_K)]
        member = jnp.zeros((sb, V7X_LANES), F32)
        for k in range(TOP_K):
            member = member + hit[k].astype(F32)
        base = _dot(earlier, member.astype(BF16)) + (run_ref[0:1, :] + start_ref[0:1, :])
        dest = jnp.zeros((sb, V7X_LANES), I32)
        for k in range(TOP_K):
            dk = jnp.sum(jnp.where(hit[k], base, 0.0), axis=-1, keepdims=True)
            dest = jnp.where(lane == k, dk.astype(I32), dest)
        dest_ref[j * sb:(j + 1) * sb, :] = dest
        run_ref[...] = run_ref[...] + jnp.sum(member, axis=0, keepdims=True)


def _routing(tile_counts, tope, n_blocks):
    t = tope.shape[0]
    tr = ROUTE_ROWS
    return pl.pallas_call(
        _routing_kernel,
        grid=(t // tr,),
        in_specs=[pl.BlockSpec(tile_counts.shape, lambda i: (0, 0)),
                  pl.BlockSpec((tr, V7X_LANES), lambda i: (i, 0))],
        out_specs=(
            pl.BlockSpec((tr, V7X_LANES), lambda i: (i, 0)),
            pl.BlockSpec((n_blocks, V7X_LANES), lambda i: (0, 0)),
        ),
        out_shape=(
            jax.ShapeDtypeStruct((t, V7X_LANES), I32),
            jax.ShapeDtypeStruct((n_blocks, V7X_LANES), I32),
        ),
        scratch_shapes=[
            pltpu.VMEM((V7X_SUBLANES, V7X_LANES), F32),
            pltpu.VMEM((V7X_SUBLANES, V7X_LANES), F32),
        ],
        compiler_params=_params(("arbitrary",)),
        name="routing",
    )(tile_counts, tope)


def _sc_worker_id():
    return lax.axis_index("s") * V7X_SC_CORES + lax.axis_index("c")


def _sc_mesh():
    return plsc.VectorSubcoreMesh(core_axis_name="c", subcore_axis_name="s",
                                  num_cores=V7X_SC_CORES, num_subcores=V7X_SC_SUBCORES)


def _sc_dispatch(dest_km, h1p, n_rows):
    t, w = h1p.shape
    per_worker = t // V7X_SC_WORKERS
    ch = SC_ROWS_PER_DMA

    @functools.partial(
        pl.kernel, mesh=_sc_mesh(),
        out_type=jax.ShapeDtypeStruct((n_rows, w), h1p.dtype),
        scratch_types=[pltpu.VMEM((ch,), I32), pltpu.VMEM((ch, w), h1p.dtype), pltpu.SemaphoreType.DMA],
        name="sc_dispatch",
    )
    def scatter_rows(dest_hbm, h1p_hbm, xs_hbm, idx_v, rows_v, sem):
        first = _sc_worker_id() * per_worker

        @pl.loop(0, per_worker // ch)
        def _(j):
            base = first + j * ch
            pltpu.sync_copy(h1p_hbm.at[pl.ds(base, ch)], rows_v)
            for k in range(TOP_K):
                pltpu.sync_copy(dest_hbm.at[pl.ds(k * t + base, ch)], idx_v)
                pltpu.async_copy(rows_v, xs_hbm.at[idx_v], sem).wait()

    return scatter_rows(dest_km, h1p)


def _sc_gather(dest_km, y_rows):
    n = dest_km.shape[0]
    w = y_rows.shape[1]
    per_worker = n // V7X_SC_WORKERS
    ch = SC_ROWS_PER_DMA

    @functools.partial(
        pl.kernel, mesh=_sc_mesh(),
        out_type=jax.ShapeDtypeStruct((n, w), y_rows.dtype),
        scratch_types=[pltpu.VMEM((ch,), I32), pltpu.VMEM((ch, w), y_rows.dtype), pltpu.SemaphoreType.DMA],
        name="sc_gather",
    )
    def gather_rows(dest_hbm, y_hbm, out_hbm, idx_v, rows_v, sem):
        first = _sc_worker_id() * per_worker

        @pl.loop(0, per_worker // ch)
        def _(j):
            base = first + j * ch
            pltpu.sync_copy(dest_hbm.at[pl.ds(base, ch)], idx_v)
            pltpu.async_copy(y_hbm.at[idx_v], rows_v, sem).wait()
            pltpu.sync_copy(rows_v, out_hbm.at[pl.ds(base, ch)])

    return gather_rows(dest_km, y_rows)


def _expert_kernel(be_ref, nv_ref, xs_ref, wgu_ref, bgu_ref, wdn_ref, bdn_ref, y_ref):
    i = pl.program_id(0)
    nv = nv_ref[i]

    @pl.when(nv == 0)
    def _():
        y_ref[...] = jnp.zeros(y_ref.shape, U32)

    @pl.when(nv > 0)
    def _():
        x = _unpack_bf16_pairs(xs_ref[...])
        rowid = lax.broadcasted_iota(I32, x.shape, 0)
        x = jnp.where(rowid < nv, x, 0.0).astype(BF16)
        gu = _dot(x, wgu_ref[...]) + bgu_ref[...]
        f = gu.shape[1] // 2
        glu = jnp.minimum(gu[:, :f], SWIGLU_LIMIT)
        lin = jnp.clip(gu[:, f:], -SWIGLU_LIMIT, SWIGLU_LIMIT)
        act = glu * _sigmoid(SWIGLU_ALPHA * glu) * (lin + 1.0)
        y_ref[...] = _pack_bf16_pairs(_dot(act.astype(BF16), wdn_ref[...]) + bdn_ref[...])


def _experts(block_e, block_valid, xs, wgu, bgu, wdn, bdn):
    n_rows, w = xs.shape
    e, d, f2 = wgu.shape
    n_blocks = n_rows // MOE_BLOCK
    grid_spec = pltpu.PrefetchScalarGridSpec(
        num_scalar_prefetch=2,
        grid=(n_blocks,),
        in_specs=[
            pl.BlockSpec((MOE_BLOCK, w), lambda i, be, nv: (i, 0)),
            pl.BlockSpec((None, d, f2), lambda i, be, nv: (be[i], 0, 0)),
            pl.BlockSpec((None, 1, f2), lambda i, be, nv: (be[i], 0, 0)),
            pl.BlockSpec((None, f2 // 2, d), lambda i, be, nv: (be[i], 0, 0)),
            pl.BlockSpec((None, 1, d), lambda i, be, nv: (be[i], 0, 0)),
        ],
        out_specs=pl.BlockSpec((MOE_BLOCK, d // 2), lambda i, be, nv: (i, 0)),
    )
    return pl.pallas_call(
        _expert_kernel,
        grid_spec=grid_spec,
        out_shape=jax.ShapeDtypeStruct((n_rows, d // 2), U32),
        compiler_params=_params(("arbitrary",)),
        name="experts",
    )(block_e, block_valid, xs, wgu, bgu, wdn, bdn)


def _combine_kernel(dn_alpha, h1_ref, gate_ref, g_ref, b_ref, yg_ref, o_ref):
    gate = gate_ref[...]
    ffn = gate[:, 0:1] * _unpack_bf16_pairs(yg_ref[0])
    for k in range(1, TOP_K):
        ffn = ffn + gate[:, k:k + 1] * _unpack_bf16_pairs(yg_ref[k])
    o_ref[...] = _layer_norm_rows(dn_alpha * h1_ref[...] + ffn, g_ref[...], b_ref[...])


def _combine(dn_alpha, h1, gate, ln_g, ln_b, yg):
    t, d = h1.shape
    tc = COMBINE_ROWS
    row = lambda i: (i, 0)
    const = lambda i: (0, 0)
    return pl.pallas_call(
        functools.partial(_combine_kernel, dn_alpha),
        grid=(t // tc,),
        in_specs=[
            pl.BlockSpec((tc, d), row),
            pl.BlockSpec((tc, V7X_LANES), row),
            pl.BlockSpec((1, d), const),
            pl.BlockSpec((1, d), const),
            pl.BlockSpec((TOP_K, tc, d // 2), lambda i: (0, i, 0)),
        ],
        out_specs=pl.BlockSpec((tc, d), row),
        out_shape=jax.ShapeDtypeStruct((t, d), F32),
        compiler_params=_params(("parallel",)),
        name="combine_ln2",
    )(h1, gate, ln_g, ln_b, yg)


def _pack_in_proj(w_in, b_in):
    d = w_in.shape[0]
    o = 0
    cols = {}
    for name, width in (("mqk", MQK_W), ("mv", M_W), ("mo", M_W), ("mi", M_HEADS), ("mf", M_HEADS),
                        ("fq", F_W), ("fk", F_W), ("fv", F_W), ("ff", F_HEADS), ("gm", d), ("gf", d)):
        cols[name] = (o, o + width)
        o += width

    def take(a, names):
        return [a[..., cols[n][0]:cols[n][1]] for n in names]

    n_gate = 2 * M_HEADS + F_HEADS
    main = ("mqk", "mv", "mo", "fq", "fk", "fv", "mi", "mf", "ff")
    w_main = jnp.concatenate(take(w_in, main) + [jnp.zeros((d, GATE_COLS - n_gate), w_in.dtype)], axis=1)
    b_main = jnp.concatenate(take(b_in, main) + [jnp.zeros((GATE_COLS - n_gate,), b_in.dtype)])
    w_gate = jnp.concatenate(take(w_in, ("gm", "gf")), axis=1)
    b_gate = jnp.concatenate(take(b_in, ("gm", "gf")))
    return w_main.astype(BF16), b_main[None, :], w_gate.astype(BF16), b_gate[None, :]


def _layer(h, depth, w_in, b_in, m_conv_w, m_conv_b, m_norm_g, w_bm, w_bf, w_o, ln1_g, ln1_b,
           w_router, b_router, w_gu, b_gu, w_dn, b_dn, ln2_g, ln2_b):
    bsz, s, d = h.shape
    t = bsz * s
    dn_alpha = (2.0 * depth) ** 0.25
    x2 = h.reshape(t, d)

    w_main, b_main, w_gate, b_gate = _pack_in_proj(w_in, b_in)
    mqk, mv, mo, fq, fk, fv, gates = _inproj(x2, w_main, b_main)
    ccol, crel_k, crel_q = _fox_cumsum(gates, bsz, s)
    hm = _mlstm(mqk, mv, mo, gates, m_conv_w, m_conv_b[None, :], m_norm_g[None, :], bsz, s)
    hf = _fox_attention(fq, fk, fv, ccol, crel_k, crel_q, bsz, s)

    n_exp = w_router.shape[1]
    wr = jnp.zeros((d, V7X_LANES), F32).at[:, :n_exp].set(w_router)
    wr_hi = wr.astype(BF16)
    wr_lo = (wr - wr_hi.astype(F32)).astype(BF16)
    br = jnp.zeros((1, V7X_LANES), F32).at[0, :n_exp].set(b_router)
    h1, h1p, gate, tope, tile_counts = _merge(
        dn_alpha, hm, hf, x2, w_gate, b_gate, w_bm.astype(BF16), w_bf.astype(BF16), w_o.astype(BF16),
        ln1_g[None, :], ln1_b[None, :], wr_hi, wr_lo, br)

    n_blocks = -(-(t * TOP_K) // MOE_BLOCK) + N_EXPERTS
    dest, table = _routing(tile_counts, tope, n_blocks)
    dest_km = dest[:, :TOP_K].T.reshape(TOP_K * t)
    block_e, block_valid = table[:, 0], table[:, 1]
    xs = _sc_dispatch(dest_km, h1p, n_blocks * MOE_BLOCK)
    y_rows = _experts(block_e, block_valid, xs, w_gu.astype(BF16), b_gu[:, None, :],
                      w_dn.astype(BF16), b_dn[:, None, :])
    yg = _sc_gather(dest_km, y_rows).reshape(TOP_K, t, d // 2)
    out = _combine(dn_alpha, h1, gate, ln2_g[None, :], ln2_b[None, :], yg)
    return out.reshape(bsz, s, d)


def kernel(x, w_in, b_in, m_conv_w, m_conv_b, m_norm_g, w_bm, w_bf, w_o, ln1_g, ln1_b,
           w_router, b_router, w_gu, b_gu, w_dn, b_dn, ln2_g, ln2_b):
    depth = w_in.shape[0]
    h = x
    for l in range(depth):
        h = _layer(h, depth, w_in[l], b_in[l], m_conv_w[l], m_conv_b[l], m_norm_g[l], w_bm[l], w_bf[l],
                   w_o[l], ln1_g[l], ln1_b[l], w_router[l], b_router[l], w_gu[l], b_gu[l], w_dn[l],
                   b_dn[l], ln2_g[l], ln2_b[l])
    return h
```

```python
import functools
import math

import jax
import jax.numpy as jnp
from jax import lax
from jax.experimental import pallas as pl
from jax.experimental.pallas import tpu as pltpu
from jax.experimental.pallas import tpu_sc as plsc

F32 = jnp.float32
BF16 = jnp.bfloat16
I32 = jnp.int32
U32 = jnp.uint32

M_HEADS = 4
M_DQK = 128
M_DV = 128
CONV_W = 4
F_HEADS = 8
F_DH = 64
N_EXPERTS = 32
TOP_K = 4
SWIGLU_ALPHA = 1.702
SWIGLU_LIMIT = 7.0
LN_EPS = 1e-5

M_W = M_HEADS * M_DV
F_W = F_HEADS * F_DH
MQK_W = 2 * M_HEADS * M_DQK

V7X_LANES = 128
V7X_SUBLANES = 8
V7X_VMEM_BYTES = 64 * 1024 * 1024
VMEM_LIMIT_BYTES = (V7X_VMEM_BYTES * 3) // 4
V7X_SC_CORES = 2
V7X_SC_SUBCORES = 16
V7X_SC_WORKERS = V7X_SC_CORES * V7X_SC_SUBCORES

INPROJ_ROWS = 512
MLSTM_CHUNK = 256
MLSTM_SEQS = 2
FOX_Q_BLOCK = 512
FOX_K_BLOCK = 256
MERGE_ROWS = 512
MERGE_SUB_ROWS = 256
ROUTE_ROWS = 1024
ROUTE_SUB_ROWS = 256
MOE_BLOCK = 512
SC_ROWS_PER_DMA = 128
COMBINE_ROWS = 512

GATE_COLS = V7X_LANES
MI_LANE = 0
MF_LANE = M_HEADS
FF_LANE = 2 * M_HEADS


def _params(semantics):
    return pltpu.CompilerParams(dimension_semantics=semantics, vmem_limit_bytes=VMEM_LIMIT_BYTES)


def _log_sigmoid(x):
    return jnp.minimum(x, 0.0) - jnp.log1p(jnp.exp(-jnp.abs(x)))


def _sigmoid(x):
    return 1.0 / (1.0 + jnp.exp(-x))


def _dot(a, b):
    return jnp.dot(a, b, preferred_element_type=F32)


def _dot_nt(a, b):
    return lax.dot_general(a, b, (((1,), (1,)), ((), ())), preferred_element_type=F32)


def _dot_tn(a, b):
    return lax.dot_general(a, b, (((0,), (0,)), ((), ())), preferred_element_type=F32)


def _split3(x):
    hi = x.astype(BF16)
    r1 = x - hi.astype(F32)
    mid = r1.astype(BF16)
    lo = (r1 - mid.astype(F32)).astype(BF16)
    return hi, mid, lo


def _dot_mask_f32(mask_bf16, x):
    hi, mid, lo = _split3(x)
    return (_dot(mask_bf16, lo) + _dot(mask_bf16, mid)) + _dot(mask_bf16, hi)


def _pack_bf16_pairs(x):
    half = x.shape[1] // 2
    bits = lax.bitcast_convert_type(x.astype(BF16).astype(F32), U32)
    return (bits[:, :half] >> 16) | bits[:, half:]


def _unpack_bf16_pairs(words):
    lo = lax.bitcast_convert_type(words << 16, F32)
    hi = lax.bitcast_convert_type(words & jnp.uint32(0xFFFF0000), F32)
    return jnp.concatenate([lo, hi], axis=1)


def _tril_mask(n, strict=False):
    r = lax.broadcasted_iota(I32, (n, n), 0)
    c = lax.broadcasted_iota(I32, (n, n), 1)
    return (r > c) if strict else (r >= c)


_OFF_MQK = 0
_OFF_MV = _OFF_MQK + MQK_W
_OFF_MO = _OFF_MV + M_W
_OFF_FQ = _OFF_MO + M_W
_OFF_FK = _OFF_FQ + F_W
_OFF_FV = _OFF_FK + F_W
_OFF_GATES = _OFF_FV + F_W
_PACKED_COLS = _OFF_GATES + GATE_COLS


def _inproj_kernel(tiles_per_seq, x_ref, w_ref, b_ref, cw_ref, cb_ref,
                   mq_ref, mk_ref, mv_ref, mo_ref, fq_ref, fk_ref, fv_ref, gates_ref, ext_ref):
    tm = x_ref.shape[0]
    pad = V7X_SUBLANES
    xb = x_ref[...].astype(BF16)

    def seg(lo, width):
        return _dot(xb, w_ref[:, lo:lo + width]) + b_ref[:, lo:lo + width]

    @pl.when(pl.program_id(0) % tiles_per_seq == 0)
    def _():
        ext_ref[0:pad, :] = jnp.zeros((pad, MQK_W), F32)

    ext_ref[pad:pad + tm, :] = seg(_OFF_MQK, MQK_W)
    y = cb_ref[...] + cw_ref[CONV_W - 1:CONV_W, :] * ext_ref[pad:pad + tm, :]
    for k in range(CONV_W - 1):
        shift = CONV_W - 1 - k
        y = y + cw_ref[k:k + 1, :] * ext_ref[pad - shift:pad - shift + tm, :]
    ext_ref[0:pad, :] = ext_ref[tm:tm + pad, :]
    act = y * _sigmoid(y)
    mq_ref[...] = act[:, :MQK_W // 2].astype(BF16)
    mk_ref[...] = act[:, MQK_W // 2:] * (M_DQK ** -0.5)
    mv_ref[...] = seg(_OFF_MV, M_W).astype(BF16)
    mo_ref[...] = seg(_OFF_MO, M_W)
    fq_ref[...] = (seg(_OFF_FQ, F_W) * (F_DH ** -0.5)).astype(BF16)
    fk_ref[...] = seg(_OFF_FK, F_W).astype(BF16)
    fv_ref[...] = seg(_OFF_FV, F_W).astype(BF16)
    gates_ref[...] = seg(_OFF_GATES, GATE_COLS)


def _inproj(x2, w_packed, b_packed, conv_w, conv_b, s):
    t, d = x2.shape
    tm = INPROJ_ROWS
    row = lambda i: (i, 0)
    const = lambda i: (0, 0)
    out_shapes = (
        jax.ShapeDtypeStruct((t, MQK_W // 2), BF16),
        jax.ShapeDtypeStruct((t, MQK_W // 2), F32),
        jax.ShapeDtypeStruct((t, M_W), BF16),
        jax.ShapeDtypeStruct((t, M_W), F32),
        jax.ShapeDtypeStruct((t, F_W), BF16),
        jax.ShapeDtypeStruct((t, F_W), BF16),
        jax.ShapeDtypeStruct((t, F_W), BF16),
        jax.ShapeDtypeStruct((t, GATE_COLS), F32),
    )
    return pl.pallas_call(
        functools.partial(_inproj_kernel, s // tm),
        grid=(t // tm,),
        in_specs=[
            pl.BlockSpec((tm, d), row),
            pl.BlockSpec((d, _PACKED_COLS), const),
            pl.BlockSpec((1, _PACKED_COLS), const),
            pl.BlockSpec((CONV_W, MQK_W), const),
            pl.BlockSpec((1, MQK_W), const),
        ],
        out_specs=tuple(pl.BlockSpec((tm, o.shape[1]), row) for o in out_shapes),
        out_shape=out_shapes,
        scratch_shapes=[pltpu.VMEM((tm + V7X_SUBLANES, MQK_W), F32)],
        compiler_params=_params(("arbitrary",)),
        name="inproj",
    )(x2, w_packed, b_packed, conv_w, conv_b)


def _fox_cumsum_kernel(g_ref, ccol_ref, crel_k_ref, crel_q_ref):
    s = g_ref.shape[0]
    cb = FOX_K_BLOCK
    per_q = FOX_Q_BLOCK // cb
    tri = _tril_mask(cb).astype(BF16)
    carry = jnp.zeros((1, GATE_COLS), F32)
    for j in range(s // cb):
        rows = slice(j * cb, (j + 1) * cb)
        if j % per_q == 0:
            q_carry = carry
        within = _dot_mask_f32(tri, _log_sigmoid(g_ref[rows, :]))
        crel_k_ref[rows, :] = within
        crel_q_ref[rows, :] = within + (carry - q_carry)
        ccol_ref[rows, :] = within + carry
        carry = carry + within[cb - 1:cb, :]


def _fox_cumsum(gates, bsz, s):
    t = gates.shape[0]
    spec = pl.BlockSpec((s, GATE_COLS), lambda b: (b, 0))
    shape = jax.ShapeDtypeStruct((t, GATE_COLS), F32)
    return pl.pallas_call(
        _fox_cumsum_kernel,
        grid=(bsz,),
        in_specs=[spec],
        out_specs=(spec, spec, spec),
        out_shape=(shape, shape, shape),
        compiler_params=_params(("parallel",)),
        name="fox_cumsum",
    )(gates)


def _mlstm_kernel(mq_ref, mk_ref, mv_ref, mo_ref, gates_ref, ng_ref, hm_ref, state_ref, m_ref):
    @pl.when(pl.program_id(1) == 0)
    def _():
        state_ref[...] = jnp.zeros(state_ref.shape, F32)
        m_ref[...] = jnp.zeros(m_ref.shape, F32)

    seqs = range(mq_ref.shape[0])
    states = [[state_ref[bb, h] for h in range(M_HEADS)] for bb in seqs]
    maxes = [[m_ref[bb, h][0:1, 0:1] for h in range(M_HEADS)] for bb in seqs]
    results = [_mlstm_chunk(mq_ref.at[bb], mk_ref.at[bb], mv_ref.at[bb], mo_ref.at[bb], gates_ref.at[bb],
                            ng_ref, states[bb], maxes[bb]) for bb in seqs]
    for bb in seqs:
        for h, (out_h, state_h, m_h) in enumerate(results[bb]):
            hm_ref[bb, :, h * M_DV:(h + 1) * M_DV] = out_h
            state_ref[bb, h] = state_h
            m_ref[bb, h] = jnp.broadcast_to(m_h, m_ref.shape[2:])


def _mlstm_chunk(mq_ref, mk_ref, mv_ref, mo_ref, gates_ref, ng_ref, states, maxes):
    L = MLSTM_CHUNK
    reps = L // V7X_LANES
    results = []
    gates = gates_ref[...]
    bfull = _dot_mask_f32(_tril_mask(L).astype(BF16), _log_sigmoid(gates))
    b_rows = bfull.T
    z_all = gates - pltpu.roll(bfull, shift=V7X_LANES - (MF_LANE - MI_LANE), axis=1)
    visible = (lax.broadcasted_iota(I32, (L, L), 0) <= lax.broadcasted_iota(I32, (L, L), 1))
    ones_rows = (lax.broadcasted_iota(I32, (M_DV, L), 0) == 0).astype(BF16)

    for h in range(M_HEADS):
        b_row = b_rows[MF_LANE + h:MF_LANE + h + 1, :]
        g_tot = b_row[:, L - 1:L]
        m_prev = maxes[h]
        z = jnp.broadcast_to(z_all[:, MI_LANE + h:MI_LANE + h + 1], (L, V7X_LANES))

        q_h = mq_ref[:, h * M_DQK:(h + 1) * M_DQK]
        k_f = mk_ref[:, h * M_DQK:(h + 1) * M_DQK]
        k_h = k_f.astype(BF16)
        v_t = mv_ref[:, h * M_DV:(h + 1) * M_DV].astype(F32).T.astype(BF16)
        cn_t = states[h]

        dlog = jnp.where(visible, b_row + jnp.tile(z, (1, reps)), -jnp.inf)
        inter_log = b_row + m_prev
        m_t = jnp.maximum(inter_log, jnp.max(dlog, axis=0, keepdims=True))
        w_inter = jnp.exp(inter_log - m_t)
        qkw = _dot_nt(k_h, q_h) * jnp.exp(dlog - m_t)
        qc = _dot_nt(cn_t.astype(BF16), q_h)
        num = w_inter * qc[:M_DV, :] + _dot(v_t, qkw.astype(BF16))
        den = w_inter * qc[M_DV:M_DV + 1, :] + jnp.sum(qkw, axis=0, keepdims=True)
        hh = num / jnp.maximum(jnp.abs(den), jnp.exp(-m_t))

        mu = jnp.mean(hh, axis=0, keepdims=True)
        dv = hh - mu
        var = jnp.mean(dv * dv, axis=0, keepdims=True)
        hn = (dv * lax.rsqrt(var + LN_EPS)) * jnp.tile(ng_ref[h * M_DV:(h + 1) * M_DV, :], (1, reps))
        out_h = (_sigmoid(mo_ref[:, h * M_DV:(h + 1) * M_DV]) * hn.T).astype(BF16)

        a = g_tot + z
        m_new = jnp.maximum(g_tot + m_prev, jnp.max(a, axis=0, keepdims=True)[:, 0:1])
        decay = jnp.exp(g_tot + m_prev - m_new)
        kw = (k_f * jnp.exp(a - m_new)).astype(BF16)
        v_aug = jnp.concatenate([v_t, ones_rows], axis=0)
        results.append((out_h, decay * cn_t + _dot(v_aug, kw), m_new))
    return results


def _mlstm(mq, mk, mv, mo, gates, norm_g, bsz, s):
    t = mq.shape[0]
    L = MLSTM_CHUNK
    nb = MLSTM_SEQS
    seq = lambda a: a.reshape(bsz, s, a.shape[1])
    blk = lambda width: pl.BlockSpec((nb, L, width), lambda g, n: (g, n, 0))
    const = lambda g, n: (0, 0)
    hm = pl.pallas_call(
        _mlstm_kernel,
        grid=(bsz // nb, s // L),
        in_specs=[
            blk(MQK_W // 2), blk(MQK_W // 2), blk(M_W), blk(M_W), blk(GATE_COLS),
            pl.BlockSpec((M_W, V7X_LANES), const),
        ],
        out_specs=blk(M_W),
        out_shape=jax.ShapeDtypeStruct((bsz, s, M_W), BF16),
        scratch_shapes=[
            pltpu.VMEM((nb, M_HEADS, 2 * M_DV, M_DQK), F32),
            pltpu.VMEM((nb, M_HEADS, V7X_SUBLANES, V7X_LANES), F32),
        ],
        compiler_params=_params(("parallel", "arbitrary")),
        name="mlstm",
    )(seq(mq), seq(mk), seq(mv), seq(mo), seq(gates),
      jnp.broadcast_to(norm_g[:, None], (M_W, V7X_LANES)))
    return hm.reshape(t, M_W)


def _fox_operand(x, c_col, hh, key_side):
    rows = x.shape[0]
    lane = lax.broadcasted_iota(I32, (rows, V7X_LANES), 1)
    own = (lane < F_DH) if hh == 0 else (lane >= F_DH)
    base = F_DH if hh == 0 else 0
    parts = [part.astype(F32) for part in _split3(c_col)]
    if key_side:
        feats = parts + [1.0, 1.0, 1.0]
    else:
        feats = [-1.0, -1.0, -1.0] + parts
    out = jnp.where(own, x, 0.0)
    for n, feat in enumerate(feats):
        out = jnp.where(lane == base + n, feat, out)
    return out.astype(BF16)


def _fox_kernel(q_ref, k_ref, v_ref, ccol_ref, crel_k_ref, crel_q_ref, o_ref, qaug_ref, kaug_ref, vt_ref,
                st_a0, st_a1, st_b0, st_b1, pe_a0, pe_a1, pe_b0, pe_b1, acc0, acc1):
    blk = FOX_K_BLOCK
    tq = FOX_Q_BLOCK
    assert tq == 2 * blk
    strips = tq // V7X_LANES
    st_a_refs, st_b_refs = (st_a0, st_a1), (st_b0, st_b1)
    pe_a_refs, pe_b_refs = (pe_a0, pe_a1), (pe_b0, pe_b1)
    acc_refs = (acc0, acc1)
    p = pl.program_id(1)
    i = pl.program_id(2)

    @pl.when(i == 0)
    def _():
        lane = lax.broadcasted_iota(I32, crel_q_ref.shape, 1)
        q_all = q_ref[...].astype(F32)
        k_all = k_ref[...].astype(F32)
        for hh in range(2):
            mine = lane == FF_LANE + 2 * p + hh
            c_q = jnp.sum(jnp.where(mine, crel_q_ref[...], 0.0), axis=-1, keepdims=True)
            c_k = jnp.sum(jnp.where(mine, crel_k_ref[...], 0.0), axis=-1, keepdims=True)
            qaug_ref[hh] = _fox_operand(q_all, c_q, hh, False)
            kaug_ref[hh] = _fox_operand(k_all, c_k, hh, True)
        v_t = v_ref[...].astype(F32).T
        for j in range(vt_ref.shape[0]):
            vt_ref[j] = v_t[:, j * blk:(j + 1) * blk].astype(BF16)

    q_start = pl.multiple_of(i * tq, tq)
    q_heads = [qaug_ref[hh, pl.ds(q_start, tq), :] for hh in range(2)]
    key_row = lax.broadcasted_iota(I32, (blk, V7X_LANES), 0)
    query_col = lax.broadcasted_iota(I32, (blk, V7X_LANES), 1)
    last_chunk = 2 * i + 1
    head_lane = lax.broadcasted_iota(I32, (1, GATE_COLS), 1) - (FF_LANE + 2 * p)

    def c_before(position, hh):
        row = ccol_ref[pl.ds(jnp.maximum(position - 1, 0), 1), :]
        keep = jnp.logical_and(head_lane == hh, position > 0)
        return jnp.sum(jnp.where(keep, row, 0.0), axis=-1, keepdims=True)

    c_query0 = [c_before(q_start, hh) for hh in range(2)]

    def scores(j, hh):
        start = pl.multiple_of(j * blk, blk)
        return _dot_nt(kaug_ref[hh, pl.ds(start, blk), :], q_heads[hh])

    def softmax_update(st_ref, pe_ref, m_old, l_old, base, key_minus_query=None):
        alphas, ms, ls = [], [], []
        for c in range(strips):
            cols = slice(c * V7X_LANES, (c + 1) * V7X_LANES)
            gap = None if key_minus_query is None else key_minus_query - c * V7X_LANES
            if gap is not None and gap - (V7X_LANES - 1) > 0:
                pe_ref[:, cols] = jnp.zeros((blk, V7X_LANES), BF16)
                alphas.append(jnp.ones((1, V7X_LANES), F32))
                ms.append(m_old[:, cols])
                ls.append(l_old[:, cols])
                continue
            st = st_ref[:, cols]
            if gap is not None and gap + (blk - 1) > 0:
                st = jnp.where(key_row + gap <= query_col, st, -jnp.inf)
            m_new = jnp.maximum(m_old[:, cols], jnp.max(st, axis=0, keepdims=True) + base)
            alpha = jnp.exp(m_old[:, cols] - m_new)
            pe = jnp.exp(st - (m_new - base))
            pe_ref[:, cols] = pe.astype(BF16)
            alphas.append(alpha)
            ms.append(m_new)
            ls.append(alpha * l_old[:, cols] + jnp.sum(pe, axis=0, keepdims=True))
        cat = lambda parts: jnp.concatenate(parts, axis=1)
        return cat(alphas), cat(ms), cat(ls)

    def pair(mi, carry, diagonal=False):
        a = 2 * mi
        b = a + 1
        v_prev = vt_ref[jnp.maximum(a - 1, 0)]
        v_a = vt_ref[a]
        partial = []
        for hh in range(2):
            alpha_prev = carry[hh][0]
            partial.append(alpha_prev * acc_refs[hh][...] + _dot(v_prev, pe_b_refs[hh][...]))
            st_b_refs[hh][...] = scores(b, hh)
        stats = []
        for hh in range(2):
            _, m_old, l_old = carry[hh]
            stats.append(softmax_update(st_a_refs[hh], pe_a_refs[hh], m_old, l_old,
                                        c_query0[hh] - c_before(a * blk, hh), 0 if diagonal else None))
        for hh in range(2):
            alpha_a = stats[hh][0]
            acc_refs[hh][...] = alpha_a * partial[hh] + _dot(v_a, pe_a_refs[hh][...])
            if not diagonal:
                st_a_refs[hh][...] = scores(a + 2, hh)
        return tuple(softmax_update(st_b_refs[hh], pe_b_refs[hh], stats[hh][1], stats[hh][2],
                                    c_query0[hh] - c_before(b * blk, hh), blk if diagonal else None)
                     for hh in range(2))

    for hh in range(2):
        st_a_refs[hh][...] = scores(0, hh)
        pe_b_refs[hh][...] = jnp.zeros((blk, tq), BF16)
        acc_refs[hh][...] = jnp.zeros((V7X_LANES, tq), F32)
    init = tuple((jnp.ones((1, tq), F32), jnp.full((1, tq), -jnp.inf, F32), jnp.zeros((1, tq), F32))
                 for _ in range(2))
    final = pair(i, lax.fori_loop(0, i, pair, init), diagonal=True)
    v_last = vt_ref[last_chunk]
    outs = []
    for hh in range(2):
        alpha, _, l_fin = final[hh]
        outs.append((alpha * acc_refs[hh][...] + _dot(v_last, pe_b_refs[hh][...])) / l_fin)
    row = lax.broadcasted_iota(I32, (V7X_LANES, tq), 0)
    o_t = jnp.where(row < F_DH, outs[0], outs[1])
    o_ref[...] = o_t.T.astype(BF16)


def _fox_attention(fq, fk, fv, ccol, crel_k, crel_q, bsz, s):
    t = fq.shape[0]
    blk = FOX_K_BLOCK
    tq = FOX_Q_BLOCK
    nq = s // tq
    pairs = F_HEADS // 2
    qmap = lambda b, p, i: (b * nq + i, p)
    kvmap = lambda b, p, i: (b, p)
    return pl.pallas_call(
        _fox_kernel,
        grid=(bsz, pairs, nq),
        in_specs=[
            pl.BlockSpec((s, V7X_LANES), kvmap),
            pl.BlockSpec((s, V7X_LANES), kvmap),
            pl.BlockSpec((s, V7X_LANES), kvmap),
            pl.BlockSpec((s, GATE_COLS), lambda b, p, i: (b, 0)),
            pl.BlockSpec((s, GATE_COLS), lambda b, p, i: (b, 0)),
            pl.BlockSpec((s, GATE_COLS), lambda b, p, i: (b, 0)),
        ],
        out_specs=pl.BlockSpec((tq, V7X_LANES), qmap),
        out_shape=jax.ShapeDtypeStruct((t, F_W), BF16),
        scratch_shapes=[
            pltpu.VMEM((2, s, V7X_LANES), BF16),
            pltpu.VMEM((2, s, V7X_LANES), BF16),
            pltpu.VMEM((s // blk, V7X_LANES, blk), BF16),
        ] + [pltpu.VMEM((blk, tq), F32)] * 4 + [pltpu.VMEM((blk, tq), BF16)] * 4
          + [pltpu.VMEM((V7X_LANES, tq), F32)] * 2,
        compiler_params=_params(("parallel", "parallel", "arbitrary")),
        name="fox_attention",
    )(fq, fk, fv, ccol, crel_k, crel_q)


def _layer_norm_rows(r, g, b):
    mu = jnp.mean(r, axis=-1, keepdims=True)
    d = r - mu
    var = jnp.mean(d * d, axis=-1, keepdims=True)
    return (d * lax.rsqrt(var + LN_EPS)) * g + b


def _merge_kernel(dn_alpha, hm_ref, hf_ref, x_ref, wg_ref, bg_ref, wbm_ref, wbf_ref, wo_ref,
                  g_ref, b_ref, wrh_ref, wrl_ref, br_ref, h1_ref, h1p_ref, gate_ref, tope_ref, cnt_ref):
    counts = jnp.zeros((1, V7X_LANES), F32)
    for n in range(x_ref.shape[0] // MERGE_SUB_ROWS):
        rows = slice(n * MERGE_SUB_ROWS, (n + 1) * MERGE_SUB_ROWS)
        counts = counts + _merge_rows(dn_alpha, rows, hm_ref, hf_ref, x_ref, wg_ref, bg_ref, wbm_ref,
                                      wbf_ref, wo_ref, g_ref, b_ref, wrh_ref, wrl_ref, br_ref,
                                      h1_ref, h1p_ref, gate_ref, tope_ref)
    sub = lax.broadcasted_iota(I32, cnt_ref.shape, 0)
    cnt_ref[...] = jnp.where(sub == 0, counts, 0.0)


def _merge_rows(dn_alpha, rows, hm_ref, hf_ref, x_ref, wg_ref, bg_ref, wbm_ref, wbf_ref, wo_ref,
                g_ref, b_ref, wrh_ref, wrl_ref, br_ref, h1_ref, h1p_ref, gate_ref, tope_ref):
    x = x_ref[rows, :]
    d = x.shape[1]
    gmf = _dot(x.astype(BF16), wg_ref[...]) + bg_ref[...]
    ym = _dot(hm_ref[rows, :], wbm_ref[...])
    yf = _dot(hf_ref[rows, :], wbf_ref[...])
    y = _sigmoid(gmf[:, :d]) * ym + _sigmoid(gmf[:, d:]) * yf
    mix = _dot(y.astype(BF16), wo_ref[...])
    h1 = _layer_norm_rows(dn_alpha * x + mix, g_ref[...], b_ref[...])
    h1_ref[rows, :] = h1

    h1p_ref[rows, :] = _pack_bf16_pairs(h1)
    hb = h1.astype(BF16)

    lo = (h1 - hb.astype(F32)).astype(BF16)
    logits = (_dot(lo, wrh_ref[...]) + _dot(hb, wrl_ref[...])) + _dot(hb, wrh_ref[...]) + br_ref[...]
    tm = logits.shape[0]
    lane = lax.broadcasted_iota(I32, (tm, V7X_LANES), 1)
    vals = jnp.where(lane < N_EXPERTS, logits, -jnp.inf)
    top_v, top_i = [], []
    for _ in range(TOP_K):
        mx = jnp.max(vals, axis=-1, keepdims=True)
        idx = jnp.min(jnp.where(vals == mx, lane, V7X_LANES), axis=-1, keepdims=True)
        top_v.append(mx)
        top_i.append(idx)
        vals = jnp.where(lane == idx, -jnp.inf, vals)
    ex = [jnp.exp(v - top_v[0]) for v in top_v]
    tot = ex[0]
    for e in ex[1:]:
        tot = tot + e
    gate = jnp.zeros((tm, V7X_LANES), F32)
    tope = jnp.zeros((tm, V7X_LANES), I32)
    member = jnp.zeros((tm, V7X_LANES), F32)
    for k in range(TOP_K):
        gate = jnp.where(lane == k, ex[k] / tot, gate)
        tope = jnp.where(lane == k, top_i[k], tope)
        member = member + (lane == top_i[k]).astype(F32)
    gate_ref[rows, :] = gate
    tope_ref[rows, :] = tope
    return jnp.sum(member, axis=0, keepdims=True)


def _merge(dn_alpha, hm, hf, x2, wg, bg, wbm, wbf, wo, ln_g, ln_b, wr_hi, wr_lo, br):
    t, d = x2.shape
    tm = MERGE_ROWS
    row = lambda i: (i, 0)
    const = lambda i: (0, 0)
    full = lambda a: pl.BlockSpec(a.shape, const)
    return pl.pallas_call(
        functools.partial(_merge_kernel, dn_alpha),
        grid=(t // tm,),
        in_specs=[
            pl.BlockSpec((tm, M_W), row),
            pl.BlockSpec((tm, F_W), row),
            pl.BlockSpec((tm, d), row),
            full(wg), full(bg), full(wbm), full(wbf), full(wo), full(ln_g), full(ln_b),
            full(wr_hi), full(wr_lo), full(br),
        ],
        out_specs=(
            pl.BlockSpec((tm, d), row),
            pl.BlockSpec((tm, d // 2), row),
            pl.BlockSpec((tm, V7X_LANES), row),
            pl.BlockSpec((tm, V7X_LANES), row),
            pl.BlockSpec((V7X_SUBLANES, V7X_LANES), row),
        ),
        out_shape=(
            jax.ShapeDtypeStruct((t, d), F32),
            jax.ShapeDtypeStruct((t, d // 2), U32),
            jax.ShapeDtypeStruct((t, V7X_LANES), F32),
            jax.ShapeDtypeStruct((t, V7X_LANES), I32),
            jax.ShapeDtypeStruct((t // tm * V7X_SUBLANES, V7X_LANES), F32),
        ),
        compiler_params=_params(("parallel",)),
        name="merge_ln1_router",
    )(hm, hf, x2, wg, bg, wbm, wbf, wo, ln_g, ln_b, wr_hi, wr_lo, br)


def _lane_cumsum(x):
    lane = lax.broadcasted_iota(I32, x.shape, 1)
    d = 1
    while d < V7X_LANES:
        x = x + jnp.where(lane >= d, pltpu.roll(x, shift=d, axis=1), 0.0)
        d *= 2
    return x


def _routing_kernel(cnt_ref, tope_ref, dest_ref, table_ref, run_ref, start_ref):
    sb = ROUTE_SUB_ROWS

    @pl.when(pl.program_id(0) == 0)
    def _():
        total = jnp.sum(cnt_ref[...], axis=0, keepdims=True)
        counts = jnp.broadcast_to(total, (V7X_SUBLANES, V7X_LANES))
        padded = jnp.ceil(counts * (1.0 / MOE_BLOCK)) * MOE_BLOCK
        pad_end = _lane_cumsum(padded)
        pad_start = pad_end - padded
        start_ref[...] = pad_start
        run_ref[...] = jnp.zeros(run_ref.shape, F32)
        nb = table_ref.shape[0]
        blk = lax.broadcasted_iota(I32, (nb, V7X_LANES), 0).astype(F32) * MOE_BLOCK
        ln = lax.broadcasted_iota(I32, (nb, V7X_LANES), 1)
        done = jnp.logical_and(pad_end[0:1, :] <= blk, ln < N_EXPERTS)
        be = jnp.minimum(jnp.sum(done.astype(F32), axis=-1, keepdims=True), N_EXPERTS - 1.0)
        onehot = ln == be.astype(I32)
        cnt_e = jnp.sum(jnp.where(onehot, counts[0:1, :], 0.0), axis=-1, keepdims=True)
        start_e = jnp.sum(jnp.where(onehot, pad_start[0:1, :], 0.0), axis=-1, keepdims=True)
        valid = jnp.clip(cnt_e - (blk[:, 0:1] - start_e), 0.0, float(MOE_BLOCK))
        table_ref[...] = jnp.where(ln == 0, be.astype(I32),
                                   jnp.where(ln == 1, valid.astype(I32), 0))

    earlier = _tril_mask(sb, strict=True).astype(BF16)
    lane = lax.broadcasted_iota(I32, (sb, V7X_LANES), 1)
    for j in range(tope_ref.shape[0] // sb):
        tope = tope_ref[j * sb:(j + 1) * sb, :]
        hit = [lane == tope[:, k:k + 1] for k in range(TOP_K)]
        member = jnp.zeros((sb, V7X_LANES), F32)
        for k in range(TOP_K):
            member = member + hit[k].astype(F32)
        base = _dot(earlier, member.astype(BF16)) + (run_ref[0:1, :] + start_ref[0:1, :])
        dest = jnp.zeros((sb, V7X_LANES), I32)
        for k in range(TOP_K):
            dk = jnp.sum(jnp.where(hit[k], base, 0.0), axis=-1, keepdims=True)
            dest = jnp.where(lane == k, dk.astype(I32), dest)
        dest_ref[j * sb:(j + 1) * sb, :] = dest
        run_ref[...] = run_ref[...] + jnp.sum(member, axis=0, keepdims=True)


def _routing(tile_counts, tope, n_blocks):
    t = tope.shape[0]
    tr = ROUTE_ROWS
    return pl.pallas_call(
        _routing_kernel,
        grid=(t // tr,),
        in_specs=[pl.BlockSpec(tile_counts.shape, lambda i: (0, 0)),
                  pl.BlockSpec((tr, V7X_LANES), lambda i: (i, 0))],
        out_specs=(
            pl.BlockSpec((tr, V7X_LANES), lambda i: (i, 0)),
            pl.BlockSpec((n_blocks, V7X_LANES), lambda i: (0, 0)),
        ),
        out_shape=(
            jax.ShapeDtypeStruct((t, V7X_LANES), I32),
            jax.ShapeDtypeStruct((n_blocks, V7X_LANES), I32),
        ),
        scratch_shapes=[
            pltpu.VMEM((V7X_SUBLANES, V7X_LANES), F32),
            pltpu.VMEM((V7X_SUBLANES, V7X_LANES), F32),
        ],
        compiler_params=_params(("arbitrary",)),
        name="routing",
    )(tile_counts, tope)


def _sc_worker_id():
    return lax.axis_index("s") * V7X_SC_CORES + lax.axis_index("c")


def _sc_mesh():
    return plsc.VectorSubcoreMesh(core_axis_name="c", subcore_axis_name="s",
                                  num_cores=V7X_SC_CORES, num_subcores=V7X_SC_SUBCORES)


def _sc_dispatch(dest_km, h1p, n_rows):
    t, w = h1p.shape
    per_worker = t // V7X_SC_WORKERS
    ch = SC_ROWS_PER_DMA

    @functools.partial(
        pl.kernel, mesh=_sc_mesh(),
        out_type=jax.ShapeDtypeStruct((n_rows, w), h1p.dtype),
        scratch_types=[pltpu.VMEM((ch,), I32), pltpu.VMEM((ch, w), h1p.dtype), pltpu.SemaphoreType.DMA],
        name="sc_dispatch",
    )
    def scatter_rows(dest_hbm, h1p_hbm, xs_hbm, idx_v, rows_v, sem):
        first = _sc_worker_id() * per_worker

        @pl.loop(0, per_worker // ch)
        def _(j):
            base = first + j * ch
            pltpu.sync_copy(h1p_hbm.at[pl.ds(base, ch)], rows_v)
            for k in range(TOP_K):
                pltpu.sync_copy(dest_hbm.at[pl.ds(k * t + base, ch)], idx_v)
                pltpu.async_copy(rows_v, xs_hbm.at[idx_v], sem).wait()

    return scatter_rows(dest_km, h1p)


def _sc_gather(dest_km, y_rows):
    n = dest_km.shape[0]
    w = y_rows.shape[1]
    per_worker = n // V7X_SC_WORKERS
    ch = SC_ROWS_PER_DMA

    @functools.partial(
        pl.kernel, mesh=_sc_mesh(),
        out_type=jax.ShapeDtypeStruct((n, w), y_rows.dtype),
        scratch_types=[pltpu.VMEM((ch,), I32), pltpu.VMEM((ch, w), y_rows.dtype), pltpu.SemaphoreType.DMA],
        name="sc_gather",
    )
    def gather_rows(dest_hbm, y_hbm, out_hbm, idx_v, rows_v, sem):
        first = _sc_worker_id() * per_worker

        @pl.loop(0, per_worker // ch)
        def _(j):
            base = first + j * ch
            pltpu.sync_copy(dest_hbm.at[pl.ds(base, ch)], idx_v)
            pltpu.async_copy(y_hbm.at[idx_v], rows_v, sem).wait()
            pltpu.sync_copy(rows_v, out_hbm.at[pl.ds(base, ch)])

    return gather_rows(dest_km, y_rows)


def _expert_kernel(be_ref, nv_ref, xs_ref, wgu_ref, bgu_ref, wdn_ref, bdn_ref, y_ref):
    i = pl.program_id(0)
    nv = nv_ref[i]

    @pl.when(nv == 0)
    def _():
        y_ref[...] = jnp.zeros(y_ref.shape, U32)

    @pl.when(nv > 0)
    def _():
        x = _unpack_bf16_pairs(xs_ref[...])
        rowid = lax.broadcasted_iota(I32, x.shape, 0)
        x = jnp.where(rowid < nv, x, 0.0).astype(BF16)
        gu = _dot(x, wgu_ref[...]) + bgu_ref[...]
        f = gu.shape[1] // 2
        glu = jnp.minimum(gu[:, :f], SWIGLU_LIMIT)
        lin = jnp.clip(gu[:, f:], -SWIGLU_LIMIT, SWIGLU_LIMIT)
        act = glu * _sigmoid(SWIGLU_ALPHA * glu) * (lin + 1.0)
        y_ref[...] = _pack_bf16_pairs(_dot(act.astype(BF16), wdn_ref[...]) + bdn_ref[...])


def _experts(block_e, block_valid, xs, wgu, bgu, wdn, bdn):
    n_rows, w = xs.shape
    e, d, f2 = wgu.shape
    n_blocks = n_rows // MOE_BLOCK
    grid_spec = pltpu.PrefetchScalarGridSpec(
        num_scalar_prefetch=2,
        grid=(n_blocks,),
        in_specs=[
            pl.BlockSpec((MOE_BLOCK, w), lambda i, be, nv: (i, 0)),
            pl.BlockSpec((None, d, f2), lambda i, be, nv: (be[i], 0, 0)),
            pl.BlockSpec((None, 1, f2), lambda i, be, nv: (be[i], 0, 0)),
            pl.BlockSpec((None, f2 // 2, d), lambda i, be, nv: (be[i], 0, 0)),
            pl.BlockSpec((None, 1, d), lambda i, be, nv: (be[i], 0, 0)),
        ],
        out_specs=pl.BlockSpec((MOE_BLOCK, d // 2), lambda i, be, nv: (i, 0)),
    )
    return pl.pallas_call(
        _expert_kernel,
        grid_spec=grid_spec,
        out_shape=jax.ShapeDtypeStruct((n_rows, d // 2), U32),
        compiler_params=_params(("arbitrary",)),
        name="experts",
    )(block_e, block_valid, xs, wgu, bgu, wdn, bdn)


def _combine_kernel(dn_alpha, h1_ref, gate_ref, g_ref, b_ref, yg_ref, o_ref):
    gate = gate_ref[...]
    ffn = gate[:, 0:1] * _unpack_bf16_pairs(yg_ref[0])
    for k in range(1, TOP_K):
        ffn = ffn + gate[:, k:k + 1] * _unpack_bf16_pairs(yg_ref[k])
    o_ref[...] = _layer_norm_rows(dn_alpha * h1_ref[...] + ffn, g_ref[...], b_ref[...])


def _combine(dn_alpha, h1, gate, ln_g, ln_b, yg):
    t, d = h1.shape
    tc = COMBINE_ROWS
    row = lambda i: (i, 0)
    const = lambda i: (0, 0)
    return pl.pallas_call(
        functools.partial(_combine_kernel, dn_alpha),
        grid=(t // tc,),
        in_specs=[
            pl.BlockSpec((tc, d), row),
            pl.BlockSpec((tc, V7X_LANES), row),
            pl.BlockSpec((1, d), const),
            pl.BlockSpec((1, d), const),
            pl.BlockSpec((TOP_K, tc, d // 2), lambda i: (0, i, 0)),
        ],
        out_specs=pl.BlockSpec((tc, d), row),
        out_shape=jax.ShapeDtypeStruct((t, d), F32),
        compiler_params=_params(("parallel",)),
        name="combine_ln2",
    )(h1, gate, ln_g, ln_b, yg)


def _pack_in_proj(w_in, b_in):
    d = w_in.shape[0]
    o = 0
    cols = {}
    for name, width in (("mqk", MQK_W), ("mv", M_W), ("mo", M_W), ("mi", M_HEADS), ("mf", M_HEADS),
                        ("fq", F_W), ("fk", F_W), ("fv", F_W), ("ff", F_HEADS), ("gm", d), ("gf", d)):
        cols[name] = (o, o + width)
        o += width

    def take(a, names):
        return [a[..., cols[n][0]:cols[n][1]] for n in names]

    n_gate = 2 * M_HEADS + F_HEADS
    main = ("mqk", "mv", "mo", "fq", "fk", "fv", "mi", "mf", "ff")
    w_main = jnp.concatenate(take(w_in, main) + [jnp.zeros((d, GATE_COLS - n_gate), w_in.dtype)], axis=1)
    b_main = jnp.concatenate(take(b_in, main) + [jnp.zeros((GATE_COLS - n_gate,), b_in.dtype)])
    w_gate = jnp.concatenate(take(w_in, ("gm", "gf")), axis=1)
    b_gate = jnp.concatenate(take(b_in, ("gm", "gf")))
    return w_main.astype(BF16), b_main[None, :], w_gate.astype(BF16), b_gate[None, :]


def _layer(h, depth, w_in, b_in, m_conv_w, m_conv_b, m_norm_g, w_bm, w_bf, w_o, ln1_g, ln1_b,
           w_router, b_router, w_gu, b_gu, w_dn, b_dn, ln2_g, ln2_b):
    bsz, s, d = h.shape
    t = bsz * s
    dn_alpha = (2.0 * depth) ** 0.25
    x2 = h.reshape(t, d)

    w_main, b_main, w_gate, b_gate = _pack_in_proj(w_in, b_in)
    mq, mk, mv, mo, fq, fk, fv, gates = _inproj(x2, w_main, b_main, m_conv_w, m_conv_b[None, :], s)
    ccol, crel_k, crel_q = _fox_cumsum(gates, bsz, s)
    hm = _mlstm(mq, mk, mv, mo, gates, m_norm_g, bsz, s)
    hf = _fox_attention(fq, fk, fv, ccol, crel_k, crel_q, bsz, s)

    n_exp = w_router.shape[1]
    wr = jnp.zeros((d, V7X_LANES), F32).at[:, :n_exp].set(w_router)
    wr_hi = wr.astype(BF16)
    wr_lo = (wr - wr_hi.astype(F32)).astype(BF16)
    br = jnp.zeros((1, V7X_LANES), F32).at[0, :n_exp].set(b_router)
    h1, h1p, gate, tope, tile_counts = _merge(
        dn_alpha, hm, hf, x2, w_gate, b_gate, w_bm.astype(BF16), w_bf.astype(BF16), w_o.astype(BF16),
        ln1_g[None, :], ln1_b[None, :], wr_hi, wr_lo, br)

    n_blocks = -(-(t * TOP_K) // MOE_BLOCK) + N_EXPERTS
    dest, table = _routing(tile_counts, tope, n_blocks)
    dest_km = dest[:, :TOP_K].T.reshape(TOP_K * t)
    block_e, block_valid = table[:, 0], table[:, 1]
    xs = _sc_dispatch(dest_km, h1p, n_blocks * MOE_BLOCK)
    y_rows = _experts(block_e, block_valid, xs, w_gu.astype(BF16), b_gu[:, None, :],
                      w_dn.astype(BF16), b_dn[:, None, :])
    yg = _sc_gather(dest_km, y_rows).reshape(TOP_K, t, d // 2)
    out = _combine(dn_alpha, h1, gate, ln2_g[None, :], ln2_b[None, :], yg)
    return out.reshape(bsz, s, d)


def kernel(x, w_in, b_in, m_conv_w, m_conv_b, m_norm_g, w_bm, w_bf, w_o, ln1_g, ln1_b,
           w_router, b_router, w_gu, b_gu, w_dn, b_dn, ln2_g, ln2_b):
    depth = w_in.shape[0]
    h = x
    for l in range(depth):
        h = _layer(h, depth, w_in[l], b_in[l], m_conv_w[l], m_conv_b[l], m_norm_g[l], w_bm[l], w_bf[l],
                   w_o[l], ln1_g[l], ln1_b[l], w_router[l], b_router[l], w_gu[l], b_gu[l], w_dn[l],
                   b_dn[l], ln2_g[l], ln2_b[l])
    return h
```

```python
import functools
import math

import jax
import jax.numpy as jnp
from jax import lax
from jax.experimental import pallas as pl
from jax.experimental.pallas import tpu as pltpu
from jax.experimental.pallas import tpu_sc as plsc

F32 = jnp.float32
BF16 = jnp.bfloat16
I32 = jnp.int32
U32 = jnp.uint32

M_HEADS = 4
M_DQK = 128
M_DV = 128
CONV_W = 4
F_HEADS = 8
F_DH = 64
N_EXPERTS = 32
TOP_K = 4
SWIGLU_ALPHA = 1.702
SWIGLU_LIMIT = 7.0
LN_EPS = 1e-5

M_W = M_HEADS * M_DV
F_W = F_HEADS * F_DH
MQK_W = 2 * M_HEADS * M_DQK

V7X_LANES = 128
V7X_SUBLANES = 8
V7X_VMEM_BYTES = 64 * 1024 * 1024
VMEM_LIMIT_BYTES = (V7X_VMEM_BYTES * 3) // 4
V7X_SC_CORES = 2
V7X_SC_SUBCORES = 16
V7X_SC_WORKERS = V7X_SC_CORES * V7X_SC_SUBCORES

INPROJ_ROWS = 512
MLSTM_CHUNK = 256
MLSTM_SEQS = 2
FOX_Q_BLOCK = 512
FOX_K_BLOCK = 256
MERGE_ROWS = 512
MERGE_SUB_ROWS = 256
ROUTE_ROWS = 1024
ROUTE_SUB_ROWS = 256
MOE_BLOCK = 512
SC_ROWS_PER_DMA = 128
COMBINE_ROWS = 512
COMBINE_PARTS = 2

GATE_COLS = V7X_LANES
MI_LANE = 0
MF_LANE = M_HEADS
FF_LANE = 2 * M_HEADS


def _params(semantics):
    return pltpu.CompilerParams(dimension_semantics=semantics, vmem_limit_bytes=VMEM_LIMIT_BYTES)


def _log_sigmoid(x):
    return jnp.minimum(x, 0.0) - jnp.log1p(jnp.exp(-jnp.abs(x)))


def _sigmoid(x):
    return 1.0 / (1.0 + jnp.exp(-x))


def _dot(a, b):
    return jnp.dot(a, b, preferred_element_type=F32)


def _dot_nt(a, b):
    return lax.dot_general(a, b, (((1,), (1,)), ((), ())), preferred_element_type=F32)


def _dot_tn(a, b):
    return lax.dot_general(a, b, (((0,), (0,)), ((), ())), preferred_element_type=F32)


def _split3(x):
    hi = x.astype(BF16)
    r1 = x - hi.astype(F32)
    mid = r1.astype(BF16)
    lo = (r1 - mid.astype(F32)).astype(BF16)
    return hi, mid, lo


def _dot_mask_f32(mask_bf16, x):
    hi, mid, lo = _split3(x)
    return (_dot(mask_bf16, lo) + _dot(mask_bf16, mid)) + _dot(mask_bf16, hi)


def _pack_bf16_pairs(x):
    half = x.shape[1] // 2
    bits = lax.bitcast_convert_type(x.astype(BF16).astype(F32), U32)
    return (bits[:, :half] >> 16) | bits[:, half:]


def _unpack_bf16_pairs(words):
    lo = lax.bitcast_convert_type(words << 16, F32)
    hi = lax.bitcast_convert_type(words & jnp.uint32(0xFFFF0000), F32)
    return jnp.concatenate([lo, hi], axis=1)


def _tril_mask(n, strict=False):
    r = lax.broadcasted_iota(I32, (n, n), 0)
    c = lax.broadcasted_iota(I32, (n, n), 1)
    return (r > c) if strict else (r >= c)


_OFF_MQK = 0
_OFF_MV = _OFF_MQK + MQK_W
_OFF_MO = _OFF_MV + M_W
_OFF_FQ = _OFF_MO + M_W
_OFF_FK = _OFF_FQ + F_W
_OFF_FV = _OFF_FK + F_W
_OFF_GATES = _OFF_FV + F_W
_PACKED_COLS = _OFF_GATES + GATE_COLS


def _inproj_kernel(tiles_per_seq, x_ref, w_ref, b_ref, cw_ref, cb_ref,
                   mq_ref, mk_ref, mv_ref, mo_ref, fq_ref, fk_ref, fv_ref, gates_ref, ext_ref):
    tm = x_ref.shape[0]
    pad = V7X_SUBLANES
    xb = x_ref[...].astype(BF16)

    def seg(lo, width):
        return _dot(xb, w_ref[:, lo:lo + width]) + b_ref[:, lo:lo + width]

    @pl.when(pl.program_id(0) % tiles_per_seq == 0)
    def _():
        ext_ref[0:pad, :] = jnp.zeros((pad, MQK_W), F32)

    ext_ref[pad:pad + tm, :] = seg(_OFF_MQK, MQK_W)
    y = cb_ref[...] + cw_ref[CONV_W - 1:CONV_W, :] * ext_ref[pad:pad + tm, :]
    for k in range(CONV_W - 1):
        shift = CONV_W - 1 - k
        y = y + cw_ref[k:k + 1, :] * ext_ref[pad - shift:pad - shift + tm, :]
    ext_ref[0:pad, :] = ext_ref[tm:tm + pad, :]
    act = y * _sigmoid(y)
    mq_ref[...] = act[:, :MQK_W // 2].astype(BF16)
    mk_ref[...] = act[:, MQK_W // 2:] * (M_DQK ** -0.5)
    mv_ref[...] = seg(_OFF_MV, M_W).astype(BF16)
    mo_ref[...] = seg(_OFF_MO, M_W)
    fq_ref[...] = (seg(_OFF_FQ, F_W) * (F_DH ** -0.5)).astype(BF16)
    fk_ref[...] = seg(_OFF_FK, F_W).astype(BF16)
    fv_ref[...] = seg(_OFF_FV, F_W).astype(BF16)
    gates_ref[...] = seg(_OFF_GATES, GATE_COLS)


def _inproj(x2, w_packed, b_packed, conv_w, conv_b, s):
    t, d = x2.shape
    tm = INPROJ_ROWS
    row = lambda i: (i, 0)
    const = lambda i: (0, 0)
    out_shapes = (
        jax.ShapeDtypeStruct((t, MQK_W // 2), BF16),
        jax.ShapeDtypeStruct((t, MQK_W // 2), F32),
        jax.ShapeDtypeStruct((t, M_W), BF16),
        jax.ShapeDtypeStruct((t, M_W), F32),
        jax.ShapeDtypeStruct((t, F_W), BF16),
        jax.ShapeDtypeStruct((t, F_W), BF16),
        jax.ShapeDtypeStruct((t, F_W), BF16),
        jax.ShapeDtypeStruct((t, GATE_COLS), F32),
    )
    return pl.pallas_call(
        functools.partial(_inproj_kernel, s // tm),
        grid=(t // tm,),
        in_specs=[
            pl.BlockSpec((tm, d), row),
            pl.BlockSpec((d, _PACKED_COLS), const),
            pl.BlockSpec((1, _PACKED_COLS), const),
            pl.BlockSpec((CONV_W, MQK_W), const),
            pl.BlockSpec((1, MQK_W), const),
        ],
        out_specs=tuple(pl.BlockSpec((tm, o.shape[1]), row) for o in out_shapes),
        out_shape=out_shapes,
        scratch_shapes=[pltpu.VMEM((tm + V7X_SUBLANES, MQK_W), F32)],
        compiler_params=_params(("arbitrary",)),
        name="inproj",
    )(x2, w_packed, b_packed, conv_w, conv_b)


def _fox_cumsum_kernel(g_ref, ccol_ref, crel_k_ref, crel_q_ref):
    s = g_ref.shape[0]
    cb = FOX_K_BLOCK
    per_q = FOX_Q_BLOCK // cb
    tri = _tril_mask(cb).astype(BF16)
    carry = jnp.zeros((1, GATE_COLS), F32)
    for j in range(s // cb):
        rows = slice(j * cb, (j + 1) * cb)
        if j % per_q == 0:
            q_carry = carry
        within = _dot_mask_f32(tri, _log_sigmoid(g_ref[rows, :]))
        crel_k_ref[rows, :] = within
        crel_q_ref[rows, :] = within + (carry - q_carry)
        ccol_ref[rows, :] = within + carry
        carry = carry + within[cb - 1:cb, :]


def _fox_cumsum(gates, bsz, s):
    t = gates.shape[0]
    spec = pl.BlockSpec((s, GATE_COLS), lambda b: (b, 0))
    shape = jax.ShapeDtypeStruct((t, GATE_COLS), F32)
    return pl.pallas_call(
        _fox_cumsum_kernel,
        grid=(bsz,),
        in_specs=[spec],
        out_specs=(spec, spec, spec),
        out_shape=(shape, shape, shape),
        compiler_params=_params(("parallel",)),
        name="fox_cumsum",
    )(gates)


def _mlstm_kernel(mq_ref, mk_ref, mv_ref, mo_ref, gates_ref, ng_ref, hm_ref, state_ref, m_ref):
    @pl.when(pl.program_id(1) == 0)
    def _():
        state_ref[...] = jnp.zeros(state_ref.shape, F32)
        m_ref[...] = jnp.zeros(m_ref.shape, F32)

    seqs = range(mq_ref.shape[0])
    states = [[state_ref[bb, h] for h in range(M_HEADS)] for bb in seqs]
    maxes = [[m_ref[bb, h][0:1, 0:1] for h in range(M_HEADS)] for bb in seqs]
    results = [_mlstm_chunk(mq_ref.at[bb], mk_ref.at[bb], mv_ref.at[bb], mo_ref.at[bb], gates_ref.at[bb],
                            ng_ref, states[bb], maxes[bb]) for bb in seqs]
    for bb in seqs:
        for h, (out_h, state_h, m_h) in enumerate(results[bb]):
            hm_ref[bb, :, h * M_DV:(h + 1) * M_DV] = out_h
            state_ref[bb, h] = state_h
            m_ref[bb, h] = jnp.broadcast_to(m_h, m_ref.shape[2:])


def _mlstm_chunk(mq_ref, mk_ref, mv_ref, mo_ref, gates_ref, ng_ref, states, maxes):
    L = MLSTM_CHUNK
    reps = L // V7X_LANES
    results = []
    gates = gates_ref[...]
    bfull = _dot_mask_f32(_tril_mask(L).astype(BF16), _log_sigmoid(gates))
    b_rows = bfull.T
    z_all = gates - pltpu.roll(bfull, shift=V7X_LANES - (MF_LANE - MI_LANE), axis=1)
    visible = (lax.broadcasted_iota(I32, (L, L), 0) <= lax.broadcasted_iota(I32, (L, L), 1))
    ones_rows = (lax.broadcasted_iota(I32, (M_DV, L), 0) == 0).astype(BF16)

    for h in range(M_HEADS):
        b_row = b_rows[MF_LANE + h:MF_LANE + h + 1, :]
        g_tot = b_row[:, L - 1:L]
        m_prev = maxes[h]
        z = jnp.broadcast_to(z_all[:, MI_LANE + h:MI_LANE + h + 1], (L, V7X_LANES))

        q_h = mq_ref[:, h * M_DQK:(h + 1) * M_DQK]
        k_f = mk_ref[:, h * M_DQK:(h + 1) * M_DQK]
        k_h = k_f.astype(BF16)
        v_t = mv_ref[:, h * M_DV:(h + 1) * M_DV].astype(F32).T.astype(BF16)
        cn_t = states[h]

        dlog = jnp.where(visible, b_row + jnp.tile(z, (1, reps)), -jnp.inf)
        inter_log = b_row + m_prev
        m_t = jnp.maximum(inter_log, jnp.max(dlog, axis=0, keepdims=True))
        w_inter = jnp.exp(inter_log - m_t)
        qkw = _dot_nt(k_h, q_h) * jnp.exp(dlog - m_t)
        qc = _dot_nt(cn_t.astype(BF16), q_h)
        num = w_inter * qc[:M_DV, :] + _dot(v_t, qkw.astype(BF16))
        den = w_inter * qc[M_DV:M_DV + 1, :] + jnp.sum(qkw, axis=0, keepdims=True)
        hh = num / jnp.maximum(jnp.abs(den), jnp.exp(-m_t))

        mu = jnp.mean(hh, axis=0, keepdims=True)
        dv = hh - mu
        var = jnp.mean(dv * dv, axis=0, keepdims=True)
        hn = (dv * lax.rsqrt(var + LN_EPS)) * jnp.tile(ng_ref[h * M_DV:(h + 1) * M_DV, :], (1, reps))
        out_h = (_sigmoid(mo_ref[:, h * M_DV:(h + 1) * M_DV]) * hn.T).astype(BF16)

        a = g_tot + z
        m_new = jnp.maximum(g_tot + m_prev, jnp.max(a, axis=0, keepdims=True)[:, 0:1])
        decay = jnp.exp(g_tot + m_prev - m_new)
        kw = (k_f * jnp.exp(a - m_new)).astype(BF16)
        v_aug = jnp.concatenate([v_t, ones_rows], axis=0)
        results.append((out_h, decay * cn_t + _dot(v_aug, kw), m_new))
    return results


def _mlstm(mq, mk, mv, mo, gates, norm_g, bsz, s):
    t = mq.shape[0]
    L = MLSTM_CHUNK
    nb = MLSTM_SEQS
    seq = lambda a: a.reshape(bsz, s, a.shape[1])
    blk = lambda width: pl.BlockSpec((nb, L, width), lambda g, n: (g, n, 0))
    const = lambda g, n: (0, 0)
    hm = pl.pallas_call(
        _mlstm_kernel,
        grid=(bsz // nb, s // L),
        in_specs=[
            blk(MQK_W // 2), blk(MQK_W // 2), blk(M_W), blk(M_W), blk(GATE_COLS),
            pl.BlockSpec((M_W, V7X_LANES), const),
        ],
        out_specs=blk(M_W),
        out_shape=jax.ShapeDtypeStruct((bsz, s, M_W), BF16),
        scratch_shapes=[
            pltpu.VMEM((nb, M_HEADS, 2 * M_DV, M_DQK), F32),
            pltpu.VMEM((nb, M_HEADS, V7X_SUBLANES, V7X_LANES), F32),
        ],
        compiler_params=_params(("parallel", "arbitrary")),
        name="mlstm",
    )(seq(mq), seq(mk), seq(mv), seq(mo), seq(gates),
      jnp.broadcast_to(norm_g[:, None], (M_W, V7X_LANES)))
    return hm.reshape(t, M_W)


def _fox_operand(x, c_col, hh, key_side):
    rows = x.shape[0]
    lane = lax.broadcasted_iota(I32, (rows, V7X_LANES), 1)
    own = (lane < F_DH) if hh == 0 else (lane >= F_DH)
    base = F_DH if hh == 0 else 0
    parts = [part.astype(F32) for part in _split3(c_col)]
    if key_side:
        feats = parts + [1.0, 1.0, 1.0]
    else:
        feats = [-1.0, -1.0, -1.0] + parts
    out = jnp.where(own, x, 0.0)
    for n, feat in enumerate(feats):
        out = jnp.where(lane == base + n, feat, out)
    return out.astype(BF16)


def _fox_kernel(q_ref, k_ref, v_ref, ccol_ref, crel_k_ref, crel_q_ref, o_ref, qaug_ref, kaug_ref, vt_ref,
                st_a0, st_a1, st_b0, st_b1, pe_a0, pe_a1, pe_b0, pe_b1, acc0, acc1):
    blk = FOX_K_BLOCK
    tq = FOX_Q_BLOCK
    assert tq == 2 * blk
    strips = tq // V7X_LANES
    st_a_refs, st_b_refs = (st_a0, st_a1), (st_b0, st_b1)
    pe_a_refs, pe_b_refs = (pe_a0, pe_a1), (pe_b0, pe_b1)
    acc_refs = (acc0, acc1)
    p = pl.program_id(1)
    i = pl.program_id(2)

    @pl.when(i == 0)
    def _():
        lane = lax.broadcasted_iota(I32, crel_q_ref.shape, 1)
        q_all = q_ref[...].astype(F32)
        k_all = k_ref[...].astype(F32)
        for hh in range(2):
            mine = lane == FF_LANE + 2 * p + hh
            c_q = jnp.sum(jnp.where(mine, crel_q_ref[...], 0.0), axis=-1, keepdims=True)
            c_k = jnp.sum(jnp.where(mine, crel_k_ref[...], 0.0), axis=-1, keepdims=True)
            qaug_ref[hh] = _fox_operand(q_all, c_q, hh, False)
            kaug_ref[hh] = _fox_operand(k_all, c_k, hh, True)
        v_t = v_ref[...].astype(F32).T
        for j in range(vt_ref.shape[0]):
            vt_ref[j] = v_t[:, j * blk:(j + 1) * blk].astype(BF16)

    q_start = pl.multiple_of(i * tq, tq)
    q_heads = [qaug_ref[hh, pl.ds(q_start, tq), :] for hh in range(2)]
    key_row = lax.broadcasted_iota(I32, (blk, V7X_LANES), 0)
    query_col = lax.broadcasted_iota(I32, (blk, V7X_LANES), 1)
    last_chunk = 2 * i + 1
    head_lane = lax.broadcasted_iota(I32, (1, GATE_COLS), 1) - (FF_LANE + 2 * p)

    def c_before(position, hh):
        row = ccol_ref[pl.ds(jnp.maximum(position - 1, 0), 1), :]
        keep = jnp.logical_and(head_lane == hh, position > 0)
        return jnp.sum(jnp.where(keep, row, 0.0), axis=-1, keepdims=True)

    c_query0 = [c_before(q_start, hh) for hh in range(2)]

    def scores(j, hh):
        start = pl.multiple_of(j * blk, blk)
        return _dot_nt(kaug_ref[hh, pl.ds(start, blk), :], q_heads[hh])

    def softmax_update(st_ref, pe_ref, m_old, l_old, base, key_minus_query=None):
        alphas, ms, ls = [], [], []
        for c in range(strips):
            cols = slice(c * V7X_LANES, (c + 1) * V7X_LANES)
            gap = None if key_minus_query is None else key_minus_query - c * V7X_LANES
            if gap is not None and gap - (V7X_LANES - 1) > 0:
                pe_ref[:, cols] = jnp.zeros((blk, V7X_LANES), BF16)
                alphas.append(jnp.ones((1, V7X_LANES), F32))
                ms.append(m_old[:, cols])
                ls.append(l_old[:, cols])
                continue
            st = st_ref[:, cols]
            if gap is not None and gap + (blk - 1) > 0:
                st = jnp.where(key_row + gap <= query_col, st, -jnp.inf)
            m_new = jnp.maximum(m_old[:, cols], jnp.max(st, axis=0, keepdims=True) + base)
            alpha = jnp.exp(m_old[:, cols] - m_new)
            pe = jnp.exp(st - (m_new - base))
            pe_ref[:, cols] = pe.astype(BF16)
            alphas.append(alpha)
            ms.append(m_new)
            ls.append(alpha * l_old[:, cols] + jnp.sum(pe, axis=0, keepdims=True))
        cat = lambda parts: jnp.concatenate(parts, axis=1)
        return cat(alphas), cat(ms), cat(ls)

    def pair(mi, carry, diagonal=False):
        a = 2 * mi
        b = a + 1
        v_prev = vt_ref[jnp.maximum(a - 1, 0)]
        v_a = vt_ref[a]
        partial = []
        for hh in range(2):
            alpha_prev = carry[hh][0]
            partial.append(alpha_prev * acc_refs[hh][...] + _dot(v_prev, pe_b_refs[hh][...]))
            st_b_refs[hh][...] = scores(b, hh)
        stats = []
        for hh in range(2):
            _, m_old, l_old = carry[hh]
            stats.append(softmax_update(st_a_refs[hh], pe_a_refs[hh], m_old, l_old,
                                        c_query0[hh] - c_before(a * blk, hh), 0 if diagonal else None))
        for hh in range(2):
            alpha_a = stats[hh][0]
            acc_refs[hh][...] = alpha_a * partial[hh] + _dot(v_a, pe_a_refs[hh][...])
            if not diagonal:
                st_a_refs[hh][...] = scores(a + 2, hh)
        return tuple(softmax_update(st_b_refs[hh], pe_b_refs[hh], stats[hh][1], stats[hh][2],
                                    c_query0[hh] - c_before(b * blk, hh), blk if diagonal else None)
                     for hh in range(2))

    for hh in range(2):
        st_a_refs[hh][...] = scores(0, hh)
        pe_b_refs[hh][...] = jnp.zeros((blk, tq), BF16)
        acc_refs[hh][...] = jnp.zeros((V7X_LANES, tq), F32)
    init = tuple((jnp.ones((1, tq), F32), jnp.full((1, tq), -jnp.inf, F32), jnp.zeros((1, tq), F32))
                 for _ in range(2))
    final = pair(i, lax.fori_loop(0, i, pair, init), diagonal=True)
    v_last = vt_ref[last_chunk]
    outs = []
    for hh in range(2):
        alpha, _, l_fin = final[hh]
        outs.append((alpha * acc_refs[hh][...] + _dot(v_last, pe_b_refs[hh][...])) / l_fin)
    row = lax.broadcasted_iota(I32, (V7X_LANES, tq), 0)
    o_t = jnp.where(row < F_DH, outs[0], outs[1])
    o_ref[...] = o_t.T.astype(BF16)


def _fox_attention(fq, fk, fv, ccol, crel_k, crel_q, bsz, s):
    t = fq.shape[0]
    blk = FOX_K_BLOCK
    tq = FOX_Q_BLOCK
    nq = s // tq
    pairs = F_HEADS // 2
    qmap = lambda b, p, i: (b * nq + i, p)
    kvmap = lambda b, p, i: (b, p)
    return pl.pallas_call(
        _fox_kernel,
        grid=(bsz, pairs, nq),
        in_specs=[
            pl.BlockSpec((s, V7X_LANES), kvmap),
            pl.BlockSpec((s, V7X_LANES), kvmap),
            pl.BlockSpec((s, V7X_LANES), kvmap),
            pl.BlockSpec((s, GATE_COLS), lambda b, p, i: (b, 0)),
            pl.BlockSpec((s, GATE_COLS), lambda b, p, i: (b, 0)),
            pl.BlockSpec((s, GATE_COLS), lambda b, p, i: (b, 0)),
        ],
        out_specs=pl.BlockSpec((tq, V7X_LANES), qmap),
        out_shape=jax.ShapeDtypeStruct((t, F_W), BF16),
        scratch_shapes=[
            pltpu.VMEM((2, s, V7X_LANES), BF16),
            pltpu.VMEM((2, s, V7X_LANES), BF16),
            pltpu.VMEM((s // blk, V7X_LANES, blk), BF16),
        ] + [pltpu.VMEM((blk, tq), F32)] * 4 + [pltpu.VMEM((blk, tq), BF16)] * 4
          + [pltpu.VMEM((V7X_LANES, tq), F32)] * 2,
        compiler_params=_params(("parallel", "parallel", "arbitrary")),
        name="fox_attention",
    )(fq, fk, fv, ccol, crel_k, crel_q)


def _layer_norm_rows(r, g, b):
    mu = jnp.mean(r, axis=-1, keepdims=True)
    d = r - mu
    var = jnp.mean(d * d, axis=-1, keepdims=True)
    return (d * lax.rsqrt(var + LN_EPS)) * g + b


def _merge_kernel(dn_alpha, hm_ref, hf_ref, x_ref, wg_ref, bg_ref, wbm_ref, wbf_ref, wo_ref,
                  g_ref, b_ref, wrh_ref, wrl_ref, br_ref, h1_ref, h1p_ref, gate_ref, tope_ref, cnt_ref):
    counts = jnp.zeros((1, V7X_LANES), F32)
    for n in range(x_ref.shape[0] // MERGE_SUB_ROWS):
        rows = slice(n * MERGE_SUB_ROWS, (n + 1) * MERGE_SUB_ROWS)
        counts = counts + _merge_rows(dn_alpha, rows, hm_ref, hf_ref, x_ref, wg_ref, bg_ref, wbm_ref,
                                      wbf_ref, wo_ref, g_ref, b_ref, wrh_ref, wrl_ref, br_ref,
                                      h1_ref, h1p_ref, gate_ref, tope_ref)
    sub = lax.broadcasted_iota(I32, cnt_ref.shape, 0)
    cnt_ref[...] = jnp.where(sub == 0, counts, 0.0)


def _merge_rows(dn_alpha, rows, hm_ref, hf_ref, x_ref, wg_ref, bg_ref, wbm_ref, wbf_ref, wo_ref,
                g_ref, b_ref, wrh_ref, wrl_ref, br_ref, h1_ref, h1p_ref, gate_ref, tope_ref):
    x = x_ref[rows, :]
    d = x.shape[1]
    gmf = _dot(x.astype(BF16), wg_ref[...]) + bg_ref[...]
    ym = _dot(hm_ref[rows, :], wbm_ref[...])
    yf = _dot(hf_ref[rows, :], wbf_ref[...])
    y = _sigmoid(gmf[:, :d]) * ym + _sigmoid(gmf[:, d:]) * yf
    mix = _dot(y.astype(BF16), wo_ref[...])
    h1 = _layer_norm_rows(dn_alpha * x + mix, g_ref[...], b_ref[...])
    h1_ref[rows, :] = h1

    h1p_ref[rows, :] = _pack_bf16_pairs(h1)
    hb = h1.astype(BF16)

    lo = (h1 - hb.astype(F32)).astype(BF16)
    logits = (_dot(lo, wrh_ref[...]) + _dot(hb, wrl_ref[...])) + _dot(hb, wrh_ref[...]) + br_ref[...]
    tm = logits.shape[0]
    lane = lax.broadcasted_iota(I32, (tm, V7X_LANES), 1)
    vals = jnp.where(lane < N_EXPERTS, logits, -jnp.inf)
    top_v, top_i = [], []
    for _ in range(TOP_K):
        mx = jnp.max(vals, axis=-1, keepdims=True)
        idx = jnp.min(jnp.where(vals == mx, lane, V7X_LANES), axis=-1, keepdims=True)
        top_v.append(mx)
        top_i.append(idx)
        vals = jnp.where(lane == idx, -jnp.inf, vals)
    ex = [jnp.exp(v - top_v[0]) for v in top_v]
    tot = ex[0]
    for e in ex[1:]:
        tot = tot + e
    gate = jnp.zeros((tm, V7X_LANES), F32)
    tope = jnp.zeros((tm, V7X_LANES), I32)
    member = jnp.zeros((tm, V7X_LANES), F32)
    for k in range(TOP_K):
        gate = jnp.where(lane == k, ex[k] / tot, gate)
        tope = jnp.where(lane == k, top_i[k], tope)
        member = member + (lane == top_i[k]).astype(F32)
    gate_ref[rows, :] = gate
    tope_ref[rows, :] = tope
    return jnp.sum(member, axis=0, keepdims=True)


def _merge(dn_alpha, hm, hf, x2, wg, bg, wbm, wbf, wo, ln_g, ln_b, wr_hi, wr_lo, br):
    t, d = x2.shape
    tm = MERGE_ROWS
    row = lambda i: (i, 0)
    const = lambda i: (0, 0)
    full = lambda a: pl.BlockSpec(a.shape, const)
    return pl.pallas_call(
        functools.partial(_merge_kernel, dn_alpha),
        grid=(t // tm,),
        in_specs=[
            pl.BlockSpec((tm, M_W), row),
            pl.BlockSpec((tm, F_W), row),
            pl.BlockSpec((tm, d), row),
            full(wg), full(bg), full(wbm), full(wbf), full(wo), full(ln_g), full(ln_b),
            full(wr_hi), full(wr_lo), full(br),
        ],
        out_specs=(
            pl.BlockSpec((tm, d), row),
            pl.BlockSpec((tm, d // 2), row),
            pl.BlockSpec((tm, V7X_LANES), row),
            pl.BlockSpec((tm, V7X_LANES), row),
            pl.BlockSpec((V7X_SUBLANES, V7X_LANES), row),
        ),
        out_shape=(
            jax.ShapeDtypeStruct((t, d), F32),
            jax.ShapeDtypeStruct((t, d // 2), U32),
            jax.ShapeDtypeStruct((t, V7X_LANES), F32),
            jax.ShapeDtypeStruct((t, V7X_LANES), I32),
            jax.ShapeDtypeStruct((t // tm * V7X_SUBLANES, V7X_LANES), F32),
        ),
        compiler_params=_params(("parallel",)),
        name="merge_ln1_router",
    )(hm, hf, x2, wg, bg, wbm, wbf, wo, ln_g, ln_b, wr_hi, wr_lo, br)


def _lane_cumsum(x):
    lane = lax.broadcasted_iota(I32, x.shape, 1)
    d = 1
    while d < V7X_LANES:
        x = x + jnp.where(lane >= d, pltpu.roll(x, shift=d, axis=1), 0.0)
        d *= 2
    return x


def _routing_kernel(cnt_ref, tope_ref, dest_ref, table_ref, run_ref, start_ref):
    sb = ROUTE_SUB_ROWS

    @pl.when(pl.program_id(0) == 0)
    def _():
        total = jnp.sum(cnt_ref[...], axis=0, keepdims=True)
        counts = jnp.broadcast_to(total, (V7X_SUBLANES, V7X_LANES))
        padded = jnp.ceil(counts * (1.0 / MOE_BLOCK)) * MOE_BLOCK
        pad_end = _lane_cumsum(padded)
        pad_start = pad_end - padded
        start_ref[...] = pad_start
        run_ref[...] = jnp.zeros(run_ref.shape, F32)
        nb = table_ref.shape[0]
        blk = lax.broadcasted_iota(I32, (nb, V7X_LANES), 0).astype(F32) * MOE_BLOCK
        ln = lax.broadcasted_iota(I32, (nb, V7X_LANES), 1)
        done = jnp.logical_and(pad_end[0:1, :] <= blk, ln < N_EXPERTS)
        be = jnp.minimum(jnp.sum(done.astype(F32), axis=-1, keepdims=True), N_EXPERTS - 1.0)
        onehot = ln == be.astype(I32)
        cnt_e = jnp.sum(jnp.where(onehot, counts[0:1, :], 0.0), axis=-1, keepdims=True)
        start_e = jnp.sum(jnp.where(onehot, pad_start[0:1, :], 0.0), axis=-1, keepdims=True)
        valid = jnp.clip(cnt_e - (blk[:, 0:1] - start_e), 0.0, float(MOE_BLOCK))
        table_ref[...] = jnp.where(ln == 0, be.astype(I32),
                                   jnp.where(ln == 1, valid.astype(I32), 0))

    earlier = _tril_mask(sb, strict=True).astype(BF16)
    lane = lax.broadcasted_iota(I32, (sb, V7X_LANES), 1)
    for j in range(tope_ref.shape[0] // sb):
        tope = tope_ref[j * sb:(j + 1) * sb, :]
        hit = [lane == tope[:, k:k + 1] for k in range(TOP_K)]
        member = jnp.zeros((sb, V7X_LANES), F32)
        for k in range(TOP_K):
            member = member + hit[k].astype(F32)
        base = _dot(earlier, member.astype(BF16)) + (run_ref[0:1, :] + start_ref[0:1, :])
        dest = jnp.zeros((sb, V7X_LANES), I32)
        for k in range(TOP_K):
            dk = jnp.sum(jnp.where(hit[k], base, 0.0), axis=-1, keepdims=True)
            dest = jnp.where(lane == k, dk.astype(I32), dest)
        dest_ref[j * sb:(j + 1) * sb, :] = dest
        run_ref[...] = run_ref[...] + jnp.sum(member, axis=0, keepdims=True)


def _routing(tile_counts, tope, n_blocks):
    t = tope.shape[0]
    tr = ROUTE_ROWS
    return pl.pallas_call(
        _routing_kernel,
        grid=(t // tr,),
        in_specs=[pl.BlockSpec(tile_counts.shape, lambda i: (0, 0)),
                  pl.BlockSpec((tr, V7X_LANES), lambda i: (i, 0))],
        out_specs=(
            pl.BlockSpec((tr, V7X_LANES), lambda i: (i, 0)),
            pl.BlockSpec((n_blocks, V7X_LANES), lambda i: (0, 0)),
        ),
        out_shape=(
            jax.ShapeDtypeStruct((t, V7X_LANES), I32),
            jax.ShapeDtypeStruct((n_blocks, V7X_LANES), I32),
        ),
        scratch_shapes=[
            pltpu.VMEM((V7X_SUBLANES, V7X_LANES), F32),
            pltpu.VMEM((V7X_SUBLANES, V7X_LANES), F32),
        ],
        compiler_params=_params(("arbitrary",)),
        name="routing",
    )(tile_counts, tope)


def _sc_worker_id():
    return lax.axis_index("s") * V7X_SC_CORES + lax.axis_index("c")


def _sc_mesh():
    return plsc.VectorSubcoreMesh(core_axis_name="c", subcore_axis_name="s",
                                  num_cores=V7X_SC_CORES, num_subcores=V7X_SC_SUBCORES)


def _sc_dispatch(dest_km, h1p, n_rows):
    t, w = h1p.shape
    per_worker = t // V7X_SC_WORKERS
    ch = SC_ROWS_PER_DMA

    @functools.partial(
        pl.kernel, mesh=_sc_mesh(),
        out_type=jax.ShapeDtypeStruct((n_rows, w), h1p.dtype),
        scratch_types=[pltpu.VMEM((ch,), I32), pltpu.VMEM((ch, w), h1p.dtype), pltpu.SemaphoreType.DMA],
        name="sc_dispatch",
    )
    def scatter_rows(dest_hbm, h1p_hbm, xs_hbm, idx_v, rows_v, sem):
        first = _sc_worker_id() * per_worker

        @pl.loop(0, per_worker // ch)
        def _(j):
            base = first + j * ch
            pltpu.sync_copy(h1p_hbm.at[pl.ds(base, ch)], rows_v)
            for k in range(TOP_K):
                pltpu.sync_copy(dest_hbm.at[pl.ds(k * t + base, ch)], idx_v)
                pltpu.async_copy(rows_v, xs_hbm.at[idx_v], sem).wait()

    return scatter_rows(dest_km, h1p)


def _sc_gather(dest_km, y_rows):
    n = dest_km.shape[0]
    w = y_rows.shape[1]
    per_worker = n // V7X_SC_WORKERS
    ch = SC_ROWS_PER_DMA

    @functools.partial(
        pl.kernel, mesh=_sc_mesh(),
        out_type=jax.ShapeDtypeStruct((n, w), y_rows.dtype),
        scratch_types=[pltpu.VMEM((ch,), I32), pltpu.VMEM((ch, w), y_rows.dtype), pltpu.SemaphoreType.DMA],
        name="sc_gather",
    )
    def gather_rows(dest_hbm, y_hbm, out_hbm, idx_v, rows_v, sem):
        first = _sc_worker_id() * per_worker

        @pl.loop(0, per_worker // ch)
        def _(j):
            base = first + j * ch
            pltpu.sync_copy(dest_hbm.at[pl.ds(base, ch)], idx_v)
            pltpu.async_copy(y_hbm.at[idx_v], rows_v, sem).wait()
            pltpu.sync_copy(rows_v, out_hbm.at[pl.ds(base, ch)])

    return gather_rows(dest_km, y_rows)


def _expert_kernel(be_ref, nv_ref, xs_ref, wgu_ref, bgu_ref, wdn_ref, bdn_ref, y_ref):
    i = pl.program_id(0)
    nv = nv_ref[i]

    half = MOE_BLOCK // 2

    def ffn(rows):
        x = _unpack_bf16_pairs(xs_ref[rows, :])
        rowid = lax.broadcasted_iota(I32, x.shape, 0)
        x = jnp.where(rowid < nv, x, 0.0).astype(BF16)
        gu = _dot(x, wgu_ref[...]) + bgu_ref[...]
        f = gu.shape[1] // 2
        glu = jnp.minimum(gu[:, :f], SWIGLU_LIMIT)
        lin = jnp.clip(gu[:, f:], -SWIGLU_LIMIT, SWIGLU_LIMIT)
        act = glu * _sigmoid(SWIGLU_ALPHA * glu) * (lin + 1.0)
        y_ref[rows, :] = _pack_bf16_pairs(_dot(act.astype(BF16), wdn_ref[...]) + bdn_ref[...])

    @pl.when(nv == 0)
    def _():
        y_ref[...] = jnp.zeros(y_ref.shape, U32)

    @pl.when(jnp.logical_and(nv > 0, nv <= half))
    def _():
        ffn(slice(0, half))
        y_ref[half:, :] = jnp.zeros((MOE_BLOCK - half, y_ref.shape[1]), U32)

    @pl.when(nv > half)
    def _():
        ffn(slice(0, MOE_BLOCK))


def _experts(block_e, block_valid, xs, wgu, bgu, wdn, bdn):
    n_rows, w = xs.shape
    e, d, f2 = wgu.shape
    n_blocks = n_rows // MOE_BLOCK
    grid_spec = pltpu.PrefetchScalarGridSpec(
        num_scalar_prefetch=2,
        grid=(n_blocks,),
        in_specs=[
            pl.BlockSpec((MOE_BLOCK, w), lambda i, be, nv: (i, 0)),
            pl.BlockSpec((None, d, f2), lambda i, be, nv: (be[i], 0, 0)),
            pl.BlockSpec((None, 1, f2), lambda i, be, nv: (be[i], 0, 0)),
            pl.BlockSpec((None, f2 // 2, d), lambda i, be, nv: (be[i], 0, 0)),
            pl.BlockSpec((None, 1, d), lambda i, be, nv: (be[i], 0, 0)),
        ],
        out_specs=pl.BlockSpec((MOE_BLOCK, d // 2), lambda i, be, nv: (i, 0)),
    )
    return pl.pallas_call(
        _expert_kernel,
        grid_spec=grid_spec,
        out_shape=jax.ShapeDtypeStruct((n_rows, d // 2), U32),
        compiler_params=_params(("arbitrary",)),
        name="experts",
    )(block_e, block_valid, xs, wgu, bgu, wdn, bdn)


def _combine_kernel(dn_alpha, h1_ref, gate_ref, g_ref, b_ref, yg_ref, *rest):
    o_ref = rest[-1]
    gate = gate_ref[...]
    ffn = gate[:, 0:1] * _unpack_bf16_pairs(yg_ref[0])
    for k in range(1, TOP_K):
        ffn = ffn + gate[:, k:k + 1] * _unpack_bf16_pairs(yg_ref[k])
    o_ref[...] = _layer_norm_rows(dn_alpha * h1_ref[...] + ffn, g_ref[...], b_ref[...])


def _combine(dn_alpha, h1, gate, ln_g, ln_b, yg, part, n_parts, prev_out):
    t, d = h1.shape
    tc = COMBINE_ROWS
    steps = t // tc // n_parts
    row = lambda i: (part * steps + i, 0)
    const = lambda i: (0, 0)
    in_specs = [
        pl.BlockSpec((tc, d), row),
        pl.BlockSpec((tc, V7X_LANES), row),
        pl.BlockSpec((1, d), const),
        pl.BlockSpec((1, d), const),
        pl.BlockSpec((TOP_K, tc, d // 2), lambda i: (0, i, 0)),
    ]
    args = [h1, gate, ln_g, ln_b, yg]
    aliases = {}
    if prev_out is not None:
        in_specs.append(pl.BlockSpec(memory_space=pl.ANY))
        args.append(prev_out)
        aliases = {len(args) - 1: 0}
    return pl.pallas_call(
        functools.partial(_combine_kernel, dn_alpha),
        grid=(steps,),
        in_specs=in_specs,
        out_specs=pl.BlockSpec((tc, d), row),
        out_shape=jax.ShapeDtypeStruct((t, d), F32),
        input_output_aliases=aliases,
        compiler_params=_params(("parallel",)),
        name="combine_ln2",
    )(*args)


def _pack_in_proj(w_in, b_in):
    d = w_in.shape[0]
    o = 0
    cols = {}
    for name, width in (("mqk", MQK_W), ("mv", M_W), ("mo", M_W), ("mi", M_HEADS), ("mf", M_HEADS),
                        ("fq", F_W), ("fk", F_W), ("fv", F_W), ("ff", F_HEADS), ("gm", d), ("gf", d)):
        cols[name] = (o, o + width)
        o += width

    def take(a, names):
        return [a[..., cols[n][0]:cols[n][1]] for n in names]

    n_gate = 2 * M_HEADS + F_HEADS
    main = ("mqk", "mv", "mo", "fq", "fk", "fv", "mi", "mf", "ff")
    w_main = jnp.concatenate(take(w_in, main) + [jnp.zeros((d, GATE_COLS - n_gate), w_in.dtype)], axis=1)
    b_main = jnp.concatenate(take(b_in, main) + [jnp.zeros((GATE_COLS - n_gate,), b_in.dtype)])
    w_gate = jnp.concatenate(take(w_in, ("gm", "gf")), axis=1)
    b_gate = jnp.concatenate(take(b_in, ("gm", "gf")))
    return w_main.astype(BF16), b_main[None, :], w_gate.astype(BF16), b_gate[None, :]


def _layer(h, depth, w_in, b_in, m_conv_w, m_conv_b, m_norm_g, w_bm, w_bf, w_o, ln1_g, ln1_b,
           w_router, b_router, w_gu, b_gu, w_dn, b_dn, ln2_g, ln2_b):
    bsz, s, d = h.shape
    t = bsz * s
    dn_alpha = (2.0 * depth) ** 0.25
    x2 = h.reshape(t, d)

    w_main, b_main, w_gate, b_gate = _pack_in_proj(w_in, b_in)
    mq, mk, mv, mo, fq, fk, fv, gates = _inproj(x2, w_main, b_main, m_conv_w, m_conv_b[None, :], s)
    ccol, crel_k, crel_q = _fox_cumsum(gates, bsz, s)
    hm = _mlstm(mq, mk, mv, mo, gates, m_norm_g, bsz, s)
    hf = _fox_attention(fq, fk, fv, ccol, crel_k, crel_q, bsz, s)

    n_exp = w_router.shape[1]
    wr = jnp.zeros((d, V7X_LANES), F32).at[:, :n_exp].set(w_router)
    wr_hi = wr.astype(BF16)
    wr_lo = (wr - wr_hi.astype(F32)).astype(BF16)
    br = jnp.zeros((1, V7X_LANES), F32).at[0, :n_exp].set(b_router)
    h1, h1p, gate, tope, tile_counts = _merge(
        dn_alpha, hm, hf, x2, w_gate, b_gate, w_bm.astype(BF16), w_bf.astype(BF16), w_o.astype(BF16),
        ln1_g[None, :], ln1_b[None, :], wr_hi, wr_lo, br)

    n_blocks = -(-(t * TOP_K) // MOE_BLOCK) + N_EXPERTS
    dest, table = _routing(tile_counts, tope, n_blocks)
    dest_km = dest[:, :TOP_K].T.reshape(TOP_K * t)
    block_e, block_valid = table[:, 0], table[:, 1]
    xs = _sc_dispatch(dest_km, h1p, n_blocks * MOE_BLOCK)
    y_rows = _experts(block_e, block_valid, xs, w_gu.astype(BF16), b_gu[:, None, :],
                      w_dn.astype(BF16), b_dn[:, None, :])
    part_t = t // COMBINE_PARTS
    dest_parts = dest_km.reshape(TOP_K, COMBINE_PARTS, part_t)
    out = None
    for part in range(COMBINE_PARTS):
        yg = _sc_gather(dest_parts[:, part, :].reshape(TOP_K * part_t), y_rows)
        out = _combine(dn_alpha, h1, gate, ln2_g[None, :], ln2_b[None, :],
                       yg.reshape(TOP_K, part_t, d // 2), part, COMBINE_PARTS, out)
    return out.reshape(bsz, s, d)


def kernel(x, w_in, b_in, m_conv_w, m_conv_b, m_norm_g, w_bm, w_bf, w_o, ln1_g, ln1_b,
           w_router, b_router, w_gu, b_gu, w_dn, b_dn, ln2_g, ln2_b):
    depth = w_in.shape[0]
    h = x
    for l in range(depth):
        h = _layer(h, depth, w_in[l], b_in[l], m_conv_w[l], m_conv_b[l], m_norm_g[l], w_bm[l], w_bf[l],
                   w_o[l], ln1_g[l], ln1_b[l], w_router[l], b_router[l], w_gu[l], b_gu[l], w_dn[l],
                   b_dn[l], ln2_g[l], ln2_b[l])
    return h
```

```python
import functools
import math

import jax
import jax.numpy as jnp
from jax import lax
from jax.experimental import pallas as pl
from jax.experimental.pallas import tpu as pltpu
from jax.experimental.pallas import tpu_sc as plsc

F32 = jnp.float32
BF16 = jnp.bfloat16
I32 = jnp.int32
U32 = jnp.uint32

M_HEADS = 4
M_DQK = 128
M_DV = 128
CONV_W = 4
F_HEADS = 8
F_DH = 64
N_EXPERTS = 32
TOP_K = 4
SWIGLU_ALPHA = 1.702
SWIGLU_LIMIT = 7.0
LN_EPS = 1e-5

M_W = M_HEADS * M_DV
F_W = F_HEADS * F_DH
MQK_W = 2 * M_HEADS * M_DQK

V7X_LANES = 128
V7X_SUBLANES = 8
V7X_VMEM_BYTES = 64 * 1024 * 1024
VMEM_LIMIT_BYTES = (V7X_VMEM_BYTES * 3) // 4
EXPERT_VMEM_LIMIT_BYTES = (V7X_VMEM_BYTES * 7) // 8
V7X_SC_CORES = 2
V7X_SC_SUBCORES = 16
V7X_SC_WORKERS = V7X_SC_CORES * V7X_SC_SUBCORES

INPROJ_ROWS = 512
MLSTM_CHUNK = 256
MLSTM_SEQS = 2
FOX_Q_BLOCK = 512
FOX_K_BLOCK = 256
MERGE_ROWS = 512
MERGE_SUB_ROWS = 256
ROUTE_ROWS = 1024
ROUTE_SUB_ROWS = 256
MOE_BLOCK = 512
SC_ROWS_PER_DMA = 128
COMBINE_ROWS = 512

GATE_COLS = V7X_LANES
MI_LANE = 0
MF_LANE = M_HEADS
FF_LANE = 2 * M_HEADS


def _params(semantics):
    return pltpu.CompilerParams(dimension_semantics=semantics, vmem_limit_bytes=VMEM_LIMIT_BYTES)


def _log_sigmoid(x):
    return jnp.minimum(x, 0.0) - jnp.log1p(jnp.exp(-jnp.abs(x)))


def _sigmoid(x):
    return 1.0 / (1.0 + jnp.exp(-x))


def _dot(a, b):
    return jnp.dot(a, b, preferred_element_type=F32)


def _dot_nt(a, b):
    return lax.dot_general(a, b, (((1,), (1,)), ((), ())), preferred_element_type=F32)


def _dot_tn(a, b):
    return lax.dot_general(a, b, (((0,), (0,)), ((), ())), preferred_element_type=F32)


def _split3(x):
    hi = x.astype(BF16)
    r1 = x - hi.astype(F32)
    mid = r1.astype(BF16)
    lo = (r1 - mid.astype(F32)).astype(BF16)
    return hi, mid, lo


def _dot_mask_f32(mask_bf16, x):
    hi, mid, lo = _split3(x)
    return (_dot(mask_bf16, lo) + _dot(mask_bf16, mid)) + _dot(mask_bf16, hi)


def _pack_bf16_pairs(x):
    half = x.shape[1] // 2
    bits = lax.bitcast_convert_type(x.astype(BF16).astype(F32), U32)
    return (bits[:, :half] >> 16) | bits[:, half:]


def _unpack_bf16_pairs(words):
    lo = lax.bitcast_convert_type(words << 16, F32)
    hi = lax.bitcast_convert_type(words & jnp.uint32(0xFFFF0000), F32)
    return jnp.concatenate([lo, hi], axis=1)


def _tril_mask(n, strict=False):
    r = lax.broadcasted_iota(I32, (n, n), 0)
    c = lax.broadcasted_iota(I32, (n, n), 1)
    return (r > c) if strict else (r >= c)


_OFF_MQK = 0
_OFF_MV = _OFF_MQK + MQK_W
_OFF_MO = _OFF_MV + M_W
_OFF_FQ = _OFF_MO + M_W
_OFF_FK = _OFF_FQ + F_W
_OFF_FV = _OFF_FK + F_W
_OFF_GATES = _OFF_FV + F_W
_PACKED_COLS = _OFF_GATES + GATE_COLS


def _inproj_kernel(tiles_per_seq, x_ref, w_ref, b_ref, cw_ref, cb_ref,
                   mq_ref, mk_ref, mv_ref, mo_ref, fq_ref, fk_ref, fv_ref, gates_ref, ext_ref):
    tm = x_ref.shape[0]
    pad = V7X_SUBLANES
    xb = x_ref[...].astype(BF16)

    def seg(lo, width):
        return _dot(xb, w_ref[:, lo:lo + width]) + b_ref[:, lo:lo + width]

    @pl.when(pl.program_id(0) % tiles_per_seq == 0)
    def _():
        ext_ref[0:pad, :] = jnp.zeros((pad, MQK_W), F32)

    ext_ref[pad:pad + tm, :] = seg(_OFF_MQK, MQK_W)
    y = cb_ref[...] + cw_ref[CONV_W - 1:CONV_W, :] * ext_ref[pad:pad + tm, :]
    for k in range(CONV_W - 1):
        shift = CONV_W - 1 - k
        y = y + cw_ref[k:k + 1, :] * ext_ref[pad - shift:pad - shift + tm, :]
    ext_ref[0:pad, :] = ext_ref[tm:tm + pad, :]
    act = y * _sigmoid(y)
    mq_ref[...] = act[:, :MQK_W // 2].astype(BF16)
    mk_ref[...] = act[:, MQK_W // 2:] * (M_DQK ** -0.5)
    mv_ref[...] = seg(_OFF_MV, M_W).astype(BF16)
    mo_ref[...] = seg(_OFF_MO, M_W)
    fq_ref[...] = (seg(_OFF_FQ, F_W) * (F_DH ** -0.5)).astype(BF16)
    fk_ref[...] = seg(_OFF_FK, F_W).astype(BF16)
    fv_ref[...] = seg(_OFF_FV, F_W).astype(BF16)
    gates_ref[...] = seg(_OFF_GATES, GATE_COLS)


def _inproj(x2, w_packed, b_packed, conv_w, conv_b, s):
    t, d = x2.shape
    tm = INPROJ_ROWS
    row = lambda i: (i, 0)
    const = lambda i: (0, 0)
    out_shapes = (
        jax.ShapeDtypeStruct((t, MQK_W // 2), BF16),
        jax.ShapeDtypeStruct((t, MQK_W // 2), F32),
        jax.ShapeDtypeStruct((t, M_W), BF16),
        jax.ShapeDtypeStruct((t, M_W), F32),
        jax.ShapeDtypeStruct((t, F_W), BF16),
        jax.ShapeDtypeStruct((t, F_W), BF16),
        jax.ShapeDtypeStruct((t, F_W), BF16),
        jax.ShapeDtypeStruct((t, GATE_COLS), F32),
    )
    return pl.pallas_call(
        functools.partial(_inproj_kernel, s // tm),
        grid=(t // tm,),
        in_specs=[
            pl.BlockSpec((tm, d), row),
            pl.BlockSpec((d, _PACKED_COLS), const),
            pl.BlockSpec((1, _PACKED_COLS), const),
            pl.BlockSpec((CONV_W, MQK_W), const),
            pl.BlockSpec((1, MQK_W), const),
        ],
        out_specs=tuple(pl.BlockSpec((tm, o.shape[1]), row) for o in out_shapes),
        out_shape=out_shapes,
        scratch_shapes=[pltpu.VMEM((tm + V7X_SUBLANES, MQK_W), F32)],
        compiler_params=_params(("arbitrary",)),
        name="inproj",
    )(x2, w_packed, b_packed, conv_w, conv_b)


def _fox_cumsum_kernel(g_ref, ccol_ref, crel_k_ref, crel_q_ref):
    s = g_ref.shape[0]
    cb = FOX_K_BLOCK
    per_q = FOX_Q_BLOCK // cb
    tri = _tril_mask(cb).astype(BF16)
    carry = jnp.zeros((1, GATE_COLS), F32)
    for j in range(s // cb):
        rows = slice(j * cb, (j + 1) * cb)
        if j % per_q == 0:
            q_carry = carry
        within = _dot_mask_f32(tri, _log_sigmoid(g_ref[rows, :]))
        crel_k_ref[rows, :] = within
        crel_q_ref[rows, :] = within + (carry - q_carry)
        ccol_ref[rows, :] = within + carry
        carry = carry + within[cb - 1:cb, :]


def _fox_cumsum(gates, bsz, s):
    t = gates.shape[0]
    spec = pl.BlockSpec((s, GATE_COLS), lambda b: (b, 0))
    shape = jax.ShapeDtypeStruct((t, GATE_COLS), F32)
    return pl.pallas_call(
        _fox_cumsum_kernel,
        grid=(bsz,),
        in_specs=[spec],
        out_specs=(spec, spec, spec),
        out_shape=(shape, shape, shape),
        compiler_params=_params(("parallel",)),
        name="fox_cumsum",
    )(gates)


def _mlstm_kernel(mq_ref, mk_ref, mv_ref, mo_ref, gates_ref, ng_ref, hm_ref, state_ref, m_ref):
    @pl.when(pl.program_id(1) == 0)
    def _():
        state_ref[...] = jnp.zeros(state_ref.shape, F32)
        m_ref[...] = jnp.zeros(m_ref.shape, F32)

    seqs = range(mq_ref.shape[0])
    states = [[state_ref[bb, h] for h in range(M_HEADS)] for bb in seqs]
    maxes = [[m_ref[bb, h][0:1, 0:1] for h in range(M_HEADS)] for bb in seqs]
    results = [_mlstm_chunk(mq_ref.at[bb], mk_ref.at[bb], mv_ref.at[bb], mo_ref.at[bb], gates_ref.at[bb],
                            ng_ref, states[bb], maxes[bb]) for bb in seqs]
    for bb in seqs:
        for h, (out_h, state_h, m_h) in enumerate(results[bb]):
            hm_ref[bb, :, h * M_DV:(h + 1) * M_DV] = out_h
            state_ref[bb, h] = state_h
            m_ref[bb, h] = jnp.broadcast_to(m_h, m_ref.shape[2:])


def _mlstm_chunk(mq_ref, mk_ref, mv_ref, mo_ref, gates_ref, ng_ref, states, maxes):
    L = MLSTM_CHUNK
    reps = L // V7X_LANES
    results = []
    gates = gates_ref[...]
    bfull = _dot_mask_f32(_tril_mask(L).astype(BF16), _log_sigmoid(gates))
    b_rows = bfull.T
    z_all = gates - pltpu.roll(bfull, shift=V7X_LANES - (MF_LANE - MI_LANE), axis=1)
    visible = (lax.broadcasted_iota(I32, (L, L), 0) <= lax.broadcasted_iota(I32, (L, L), 1))
    ones_rows = (lax.broadcasted_iota(I32, (M_DV, L), 0) == 0).astype(BF16)

    for h in range(M_HEADS):
        b_row = b_rows[MF_LANE + h:MF_LANE + h + 1, :]
        g_tot = b_row[:, L - 1:L]
        m_prev = maxes[h]
        z = jnp.broadcast_to(z_all[:, MI_LANE + h:MI_LANE + h + 1], (L, V7X_LANES))

        q_h = mq_ref[:, h * M_DQK:(h + 1) * M_DQK]
        k_f = mk_ref[:, h * M_DQK:(h + 1) * M_DQK]
        k_h = k_f.astype(BF16)
        v_t = mv_ref[:, h * M_DV:(h + 1) * M_DV].astype(F32).T.astype(BF16)
        cn_t = states[h]

        dlog = jnp.where(visible, b_row + jnp.tile(z, (1, reps)), -jnp.inf)
        inter_log = b_row + m_prev
        m_t = jnp.maximum(inter_log, jnp.max(dlog, axis=0, keepdims=True))
        w_inter = jnp.exp(inter_log - m_t)
        qkw = _dot_nt(k_h, q_h) * jnp.exp(dlog - m_t)
        qc = _dot_nt(cn_t.astype(BF16), q_h)
        num = w_inter * qc[:M_DV, :] + _dot(v_t, qkw.astype(BF16))
        den = w_inter * qc[M_DV:M_DV + 1, :] + jnp.sum(qkw, axis=0, keepdims=True)
        hh = num / jnp.maximum(jnp.abs(den), jnp.exp(-m_t))

        mu = jnp.mean(hh, axis=0, keepdims=True)
        dv = hh - mu
        var = jnp.mean(dv * dv, axis=0, keepdims=True)
        hn = (dv * lax.rsqrt(var + LN_EPS)) * jnp.tile(ng_ref[h * M_DV:(h + 1) * M_DV, :], (1, reps))
        out_h = (_sigmoid(mo_ref[:, h * M_DV:(h + 1) * M_DV]) * hn.T).astype(BF16)

        a = g_tot + z
        m_new = jnp.maximum(g_tot + m_prev, jnp.max(a, axis=0, keepdims=True)[:, 0:1])
        decay = jnp.exp(g_tot + m_prev - m_new)
        kw = (k_f * jnp.exp(a - m_new)).astype(BF16)
        v_aug = jnp.concatenate([v_t, ones_rows], axis=0)
        results.append((out_h, decay * cn_t + _dot(v_aug, kw), m_new))
    return results


def _mlstm(mq, mk, mv, mo, gates, norm_g, bsz, s):
    t = mq.shape[0]
    L = MLSTM_CHUNK
    nb = MLSTM_SEQS
    seq = lambda a: a.reshape(bsz, s, a.shape[1])
    blk = lambda width: pl.BlockSpec((nb, L, width), lambda g, n: (g, n, 0))
    const = lambda g, n: (0, 0)
    hm = pl.pallas_call(
        _mlstm_kernel,
        grid=(bsz // nb, s // L),
        in_specs=[
            blk(MQK_W // 2), blk(MQK_W // 2), blk(M_W), blk(M_W), blk(GATE_COLS),
            pl.BlockSpec((M_W, V7X_LANES), const),
        ],
        out_specs=blk(M_W),
        out_shape=jax.ShapeDtypeStruct((bsz, s, M_W), BF16),
        scratch_shapes=[
            pltpu.VMEM((nb, M_HEADS, 2 * M_DV, M_DQK), F32),
            pltpu.VMEM((nb, M_HEADS, V7X_SUBLANES, V7X_LANES), F32),
        ],
        compiler_params=_params(("parallel", "arbitrary")),
        name="mlstm",
    )(seq(mq), seq(mk), seq(mv), seq(mo), seq(gates),
      jnp.broadcast_to(norm_g[:, None], (M_W, V7X_LANES)))
    return hm.reshape(t, M_W)


def _fox_operand(x, c_col, hh, key_side):
    rows = x.shape[0]
    lane = lax.broadcasted_iota(I32, (rows, V7X_LANES), 1)
    own = (lane < F_DH) if hh == 0 else (lane >= F_DH)
    base = F_DH if hh == 0 else 0
    parts = [part.astype(F32) for part in _split3(c_col)]
    if key_side:
        feats = parts + [1.0, 1.0, 1.0]
    else:
        feats = [-1.0, -1.0, -1.0] + parts
    out = jnp.where(own, x, 0.0)
    for n, feat in enumerate(feats):
        out = jnp.where(lane == base + n, feat, out)
    return out.astype(BF16)


def _fox_kernel(q_ref, k_ref, v_ref, ccol_ref, crel_k_ref, crel_q_ref, o_ref, qaug_ref, kaug_ref, vt_ref,
                st_a0, st_a1, st_b0, st_b1, pe_a0, pe_a1, pe_b0, pe_b1, acc0, acc1):
    blk = FOX_K_BLOCK
    tq = FOX_Q_BLOCK
    assert tq == 2 * blk
    strips = tq // V7X_LANES
    st_a_refs, st_b_refs = (st_a0, st_a1), (st_b0, st_b1)
    pe_a_refs, pe_b_refs = (pe_a0, pe_a1), (pe_b0, pe_b1)
    acc_refs = (acc0, acc1)
    p = pl.program_id(1)
    i = pl.program_id(2)

    @pl.when(i == 0)
    def _():
        lane = lax.broadcasted_iota(I32, crel_q_ref.shape, 1)
        q_all = q_ref[...].astype(F32)
        k_all = k_ref[...].astype(F32)
        for hh in range(2):
            mine = lane == FF_LANE + 2 * p + hh
            c_q = jnp.sum(jnp.where(mine, crel_q_ref[...], 0.0), axis=-1, keepdims=True)
            c_k = jnp.sum(jnp.where(mine, crel_k_ref[...], 0.0), axis=-1, keepdims=True)
            qaug_ref[hh] = _fox_operand(q_all, c_q, hh, False)
            kaug_ref[hh] = _fox_operand(k_all, c_k, hh, True)
        v_t = v_ref[...].astype(F32).T
        for j in range(vt_ref.shape[0]):
            vt_ref[j] = v_t[:, j * blk:(j + 1) * blk].astype(BF16)

    q_start = pl.multiple_of(i * tq, tq)
    q_heads = [qaug_ref[hh, pl.ds(q_start, tq), :] for hh in range(2)]
    key_row = lax.broadcasted_iota(I32, (blk, V7X_LANES), 0)
    query_col = lax.broadcasted_iota(I32, (blk, V7X_LANES), 1)
    last_chunk = 2 * i + 1
    head_lane = lax.broadcasted_iota(I32, (1, GATE_COLS), 1) - (FF_LANE + 2 * p)

    def c_before(position, hh):
        row = ccol_ref[pl.ds(jnp.maximum(position - 1, 0), 1), :]
        keep = jnp.logical_and(head_lane == hh, position > 0)
        return jnp.sum(jnp.where(keep, row, 0.0), axis=-1, keepdims=True)

    c_query0 = [c_before(q_start, hh) for hh in range(2)]

    def scores(j, hh):
        start = pl.multiple_of(j * blk, blk)
        return _dot_nt(kaug_ref[hh, pl.ds(start, blk), :], q_heads[hh])

    def softmax_update(st_ref, pe_ref, m_old, l_old, base, key_minus_query=None):
        alphas, ms, ls = [], [], []
        for c in range(strips):
            cols = slice(c * V7X_LANES, (c + 1) * V7X_LANES)
            gap = None if key_minus_query is None else key_minus_query - c * V7X_LANES
            if gap is not None and gap - (V7X_LANES - 1) > 0:
                pe_ref[:, cols] = jnp.zeros((blk, V7X_LANES), BF16)
                alphas.append(jnp.ones((1, V7X_LANES), F32))
                ms.append(m_old[:, cols])
                ls.append(l_old[:, cols])
                continue
            st = st_ref[:, cols]
            if gap is not None and gap + (blk - 1) > 0:
                st = jnp.where(key_row + gap <= query_col, st, -jnp.inf)
            m_new = jnp.maximum(m_old[:, cols], jnp.max(st, axis=0, keepdims=True) + base)
            alpha = jnp.exp(m_old[:, cols] - m_new)
            pe = jnp.exp(st - (m_new - base))
            pe_ref[:, cols] = pe.astype(BF16)
            alphas.append(alpha)
            ms.append(m_new)
            ls.append(alpha * l_old[:, cols] + jnp.sum(pe, axis=0, keepdims=True))
        cat = lambda parts: jnp.concatenate(parts, axis=1)
        return cat(alphas), cat(ms), cat(ls)

    def pair(mi, carry, diagonal=False):
        a = 2 * mi
        b = a + 1
        v_prev = vt_ref[jnp.maximum(a - 1, 0)]
        v_a = vt_ref[a]
        partial = []
        for hh in range(2):
            alpha_prev = carry[hh][0]
            partial.append(alpha_prev * acc_refs[hh][...] + _dot(v_prev, pe_b_refs[hh][...]))
            st_b_refs[hh][...] = scores(b, hh)
        stats = []
        for hh in range(2):
            _, m_old, l_old = carry[hh]
            stats.append(softmax_update(st_a_refs[hh], pe_a_refs[hh], m_old, l_old,
                                        c_query0[hh] - c_before(a * blk, hh), 0 if diagonal else None))
        for hh in range(2):
            alpha_a = stats[hh][0]
            acc_refs[hh][...] = alpha_a * partial[hh] + _dot(v_a, pe_a_refs[hh][...])
            if not diagonal:
                st_a_refs[hh][...] = scores(a + 2, hh)
        return tuple(softmax_update(st_b_refs[hh], pe_b_refs[hh], stats[hh][1], stats[hh][2],
                                    c_query0[hh] - c_before(b * blk, hh), blk if diagonal else None)
                     for hh in range(2))

    for hh in range(2):
        st_a_refs[hh][...] = scores(0, hh)
        pe_b_refs[hh][...] = jnp.zeros((blk, tq), BF16)
        acc_refs[hh][...] = jnp.zeros((V7X_LANES, tq), F32)
    init = tuple((jnp.ones((1, tq), F32), jnp.full((1, tq), -jnp.inf, F32), jnp.zeros((1, tq), F32))
                 for _ in range(2))
    final = pair(i, lax.fori_loop(0, i, pair, init), diagonal=True)
    v_last = vt_ref[last_chunk]
    outs = []
    for hh in range(2):
        alpha, _, l_fin = final[hh]
        outs.append((alpha * acc_refs[hh][...] + _dot(v_last, pe_b_refs[hh][...])) / l_fin)
    row = lax.broadcasted_iota(I32, (V7X_LANES, tq), 0)
    o_t = jnp.where(row < F_DH, outs[0], outs[1])
    o_ref[...] = o_t.T.astype(BF16)


def _fox_attention(fq, fk, fv, ccol, crel_k, crel_q, bsz, s):
    t = fq.shape[0]
    blk = FOX_K_BLOCK
    tq = FOX_Q_BLOCK
    nq = s // tq
    pairs = F_HEADS // 2
    qmap = lambda b, p, i: (b * nq + i, p)
    kvmap = lambda b, p, i: (b, p)
    return pl.pallas_call(
        _fox_kernel,
        grid=(bsz, pairs, nq),
        in_specs=[
            pl.BlockSpec((s, V7X_LANES), kvmap),
            pl.BlockSpec((s, V7X_LANES), kvmap),
            pl.BlockSpec((s, V7X_LANES), kvmap),
            pl.BlockSpec((s, GATE_COLS), lambda b, p, i: (b, 0)),
            pl.BlockSpec((s, GATE_COLS), lambda b, p, i: (b, 0)),
            pl.BlockSpec((s, GATE_COLS), lambda b, p, i: (b, 0)),
        ],
        out_specs=pl.BlockSpec((tq, V7X_LANES), qmap),
        out_shape=jax.ShapeDtypeStruct((t, F_W), BF16),
        scratch_shapes=[
            pltpu.VMEM((2, s, V7X_LANES), BF16),
            pltpu.VMEM((2, s, V7X_LANES), BF16),
            pltpu.VMEM((s // blk, V7X_LANES, blk), BF16),
        ] + [pltpu.VMEM((blk, tq), F32)] * 4 + [pltpu.VMEM((blk, tq), BF16)] * 4
          + [pltpu.VMEM((V7X_LANES, tq), F32)] * 2,
        compiler_params=_params(("parallel", "parallel", "arbitrary")),
        name="fox_attention",
    )(fq, fk, fv, ccol, crel_k, crel_q)


def _layer_norm_rows(r, g, b):
    mu = jnp.mean(r, axis=-1, keepdims=True)
    d = r - mu
    var = jnp.mean(d * d, axis=-1, keepdims=True)
    return (d * lax.rsqrt(var + LN_EPS)) * g + b


def _merge_kernel(dn_alpha, hm_ref, hf_ref, x_ref, wg_ref, bg_ref, wbm_ref, wbf_ref, wo_ref,
                  g_ref, b_ref, wrh_ref, wrl_ref, br_ref, h1_ref, h1p_ref, gate_ref, tope_ref, cnt_ref):
    counts = jnp.zeros((1, V7X_LANES), F32)
    for n in range(x_ref.shape[0] // MERGE_SUB_ROWS):
        rows = slice(n * MERGE_SUB_ROWS, (n + 1) * MERGE_SUB_ROWS)
        counts = counts + _merge_rows(dn_alpha, rows, hm_ref, hf_ref, x_ref, wg_ref, bg_ref, wbm_ref,
                                      wbf_ref, wo_ref, g_ref, b_ref, wrh_ref, wrl_ref, br_ref,
                                      h1_ref, h1p_ref, gate_ref, tope_ref)
    sub = lax.broadcasted_iota(I32, cnt_ref.shape, 0)
    cnt_ref[...] = jnp.where(sub == 0, counts, 0.0)


def _merge_rows(dn_alpha, rows, hm_ref, hf_ref, x_ref, wg_ref, bg_ref, wbm_ref, wbf_ref, wo_ref,
                g_ref, b_ref, wrh_ref, wrl_ref, br_ref, h1_ref, h1p_ref, gate_ref, tope_ref):
    x = x_ref[rows, :]
    d = x.shape[1]
    gmf = _dot(x.astype(BF16), wg_ref[...]) + bg_ref[...]
    ym = _dot(hm_ref[rows, :], wbm_ref[...])
    yf = _dot(hf_ref[rows, :], wbf_ref[...])
    y = _sigmoid(gmf[:, :d]) * ym + _sigmoid(gmf[:, d:]) * yf
    mix = _dot(y.astype(BF16), wo_ref[...])
    h1 = _layer_norm_rows(dn_alpha * x + mix, g_ref[...], b_ref[...])
    h1_ref[rows, :] = h1

    h1p_ref[rows, :] = _pack_bf16_pairs(h1)
    hb = h1.astype(BF16)

    lo = (h1 - hb.astype(F32)).astype(BF16)
    logits = (_dot(lo, wrh_ref[...]) + _dot(hb, wrl_ref[...])) + _dot(hb, wrh_ref[...]) + br_ref[...]
    tm = logits.shape[0]
    lane = lax.broadcasted_iota(I32, (tm, V7X_LANES), 1)
    vals = jnp.where(lane < N_EXPERTS, logits, -jnp.inf)
    top_v, top_i = [], []
    for _ in range(TOP_K):
        mx = jnp.max(vals, axis=-1, keepdims=True)
        idx = jnp.min(jnp.where(vals == mx, lane, V7X_LANES), axis=-1, keepdims=True)
        top_v.append(mx)
        top_i.append(idx)
        vals = jnp.where(lane == idx, -jnp.inf, vals)
    ex = [jnp.exp(v - top_v[0]) for v in top_v]
    tot = ex[0]
    for e in ex[1:]:
        tot = tot + e
    gate = jnp.zeros((tm, V7X_LANES), F32)
    tope = jnp.zeros((tm, V7X_LANES), I32)
    member = jnp.zeros((tm, V7X_LANES), F32)
    for k in range(TOP_K):
        gate = jnp.where(lane == k, ex[k] / tot, gate)
        tope = jnp.where(lane == k, top_i[k], tope)
        member = member + (lane == top_i[k]).astype(F32)
    gate_ref[rows, :] = gate
    tope_ref[rows, :] = tope
    return jnp.sum(member, axis=0, keepdims=True)


def _merge(dn_alpha, hm, hf, x2, wg, bg, wbm, wbf, wo, ln_g, ln_b, wr_hi, wr_lo, br):
    t, d = x2.shape
    tm = MERGE_ROWS
    row = lambda i: (i, 0)
    const = lambda i: (0, 0)
    full = lambda a: pl.BlockSpec(a.shape, const)
    return pl.pallas_call(
        functools.partial(_merge_kernel, dn_alpha),
        grid=(t // tm,),
        in_specs=[
            pl.BlockSpec((tm, M_W), row),
            pl.BlockSpec((tm, F_W), row),
            pl.BlockSpec((tm, d), row),
            full(wg), full(bg), full(wbm), full(wbf), full(wo), full(ln_g), full(ln_b),
            full(wr_hi), full(wr_lo), full(br),
        ],
        out_specs=(
            pl.BlockSpec((tm, d), row),
            pl.BlockSpec((tm, d // 2), row),
            pl.BlockSpec((tm, V7X_LANES), row),
            pl.BlockSpec((tm, V7X_LANES), row),
            pl.BlockSpec((V7X_SUBLANES, V7X_LANES), row),
        ),
        out_shape=(
            jax.ShapeDtypeStruct((t, d), F32),
            jax.ShapeDtypeStruct((t, d // 2), U32),
            jax.ShapeDtypeStruct((t, V7X_LANES), F32),
            jax.ShapeDtypeStruct((t, V7X_LANES), I32),
            jax.ShapeDtypeStruct((t // tm * V7X_SUBLANES, V7X_LANES), F32),
        ),
        compiler_params=_params(("parallel",)),
        name="merge_ln1_router",
    )(hm, hf, x2, wg, bg, wbm, wbf, wo, ln_g, ln_b, wr_hi, wr_lo, br)


def _lane_cumsum(x):
    lane = lax.broadcasted_iota(I32, x.shape, 1)
    d = 1
    while d < V7X_LANES:
        x = x + jnp.where(lane >= d, pltpu.roll(x, shift=d, axis=1), 0.0)
        d *= 2
    return x


def _routing_kernel(cnt_ref, tope_ref, dest_ref, table_ref, run_ref, start_ref):
    sb = ROUTE_SUB_ROWS

    @pl.when(pl.program_id(0) == 0)
    def _():
        total = jnp.sum(cnt_ref[...], axis=0, keepdims=True)
        counts = jnp.broadcast_to(total, (V7X_SUBLANES, V7X_LANES))
        padded = jnp.ceil(counts * (1.0 / MOE_BLOCK)) * MOE_BLOCK
        pad_end = _lane_cumsum(padded)
        pad_start = pad_end - padded
        start_ref[...] = pad_start
        run_ref[...] = jnp.zeros(run_ref.shape, F32)
        nb = table_ref.shape[0]
        blk = lax.broadcasted_iota(I32, (nb, V7X_LANES), 0).astype(F32) * MOE_BLOCK
        ln = lax.broadcasted_iota(I32, (nb, V7X_LANES), 1)
        done = jnp.logical_and(pad_end[0:1, :] <= blk, ln < N_EXPERTS)
        be = jnp.minimum(jnp.sum(done.astype(F32), axis=-1, keepdims=True), N_EXPERTS - 1.0)
        onehot = ln == be.astype(I32)
        cnt_e = jnp.sum(jnp.where(onehot, counts[0:1, :], 0.0), axis=-1, keepdims=True)
        start_e = jnp.sum(jnp.where(onehot, pad_start[0:1, :], 0.0), axis=-1, keepdims=True)
        valid = jnp.clip(cnt_e - (blk[:, 0:1] - start_e), 0.0, float(MOE_BLOCK))
        table_ref[...] = jnp.where(ln == 0, be.astype(I32),
                                   jnp.where(ln == 1, valid.astype(I32), 0))

    earlier = _tril_mask(sb, strict=True).astype(BF16)
    lane = lax.broadcasted_iota(I32, (sb, V7X_LANES), 1)
    for j in range(tope_ref.shape[0] // sb):
        tope = tope_ref[j * sb:(j + 1) * sb, :]
        hit = [lane == tope[:, k:k + 1] for k in range(TOP_K)]
        member = jnp.zeros((sb, V7X_LANES), F32)
        for k in range(TOP_K):
            member = member + hit[k].astype(F32)
        base = _dot(earlier, member.astype(BF16)) + (run_ref[0:1, :] + start_ref[0:1, :])
        dest = jnp.zeros((sb, V7X_LANES), I32)
        for k in range(TOP_K):
            dk = jnp.sum(jnp.where(hit[k], base, 0.0), axis=-1, keepdims=True)
            dest = jnp.where(lane == k, dk.astype(I32), dest)
        dest_ref[j * sb:(j + 1) * sb, :] = dest
        run_ref[...] = run_ref[...] + jnp.sum(member, axis=0, keepdims=True)


def _routing(tile_counts, tope, n_blocks):
    t = tope.shape[0]
    tr = ROUTE_ROWS
    return pl.pallas_call(
        _routing_kernel,
        grid=(t // tr,),
        in_specs=[pl.BlockSpec(tile_counts.shape, lambda i: (0, 0)),
                  pl.BlockSpec((tr, V7X_LANES), lambda i: (i, 0))],
        out_specs=(
            pl.BlockSpec((tr, V7X_LANES), lambda i: (i, 0)),
            pl.BlockSpec((n_blocks, V7X_LANES), lambda i: (0, 0)),
        ),
        out_shape=(
            jax.ShapeDtypeStruct((t, V7X_LANES), I32),
            jax.ShapeDtypeStruct((n_blocks, V7X_LANES), I32),
        ),
        scratch_shapes=[
            pltpu.VMEM((V7X_SUBLANES, V7X_LANES), F32),
            pltpu.VMEM((V7X_SUBLANES, V7X_LANES), F32),
        ],
        compiler_params=_params(("arbitrary",)),
        name="routing",
    )(tile_counts, tope)


def _sc_worker_id():
    return lax.axis_index("s") * V7X_SC_CORES + lax.axis_index("c")


def _sc_mesh():
    return plsc.VectorSubcoreMesh(core_axis_name="c", subcore_axis_name="s",
                                  num_cores=V7X_SC_CORES, num_subcores=V7X_SC_SUBCORES)


def _sc_dispatch(dest_km, h1p, n_rows):
    t, w = h1p.shape
    per_worker = t // V7X_SC_WORKERS
    ch = SC_ROWS_PER_DMA

    @functools.partial(
        pl.kernel, mesh=_sc_mesh(),
        out_type=jax.ShapeDtypeStruct((n_rows, w), h1p.dtype),
        scratch_types=[pltpu.VMEM((ch,), I32), pltpu.VMEM((ch, w), h1p.dtype), pltpu.SemaphoreType.DMA],
        name="sc_dispatch",
    )
    def scatter_rows(dest_hbm, h1p_hbm, xs_hbm, idx_v, rows_v, sem):
        first = _sc_worker_id() * per_worker

        @pl.loop(0, per_worker // ch)
        def _(j):
            base = first + j * ch
            pltpu.sync_copy(h1p_hbm.at[pl.ds(base, ch)], rows_v)
            for k in range(TOP_K):
                pltpu.sync_copy(dest_hbm.at[pl.ds(k * t + base, ch)], idx_v)
                pltpu.async_copy(rows_v, xs_hbm.at[idx_v], sem).wait()

    return scatter_rows(dest_km, h1p)


def _sc_gather(dest_km, y_rows):
    n = dest_km.shape[0]
    w = y_rows.shape[1]
    per_worker = n // V7X_SC_WORKERS
    ch = SC_ROWS_PER_DMA

    @functools.partial(
        pl.kernel, mesh=_sc_mesh(),
        out_type=jax.ShapeDtypeStruct((n, w), y_rows.dtype),
        scratch_types=[pltpu.VMEM((ch,), I32), pltpu.VMEM((ch, w), y_rows.dtype), pltpu.SemaphoreType.DMA],
        name="sc_gather",
    )
    def gather_rows(dest_hbm, y_hbm, out_hbm, idx_v, rows_v, sem):
        first = _sc_worker_id() * per_worker

        @pl.loop(0, per_worker // ch)
        def _(j):
            base = first + j * ch
            pltpu.sync_copy(dest_hbm.at[pl.ds(base, ch)], idx_v)
            pltpu.async_copy(y_hbm.at[idx_v], rows_v, sem).wait()
            pltpu.sync_copy(rows_v, out_hbm.at[pl.ds(base, ch)])

    return gather_rows(dest_km, y_rows)


def _expert_kernel(be_ref, nv_ref, xs_ref, wgu_f32_ref, bgu_ref, wdn_f32_ref, bdn_ref, y_ref,
                   wgu_ref, wdn_ref):
    i = pl.program_id(0)
    nv = nv_ref[i]
    half = MOE_BLOCK // 2

    @pl.when(jnp.logical_or(i == 0, be_ref[i] != be_ref[jnp.maximum(i - 1, 0)]))
    def _():
        wgu_ref[...] = wgu_f32_ref[...].astype(BF16)
        wdn_ref[...] = wdn_f32_ref[...].astype(BF16)

    def ffn(rows):
        x = _unpack_bf16_pairs(xs_ref[rows, :])
        rowid = lax.broadcasted_iota(I32, x.shape, 0)
        x = jnp.where(rowid < nv, x, 0.0).astype(BF16)
        gu = _dot(x, wgu_ref[...]) + bgu_ref[...]
        f = gu.shape[1] // 2
        glu = jnp.minimum(gu[:, :f], SWIGLU_LIMIT)
        lin = jnp.clip(gu[:, f:], -SWIGLU_LIMIT, SWIGLU_LIMIT)
        act = glu * _sigmoid(SWIGLU_ALPHA * glu) * (lin + 1.0)
        y_ref[rows, :] = _pack_bf16_pairs(_dot(act.astype(BF16), wdn_ref[...]) + bdn_ref[...])

    @pl.when(nv == 0)
    def _():
        y_ref[...] = jnp.zeros(y_ref.shape, U32)

    @pl.when(jnp.logical_and(nv > 0, nv <= half))
    def _():
        ffn(slice(0, half))
        y_ref[half:, :] = jnp.zeros((MOE_BLOCK - half, y_ref.shape[1]), U32)

    @pl.when(nv > half)
    def _():
        ffn(slice(0, MOE_BLOCK))


def _experts(block_e, block_valid, xs, wgu, bgu, wdn, bdn):
    n_rows, w = xs.shape
    e, d, f2 = wgu.shape
    n_blocks = n_rows // MOE_BLOCK
    grid_spec = pltpu.PrefetchScalarGridSpec(
        num_scalar_prefetch=2,
        grid=(n_blocks,),
        in_specs=[
            pl.BlockSpec((MOE_BLOCK, w), lambda i, be, nv: (i, 0)),
            pl.BlockSpec((None, d, f2), lambda i, be, nv: (be[i], 0, 0)),
            pl.BlockSpec((None, 1, f2), lambda i, be, nv: (be[i], 0, 0)),
            pl.BlockSpec((None, f2 // 2, d), lambda i, be, nv: (be[i], 0, 0)),
            pl.BlockSpec((None, 1, d), lambda i, be, nv: (be[i], 0, 0)),
        ],
        out_specs=pl.BlockSpec((MOE_BLOCK, d // 2), lambda i, be, nv: (i, 0)),
        scratch_shapes=[pltpu.VMEM((d, f2), BF16), pltpu.VMEM((f2 // 2, d), BF16)],
    )
    return pl.pallas_call(
        _expert_kernel,
        grid_spec=grid_spec,
        out_shape=jax.ShapeDtypeStruct((n_rows, d // 2), U32),
        compiler_params=pltpu.CompilerParams(dimension_semantics=("arbitrary",),
                                             vmem_limit_bytes=EXPERT_VMEM_LIMIT_BYTES),
        name="experts",
    )(block_e, block_valid, xs, wgu, bgu, wdn, bdn)


def _combine_kernel(dn_alpha, h1_ref, gate_ref, g_ref, b_ref, yg_ref, o_ref):
    gate = gate_ref[...]
    ffn = gate[:, 0:1] * _unpack_bf16_pairs(yg_ref[0])
    for k in range(1, TOP_K):
        ffn = ffn + gate[:, k:k + 1] * _unpack_bf16_pairs(yg_ref[k])
    o_ref[...] = _layer_norm_rows(dn_alpha * h1_ref[...] + ffn, g_ref[...], b_ref[...])


def _combine(dn_alpha, h1, gate, ln_g, ln_b, yg):
    t, d = h1.shape
    tc = COMBINE_ROWS
    row = lambda i: (i, 0)
    const = lambda i: (0, 0)
    return pl.pallas_call(
        functools.partial(_combine_kernel, dn_alpha),
        grid=(t // tc,),
        in_specs=[
            pl.BlockSpec((tc, d), row),
            pl.BlockSpec((tc, V7X_LANES), row),
            pl.BlockSpec((1, d), const),
            pl.BlockSpec((1, d), const),
            pl.BlockSpec((TOP_K, tc, d // 2), lambda i: (0, i, 0)),
        ],
        out_specs=pl.BlockSpec((tc, d), row),
        out_shape=jax.ShapeDtypeStruct((t, d), F32),
        compiler_params=_params(("parallel",)),
        name="combine_ln2",
    )(h1, gate, ln_g, ln_b, yg)


def _pack_in_proj(w_in, b_in):
    d = w_in.shape[0]
    o = 0
    cols = {}
    for name, width in (("mqk", MQK_W), ("mv", M_W), ("mo", M_W), ("mi", M_HEADS), ("mf", M_HEADS),
                        ("fq", F_W), ("fk", F_W), ("fv", F_W), ("ff", F_HEADS), ("gm", d), ("gf", d)):
        cols[name] = (o, o + width)
        o += width

    def take(a, names):
        return [a[..., cols[n][0]:cols[n][1]] for n in names]

    n_gate = 2 * M_HEADS + F_HEADS
    main = ("mqk", "mv", "mo", "fq", "fk", "fv", "mi", "mf", "ff")
    w_main = jnp.concatenate(take(w_in, main) + [jnp.zeros((d, GATE_COLS - n_gate), w_in.dtype)], axis=1)
    b_main = jnp.concatenate(take(b_in, main) + [jnp.zeros((GATE_COLS - n_gate,), b_in.dtype)])
    w_gate = jnp.concatenate(take(w_in, ("gm", "gf")), axis=1)
    b_gate = jnp.concatenate(take(b_in, ("gm", "gf")))
    return w_main.astype(BF16), b_main[None, :], w_gate.astype(BF16), b_gate[None, :]


def _layer(h, depth, w_in, b_in, m_conv_w, m_conv_b, m_norm_g, w_bm, w_bf, w_o, ln1_g, ln1_b,
           w_router, b_router, w_gu, b_gu, w_dn, b_dn, ln2_g, ln2_b):
    bsz, s, d = h.shape
    t = bsz * s
    dn_alpha = (2.0 * depth) ** 0.25
    x2 = h.reshape(t, d)

    w_main, b_main, w_gate, b_gate = _pack_in_proj(w_in, b_in)
    mq, mk, mv, mo, fq, fk, fv, gates = _inproj(x2, w_main, b_main, m_conv_w, m_conv_b[None, :], s)
    ccol, crel_k, crel_q = _fox_cumsum(gates, bsz, s)
    hm = _mlstm(mq, mk, mv, mo, gates, m_norm_g, bsz, s)
    hf = _fox_attention(fq, fk, fv, ccol, crel_k, crel_q, bsz, s)

    n_exp = w_router.shape[1]
    wr = jnp.zeros((d, V7X_LANES), F32).at[:, :n_exp].set(w_router)
    wr_hi = wr.astype(BF16)
    wr_lo = (wr - wr_hi.astype(F32)).astype(BF16)
    br = jnp.zeros((1, V7X_LANES), F32).at[0, :n_exp].set(b_router)
    h1, h1p, gate, tope, tile_counts = _merge(
        dn_alpha, hm, hf, x2, w_gate, b_gate, w_bm.astype(BF16), w_bf.astype(BF16), w_o.astype(BF16),
        ln1_g[None, :], ln1_b[None, :], wr_hi, wr_lo, br)

    n_blocks = -(-(t * TOP_K) // MOE_BLOCK) + N_EXPERTS
    dest, table = _routing(tile_counts, tope, n_blocks)
    dest_km = dest[:, :TOP_K].T.reshape(TOP_K * t)
    block_e, block_valid = table[:, 0], table[:, 1]
    xs = _sc_dispatch(dest_km, h1p, n_blocks * MOE_BLOCK)
    y_rows = _experts(block_e, block_valid, xs, w_gu, b_gu[:, None, :], w_dn, b_dn[:, None, :])
    yg = _sc_gather(dest_km, y_rows).reshape(TOP_K, t, d // 2)
    out = _combine(dn_alpha, h1, gate, ln2_g[None, :], ln2_b[None, :], yg)
    return out.reshape(bsz, s, d)


def kernel(x, w_in, b_in, m_conv_w, m_conv_b, m_norm_g, w_bm, w_bf, w_o, ln1_g, ln1_b,
           w_router, b_router, w_gu, b_gu, w_dn, b_dn, ln2_g, ln2_b):
    depth = w_in.shape[0]
    h = x
    for l in range(depth):
        h = _layer(h, depth, w_in[l], b_in[l], m_conv_w[l], m_conv_b[l], m_norm_g[l], w_bm[l], w_bf[l],
                   w_o[l], ln1_g[l], ln1_b[l], w_router[l], b_router[l], w_gu[l], b_gu[l], w_dn[l],
                   b_dn[l], ln2_g[l], ln2_b[l])
    return h
```

```python
import functools
import math

import jax
import jax.numpy as jnp
from jax import lax
from jax.experimental import pallas as pl
from jax.experimental.pallas import tpu as pltpu
from jax.experimental.pallas import tpu_sc as plsc

F32 = jnp.float32
BF16 = jnp.bfloat16
I32 = jnp.int32
U32 = jnp.uint32

M_HEADS = 4
M_DQK = 128
M_DV = 128
CONV_W = 4
F_HEADS = 8
F_DH = 64
N_EXPERTS = 32
TOP_K = 4
SWIGLU_ALPHA = 1.702
SWIGLU_LIMIT = 7.0
LN_EPS = 1e-5

M_W = M_HEADS * M_DV
F_W = F_HEADS * F_DH
MQK_W = 2 * M_HEADS * M_DQK

V7X_LANES = 128
V7X_SUBLANES = 8
V7X_VMEM_BYTES = 64 * 1024 * 1024
VMEM_LIMIT_BYTES = (V7X_VMEM_BYTES * 3) // 4
EXPERT_VMEM_LIMIT_BYTES = (V7X_VMEM_BYTES * 7) // 8
V7X_SC_CORES = 2
V7X_SC_SUBCORES = 16
V7X_SC_WORKERS = V7X_SC_CORES * V7X_SC_SUBCORES

INPROJ_ROWS = 512
MLSTM_CHUNK = 256
MLSTM_SEQS = 2
FOX_Q_BLOCK = 512
FOX_K_BLOCK = 256
MERGE_ROWS = 512
MERGE_SUB_ROWS = 256
ROUTE_ROWS = 1024
ROUTE_SUB_ROWS = 256
MOE_BLOCK = 512
SC_SCATTER_ROWS = 128
SC_GATHER_ROWS = 64
SC_GATHER_WAYS = 2
COMBINE_ROWS = 512

GATE_COLS = V7X_LANES
MI_LANE = 0
MF_LANE = M_HEADS
FF_LANE = 2 * M_HEADS


def _params(semantics):
    return pltpu.CompilerParams(dimension_semantics=semantics, vmem_limit_bytes=VMEM_LIMIT_BYTES)


def _log_sigmoid(x):
    return jnp.minimum(x, 0.0) - jnp.log1p(jnp.exp(-jnp.abs(x)))


def _sigmoid(x):
    return 1.0 / (1.0 + jnp.exp(-x))


def _dot(a, b):
    return jnp.dot(a, b, preferred_element_type=F32)


def _dot_nt(a, b):
    return lax.dot_general(a, b, (((1,), (1,)), ((), ())), preferred_element_type=F32)


def _dot_tn(a, b):
    return lax.dot_general(a, b, (((0,), (0,)), ((), ())), preferred_element_type=F32)


def _split3(x):
    hi = x.astype(BF16)
    r1 = x - hi.astype(F32)
    mid = r1.astype(BF16)
    lo = (r1 - mid.astype(F32)).astype(BF16)
    return hi, mid, lo


def _dot_mask_f32(mask_bf16, x):
    hi, mid, lo = _split3(x)
    return (_dot(mask_bf16, lo) + _dot(mask_bf16, mid)) + _dot(mask_bf16, hi)


def _pack_bf16_pairs(x):
    half = x.shape[1] // 2
    bits = lax.bitcast_convert_type(x.astype(BF16).astype(F32), U32)
    return (bits[:, :half] >> 16) | bits[:, half:]


def _unpack_bf16_pairs(words):
    lo = lax.bitcast_convert_type(words << 16, F32)
    hi = lax.bitcast_convert_type(words & jnp.uint32(0xFFFF0000), F32)
    return jnp.concatenate([lo, hi], axis=1)


def _tril_mask(n, strict=False):
    r = lax.broadcasted_iota(I32, (n, n), 0)
    c = lax.broadcasted_iota(I32, (n, n), 1)
    return (r > c) if strict else (r >= c)


_OFF_MQK = 0
_OFF_MV = _OFF_MQK + MQK_W
_OFF_MO = _OFF_MV + M_W
_OFF_FQ = _OFF_MO + M_W
_OFF_FK = _OFF_FQ + F_W
_OFF_FV = _OFF_FK + F_W
_OFF_GATES = _OFF_FV + F_W
_PACKED_COLS = _OFF_GATES + GATE_COLS


def _inproj_kernel(tiles_per_seq, x_ref, w_ref, b_ref, cw_ref, cb_ref,
                   mq_ref, mk_ref, mv_ref, mo_ref, fq_ref, fk_ref, fv_ref, gates_ref, ext_ref):
    tm = x_ref.shape[0]
    pad = V7X_SUBLANES
    xb = x_ref[...].astype(BF16)

    def seg(lo, width):
        return _dot(xb, w_ref[:, lo:lo + width]) + b_ref[:, lo:lo + width]

    @pl.when(pl.program_id(0) % tiles_per_seq == 0)
    def _():
        ext_ref[0:pad, :] = jnp.zeros((pad, MQK_W), F32)

    ext_ref[pad:pad + tm, :] = seg(_OFF_MQK, MQK_W)
    y = cb_ref[...] + cw_ref[CONV_W - 1:CONV_W, :] * ext_ref[pad:pad + tm, :]
    for k in range(CONV_W - 1):
        shift = CONV_W - 1 - k
        y = y + cw_ref[k:k + 1, :] * ext_ref[pad - shift:pad - shift + tm, :]
    ext_ref[0:pad, :] = ext_ref[tm:tm + pad, :]
    act = y * _sigmoid(y)
    mq_ref[...] = act[:, :MQK_W // 2].astype(BF16)
    mk_ref[...] = act[:, MQK_W // 2:] * (M_DQK ** -0.5)
    mv_ref[...] = seg(_OFF_MV, M_W).astype(BF16)
    mo_ref[...] = seg(_OFF_MO, M_W)
    fq_ref[...] = (seg(_OFF_FQ, F_W) * (F_DH ** -0.5)).astype(BF16)
    fk_ref[...] = seg(_OFF_FK, F_W).astype(BF16)
    fv_ref[...] = seg(_OFF_FV, F_W).astype(BF16)
    gates_ref[...] = seg(_OFF_GATES, GATE_COLS)


def _inproj(x2, w_packed, b_packed, conv_w, conv_b, s):
    t, d = x2.shape
    tm = INPROJ_ROWS
    row = lambda i: (i, 0)
    const = lambda i: (0, 0)
    out_shapes = (
        jax.ShapeDtypeStruct((t, MQK_W // 2), BF16),
        jax.ShapeDtypeStruct((t, MQK_W // 2), F32),
        jax.ShapeDtypeStruct((t, M_W), BF16),
        jax.ShapeDtypeStruct((t, M_W), F32),
        jax.ShapeDtypeStruct((t, F_W), BF16),
        jax.ShapeDtypeStruct((t, F_W), BF16),
        jax.ShapeDtypeStruct((t, F_W), BF16),
        jax.ShapeDtypeStruct((t, GATE_COLS), F32),
    )
    return pl.pallas_call(
        functools.partial(_inproj_kernel, s // tm),
        grid=(t // tm,),
        in_specs=[
            pl.BlockSpec((tm, d), row),
            pl.BlockSpec((d, _PACKED_COLS), const),
            pl.BlockSpec((1, _PACKED_COLS), const),
            pl.BlockSpec((CONV_W, MQK_W), const),
            pl.BlockSpec((1, MQK_W), const),
        ],
        out_specs=tuple(pl.BlockSpec((tm, o.shape[1]), row) for o in out_shapes),
        out_shape=out_shapes,
        scratch_shapes=[pltpu.VMEM((tm + V7X_SUBLANES, MQK_W), F32)],
        compiler_params=_params(("arbitrary",)),
        name="inproj",
    )(x2, w_packed, b_packed, conv_w, conv_b)


def _fox_cumsum_kernel(g_ref, ccol_ref, crel_k_ref, crel_q_ref):
    s = g_ref.shape[0]
    cb = FOX_K_BLOCK
    per_q = FOX_Q_BLOCK // cb
    tri = _tril_mask(cb).astype(BF16)
    carry = jnp.zeros((1, GATE_COLS), F32)
    for j in range(s // cb):
        rows = slice(j * cb, (j + 1) * cb)
        if j % per_q == 0:
            q_carry = carry
        within = _dot_mask_f32(tri, _log_sigmoid(g_ref[rows, :]))
        crel_k_ref[rows, :] = within
        crel_q_ref[rows, :] = within + (carry - q_carry)
        ccol_ref[rows, :] = within + carry
        carry = carry + within[cb - 1:cb, :]


def _fox_cumsum(gates, bsz, s):
    t = gates.shape[0]
    spec = pl.BlockSpec((s, GATE_COLS), lambda b: (b, 0))
    shape = jax.ShapeDtypeStruct((t, GATE_COLS), F32)
    return pl.pallas_call(
        _fox_cumsum_kernel,
        grid=(bsz,),
        in_specs=[spec],
        out_specs=(spec, spec, spec),
        out_shape=(shape, shape, shape),
        compiler_params=_params(("parallel",)),
        name="fox_cumsum",
    )(gates)


def _mlstm_kernel(mq_ref, mk_ref, mv_ref, mo_ref, gates_ref, ng_ref, hm_ref, state_ref, m_ref):
    @pl.when(pl.program_id(1) == 0)
    def _():
        state_ref[...] = jnp.zeros(state_ref.shape, F32)
        m_ref[...] = jnp.zeros(m_ref.shape, F32)

    seqs = range(mq_ref.shape[0])
    states = [[state_ref[bb, h] for h in range(M_HEADS)] for bb in seqs]
    maxes = [[m_ref[bb, h][0:1, 0:1] for h in range(M_HEADS)] for bb in seqs]
    results = [_mlstm_chunk(mq_ref.at[bb], mk_ref.at[bb], mv_ref.at[bb], mo_ref.at[bb], gates_ref.at[bb],
                            ng_ref, states[bb], maxes[bb]) for bb in seqs]
    for bb in seqs:
        for h, (out_h, state_h, m_h) in enumerate(results[bb]):
            hm_ref[bb, :, h * M_DV:(h + 1) * M_DV] = out_h
            state_ref[bb, h] = state_h
            m_ref[bb, h] = jnp.broadcast_to(m_h, m_ref.shape[2:])


def _mlstm_chunk(mq_ref, mk_ref, mv_ref, mo_ref, gates_ref, ng_ref, states, maxes):
    L = MLSTM_CHUNK
    reps = L // V7X_LANES
    results = []
    gates = gates_ref[...]
    bfull = _dot_mask_f32(_tril_mask(L).astype(BF16), _log_sigmoid(gates))
    b_rows = bfull.T
    z_all = gates - pltpu.roll(bfull, shift=V7X_LANES - (MF_LANE - MI_LANE), axis=1)
    visible = (lax.broadcasted_iota(I32, (L, L), 0) <= lax.broadcasted_iota(I32, (L, L), 1))
    ones_rows = (lax.broadcasted_iota(I32, (M_DV, L), 0) == 0).astype(BF16)

    for h in range(M_HEADS):
        b_row = b_rows[MF_LANE + h:MF_LANE + h + 1, :]
        g_tot = b_row[:, L - 1:L]
        m_prev = maxes[h]
        z = jnp.broadcast_to(z_all[:, MI_LANE + h:MI_LANE + h + 1], (L, V7X_LANES))

        q_h = mq_ref[:, h * M_DQK:(h + 1) * M_DQK]
        k_f = mk_ref[:, h * M_DQK:(h + 1) * M_DQK]
        k_h = k_f.astype(BF16)
        v_t = mv_ref[:, h * M_DV:(h + 1) * M_DV].astype(F32).T.astype(BF16)
        cn_t = states[h]

        dlog = jnp.where(visible, b_row + jnp.tile(z, (1, reps)), -jnp.inf)
        inter_log = b_row + m_prev
        m_t = jnp.maximum(inter_log, jnp.max(dlog, axis=0, keepdims=True))
        w_inter = jnp.exp(inter_log - m_t)
        qkw = _dot_nt(k_h, q_h) * jnp.exp(dlog - m_t)
        qc = _dot_nt(cn_t.astype(BF16), q_h)
        num = w_inter * qc[:M_DV, :] + _dot(v_t, qkw.astype(BF16))
        den = w_inter * qc[M_DV:M_DV + 1, :] + jnp.sum(qkw, axis=0, keepdims=True)
        hh = num / jnp.maximum(jnp.abs(den), jnp.exp(-m_t))

        mu = jnp.mean(hh, axis=0, keepdims=True)
        dv = hh - mu
        var = jnp.mean(dv * dv, axis=0, keepdims=True)
        hn = (dv * lax.rsqrt(var + LN_EPS)) * jnp.tile(ng_ref[h * M_DV:(h + 1) * M_DV, :], (1, reps))
        out_h = (_sigmoid(mo_ref[:, h * M_DV:(h + 1) * M_DV]) * hn.T).astype(BF16)

        a = g_tot + z
        m_new = jnp.maximum(g_tot + m_prev, jnp.max(a, axis=0, keepdims=True)[:, 0:1])
        decay = jnp.exp(g_tot + m_prev - m_new)
        kw = (k_f * jnp.exp(a - m_new)).astype(BF16)
        v_aug = jnp.concatenate([v_t, ones_rows], axis=0)
        results.append((out_h, decay * cn_t + _dot(v_aug, kw), m_new))
    return results


def _mlstm(mq, mk, mv, mo, gates, norm_g, bsz, s):
    t = mq.shape[0]
    L = MLSTM_CHUNK
    nb = MLSTM_SEQS
    seq = lambda a: a.reshape(bsz, s, a.shape[1])
    blk = lambda width: pl.BlockSpec((nb, L, width), lambda g, n: (g, n, 0))
    const = lambda g, n: (0, 0)
    hm = pl.pallas_call(
        _mlstm_kernel,
        grid=(bsz // nb, s // L),
        in_specs=[
            blk(MQK_W // 2), blk(MQK_W // 2), blk(M_W), blk(M_W), blk(GATE_COLS),
            pl.BlockSpec((M_W, V7X_LANES), const),
        ],
        out_specs=blk(M_W),
        out_shape=jax.ShapeDtypeStruct((bsz, s, M_W), BF16),
        scratch_shapes=[
            pltpu.VMEM((nb, M_HEADS, 2 * M_DV, M_DQK), F32),
            pltpu.VMEM((nb, M_HEADS, V7X_SUBLANES, V7X_LANES), F32),
        ],
        compiler_params=_params(("parallel", "arbitrary")),
        name="mlstm",
    )(seq(mq), seq(mk), seq(mv), seq(mo), seq(gates),
      jnp.broadcast_to(norm_g[:, None], (M_W, V7X_LANES)))
    return hm.reshape(t, M_W)


def _fox_operand(x, c_col, hh, key_side):
    rows = x.shape[0]
    lane = lax.broadcasted_iota(I32, (rows, V7X_LANES), 1)
    own = (lane < F_DH) if hh == 0 else (lane >= F_DH)
    base = F_DH if hh == 0 else 0
    parts = [part.astype(F32) for part in _split3(c_col)]
    if key_side:
        feats = parts + [1.0, 1.0, 1.0]
    else:
        feats = [-1.0, -1.0, -1.0] + parts
    out = jnp.where(own, x, 0.0)
    for n, feat in enumerate(feats):
        out = jnp.where(lane == base + n, feat, out)
    return out.astype(BF16)


def _fox_kernel(q_ref, k_ref, v_ref, ccol_ref, crel_k_ref, crel_q_ref, o_ref, qaug_ref, kaug_ref, vt_ref,
                st_a0, st_a1, st_b0, st_b1, pe_a0, pe_a1, pe_b0, pe_b1, acc0, acc1):
    blk = FOX_K_BLOCK
    tq = FOX_Q_BLOCK
    assert tq == 2 * blk
    strips = tq // V7X_LANES
    st_a_refs, st_b_refs = (st_a0, st_a1), (st_b0, st_b1)
    pe_a_refs, pe_b_refs = (pe_a0, pe_a1), (pe_b0, pe_b1)
    acc_refs = (acc0, acc1)
    p = pl.program_id(1)
    i = pl.program_id(2)

    @pl.when(i == 0)
    def _():
        lane = lax.broadcasted_iota(I32, crel_q_ref.shape, 1)
        q_all = q_ref[...].astype(F32)
        k_all = k_ref[...].astype(F32)
        for hh in range(2):
            mine = lane == FF_LANE + 2 * p + hh
            c_q = jnp.sum(jnp.where(mine, crel_q_ref[...], 0.0), axis=-1, keepdims=True)
            c_k = jnp.sum(jnp.where(mine, crel_k_ref[...], 0.0), axis=-1, keepdims=True)
            qaug_ref[hh] = _fox_operand(q_all, c_q, hh, False)
            kaug_ref[hh] = _fox_operand(k_all, c_k, hh, True)
        v_t = v_ref[...].astype(F32).T
        for j in range(vt_ref.shape[0]):
            vt_ref[j] = v_t[:, j * blk:(j + 1) * blk].astype(BF16)

    q_start = pl.multiple_of(i * tq, tq)
    q_heads = [qaug_ref[hh, pl.ds(q_start, tq), :] for hh in range(2)]
    key_row = lax.broadcasted_iota(I32, (blk, V7X_LANES), 0)
    query_col = lax.broadcasted_iota(I32, (blk, V7X_LANES), 1)
    last_chunk = 2 * i + 1
    head_lane = lax.broadcasted_iota(I32, (1, GATE_COLS), 1) - (FF_LANE + 2 * p)

    def c_before(position, hh):
        row = ccol_ref[pl.ds(jnp.maximum(position - 1, 0), 1), :]
        keep = jnp.logical_and(head_lane == hh, position > 0)
        return jnp.sum(jnp.where(keep, row, 0.0), axis=-1, keepdims=True)

    c_query0 = [c_before(q_start, hh) for hh in range(2)]

    def scores(j, hh):
        start = pl.multiple_of(j * blk, blk)
        return _dot_nt(kaug_ref[hh, pl.ds(start, blk), :], q_heads[hh])

    def softmax_update(st_ref, pe_ref, m_old, l_old, base, key_minus_query=None):
        alphas, ms, ls = [], [], []
        for c in range(strips):
            cols = slice(c * V7X_LANES, (c + 1) * V7X_LANES)
            gap = None if key_minus_query is None else key_minus_query - c * V7X_LANES
            if gap is not None and gap - (V7X_LANES - 1) > 0:
                pe_ref[:, cols] = jnp.zeros((blk, V7X_LANES), BF16)
                alphas.append(jnp.ones((1, V7X_LANES), F32))
                ms.append(m_old[:, cols])
                ls.append(l_old[:, cols])
                continue
            st = st_ref[:, cols]
            if gap is not None and gap + (blk - 1) > 0:
                st = jnp.where(key_row + gap <= query_col, st, -jnp.inf)
            m_new = jnp.maximum(m_old[:, cols], jnp.max(st, axis=0, keepdims=True) + base)
            alpha = jnp.exp(m_old[:, cols] - m_new)
            pe = jnp.exp(st - (m_new - base))
            pe_ref[:, cols] = pe.astype(BF16)
            alphas.append(alpha)
            ms.append(m_new)
            ls.append(alpha * l_old[:, cols] + jnp.sum(pe, axis=0, keepdims=True))
        cat = lambda parts: jnp.concatenate(parts, axis=1)
        return cat(alphas), cat(ms), cat(ls)

    def pair(mi, carry, diagonal=False):
        a = 2 * mi
        b = a + 1
        v_prev = vt_ref[jnp.maximum(a - 1, 0)]
        v_a = vt_ref[a]
        partial = []
        for hh in range(2):
            alpha_prev = carry[hh][0]
            partial.append(alpha_prev * acc_refs[hh][...] + _dot(v_prev, pe_b_refs[hh][...]))
            st_b_refs[hh][...] = scores(b, hh)
        stats = []
        for hh in range(2):
            _, m_old, l_old = carry[hh]
            stats.append(softmax_update(st_a_refs[hh], pe_a_refs[hh], m_old, l_old,
                                        c_query0[hh] - c_before(a * blk, hh), 0 if diagonal else None))
        for hh in range(2):
            alpha_a = stats[hh][0]
            acc_refs[hh][...] = alpha_a * partial[hh] + _dot(v_a, pe_a_refs[hh][...])
            if not diagonal:
                st_a_refs[hh][...] = scores(a + 2, hh)
        return tuple(softmax_update(st_b_refs[hh], pe_b_refs[hh], stats[hh][1], stats[hh][2],
                                    c_query0[hh] - c_before(b * blk, hh), blk if diagonal else None)
                     for hh in range(2))

    for hh in range(2):
        st_a_refs[hh][...] = scores(0, hh)
        pe_b_refs[hh][...] = jnp.zeros((blk, tq), BF16)
        acc_refs[hh][...] = jnp.zeros((V7X_LANES, tq), F32)
    init = tuple((jnp.ones((1, tq), F32), jnp.full((1, tq), -jnp.inf, F32), jnp.zeros((1, tq), F32))
                 for _ in range(2))
    final = pair(i, lax.fori_loop(0, i, pair, init), diagonal=True)
    v_last = vt_ref[last_chunk]
    outs = []
    for hh in range(2):
        alpha, _, l_fin = final[hh]
        outs.append((alpha * acc_refs[hh][...] + _dot(v_last, pe_b_refs[hh][...])) / l_fin)
    row = lax.broadcasted_iota(I32, (V7X_LANES, tq), 0)
    o_t = jnp.where(row < F_DH, outs[0], outs[1])
    o_ref[...] = o_t.T.astype(BF16)


def _fox_attention(fq, fk, fv, ccol, crel_k, crel_q, bsz, s):
    t = fq.shape[0]
    blk = FOX_K_BLOCK
    tq = FOX_Q_BLOCK
    nq = s // tq
    pairs = F_HEADS // 2
    qmap = lambda b, p, i: (b * nq + i, p)
    kvmap = lambda b, p, i: (b, p)
    return pl.pallas_call(
        _fox_kernel,
        grid=(bsz, pairs, nq),
        in_specs=[
            pl.BlockSpec((s, V7X_LANES), kvmap),
            pl.BlockSpec((s, V7X_LANES), kvmap),
            pl.BlockSpec((s, V7X_LANES), kvmap),
            pl.BlockSpec((s, GATE_COLS), lambda b, p, i: (b, 0)),
            pl.BlockSpec((s, GATE_COLS), lambda b, p, i: (b, 0)),
            pl.BlockSpec((s, GATE_COLS), lambda b, p, i: (b, 0)),
        ],
        out_specs=pl.BlockSpec((tq, V7X_LANES), qmap),
        out_shape=jax.ShapeDtypeStruct((t, F_W), BF16),
        scratch_shapes=[
            pltpu.VMEM((2, s, V7X_LANES), BF16),
            pltpu.VMEM((2, s, V7X_LANES), BF16),
            pltpu.VMEM((s // blk, V7X_LANES, blk), BF16),
        ] + [pltpu.VMEM((blk, tq), F32)] * 4 + [pltpu.VMEM((blk, tq), BF16)] * 4
          + [pltpu.VMEM((V7X_LANES, tq), F32)] * 2,
        compiler_params=_params(("parallel", "parallel", "arbitrary")),
        name="fox_attention",
    )(fq, fk, fv, ccol, crel_k, crel_q)


def _layer_norm_rows(r, g, b):
    mu = jnp.mean(r, axis=-1, keepdims=True)
    d = r - mu
    var = jnp.mean(d * d, axis=-1, keepdims=True)
    return (d * lax.rsqrt(var + LN_EPS)) * g + b


def _merge_kernel(dn_alpha, hm_ref, hf_ref, x_ref, wg_ref, bg_ref, wbm_ref, wbf_ref, wo_ref,
                  g_ref, b_ref, wrh_ref, wrl_ref, br_ref, h1_ref, h1p_ref, gate_ref, tope_ref, cnt_ref):
    counts = jnp.zeros((1, V7X_LANES), F32)
    for n in range(x_ref.shape[0] // MERGE_SUB_ROWS):
        rows = slice(n * MERGE_SUB_ROWS, (n + 1) * MERGE_SUB_ROWS)
        counts = counts + _merge_rows(dn_alpha, rows, hm_ref, hf_ref, x_ref, wg_ref, bg_ref, wbm_ref,
                                      wbf_ref, wo_ref, g_ref, b_ref, wrh_ref, wrl_ref, br_ref,
                                      h1_ref, h1p_ref, gate_ref, tope_ref)
    sub = lax.broadcasted_iota(I32, cnt_ref.shape, 0)
    cnt_ref[...] = jnp.where(sub == 0, counts, 0.0)


def _merge_rows(dn_alpha, rows, hm_ref, hf_ref, x_ref, wg_ref, bg_ref, wbm_ref, wbf_ref, wo_ref,
                g_ref, b_ref, wrh_ref, wrl_ref, br_ref, h1_ref, h1p_ref, gate_ref, tope_ref):
    x = x_ref[rows, :]
    d = x.shape[1]
    gmf = _dot(x.astype(BF16), wg_ref[...]) + bg_ref[...]
    ym = _dot(hm_ref[rows, :], wbm_ref[...])
    yf = _dot(hf_ref[rows, :], wbf_ref[...])
    y = _sigmoid(gmf[:, :d]) * ym + _sigmoid(gmf[:, d:]) * yf
    mix = _dot(y.astype(BF16), wo_ref[...])
    h1 = _layer_norm_rows(dn_alpha * x + mix, g_ref[...], b_ref[...])
    h1_ref[rows, :] = h1

    h1p_ref[rows, :] = _pack_bf16_pairs(h1)
    hb = h1.astype(BF16)

    lo = (h1 - hb.astype(F32)).astype(BF16)
    logits = (_dot(lo, wrh_ref[...]) + _dot(hb, wrl_ref[...])) + _dot(hb, wrh_ref[...]) + br_ref[...]
    tm = logits.shape[0]
    lane = lax.broadcasted_iota(I32, (tm, V7X_LANES), 1)
    vals = jnp.where(lane < N_EXPERTS, logits, -jnp.inf)
    top_v, top_i = [], []
    for _ in range(TOP_K):
        mx = jnp.max(vals, axis=-1, keepdims=True)
        idx = jnp.min(jnp.where(vals == mx, lane, V7X_LANES), axis=-1, keepdims=True)
        top_v.append(mx)
        top_i.append(idx)
        vals = jnp.where(lane == idx, -jnp.inf, vals)
    ex = [jnp.exp(v - top_v[0]) for v in top_v]
    tot = ex[0]
    for e in ex[1:]:
        tot = tot + e
    gate = jnp.zeros((tm, V7X_LANES), F32)
    tope = jnp.zeros((tm, V7X_LANES), I32)
    member = jnp.zeros((tm, V7X_LANES), F32)
    for k in range(TOP_K):
        gate = jnp.where(lane == k, ex[k] / tot, gate)
        tope = jnp.where(lane == k, top_i[k], tope)
        member = member + (lane == top_i[k]).astype(F32)
    gate_ref[rows, :] = gate
    tope_ref[rows, :] = tope
    return jnp.sum(member, axis=0, keepdims=True)


def _merge(dn_alpha, hm, hf, x2, wg, bg, wbm, wbf, wo, ln_g, ln_b, wr_hi, wr_lo, br):
    t, d = x2.shape
    tm = MERGE_ROWS
    row = lambda i: (i, 0)
    const = lambda i: (0, 0)
    full = lambda a: pl.BlockSpec(a.shape, const)
    return pl.pallas_call(
        functools.partial(_merge_kernel, dn_alpha),
        grid=(t // tm,),
        in_specs=[
            pl.BlockSpec((tm, M_W), row),
            pl.BlockSpec((tm, F_W), row),
            pl.BlockSpec((tm, d), row),
            full(wg), full(bg), full(wbm), full(wbf), full(wo), full(ln_g), full(ln_b),
            full(wr_hi), full(wr_lo), full(br),
        ],
        out_specs=(
            pl.BlockSpec((tm, d), row),
            pl.BlockSpec((tm, d // 2), row),
            pl.BlockSpec((tm, V7X_LANES), row),
            pl.BlockSpec((tm, V7X_LANES), row),
            pl.BlockSpec((V7X_SUBLANES, V7X_LANES), row),
        ),
        out_shape=(
            jax.ShapeDtypeStruct((t, d), F32),
            jax.ShapeDtypeStruct((t, d // 2), U32),
            jax.ShapeDtypeStruct((t, V7X_LANES), F32),
            jax.ShapeDtypeStruct((t, V7X_LANES), I32),
            jax.ShapeDtypeStruct((t // tm * V7X_SUBLANES, V7X_LANES), F32),
        ),
        compiler_params=_params(("parallel",)),
        name="merge_ln1_router",
    )(hm, hf, x2, wg, bg, wbm, wbf, wo, ln_g, ln_b, wr_hi, wr_lo, br)


def _lane_cumsum(x):
    lane = lax.broadcasted_iota(I32, x.shape, 1)
    d = 1
    while d < V7X_LANES:
        x = x + jnp.where(lane >= d, pltpu.roll(x, shift=d, axis=1), 0.0)
        d *= 2
    return x


def _routing_kernel(cnt_ref, tope_ref, dest_ref, table_ref, run_ref, start_ref):
    sb = ROUTE_SUB_ROWS

    @pl.when(pl.program_id(0) == 0)
    def _():
        total = jnp.sum(cnt_ref[...], axis=0, keepdims=True)
        counts = jnp.broadcast_to(total, (V7X_SUBLANES, V7X_LANES))
        padded = jnp.ceil(counts * (1.0 / MOE_BLOCK)) * MOE_BLOCK
        pad_end = _lane_cumsum(padded)
        pad_start = pad_end - padded
        start_ref[...] = pad_start
        run_ref[...] = jnp.zeros(run_ref.shape, F32)
        nb = table_ref.shape[0]
        blk = lax.broadcasted_iota(I32, (nb, V7X_LANES), 0).astype(F32) * MOE_BLOCK
        ln = lax.broadcasted_iota(I32, (nb, V7X_LANES), 1)
        done = jnp.logical_and(pad_end[0:1, :] <= blk, ln < N_EXPERTS)
        be = jnp.minimum(jnp.sum(done.astype(F32), axis=-1, keepdims=True), N_EXPERTS - 1.0)
        onehot = ln == be.astype(I32)
        cnt_e = jnp.sum(jnp.where(onehot, counts[0:1, :], 0.0), axis=-1, keepdims=True)
        start_e = jnp.sum(jnp.where(onehot, pad_start[0:1, :], 0.0), axis=-1, keepdims=True)
        valid = jnp.clip(cnt_e - (blk[:, 0:1] - start_e), 0.0, float(MOE_BLOCK))
        table_ref[...] = jnp.where(ln == 0, be.astype(I32),
                                   jnp.where(ln == 1, valid.astype(I32), 0))

    earlier = _tril_mask(sb, strict=True).astype(BF16)
    lane = lax.broadcasted_iota(I32, (sb, V7X_LANES), 1)
    for j in range(tope_ref.shape[0] // sb):
        tope = tope_ref[j * sb:(j + 1) * sb, :]
        hit = [lane == tope[:, k:k + 1] for k in range(TOP_K)]
        member = jnp.zeros((sb, V7X_LANES), F32)
        for k in range(TOP_K):
            member = member + hit[k].astype(F32)
        base = _dot(earlier, member.astype(BF16)) + (run_ref[0:1, :] + start_ref[0:1, :])
        dest = jnp.zeros((sb, V7X_LANES), I32)
        for k in range(TOP_K):
            dk = jnp.sum(jnp.where(hit[k], base, 0.0), axis=-1, keepdims=True)
            dest = jnp.where(lane == k, dk.astype(I32), dest)
        dest_ref[j * sb:(j + 1) * sb, :] = dest
        run_ref[...] = run_ref[...] + jnp.sum(member, axis=0, keepdims=True)


def _routing(tile_counts, tope, n_blocks):
    t = tope.shape[0]
    tr = ROUTE_ROWS
    return pl.pallas_call(
        _routing_kernel,
        grid=(t // tr,),
        in_specs=[pl.BlockSpec(tile_counts.shape, lambda i: (0, 0)),
                  pl.BlockSpec((tr, V7X_LANES), lambda i: (i, 0))],
        out_specs=(
            pl.BlockSpec((tr, V7X_LANES), lambda i: (i, 0)),
            pl.BlockSpec((n_blocks, V7X_LANES), lambda i: (0, 0)),
        ),
        out_shape=(
            jax.ShapeDtypeStruct((t, V7X_LANES), I32),
            jax.ShapeDtypeStruct((n_blocks, V7X_LANES), I32),
        ),
        scratch_shapes=[
            pltpu.VMEM((V7X_SUBLANES, V7X_LANES), F32),
            pltpu.VMEM((V7X_SUBLANES, V7X_LANES), F32),
        ],
        compiler_params=_params(("arbitrary",)),
        name="routing",
    )(tile_counts, tope)


def _sc_worker_id():
    return lax.axis_index("s") * V7X_SC_CORES + lax.axis_index("c")


def _sc_mesh():
    return plsc.VectorSubcoreMesh(core_axis_name="c", subcore_axis_name="s",
                                  num_cores=V7X_SC_CORES, num_subcores=V7X_SC_SUBCORES)


def _sc_dispatch(dest_km, h1p, n_rows):
    t, w = h1p.shape
    per_worker = t // V7X_SC_WORKERS
    ch = SC_SCATTER_ROWS

    @functools.partial(
        pl.kernel, mesh=_sc_mesh(),
        out_type=jax.ShapeDtypeStruct((n_rows, w), h1p.dtype),
        scratch_types=[pltpu.VMEM((ch, w), h1p.dtype)]
        + [pltpu.VMEM((ch,), I32)] * TOP_K + [pltpu.SemaphoreType.DMA] * TOP_K,
        name="sc_dispatch",
    )
    def scatter_rows(dest_hbm, h1p_hbm, xs_hbm, rows_v, *idx_and_sems):
        idx_refs, sems = idx_and_sems[:TOP_K], idx_and_sems[TOP_K:]
        first = _sc_worker_id() * per_worker

        @pl.loop(0, per_worker // ch)
        def _(j):
            base = first + j * ch
            pltpu.sync_copy(h1p_hbm.at[pl.ds(base, ch)], rows_v)
            copies = []
            for k in range(TOP_K):
                pltpu.sync_copy(dest_hbm.at[pl.ds(k * t + base, ch)], idx_refs[k])
                copies.append(pltpu.async_copy(rows_v, xs_hbm.at[idx_refs[k]], sems[k]))
            for copy in copies:
                copy.wait()

    return scatter_rows(dest_km, h1p)


def _sc_gather(dest_km, y_rows):
    n = dest_km.shape[0]
    w = y_rows.shape[1]
    per_worker = n // V7X_SC_WORKERS
    ch = SC_GATHER_ROWS
    ways = SC_GATHER_WAYS

    @functools.partial(
        pl.kernel, mesh=_sc_mesh(),
        out_type=jax.ShapeDtypeStruct((n, w), y_rows.dtype),
        scratch_types=[pltpu.VMEM((ch,), I32)] * ways + [pltpu.VMEM((ch, w), y_rows.dtype)] * ways
        + [pltpu.SemaphoreType.DMA] * (2 * ways),
        name="sc_gather",
    )
    def gather_rows(dest_hbm, y_hbm, out_hbm, *scratch):
        idx_refs, row_refs = scratch[:ways], scratch[ways:2 * ways]
        gather_sems, store_sems = scratch[2 * ways:3 * ways], scratch[3 * ways:]
        first = _sc_worker_id() * per_worker

        @pl.loop(0, per_worker // (ch * ways))
        def _(j):
            bases = [first + (j * ways + u) * ch for u in range(ways)]
            gathers = []
            for u in range(ways):
                pltpu.sync_copy(dest_hbm.at[pl.ds(bases[u], ch)], idx_refs[u])
                gathers.append(pltpu.async_copy(y_hbm.at[idx_refs[u]], row_refs[u], gather_sems[u]))
            stores = []
            for u in range(ways):
                gathers[u].wait()
                stores.append(pltpu.async_copy(row_refs[u], out_hbm.at[pl.ds(bases[u], ch)], store_sems[u]))
            for store in stores:
                store.wait()

    return gather_rows(dest_km, y_rows)


def _expert_kernel(be_ref, nv_ref, xs_ref, wgu_f32_ref, bgu_ref, wdn_f32_ref, bdn_ref, y_ref,
                   wgu_ref, wdn_ref):
    i = pl.program_id(0)
    nv = nv_ref[i]
    half = MOE_BLOCK // 2

    @pl.when(jnp.logical_or(i == 0, be_ref[i] != be_ref[jnp.maximum(i - 1, 0)]))
    def _():
        wgu_ref[...] = wgu_f32_ref[...].astype(BF16)
        wdn_ref[...] = wdn_f32_ref[...].astype(BF16)

    def ffn(rows):
        x = _unpack_bf16_pairs(xs_ref[rows, :])
        rowid = lax.broadcasted_iota(I32, x.shape, 0)
        x = jnp.where(rowid < nv, x, 0.0).astype(BF16)
        gu = _dot(x, wgu_ref[...]) + bgu_ref[...]
        f = gu.shape[1] // 2
        glu = jnp.minimum(gu[:, :f], SWIGLU_LIMIT)
        lin = jnp.clip(gu[:, f:], -SWIGLU_LIMIT, SWIGLU_LIMIT)
        act = glu * _sigmoid(SWIGLU_ALPHA * glu) * (lin + 1.0)
        y_ref[rows, :] = _pack_bf16_pairs(_dot(act.astype(BF16), wdn_ref[...]) + bdn_ref[...])

    @pl.when(nv == 0)
    def _():
        y_ref[...] = jnp.zeros(y_ref.shape, U32)

    @pl.when(jnp.logical_and(nv > 0, nv <= half))
    def _():
        ffn(slice(0, half))
        y_ref[half:, :] = jnp.zeros((MOE_BLOCK - half, y_ref.shape[1]), U32)

    @pl.when(nv > half)
    def _():
        ffn(slice(0, MOE_BLOCK))


def _experts(block_e, block_valid, xs, wgu, bgu, wdn, bdn):
    n_rows, w = xs.shape
    e, d, f2 = wgu.shape
    n_blocks = n_rows // MOE_BLOCK
    grid_spec = pltpu.PrefetchScalarGridSpec(
        num_scalar_prefetch=2,
        grid=(n_blocks,),
        in_specs=[
            pl.BlockSpec((MOE_BLOCK, w), lambda i, be, nv: (i, 0)),
            pl.BlockSpec((None, d, f2), lambda i, be, nv: (be[i], 0, 0)),
            pl.BlockSpec((None, 1, f2), lambda i, be, nv: (be[i], 0, 0)),
            pl.BlockSpec((None, f2 // 2, d), lambda i, be, nv: (be[i], 0, 0)),
            pl.BlockSpec((None, 1, d), lambda i, be, nv: (be[i], 0, 0)),
        ],
        out_specs=pl.BlockSpec((MOE_BLOCK, d // 2), lambda i, be, nv: (i, 0)),
        scratch_shapes=[pltpu.VMEM((d, f2), BF16), pltpu.VMEM((f2 // 2, d), BF16)],
    )
    return pl.pallas_call(
        _expert_kernel,
        grid_spec=grid_spec,
        out_shape=jax.ShapeDtypeStruct((n_rows, d // 2), U32),
        compiler_params=pltpu.CompilerParams(dimension_semantics=("arbitrary",),
                                             vmem_limit_bytes=EXPERT_VMEM_LIMIT_BYTES),
        name="experts",
    )(block_e, block_valid, xs, wgu, bgu, wdn, bdn)


def _combine_kernel(dn_alpha, h1_ref, gate_ref, g_ref, b_ref, yg_ref, o_ref):
    gate = gate_ref[...]
    ffn = gate[:, 0:1] * _unpack_bf16_pairs(yg_ref[0])
    for k in range(1, TOP_K):
        ffn = ffn + gate[:, k:k + 1] * _unpack_bf16_pairs(yg_ref[k])
    o_ref[...] = _layer_norm_rows(dn_alpha * h1_ref[...] + ffn, g_ref[...], b_ref[...])


def _combine(dn_alpha, h1, gate, ln_g, ln_b, yg):
    t, d = h1.shape
    tc = COMBINE_ROWS
    row = lambda i: (i, 0)
    const = lambda i: (0, 0)
    return pl.pallas_call(
        functools.partial(_combine_kernel, dn_alpha),
        grid=(t // tc,),
        in_specs=[
            pl.BlockSpec((tc, d), row),
            pl.BlockSpec((tc, V7X_LANES), row),
            pl.BlockSpec((1, d), const),
            pl.BlockSpec((1, d), const),
            pl.BlockSpec((TOP_K, tc, d // 2), lambda i: (0, i, 0)),
        ],
        out_specs=pl.BlockSpec((tc, d), row),
        out_shape=jax.ShapeDtypeStruct((t, d), F32),
        compiler_params=_params(("parallel",)),
        name="combine_ln2",
    )(h1, gate, ln_g, ln_b, yg)


def _pack_in_proj(w_in, b_in):
    d = w_in.shape[0]
    o = 0
    cols = {}
    for name, width in (("mqk", MQK_W), ("mv", M_W), ("mo", M_W), ("mi", M_HEADS), ("mf", M_HEADS),
                        ("fq", F_W), ("fk", F_W), ("fv", F_W), ("ff", F_HEADS), ("gm", d), ("gf", d)):
        cols[name] = (o, o + width)
        o += width

    def take(a, names):
        return [a[..., cols[n][0]:cols[n][1]] for n in names]

    n_gate = 2 * M_HEADS + F_HEADS
    main = ("mqk", "mv", "mo", "fq", "fk", "fv", "mi", "mf", "ff")
    w_main = jnp.concatenate(take(w_in, main) + [jnp.zeros((d, GATE_COLS - n_gate), w_in.dtype)], axis=1)
    b_main = jnp.concatenate(take(b_in, main) + [jnp.zeros((GATE_COLS - n_gate,), b_in.dtype)])
    w_gate = jnp.concatenate(take(w_in, ("gm", "gf")), axis=1)
    b_gate = jnp.concatenate(take(b_in, ("gm", "gf")))
    return w_main.astype(BF16), b_main[None, :], w_gate.astype(BF16), b_gate[None, :]


def _layer(h, depth, w_in, b_in, m_conv_w, m_conv_b, m_norm_g, w_bm, w_bf, w_o, ln1_g, ln1_b,
           w_router, b_router, w_gu, b_gu, w_dn, b_dn, ln2_g, ln2_b):
    bsz, s, d = h.shape
    t = bsz * s
    dn_alpha = (2.0 * depth) ** 0.25
    x2 = h.reshape(t, d)

    w_main, b_main, w_gate, b_gate = _pack_in_proj(w_in, b_in)
    mq, mk, mv, mo, fq, fk, fv, gates = _inproj(x2, w_main, b_main, m_conv_w, m_conv_b[None, :], s)
    ccol, crel_k, crel_q = _fox_cumsum(gates, bsz, s)
    hm = _mlstm(mq, mk, mv, mo, gates, m_norm_g, bsz, s)
    hf = _fox_attention(fq, fk, fv, ccol, crel_k, crel_q, bsz, s)

    n_exp = w_router.shape[1]
    wr = jnp.zeros((d, V7X_LANES), F32).at[:, :n_exp].set(w_router)
    wr_hi = wr.astype(BF16)
    wr_lo = (wr - wr_hi.astype(F32)).astype(BF16)
    br = jnp.zeros((1, V7X_LANES), F32).at[0, :n_exp].set(b_router)
    h1, h1p, gate, tope, tile_counts = _merge(
        dn_alpha, hm, hf, x2, w_gate, b_gate, w_bm.astype(BF16), w_bf.astype(BF16), w_o.astype(BF16),
        ln1_g[None, :], ln1_b[None, :], wr_hi, wr_lo, br)

    n_blocks = -(-(t * TOP_K) // MOE_BLOCK) + N_EXPERTS
    dest, table = _routing(tile_counts, tope, n_blocks)
    dest_km = dest[:, :TOP_K].T.reshape(TOP_K * t)
    block_e, block_valid = table[:, 0], table[:, 1]
    xs = _sc_dispatch(dest_km, h1p, n_blocks * MOE_BLOCK)
    y_rows = _experts(block_e, block_valid, xs, w_gu, b_gu[:, None, :], w_dn, b_dn[:, None, :])
    yg = _sc_gather(dest_km, y_rows).reshape(TOP_K, t, d // 2)
    out = _combine(dn_alpha, h1, gate, ln2_g[None, :], ln2_b[None, :], yg)
    return out.reshape(bsz, s, d)


def kernel(x, w_in, b_in, m_conv_w, m_conv_b, m_norm_g, w_bm, w_bf, w_o, ln1_g, ln1_b,
           w_router, b_router, w_gu, b_gu, w_dn, b_dn, ln2_g, ln2_b):
    depth = w_in.shape[0]
    h = x
    for l in range(depth):
        h = _layer(h, depth, w_in[l], b_in[l], m_conv_w[l], m_conv_b[l], m_norm_g[l], w_bm[l], w_bf[l],
                   w_o[l], ln1_g[l], ln1_b[l], w_router[l], b_router[l], w_gu[l], b_gu[l], w_dn[l],
                   b_dn[l], ln2_g[l], ln2_b[l])
    return h
```

```python
import functools
import math

import jax
import jax.numpy as jnp
from jax import lax
from jax.experimental import pallas as pl
from jax.experimental.pallas import tpu as pltpu
from jax.experimental.pallas import tpu_sc as plsc

F32 = jnp.float32
BF16 = jnp.bfloat16
I32 = jnp.int32
U32 = jnp.uint32

M_HEADS = 4
M_DQK = 128
M_DV = 128
CONV_W = 4
F_HEADS = 8
F_DH = 64
N_EXPERTS = 32
TOP_K = 4
SWIGLU_ALPHA = 1.702
SWIGLU_LIMIT = 7.0
LN_EPS = 1e-5

M_W = M_HEADS * M_DV
F_W = F_HEADS * F_DH
MQK_W = 2 * M_HEADS * M_DQK

V7X_LANES = 128
V7X_SUBLANES = 8
V7X_VMEM_BYTES = 64 * 1024 * 1024
VMEM_LIMIT_BYTES = (V7X_VMEM_BYTES * 3) // 4
EXPERT_VMEM_LIMIT_BYTES = (V7X_VMEM_BYTES * 7) // 8
V7X_SC_CORES = 2
V7X_SC_SUBCORES = 16
V7X_SC_WORKERS = V7X_SC_CORES * V7X_SC_SUBCORES

INPROJ_ROWS = 512
MLSTM_CHUNK = 256
MLSTM_SEQS = 2
FOX_Q_BLOCK = 512
FOX_K_BLOCK = 256
MERGE_ROWS = 512
MERGE_SUB_ROWS = 256
ROUTE_ROWS = 1024
ROUTE_SUB_ROWS = 256
MOE_BLOCK = 512
SC_SCATTER_ROWS = 128
SC_GATHER_ROWS = 64
SC_GATHER_WAYS = 2
COMBINE_ROWS = 512

GATE_COLS = V7X_LANES
MI_LANE = 0
MF_LANE = M_HEADS
FF_LANE = 2 * M_HEADS


def _params(semantics):
    return pltpu.CompilerParams(dimension_semantics=semantics, vmem_limit_bytes=VMEM_LIMIT_BYTES)


def _log_sigmoid(x):
    return jnp.minimum(x, 0.0) - jnp.log1p(jnp.exp(-jnp.abs(x)))


def _sigmoid(x):
    return 1.0 / (1.0 + jnp.exp(-x))


def _dot(a, b):
    return jnp.dot(a, b, preferred_element_type=F32)


def _dot_nt(a, b):
    return lax.dot_general(a, b, (((1,), (1,)), ((), ())), preferred_element_type=F32)


def _dot_tn(a, b):
    return lax.dot_general(a, b, (((0,), (0,)), ((), ())), preferred_element_type=F32)


def _split3(x):
    hi = x.astype(BF16)
    r1 = x - hi.astype(F32)
    mid = r1.astype(BF16)
    lo = (r1 - mid.astype(F32)).astype(BF16)
    return hi, mid, lo


def _dot_mask_f32(mask_bf16, x):
    hi, mid, lo = _split3(x)
    return (_dot(mask_bf16, lo) + _dot(mask_bf16, mid)) + _dot(mask_bf16, hi)


def _pack_bf16_pairs(x):
    half = x.shape[1] // 2
    bits = lax.bitcast_convert_type(x.astype(BF16).astype(F32), U32)
    return (bits[:, :half] >> 16) | bits[:, half:]


def _unpack_bf16_pairs(words):
    lo = lax.bitcast_convert_type(words << 16, F32)
    hi = lax.bitcast_convert_type(words & jnp.uint32(0xFFFF0000), F32)
    return jnp.concatenate([lo, hi], axis=1)


def _tril_mask(n, strict=False):
    r = lax.broadcasted_iota(I32, (n, n), 0)
    c = lax.broadcasted_iota(I32, (n, n), 1)
    return (r > c) if strict else (r >= c)


_OFF_MQK = 0
_OFF_MV = _OFF_MQK + MQK_W
_OFF_MO = _OFF_MV + M_W
_OFF_FQ = _OFF_MO + M_W
_OFF_FK = _OFF_FQ + F_W
_OFF_FV = _OFF_FK + F_W
_OFF_GATES = _OFF_FV + F_W
_PACKED_COLS = _OFF_GATES + GATE_COLS


def _inproj_kernel(tiles_per_seq, x_ref, w_ref, b_ref, cw_ref, cb_ref,
                   mq_ref, mk_ref, mv_ref, mo_ref, fq_ref, fk_ref, fv_ref, gates_ref, ext_ref):
    tm = x_ref.shape[0]
    pad = V7X_SUBLANES
    xb = x_ref[...].astype(BF16)

    def seg(lo, width):
        return _dot(xb, w_ref[:, lo:lo + width]) + b_ref[:, lo:lo + width]

    @pl.when(pl.program_id(0) % tiles_per_seq == 0)
    def _():
        ext_ref[0:pad, :] = jnp.zeros((pad, MQK_W), F32)

    ext_ref[pad:pad + tm, :] = seg(_OFF_MQK, MQK_W)
    y = cb_ref[...] + cw_ref[CONV_W - 1:CONV_W, :] * ext_ref[pad:pad + tm, :]
    for k in range(CONV_W - 1):
        shift = CONV_W - 1 - k
        y = y + cw_ref[k:k + 1, :] * ext_ref[pad - shift:pad - shift + tm, :]
    ext_ref[0:pad, :] = ext_ref[tm:tm + pad, :]
    act = y * _sigmoid(y)
    mq_ref[...] = act[:, :MQK_W // 2].astype(BF16)
    mk_ref[...] = act[:, MQK_W // 2:] * (M_DQK ** -0.5)
    mv_ref[...] = seg(_OFF_MV, M_W).astype(BF16)
    mo_ref[...] = seg(_OFF_MO, M_W)
    fq_ref[...] = (seg(_OFF_FQ, F_W) * (F_DH ** -0.5)).astype(BF16)
    fk_ref[...] = seg(_OFF_FK, F_W).astype(BF16)
    fv_ref[...] = seg(_OFF_FV, F_W).astype(BF16)
    gates_ref[...] = seg(_OFF_GATES, GATE_COLS)


def _inproj(x2, w_packed, b_packed, conv_w, conv_b, s):
    t, d = x2.shape
    tm = INPROJ_ROWS
    row = lambda i: (i, 0)
    const = lambda i: (0, 0)
    out_shapes = (
        jax.ShapeDtypeStruct((t, MQK_W // 2), BF16),
        jax.ShapeDtypeStruct((t, MQK_W // 2), F32),
        jax.ShapeDtypeStruct((t, M_W), BF16),
        jax.ShapeDtypeStruct((t, M_W), F32),
        jax.ShapeDtypeStruct((t, F_W), BF16),
        jax.ShapeDtypeStruct((t, F_W), BF16),
        jax.ShapeDtypeStruct((t, F_W), BF16),
        jax.ShapeDtypeStruct((t, GATE_COLS), F32),
    )
    return pl.pallas_call(
        functools.partial(_inproj_kernel, s // tm),
        grid=(t // tm,),
        in_specs=[
            pl.BlockSpec((tm, d), row),
            pl.BlockSpec((d, _PACKED_COLS), const),
            pl.BlockSpec((1, _PACKED_COLS), const),
            pl.BlockSpec((CONV_W, MQK_W), const),
            pl.BlockSpec((1, MQK_W), const),
        ],
        out_specs=tuple(pl.BlockSpec((tm, o.shape[1]), row) for o in out_shapes),
        out_shape=out_shapes,
        scratch_shapes=[pltpu.VMEM((tm + V7X_SUBLANES, MQK_W), F32)],
        compiler_params=_params(("arbitrary",)),
        name="inproj",
    )(x2, w_packed, b_packed, conv_w, conv_b)


def _fox_cumsum_kernel(g_ref, ccol_ref, crel_k_ref, crel_q_ref):
    s = g_ref.shape[0]
    cb = FOX_K_BLOCK
    per_q = FOX_Q_BLOCK // cb
    tri = _tril_mask(cb).astype(BF16)
    carry = jnp.zeros((1, GATE_COLS), F32)
    for j in range(s // cb):
        rows = slice(j * cb, (j + 1) * cb)
        if j % per_q == 0:
            q_carry = carry
        within = _dot_mask_f32(tri, _log_sigmoid(g_ref[rows, :]))
        crel_k_ref[rows, :] = within
        crel_q_ref[rows, :] = within + (carry - q_carry)
        ccol_ref[rows, :] = within + carry
        carry = carry + within[cb - 1:cb, :]


def _fox_cumsum(gates, bsz, s):
    t = gates.shape[0]
    spec = pl.BlockSpec((s, GATE_COLS), lambda b: (b, 0))
    shape = jax.ShapeDtypeStruct((t, GATE_COLS), F32)
    return pl.pallas_call(
        _fox_cumsum_kernel,
        grid=(bsz,),
        in_specs=[spec],
        out_specs=(spec, spec, spec),
        out_shape=(shape, shape, shape),
        compiler_params=_params(("parallel",)),
        name="fox_cumsum",
    )(gates)


def _mlstm_kernel(mq_ref, mk_ref, mv_ref, mo_ref, gates_ref, ng_ref, hm_ref, state_ref, m_ref):
    @pl.when(pl.program_id(1) == 0)
    def _():
        state_ref[...] = jnp.zeros(state_ref.shape, F32)
        m_ref[...] = jnp.zeros(m_ref.shape, F32)

    seqs = range(mq_ref.shape[0])
    states = [[state_ref[bb, h] for h in range(M_HEADS)] for bb in seqs]
    maxes = [[m_ref[bb, h][0:1, 0:1] for h in range(M_HEADS)] for bb in seqs]
    results = [_mlstm_chunk(mq_ref.at[bb], mk_ref.at[bb], mv_ref.at[bb], mo_ref.at[bb], gates_ref.at[bb],
                            ng_ref, states[bb], maxes[bb]) for bb in seqs]
    for bb in seqs:
        for h, (out_h, state_h, m_h) in enumerate(results[bb]):
            hm_ref[bb, :, h * M_DV:(h + 1) * M_DV] = out_h
            state_ref[bb, h] = state_h
            m_ref[bb, h] = jnp.broadcast_to(m_h, m_ref.shape[2:])


def _mlstm_chunk(mq_ref, mk_ref, mv_ref, mo_ref, gates_ref, ng_ref, states, maxes):
    L = MLSTM_CHUNK
    reps = L // V7X_LANES
    results = []
    gates = gates_ref[...]
    bfull = _dot_mask_f32(_tril_mask(L).astype(BF16), _log_sigmoid(gates))
    b_rows = bfull.T
    z_all = gates - pltpu.roll(bfull, shift=V7X_LANES - (MF_LANE - MI_LANE), axis=1)
    visible = (lax.broadcasted_iota(I32, (L, L), 0) <= lax.broadcasted_iota(I32, (L, L), 1))
    ones_rows = (lax.broadcasted_iota(I32, (M_DV, L), 0) == 0).astype(BF16)

    for h in range(M_HEADS):
        b_row = b_rows[MF_LANE + h:MF_LANE + h + 1, :]
        g_tot = b_row[:, L - 1:L]
        m_prev = maxes[h]
        z = jnp.broadcast_to(z_all[:, MI_LANE + h:MI_LANE + h + 1], (L, V7X_LANES))

        q_h = mq_ref[:, h * M_DQK:(h + 1) * M_DQK]
        k_f = mk_ref[:, h * M_DQK:(h + 1) * M_DQK]
        k_h = k_f.astype(BF16)
        v_t = mv_ref[:, h * M_DV:(h + 1) * M_DV].astype(F32).T.astype(BF16)
        cn_t = states[h]

        dlog = jnp.where(visible, b_row + jnp.tile(z, (1, reps)), -jnp.inf)
        inter_log = b_row + m_prev
        m_t = jnp.maximum(inter_log, jnp.max(dlog, axis=0, keepdims=True))
        w_inter = jnp.exp(inter_log - m_t)
        qkw = _dot_nt(k_h, q_h) * jnp.exp(dlog - m_t)
        qc = _dot_nt(cn_t.astype(BF16), q_h)
        num = w_inter * qc[:M_DV, :] + _dot(v_t, qkw.astype(BF16))
        den = w_inter * qc[M_DV:M_DV + 1, :] + jnp.sum(qkw, axis=0, keepdims=True)
        hh = num / jnp.maximum(jnp.abs(den), jnp.exp(-m_t))

        mu = jnp.mean(hh, axis=0, keepdims=True)
        dv = hh - mu
        var = jnp.mean(dv * dv, axis=0, keepdims=True)
        hn = (dv * lax.rsqrt(var + LN_EPS)) * jnp.tile(ng_ref[h * M_DV:(h + 1) * M_DV, :], (1, reps))
        out_h = (_sigmoid(mo_ref[:, h * M_DV:(h + 1) * M_DV]) * hn.T).astype(BF16)

        a = g_tot + z
        m_new = jnp.maximum(g_tot + m_prev, jnp.max(a, axis=0, keepdims=True)[:, 0:1])
        decay = jnp.exp(g_tot + m_prev - m_new)
        kw = (k_f * jnp.exp(a - m_new)).astype(BF16)
        v_aug = jnp.concatenate([v_t, ones_rows], axis=0)
        results.append((out_h, decay * cn_t + _dot(v_aug, kw), m_new))
    return results


def _mlstm(mq, mk, mv, mo, gates, norm_g, bsz, s):
    t = mq.shape[0]
    L = MLSTM_CHUNK
    nb = MLSTM_SEQS
    seq = lambda a: a.reshape(bsz, s, a.shape[1])
    blk = lambda width: pl.BlockSpec((nb, L, width), lambda g, n: (g, n, 0))
    const = lambda g, n: (0, 0)
    hm = pl.pallas_call(
        _mlstm_kernel,
        grid=(bsz // nb, s // L),
        in_specs=[
            blk(MQK_W // 2), blk(MQK_W // 2), blk(M_W), blk(M_W), blk(GATE_COLS),
            pl.BlockSpec((M_W, V7X_LANES), const),
        ],
        out_specs=blk(M_W),
        out_shape=jax.ShapeDtypeStruct((bsz, s, M_W), BF16),
        scratch_shapes=[
            pltpu.VMEM((nb, M_HEADS, 2 * M_DV, M_DQK), F32),
            pltpu.VMEM((nb, M_HEADS, V7X_SUBLANES, V7X_LANES), F32),
        ],
        compiler_params=_params(("parallel", "arbitrary")),
        name="mlstm",
    )(seq(mq), seq(mk), seq(mv), seq(mo), seq(gates),
      jnp.broadcast_to(norm_g[:, None], (M_W, V7X_LANES)))
    return hm.reshape(t, M_W)


def _fox_operand(x, c_col, hh, key_side):
    rows = x.shape[0]
    lane = lax.broadcasted_iota(I32, (rows, V7X_LANES), 1)
    own = (lane < F_DH) if hh == 0 else (lane >= F_DH)
    base = F_DH if hh == 0 else 0
    parts = [part.astype(F32) for part in _split3(c_col)]
    if key_side:
        feats = parts + [1.0, 1.0, 1.0]
    else:
        feats = [-1.0, -1.0, -1.0] + parts
    out = jnp.where(own, x, 0.0)
    for n, feat in enumerate(feats):
        out = jnp.where(lane == base + n, feat, out)
    return out.astype(BF16)


def _fox_kernel(q_ref, k_ref, v_ref, ccol_ref, crel_k_ref, crel_q_ref, o_ref, qaug_ref, kaug_ref, vt_ref,
                st_a0, st_a1, st_b0, st_b1, pe_a0, pe_a1, pe_b0, pe_b1, acc0, acc1):
    blk = FOX_K_BLOCK
    tq = FOX_Q_BLOCK
    assert tq == 2 * blk
    strips = tq // V7X_LANES
    st_a_refs, st_b_refs = (st_a0, st_a1), (st_b0, st_b1)
    pe_a_refs, pe_b_refs = (pe_a0, pe_a1), (pe_b0, pe_b1)
    acc_refs = (acc0, acc1)
    p = pl.program_id(1)
    i = pl.program_id(2)

    @pl.when(i == 0)
    def _():
        lane = lax.broadcasted_iota(I32, crel_q_ref.shape, 1)
        q_all = q_ref[...].astype(F32)
        k_all = k_ref[...].astype(F32)
        for hh in range(2):
            mine = lane == FF_LANE + 2 * p + hh
            c_q = jnp.sum(jnp.where(mine, crel_q_ref[...], 0.0), axis=-1, keepdims=True)
            c_k = jnp.sum(jnp.where(mine, crel_k_ref[...], 0.0), axis=-1, keepdims=True)
            qaug_ref[hh] = _fox_operand(q_all, c_q, hh, False)
            kaug_ref[hh] = _fox_operand(k_all, c_k, hh, True)
        v_t = v_ref[...].astype(F32).T
        for j in range(vt_ref.shape[0]):
            vt_ref[j] = v_t[:, j * blk:(j + 1) * blk].astype(BF16)

    q_start = pl.multiple_of(i * tq, tq)
    q_heads = [qaug_ref[hh, pl.ds(q_start, tq), :] for hh in range(2)]
    key_row = lax.broadcasted_iota(I32, (blk, V7X_LANES), 0)
    query_col = lax.broadcasted_iota(I32, (blk, V7X_LANES), 1)
    last_chunk = 2 * i + 1
    head_lane = lax.broadcasted_iota(I32, (1, GATE_COLS), 1) - (FF_LANE + 2 * p)

    def c_before(position, hh):
        row = ccol_ref[pl.ds(jnp.maximum(position - 1, 0), 1), :]
        keep = jnp.logical_and(head_lane == hh, position > 0)
        return jnp.sum(jnp.where(keep, row, 0.0), axis=-1, keepdims=True)

    c_query0 = [c_before(q_start, hh) for hh in range(2)]

    def scores(j, hh):
        start = pl.multiple_of(j * blk, blk)
        return _dot_nt(kaug_ref[hh, pl.ds(start, blk), :], q_heads[hh])

    def softmax_update(st_ref, pe_ref, m_old, l_old, base, key_minus_query=None):
        alphas, ms, ls = [], [], []
        for c in range(strips):
            cols = slice(c * V7X_LANES, (c + 1) * V7X_LANES)
            gap = None if key_minus_query is None else key_minus_query - c * V7X_LANES
            if gap is not None and gap - (V7X_LANES - 1) > 0:
                pe_ref[:, cols] = jnp.zeros((blk, V7X_LANES), BF16)
                alphas.append(jnp.ones((1, V7X_LANES), F32))
                ms.append(m_old[:, cols])
                ls.append(l_old[:, cols])
                continue
            st = st_ref[:, cols]
            if gap is not None and gap + (blk - 1) > 0:
                st = jnp.where(key_row + gap <= query_col, st, -jnp.inf)
            m_new = jnp.maximum(m_old[:, cols], jnp.max(st, axis=0, keepdims=True) + base)
            alpha = jnp.exp(m_old[:, cols] - m_new)
            pe = jnp.exp(st - (m_new - base))
            pe_ref[:, cols] = pe.astype(BF16)
            alphas.append(alpha)
            ms.append(m_new)
            ls.append(alpha * l_old[:, cols] + jnp.sum(pe, axis=0, keepdims=True))
        cat = lambda parts: jnp.concatenate(parts, axis=1)
        return cat(alphas), cat(ms), cat(ls)

    def pair(mi, carry, diagonal=False):
        a = 2 * mi
        b = a + 1
        v_prev = vt_ref[jnp.maximum(a - 1, 0)]
        v_a = vt_ref[a]
        partial = []
        for hh in range(2):
            alpha_prev = carry[hh][0]
            partial.append(alpha_prev * acc_refs[hh][...] + _dot(v_prev, pe_b_refs[hh][...]))
            st_b_refs[hh][...] = scores(b, hh)
        stats = []
        for hh in range(2):
            _, m_old, l_old = carry[hh]
            stats.append(softmax_update(st_a_refs[hh], pe_a_refs[hh], m_old, l_old,
                                        c_query0[hh] - c_before(a * blk, hh), 0 if diagonal else None))
        for hh in range(2):
            alpha_a = stats[hh][0]
            acc_refs[hh][...] = alpha_a * partial[hh] + _dot(v_a, pe_a_refs[hh][...])
            if not diagonal:
                st_a_refs[hh][...] = scores(a + 2, hh)
        return tuple(softmax_update(st_b_refs[hh], pe_b_refs[hh], stats[hh][1], stats[hh][2],
                                    c_query0[hh] - c_before(b * blk, hh), blk if diagonal else None)
                     for hh in range(2))

    for hh in range(2):
        st_a_refs[hh][...] = scores(0, hh)
        pe_b_refs[hh][...] = jnp.zeros((blk, tq), BF16)
        acc_refs[hh][...] = jnp.zeros((V7X_LANES, tq), F32)
    init = tuple((jnp.ones((1, tq), F32), jnp.full((1, tq), -jnp.inf, F32), jnp.zeros((1, tq), F32))
                 for _ in range(2))
    final = pair(i, lax.fori_loop(0, i, pair, init), diagonal=True)
    v_last = vt_ref[last_chunk]
    outs = []
    for hh in range(2):
        alpha, _, l_fin = final[hh]
        outs.append((alpha * acc_refs[hh][...] + _dot(v_last, pe_b_refs[hh][...])) / l_fin)
    row = lax.broadcasted_iota(I32, (V7X_LANES, tq), 0)
    o_t = jnp.where(row < F_DH, outs[0], outs[1])
    o_ref[...] = o_t.T.astype(BF16)


def _fox_attention(fq, fk, fv, ccol, crel_k, crel_q, bsz, s):
    t = fq.shape[0]
    blk = FOX_K_BLOCK
    tq = FOX_Q_BLOCK
    nq = s // tq
    pairs = F_HEADS // 2
    qmap = lambda b, p, i: (b * nq + i, p)
    kvmap = lambda b, p, i: (b, p)
    return pl.pallas_call(
        _fox_kernel,
        grid=(bsz, pairs, nq),
        in_specs=[
            pl.BlockSpec((s, V7X_LANES), kvmap),
            pl.BlockSpec((s, V7X_LANES), kvmap),
            pl.BlockSpec((s, V7X_LANES), kvmap),
            pl.BlockSpec((s, GATE_COLS), lambda b, p, i: (b, 0)),
            pl.BlockSpec((s, GATE_COLS), lambda b, p, i: (b, 0)),
            pl.BlockSpec((s, GATE_COLS), lambda b, p, i: (b, 0)),
        ],
        out_specs=pl.BlockSpec((tq, V7X_LANES), qmap),
        out_shape=jax.ShapeDtypeStruct((t, F_W), BF16),
        scratch_shapes=[
            pltpu.VMEM((2, s, V7X_LANES), BF16),
            pltpu.VMEM((2, s, V7X_LANES), BF16),
            pltpu.VMEM((s // blk, V7X_LANES, blk), BF16),
        ] + [pltpu.VMEM((blk, tq), F32)] * 4 + [pltpu.VMEM((blk, tq), BF16)] * 4
          + [pltpu.VMEM((V7X_LANES, tq), F32)] * 2,
        compiler_params=_params(("parallel", "parallel", "arbitrary")),
        name="fox_attention",
    )(fq, fk, fv, ccol, crel_k, crel_q)


def _layer_norm_rows(r, g, b):
    mu = jnp.mean(r, axis=-1, keepdims=True)
    d = r - mu
    var = jnp.mean(d * d, axis=-1, keepdims=True)
    return (d * lax.rsqrt(var + LN_EPS)) * g + b


def _merge_kernel(dn_alpha, hm_ref, hf_ref, x_ref, wg_ref, bg_ref, wbm_ref, wbf_ref, wo_ref,
                  g_ref, b_ref, wrh_ref, wrl_ref, br_ref, h1_ref, h1p_ref, gate_ref, tope_ref, cnt_ref,
                  resid_ref):
    i = pl.program_id(0)
    slot = i % 2

    @pl.when(i == 0)
    def _():
        resid_ref[1] = jnp.zeros(resid_ref.shape[1:], F32)

    subs = [slice(n * MERGE_SUB_ROWS, (n + 1) * MERGE_SUB_ROWS)
            for n in range(x_ref.shape[0] // MERGE_SUB_ROWS)]
    counts = jnp.zeros((1, V7X_LANES), F32)
    for rows in subs:
        counts = counts + _merge_tail(resid_ref[1 - slot, rows, :], rows, g_ref, b_ref, wrh_ref, wrl_ref,
                                      br_ref, h1_ref, h1p_ref, gate_ref, tope_ref)
    sub = lax.broadcasted_iota(I32, cnt_ref.shape, 0)
    cnt_ref[...] = jnp.where(sub == 0, counts, 0.0)

    for rows in subs:
        x = x_ref[rows, :]
        d = x.shape[1]
        gmf = _dot(x.astype(BF16), wg_ref[...]) + bg_ref[...]
        ym = _dot(hm_ref[rows, :], wbm_ref[...])
        yf = _dot(hf_ref[rows, :], wbf_ref[...])
        y = _sigmoid(gmf[:, :d]) * ym + _sigmoid(gmf[:, d:]) * yf
        resid_ref[slot, rows, :] = dn_alpha * x + _dot(y.astype(BF16), wo_ref[...])


def _merge_tail(resid, rows, g_ref, b_ref, wrh_ref, wrl_ref, br_ref, h1_ref, h1p_ref, gate_ref, tope_ref):
    h1 = _layer_norm_rows(resid, g_ref[...], b_ref[...])
    h1_ref[rows, :] = h1

    h1p_ref[rows, :] = _pack_bf16_pairs(h1)
    hb = h1.astype(BF16)

    lo = (h1 - hb.astype(F32)).astype(BF16)
    logits = (_dot(lo, wrh_ref[...]) + _dot(hb, wrl_ref[...])) + _dot(hb, wrh_ref[...]) + br_ref[...]
    tm = logits.shape[0]
    lane = lax.broadcasted_iota(I32, (tm, V7X_LANES), 1)
    vals = jnp.where(lane < N_EXPERTS, logits, -jnp.inf)
    top_v, top_i = [], []
    for _ in range(TOP_K):
        mx = jnp.max(vals, axis=-1, keepdims=True)
        idx = jnp.min(jnp.where(vals == mx, lane, V7X_LANES), axis=-1, keepdims=True)
        top_v.append(mx)
        top_i.append(idx)
        vals = jnp.where(lane == idx, -jnp.inf, vals)
    ex = [jnp.exp(v - top_v[0]) for v in top_v]
    tot = ex[0]
    for e in ex[1:]:
        tot = tot + e
    gate = jnp.zeros((tm, V7X_LANES), F32)
    tope = jnp.zeros((tm, V7X_LANES), I32)
    member = jnp.zeros((tm, V7X_LANES), F32)
    for k in range(TOP_K):
        gate = jnp.where(lane == k, ex[k] / tot, gate)
        tope = jnp.where(lane == k, top_i[k], tope)
        member = member + (lane == top_i[k]).astype(F32)
    gate_ref[rows, :] = gate
    tope_ref[rows, :] = tope
    return jnp.sum(member, axis=0, keepdims=True)


def _merge(dn_alpha, hm, hf, x2, wg, bg, wbm, wbf, wo, ln_g, ln_b, wr_hi, wr_lo, br):
    t, d = x2.shape
    tm = MERGE_ROWS
    nt = t // tm
    row = lambda i: (jnp.minimum(i, nt - 1), 0)
    out_row = lambda i: (jnp.maximum(i - 1, 0), 0)
    const = lambda i: (0, 0)
    full = lambda a: pl.BlockSpec(a.shape, const)
    return pl.pallas_call(
        functools.partial(_merge_kernel, dn_alpha),
        grid=(nt + 1,),
        in_specs=[
            pl.BlockSpec((tm, M_W), row),
            pl.BlockSpec((tm, F_W), row),
            pl.BlockSpec((tm, d), row),
            full(wg), full(bg), full(wbm), full(wbf), full(wo), full(ln_g), full(ln_b),
            full(wr_hi), full(wr_lo), full(br),
        ],
        out_specs=(
            pl.BlockSpec((tm, d), out_row),
            pl.BlockSpec((tm, d // 2), out_row),
            pl.BlockSpec((tm, V7X_LANES), out_row),
            pl.BlockSpec((tm, V7X_LANES), out_row),
            pl.BlockSpec((V7X_SUBLANES, V7X_LANES), out_row),
        ),
        out_shape=(
            jax.ShapeDtypeStruct((t, d), F32),
            jax.ShapeDtypeStruct((t, d // 2), U32),
            jax.ShapeDtypeStruct((t, V7X_LANES), F32),
            jax.ShapeDtypeStruct((t, V7X_LANES), I32),
            jax.ShapeDtypeStruct((t // tm * V7X_SUBLANES, V7X_LANES), F32),
        ),
        scratch_shapes=[pltpu.VMEM((2, tm, d), F32)],
        compiler_params=_params(("arbitrary",)),
        name="merge_ln1_router",
    )(hm, hf, x2, wg, bg, wbm, wbf, wo, ln_g, ln_b, wr_hi, wr_lo, br)


def _lane_cumsum(x):
    lane = lax.broadcasted_iota(I32, x.shape, 1)
    d = 1
    while d < V7X_LANES:
        x = x + jnp.where(lane >= d, pltpu.roll(x, shift=d, axis=1), 0.0)
        d *= 2
    return x


def _routing_kernel(cnt_ref, tope_ref, dest_ref, table_ref, run_ref, start_ref):
    sb = ROUTE_SUB_ROWS

    @pl.when(pl.program_id(0) == 0)
    def _():
        total = jnp.sum(cnt_ref[...], axis=0, keepdims=True)
        counts = jnp.broadcast_to(total, (V7X_SUBLANES, V7X_LANES))
        padded = jnp.ceil(counts * (1.0 / MOE_BLOCK)) * MOE_BLOCK
        pad_end = _lane_cumsum(padded)
        pad_start = pad_end - padded
        start_ref[...] = pad_start
        run_ref[...] = jnp.zeros(run_ref.shape, F32)
        nb = table_ref.shape[0]
        blk = lax.broadcasted_iota(I32, (nb, V7X_LANES), 0).astype(F32) * MOE_BLOCK
        ln = lax.broadcasted_iota(I32, (nb, V7X_LANES), 1)
        done = jnp.logical_and(pad_end[0:1, :] <= blk, ln < N_EXPERTS)
        be = jnp.minimum(jnp.sum(done.astype(F32), axis=-1, keepdims=True), N_EXPERTS - 1.0)
        onehot = ln == be.astype(I32)
        cnt_e = jnp.sum(jnp.where(onehot, counts[0:1, :], 0.0), axis=-1, keepdims=True)
        start_e = jnp.sum(jnp.where(onehot, pad_start[0:1, :], 0.0), axis=-1, keepdims=True)
        valid = jnp.clip(cnt_e - (blk[:, 0:1] - start_e), 0.0, float(MOE_BLOCK))
        table_ref[...] = jnp.where(ln == 0, be.astype(I32),
                                   jnp.where(ln == 1, valid.astype(I32), 0))

    earlier = _tril_mask(sb, strict=True).astype(BF16)
    lane = lax.broadcasted_iota(I32, (sb, V7X_LANES), 1)
    for j in range(tope_ref.shape[0] // sb):
        tope = tope_ref[j * sb:(j + 1) * sb, :]
        hit = [lane == tope[:, k:k + 1] for k in range(TOP_K)]
        member = jnp.zeros((sb, V7X_LANES), F32)
        for k in range(TOP_K):
            member = member + hit[k].astype(F32)
        base = _dot(earlier, member.astype(BF16)) + (run_ref[0:1, :] + start_ref[0:1, :])
        dest = jnp.zeros((sb, V7X_LANES), I32)
        for k in range(TOP_K):
            dk = jnp.sum(jnp.where(hit[k], base, 0.0), axis=-1, keepdims=True)
            dest = jnp.where(lane == k, dk.astype(I32), dest)
        dest_ref[j * sb:(j + 1) * sb, :] = dest
        run_ref[...] = run_ref[...] + jnp.sum(member, axis=0, keepdims=True)


def _routing(tile_counts, tope, n_blocks):
    t = tope.shape[0]
    tr = ROUTE_ROWS
    return pl.pallas_call(
        _routing_kernel,
        grid=(t // tr,),
        in_specs=[pl.BlockSpec(tile_counts.shape, lambda i: (0, 0)),
                  pl.BlockSpec((tr, V7X_LANES), lambda i: (i, 0))],
        out_specs=(
            pl.BlockSpec((tr, V7X_LANES), lambda i: (i, 0)),
            pl.BlockSpec((n_blocks, V7X_LANES), lambda i: (0, 0)),
        ),
        out_shape=(
            jax.ShapeDtypeStruct((t, V7X_LANES), I32),
            jax.ShapeDtypeStruct((n_blocks, V7X_LANES), I32),
        ),
        scratch_shapes=[
            pltpu.VMEM((V7X_SUBLANES, V7X_LANES), F32),
            pltpu.VMEM((V7X_SUBLANES, V7X_LANES), F32),
        ],
        compiler_params=_params(("arbitrary",)),
        name="routing",
    )(tile_counts, tope)


def _sc_worker_id():
    return lax.axis_index("s") * V7X_SC_CORES + lax.axis_index("c")


def _sc_mesh():
    return plsc.VectorSubcoreMesh(core_axis_name="c", subcore_axis_name="s",
                                  num_cores=V7X_SC_CORES, num_subcores=V7X_SC_SUBCORES)


def _sc_dispatch(dest_km, h1p, n_rows):
    t, w = h1p.shape
    per_worker = t // V7X_SC_WORKERS
    ch = SC_SCATTER_ROWS

    @functools.partial(
        pl.kernel, mesh=_sc_mesh(),
        out_type=jax.ShapeDtypeStruct((n_rows, w), h1p.dtype),
        scratch_types=[pltpu.VMEM((ch, w), h1p.dtype)]
        + [pltpu.VMEM((ch,), I32)] * TOP_K + [pltpu.SemaphoreType.DMA] * TOP_K,
        name="sc_dispatch",
    )
    def scatter_rows(dest_hbm, h1p_hbm, xs_hbm, rows_v, *idx_and_sems):
        idx_refs, sems = idx_and_sems[:TOP_K], idx_and_sems[TOP_K:]
        first = _sc_worker_id() * per_worker

        @pl.loop(0, per_worker // ch)
        def _(j):
            base = first + j * ch
            pltpu.sync_copy(h1p_hbm.at[pl.ds(base, ch)], rows_v)
            copies = []
            for k in range(TOP_K):
                pltpu.sync_copy(dest_hbm.at[pl.ds(k * t + base, ch)], idx_refs[k])
                copies.append(pltpu.async_copy(rows_v, xs_hbm.at[idx_refs[k]], sems[k]))
            for copy in copies:
                copy.wait()

    return scatter_rows(dest_km, h1p)


def _sc_gather(dest_km, y_rows):
    n = dest_km.shape[0]
    w = y_rows.shape[1]
    per_worker = n // V7X_SC_WORKERS
    ch = SC_GATHER_ROWS
    ways = SC_GATHER_WAYS

    @functools.partial(
        pl.kernel, mesh=_sc_mesh(),
        out_type=jax.ShapeDtypeStruct((n, w), y_rows.dtype),
        scratch_types=[pltpu.VMEM((ch,), I32)] * ways + [pltpu.VMEM((ch, w), y_rows.dtype)] * ways
        + [pltpu.SemaphoreType.DMA] * (2 * ways),
        name="sc_gather",
    )
    def gather_rows(dest_hbm, y_hbm, out_hbm, *scratch):
        idx_refs, row_refs = scratch[:ways], scratch[ways:2 * ways]
        gather_sems, store_sems = scratch[2 * ways:3 * ways], scratch[3 * ways:]
        first = _sc_worker_id() * per_worker

        @pl.loop(0, per_worker // (ch * ways))
        def _(j):
            bases = [first + (j * ways + u) * ch for u in range(ways)]
            gathers = []
            for u in range(ways):
                pltpu.sync_copy(dest_hbm.at[pl.ds(bases[u], ch)], idx_refs[u])
                gathers.append(pltpu.async_copy(y_hbm.at[idx_refs[u]], row_refs[u], gather_sems[u]))
            stores = []
            for u in range(ways):
                gathers[u].wait()
                stores.append(pltpu.async_copy(row_refs[u], out_hbm.at[pl.ds(bases[u], ch)], store_sems[u]))
            for store in stores:
                store.wait()

    return gather_rows(dest_km, y_rows)


def _expert_kernel(be_ref, nv_ref, xs_ref, wgu_f32_ref, bgu_ref, wdn_f32_ref, bdn_ref, y_ref,
                   wgu_ref, wdn_ref):
    i = pl.program_id(0)
    nv = nv_ref[i]
    half = MOE_BLOCK // 2

    @pl.when(jnp.logical_or(i == 0, be_ref[i] != be_ref[jnp.maximum(i - 1, 0)]))
    def _():
        wgu_ref[...] = wgu_f32_ref[...].astype(BF16)
        wdn_ref[...] = wdn_f32_ref[...].astype(BF16)

    def ffn(rows):
        x = _unpack_bf16_pairs(xs_ref[rows, :])
        rowid = lax.broadcasted_iota(I32, x.shape, 0)
        x = jnp.where(rowid < nv, x, 0.0).astype(BF16)
        gu = _dot(x, wgu_ref[...]) + bgu_ref[...]
        f = gu.shape[1] // 2
        glu = jnp.minimum(gu[:, :f], SWIGLU_LIMIT)
        lin = jnp.clip(gu[:, f:], -SWIGLU_LIMIT, SWIGLU_LIMIT)
        act = glu * _sigmoid(SWIGLU_ALPHA * glu) * (lin + 1.0)
        y_ref[rows, :] = _pack_bf16_pairs(_dot(act.astype(BF16), wdn_ref[...]) + bdn_ref[...])

    @pl.when(nv == 0)
    def _():
        y_ref[...] = jnp.zeros(y_ref.shape, U32)

    @pl.when(jnp.logical_and(nv > 0, nv <= half))
    def _():
        ffn(slice(0, half))
        y_ref[half:, :] = jnp.zeros((MOE_BLOCK - half, y_ref.shape[1]), U32)

    @pl.when(nv > half)
    def _():
        ffn(slice(0, MOE_BLOCK))


def _experts(block_e, block_valid, xs, wgu, bgu, wdn, bdn):
    n_rows, w = xs.shape
    e, d, f2 = wgu.shape
    n_blocks = n_rows // MOE_BLOCK
    grid_spec = pltpu.PrefetchScalarGridSpec(
        num_scalar_prefetch=2,
        grid=(n_blocks,),
        in_specs=[
            pl.BlockSpec((MOE_BLOCK, w), lambda i, be, nv: (i, 0)),
            pl.BlockSpec((None, d, f2), lambda i, be, nv: (be[i], 0, 0)),
            pl.BlockSpec((None, 1, f2), lambda i, be, nv: (be[i], 0, 0)),
            pl.BlockSpec((None, f2 // 2, d), lambda i, be, nv: (be[i], 0, 0)),
            pl.BlockSpec((None, 1, d), lambda i, be, nv: (be[i], 0, 0)),
        ],
        out_specs=pl.BlockSpec((MOE_BLOCK, d // 2), lambda i, be, nv: (i, 0)),
        scratch_shapes=[pltpu.VMEM((d, f2), BF16), pltpu.VMEM((f2 // 2, d), BF16)],
    )
    return pl.pallas_call(
        _expert_kernel,
        grid_spec=grid_spec,
        out_shape=jax.ShapeDtypeStruct((n_rows, d // 2), U32),
        compiler_params=pltpu.CompilerParams(dimension_semantics=("arbitrary",),
                                             vmem_limit_bytes=EXPERT_VMEM_LIMIT_BYTES),
        name="experts",
    )(block_e, block_valid, xs, wgu, bgu, wdn, bdn)


def _combine_kernel(dn_alpha, h1_ref, gate_ref, g_ref, b_ref, yg_ref, o_ref):
    gate = gate_ref[...]
    ffn = gate[:, 0:1] * _unpack_bf16_pairs(yg_ref[0])
    for k in range(1, TOP_K):
        ffn = ffn + gate[:, k:k + 1] * _unpack_bf16_pairs(yg_ref[k])
    o_ref[...] = _layer_norm_rows(dn_alpha * h1_ref[...] + ffn, g_ref[...], b_ref[...])


def _combine(dn_alpha, h1, gate, ln_g, ln_b, yg):
    t, d = h1.shape
    tc = COMBINE_ROWS
    row = lambda i: (i, 0)
    const = lambda i: (0, 0)
    return pl.pallas_call(
        functools.partial(_combine_kernel, dn_alpha),
        grid=(t // tc,),
        in_specs=[
            pl.BlockSpec((tc, d), row),
            pl.BlockSpec((tc, V7X_LANES), row),
            pl.BlockSpec((1, d), const),
            pl.BlockSpec((1, d), const),
            pl.BlockSpec((TOP_K, tc, d // 2), lambda i: (0, i, 0)),
        ],
        out_specs=pl.BlockSpec((tc, d), row),
        out_shape=jax.ShapeDtypeStruct((t, d), F32),
        compiler_params=_params(("parallel",)),
        name="combine_ln2",
    )(h1, gate, ln_g, ln_b, yg)


def _pack_in_proj(w_in, b_in):
    d = w_in.shape[0]
    o = 0
    cols = {}
    for name, width in (("mqk", MQK_W), ("mv", M_W), ("mo", M_W), ("mi", M_HEADS), ("mf", M_HEADS),
                        ("fq", F_W), ("fk", F_W), ("fv", F_W), ("ff", F_HEADS), ("gm", d), ("gf", d)):
        cols[name] = (o, o + width)
        o += width

    def take(a, names):
        return [a[..., cols[n][0]:cols[n][1]] for n in names]

    n_gate = 2 * M_HEADS + F_HEADS
    main = ("mqk", "mv", "mo", "fq", "fk", "fv", "mi", "mf", "ff")
    w_main = jnp.concatenate(take(w_in, main) + [jnp.zeros((d, GATE_COLS - n_gate), w_in.dtype)], axis=1)
    b_main = jnp.concatenate(take(b_in, main) + [jnp.zeros((GATE_COLS - n_gate,), b_in.dtype)])
    w_gate = jnp.concatenate(take(w_in, ("gm", "gf")), axis=1)
    b_gate = jnp.concatenate(take(b_in, ("gm", "gf")))
    return w_main.astype(BF16), b_main[None, :], w_gate.astype(BF16), b_gate[None, :]


def _layer(h, depth, w_in, b_in, m_conv_w, m_conv_b, m_norm_g, w_bm, w_bf, w_o, ln1_g, ln1_b,
           w_router, b_router, w_gu, b_gu, w_dn, b_dn, ln2_g, ln2_b):
    bsz, s, d = h.shape
    t = bsz * s
    dn_alpha = (2.0 * depth) ** 0.25
    x2 = h.reshape(t, d)
    assert d == MQK_W and w_gu.shape[0] == N_EXPERTS, "kernels are written for this layer geometry"
    assert s % INPROJ_ROWS == 0 and s % MLSTM_CHUNK == 0 and s % FOX_Q_BLOCK == 0
    assert bsz % MLSTM_SEQS == 0
    assert t % MERGE_ROWS == 0 and t % ROUTE_ROWS == 0 and t % COMBINE_ROWS == 0
    assert t % (V7X_SC_WORKERS * SC_SCATTER_ROWS) == 0
    assert (t * TOP_K) % (V7X_SC_WORKERS * SC_GATHER_ROWS * SC_GATHER_WAYS) == 0

    w_main, b_main, w_gate, b_gate = _pack_in_proj(w_in, b_in)
    mq, mk, mv, mo, fq, fk, fv, gates = _inproj(x2, w_main, b_main, m_conv_w, m_conv_b[None, :], s)
    ccol, crel_k, crel_q = _fox_cumsum(gates, bsz, s)
    hm = _mlstm(mq, mk, mv, mo, gates, m_norm_g, bsz, s)
    hf = _fox_attention(fq, fk, fv, ccol, crel_k, crel_q, bsz, s)

    n_exp = w_router.shape[1]
    wr = jnp.zeros((d, V7X_LANES), F32).at[:, :n_exp].set(w_router)
    wr_hi = wr.astype(BF16)
    wr_lo = (wr - wr_hi.astype(F32)).astype(BF16)
    br = jnp.zeros((1, V7X_LANES), F32).at[0, :n_exp].set(b_router)
    h1, h1p, gate, tope, tile_counts = _merge(
        dn_alpha, hm, hf, x2, w_gate, b_gate, w_bm.astype(BF16), w_bf.astype(BF16), w_o.astype(BF16),
        ln1_g[None, :], ln1_b[None, :], wr_hi, wr_lo, br)

    n_blocks = -(-(t * TOP_K) // MOE_BLOCK) + N_EXPERTS
    dest, table = _routing(tile_counts, tope, n_blocks)
    dest_km = dest[:, :TOP_K].T.reshape(TOP_K * t)
    block_e, block_valid = table[:, 0], table[:, 1]
    xs = _sc_dispatch(dest_km, h1p, n_blocks * MOE_BLOCK)
    y_rows = _experts(block_e, block_valid, xs, w_gu, b_gu[:, None, :], w_dn, b_dn[:, None, :])
    yg = _sc_gather(dest_km, y_rows).reshape(TOP_K, t, d // 2)
    out = _combine(dn_alpha, h1, gate, ln2_g[None, :], ln2_b[None, :], yg)
    return out.reshape(bsz, s, d)


def kernel(x, w_in, b_in, m_conv_w, m_conv_b, m_norm_g, w_bm, w_bf, w_o, ln1_g, ln1_b,
           w_router, b_router, w_gu, b_gu, w_dn, b_dn, ln2_g, ln2_b):
    depth = w_in.shape[0]
    h = x
    for l in range(depth):
        h = _layer(h, depth, w_in[l], b_in[l], m_conv_w[l], m_conv_b[l], m_norm_g[l], w_bm[l], w_bf[l],
                   w_o[l], ln1_g[l], ln1_b[l], w_router[l], b_router[l], w_gu[l], b_gu[l], w_dn[l],
                   b_dn[l], ln2_g[l], ln2_b[l])
    return h
```

```python
import functools
import math

import jax
import jax.numpy as jnp
from jax import lax
from jax.experimental import pallas as pl
from jax.experimental.pallas import tpu as pltpu
from jax.experimental.pallas import tpu_sc as plsc

F32 = jnp.float32
BF16 = jnp.bfloat16
I32 = jnp.int32
U32 = jnp.uint32

M_HEADS = 4
M_DQK = 128
M_DV = 128
CONV_W = 4
F_HEADS = 8
F_DH = 64
N_EXPERTS = 32
TOP_K = 4
SWIGLU_ALPHA = 1.702
SWIGLU_LIMIT = 7.0
LN_EPS = 1e-5

M_W = M_HEADS * M_DV
F_W = F_HEADS * F_DH
MQK_W = 2 * M_HEADS * M_DQK

V7X_LANES = 128
V7X_SUBLANES = 8
V7X_VMEM_BYTES = 64 * 1024 * 1024
VMEM_LIMIT_BYTES = (V7X_VMEM_BYTES * 3) // 4
EXPERT_VMEM_LIMIT_BYTES = (V7X_VMEM_BYTES * 7) // 8
V7X_SC_CORES = 2
V7X_SC_SUBCORES = 16
V7X_SC_WORKERS = V7X_SC_CORES * V7X_SC_SUBCORES

INPROJ_ROWS = 512
MLSTM_CHUNK = 256
MLSTM_SEQS = 2
FOX_Q_BLOCK = 512
FOX_K_BLOCK = 256
MERGE_ROWS = 512
MERGE_SUB_ROWS = 256
ROUTE_ROWS = 1024
ROUTE_SUB_ROWS = 256
MOE_BLOCK = 512
SC_SCATTER_ROWS = 128
SC_GATHER_ROWS = 64
SC_GATHER_WAYS = 2
COMBINE_ROWS = 512

GATE_COLS = V7X_LANES
MI_LANE = 0
MF_LANE = M_HEADS
FF_LANE = 2 * M_HEADS


def _params(semantics):
    return pltpu.CompilerParams(dimension_semantics=semantics, vmem_limit_bytes=VMEM_LIMIT_BYTES)


def _log_sigmoid(x):
    return jnp.minimum(x, 0.0) - jnp.log1p(jnp.exp(-jnp.abs(x)))


def _sigmoid(x):
    return 1.0 / (1.0 + jnp.exp(-x))


def _dot(a, b):
    return jnp.dot(a, b, preferred_element_type=F32)


def _dot_nt(a, b):
    return lax.dot_general(a, b, (((1,), (1,)), ((), ())), preferred_element_type=F32)


def _dot_tn(a, b):
    return lax.dot_general(a, b, (((0,), (0,)), ((), ())), preferred_element_type=F32)


def _split3(x):
    hi = x.astype(BF16)
    r1 = x - hi.astype(F32)
    mid = r1.astype(BF16)
    lo = (r1 - mid.astype(F32)).astype(BF16)
    return hi, mid, lo


def _dot_mask_f32(mask_bf16, x):
    hi, mid, lo = _split3(x)
    return (_dot(mask_bf16, lo) + _dot(mask_bf16, mid)) + _dot(mask_bf16, hi)


def _pack_bf16_pairs(x):
    half = x.shape[1] // 2
    bits = lax.bitcast_convert_type(x.astype(BF16).astype(F32), U32)
    return (bits[:, :half] >> 16) | bits[:, half:]


def _unpack_bf16_pairs(words):
    lo = lax.bitcast_convert_type(words << 16, F32)
    hi = lax.bitcast_convert_type(words & jnp.uint32(0xFFFF0000), F32)
    return jnp.concatenate([lo, hi], axis=1)


def _tril_mask(n, strict=False):
    r = lax.broadcasted_iota(I32, (n, n), 0)
    c = lax.broadcasted_iota(I32, (n, n), 1)
    return (r > c) if strict else (r >= c)


_OFF_MQK = 0
_OFF_MV = _OFF_MQK + MQK_W
_OFF_MO = _OFF_MV + M_W
_OFF_FQ = _OFF_MO + M_W
_OFF_FK = _OFF_FQ + F_W
_OFF_FV = _OFF_FK + F_W
_OFF_GATES = _OFF_FV + F_W
_PACKED_COLS = _OFF_GATES + GATE_COLS


def _inproj_kernel(tiles_per_seq, x_ref, w_ref, b_ref, cw_ref, cb_ref,
                   mq_ref, mk_ref, mv_ref, mo_ref, fq_ref, fk_ref, fv_ref, gates_ref, ext_ref):
    tm = x_ref.shape[0]
    pad = V7X_SUBLANES
    xb = x_ref[...].astype(BF16)

    def seg(lo, width):
        return _dot(xb, w_ref[:, lo:lo + width]) + b_ref[:, lo:lo + width]

    @pl.when(pl.program_id(0) % tiles_per_seq == 0)
    def _():
        ext_ref[0:pad, :] = jnp.zeros((pad, MQK_W), F32)

    ext_ref[pad:pad + tm, :] = seg(_OFF_MQK, MQK_W)
    y = cb_ref[...] + cw_ref[CONV_W - 1:CONV_W, :] * ext_ref[pad:pad + tm, :]
    for k in range(CONV_W - 1):
        shift = CONV_W - 1 - k
        y = y + cw_ref[k:k + 1, :] * ext_ref[pad - shift:pad - shift + tm, :]
    ext_ref[0:pad, :] = ext_ref[tm:tm + pad, :]
    act = y * _sigmoid(y)
    mq_ref[...] = act[:, :MQK_W // 2].astype(BF16)
    mk_ref[...] = act[:, MQK_W // 2:] * (M_DQK ** -0.5)
    mv_ref[...] = seg(_OFF_MV, M_W).astype(BF16)
    mo_ref[...] = seg(_OFF_MO, M_W)
    fq_ref[...] = (seg(_OFF_FQ, F_W) * (F_DH ** -0.5)).astype(BF16)
    fk_ref[...] = seg(_OFF_FK, F_W).astype(BF16)
    fv_ref[...] = seg(_OFF_FV, F_W).astype(BF16)
    gates_ref[...] = seg(_OFF_GATES, GATE_COLS)


def _inproj(x2, w_packed, b_packed, conv_w, conv_b, s):
    t, d = x2.shape
    tm = INPROJ_ROWS
    row = lambda i: (i, 0)
    const = lambda i: (0, 0)
    out_shapes = (
        jax.ShapeDtypeStruct((t, MQK_W // 2), BF16),
        jax.ShapeDtypeStruct((t, MQK_W // 2), F32),
        jax.ShapeDtypeStruct((t, M_W), BF16),
        jax.ShapeDtypeStruct((t, M_W), F32),
        jax.ShapeDtypeStruct((t, F_W), BF16),
        jax.ShapeDtypeStruct((t, F_W), BF16),
        jax.ShapeDtypeStruct((t, F_W), BF16),
        jax.ShapeDtypeStruct((t, GATE_COLS), F32),
    )
    return pl.pallas_call(
        functools.partial(_inproj_kernel, s // tm),
        grid=(t // tm,),
        in_specs=[
            pl.BlockSpec((tm, d), row),
            pl.BlockSpec((d, _PACKED_COLS), const),
            pl.BlockSpec((1, _PACKED_COLS), const),
            pl.BlockSpec((CONV_W, MQK_W), const),
            pl.BlockSpec((1, MQK_W), const),
        ],
        out_specs=tuple(pl.BlockSpec((tm, o.shape[1]), row) for o in out_shapes),
        out_shape=out_shapes,
        scratch_shapes=[pltpu.VMEM((tm + V7X_SUBLANES, MQK_W), F32)],
        compiler_params=_params(("arbitrary",)),
        name="inproj",
    )(x2, w_packed, b_packed, conv_w, conv_b)


def _fox_cumsum_kernel(g_ref, ccol_ref, crel_k_ref, crel_q_ref):
    s = g_ref.shape[0]
    cb = FOX_K_BLOCK
    per_q = FOX_Q_BLOCK // cb
    tri = _tril_mask(cb).astype(BF16)
    carry = jnp.zeros((1, GATE_COLS), F32)
    for j in range(s // cb):
        rows = slice(j * cb, (j + 1) * cb)
        if j % per_q == 0:
            q_carry = carry
        within = _dot_mask_f32(tri, _log_sigmoid(g_ref[rows, :]))
        crel_k_ref[rows, :] = within
        crel_q_ref[rows, :] = within + (carry - q_carry)
        ccol_ref[rows, :] = within + carry
        carry = carry + within[cb - 1:cb, :]


def _fox_cumsum(gates, bsz, s):
    t = gates.shape[0]
    spec = pl.BlockSpec((s, GATE_COLS), lambda b: (b, 0))
    shape = jax.ShapeDtypeStruct((t, GATE_COLS), F32)
    return pl.pallas_call(
        _fox_cumsum_kernel,
        grid=(bsz,),
        in_specs=[spec],
        out_specs=(spec, spec, spec),
        out_shape=(shape, shape, shape),
        compiler_params=_params(("parallel",)),
        name="fox_cumsum",
    )(gates)


def _mlstm_kernel(mq_ref, mk_ref, mv_ref, mo_ref, gates_ref, ng_ref, hm_ref, state_ref, m_ref):
    @pl.when(pl.program_id(1) == 0)
    def _():
        state_ref[...] = jnp.zeros(state_ref.shape, F32)
        m_ref[...] = jnp.zeros(m_ref.shape, F32)

    seqs = range(mq_ref.shape[0])
    states = [[state_ref[bb, h] for h in range(M_HEADS)] for bb in seqs]
    maxes = [[m_ref[bb, h][0:1, 0:1] for h in range(M_HEADS)] for bb in seqs]
    results = [_mlstm_chunk(mq_ref.at[bb], mk_ref.at[bb], mv_ref.at[bb], mo_ref.at[bb], gates_ref.at[bb],
                            ng_ref, states[bb], maxes[bb]) for bb in seqs]
    for bb in seqs:
        for h, (out_h, state_h, m_h) in enumerate(results[bb]):
            hm_ref[bb, :, h * M_DV:(h + 1) * M_DV] = out_h
            state_ref[bb, h] = state_h
            m_ref[bb, h] = jnp.broadcast_to(m_h, m_ref.shape[2:])


def _mlstm_chunk(mq_ref, mk_ref, mv_ref, mo_ref, gates_ref, ng_ref, states, maxes):
    L = MLSTM_CHUNK
    reps = L // V7X_LANES
    results = []
    gates = gates_ref[...]
    bfull = _dot_mask_f32(_tril_mask(L).astype(BF16), _log_sigmoid(gates))
    b_rows = bfull.T
    z_all = gates - pltpu.roll(bfull, shift=V7X_LANES - (MF_LANE - MI_LANE), axis=1)
    visible = (lax.broadcasted_iota(I32, (L, L), 0) <= lax.broadcasted_iota(I32, (L, L), 1))
    ones_rows = (lax.broadcasted_iota(I32, (M_DV, L), 0) == 0).astype(BF16)

    for h in range(M_HEADS):
        b_row = b_rows[MF_LANE + h:MF_LANE + h + 1, :]
        g_tot = b_row[:, L - 1:L]
        m_prev = maxes[h]
        z = jnp.broadcast_to(z_all[:, MI_LANE + h:MI_LANE + h + 1], (L, V7X_LANES))

        q_h = mq_ref[:, h * M_DQK:(h + 1) * M_DQK]
        k_f = mk_ref[:, h * M_DQK:(h + 1) * M_DQK]
        k_h = k_f.astype(BF16)
        v_t = mv_ref[:, h * M_DV:(h + 1) * M_DV].astype(F32).T.astype(BF16)
        cn_t = states[h]

        dlog = jnp.where(visible, b_row + jnp.tile(z, (1, reps)), -jnp.inf)
        inter_log = b_row + m_prev
        m_t = jnp.maximum(inter_log, jnp.max(dlog, axis=0, keepdims=True))
        w_inter = jnp.exp(inter_log - m_t)
        qkw = _dot_nt(k_h, q_h) * jnp.exp(dlog - m_t)
        qc = _dot_nt(cn_t.astype(BF16), q_h)
        num = w_inter * qc[:M_DV, :] + _dot(v_t, qkw.astype(BF16))
        den = w_inter * qc[M_DV:M_DV + 1, :] + jnp.sum(qkw, axis=0, keepdims=True)
        hh = num / jnp.maximum(jnp.abs(den), jnp.exp(-m_t))

        mu = jnp.mean(hh, axis=0, keepdims=True)
        dv = hh - mu
        var = jnp.mean(dv * dv, axis=0, keepdims=True)
        hn = (dv * lax.rsqrt(var + LN_EPS)) * jnp.tile(ng_ref[h * M_DV:(h + 1) * M_DV, :], (1, reps))
        out_h = (_sigmoid(mo_ref[:, h * M_DV:(h + 1) * M_DV]) * hn.T).astype(BF16)

        a = g_tot + z
        m_new = jnp.maximum(g_tot + m_prev, jnp.max(a, axis=0, keepdims=True)[:, 0:1])
        decay = jnp.exp(g_tot + m_prev - m_new)
        kw = (k_f * jnp.exp(a - m_new)).astype(BF16)
        v_aug = jnp.concatenate([v_t, ones_rows], axis=0)
        results.append((out_h, decay * cn_t + _dot(v_aug, kw), m_new))
    return results


def _mlstm(mq, mk, mv, mo, gates, norm_g, bsz, s):
    t = mq.shape[0]
    L = MLSTM_CHUNK
    nb = MLSTM_SEQS
    seq = lambda a: a.reshape(bsz, s, a.shape[1])
    blk = lambda width: pl.BlockSpec((nb, L, width), lambda g, n: (g, n, 0))
    const = lambda g, n: (0, 0)
    hm = pl.pallas_call(
        _mlstm_kernel,
        grid=(bsz // nb, s // L),
        in_specs=[
            blk(MQK_W // 2), blk(MQK_W // 2), blk(M_W), blk(M_W), blk(GATE_COLS),
            pl.BlockSpec((M_W, V7X_LANES), const),
        ],
        out_specs=blk(M_W),
        out_shape=jax.ShapeDtypeStruct((bsz, s, M_W), BF16),
        scratch_shapes=[
            pltpu.VMEM((nb, M_HEADS, 2 * M_DV, M_DQK), F32),
            pltpu.VMEM((nb, M_HEADS, V7X_SUBLANES, V7X_LANES), F32),
        ],
        compiler_params=_params(("parallel", "arbitrary")),
        name="mlstm",
    )(seq(mq), seq(mk), seq(mv), seq(mo), seq(gates),
      jnp.broadcast_to(norm_g[:, None], (M_W, V7X_LANES)))
    return hm.reshape(t, M_W)


def _fox_operand(x, c_col, hh, key_side):
    rows = x.shape[0]
    lane = lax.broadcasted_iota(I32, (rows, V7X_LANES), 1)
    own = (lane < F_DH) if hh == 0 else (lane >= F_DH)
    base = F_DH if hh == 0 else 0
    parts = [part.astype(F32) for part in _split3(c_col)]
    if key_side:
        feats = parts + [1.0, 1.0, 1.0]
    else:
        feats = [-1.0, -1.0, -1.0] + parts
    out = jnp.where(own, x, 0.0)
    for n, feat in enumerate(feats):
        out = jnp.where(lane == base + n, feat, out)
    return out.astype(BF16)


def _fox_kernel(q_ref, k_ref, v_ref, ccol_ref, crel_k_ref, crel_q_ref, o_ref, qaug_ref, kaug_ref, vt_ref,
                st_a0, st_a1, st_b0, st_b1, pe_a0, pe_a1, pe_b0, pe_b1, acc0, acc1):
    blk = FOX_K_BLOCK
    tq = FOX_Q_BLOCK
    assert tq == 2 * blk
    strips = tq // V7X_LANES
    st_a_refs, st_b_refs = (st_a0, st_a1), (st_b0, st_b1)
    pe_a_refs, pe_b_refs = (pe_a0, pe_a1), (pe_b0, pe_b1)
    acc_refs = (acc0, acc1)
    p = pl.program_id(1)
    i = pl.program_id(2)

    @pl.when(i == 0)
    def _():
        lane = lax.broadcasted_iota(I32, crel_q_ref.shape, 1)
        q_all = q_ref[...].astype(F32)
        k_all = k_ref[...].astype(F32)
        for hh in range(2):
            mine = lane == FF_LANE + 2 * p + hh
            c_q = jnp.sum(jnp.where(mine, crel_q_ref[...], 0.0), axis=-1, keepdims=True)
            c_k = jnp.sum(jnp.where(mine, crel_k_ref[...], 0.0), axis=-1, keepdims=True)
            qaug_ref[hh] = _fox_operand(q_all, c_q, hh, False)
            kaug_ref[hh] = _fox_operand(k_all, c_k, hh, True)
        v_t = v_ref[...].astype(F32).T
        for j in range(vt_ref.shape[0]):
            vt_ref[j] = v_t[:, j * blk:(j + 1) * blk].astype(BF16)

    q_start = pl.multiple_of(i * tq, tq)
    q_heads = [qaug_ref[hh, pl.ds(q_start, tq), :] for hh in range(2)]
    key_row = lax.broadcasted_iota(I32, (blk, V7X_LANES), 0)
    query_col = lax.broadcasted_iota(I32, (blk, V7X_LANES), 1)
    last_chunk = 2 * i + 1
    head_lane = lax.broadcasted_iota(I32, (1, GATE_COLS), 1) - (FF_LANE + 2 * p)

    def c_before(position, hh):
        row = ccol_ref[pl.ds(jnp.maximum(position - 1, 0), 1), :]
        keep = jnp.logical_and(head_lane == hh, position > 0)
        return jnp.sum(jnp.where(keep, row, 0.0), axis=-1, keepdims=True)

    c_query0 = [c_before(q_start, hh) for hh in range(2)]

    def scores(j, hh):
        start = pl.multiple_of(j * blk, blk)
        return _dot_nt(kaug_ref[hh, pl.ds(start, blk), :], q_heads[hh])

    def softmax_update(st_ref, pe_ref, m_old, l_old, base, key_minus_query=None):
        alphas, ms, ls = [], [], []
        for c in range(strips):
            cols = slice(c * V7X_LANES, (c + 1) * V7X_LANES)
            gap = None if key_minus_query is None else key_minus_query - c * V7X_LANES
            if gap is not None and gap - (V7X_LANES - 1) > 0:
                pe_ref[:, cols] = jnp.zeros((blk, V7X_LANES), BF16)
                alphas.append(jnp.ones((1, V7X_LANES), F32))
                ms.append(m_old[:, cols])
                ls.append(l_old[:, cols])
                continue
            st = st_ref[:, cols]
            if gap is not None and gap + (blk - 1) > 0:
                st = jnp.where(key_row + gap <= query_col, st, -jnp.inf)
            m_new = jnp.maximum(m_old[:, cols], jnp.max(st, axis=0, keepdims=True) + base)
            alpha = jnp.exp(m_old[:, cols] - m_new)
            pe = jnp.exp(st - (m_new - base))
            pe_ref[:, cols] = pe.astype(BF16)
            alphas.append(alpha)
            ms.append(m_new)
            ls.append(alpha * l_old[:, cols] + jnp.sum(pe, axis=0, keepdims=True))
        cat = lambda parts: jnp.concatenate(parts, axis=1)
        return cat(alphas), cat(ms), cat(ls)

    def pair(mi, carry, diagonal=False):
        a = 2 * mi
        b = a + 1
        v_prev = vt_ref[jnp.maximum(a - 1, 0)]
        v_a = vt_ref[a]
        partial = []
        for hh in range(2):
            alpha_prev = carry[hh][0]
            partial.append(alpha_prev * acc_refs[hh][...] + _dot(v_prev, pe_b_refs[hh][...]))
            st_b_refs[hh][...] = scores(b, hh)
        stats = []
        for hh in range(2):
            _, m_old, l_old = carry[hh]
            stats.append(softmax_update(st_a_refs[hh], pe_a_refs[hh], m_old, l_old,
                                        c_query0[hh] - c_before(a * blk, hh), 0 if diagonal else None))
        for hh in range(2):
            alpha_a = stats[hh][0]
            acc_refs[hh][...] = alpha_a * partial[hh] + _dot(v_a, pe_a_refs[hh][...])
            if not diagonal:
                st_a_refs[hh][...] = scores(a + 2, hh)
        return tuple(softmax_update(st_b_refs[hh], pe_b_refs[hh], stats[hh][1], stats[hh][2],
                                    c_query0[hh] - c_before(b * blk, hh), blk if diagonal else None)
                     for hh in range(2))

    for hh in range(2):
        st_a_refs[hh][...] = scores(0, hh)
        pe_b_refs[hh][...] = jnp.zeros((blk, tq), BF16)
        acc_refs[hh][...] = jnp.zeros((V7X_LANES, tq), F32)
    init = tuple((jnp.ones((1, tq), F32), jnp.full((1, tq), -jnp.inf, F32), jnp.zeros((1, tq), F32))
                 for _ in range(2))
    final = pair(i, lax.fori_loop(0, i, pair, init), diagonal=True)
    v_last = vt_ref[last_chunk]
    outs = []
    for hh in range(2):
        alpha, _, l_fin = final[hh]
        outs.append((alpha * acc_refs[hh][...] + _dot(v_last, pe_b_refs[hh][...])) / l_fin)
    row = lax.broadcasted_iota(I32, (V7X_LANES, tq), 0)
    o_t = jnp.where(row < F_DH, outs[0], outs[1])
    o_ref[...] = o_t.T.astype(BF16)


def _fox_attention(fq, fk, fv, ccol, crel_k, crel_q, bsz, s):
    t = fq.shape[0]
    blk = FOX_K_BLOCK
    tq = FOX_Q_BLOCK
    nq = s // tq
    pairs = F_HEADS // 2
    qmap = lambda b, p, i: (b * nq + i, p)
    kvmap = lambda b, p, i: (b, p)
    return pl.pallas_call(
        _fox_kernel,
        grid=(bsz, pairs, nq),
        in_specs=[
            pl.BlockSpec((s, V7X_LANES), kvmap),
            pl.BlockSpec((s, V7X_LANES), kvmap),
            pl.BlockSpec((s, V7X_LANES), kvmap),
            pl.BlockSpec((s, GATE_COLS), lambda b, p, i: (b, 0)),
            pl.BlockSpec((s, GATE_COLS), lambda b, p, i: (b, 0)),
            pl.BlockSpec((s, GATE_COLS), lambda b, p, i: (b, 0)),
        ],
        out_specs=pl.BlockSpec((tq, V7X_LANES), qmap),
        out_shape=jax.ShapeDtypeStruct((t, F_W), BF16),
        scratch_shapes=[
            pltpu.VMEM((2, s, V7X_LANES), BF16),
            pltpu.VMEM((2, s, V7X_LANES), BF16),
            pltpu.VMEM((s // blk, V7X_LANES, blk), BF16),
        ] + [pltpu.VMEM((blk, tq), F32)] * 4 + [pltpu.VMEM((blk, tq), BF16)] * 4
          + [pltpu.VMEM((V7X_LANES, tq), F32)] * 2,
        compiler_params=_params(("parallel", "parallel", "arbitrary")),
        name="fox_attention",
    )(fq, fk, fv, ccol, crel_k, crel_q)


def _layer_norm_rows(r, g, b):
    mu = jnp.mean(r, axis=-1, keepdims=True)
    d = r - mu
    var = jnp.mean(d * d, axis=-1, keepdims=True)
    return (d * lax.rsqrt(var + LN_EPS)) * g + b


def _merge_kernel(dn_alpha, hm_ref, hf_ref, x_ref, wg_ref, bg_ref, wbm_ref, wbf_ref, wo_ref,
                  g_ref, b_ref, wrh_ref, wrl_ref, br_ref, h1_ref, h1p_ref, gate_ref, tope_ref, cnt_ref,
                  resid_ref):
    i = pl.program_id(0)
    slot = i % 2

    @pl.when(i == 0)
    def _():
        resid_ref[1] = jnp.zeros(resid_ref.shape[1:], F32)

    subs = [slice(n * MERGE_SUB_ROWS, (n + 1) * MERGE_SUB_ROWS)
            for n in range(x_ref.shape[0] // MERGE_SUB_ROWS)]
    counts = jnp.zeros((1, V7X_LANES), F32)
    for rows in subs:
        counts = counts + _merge_tail(resid_ref[1 - slot, rows, :], rows, g_ref, b_ref, wrh_ref, wrl_ref,
                                      br_ref, h1_ref, h1p_ref, gate_ref, tope_ref)
    sub = lax.broadcasted_iota(I32, cnt_ref.shape, 0)
    cnt_ref[...] = jnp.where(sub == 0, counts, 0.0)

    for rows in subs:
        x = x_ref[rows, :]
        d = x.shape[1]
        gmf = _dot(x.astype(BF16), wg_ref[...]) + bg_ref[...]
        ym = _dot(hm_ref[rows, :], wbm_ref[...])
        yf = _dot(hf_ref[rows, :], wbf_ref[...])
        y = _sigmoid(gmf[:, :d]) * ym + _sigmoid(gmf[:, d:]) * yf
        resid_ref[slot, rows, :] = dn_alpha * x + _dot(y.astype(BF16), wo_ref[...])


def _merge_tail(resid, rows, g_ref, b_ref, wrh_ref, wrl_ref, br_ref, h1_ref, h1p_ref, gate_ref, tope_ref):
    h1 = _layer_norm_rows(resid, g_ref[...], b_ref[...])
    h1_ref[rows, :] = h1

    h1p_ref[rows, :] = _pack_bf16_pairs(h1)
    hb = h1.astype(BF16)

    lo = (h1 - hb.astype(F32)).astype(BF16)
    logits = (_dot(lo, wrh_ref[...]) + _dot(hb, wrl_ref[...])) + _dot(hb, wrh_ref[...]) + br_ref[...]
    tm = logits.shape[0]
    lane = lax.broadcasted_iota(I32, (tm, V7X_LANES), 1)
    vals = jnp.where(lane < N_EXPERTS, logits, -jnp.inf)
    top_v, top_i = [], []
    for _ in range(TOP_K):
        mx = jnp.max(vals, axis=-1, keepdims=True)
        idx = jnp.min(jnp.where(vals == mx, lane, V7X_LANES), axis=-1, keepdims=True)
        top_v.append(mx)
        top_i.append(idx)
        vals = jnp.where(lane == idx, -jnp.inf, vals)
    ex = [jnp.exp(v - top_v[0]) for v in top_v]
    tot = ex[0]
    for e in ex[1:]:
        tot = tot + e
    gate = jnp.zeros((tm, V7X_LANES), F32)
    tope = jnp.zeros((tm, V7X_LANES), I32)
    member = jnp.zeros((tm, V7X_LANES), F32)
    for k in range(TOP_K):
        gate = jnp.where(lane == k, ex[k] / tot, gate)
        tope = jnp.where(lane == k, top_i[k], tope)
        member = member + (lane == top_i[k]).astype(F32)
    gate_ref[rows, :] = gate
    tope_ref[rows, :] = tope
    return jnp.sum(member, axis=0, keepdims=True)


def _merge(dn_alpha, hm, hf, x2, wg, bg, wbm, wbf, wo, ln_g, ln_b, wr_hi, wr_lo, br):
    t, d = x2.shape
    tm = MERGE_ROWS
    nt = t // tm
    row = lambda i: (jnp.minimum(i, nt - 1), 0)
    out_row = lambda i: (jnp.maximum(i - 1, 0), 0)
    const = lambda i: (0, 0)
    full = lambda a: pl.BlockSpec(a.shape, const)
    return pl.pallas_call(
        functools.partial(_merge_kernel, dn_alpha),
        grid=(nt + 1,),
        in_specs=[
            pl.BlockSpec((tm, M_W), row),
            pl.BlockSpec((tm, F_W), row),
            pl.BlockSpec((tm, d), row),
            full(wg), full(bg), full(wbm), full(wbf), full(wo), full(ln_g), full(ln_b),
            full(wr_hi), full(wr_lo), full(br),
        ],
        out_specs=(
            pl.BlockSpec((tm, d), out_row),
            pl.BlockSpec((tm, d // 2), out_row),
            pl.BlockSpec((tm, V7X_LANES), out_row),
            pl.BlockSpec((tm, V7X_LANES), out_row),
            pl.BlockSpec((V7X_SUBLANES, V7X_LANES), out_row),
        ),
        out_shape=(
            jax.ShapeDtypeStruct((t, d), F32),
            jax.ShapeDtypeStruct((t, d // 2), U32),
            jax.ShapeDtypeStruct((t, V7X_LANES), F32),
            jax.ShapeDtypeStruct((t, V7X_LANES), I32),
            jax.ShapeDtypeStruct((t // tm * V7X_SUBLANES, V7X_LANES), F32),
        ),
        scratch_shapes=[pltpu.VMEM((2, tm, d), F32)],
        compiler_params=_params(("arbitrary",)),
        name="merge_ln1_router",
    )(hm, hf, x2, wg, bg, wbm, wbf, wo, ln_g, ln_b, wr_hi, wr_lo, br)


def _lane_cumsum(x):
    lane = lax.broadcasted_iota(I32, x.shape, 1)
    d = 1
    while d < V7X_LANES:
        x = x + jnp.where(lane >= d, pltpu.roll(x, shift=d, axis=1), 0.0)
        d *= 2
    return x


def _routing_kernel(cnt_ref, tope_ref, dest_ref, table_ref, run_ref, start_ref):
    sb = ROUTE_SUB_ROWS

    @pl.when(pl.program_id(0) == 0)
    def _():
        total = jnp.sum(cnt_ref[...], axis=0, keepdims=True)
        counts = jnp.broadcast_to(total, (V7X_SUBLANES, V7X_LANES))
        padded = jnp.ceil(counts * (1.0 / MOE_BLOCK)) * MOE_BLOCK
        pad_end = _lane_cumsum(padded)
        pad_start = pad_end - padded
        start_ref[...] = pad_start
        run_ref[...] = jnp.zeros(run_ref.shape, F32)
        nb = table_ref.shape[0]
        blk = lax.broadcasted_iota(I32, (nb, V7X_LANES), 0).astype(F32) * MOE_BLOCK
        ln = lax.broadcasted_iota(I32, (nb, V7X_LANES), 1)
        done = jnp.logical_and(pad_end[0:1, :] <= blk, ln < N_EXPERTS)
        be = jnp.minimum(jnp.sum(done.astype(F32), axis=-1, keepdims=True), N_EXPERTS - 1.0)
        onehot = ln == be.astype(I32)
        cnt_e = jnp.sum(jnp.where(onehot, counts[0:1, :], 0.0), axis=-1, keepdims=True)
        start_e = jnp.sum(jnp.where(onehot, pad_start[0:1, :], 0.0), axis=-1, keepdims=True)
        valid = jnp.clip(cnt_e - (blk[:, 0:1] - start_e), 0.0, float(MOE_BLOCK))
        table_ref[...] = jnp.where(ln == 0, be.astype(I32),
                                   jnp.where(ln == 1, valid.astype(I32), 0))

    earlier = _tril_mask(sb, strict=True).astype(BF16)
    lane = lax.broadcasted_iota(I32, (sb, V7X_LANES), 1)
    for j in range(tope_ref.shape[0] // sb):
        tope = tope_ref[j * sb:(j + 1) * sb, :]
        hit = [lane == tope[:, k:k + 1] for k in range(TOP_K)]
        member = jnp.zeros((sb, V7X_LANES), F32)
        for k in range(TOP_K):
            member = member + hit[k].astype(F32)
        base = _dot(earlier, member.astype(BF16)) + (run_ref[0:1, :] + start_ref[0:1, :])
        dest = jnp.zeros((sb, V7X_LANES), I32)
        for k in range(TOP_K):
            dk = jnp.sum(jnp.where(hit[k], base, 0.0), axis=-1, keepdims=True)
            dest = jnp.where(lane == k, dk.astype(I32), dest)
        dest_ref[j * sb:(j + 1) * sb, :] = dest
        run_ref[...] = run_ref[...] + jnp.sum(member, axis=0, keepdims=True)


def _routing(tile_counts, tope, n_blocks):
    t = tope.shape[0]
    tr = ROUTE_ROWS
    return pl.pallas_call(
        _routing_kernel,
        grid=(t // tr,),
        in_specs=[pl.BlockSpec(tile_counts.shape, lambda i: (0, 0)),
                  pl.BlockSpec((tr, V7X_LANES), lambda i: (i, 0))],
        out_specs=(
            pl.BlockSpec((tr, V7X_LANES), lambda i: (i, 0)),
            pl.BlockSpec((n_blocks, V7X_LANES), lambda i: (0, 0)),
        ),
        out_shape=(
            jax.ShapeDtypeStruct((t, V7X_LANES), I32),
            jax.ShapeDtypeStruct((n_blocks, V7X_LANES), I32),
        ),
        scratch_shapes=[
            pltpu.VMEM((V7X_SUBLANES, V7X_LANES), F32),
            pltpu.VMEM((V7X_SUBLANES, V7X_LANES), F32),
        ],
        compiler_params=_params(("arbitrary",)),
        name="routing",
    )(tile_counts, tope)


def _sc_worker_id():
    return lax.axis_index("s") * V7X_SC_CORES + lax.axis_index("c")


def _sc_mesh():
    return plsc.VectorSubcoreMesh(core_axis_name="c", subcore_axis_name="s",
                                  num_cores=V7X_SC_CORES, num_subcores=V7X_SC_SUBCORES)


def _sc_dispatch(dest_km, h1p, n_rows):
    t, w = h1p.shape
    per_worker = t // V7X_SC_WORKERS
    ch = SC_SCATTER_ROWS

    @functools.partial(
        pl.kernel, mesh=_sc_mesh(),
        out_type=jax.ShapeDtypeStruct((n_rows, w), h1p.dtype),
        scratch_types=[pltpu.VMEM((ch, w), h1p.dtype)]
        + [pltpu.VMEM((ch,), I32)] * TOP_K + [pltpu.SemaphoreType.DMA] * TOP_K,
        name="sc_dispatch",
    )
    def scatter_rows(dest_hbm, h1p_hbm, xs_hbm, rows_v, *idx_and_sems):
        idx_refs, sems = idx_and_sems[:TOP_K], idx_and_sems[TOP_K:]
        first = _sc_worker_id() * per_worker

        @pl.loop(0, per_worker // ch)
        def _(j):
            base = first + j * ch
            pltpu.sync_copy(h1p_hbm.at[pl.ds(base, ch)], rows_v)
            copies = []
            for k in range(TOP_K):
                pltpu.sync_copy(dest_hbm.at[pl.ds(k * t + base, ch)], idx_refs[k])
                copies.append(pltpu.async_copy(rows_v, xs_hbm.at[idx_refs[k]], sems[k]))
            for copy in copies:
                copy.wait()

    return scatter_rows(dest_km, h1p)


def _sc_gather(dest_km, y_rows):
    n = dest_km.shape[0]
    w = y_rows.shape[1]
    per_worker = n // V7X_SC_WORKERS
    ch = SC_GATHER_ROWS
    ways = SC_GATHER_WAYS

    @functools.partial(
        pl.kernel, mesh=_sc_mesh(),
        out_type=jax.ShapeDtypeStruct((n, w), y_rows.dtype),
        scratch_types=[pltpu.VMEM((ch,), I32)] * ways + [pltpu.VMEM((ch, w), y_rows.dtype)] * ways
        + [pltpu.SemaphoreType.DMA] * (2 * ways),
        name="sc_gather",
    )
    def gather_rows(dest_hbm, y_hbm, out_hbm, *scratch):
        idx_refs, row_refs = scratch[:ways], scratch[ways:2 * ways]
        gather_sems, store_sems = scratch[2 * ways:3 * ways], scratch[3 * ways:]
        first = _sc_worker_id() * per_worker

        @pl.loop(0, per_worker // (ch * ways))
        def _(j):
            bases = [first + (j * ways + u) * ch for u in range(ways)]
            gathers = []
            for u in range(ways):
                pltpu.sync_copy(dest_hbm.at[pl.ds(bases[u], ch)], idx_refs[u])
                gathers.append(pltpu.async_copy(y_hbm.at[idx_refs[u]], row_refs[u], gather_sems[u]))
            stores = []
            for u in range(ways):
                gathers[u].wait()
                stores.append(pltpu.async_copy(row_refs[u], out_hbm.at[pl.ds(bases[u], ch)], store_sems[u]))
            for store in stores:
                store.wait()

    return gather_rows(dest_km, y_rows)


def _expert_kernel(be_ref, nv_ref, xs_ref, wgu_f32_ref, bgu_ref, wdn_f32_ref, bdn_ref, y_ref,
                   wgu_slots, wdn_slots, slot_ref):
    i = pl.program_id(0)
    n_blocks = pl.num_programs(0) - 1
    nv = jnp.where(i > 0, nv_ref[jnp.maximum(i - 1, 0)], 0)
    half = MOE_BLOCK // 2

    @pl.when(i == 0)
    def _():
        slot_ref[0] = 0

    slot = slot_ref[0]
    wgu_ref = wgu_slots.at[slot]
    wdn_ref = wdn_slots.at[slot]

    def ffn(rows):
        x = _unpack_bf16_pairs(xs_ref[rows, :])
        rowid = rows.start + lax.broadcasted_iota(I32, x.shape, 0)
        x = jnp.where(rowid < nv, x, 0.0).astype(BF16)
        gu = _dot(x, wgu_ref[...]) + bgu_ref[...]
        f = gu.shape[1] // 2
        glu = jnp.minimum(gu[:, :f], SWIGLU_LIMIT)
        lin = jnp.clip(gu[:, f:], -SWIGLU_LIMIT, SWIGLU_LIMIT)
        act = glu * _sigmoid(SWIGLU_ALPHA * glu) * (lin + 1.0)
        y_ref[rows, :] = _pack_bf16_pairs(_dot(act.astype(BF16), wdn_ref[...]) + bdn_ref[...])

    @pl.when(nv == 0)
    def _():
        y_ref[...] = jnp.zeros(y_ref.shape, U32)

    @pl.when(jnp.logical_and(nv > 0, nv <= half))
    def _():
        ffn(slice(0, half))
        y_ref[half:, :] = jnp.zeros((MOE_BLOCK - half, y_ref.shape[1]), U32)

    @pl.when(nv > half)
    def _():
        ffn(slice(0, half))
        ffn(slice(half, MOE_BLOCK))

    new_expert = jnp.logical_or(i == 0, be_ref[jnp.minimum(i, n_blocks - 1)] != be_ref[jnp.maximum(i - 1, 0)])

    @pl.when(jnp.logical_and(i < n_blocks, new_expert))
    def _():
        wgu_slots[1 - slot] = wgu_f32_ref[...].astype(BF16)
        wdn_slots[1 - slot] = wdn_f32_ref[...].astype(BF16)
        slot_ref[0] = 1 - slot


def _experts(block_e, block_valid, xs, wgu, bgu, wdn, bdn):
    n_rows, w = xs.shape
    e, d, f2 = wgu.shape
    n_blocks = n_rows // MOE_BLOCK
    ahead = lambda i, be: be[jnp.minimum(i, n_blocks - 1)]
    behind = lambda i: jnp.maximum(i - 1, 0)
    grid_spec = pltpu.PrefetchScalarGridSpec(
        num_scalar_prefetch=2,
        grid=(n_blocks + 1,),
        in_specs=[
            pl.BlockSpec((MOE_BLOCK, w), lambda i, be, nv: (behind(i), 0)),
            pl.BlockSpec((None, d, f2), lambda i, be, nv: (ahead(i, be), 0, 0)),
            pl.BlockSpec((None, 1, f2), lambda i, be, nv: (be[behind(i)], 0, 0)),
            pl.BlockSpec((None, f2 // 2, d), lambda i, be, nv: (ahead(i, be), 0, 0)),
            pl.BlockSpec((None, 1, d), lambda i, be, nv: (be[behind(i)], 0, 0)),
        ],
        out_specs=pl.BlockSpec((MOE_BLOCK, d // 2), lambda i, be, nv: (behind(i), 0)),
        scratch_shapes=[pltpu.VMEM((2, d, f2), BF16), pltpu.VMEM((2, f2 // 2, d), BF16),
                        pltpu.SMEM((1,), I32)],
    )
    return pl.pallas_call(
        _expert_kernel,
        grid_spec=grid_spec,
        out_shape=jax.ShapeDtypeStruct((n_rows, d // 2), U32),
        compiler_params=pltpu.CompilerParams(dimension_semantics=("arbitrary",),
                                             vmem_limit_bytes=EXPERT_VMEM_LIMIT_BYTES),
        name="experts",
    )(block_e, block_valid, xs, wgu, bgu, wdn, bdn)


def _combine_kernel(dn_alpha, h1_ref, gate_ref, g_ref, b_ref, yg_ref, o_ref):
    gate = gate_ref[...]
    ffn = gate[:, 0:1] * _unpack_bf16_pairs(yg_ref[0])
    for k in range(1, TOP_K):
        ffn = ffn + gate[:, k:k + 1] * _unpack_bf16_pairs(yg_ref[k])
    o_ref[...] = _layer_norm_rows(dn_alpha * h1_ref[...] + ffn, g_ref[...], b_ref[...])


def _combine(dn_alpha, h1, gate, ln_g, ln_b, yg):
    t, d = h1.shape
    tc = COMBINE_ROWS
    row = lambda i: (i, 0)
    const = lambda i: (0, 0)
    return pl.pallas_call(
        functools.partial(_combine_kernel, dn_alpha),
        grid=(t // tc,),
        in_specs=[
            pl.BlockSpec((tc, d), row),
            pl.BlockSpec((tc, V7X_LANES), row),
            pl.BlockSpec((1, d), const),
            pl.BlockSpec((1, d), const),
            pl.BlockSpec((TOP_K, tc, d // 2), lambda i: (0, i, 0)),
        ],
        out_specs=pl.BlockSpec((tc, d), row),
        out_shape=jax.ShapeDtypeStruct((t, d), F32),
        compiler_params=_params(("parallel",)),
        name="combine_ln2",
    )(h1, gate, ln_g, ln_b, yg)


def _pack_in_proj(w_in, b_in):
    d = w_in.shape[0]
    o = 0
    cols = {}
    for name, width in (("mqk", MQK_W), ("mv", M_W), ("mo", M_W), ("mi", M_HEADS), ("mf", M_HEADS),
                        ("fq", F_W), ("fk", F_W), ("fv", F_W), ("ff", F_HEADS), ("gm", d), ("gf", d)):
        cols[name] = (o, o + width)
        o += width

    def take(a, names):
        return [a[..., cols[n][0]:cols[n][1]] for n in names]

    n_gate = 2 * M_HEADS + F_HEADS
    main = ("mqk", "mv", "mo", "fq", "fk", "fv", "mi", "mf", "ff")
    w_main = jnp.concatenate(take(w_in, main) + [jnp.zeros((d, GATE_COLS - n_gate), w_in.dtype)], axis=1)
    b_main = jnp.concatenate(take(b_in, main) + [jnp.zeros((GATE_COLS - n_gate,), b_in.dtype)])
    w_gate = jnp.concatenate(take(w_in, ("gm", "gf")), axis=1)
    b_gate = jnp.concatenate(take(b_in, ("gm", "gf")))
    return w_main.astype(BF16), b_main[None, :], w_gate.astype(BF16), b_gate[None, :]


def _layer(h, depth, w_in, b_in, m_conv_w, m_conv_b, m_norm_g, w_bm, w_bf, w_o, ln1_g, ln1_b,
           w_router, b_router, w_gu, b_gu, w_dn, b_dn, ln2_g, ln2_b):
    bsz, s, d = h.shape
    t = bsz * s
    dn_alpha = (2.0 * depth) ** 0.25
    x2 = h.reshape(t, d)
    assert d == MQK_W and w_gu.shape[0] == N_EXPERTS, "kernels are written for this layer geometry"
    assert s % INPROJ_ROWS == 0 and s % MLSTM_CHUNK == 0 and s % FOX_Q_BLOCK == 0
    assert bsz % MLSTM_SEQS == 0
    assert t % MERGE_ROWS == 0 and t % ROUTE_ROWS == 0 and t % COMBINE_ROWS == 0
    assert t % (V7X_SC_WORKERS * SC_SCATTER_ROWS) == 0
    assert (t * TOP_K) % (V7X_SC_WORKERS * SC_GATHER_ROWS * SC_GATHER_WAYS) == 0

    w_main, b_main, w_gate, b_gate = _pack_in_proj(w_in, b_in)
    mq, mk, mv, mo, fq, fk, fv, gates = _inproj(x2, w_main, b_main, m_conv_w, m_conv_b[None, :], s)
    ccol, crel_k, crel_q = _fox_cumsum(gates, bsz, s)
    hm = _mlstm(mq, mk, mv, mo, gates, m_norm_g, bsz, s)
    hf = _fox_attention(fq, fk, fv, ccol, crel_k, crel_q, bsz, s)

    n_exp = w_router.shape[1]
    wr = jnp.zeros((d, V7X_LANES), F32).at[:, :n_exp].set(w_router)
    wr_hi = wr.astype(BF16)
    wr_lo = (wr - wr_hi.astype(F32)).astype(BF16)
    br = jnp.zeros((1, V7X_LANES), F32).at[0, :n_exp].set(b_router)
    h1, h1p, gate, tope, tile_counts = _merge(
        dn_alpha, hm, hf, x2, w_gate, b_gate, w_bm.astype(BF16), w_bf.astype(BF16), w_o.astype(BF16),
        ln1_g[None, :], ln1_b[None, :], wr_hi, wr_lo, br)

    n_blocks = -(-(t * TOP_K) // MOE_BLOCK) + N_EXPERTS
    dest, table = _routing(tile_counts, tope, n_blocks)
    dest_km = dest[:, :TOP_K].T.reshape(TOP_K * t)
    block_e, block_valid = table[:, 0], table[:, 1]
    xs = _sc_dispatch(dest_km, h1p, n_blocks * MOE_BLOCK)
    y_rows = _experts(block_e, block_valid, xs, w_gu, b_gu[:, None, :], w_dn, b_dn[:, None, :])
    yg = _sc_gather(dest_km, y_rows).reshape(TOP_K, t, d // 2)
    out = _combine(dn_alpha, h1, gate, ln2_g[None, :], ln2_b[None, :], yg)
    return out.reshape(bsz, s, d)


def kernel(x, w_in, b_in, m_conv_w, m_conv_b, m_norm_g, w_bm, w_bf, w_o, ln1_g, ln1_b,
           w_router, b_router, w_gu, b_gu, w_dn, b_dn, ln2_g, ln2_b):
    depth = w_in.shape[0]
    h = x
    for l in range(depth):
        h = _layer(h, depth, w_in[l], b_in[l], m_conv_w[l], m_conv_b[l], m_norm_g[l], w_bm[l], w_bf[l],
                   w_o[l], ln1_g[l], ln1_b[l], w_router[l], b_router[l], w_gu[l], b_gu[l], w_dn[l],
                   b_dn[l], ln2_g[l], ln2_b[l])
    return h
```

```python
import functools
import math

import jax
import jax.numpy as jnp
from jax import lax
from jax.experimental import pallas as pl
from jax.experimental.pallas import tpu as pltpu
from jax.experimental.pallas import tpu_sc as plsc

F32 = jnp.float32
BF16 = jnp.bfloat16
I32 = jnp.int32
U32 = jnp.uint32

M_HEADS = 4
M_DQK = 128
M_DV = 128
CONV_W = 4
F_HEADS = 8
F_DH = 64
N_EXPERTS = 32
TOP_K = 4
SWIGLU_ALPHA = 1.702
SWIGLU_LIMIT = 7.0
LN_EPS = 1e-5

M_W = M_HEADS * M_DV
F_W = F_HEADS * F_DH
MQK_W = 2 * M_HEADS * M_DQK

V7X_LANES = 128
V7X_SUBLANES = 8
V7X_VMEM_BYTES = 64 * 1024 * 1024
VMEM_LIMIT_BYTES = (V7X_VMEM_BYTES * 3) // 4
EXPERT_VMEM_LIMIT_BYTES = (V7X_VMEM_BYTES * 7) // 8
V7X_SC_CORES = 2
V7X_SC_SUBCORES = 16
V7X_SC_WORKERS = V7X_SC_CORES * V7X_SC_SUBCORES

INPROJ_ROWS = 512
MLSTM_CHUNK = 256
MLSTM_SEQS = 4
FOX_Q_BLOCK = 512
FOX_K_BLOCK = 256
MERGE_ROWS = 512
MERGE_SUB_ROWS = 256
ROUTE_ROWS = 1024
ROUTE_SUB_ROWS = 256
MOE_BLOCK = 512
SC_SCATTER_ROWS = 128
SC_GATHER_ROWS = 64
SC_GATHER_WAYS = 2
COMBINE_ROWS = 512

GATE_COLS = V7X_LANES
MI_LANE = 0
MF_LANE = M_HEADS
FF_LANE = 2 * M_HEADS


def _params(semantics):
    return pltpu.CompilerParams(dimension_semantics=semantics, vmem_limit_bytes=VMEM_LIMIT_BYTES)


def _log_sigmoid(x):
    return jnp.minimum(x, 0.0) - jnp.log1p(jnp.exp(-jnp.abs(x)))


def _sigmoid(x):
    return 1.0 / (1.0 + jnp.exp(-x))


def _dot(a, b):
    return jnp.dot(a, b, preferred_element_type=F32)


def _dot_nt(a, b):
    return lax.dot_general(a, b, (((1,), (1,)), ((), ())), preferred_element_type=F32)


def _dot_tn(a, b):
    return lax.dot_general(a, b, (((0,), (0,)), ((), ())), preferred_element_type=F32)


def _split3(x):
    hi = x.astype(BF16)
    r1 = x - hi.astype(F32)
    mid = r1.astype(BF16)
    lo = (r1 - mid.astype(F32)).astype(BF16)
    return hi, mid, lo


def _dot_mask_f32(mask_bf16, x):
    hi, mid, lo = _split3(x)
    return (_dot(mask_bf16, lo) + _dot(mask_bf16, mid)) + _dot(mask_bf16, hi)


def _pack_bf16_pairs(x):
    half = x.shape[1] // 2
    bits = lax.bitcast_convert_type(x.astype(BF16).astype(F32), U32)
    return (bits[:, :half] >> 16) | bits[:, half:]


def _unpack_bf16_pairs(words):
    lo = lax.bitcast_convert_type(words << 16, F32)
    hi = lax.bitcast_convert_type(words & jnp.uint32(0xFFFF0000), F32)
    return jnp.concatenate([lo, hi], axis=1)


def _tril_mask(n, strict=False):
    r = lax.broadcasted_iota(I32, (n, n), 0)
    c = lax.broadcasted_iota(I32, (n, n), 1)
    return (r > c) if strict else (r >= c)


_OFF_MQK = 0
_OFF_MV = _OFF_MQK + MQK_W
_OFF_MO = _OFF_MV + M_W
_OFF_FQ = _OFF_MO + M_W
_OFF_FK = _OFF_FQ + F_W
_OFF_FV = _OFF_FK + F_W
_OFF_GATES = _OFF_FV + F_W
_PACKED_COLS = _OFF_GATES + GATE_COLS


def _inproj_kernel(tiles_per_seq, x_ref, w_ref, b_ref, cw_ref, cb_ref,
                   mq_ref, mk_ref, mv_ref, mo_ref, fq_ref, fk_ref, fv_ref, gates_ref, ext_ref):
    tm = x_ref.shape[0]
    pad = V7X_SUBLANES
    xb = x_ref[...].astype(BF16)

    def seg(lo, width):
        return _dot(xb, w_ref[:, lo:lo + width]) + b_ref[:, lo:lo + width]

    @pl.when(pl.program_id(0) % tiles_per_seq == 0)
    def _():
        ext_ref[0:pad, :] = jnp.zeros((pad, MQK_W), F32)

    ext_ref[pad:pad + tm, :] = seg(_OFF_MQK, MQK_W)
    y = cb_ref[...] + cw_ref[CONV_W - 1:CONV_W, :] * ext_ref[pad:pad + tm, :]
    for k in range(CONV_W - 1):
        shift = CONV_W - 1 - k
        y = y + cw_ref[k:k + 1, :] * ext_ref[pad - shift:pad - shift + tm, :]
    ext_ref[0:pad, :] = ext_ref[tm:tm + pad, :]
    act = y * _sigmoid(y)
    mq_ref[...] = act[:, :MQK_W // 2].astype(BF16)
    mk_ref[...] = act[:, MQK_W // 2:] * (M_DQK ** -0.5)
    mv_ref[...] = seg(_OFF_MV, M_W).astype(BF16)
    mo_ref[...] = seg(_OFF_MO, M_W)
    fq_ref[...] = (seg(_OFF_FQ, F_W) * (F_DH ** -0.5)).astype(BF16)
    fk_ref[...] = seg(_OFF_FK, F_W).astype(BF16)
    fv_ref[...] = seg(_OFF_FV, F_W).astype(BF16)
    gates_ref[...] = seg(_OFF_GATES, GATE_COLS)


def _inproj(x2, w_packed, b_packed, conv_w, conv_b, s):
    t, d = x2.shape
    tm = INPROJ_ROWS
    row = lambda i: (i, 0)
    const = lambda i: (0, 0)
    out_shapes = (
        jax.ShapeDtypeStruct((t, MQK_W // 2), BF16),
        jax.ShapeDtypeStruct((t, MQK_W // 2), F32),
        jax.ShapeDtypeStruct((t, M_W), BF16),
        jax.ShapeDtypeStruct((t, M_W), F32),
        jax.ShapeDtypeStruct((t, F_W), BF16),
        jax.ShapeDtypeStruct((t, F_W), BF16),
        jax.ShapeDtypeStruct((t, F_W), BF16),
        jax.ShapeDtypeStruct((t, GATE_COLS), F32),
    )
    return pl.pallas_call(
        functools.partial(_inproj_kernel, s // tm),
        grid=(t // tm,),
        in_specs=[
            pl.BlockSpec((tm, d), row),
            pl.BlockSpec((d, _PACKED_COLS), const),
            pl.BlockSpec((1, _PACKED_COLS), const),
            pl.BlockSpec((CONV_W, MQK_W), const),
            pl.BlockSpec((1, MQK_W), const),
        ],
        out_specs=tuple(pl.BlockSpec((tm, o.shape[1]), row) for o in out_shapes),
        out_shape=out_shapes,
        scratch_shapes=[pltpu.VMEM((tm + V7X_SUBLANES, MQK_W), F32)],
        compiler_params=_params(("arbitrary",)),
        name="inproj",
    )(x2, w_packed, b_packed, conv_w, conv_b)


def _fox_cumsum_kernel(g_ref, ccol_ref, crel_k_ref, crel_q_ref):
    s = g_ref.shape[0]
    cb = FOX_K_BLOCK
    per_q = FOX_Q_BLOCK // cb
    tri = _tril_mask(cb).astype(BF16)
    carry = jnp.zeros((1, GATE_COLS), F32)
    for j in range(s // cb):
        rows = slice(j * cb, (j + 1) * cb)
        if j % per_q == 0:
            q_carry = carry
        within = _dot_mask_f32(tri, _log_sigmoid(g_ref[rows, :]))
        crel_k_ref[rows, :] = within
        crel_q_ref[rows, :] = within + (carry - q_carry)
        ccol_ref[rows, :] = within + carry
        carry = carry + within[cb - 1:cb, :]


def _fox_cumsum(gates, bsz, s):
    t = gates.shape[0]
    spec = pl.BlockSpec((s, GATE_COLS), lambda b: (b, 0))
    shape = jax.ShapeDtypeStruct((t, GATE_COLS), F32)
    return pl.pallas_call(
        _fox_cumsum_kernel,
        grid=(bsz,),
        in_specs=[spec],
        out_specs=(spec, spec, spec),
        out_shape=(shape, shape, shape),
        compiler_params=_params(("parallel",)),
        name="fox_cumsum",
    )(gates)


def _mlstm_kernel(mq_ref, mk_ref, mv_ref, mo_ref, gates_ref, ng_ref, hm_ref, state_ref, m_ref):
    @pl.when(pl.program_id(1) == 0)
    def _():
        state_ref[...] = jnp.zeros(state_ref.shape, F32)
        m_ref[...] = jnp.zeros(m_ref.shape, F32)

    seqs = range(mq_ref.shape[0])
    states = [[state_ref[bb, h] for h in range(M_HEADS)] for bb in seqs]
    maxes = [[m_ref[bb, h][0:1, 0:1] for h in range(M_HEADS)] for bb in seqs]
    results = [_mlstm_chunk(mq_ref.at[bb], mk_ref.at[bb], mv_ref.at[bb], mo_ref.at[bb], gates_ref.at[bb],
                            ng_ref, states[bb], maxes[bb]) for bb in seqs]
    for bb in seqs:
        for h, (out_h, state_h, m_h) in enumerate(results[bb]):
            hm_ref[bb, :, h * M_DV:(h + 1) * M_DV] = out_h
            state_ref[bb, h] = state_h
            m_ref[bb, h] = jnp.broadcast_to(m_h, m_ref.shape[2:])


def _mlstm_chunk(mq_ref, mk_ref, mv_ref, mo_ref, gates_ref, ng_ref, states, maxes):
    L = MLSTM_CHUNK
    reps = L // V7X_LANES
    results = []
    gates = gates_ref[...]
    bfull = _dot_mask_f32(_tril_mask(L).astype(BF16), _log_sigmoid(gates))
    b_rows = bfull.T
    z_all = gates - pltpu.roll(bfull, shift=V7X_LANES - (MF_LANE - MI_LANE), axis=1)
    visible = (lax.broadcasted_iota(I32, (L, L), 0) <= lax.broadcasted_iota(I32, (L, L), 1))
    ones_rows = (lax.broadcasted_iota(I32, (M_DV, L), 0) == 0).astype(BF16)

    for h in range(M_HEADS):
        b_row = b_rows[MF_LANE + h:MF_LANE + h + 1, :]
        g_tot = b_row[:, L - 1:L]
        m_prev = maxes[h]
        z = jnp.broadcast_to(z_all[:, MI_LANE + h:MI_LANE + h + 1], (L, V7X_LANES))

        q_h = mq_ref[:, h * M_DQK:(h + 1) * M_DQK]
        k_f = mk_ref[:, h * M_DQK:(h + 1) * M_DQK]
        k_h = k_f.astype(BF16)
        v_t = mv_ref[:, h * M_DV:(h + 1) * M_DV].astype(F32).T.astype(BF16)
        cn_t = states[h]

        dlog = jnp.where(visible, b_row + jnp.tile(z, (1, reps)), -jnp.inf)
        inter_log = b_row + m_prev
        m_t = jnp.maximum(inter_log, jnp.max(dlog, axis=0, keepdims=True))
        w_inter = jnp.exp(inter_log - m_t)
        qkw = _dot_nt(k_h, q_h) * jnp.exp(dlog - m_t)
        qc = _dot_nt(cn_t.astype(BF16), q_h)
        num = w_inter * qc[:M_DV, :] + _dot(v_t, qkw.astype(BF16))
        den = w_inter * qc[M_DV:M_DV + 1, :] + jnp.sum(qkw, axis=0, keepdims=True)
        hh = num / jnp.maximum(jnp.abs(den), jnp.exp(-m_t))

        mu = jnp.mean(hh, axis=0, keepdims=True)
        dv = hh - mu
        var = jnp.mean(dv * dv, axis=0, keepdims=True)
        hn = (dv * lax.rsqrt(var + LN_EPS)) * jnp.tile(ng_ref[h * M_DV:(h + 1) * M_DV, :], (1, reps))
        out_h = (_sigmoid(mo_ref[:, h * M_DV:(h + 1) * M_DV]) * hn.T).astype(BF16)

        a = g_tot + z
        m_new = jnp.maximum(g_tot + m_prev, jnp.max(a, axis=0, keepdims=True)[:, 0:1])
        decay = jnp.exp(g_tot + m_prev - m_new)
        kw = (k_f * jnp.exp(a - m_new)).astype(BF16)
        v_aug = jnp.concatenate([v_t, ones_rows], axis=0)
        results.append((out_h, decay * cn_t + _dot(v_aug, kw), m_new))
    return results


def _mlstm(mq, mk, mv, mo, gates, norm_g, bsz, s):
    t = mq.shape[0]
    L = MLSTM_CHUNK
    nb = MLSTM_SEQS
    seq = lambda a: a.reshape(bsz, s, a.shape[1])
    blk = lambda width: pl.BlockSpec((nb, L, width), lambda g, n: (g, n, 0))
    const = lambda g, n: (0, 0)
    hm = pl.pallas_call(
        _mlstm_kernel,
        grid=(bsz // nb, s // L),
        in_specs=[
            blk(MQK_W // 2), blk(MQK_W // 2), blk(M_W), blk(M_W), blk(GATE_COLS),
            pl.BlockSpec((M_W, V7X_LANES), const),
        ],
        out_specs=blk(M_W),
        out_shape=jax.ShapeDtypeStruct((bsz, s, M_W), BF16),
        scratch_shapes=[
            pltpu.VMEM((nb, M_HEADS, 2 * M_DV, M_DQK), F32),
            pltpu.VMEM((nb, M_HEADS, V7X_SUBLANES, V7X_LANES), F32),
        ],
        compiler_params=_params(("parallel", "arbitrary")),
        name="mlstm",
    )(seq(mq), seq(mk), seq(mv), seq(mo), seq(gates),
      jnp.broadcast_to(norm_g[:, None], (M_W, V7X_LANES)))
    return hm.reshape(t, M_W)


_FOX_FEATURES = 3


def _fox_operands(x, c_tile, c_lane, key_side):
    rows = x.shape[0]
    nf = _FOX_FEATURES
    first = 0 if key_side else nf
    src = lax.broadcasted_iota(I32, (V7X_LANES, V7X_LANES), 0) - c_lane
    dst = lax.broadcasted_iota(I32, (V7X_LANES, V7X_LANES), 1)
    feats = jnp.zeros((rows, V7X_LANES), F32)
    for n, part in enumerate(_split3(c_tile)):
        place = jnp.logical_or(jnp.logical_and(src == 0, dst == F_DH + first + n),
                               jnp.logical_and(src == 1, dst == first + n))
        feats = feats + _dot(part, place.astype(BF16))
    lane = lax.broadcasted_iota(I32, (rows, V7X_LANES), 1)
    within = lane % F_DH
    const_lanes = jnp.logical_and(within >= nf - first, within < 2 * nf - first)
    feats = jnp.where(const_lanes, 1.0 if key_side else -1.0, feats).astype(BF16)
    low = lane < F_DH
    return jnp.where(low, x, feats), jnp.where(low, feats, x)


def _fox_kernel(q_ref, k_ref, v_ref, ccol_ref, crel_k_ref, crel_q_ref, o_ref, qaug_ref, kaug_ref, vt_ref,
                st_a0, st_a1, st_b0, st_b1, pe_a0, pe_a1, pe_b0, pe_b1, acc0, acc1):
    blk = FOX_K_BLOCK
    tq = FOX_Q_BLOCK
    assert tq == 2 * blk
    strips = tq // V7X_LANES
    st_a_refs, st_b_refs = (st_a0, st_a1), (st_b0, st_b1)
    pe_a_refs, pe_b_refs = (pe_a0, pe_a1), (pe_b0, pe_b1)
    acc_refs = (acc0, acc1)
    p = pl.program_id(1)
    i = pl.program_id(2)

    @pl.when(i == 0)
    def _():
        c_lane = FF_LANE + 2 * p
        qaug_ref[0], qaug_ref[1] = _fox_operands(q_ref[...], crel_q_ref[...], c_lane, False)
        kaug_ref[0], kaug_ref[1] = _fox_operands(k_ref[...], crel_k_ref[...], c_lane, True)
        v_t = v_ref[...].astype(F32).T
        for j in range(vt_ref.shape[0]):
            vt_ref[j] = v_t[:, j * blk:(j + 1) * blk].astype(BF16)

    q_start = pl.multiple_of(i * tq, tq)
    q_heads = [qaug_ref[hh, pl.ds(q_start, tq), :] for hh in range(2)]
    key_row = lax.broadcasted_iota(I32, (blk, V7X_LANES), 0)
    query_col = lax.broadcasted_iota(I32, (blk, V7X_LANES), 1)
    last_chunk = 2 * i + 1
    head_lane = lax.broadcasted_iota(I32, (1, GATE_COLS), 1) - (FF_LANE + 2 * p)

    def c_before(position, hh):
        row = ccol_ref[pl.ds(jnp.maximum(position - 1, 0), 1), :]
        keep = jnp.logical_and(head_lane == hh, position > 0)
        return jnp.sum(jnp.where(keep, row, 0.0), axis=-1, keepdims=True)

    c_query0 = [c_before(q_start, hh) for hh in range(2)]

    def scores(j, hh):
        start = pl.multiple_of(j * blk, blk)
        return _dot_nt(kaug_ref[hh, pl.ds(start, blk), :], q_heads[hh])

    def softmax_update(st_ref, pe_ref, m_old, l_old, base, key_minus_query=None):
        alphas, ms, ls = [], [], []
        for c in range(strips):
            cols = slice(c * V7X_LANES, (c + 1) * V7X_LANES)
            gap = None if key_minus_query is None else key_minus_query - c * V7X_LANES
            if gap is not None and gap - (V7X_LANES - 1) > 0:
                pe_ref[:, cols] = jnp.zeros((blk, V7X_LANES), BF16)
                alphas.append(jnp.ones((1, V7X_LANES), F32))
                ms.append(m_old[:, cols])
                ls.append(l_old[:, cols])
                continue
            st = st_ref[:, cols]
            if gap is not None and gap + (blk - 1) > 0:
                st = jnp.where(key_row + gap <= query_col, st, -jnp.inf)
            m_new = jnp.maximum(m_old[:, cols], jnp.max(st, axis=0, keepdims=True) + base)
            alpha = jnp.exp(m_old[:, cols] - m_new)
            pe = jnp.exp(st - (m_new - base))
            pe_ref[:, cols] = pe.astype(BF16)
            alphas.append(alpha)
            ms.append(m_new)
            ls.append(alpha * l_old[:, cols] + jnp.sum(pe, axis=0, keepdims=True))
        cat = lambda parts: jnp.concatenate(parts, axis=1)
        return cat(alphas), cat(ms), cat(ls)

    def pair(mi, carry, diagonal=False):
        a = 2 * mi
        b = a + 1
        v_prev = vt_ref[jnp.maximum(a - 1, 0)]
        v_a = vt_ref[a]
        partial = []
        for hh in range(2):
            alpha_prev = carry[hh][0]
            partial.append(alpha_prev * acc_refs[hh][...] + _dot(v_prev, pe_b_refs[hh][...]))
            st_b_refs[hh][...] = scores(b, hh)
        stats = []
        for hh in range(2):
            _, m_old, l_old = carry[hh]
            stats.append(softmax_update(st_a_refs[hh], pe_a_refs[hh], m_old, l_old,
                                        c_query0[hh] - c_before(a * blk, hh), 0 if diagonal else None))
        for hh in range(2):
            alpha_a = stats[hh][0]
            acc_refs[hh][...] = alpha_a * partial[hh] + _dot(v_a, pe_a_refs[hh][...])
            if not diagonal:
                st_a_refs[hh][...] = scores(a + 2, hh)
        return tuple(softmax_update(st_b_refs[hh], pe_b_refs[hh], stats[hh][1], stats[hh][2],
                                    c_query0[hh] - c_before(b * blk, hh), blk if diagonal else None)
                     for hh in range(2))

    for hh in range(2):
        st_a_refs[hh][...] = scores(0, hh)
        pe_b_refs[hh][...] = jnp.zeros((blk, tq), BF16)
        acc_refs[hh][...] = jnp.zeros((V7X_LANES, tq), F32)
    init = tuple((jnp.ones((1, tq), F32), jnp.full((1, tq), -jnp.inf, F32), jnp.zeros((1, tq), F32))
                 for _ in range(2))
    final = pair(i, lax.fori_loop(0, i, pair, init), diagonal=True)
    v_last = vt_ref[last_chunk]
    outs = []
    for hh in range(2):
        alpha, _, l_fin = final[hh]
        outs.append((alpha * acc_refs[hh][...] + _dot(v_last, pe_b_refs[hh][...])) / l_fin)
    row = lax.broadcasted_iota(I32, (V7X_LANES, tq), 0)
    o_t = jnp.where(row < F_DH, outs[0], outs[1])
    o_ref[...] = o_t.T.astype(BF16)


def _fox_attention(fq, fk, fv, ccol, crel_k, crel_q, bsz, s):
    t = fq.shape[0]
    blk = FOX_K_BLOCK
    tq = FOX_Q_BLOCK
    nq = s // tq
    pairs = F_HEADS // 2
    qmap = lambda b, p, i: (b * nq + i, p)
    kvmap = lambda b, p, i: (b, p)
    return pl.pallas_call(
        _fox_kernel,
        grid=(bsz, pairs, nq),
        in_specs=[
            pl.BlockSpec((s, V7X_LANES), kvmap),
            pl.BlockSpec((s, V7X_LANES), kvmap),
            pl.BlockSpec((s, V7X_LANES), kvmap),
            pl.BlockSpec((s, GATE_COLS), lambda b, p, i: (b, 0)),
            pl.BlockSpec((s, GATE_COLS), lambda b, p, i: (b, 0)),
            pl.BlockSpec((s, GATE_COLS), lambda b, p, i: (b, 0)),
        ],
        out_specs=pl.BlockSpec((tq, V7X_LANES), qmap),
        out_shape=jax.ShapeDtypeStruct((t, F_W), BF16),
        scratch_shapes=[
            pltpu.VMEM((2, s, V7X_LANES), BF16),
            pltpu.VMEM((2, s, V7X_LANES), BF16),
            pltpu.VMEM((s // blk, V7X_LANES, blk), BF16),
        ] + [pltpu.VMEM((blk, tq), F32)] * 4 + [pltpu.VMEM((blk, tq), BF16)] * 4
          + [pltpu.VMEM((V7X_LANES, tq), F32)] * 2,
        compiler_params=_params(("parallel", "parallel", "arbitrary")),
        name="fox_attention",
    )(fq, fk, fv, ccol, crel_k, crel_q)


def _layer_norm_rows(r, g, b):
    mu = jnp.mean(r, axis=-1, keepdims=True)
    d = r - mu
    var = jnp.mean(d * d, axis=-1, keepdims=True)
    return (d * lax.rsqrt(var + LN_EPS)) * g + b


def _merge_kernel(dn_alpha, hm_ref, hf_ref, x_ref, wg_ref, bg_ref, wbm_ref, wbf_ref, wo_ref,
                  g_ref, b_ref, wrh_ref, wrl_ref, br_ref, h1_ref, h1p_ref, gate_ref, tope_ref, cnt_ref,
                  resid_ref):
    i = pl.program_id(0)
    slot = i % 2

    @pl.when(i == 0)
    def _():
        resid_ref[1] = jnp.zeros(resid_ref.shape[1:], F32)

    subs = [slice(n * MERGE_SUB_ROWS, (n + 1) * MERGE_SUB_ROWS)
            for n in range(x_ref.shape[0] // MERGE_SUB_ROWS)]
    counts = jnp.zeros((1, V7X_LANES), F32)
    for rows in subs:
        counts = counts + _merge_tail(resid_ref[1 - slot, rows, :], rows, g_ref, b_ref, wrh_ref, wrl_ref,
                                      br_ref, h1_ref, h1p_ref, gate_ref, tope_ref)
    sub = lax.broadcasted_iota(I32, cnt_ref.shape, 0)
    cnt_ref[...] = jnp.where(sub == 0, counts, 0.0)

    for rows in subs:
        x = x_ref[rows, :]
        d = x.shape[1]
        gmf = _dot(x.astype(BF16), wg_ref[...]) + bg_ref[...]
        ym = _dot(hm_ref[rows, :], wbm_ref[...])
        yf = _dot(hf_ref[rows, :], wbf_ref[...])
        y = _sigmoid(gmf[:, :d]) * ym + _sigmoid(gmf[:, d:]) * yf
        resid_ref[slot, rows, :] = dn_alpha * x + _dot(y.astype(BF16), wo_ref[...])


def _merge_tail(resid, rows, g_ref, b_ref, wrh_ref, wrl_ref, br_ref, h1_ref, h1p_ref, gate_ref, tope_ref):
    h1 = _layer_norm_rows(resid, g_ref[...], b_ref[...])
    h1_ref[rows, :] = h1

    h1p_ref[rows, :] = _pack_bf16_pairs(h1)
    hb = h1.astype(BF16)

    lo = (h1 - hb.astype(F32)).astype(BF16)
    logits = (_dot(lo, wrh_ref[...]) + _dot(hb, wrl_ref[...])) + _dot(hb, wrh_ref[...]) + br_ref[...]
    tm = logits.shape[0]
    lane = lax.broadcasted_iota(I32, (tm, V7X_LANES), 1)
    vals = jnp.where(lane < N_EXPERTS, logits, -jnp.inf)
    top_v, top_i = [], []
    for _ in range(TOP_K):
        mx = jnp.max(vals, axis=-1, keepdims=True)
        idx = jnp.min(jnp.where(vals == mx, lane, V7X_LANES), axis=-1, keepdims=True)
        top_v.append(mx)
        top_i.append(idx)
        vals = jnp.where(lane == idx, -jnp.inf, vals)
    ex = [jnp.exp(v - top_v[0]) for v in top_v]
    tot = ex[0]
    for e in ex[1:]:
        tot = tot + e
    gate = jnp.zeros((tm, V7X_LANES), F32)
    tope = jnp.zeros((tm, V7X_LANES), I32)
    member = jnp.zeros((tm, V7X_LANES), F32)
    for k in range(TOP_K):
        gate = jnp.where(lane == k, ex[k] / tot, gate)
        tope = jnp.where(lane == k, top_i[k], tope)
        member = member + (lane == top_i[k]).astype(F32)
    gate_ref[rows, :] = gate
    tope_ref[rows, :] = tope
    return jnp.sum(member, axis=0, keepdims=True)


def _merge(dn_alpha, hm, hf, x2, wg, bg, wbm, wbf, wo, ln_g, ln_b, wr_hi, wr_lo, br):
    t, d = x2.shape
    tm = MERGE_ROWS
    nt = t // tm
    row = lambda i: (jnp.minimum(i, nt - 1), 0)
    out_row = lambda i: (jnp.maximum(i - 1, 0), 0)
    const = lambda i: (0, 0)
    full = lambda a: pl.BlockSpec(a.shape, const)
    return pl.pallas_call(
        functools.partial(_merge_kernel, dn_alpha),
        grid=(nt + 1,),
        in_specs=[
            pl.BlockSpec((tm, M_W), row),
            pl.BlockSpec((tm, F_W), row),
            pl.BlockSpec((tm, d), row),
            full(wg), full(bg), full(wbm), full(wbf), full(wo), full(ln_g), full(ln_b),
            full(wr_hi), full(wr_lo), full(br),
        ],
        out_specs=(
            pl.BlockSpec((tm, d), out_row),
            pl.BlockSpec((tm, d // 2), out_row),
            pl.BlockSpec((tm, V7X_LANES), out_row),
            pl.BlockSpec((tm, V7X_LANES), out_row),
            pl.BlockSpec((V7X_SUBLANES, V7X_LANES), out_row),
        ),
        out_shape=(
            jax.ShapeDtypeStruct((t, d), F32),
            jax.ShapeDtypeStruct((t, d // 2), U32),
            jax.ShapeDtypeStruct((t, V7X_LANES), F32),
            jax.ShapeDtypeStruct((t, V7X_LANES), I32),
            jax.ShapeDtypeStruct((t // tm * V7X_SUBLANES, V7X_LANES), F32),
        ),
        scratch_shapes=[pltpu.VMEM((2, tm, d), F32)],
        compiler_params=_params(("arbitrary",)),
        name="merge_ln1_router",
    )(hm, hf, x2, wg, bg, wbm, wbf, wo, ln_g, ln_b, wr_hi, wr_lo, br)


def _lane_cumsum(x):
    lane = lax.broadcasted_iota(I32, x.shape, 1)
    d = 1
    while d < V7X_LANES:
        x = x + jnp.where(lane >= d, pltpu.roll(x, shift=d, axis=1), 0.0)
        d *= 2
    return x


def _routing_kernel(cnt_ref, tope_ref, dest_ref, table_ref, run_ref, start_ref):
    sb = ROUTE_SUB_ROWS

    @pl.when(pl.program_id(0) == 0)
    def _():
        total = jnp.sum(cnt_ref[...], axis=0, keepdims=True)
        counts = jnp.broadcast_to(total, (V7X_SUBLANES, V7X_LANES))
        padded = jnp.ceil(counts * (1.0 / MOE_BLOCK)) * MOE_BLOCK
        pad_end = _lane_cumsum(padded)
        pad_start = pad_end - padded
        start_ref[...] = pad_start
        run_ref[...] = jnp.zeros(run_ref.shape, F32)
        nb = table_ref.shape[0]
        blk = lax.broadcasted_iota(I32, (nb, V7X_LANES), 0).astype(F32) * MOE_BLOCK
        ln = lax.broadcasted_iota(I32, (nb, V7X_LANES), 1)
        done = jnp.logical_and(pad_end[0:1, :] <= blk, ln < N_EXPERTS)
        be = jnp.minimum(jnp.sum(done.astype(F32), axis=-1, keepdims=True), N_EXPERTS - 1.0)
        onehot = ln == be.astype(I32)
        cnt_e = jnp.sum(jnp.where(onehot, counts[0:1, :], 0.0), axis=-1, keepdims=True)
        start_e = jnp.sum(jnp.where(onehot, pad_start[0:1, :], 0.0), axis=-1, keepdims=True)
        valid = jnp.clip(cnt_e - (blk[:, 0:1] - start_e), 0.0, float(MOE_BLOCK))
        table_ref[...] = jnp.where(ln == 0, be.astype(I32),
                                   jnp.where(ln == 1, valid.astype(I32), 0))

    earlier = _tril_mask(sb, strict=True).astype(BF16)
    lane = lax.broadcasted_iota(I32, (sb, V7X_LANES), 1)
    for j in range(tope_ref.shape[0] // sb):
        tope = tope_ref[j * sb:(j + 1) * sb, :]
        hit = [lane == tope[:, k:k + 1] for k in range(TOP_K)]
        member = jnp.zeros((sb, V7X_LANES), F32)
        for k in range(TOP_K):
            member = member + hit[k].astype(F32)
        base = _dot(earlier, member.astype(BF16)) + (run_ref[0:1, :] + start_ref[0:1, :])
        dest = jnp.zeros((sb, V7X_LANES), F32)
        for k in range(TOP_K):
            dk = jnp.sum(jnp.where(hit[k], base, 0.0), axis=-1, keepdims=True)
            dest = jnp.where(lane == k, dk, dest)
        dest_ref[:, j * sb:(j + 1) * sb] = dest.T[0:V7X_SUBLANES, :].astype(I32)
        run_ref[...] = run_ref[...] + jnp.sum(member, axis=0, keepdims=True)


def _routing(tile_counts, tope, n_blocks):
    t = tope.shape[0]
    tr = ROUTE_ROWS
    return pl.pallas_call(
        _routing_kernel,
        grid=(t // tr,),
        in_specs=[pl.BlockSpec(tile_counts.shape, lambda i: (0, 0)),
                  pl.BlockSpec((tr, V7X_LANES), lambda i: (i, 0))],
        out_specs=(
            pl.BlockSpec((V7X_SUBLANES, tr), lambda i: (0, i)),
            pl.BlockSpec((n_blocks, V7X_LANES), lambda i: (0, 0)),
        ),
        out_shape=(
            jax.ShapeDtypeStruct((V7X_SUBLANES, t), I32),
            jax.ShapeDtypeStruct((n_blocks, V7X_LANES), I32),
        ),
        scratch_shapes=[
            pltpu.VMEM((V7X_SUBLANES, V7X_LANES), F32),
            pltpu.VMEM((V7X_SUBLANES, V7X_LANES), F32),
        ],
        compiler_params=_params(("arbitrary",)),
        name="routing",
    )(tile_counts, tope)


def _sc_worker_id():
    return lax.axis_index("s") * V7X_SC_CORES + lax.axis_index("c")


def _sc_mesh():
    return plsc.VectorSubcoreMesh(core_axis_name="c", subcore_axis_name="s",
                                  num_cores=V7X_SC_CORES, num_subcores=V7X_SC_SUBCORES)


def _sc_dispatch(dest_km, h1p, n_rows):
    t, w = h1p.shape
    per_worker = t // V7X_SC_WORKERS
    ch = SC_SCATTER_ROWS

    @functools.partial(
        pl.kernel, mesh=_sc_mesh(),
        out_type=jax.ShapeDtypeStruct((n_rows, w), h1p.dtype),
        scratch_types=[pltpu.VMEM((ch, w), h1p.dtype)]
        + [pltpu.VMEM((ch,), I32)] * TOP_K + [pltpu.SemaphoreType.DMA] * TOP_K,
        name="sc_dispatch",
    )
    def scatter_rows(dest_hbm, h1p_hbm, xs_hbm, rows_v, *idx_and_sems):
        idx_refs, sems = idx_and_sems[:TOP_K], idx_and_sems[TOP_K:]
        first = _sc_worker_id() * per_worker

        @pl.loop(0, per_worker // ch)
        def _(j):
            base = first + j * ch
            pltpu.sync_copy(h1p_hbm.at[pl.ds(base, ch)], rows_v)
            copies = []
            for k in range(TOP_K):
                pltpu.sync_copy(dest_hbm.at[pl.ds(k * t + base, ch)], idx_refs[k])
                copies.append(pltpu.async_copy(rows_v, xs_hbm.at[idx_refs[k]], sems[k]))
            for copy in copies:
                copy.wait()

    return scatter_rows(dest_km, h1p)


def _sc_gather(dest_km, y_rows):
    n = dest_km.shape[0]
    w = y_rows.shape[1]
    per_worker = n // V7X_SC_WORKERS
    ch = SC_GATHER_ROWS
    ways = SC_GATHER_WAYS

    @functools.partial(
        pl.kernel, mesh=_sc_mesh(),
        out_type=jax.ShapeDtypeStruct((n, w), y_rows.dtype),
        scratch_types=[pltpu.VMEM((ch,), I32)] * ways + [pltpu.VMEM((ch, w), y_rows.dtype)] * ways
        + [pltpu.SemaphoreType.DMA] * (2 * ways),
        name="sc_gather",
    )
    def gather_rows(dest_hbm, y_hbm, out_hbm, *scratch):
        idx_refs, row_refs = scratch[:ways], scratch[ways:2 * ways]
        gather_sems, store_sems = scratch[2 * ways:3 * ways], scratch[3 * ways:]
        first = _sc_worker_id() * per_worker

        @pl.loop(0, per_worker // (ch * ways))
        def _(j):
            bases = [first + (j * ways + u) * ch for u in range(ways)]
            gathers = []
            for u in range(ways):
                pltpu.sync_copy(dest_hbm.at[pl.ds(bases[u], ch)], idx_refs[u])
                gathers.append(pltpu.async_copy(y_hbm.at[idx_refs[u]], row_refs[u], gather_sems[u]))
            stores = []
            for u in range(ways):
                gathers[u].wait()
                stores.append(pltpu.async_copy(row_refs[u], out_hbm.at[pl.ds(bases[u], ch)], store_sems[u]))
            for store in stores:
                store.wait()

    return gather_rows(dest_km, y_rows)


def _expert_kernel(be_ref, nv_ref, xs_ref, wgu_f32_ref, bgu_ref, wdn_f32_ref, bdn_ref, y_ref,
                   wgu_slots, wdn_slots, slot_ref):
    i = pl.program_id(0)
    n_blocks = pl.num_programs(0) - 1
    nv = jnp.where(i > 0, nv_ref[jnp.maximum(i - 1, 0)], 0)
    half = MOE_BLOCK // 2

    @pl.when(i == 0)
    def _():
        slot_ref[0] = 0

    slot = slot_ref[0]
    wgu_ref = wgu_slots.at[slot]
    wdn_ref = wdn_slots.at[slot]

    def ffn(rows):
        x = _unpack_bf16_pairs(xs_ref[rows, :])
        rowid = rows.start + lax.broadcasted_iota(I32, x.shape, 0)
        x = jnp.where(rowid < nv, x, 0.0).astype(BF16)
        gu = _dot(x, wgu_ref[...]) + bgu_ref[...]
        f = gu.shape[1] // 2
        glu = jnp.minimum(gu[:, :f], SWIGLU_LIMIT)
        lin = jnp.clip(gu[:, f:], -SWIGLU_LIMIT, SWIGLU_LIMIT)
        act = glu * _sigmoid(SWIGLU_ALPHA * glu) * (lin + 1.0)
        y_ref[rows, :] = _pack_bf16_pairs(_dot(act.astype(BF16), wdn_ref[...]) + bdn_ref[...])

    @pl.when(nv == 0)
    def _():
        y_ref[...] = jnp.zeros(y_ref.shape, U32)

    @pl.when(jnp.logical_and(nv > 0, nv <= half))
    def _():
        ffn(slice(0, half))
        y_ref[half:, :] = jnp.zeros((MOE_BLOCK - half, y_ref.shape[1]), U32)

    @pl.when(nv > half)
    def _():
        ffn(slice(0, half))
        ffn(slice(half, MOE_BLOCK))

    new_expert = jnp.logical_or(i == 0, be_ref[jnp.minimum(i, n_blocks - 1)] != be_ref[jnp.maximum(i - 1, 0)])

    @pl.when(jnp.logical_and(i < n_blocks, new_expert))
    def _():
        wgu_slots[1 - slot] = wgu_f32_ref[...].astype(BF16)
        wdn_slots[1 - slot] = wdn_f32_ref[...].astype(BF16)
        slot_ref[0] = 1 - slot


def _experts(block_e, block_valid, xs, wgu, bgu, wdn, bdn):
    n_rows, w = xs.shape
    e, d, f2 = wgu.shape
    n_blocks = n_rows // MOE_BLOCK
    ahead = lambda i, be: be[jnp.minimum(i, n_blocks - 1)]
    behind = lambda i: jnp.maximum(i - 1, 0)
    grid_spec = pltpu.PrefetchScalarGridSpec(
        num_scalar_prefetch=2,
        grid=(n_blocks + 1,),
        in_specs=[
            pl.BlockSpec((MOE_BLOCK, w), lambda i, be, nv: (behind(i), 0)),
            pl.BlockSpec((None, d, f2), lambda i, be, nv: (ahead(i, be), 0, 0)),
            pl.BlockSpec((None, 1, f2), lambda i, be, nv: (be[behind(i)], 0, 0)),
            pl.BlockSpec((None, f2 // 2, d), lambda i, be, nv: (ahead(i, be), 0, 0)),
            pl.BlockSpec((None, 1, d), lambda i, be, nv: (be[behind(i)], 0, 0)),
        ],
        out_specs=pl.BlockSpec((MOE_BLOCK, d // 2), lambda i, be, nv: (behind(i), 0)),
        scratch_shapes=[pltpu.VMEM((2, d, f2), BF16), pltpu.VMEM((2, f2 // 2, d), BF16),
                        pltpu.SMEM((1,), I32)],
    )
    return pl.pallas_call(
        _expert_kernel,
        grid_spec=grid_spec,
        out_shape=jax.ShapeDtypeStruct((n_rows, d // 2), U32),
        compiler_params=pltpu.CompilerParams(dimension_semantics=("arbitrary",),
                                             vmem_limit_bytes=EXPERT_VMEM_LIMIT_BYTES),
        name="experts",
    )(block_e, block_valid, xs, wgu, bgu, wdn, bdn)


def _combine_kernel(dn_alpha, h1_ref, gate_ref, g_ref, b_ref, yg_ref, o_ref):
    gate = gate_ref[...]
    ffn = gate[:, 0:1] * _unpack_bf16_pairs(yg_ref[0])
    for k in range(1, TOP_K):
        ffn = ffn + gate[:, k:k + 1] * _unpack_bf16_pairs(yg_ref[k])
    o_ref[...] = _layer_norm_rows(dn_alpha * h1_ref[...] + ffn, g_ref[...], b_ref[...])


def _combine(dn_alpha, h1, gate, ln_g, ln_b, yg):
    t, d = h1.shape
    tc = COMBINE_ROWS
    row = lambda i: (i, 0)
    const = lambda i: (0, 0)
    return pl.pallas_call(
        functools.partial(_combine_kernel, dn_alpha),
        grid=(t // tc,),
        in_specs=[
            pl.BlockSpec((tc, d), row),
            pl.BlockSpec((tc, V7X_LANES), row),
            pl.BlockSpec((1, d), const),
            pl.BlockSpec((1, d), const),
            pl.BlockSpec((TOP_K, tc, d // 2), lambda i: (0, i, 0)),
        ],
        out_specs=pl.BlockSpec((tc, d), row),
        out_shape=jax.ShapeDtypeStruct((t, d), F32),
        compiler_params=_params(("parallel",)),
        name="combine_ln2",
    )(h1, gate, ln_g, ln_b, yg)


def _pack_in_proj(w_in, b_in):
    d = w_in.shape[0]
    o = 0
    cols = {}
    for name, width in (("mqk", MQK_W), ("mv", M_W), ("mo", M_W), ("mi", M_HEADS), ("mf", M_HEADS),
                        ("fq", F_W), ("fk", F_W), ("fv", F_W), ("ff", F_HEADS), ("gm", d), ("gf", d)):
        cols[name] = (o, o + width)
        o += width

    def take(a, names):
        return [a[..., cols[n][0]:cols[n][1]] for n in names]

    n_gate = 2 * M_HEADS + F_HEADS
    main = ("mqk", "mv", "mo", "fq", "fk", "fv", "mi", "mf", "ff")
    w_main = jnp.concatenate(take(w_in, main) + [jnp.zeros((d, GATE_COLS - n_gate), w_in.dtype)], axis=1)
    b_main = jnp.concatenate(take(b_in, main) + [jnp.zeros((GATE_COLS - n_gate,), b_in.dtype)])
    w_gate = jnp.concatenate(take(w_in, ("gm", "gf")), axis=1)
    b_gate = jnp.concatenate(take(b_in, ("gm", "gf")))
    return w_main.astype(BF16), b_main[None, :], w_gate.astype(BF16), b_gate[None, :]


def _layer(h, depth, w_in, b_in, m_conv_w, m_conv_b, m_norm_g, w_bm, w_bf, w_o, ln1_g, ln1_b,
           w_router, b_router, w_gu, b_gu, w_dn, b_dn, ln2_g, ln2_b):
    bsz, s, d = h.shape
    t = bsz * s
    dn_alpha = (2.0 * depth) ** 0.25
    x2 = h.reshape(t, d)
    assert d == MQK_W and w_gu.shape[0] == N_EXPERTS, "kernels are written for this layer geometry"
    assert s % INPROJ_ROWS == 0 and s % MLSTM_CHUNK == 0 and s % FOX_Q_BLOCK == 0
    assert bsz % MLSTM_SEQS == 0
    assert t % MERGE_ROWS == 0 and t % ROUTE_ROWS == 0 and t % COMBINE_ROWS == 0
    assert t % (V7X_SC_WORKERS * SC_SCATTER_ROWS) == 0
    assert (t * TOP_K) % (V7X_SC_WORKERS * SC_GATHER_ROWS * SC_GATHER_WAYS) == 0

    w_main, b_main, w_gate, b_gate = _pack_in_proj(w_in, b_in)
    mq, mk, mv, mo, fq, fk, fv, gates = _inproj(x2, w_main, b_main, m_conv_w, m_conv_b[None, :], s)
    ccol, crel_k, crel_q = _fox_cumsum(gates, bsz, s)
    hm = _mlstm(mq, mk, mv, mo, gates, m_norm_g, bsz, s)
    hf = _fox_attention(fq, fk, fv, ccol, crel_k, crel_q, bsz, s)

    n_exp = w_router.shape[1]
    wr = jnp.zeros((d, V7X_LANES), F32).at[:, :n_exp].set(w_router)
    wr_hi = wr.astype(BF16)
    wr_lo = (wr - wr_hi.astype(F32)).astype(BF16)
    br = jnp.zeros((1, V7X_LANES), F32).at[0, :n_exp].set(b_router)
    h1, h1p, gate, tope, tile_counts = _merge(
        dn_alpha, hm, hf, x2, w_gate, b_gate, w_bm.astype(BF16), w_bf.astype(BF16), w_o.astype(BF16),
        ln1_g[None, :], ln1_b[None, :], wr_hi, wr_lo, br)

    n_blocks = -(-(t * TOP_K) // MOE_BLOCK) + N_EXPERTS
    dest, table = _routing(tile_counts, tope, n_blocks)
    dest_km = dest[:TOP_K].reshape(TOP_K * t)
    block_e, block_valid = table[:, 0], table[:, 1]
    xs = _sc_dispatch(dest_km, h1p, n_blocks * MOE_BLOCK)
    y_rows = _experts(block_e, block_valid, xs, w_gu, b_gu[:, None, :], w_dn, b_dn[:, None, :])
    yg = _sc_gather(dest_km, y_rows).reshape(TOP_K, t, d // 2)
    out = _combine(dn_alpha, h1, gate, ln2_g[None, :], ln2_b[None, :], yg)
    return out.reshape(bsz, s, d)


def kernel(x, w_in, b_in, m_conv_w, m_conv_b, m_norm_g, w_bm, w_bf, w_o, ln1_g, ln1_b,
           w_router, b_router, w_gu, b_gu, w_dn, b_dn, ln2_g, ln2_b):
    depth = w_in.shape[0]
    h = x
    for l in range(depth):
        h = _layer(h, depth, w_in[l], b_in[l], m_conv_w[l], m_conv_b[l], m_norm_g[l], w_bm[l], w_bf[l],
                   w_o[l], ln1_g[l], ln1_b[l], w_router[l], b_router[l], w_gu[l], b_gu[l], w_dn[l],
                   b_dn[l], ln2_g[l], ln2_b[l])
    return h
```

```python
import functools
import math

import jax
import jax.numpy as jnp
from jax import lax
from jax.experimental import pallas as pl
from jax.experimental.pallas import tpu as pltpu
from jax.experimental.pallas import tpu_sc as plsc

F32 = jnp.float32
BF16 = jnp.bfloat16
I32 = jnp.int32
U32 = jnp.uint32

M_HEADS = 4
M_DQK = 128
M_DV = 128
CONV_W = 4
F_HEADS = 8
F_DH = 64
N_EXPERTS = 32
TOP_K = 4
SWIGLU_ALPHA = 1.702
SWIGLU_LIMIT = 7.0
LN_EPS = 1e-5

M_W = M_HEADS * M_DV
F_W = F_HEADS * F_DH
MQK_W = 2 * M_HEADS * M_DQK

V7X_LANES = 128
V7X_SUBLANES = 8
V7X_VMEM_BYTES = 64 * 1024 * 1024
VMEM_LIMIT_BYTES = (V7X_VMEM_BYTES * 3) // 4
EXPERT_VMEM_LIMIT_BYTES = (V7X_VMEM_BYTES * 7) // 8
V7X_SC_CORES = 2
V7X_SC_SUBCORES = 16
V7X_SC_WORKERS = V7X_SC_CORES * V7X_SC_SUBCORES

INPROJ_ROWS = 512
MLSTM_CHUNK = 256
MLSTM_SEQS = 4
FOX_Q_BLOCK = 512
FOX_K_BLOCK = 256
MERGE_ROWS = 512
MERGE_SUB_ROWS = 256
ROUTE_ROWS = 1024
ROUTE_SUB_ROWS = 256
MOE_BLOCK = 512
SC_SCATTER_ROWS = 128
SC_GATHER_ROWS = 64
SC_GATHER_WAYS = 2
COMBINE_ROWS = 512

GATE_COLS = V7X_LANES
MI_LANE = 0
MF_LANE = M_HEADS
FF_LANE = 2 * M_HEADS


def _params(semantics):
    return pltpu.CompilerParams(dimension_semantics=semantics, vmem_limit_bytes=VMEM_LIMIT_BYTES)


def _log_sigmoid(x):
    return jnp.minimum(x, 0.0) - jnp.log1p(jnp.exp(-jnp.abs(x)))


def _sigmoid(x):
    return 1.0 / (1.0 + jnp.exp(-x))


def _dot(a, b):
    return jnp.dot(a, b, preferred_element_type=F32)


def _dot_nt(a, b):
    return lax.dot_general(a, b, (((1,), (1,)), ((), ())), preferred_element_type=F32)


def _dot_tn(a, b):
    return lax.dot_general(a, b, (((0,), (0,)), ((), ())), preferred_element_type=F32)


def _split3(x):
    hi = x.astype(BF16)
    r1 = x - hi.astype(F32)
    mid = r1.astype(BF16)
    lo = (r1 - mid.astype(F32)).astype(BF16)
    return hi, mid, lo


def _dot_mask_f32(mask_bf16, x):
    hi, mid, lo = _split3(x)
    return (_dot(mask_bf16, lo) + _dot(mask_bf16, mid)) + _dot(mask_bf16, hi)


def _pack_bf16_pairs(x):
    half = x.shape[1] // 2
    bits = lax.bitcast_convert_type(x.astype(BF16).astype(F32), U32)
    return (bits[:, :half] >> 16) | bits[:, half:]


def _unpack_bf16_pairs(words):
    lo = lax.bitcast_convert_type(words << 16, F32)
    hi = lax.bitcast_convert_type(words & jnp.uint32(0xFFFF0000), F32)
    return jnp.concatenate([lo, hi], axis=1)


def _tril_mask(n, strict=False):
    r = lax.broadcasted_iota(I32, (n, n), 0)
    c = lax.broadcasted_iota(I32, (n, n), 1)
    return (r > c) if strict else (r >= c)


_OFF_MQK = 0
_OFF_MV = _OFF_MQK + MQK_W
_OFF_MO = _OFF_MV + M_W
_OFF_FQ = _OFF_MO + M_W
_OFF_FK = _OFF_FQ + F_W
_OFF_FV = _OFF_FK + F_W
_OFF_GATES = _OFF_FV + F_W
_PACKED_COLS = _OFF_GATES + GATE_COLS


def _inproj_kernel(tiles_per_seq, x_ref, w_ref, b_ref, cw_ref, cb_ref,
                   mq_ref, mk_ref, mv_ref, mo_ref, fq_ref, fk_ref, fv_ref, gates_ref, ext_ref):
    tm = x_ref.shape[0]
    pad = V7X_SUBLANES
    xb = x_ref[...].astype(BF16)

    def seg(lo, width):
        return _dot(xb, w_ref[:, lo:lo + width]) + b_ref[:, lo:lo + width]

    @pl.when(pl.program_id(0) % tiles_per_seq == 0)
    def _():
        ext_ref[0:pad, :] = jnp.zeros((pad, MQK_W), F32)

    ext_ref[pad:pad + tm, :] = seg(_OFF_MQK, MQK_W)
    y = cb_ref[...] + cw_ref[CONV_W - 1:CONV_W, :] * ext_ref[pad:pad + tm, :]
    for k in range(CONV_W - 1):
        shift = CONV_W - 1 - k
        y = y + cw_ref[k:k + 1, :] * ext_ref[pad - shift:pad - shift + tm, :]
    ext_ref[0:pad, :] = ext_ref[tm:tm + pad, :]
    act = y * _sigmoid(y)
    mq_ref[...] = act[:, :MQK_W // 2].astype(BF16)
    mk_ref[...] = act[:, MQK_W // 2:] * (M_DQK ** -0.5)
    mv_ref[...] = seg(_OFF_MV, M_W).astype(BF16)
    mo_ref[...] = seg(_OFF_MO, M_W)
    fq_ref[...] = (seg(_OFF_FQ, F_W) * (F_DH ** -0.5)).astype(BF16)
    fk_ref[...] = seg(_OFF_FK, F_W).astype(BF16)
    fv_ref[...] = seg(_OFF_FV, F_W).astype(BF16)
    gates_ref[...] = seg(_OFF_GATES, GATE_COLS)


def _inproj(x2, w_packed, b_packed, conv_w, conv_b, s):
    t, d = x2.shape
    tm = INPROJ_ROWS
    row = lambda i: (i, 0)
    const = lambda i: (0, 0)
    out_shapes = (
        jax.ShapeDtypeStruct((t, MQK_W // 2), BF16),
        jax.ShapeDtypeStruct((t, MQK_W // 2), F32),
        jax.ShapeDtypeStruct((t, M_W), BF16),
        jax.ShapeDtypeStruct((t, M_W), F32),
        jax.ShapeDtypeStruct((t, F_W), BF16),
        jax.ShapeDtypeStruct((t, F_W), BF16),
        jax.ShapeDtypeStruct((t, F_W), BF16),
        jax.ShapeDtypeStruct((t, GATE_COLS), F32),
    )
    return pl.pallas_call(
        functools.partial(_inproj_kernel, s // tm),
        grid=(t // tm,),
        in_specs=[
            pl.BlockSpec((tm, d), row),
            pl.BlockSpec((d, _PACKED_COLS), const),
            pl.BlockSpec((1, _PACKED_COLS), const),
            pl.BlockSpec((CONV_W, MQK_W), const),
            pl.BlockSpec((1, MQK_W), const),
        ],
        out_specs=tuple(pl.BlockSpec((tm, o.shape[1]), row) for o in out_shapes),
        out_shape=out_shapes,
        scratch_shapes=[pltpu.VMEM((tm + V7X_SUBLANES, MQK_W), F32)],
        compiler_params=_params(("arbitrary",)),
        name="inproj",
    )(x2, w_packed, b_packed, conv_w, conv_b)


def _fox_cumsum_kernel(g_ref, ccol_ref, crel_k_ref, crel_q_ref):
    s = g_ref.shape[0]
    cb = FOX_K_BLOCK
    per_q = FOX_Q_BLOCK // cb
    tri = _tril_mask(cb).astype(BF16)
    carry = jnp.zeros((1, GATE_COLS), F32)
    for j in range(s // cb):
        rows = slice(j * cb, (j + 1) * cb)
        if j % per_q == 0:
            q_carry = carry
        within = _dot_mask_f32(tri, _log_sigmoid(g_ref[rows, :]))
        crel_k_ref[rows, :] = within
        crel_q_ref[rows, :] = within + (carry - q_carry)
        ccol_ref[rows, :] = within + carry
        carry = carry + within[cb - 1:cb, :]


def _fox_cumsum(gates, bsz, s):
    t = gates.shape[0]
    spec = pl.BlockSpec((s, GATE_COLS), lambda b: (b, 0))
    shape = jax.ShapeDtypeStruct((t, GATE_COLS), F32)
    return pl.pallas_call(
        _fox_cumsum_kernel,
        grid=(bsz,),
        in_specs=[spec],
        out_specs=(spec, spec, spec),
        out_shape=(shape, shape, shape),
        compiler_params=_params(("parallel",)),
        name="fox_cumsum",
    )(gates)


def _mlstm_kernel(mq_ref, mk_ref, mv_ref, mo_ref, gates_ref, ng_ref, hm_ref, state_ref, m_ref):
    @pl.when(pl.program_id(1) == 0)
    def _():
        state_ref[...] = jnp.zeros(state_ref.shape, F32)
        m_ref[...] = jnp.zeros(m_ref.shape, F32)

    seqs = range(mq_ref.shape[0])
    states = [[state_ref[bb, h] for h in range(M_HEADS)] for bb in seqs]
    maxes = [[m_ref[bb, h][0:1, 0:1] for h in range(M_HEADS)] for bb in seqs]
    results = [_mlstm_chunk(mq_ref.at[bb], mk_ref.at[bb], mv_ref.at[bb], mo_ref.at[bb], gates_ref.at[bb],
                            ng_ref, states[bb], maxes[bb]) for bb in seqs]
    for bb in seqs:
        for h, (out_h, state_h, m_h) in enumerate(results[bb]):
            hm_ref[bb, :, h * M_DV:(h + 1) * M_DV] = out_h
            state_ref[bb, h] = state_h
            m_ref[bb, h] = jnp.broadcast_to(m_h, m_ref.shape[2:])


def _mlstm_chunk(mq_ref, mk_ref, mv_ref, mo_ref, gates_ref, ng_ref, states, maxes):
    L = MLSTM_CHUNK
    reps = L // V7X_LANES
    results = []
    gates = gates_ref[...]
    bfull = _dot_mask_f32(_tril_mask(L).astype(BF16), _log_sigmoid(gates))
    b_rows = bfull.T
    z_all = gates - pltpu.roll(bfull, shift=V7X_LANES - (MF_LANE - MI_LANE), axis=1)
    visible = (lax.broadcasted_iota(I32, (L, L), 0) <= lax.broadcasted_iota(I32, (L, L), 1))
    ones_rows = (lax.broadcasted_iota(I32, (M_DV, L), 0) == 0).astype(BF16)

    for h in range(M_HEADS):
        b_row = b_rows[MF_LANE + h:MF_LANE + h + 1, :]
        g_tot = b_row[:, L - 1:L]
        m_prev = maxes[h]
        z = jnp.broadcast_to(z_all[:, MI_LANE + h:MI_LANE + h + 1], (L, V7X_LANES))

        q_h = mq_ref[:, h * M_DQK:(h + 1) * M_DQK]
        k_f = mk_ref[:, h * M_DQK:(h + 1) * M_DQK]
        k_h = k_f.astype(BF16)
        v_t = mv_ref[:, h * M_DV:(h + 1) * M_DV].astype(F32).T.astype(BF16)
        cn_t = states[h]

        dlog = jnp.where(visible, b_row + jnp.tile(z, (1, reps)), -jnp.inf)
        inter_log = b_row + m_prev
        m_t = jnp.maximum(inter_log, jnp.max(dlog, axis=0, keepdims=True))
        w_inter = jnp.exp(inter_log - m_t)
        qkw = _dot_nt(k_h, q_h) * jnp.exp(dlog - m_t)
        qc = _dot_nt(cn_t.astype(BF16), q_h)
        num = w_inter * qc[:M_DV, :] + _dot(v_t, qkw.astype(BF16))
        den = w_inter * qc[M_DV:M_DV + 1, :] + jnp.sum(qkw, axis=0, keepdims=True)
        hh = num / jnp.maximum(jnp.abs(den), jnp.exp(-m_t))

        mu = jnp.mean(hh, axis=0, keepdims=True)
        dv = hh - mu
        var = jnp.mean(dv * dv, axis=0, keepdims=True)
        hn = (dv * lax.rsqrt(var + LN_EPS)) * jnp.tile(ng_ref[h * M_DV:(h + 1) * M_DV, :], (1, reps))
        out_h = (_sigmoid(mo_ref[:, h * M_DV:(h + 1) * M_DV]) * hn.T).astype(BF16)

        a = g_tot + z
        m_new = jnp.maximum(g_tot + m_prev, jnp.max(a, axis=0, keepdims=True)[:, 0:1])
        decay = jnp.exp(g_tot + m_prev - m_new)
        kw = (k_f * jnp.exp(a - m_new)).astype(BF16)
        v_aug = jnp.concatenate([v_t, ones_rows], axis=0)
        results.append((out_h, decay * cn_t + _dot(v_aug, kw), m_new))
    return results


def _mlstm(mq, mk, mv, mo, gates, norm_g, bsz, s):
    t = mq.shape[0]
    L = MLSTM_CHUNK
    nb = MLSTM_SEQS
    seq = lambda a: a.reshape(bsz, s, a.shape[1])
    blk = lambda width: pl.BlockSpec((nb, L, width), lambda g, n: (g, n, 0))
    const = lambda g, n: (0, 0)
    hm = pl.pallas_call(
        _mlstm_kernel,
        grid=(bsz // nb, s // L),
        in_specs=[
            blk(MQK_W // 2), blk(MQK_W // 2), blk(M_W), blk(M_W), blk(GATE_COLS),
            pl.BlockSpec((M_W, V7X_LANES), const),
        ],
        out_specs=blk(M_W),
        out_shape=jax.ShapeDtypeStruct((bsz, s, M_W), BF16),
        scratch_shapes=[
            pltpu.VMEM((nb, M_HEADS, 2 * M_DV, M_DQK), F32),
            pltpu.VMEM((nb, M_HEADS, V7X_SUBLANES, V7X_LANES), F32),
        ],
        compiler_params=_params(("parallel", "arbitrary")),
        name="mlstm",
    )(seq(mq), seq(mk), seq(mv), seq(mo), seq(gates),
      jnp.broadcast_to(norm_g[:, None], (M_W, V7X_LANES)))
    return hm.reshape(t, M_W)


_FOX_FEATURES = 3


def _fox_operands(x, c_tile, c_lane, key_side):
    rows = x.shape[0]
    nf = _FOX_FEATURES
    first = 0 if key_side else nf
    src = lax.broadcasted_iota(I32, (V7X_LANES, V7X_LANES), 0) - c_lane
    dst = lax.broadcasted_iota(I32, (V7X_LANES, V7X_LANES), 1)
    feats = jnp.zeros((rows, V7X_LANES), F32)
    for n, part in enumerate(_split3(c_tile)):
        place = jnp.logical_or(jnp.logical_and(src == 0, dst == F_DH + first + n),
                               jnp.logical_and(src == 1, dst == first + n))
        feats = feats + _dot(part, place.astype(BF16))
    lane = lax.broadcasted_iota(I32, (rows, V7X_LANES), 1)
    within = lane % F_DH
    const_lanes = jnp.logical_and(within >= nf - first, within < 2 * nf - first)
    feats = jnp.where(const_lanes, 1.0 if key_side else -1.0, feats).astype(BF16)
    low = lane < F_DH
    return jnp.where(low, x, feats), jnp.where(low, feats, x)


def _fox_kernel(q_ref, k_ref, v_ref, ccol_ref, crel_k_ref, crel_q_ref, o_ref, qaug_ref, kaug_ref, vt_ref,
                st_a0, st_a1, st_b0, st_b1, pe_a0, pe_a1, pe_b0, pe_b1, acc0, acc1):
    blk = FOX_K_BLOCK
    tq = FOX_Q_BLOCK
    assert tq == 2 * blk
    strips = tq // V7X_LANES
    st_a_refs, st_b_refs = (st_a0, st_a1), (st_b0, st_b1)
    pe_a_refs, pe_b_refs = (pe_a0, pe_a1), (pe_b0, pe_b1)
    acc_refs = (acc0, acc1)
    p = pl.program_id(1)
    i = pl.program_id(2)

    @pl.when(i == 0)
    def _():
        c_lane = FF_LANE + 2 * p
        qaug_ref[0], qaug_ref[1] = _fox_operands(q_ref[...], crel_q_ref[...], c_lane, False)
        kaug_ref[0], kaug_ref[1] = _fox_operands(k_ref[...], crel_k_ref[...], c_lane, True)
        v_t = v_ref[...].astype(F32).T
        for j in range(vt_ref.shape[0]):
            vt_ref[j] = v_t[:, j * blk:(j + 1) * blk].astype(BF16)

    q_start = pl.multiple_of(i * tq, tq)
    q_heads = [qaug_ref[hh, pl.ds(q_start, tq), :] for hh in range(2)]
    key_row = lax.broadcasted_iota(I32, (blk, V7X_LANES), 0)
    query_col = lax.broadcasted_iota(I32, (blk, V7X_LANES), 1)
    last_chunk = 2 * i + 1
    head_lane = lax.broadcasted_iota(I32, (1, GATE_COLS), 1) - (FF_LANE + 2 * p)

    def c_before(position, hh):
        row = ccol_ref[pl.ds(jnp.maximum(position - 1, 0), 1), :]
        keep = jnp.logical_and(head_lane == hh, position > 0)
        return jnp.sum(jnp.where(keep, row, 0.0), axis=-1, keepdims=True)

    c_query0 = [c_before(q_start, hh) for hh in range(2)]

    def scores(j, hh):
        start = pl.multiple_of(j * blk, blk)
        return _dot_nt(kaug_ref[hh, pl.ds(start, blk), :], q_heads[hh])

    def softmax_update(st_ref, pe_ref, m_old, l_old, base, key_minus_query=None):
        alphas, ms, ls = [], [], []
        for c in range(strips):
            cols = slice(c * V7X_LANES, (c + 1) * V7X_LANES)
            gap = None if key_minus_query is None else key_minus_query - c * V7X_LANES
            if gap is not None and gap - (V7X_LANES - 1) > 0:
                pe_ref[:, cols] = jnp.zeros((blk, V7X_LANES), BF16)
                alphas.append(jnp.ones((1, V7X_LANES), F32))
                ms.append(m_old[:, cols])
                ls.append(l_old[:, cols])
                continue
            st = st_ref[:, cols]
            if gap is not None and gap + (blk - 1) > 0:
                st = jnp.where(key_row + gap <= query_col, st, -jnp.inf)
            m_new = jnp.maximum(m_old[:, cols], jnp.max(st, axis=0, keepdims=True) + base)
            alpha = jnp.exp(m_old[:, cols] - m_new)
            pe = jnp.exp(st - (m_new - base))
            pe_ref[:, cols] = pe.astype(BF16)
            alphas.append(alpha)
            ms.append(m_new)
            ls.append(alpha * l_old[:, cols] + jnp.sum(pe, axis=0, keepdims=True))
        cat = lambda parts: jnp.concatenate(parts, axis=1)
        return cat(alphas), cat(ms), cat(ls)

    def pair(mi, carry, diagonal=False):
        a = 2 * mi
        b = a + 1
        v_prev = vt_ref[jnp.maximum(a - 1, 0)]
        v_a = vt_ref[a]
        partial = []
        for hh in range(2):
            alpha_prev = carry[hh][0]
            partial.append(alpha_prev * acc_refs[hh][...] + _dot(v_prev, pe_b_refs[hh][...]))
            st_b_refs[hh][...] = scores(b, hh)
        stats = []
        for hh in range(2):
            _, m_old, l_old = carry[hh]
            stats.append(softmax_update(st_a_refs[hh], pe_a_refs[hh], m_old, l_old,
                                        c_query0[hh] - c_before(a * blk, hh), 0 if diagonal else None))
        for hh in range(2):
            alpha_a = stats[hh][0]
            acc_refs[hh][...] = alpha_a * partial[hh] + _dot(v_a, pe_a_refs[hh][...])
            if not diagonal:
                st_a_refs[hh][...] = scores(a + 2, hh)
        return tuple(softmax_update(st_b_refs[hh], pe_b_refs[hh], stats[hh][1], stats[hh][2],
                                    c_query0[hh] - c_before(b * blk, hh), blk if diagonal else None)
                     for hh in range(2))

    for hh in range(2):
        st_a_refs[hh][...] = scores(0, hh)
        pe_b_refs[hh][...] = jnp.zeros((blk, tq), BF16)
        acc_refs[hh][...] = jnp.zeros((V7X_LANES, tq), F32)
    init = tuple((jnp.ones((1, tq), F32), jnp.full((1, tq), -jnp.inf, F32), jnp.zeros((1, tq), F32))
                 for _ in range(2))
    final = pair(i, lax.fori_loop(0, i, pair, init), diagonal=True)
    v_last = vt_ref[last_chunk]
    outs = []
    for hh in range(2):
        alpha, _, l_fin = final[hh]
        outs.append((alpha * acc_refs[hh][...] + _dot(v_last, pe_b_refs[hh][...])) / l_fin)
    row = lax.broadcasted_iota(I32, (V7X_LANES, tq), 0)
    o_t = jnp.where(row < F_DH, outs[0], outs[1])
    o_ref[...] = o_t.T.astype(BF16)


def _fox_attention(fq, fk, fv, ccol, crel_k, crel_q, bsz, s):
    t = fq.shape[0]
    blk = FOX_K_BLOCK
    tq = FOX_Q_BLOCK
    nq = s // tq
    pairs = F_HEADS // 2
    qmap = lambda b, p, i: (b * nq + i, p)
    kvmap = lambda b, p, i: (b, p)
    return pl.pallas_call(
        _fox_kernel,
        grid=(bsz, pairs, nq),
        in_specs=[
            pl.BlockSpec((s, V7X_LANES), kvmap),
            pl.BlockSpec((s, V7X_LANES), kvmap),
            pl.BlockSpec((s, V7X_LANES), kvmap),
            pl.BlockSpec((s, GATE_COLS), lambda b, p, i: (b, 0)),
            pl.BlockSpec((s, GATE_COLS), lambda b, p, i: (b, 0)),
            pl.BlockSpec((s, GATE_COLS), lambda b, p, i: (b, 0)),
        ],
        out_specs=pl.BlockSpec((tq, V7X_LANES), qmap),
        out_shape=jax.ShapeDtypeStruct((t, F_W), BF16),
        scratch_shapes=[
            pltpu.VMEM((2, s, V7X_LANES), BF16),
            pltpu.VMEM((2, s, V7X_LANES), BF16),
            pltpu.VMEM((s // blk, V7X_LANES, blk), BF16),
        ] + [pltpu.VMEM((blk, tq), F32)] * 4 + [pltpu.VMEM((blk, tq), BF16)] * 4
          + [pltpu.VMEM((V7X_LANES, tq), F32)] * 2,
        compiler_params=_params(("parallel", "parallel", "arbitrary")),
        name="fox_attention",
    )(fq, fk, fv, ccol, crel_k, crel_q)


def _layer_norm_rows(r, g, b):
    mu = jnp.mean(r, axis=-1, keepdims=True)
    d = r - mu
    var = jnp.mean(d * d, axis=-1, keepdims=True)
    return (d * lax.rsqrt(var + LN_EPS)) * g + b


def _merge_kernel(dn_alpha, hm_ref, hf_ref, x_ref, wg_ref, bg_ref, wbm_ref, wbf_ref, wo_ref,
                  g_ref, b_ref, wrh_ref, wrl_ref, br_ref, h1_ref, h1p_ref, gate_ref, tope_ref, cnt_ref,
                  resid_ref):
    i = pl.program_id(0)
    slot = i % 2

    @pl.when(i == 0)
    def _():
        resid_ref[1] = jnp.zeros(resid_ref.shape[1:], F32)

    subs = [slice(n * MERGE_SUB_ROWS, (n + 1) * MERGE_SUB_ROWS)
            for n in range(x_ref.shape[0] // MERGE_SUB_ROWS)]
    counts = jnp.zeros((1, V7X_LANES), F32)
    for rows in subs:
        counts = counts + _merge_tail(resid_ref[1 - slot, rows, :], rows, g_ref, b_ref, wrh_ref, wrl_ref,
                                      br_ref, h1_ref, h1p_ref, gate_ref, tope_ref)
    sub = lax.broadcasted_iota(I32, cnt_ref.shape, 0)
    cnt_ref[...] = jnp.where(sub == 0, counts, 0.0)

    for rows in subs:
        x = x_ref[rows, :]
        d = x.shape[1]
        gmf = _dot(x.astype(BF16), wg_ref[...]) + bg_ref[...]
        ym = _dot(hm_ref[rows, :], wbm_ref[...])
        yf = _dot(hf_ref[rows, :], wbf_ref[...])
        y = _sigmoid(gmf[:, :d]) * ym + _sigmoid(gmf[:, d:]) * yf
        resid_ref[slot, rows, :] = dn_alpha * x + _dot(y.astype(BF16), wo_ref[...])


def _merge_tail(resid, rows, g_ref, b_ref, wrh_ref, wrl_ref, br_ref, h1_ref, h1p_ref, gate_ref, tope_ref):
    h1 = _layer_norm_rows(resid, g_ref[...], b_ref[...])
    h1_ref[rows, :] = h1

    h1p_ref[rows, :] = _pack_bf16_pairs(h1)
    hb = h1.astype(BF16)

    lo = (h1 - hb.astype(F32)).astype(BF16)
    logits = (_dot(lo, wrh_ref[...]) + _dot(hb, wrl_ref[...])) + _dot(hb, wrh_ref[...]) + br_ref[...]
    tm = logits.shape[0]
    lane = lax.broadcasted_iota(I32, (tm, V7X_LANES), 1)
    vals = jnp.where(lane < N_EXPERTS, logits, -jnp.inf)
    top_v, top_i = [], []
    for _ in range(TOP_K):
        mx = jnp.max(vals, axis=-1, keepdims=True)
        idx = jnp.min(jnp.where(vals == mx, lane, V7X_LANES), axis=-1, keepdims=True)
        top_v.append(mx)
        top_i.append(idx)
        vals = jnp.where(lane == idx, -jnp.inf, vals)
    ex = [jnp.exp(v - top_v[0]) for v in top_v]
    tot = ex[0]
    for e in ex[1:]:
        tot = tot + e
    gate = jnp.zeros((tm, V7X_LANES), F32)
    tope = jnp.zeros((tm, V7X_LANES), I32)
    member = jnp.zeros((tm, V7X_LANES), F32)
    for k in range(TOP_K):
        gate = jnp.where(lane == k, ex[k] / tot, gate)
        tope = jnp.where(lane == k, top_i[k], tope)
        member = member + (lane == top_i[k]).astype(F32)
    gate_ref[rows, :] = gate
    tope_ref[rows, :] = tope
    return jnp.sum(member, axis=0, keepdims=True)


def _merge(dn_alpha, hm, hf, x2, wg, bg, wbm, wbf, wo, ln_g, ln_b, wr_hi, wr_lo, br):
    t, d = x2.shape
    tm = MERGE_ROWS
    nt = t // tm
    row = lambda i: (jnp.minimum(i, nt - 1), 0)
    out_row = lambda i: (jnp.maximum(i - 1, 0), 0)
    const = lambda i: (0, 0)
    full = lambda a: pl.BlockSpec(a.shape, const)
    return pl.pallas_call(
        functools.partial(_merge_kernel, dn_alpha),
        grid=(nt + 1,),
        in_specs=[
            pl.BlockSpec((tm, M_W), row),
            pl.BlockSpec((tm, F_W), row),
            pl.BlockSpec((tm, d), row),
            full(wg), full(bg), full(wbm), full(wbf), full(wo), full(ln_g), full(ln_b),
            full(wr_hi), full(wr_lo), full(br),
        ],
        out_specs=(
            pl.BlockSpec((tm, d), out_row),
            pl.BlockSpec((tm, d // 2), out_row),
            pl.BlockSpec((tm, V7X_LANES), out_row),
            pl.BlockSpec((tm, V7X_LANES), out_row),
            pl.BlockSpec((V7X_SUBLANES, V7X_LANES), out_row),
        ),
        out_shape=(
            jax.ShapeDtypeStruct((t, d), F32),
            jax.ShapeDtypeStruct((t, d // 2), U32),
            jax.ShapeDtypeStruct((t, V7X_LANES), F32),
            jax.ShapeDtypeStruct((t, V7X_LANES), I32),
            jax.ShapeDtypeStruct((t // tm * V7X_SUBLANES, V7X_LANES), F32),
        ),
        scratch_shapes=[pltpu.VMEM((2, tm, d), F32)],
        compiler_params=_params(("arbitrary",)),
        name="merge_ln1_router",
    )(hm, hf, x2, wg, bg, wbm, wbf, wo, ln_g, ln_b, wr_hi, wr_lo, br)


def _lane_cumsum(x):
    lane = lax.broadcasted_iota(I32, x.shape, 1)
    d = 1
    while d < V7X_LANES:
        x = x + jnp.where(lane >= d, pltpu.roll(x, shift=d, axis=1), 0.0)
        d *= 2
    return x


def _routing_kernel(cnt_ref, tope_ref, dest_ref, table_ref, run_ref, start_ref):
    sb = ROUTE_SUB_ROWS

    @pl.when(pl.program_id(0) == 0)
    def _():
        total = jnp.sum(cnt_ref[...], axis=0, keepdims=True)
        counts = jnp.broadcast_to(total, (V7X_SUBLANES, V7X_LANES))
        padded = jnp.ceil(counts * (1.0 / MOE_BLOCK)) * MOE_BLOCK
        pad_end = _lane_cumsum(padded)
        pad_start = pad_end - padded
        start_ref[...] = pad_start
        run_ref[...] = jnp.zeros(run_ref.shape, F32)
        nb = table_ref.shape[0]
        blk = lax.broadcasted_iota(I32, (nb, V7X_LANES), 0).astype(F32) * MOE_BLOCK
        ln = lax.broadcasted_iota(I32, (nb, V7X_LANES), 1)
        done = jnp.logical_and(pad_end[0:1, :] <= blk, ln < N_EXPERTS)
        be = jnp.minimum(jnp.sum(done.astype(F32), axis=-1, keepdims=True), N_EXPERTS - 1.0)
        onehot = ln == be.astype(I32)
        cnt_e = jnp.sum(jnp.where(onehot, counts[0:1, :], 0.0), axis=-1, keepdims=True)
        start_e = jnp.sum(jnp.where(onehot, pad_start[0:1, :], 0.0), axis=-1, keepdims=True)
        valid = jnp.clip(cnt_e - (blk[:, 0:1] - start_e), 0.0, float(MOE_BLOCK))
        table_ref[...] = jnp.where(ln == 0, be.astype(I32),
                                   jnp.where(ln == 1, valid.astype(I32), 0))

    earlier = _tril_mask(sb, strict=True).astype(BF16)
    lane = lax.broadcasted_iota(I32, (sb, V7X_LANES), 1)
    for j in range(tope_ref.shape[0] // sb):
        tope = tope_ref[j * sb:(j + 1) * sb, :]
        hit = [lane == tope[:, k:k + 1] for k in range(TOP_K)]
        member = jnp.zeros((sb, V7X_LANES), F32)
        for k in range(TOP_K):
            member = member + hit[k].astype(F32)
        base = _dot(earlier, member.astype(BF16)) + (run_ref[0:1, :] + start_ref[0:1, :])
        dest = jnp.zeros((sb, V7X_LANES), F32)
        for k in range(TOP_K):
            dk = jnp.sum(jnp.where(hit[k], base, 0.0), axis=-1, keepdims=True)
            dest = jnp.where(lane == k, dk, dest)
        dest_ref[:, j * sb:(j + 1) * sb] = dest.T[0:V7X_SUBLANES, :].astype(I32)
        run_ref[...] = run_ref[...] + jnp.sum(member, axis=0, keepdims=True)


def _routing(tile_counts, tope, n_blocks):
    t = tope.shape[0]
    tr = ROUTE_ROWS
    return pl.pallas_call(
        _routing_kernel,
        grid=(t // tr,),
        in_specs=[pl.BlockSpec(tile_counts.shape, lambda i: (0, 0)),
                  pl.BlockSpec((tr, V7X_LANES), lambda i: (i, 0))],
        out_specs=(
            pl.BlockSpec((V7X_SUBLANES, tr), lambda i: (0, i)),
            pl.BlockSpec((n_blocks, V7X_LANES), lambda i: (0, 0)),
        ),
        out_shape=(
            jax.ShapeDtypeStruct((V7X_SUBLANES, t), I32),
            jax.ShapeDtypeStruct((n_blocks, V7X_LANES), I32),
        ),
        scratch_shapes=[
            pltpu.VMEM((V7X_SUBLANES, V7X_LANES), F32),
            pltpu.VMEM((V7X_SUBLANES, V7X_LANES), F32),
        ],
        compiler_params=_params(("arbitrary",)),
        name="routing",
    )(tile_counts, tope)


def _sc_worker_id():
    return lax.axis_index("s") * V7X_SC_CORES + lax.axis_index("c")


def _sc_mesh():
    return plsc.VectorSubcoreMesh(core_axis_name="c", subcore_axis_name="s",
                                  num_cores=V7X_SC_CORES, num_subcores=V7X_SC_SUBCORES)


def _sc_dispatch(dest_km, h1p, n_rows):
    t, w = h1p.shape
    per_worker = t // V7X_SC_WORKERS
    ch = SC_SCATTER_ROWS

    @functools.partial(
        pl.kernel, mesh=_sc_mesh(),
        out_type=jax.ShapeDtypeStruct((n_rows, w), h1p.dtype),
        scratch_types=[pltpu.VMEM((ch, w), h1p.dtype)]
        + [pltpu.VMEM((ch,), I32)] * TOP_K + [pltpu.SemaphoreType.DMA] * TOP_K,
        name="sc_dispatch",
    )
    def scatter_rows(dest_hbm, h1p_hbm, xs_hbm, rows_v, *idx_and_sems):
        idx_refs, sems = idx_and_sems[:TOP_K], idx_and_sems[TOP_K:]
        first = _sc_worker_id() * per_worker

        @pl.loop(0, per_worker // ch)
        def _(j):
            base = first + j * ch
            pltpu.sync_copy(h1p_hbm.at[pl.ds(base, ch)], rows_v)
            copies = []
            for k in range(TOP_K):
                pltpu.sync_copy(dest_hbm.at[pl.ds(k * t + base, ch)], idx_refs[k])
                copies.append(pltpu.async_copy(rows_v, xs_hbm.at[idx_refs[k]], sems[k]))
            for copy in copies:
                copy.wait()

    return scatter_rows(dest_km, h1p)


def _sc_gather(dest_km, y_rows):
    n = dest_km.shape[0]
    w = y_rows.shape[1]
    per_worker = n // V7X_SC_WORKERS
    ch = SC_GATHER_ROWS
    ways = SC_GATHER_WAYS

    @functools.partial(
        pl.kernel, mesh=_sc_mesh(),
        out_type=jax.ShapeDtypeStruct((n, w), y_rows.dtype),
        scratch_types=[pltpu.VMEM((ch,), I32)] * ways + [pltpu.VMEM((ch, w), y_rows.dtype)] * ways
        + [pltpu.SemaphoreType.DMA] * (2 * ways),
        name="sc_gather",
    )
    def gather_rows(dest_hbm, y_hbm, out_hbm, *scratch):
        idx_refs, row_refs = scratch[:ways], scratch[ways:2 * ways]
        gather_sems, store_sems = scratch[2 * ways:3 * ways], scratch[3 * ways:]
        first = _sc_worker_id() * per_worker

        @pl.loop(0, per_worker // (ch * ways))
        def _(j):
            bases = [first + (j * ways + u) * ch for u in range(ways)]
            gathers = []
            for u in range(ways):
                pltpu.sync_copy(dest_hbm.at[pl.ds(bases[u], ch)], idx_refs[u])
                gathers.append(pltpu.async_copy(y_hbm.at[idx_refs[u]], row_refs[u], gather_sems[u]))
            stores = []
            for u in range(ways):
                gathers[u].wait()
                stores.append(pltpu.async_copy(row_refs[u], out_hbm.at[pl.ds(bases[u], ch)], store_sems[u]))
            for store in stores:
                store.wait()

    return gather_rows(dest_km, y_rows)


def _expert_kernel(be_ref, nv_ref, xs_ref, wgu_f32_ref, bgu_ref, wdn_f32_ref, bdn_ref, y_ref,
                   wgu_slots, wdn_slots, slot_ref):
    i = pl.program_id(0)
    n_blocks = pl.num_programs(0) - 1
    nv = jnp.where(i > 0, nv_ref[jnp.maximum(i - 1, 0)], 0)
    half = MOE_BLOCK // 2

    @pl.when(i == 0)
    def _():
        slot_ref[0] = 0

    slot = slot_ref[0]
    wgu_ref = wgu_slots.at[slot]
    wdn_ref = wdn_slots.at[slot]

    def ffn(rows):
        x = _unpack_bf16_pairs(xs_ref[rows, :])
        rowid = rows.start + lax.broadcasted_iota(I32, x.shape, 0)
        x = jnp.where(rowid < nv, x, 0.0).astype(BF16)
        gu = _dot(x, wgu_ref[...]) + bgu_ref[...]
        f = gu.shape[1] // 2
        glu = jnp.minimum(gu[:, :f], SWIGLU_LIMIT)
        lin = jnp.clip(gu[:, f:], -SWIGLU_LIMIT, SWIGLU_LIMIT)
        act = glu * _sigmoid(SWIGLU_ALPHA * glu) * (lin + 1.0)
        y_ref[rows, :] = _pack_bf16_pairs(_dot(act.astype(BF16), wdn_ref[...]) + bdn_ref[...])

    @pl.when(nv == 0)
    def _():
        y_ref[...] = jnp.zeros(y_ref.shape, U32)

    @pl.when(jnp.logical_and(nv > 0, nv <= half))
    def _():
        ffn(slice(0, half))
        y_ref[half:, :] = jnp.zeros((MOE_BLOCK - half, y_ref.shape[1]), U32)

    @pl.when(nv > half)
    def _():
        ffn(slice(0, half))
        ffn(slice(half, MOE_BLOCK))

    new_expert = jnp.logical_or(i == 0, be_ref[jnp.minimum(i, n_blocks - 1)] != be_ref[jnp.maximum(i - 1, 0)])

    @pl.when(jnp.logical_and(i < n_blocks, new_expert))
    def _():
        wgu_slots[1 - slot] = wgu_f32_ref[...].astype(BF16)
        wdn_slots[1 - slot] = wdn_f32_ref[...].astype(BF16)
        slot_ref[0] = 1 - slot


def _experts(block_e, block_valid, xs, wgu, bgu, wdn, bdn):
    n_rows, w = xs.shape
    e, d, f2 = wgu.shape
    n_blocks = n_rows // MOE_BLOCK
    ahead = lambda i, be: be[jnp.minimum(i, n_blocks - 1)]
    behind = lambda i: jnp.maximum(i - 1, 0)
    grid_spec = pltpu.PrefetchScalarGridSpec(
        num_scalar_prefetch=2,
        grid=(n_blocks + 1,),
        in_specs=[
            pl.BlockSpec((MOE_BLOCK, w), lambda i, be, nv: (behind(i), 0)),
            pl.BlockSpec((None, d, f2), lambda i, be, nv: (ahead(i, be), 0, 0)),
            pl.BlockSpec((None, 1, f2), lambda i, be, nv: (be[behind(i)], 0, 0)),
            pl.BlockSpec((None, f2 // 2, d), lambda i, be, nv: (ahead(i, be), 0, 0)),
            pl.BlockSpec((None, 1, d), lambda i, be, nv: (be[behind(i)], 0, 0)),
        ],
        out_specs=pl.BlockSpec((MOE_BLOCK, d // 2), lambda i, be, nv: (behind(i), 0)),
        scratch_shapes=[pltpu.VMEM((2, d, f2), BF16), pltpu.VMEM((2, f2 // 2, d), BF16),
                        pltpu.SMEM((1,), I32)],
    )
    return pl.pallas_call(
        _expert_kernel,
        grid_spec=grid_spec,
        out_shape=jax.ShapeDtypeStruct((n_rows, d // 2), U32),
        compiler_params=pltpu.CompilerParams(dimension_semantics=("arbitrary",),
                                             vmem_limit_bytes=EXPERT_VMEM_LIMIT_BYTES),
        name="experts",
    )(block_e, block_valid, xs, wgu, bgu, wdn, bdn)


def _combine_kernel(dn_alpha, h1_ref, gate_ref, g_ref, b_ref, yg_ref, o_ref):
    gate = gate_ref[...]
    ffn = gate[:, 0:1] * _unpack_bf16_pairs(yg_ref[0])
    for k in range(1, TOP_K):
        ffn = ffn + gate[:, k:k + 1] * _unpack_bf16_pairs(yg_ref[k])
    o_ref[...] = _layer_norm_rows(dn_alpha * h1_ref[...] + ffn, g_ref[...], b_ref[...])


def _combine(dn_alpha, h1, gate, ln_g, ln_b, yg):
    t, d = h1.shape
    tc = COMBINE_ROWS
    row = lambda i: (i, 0)
    const = lambda i: (0, 0)
    return pl.pallas_call(
        functools.partial(_combine_kernel, dn_alpha),
        grid=(t // tc,),
        in_specs=[
            pl.BlockSpec((tc, d), row),
            pl.BlockSpec((tc, V7X_LANES), row),
            pl.BlockSpec((1, d), const),
            pl.BlockSpec((1, d), const),
            pl.BlockSpec((TOP_K, tc, d // 2), lambda i: (0, i, 0)),
        ],
        out_specs=pl.BlockSpec((tc, d), row),
        out_shape=jax.ShapeDtypeStruct((t, d), F32),
        compiler_params=_params(("parallel",)),
        name="combine_ln2",
    )(h1, gate, ln_g, ln_b, yg)


def _in_proj_columns(d):
    o = 0
    cols = {}
    for name, width in (("mqk", MQK_W), ("mv", M_W), ("mo", M_W), ("mi", M_HEADS), ("mf", M_HEADS),
                        ("fq", F_W), ("fk", F_W), ("fv", F_W), ("ff", F_HEADS), ("gm", d), ("gf", d)):
        cols[name] = (o, o + width)
        o += width
    return cols


def _pack_w_kernel(w_ref, main_ref, gate_ref):
    d = gate_ref.shape[1] // 2
    cols = _in_proj_columns(d)
    out = 0
    for name in ("mqk", "mv", "mo", "fq", "fk", "fv"):
        lo, hi = cols[name]
        main_ref[:, out:out + hi - lo] = w_ref[:, lo:hi].astype(BF16)
        out += hi - lo
    lane = lax.broadcasted_iota(I32, (w_ref.shape[0], GATE_COLS), 1)
    mi_lo, ff_lo = cols["mi"][0], cols["ff"][0] - 2 * M_HEADS
    assert mi_lo % V7X_LANES == 0 and ff_lo % V7X_LANES == 0 and cols["mf"][0] == mi_lo + M_HEADS
    gates = jnp.where(lane < 2 * M_HEADS, w_ref[:, mi_lo:mi_lo + GATE_COLS],
                      jnp.where(lane < 2 * M_HEADS + F_HEADS, w_ref[:, ff_lo:ff_lo + GATE_COLS], 0.0))
    main_ref[:, out:out + GATE_COLS] = gates.astype(BF16)
    gate_ref[...] = w_ref[:, cols["gm"][0]:cols["gf"][1]].astype(BF16)


def _pack_in_proj(w_in_layers, layer, b_in):
    _, d, n_cols = w_in_layers.shape
    cols = _in_proj_columns(d)
    tr = V7X_LANES
    w_main, w_gate = pl.pallas_call(
        _pack_w_kernel,
        grid=(d // tr,),
        in_specs=[pl.BlockSpec((None, tr, n_cols), lambda i: (layer, i, 0))],
        out_specs=(pl.BlockSpec((tr, _PACKED_COLS), lambda i: (i, 0)),
                   pl.BlockSpec((tr, 2 * d), lambda i: (i, 0))),
        out_shape=(jax.ShapeDtypeStruct((d, _PACKED_COLS), BF16),
                   jax.ShapeDtypeStruct((d, 2 * d), BF16)),
        compiler_params=_params(("parallel",)),
        name="pack_in_proj",
    )(w_in_layers)

    def take(names):
        return [b_in[cols[n][0]:cols[n][1]] for n in names]

    n_gate = 2 * M_HEADS + F_HEADS
    b_main = jnp.concatenate(take(("mqk", "mv", "mo", "fq", "fk", "fv", "mi", "mf", "ff"))
                             + [jnp.zeros((GATE_COLS - n_gate,), b_in.dtype)])
    b_gate = jnp.concatenate(take(("gm", "gf")))
    return w_main, b_main[None, :], w_gate, b_gate[None, :]


def _layer(h, depth, layer, w_in_layers, b_in, m_conv_w, m_conv_b, m_norm_g, w_bm, w_bf, w_o, ln1_g, ln1_b,
           w_router, b_router, w_gu, b_gu, w_dn, b_dn, ln2_g, ln2_b):
    bsz, s, d = h.shape
    t = bsz * s
    dn_alpha = (2.0 * depth) ** 0.25
    x2 = h.reshape(t, d)
    assert d == MQK_W and w_gu.shape[0] == N_EXPERTS, "kernels are written for this layer geometry"
    assert s % INPROJ_ROWS == 0 and s % MLSTM_CHUNK == 0 and s % FOX_Q_BLOCK == 0
    assert bsz % MLSTM_SEQS == 0
    assert t % MERGE_ROWS == 0 and t % ROUTE_ROWS == 0 and t % COMBINE_ROWS == 0
    assert t % (V7X_SC_WORKERS * SC_SCATTER_ROWS) == 0
    assert (t * TOP_K) % (V7X_SC_WORKERS * SC_GATHER_ROWS * SC_GATHER_WAYS) == 0

    w_main, b_main, w_gate, b_gate = _pack_in_proj(w_in_layers, layer, b_in)
    mq, mk, mv, mo, fq, fk, fv, gates = _inproj(x2, w_main, b_main, m_conv_w, m_conv_b[None, :], s)
    ccol, crel_k, crel_q = _fox_cumsum(gates, bsz, s)
    hm = _mlstm(mq, mk, mv, mo, gates, m_norm_g, bsz, s)
    hf = _fox_attention(fq, fk, fv, ccol, crel_k, crel_q, bsz, s)

    n_exp = w_router.shape[1]
    wr = jnp.zeros((d, V7X_LANES), F32).at[:, :n_exp].set(w_router)
    wr_hi = wr.astype(BF16)
    wr_lo = (wr - wr_hi.astype(F32)).astype(BF16)
    br = jnp.zeros((1, V7X_LANES), F32).at[0, :n_exp].set(b_router)
    h1, h1p, gate, tope, tile_counts = _merge(
        dn_alpha, hm, hf, x2, w_gate, b_gate, w_bm.astype(BF16), w_bf.astype(BF16), w_o.astype(BF16),
        ln1_g[None, :], ln1_b[None, :], wr_hi, wr_lo, br)

    n_blocks = -(-(t * TOP_K) // MOE_BLOCK) + N_EXPERTS
    dest, table = _routing(tile_counts, tope, n_blocks)
    dest_km = dest[:TOP_K].reshape(TOP_K * t)
    block_e, block_valid = table[:, 0], table[:, 1]
    xs = _sc_dispatch(dest_km, h1p, n_blocks * MOE_BLOCK)
    y_rows = _experts(block_e, block_valid, xs, w_gu, b_gu[:, None, :], w_dn, b_dn[:, None, :])
    yg = _sc_gather(dest_km, y_rows).reshape(TOP_K, t, d // 2)
    out = _combine(dn_alpha, h1, gate, ln2_g[None, :], ln2_b[None, :], yg)
    return out.reshape(bsz, s, d)


def kernel(x, w_in, b_in, m_conv_w, m_conv_b, m_norm_g, w_bm, w_bf, w_o, ln1_g, ln1_b,
           w_router, b_router, w_gu, b_gu, w_dn, b_dn, ln2_g, ln2_b):
    depth = w_in.shape[0]
    h = x
    for l in range(depth):
        h = _layer(h, depth, l, w_in, b_in[l], m_conv_w[l], m_conv_b[l], m_norm_g[l], w_bm[l], w_bf[l],
                   w_o[l], ln1_g[l], ln1_b[l], w_router[l], b_router[l], w_gu[l], b_gu[l], w_dn[l],
                   b_dn[l], ln2_g[l], ln2_b[l])
    return h
```

```python
import functools
import math

import jax
import jax.numpy as jnp
from jax import lax
from jax.experimental import pallas as pl
from jax.experimental.pallas import tpu as pltpu
from jax.experimental.pallas import tpu_sc as plsc

F32 = jnp.float32
BF16 = jnp.bfloat16
I32 = jnp.int32
U32 = jnp.uint32

M_HEADS = 4
M_DQK = 128
M_DV = 128
CONV_W = 4
F_HEADS = 8
F_DH = 64
N_EXPERTS = 32
TOP_K = 4
SWIGLU_ALPHA = 1.702
SWIGLU_LIMIT = 7.0
LN_EPS = 1e-5

M_W = M_HEADS * M_DV
F_W = F_HEADS * F_DH
MQK_W = 2 * M_HEADS * M_DQK

V7X_LANES = 128
V7X_SUBLANES = 8
V7X_VMEM_BYTES = 64 * 1024 * 1024
VMEM_LIMIT_BYTES = (V7X_VMEM_BYTES * 3) // 4
EXPERT_VMEM_LIMIT_BYTES = (V7X_VMEM_BYTES * 7) // 8
V7X_SC_CORES = 2
V7X_SC_SUBCORES = 16
V7X_SC_WORKERS = V7X_SC_CORES * V7X_SC_SUBCORES

INPROJ_ROWS = 512
MLSTM_CHUNK = 256
MLSTM_SEQS = 4
FOX_Q_BLOCK = 512
FOX_K_BLOCK = 256
MERGE_ROWS = 512
MERGE_SUB_ROWS = 256
ROUTE_ROWS = 1024
ROUTE_SUB_ROWS = 256
MOE_BLOCK = 512
SC_SCATTER_ROWS = 128
SC_GATHER_ROWS = 64
SC_GATHER_WAYS = 2
COMBINE_ROWS = 512

GATE_COLS = V7X_LANES
MI_LANE = 0
MF_LANE = M_HEADS
FF_LANE = 2 * M_HEADS


def _params(semantics):
    return pltpu.CompilerParams(dimension_semantics=semantics, vmem_limit_bytes=VMEM_LIMIT_BYTES)


def _log_sigmoid(x):
    return jnp.minimum(x, 0.0) - jnp.log1p(jnp.exp(-jnp.abs(x)))


def _sigmoid(x):
    return 0.5 * jnp.tanh(0.5 * x) + 0.5


def _dot(a, b):
    return jnp.dot(a, b, preferred_element_type=F32)


def _dot_nt(a, b):
    return lax.dot_general(a, b, (((1,), (1,)), ((), ())), preferred_element_type=F32)


def _dot_tn(a, b):
    return lax.dot_general(a, b, (((0,), (0,)), ((), ())), preferred_element_type=F32)


def _split3(x):
    hi = x.astype(BF16)
    r1 = x - hi.astype(F32)
    mid = r1.astype(BF16)
    lo = (r1 - mid.astype(F32)).astype(BF16)
    return hi, mid, lo


def _dot_mask_f32(mask_bf16, x):
    hi, mid, lo = _split3(x)
    return (_dot(mask_bf16, lo) + _dot(mask_bf16, mid)) + _dot(mask_bf16, hi)


def _pack_bf16_pairs(x):
    half = x.shape[1] // 2
    bits = lax.bitcast_convert_type(x.astype(BF16).astype(F32), U32)
    return (bits[:, :half] >> 16) | bits[:, half:]


def _unpack_bf16_pairs(words):
    lo = lax.bitcast_convert_type(words << 16, F32)
    hi = lax.bitcast_convert_type(words & jnp.uint32(0xFFFF0000), F32)
    return jnp.concatenate([lo, hi], axis=1)


def _tril_mask(n, strict=False):
    r = lax.broadcasted_iota(I32, (n, n), 0)
    c = lax.broadcasted_iota(I32, (n, n), 1)
    return (r > c) if strict else (r >= c)


_OFF_MQK = 0
_OFF_MV = _OFF_MQK + MQK_W
_OFF_MO = _OFF_MV + M_W
_OFF_FQ = _OFF_MO + M_W
_OFF_FK = _OFF_FQ + F_W
_OFF_FV = _OFF_FK + F_W
_OFF_GATES = _OFF_FV + F_W
_PACKED_COLS = _OFF_GATES + GATE_COLS


def _inproj_kernel(tiles_per_seq, x_ref, w_ref, b_ref, cw_ref, cb_ref,
                   mq_ref, mk_ref, mv_ref, mo_ref, fq_ref, fk_ref, fv_ref, gates_ref, ext_ref):
    tm = x_ref.shape[0]
    pad = V7X_SUBLANES
    xb = x_ref[...].astype(BF16)

    def seg(lo, width):
        return _dot(xb, w_ref[:, lo:lo + width]) + b_ref[:, lo:lo + width]

    @pl.when(pl.program_id(0) % tiles_per_seq == 0)
    def _():
        ext_ref[0:pad, :] = jnp.zeros((pad, MQK_W), F32)

    def conv_silu(cols):
        ext_ref[pad:pad + tm, cols] = seg(_OFF_MQK + cols.start, cols.stop - cols.start)
        y = cb_ref[:, cols] + cw_ref[CONV_W - 1:CONV_W, cols] * ext_ref[pad:pad + tm, cols]
        for k in range(CONV_W - 1):
            shift = CONV_W - 1 - k
            y = y + cw_ref[k:k + 1, cols] * ext_ref[pad - shift:pad - shift + tm, cols]
        ext_ref[0:pad, cols] = ext_ref[tm:tm + pad, cols]
        return y * _sigmoid(y)

    half = MQK_W // 2
    group = half // 2
    mq_ref[:, :group] = conv_silu(slice(0, group)).astype(BF16)
    mv_ref[...] = seg(_OFF_MV, M_W).astype(BF16)
    mq_ref[:, group:] = conv_silu(slice(group, half)).astype(BF16)
    mo_ref[...] = seg(_OFF_MO, M_W)
    mk_ref[:, :group] = conv_silu(slice(half, half + group)) * (M_DQK ** -0.5)
    fq_ref[...] = (seg(_OFF_FQ, F_W) * (F_DH ** -0.5)).astype(BF16)
    mk_ref[:, group:] = conv_silu(slice(half + group, MQK_W)) * (M_DQK ** -0.5)
    fk_ref[...] = seg(_OFF_FK, F_W).astype(BF16)
    fv_ref[...] = seg(_OFF_FV, F_W).astype(BF16)
    gates_ref[...] = seg(_OFF_GATES, GATE_COLS)


def _inproj(x2, w_packed, b_packed, conv_w, conv_b, s):
    t, d = x2.shape
    tm = INPROJ_ROWS
    row = lambda i: (i, 0)
    const = lambda i: (0, 0)
    out_shapes = (
        jax.ShapeDtypeStruct((t, MQK_W // 2), BF16),
        jax.ShapeDtypeStruct((t, MQK_W // 2), F32),
        jax.ShapeDtypeStruct((t, M_W), BF16),
        jax.ShapeDtypeStruct((t, M_W), F32),
        jax.ShapeDtypeStruct((t, F_W), BF16),
        jax.ShapeDtypeStruct((t, F_W), BF16),
        jax.ShapeDtypeStruct((t, F_W), BF16),
        jax.ShapeDtypeStruct((t, GATE_COLS), F32),
    )
    return pl.pallas_call(
        functools.partial(_inproj_kernel, s // tm),
        grid=(t // tm,),
        in_specs=[
            pl.BlockSpec((tm, d), row),
            pl.BlockSpec((d, _PACKED_COLS), const),
            pl.BlockSpec((1, _PACKED_COLS), const),
            pl.BlockSpec((CONV_W, MQK_W), const),
            pl.BlockSpec((1, MQK_W), const),
        ],
        out_specs=tuple(pl.BlockSpec((tm, o.shape[1]), row) for o in out_shapes),
        out_shape=out_shapes,
        scratch_shapes=[pltpu.VMEM((tm + V7X_SUBLANES, MQK_W), F32)],
        compiler_params=_params(("arbitrary",)),
        name="inproj",
    )(x2, w_packed, b_packed, conv_w, conv_b)


def _fox_cumsum_kernel(g_ref, ccol_ref, crel_k_ref, crel_q_ref):
    s = g_ref.shape[0]
    cb = FOX_K_BLOCK
    per_q = FOX_Q_BLOCK // cb
    tri = _tril_mask(cb).astype(BF16)
    carry = jnp.zeros((1, GATE_COLS), F32)
    for j in range(s // cb):
        rows = slice(j * cb, (j + 1) * cb)
        if j % per_q == 0:
            q_carry = carry
        within = _dot_mask_f32(tri, _log_sigmoid(g_ref[rows, :]))
        crel_k_ref[rows, :] = within
        crel_q_ref[rows, :] = within + (carry - q_carry)
        ccol_ref[rows, :] = within + carry
        carry = carry + within[cb - 1:cb, :]


def _fox_cumsum(gates, bsz, s):
    t = gates.shape[0]
    spec = pl.BlockSpec((s, GATE_COLS), lambda b: (b, 0))
    shape = jax.ShapeDtypeStruct((t, GATE_COLS), F32)
    return pl.pallas_call(
        _fox_cumsum_kernel,
        grid=(bsz,),
        in_specs=[spec],
        out_specs=(spec, spec, spec),
        out_shape=(shape, shape, shape),
        compiler_params=_params(("parallel",)),
        name="fox_cumsum",
    )(gates)


def _mlstm_kernel(mq_ref, mk_ref, mv_ref, mo_ref, gates_ref, ng_ref, hm_ref, state_ref, m_ref):
    @pl.when(pl.program_id(1) == 0)
    def _():
        state_ref[...] = jnp.zeros(state_ref.shape, F32)
        m_ref[...] = jnp.zeros(m_ref.shape, F32)

    seqs = range(mq_ref.shape[0])
    states = [[state_ref[bb, h] for h in range(M_HEADS)] for bb in seqs]
    maxes = [[m_ref[bb, h][0:1, 0:1] for h in range(M_HEADS)] for bb in seqs]
    results = [_mlstm_chunk(mq_ref.at[bb], mk_ref.at[bb], mv_ref.at[bb], mo_ref.at[bb], gates_ref.at[bb],
                            ng_ref, states[bb], maxes[bb]) for bb in seqs]
    for bb in seqs:
        for h, (out_h, state_h, m_h) in enumerate(results[bb]):
            hm_ref[bb, :, h * M_DV:(h + 1) * M_DV] = out_h
            state_ref[bb, h] = state_h
            m_ref[bb, h] = jnp.broadcast_to(m_h, m_ref.shape[2:])


def _mlstm_chunk(mq_ref, mk_ref, mv_ref, mo_ref, gates_ref, ng_ref, states, maxes):
    L = MLSTM_CHUNK
    reps = L // V7X_LANES
    results = []
    gates = gates_ref[...]
    bfull = _dot_mask_f32(_tril_mask(L).astype(BF16), _log_sigmoid(gates))
    b_rows = bfull.T
    z_all = gates - pltpu.roll(bfull, shift=V7X_LANES - (MF_LANE - MI_LANE), axis=1)
    visible = (lax.broadcasted_iota(I32, (L, L), 0) <= lax.broadcasted_iota(I32, (L, L), 1))
    ones_rows = (lax.broadcasted_iota(I32, (M_DV, L), 0) == 0).astype(BF16)

    for h in range(M_HEADS):
        b_row = b_rows[MF_LANE + h:MF_LANE + h + 1, :]
        g_tot = b_row[:, L - 1:L]
        m_prev = maxes[h]
        z = jnp.broadcast_to(z_all[:, MI_LANE + h:MI_LANE + h + 1], (L, V7X_LANES))

        q_h = mq_ref[:, h * M_DQK:(h + 1) * M_DQK]
        k_f = mk_ref[:, h * M_DQK:(h + 1) * M_DQK]
        k_h = k_f.astype(BF16)
        v_t = mv_ref[:, h * M_DV:(h + 1) * M_DV].astype(F32).T.astype(BF16)
        cn_t = states[h]

        dlog = jnp.where(visible, b_row + jnp.tile(z, (1, reps)), -jnp.inf)
        inter_log = b_row + m_prev
        m_t = jnp.maximum(inter_log, jnp.max(dlog, axis=0, keepdims=True))
        w_inter = jnp.exp(inter_log - m_t)
        qkw = _dot_nt(k_h, q_h) * jnp.exp(dlog - m_t)
        qc = _dot_nt(cn_t.astype(BF16), q_h)
        num = w_inter * qc[:M_DV, :] + _dot(v_t, qkw.astype(BF16))
        den = w_inter * qc[M_DV:M_DV + 1, :] + jnp.sum(qkw, axis=0, keepdims=True)
        hh = num / jnp.maximum(jnp.abs(den), jnp.exp(-m_t))

        mu = jnp.mean(hh, axis=0, keepdims=True)
        dv = hh - mu
        var = jnp.mean(dv * dv, axis=0, keepdims=True)
        hn = (dv * lax.rsqrt(var + LN_EPS)) * jnp.tile(ng_ref[h * M_DV:(h + 1) * M_DV, :], (1, reps))
        out_h = (_sigmoid(mo_ref[:, h * M_DV:(h + 1) * M_DV]) * hn.T).astype(BF16)

        a = g_tot + z
        m_new = jnp.maximum(g_tot + m_prev, jnp.max(a, axis=0, keepdims=True)[:, 0:1])
        decay = jnp.exp(g_tot + m_prev - m_new)
        kw = (k_f * jnp.exp(a - m_new)).astype(BF16)
        v_aug = jnp.concatenate([v_t, ones_rows], axis=0)
        results.append((out_h, decay * cn_t + _dot(v_aug, kw), m_new))
    return results


def _mlstm(mq, mk, mv, mo, gates, norm_g, bsz, s):
    t = mq.shape[0]
    L = MLSTM_CHUNK
    nb = MLSTM_SEQS
    seq = lambda a: a.reshape(bsz, s, a.shape[1])
    blk = lambda width: pl.BlockSpec((nb, L, width), lambda g, n: (g, n, 0))
    const = lambda g, n: (0, 0)
    hm = pl.pallas_call(
        _mlstm_kernel,
        grid=(bsz // nb, s // L),
        in_specs=[
            blk(MQK_W // 2), blk(MQK_W // 2), blk(M_W), blk(M_W), blk(GATE_COLS),
            pl.BlockSpec((M_W, V7X_LANES), const),
        ],
        out_specs=blk(M_W),
        out_shape=jax.ShapeDtypeStruct((bsz, s, M_W), BF16),
        scratch_shapes=[
            pltpu.VMEM((nb, M_HEADS, 2 * M_DV, M_DQK), F32),
            pltpu.VMEM((nb, M_HEADS, V7X_SUBLANES, V7X_LANES), F32),
        ],
        compiler_params=_params(("parallel", "arbitrary")),
        name="mlstm",
    )(seq(mq), seq(mk), seq(mv), seq(mo), seq(gates),
      jnp.broadcast_to(norm_g[:, None], (M_W, V7X_LANES)))
    return hm.reshape(t, M_W)


_FOX_FEATURES = 3


def _fox_operands(x, c_tile, c_lane, key_side):
    rows = x.shape[0]
    nf = _FOX_FEATURES
    first = 0 if key_side else nf
    src = lax.broadcasted_iota(I32, (V7X_LANES, V7X_LANES), 0) - c_lane
    dst = lax.broadcasted_iota(I32, (V7X_LANES, V7X_LANES), 1)
    feats = jnp.zeros((rows, V7X_LANES), F32)
    for n, part in enumerate(_split3(c_tile)):
        place = jnp.logical_or(jnp.logical_and(src == 0, dst == F_DH + first + n),
                               jnp.logical_and(src == 1, dst == first + n))
        feats = feats + _dot(part, place.astype(BF16))
    lane = lax.broadcasted_iota(I32, (rows, V7X_LANES), 1)
    within = lane % F_DH
    const_lanes = jnp.logical_and(within >= nf - first, within < 2 * nf - first)
    feats = jnp.where(const_lanes, 1.0 if key_side else -1.0, feats).astype(BF16)
    low = lane < F_DH
    return jnp.where(low, x, feats), jnp.where(low, feats, x)


def _fox_kernel(q_ref, k_ref, v_ref, ccol_ref, crel_k_ref, crel_q_ref, o_ref, qaug_ref, kaug_ref, vt_ref,
                st_a0, st_a1, st_b0, st_b1, pe_a0, pe_a1, pe_b0, pe_b1, acc0, acc1):
    blk = FOX_K_BLOCK
    tq = FOX_Q_BLOCK
    assert tq == 2 * blk
    strips = tq // V7X_LANES
    st_a_refs, st_b_refs = (st_a0, st_a1), (st_b0, st_b1)
    pe_a_refs, pe_b_refs = (pe_a0, pe_a1), (pe_b0, pe_b1)
    acc_refs = (acc0, acc1)
    p = pl.program_id(1)
    i = pl.program_id(2)

    @pl.when(i == 0)
    def _():
        c_lane = FF_LANE + 2 * p
        qaug_ref[0], qaug_ref[1] = _fox_operands(q_ref[...], crel_q_ref[...], c_lane, False)
        kaug_ref[0], kaug_ref[1] = _fox_operands(k_ref[...], crel_k_ref[...], c_lane, True)
        v_t = v_ref[...].astype(F32).T
        for j in range(vt_ref.shape[0]):
            vt_ref[j] = v_t[:, j * blk:(j + 1) * blk].astype(BF16)

    q_start = pl.multiple_of(i * tq, tq)
    q_heads = [qaug_ref[hh, pl.ds(q_start, tq), :] for hh in range(2)]
    key_row = lax.broadcasted_iota(I32, (blk, V7X_LANES), 0)
    query_col = lax.broadcasted_iota(I32, (blk, V7X_LANES), 1)
    last_chunk = 2 * i + 1
    head_lane = lax.broadcasted_iota(I32, (1, GATE_COLS), 1) - (FF_LANE + 2 * p)

    def c_before(position, hh):
        row = ccol_ref[pl.ds(jnp.maximum(position - 1, 0), 1), :]
        keep = jnp.logical_and(head_lane == hh, position > 0)
        return jnp.sum(jnp.where(keep, row, 0.0), axis=-1, keepdims=True)

    c_query0 = [c_before(q_start, hh) for hh in range(2)]

    def scores(j, hh):
        start = pl.multiple_of(j * blk, blk)
        return _dot_nt(kaug_ref[hh, pl.ds(start, blk), :], q_heads[hh])

    def softmax_update(st_ref, pe_ref, m_old, l_old, base, key_minus_query=None):
        alphas, ms, ls = [], [], []
        for c in range(strips):
            cols = slice(c * V7X_LANES, (c + 1) * V7X_LANES)
            gap = None if key_minus_query is None else key_minus_query - c * V7X_LANES
            if gap is not None and gap - (V7X_LANES - 1) > 0:
                pe_ref[:, cols] = jnp.zeros((blk, V7X_LANES), BF16)
                alphas.append(jnp.ones((1, V7X_LANES), F32))
                ms.append(m_old[:, cols])
                ls.append(l_old[:, cols])
                continue
            st = st_ref[:, cols]
            if gap is not None and gap + (blk - 1) > 0:
                st = jnp.where(key_row + gap <= query_col, st, -jnp.inf)
            m_new = jnp.maximum(m_old[:, cols], jnp.max(st, axis=0, keepdims=True) + base)
            alpha = jnp.exp(m_old[:, cols] - m_new)
            pe = jnp.exp(st - (m_new - base))
            pe_ref[:, cols] = pe.astype(BF16)
            alphas.append(alpha)
            ms.append(m_new)
            ls.append(alpha * l_old[:, cols] + jnp.sum(pe, axis=0, keepdims=True))
        cat = lambda parts: jnp.concatenate(parts, axis=1)
        return cat(alphas), cat(ms), cat(ls)

    def pair(mi, carry, diagonal=False):
        a = 2 * mi
        b = a + 1
        v_prev = vt_ref[jnp.maximum(a - 1, 0)]
        v_a = vt_ref[a]
        partial = []
        for hh in range(2):
            alpha_prev = carry[hh][0]
            partial.append(alpha_prev * acc_refs[hh][...] + _dot(v_prev, pe_b_refs[hh][...]))
            st_b_refs[hh][...] = scores(b, hh)
        stats = []
        for hh in range(2):
            _, m_old, l_old = carry[hh]
            stats.append(softmax_update(st_a_refs[hh], pe_a_refs[hh], m_old, l_old,
                                        c_query0[hh] - c_before(a * blk, hh), 0 if diagonal else None))
        for hh in range(2):
            alpha_a = stats[hh][0]
            acc_refs[hh][...] = alpha_a * partial[hh] + _dot(v_a, pe_a_refs[hh][...])
            if not diagonal:
                st_a_refs[hh][...] = scores(a + 2, hh)
        return tuple(softmax_update(st_b_refs[hh], pe_b_refs[hh], stats[hh][1], stats[hh][2],
                                    c_query0[hh] - c_before(b * blk, hh), blk if diagonal else None)
                     for hh in range(2))

    for hh in range(2):
        st_a_refs[hh][...] = scores(0, hh)
        pe_b_refs[hh][...] = jnp.zeros((blk, tq), BF16)
        acc_refs[hh][...] = jnp.zeros((V7X_LANES, tq), F32)
    init = tuple((jnp.ones((1, tq), F32), jnp.full((1, tq), -jnp.inf, F32), jnp.zeros((1, tq), F32))
                 for _ in range(2))
    final = pair(i, lax.fori_loop(0, i, pair, init), diagonal=True)
    v_last = vt_ref[last_chunk]
    outs = []
    for hh in range(2):
        alpha, _, l_fin = final[hh]
        outs.append((alpha * acc_refs[hh][...] + _dot(v_last, pe_b_refs[hh][...])) / l_fin)
    row = lax.broadcasted_iota(I32, (V7X_LANES, tq), 0)
    o_t = jnp.where(row < F_DH, outs[0], outs[1])
    o_ref[...] = o_t.T.astype(BF16)


def _fox_attention(fq, fk, fv, ccol, crel_k, crel_q, bsz, s):
    t = fq.shape[0]
    blk = FOX_K_BLOCK
    tq = FOX_Q_BLOCK
    nq = s // tq
    pairs = F_HEADS // 2
    qmap = lambda b, p, i: (b * nq + i, p)
    kvmap = lambda b, p, i: (b, p)
    return pl.pallas_call(
        _fox_kernel,
        grid=(bsz, pairs, nq),
        in_specs=[
            pl.BlockSpec((s, V7X_LANES), kvmap),
            pl.BlockSpec((s, V7X_LANES), kvmap),
            pl.BlockSpec((s, V7X_LANES), kvmap),
            pl.BlockSpec((s, GATE_COLS), lambda b, p, i: (b, 0)),
            pl.BlockSpec((s, GATE_COLS), lambda b, p, i: (b, 0)),
            pl.BlockSpec((s, GATE_COLS), lambda b, p, i: (b, 0)),
        ],
        out_specs=pl.BlockSpec((tq, V7X_LANES), qmap),
        out_shape=jax.ShapeDtypeStruct((t, F_W), BF16),
        scratch_shapes=[
            pltpu.VMEM((2, s, V7X_LANES), BF16),
            pltpu.VMEM((2, s, V7X_LANES), BF16),
            pltpu.VMEM((s // blk, V7X_LANES, blk), BF16),
        ] + [pltpu.VMEM((blk, tq), F32)] * 4 + [pltpu.VMEM((blk, tq), BF16)] * 4
          + [pltpu.VMEM((V7X_LANES, tq), F32)] * 2,
        compiler_params=_params(("parallel", "parallel", "arbitrary")),
        name="fox_attention",
    )(fq, fk, fv, ccol, crel_k, crel_q)


def _layer_norm_rows(r, g, b):
    mu = jnp.mean(r, axis=-1, keepdims=True)
    d = r - mu
    var = jnp.mean(d * d, axis=-1, keepdims=True)
    return (d * lax.rsqrt(var + LN_EPS)) * g + b


def _merge_kernel(dn_alpha, hm_ref, hf_ref, x_ref, wg_ref, bg_ref, wbm_ref, wbf_ref, wo_ref,
                  g_ref, b_ref, wrh_ref, wrl_ref, br_ref, h1_ref, h1p_ref, gate_ref, tope_ref, cnt_ref,
                  resid_ref):
    i = pl.program_id(0)
    slot = i % 2

    @pl.when(i == 0)
    def _():
        resid_ref[1] = jnp.zeros(resid_ref.shape[1:], F32)

    subs = [slice(n * MERGE_SUB_ROWS, (n + 1) * MERGE_SUB_ROWS)
            for n in range(x_ref.shape[0] // MERGE_SUB_ROWS)]
    counts = jnp.zeros((1, V7X_LANES), F32)
    for rows in subs:
        counts = counts + _merge_tail(resid_ref[1 - slot, rows, :], rows, g_ref, b_ref, wrh_ref, wrl_ref,
                                      br_ref, h1_ref, h1p_ref, gate_ref, tope_ref)
    sub = lax.broadcasted_iota(I32, cnt_ref.shape, 0)
    cnt_ref[...] = jnp.where(sub == 0, counts, 0.0)

    for rows in subs:
        x = x_ref[rows, :]
        d = x.shape[1]
        gmf = _dot(x.astype(BF16), wg_ref[...]) + bg_ref[...]
        ym = _dot(hm_ref[rows, :], wbm_ref[...])
        yf = _dot(hf_ref[rows, :], wbf_ref[...])
        y = _sigmoid(gmf[:, :d]) * ym + _sigmoid(gmf[:, d:]) * yf
        resid_ref[slot, rows, :] = dn_alpha * x + _dot(y.astype(BF16), wo_ref[...])


def _merge_tail(resid, rows, g_ref, b_ref, wrh_ref, wrl_ref, br_ref, h1_ref, h1p_ref, gate_ref, tope_ref):
    h1 = _layer_norm_rows(resid, g_ref[...], b_ref[...])
    h1_ref[rows, :] = h1

    h1p_ref[rows, :] = _pack_bf16_pairs(h1)
    hb = h1.astype(BF16)

    lo = (h1 - hb.astype(F32)).astype(BF16)
    logits = (_dot(lo, wrh_ref[...]) + _dot(hb, wrl_ref[...])) + _dot(hb, wrh_ref[...]) + br_ref[...]
    tm = logits.shape[0]
    lane = lax.broadcasted_iota(I32, (tm, V7X_LANES), 1)
    vals = jnp.where(lane < N_EXPERTS, logits, -jnp.inf)
    top_v, top_i = [], []
    for _ in range(TOP_K):
        mx = jnp.max(vals, axis=-1, keepdims=True)
        idx = jnp.min(jnp.where(vals == mx, lane, V7X_LANES), axis=-1, keepdims=True)
        top_v.append(mx)
        top_i.append(idx)
        vals = jnp.where(lane == idx, -jnp.inf, vals)
    ex = [jnp.exp(v - top_v[0]) for v in top_v]
    tot = ex[0]
    for e in ex[1:]:
        tot = tot + e
    gate = jnp.zeros((tm, V7X_LANES), F32)
    tope = jnp.zeros((tm, V7X_LANES), I32)
    member = jnp.zeros((tm, V7X_LANES), F32)
    for k in range(TOP_K):
        gate = jnp.where(lane == k, ex[k] / tot, gate)
        tope = jnp.where(lane == k, top_i[k], tope)
        member = member + (lane == top_i[k]).astype(F32)
    gate_ref[rows, :] = gate
    tope_ref[rows, :] = tope
    return jnp.sum(member, axis=0, keepdims=True)


def _merge(dn_alpha, hm, hf, x2, wg, bg, wbm, wbf, wo, ln_g, ln_b, wr_hi, wr_lo, br):
    t, d = x2.shape
    tm = MERGE_ROWS
    nt = t // tm
    row = lambda i: (jnp.minimum(i, nt - 1), 0)
    out_row = lambda i: (jnp.maximum(i - 1, 0), 0)
    const = lambda i: (0, 0)
    full = lambda a: pl.BlockSpec(a.shape, const)
    return pl.pallas_call(
        functools.partial(_merge_kernel, dn_alpha),
        grid=(nt + 1,),
        in_specs=[
            pl.BlockSpec((tm, M_W), row),
            pl.BlockSpec((tm, F_W), row),
            pl.BlockSpec((tm, d), row),
            full(wg), full(bg), full(wbm), full(wbf), full(wo), full(ln_g), full(ln_b),
            full(wr_hi), full(wr_lo), full(br),
        ],
        out_specs=(
            pl.BlockSpec((tm, d), out_row),
            pl.BlockSpec((tm, d // 2), out_row),
            pl.BlockSpec((tm, V7X_LANES), out_row),
            pl.BlockSpec((tm, V7X_LANES), out_row),
            pl.BlockSpec((V7X_SUBLANES, V7X_LANES), out_row),
        ),
        out_shape=(
            jax.ShapeDtypeStruct((t, d), F32),
            jax.ShapeDtypeStruct((t, d // 2), U32),
            jax.ShapeDtypeStruct((t, V7X_LANES), F32),
            jax.ShapeDtypeStruct((t, V7X_LANES), I32),
            jax.ShapeDtypeStruct((t // tm * V7X_SUBLANES, V7X_LANES), F32),
        ),
        scratch_shapes=[pltpu.VMEM((2, tm, d), F32)],
        compiler_params=_params(("arbitrary",)),
        name="merge_ln1_router",
    )(hm, hf, x2, wg, bg, wbm, wbf, wo, ln_g, ln_b, wr_hi, wr_lo, br)


def _lane_cumsum(x):
    lane = lax.broadcasted_iota(I32, x.shape, 1)
    d = 1
    while d < V7X_LANES:
        x = x + jnp.where(lane >= d, pltpu.roll(x, shift=d, axis=1), 0.0)
        d *= 2
    return x


def _routing_kernel(cnt_ref, tope_ref, dest_ref, table_ref, run_ref, start_ref):
    sb = ROUTE_SUB_ROWS

    @pl.when(pl.program_id(0) == 0)
    def _():
        total = jnp.sum(cnt_ref[...], axis=0, keepdims=True)
        counts = jnp.broadcast_to(total, (V7X_SUBLANES, V7X_LANES))
        padded = jnp.ceil(counts * (1.0 / MOE_BLOCK)) * MOE_BLOCK
        pad_end = _lane_cumsum(padded)
        pad_start = pad_end - padded
        start_ref[...] = pad_start
        run_ref[...] = jnp.zeros(run_ref.shape, F32)
        nb = table_ref.shape[0]
        blk = lax.broadcasted_iota(I32, (nb, V7X_LANES), 0).astype(F32) * MOE_BLOCK
        ln = lax.broadcasted_iota(I32, (nb, V7X_LANES), 1)
        done = jnp.logical_and(pad_end[0:1, :] <= blk, ln < N_EXPERTS)
        be = jnp.minimum(jnp.sum(done.astype(F32), axis=-1, keepdims=True), N_EXPERTS - 1.0)
        onehot = ln == be.astype(I32)
        cnt_e = jnp.sum(jnp.where(onehot, counts[0:1, :], 0.0), axis=-1, keepdims=True)
        start_e = jnp.sum(jnp.where(onehot, pad_start[0:1, :], 0.0), axis=-1, keepdims=True)
        valid = jnp.clip(cnt_e - (blk[:, 0:1] - start_e), 0.0, float(MOE_BLOCK))
        table_ref[...] = jnp.where(ln == 0, be.astype(I32),
                                   jnp.where(ln == 1, valid.astype(I32), 0))

    earlier = _tril_mask(sb, strict=True).astype(BF16)
    lane = lax.broadcasted_iota(I32, (sb, V7X_LANES), 1)
    for j in range(tope_ref.shape[0] // sb):
        tope = tope_ref[j * sb:(j + 1) * sb, :]
        hit = [lane == tope[:, k:k + 1] for k in range(TOP_K)]
        member = jnp.zeros((sb, V7X_LANES), F32)
        for k in range(TOP_K):
            member = member + hit[k].astype(F32)
        base = _dot(earlier, member.astype(BF16)) + (run_ref[0:1, :] + start_ref[0:1, :])
        dest = jnp.zeros((sb, V7X_LANES), F32)
        for k in range(TOP_K):
            dk = jnp.sum(jnp.where(hit[k], base, 0.0), axis=-1, keepdims=True)
            dest = jnp.where(lane == k, dk, dest)
        dest_ref[:, j * sb:(j + 1) * sb] = dest.T[0:V7X_SUBLANES, :].astype(I32)
        run_ref[...] = run_ref[...] + jnp.sum(member, axis=0, keepdims=True)


def _routing(tile_counts, tope, n_blocks):
    t = tope.shape[0]
    tr = ROUTE_ROWS
    return pl.pallas_call(
        _routing_kernel,
        grid=(t // tr,),
        in_specs=[pl.BlockSpec(tile_counts.shape, lambda i: (0, 0)),
                  pl.BlockSpec((tr, V7X_LANES), lambda i: (i, 0))],
        out_specs=(
            pl.BlockSpec((V7X_SUBLANES, tr), lambda i: (0, i)),
            pl.BlockSpec((n_blocks, V7X_LANES), lambda i: (0, 0)),
        ),
        out_shape=(
            jax.ShapeDtypeStruct((V7X_SUBLANES, t), I32),
            jax.ShapeDtypeStruct((n_blocks, V7X_LANES), I32),
        ),
        scratch_shapes=[
            pltpu.VMEM((V7X_SUBLANES, V7X_LANES), F32),
            pltpu.VMEM((V7X_SUBLANES, V7X_LANES), F32),
        ],
        compiler_params=_params(("arbitrary",)),
        name="routing",
    )(tile_counts, tope)


def _sc_worker_id():
    return lax.axis_index("s") * V7X_SC_CORES + lax.axis_index("c")


def _sc_mesh():
    return plsc.VectorSubcoreMesh(core_axis_name="c", subcore_axis_name="s",
                                  num_cores=V7X_SC_CORES, num_subcores=V7X_SC_SUBCORES)


def _sc_dispatch(dest_km, h1p, n_rows):
    t, w = h1p.shape
    per_worker = t // V7X_SC_WORKERS
    ch = SC_SCATTER_ROWS

    @functools.partial(
        pl.kernel, mesh=_sc_mesh(),
        out_type=jax.ShapeDtypeStruct((n_rows, w), h1p.dtype),
        scratch_types=[pltpu.VMEM((ch, w), h1p.dtype)]
        + [pltpu.VMEM((ch,), I32)] * TOP_K + [pltpu.SemaphoreType.DMA] * TOP_K,
        name="sc_dispatch",
    )
    def scatter_rows(dest_hbm, h1p_hbm, xs_hbm, rows_v, *idx_and_sems):
        idx_refs, sems = idx_and_sems[:TOP_K], idx_and_sems[TOP_K:]
        first = _sc_worker_id() * per_worker

        @pl.loop(0, per_worker // ch)
        def _(j):
            base = first + j * ch
            pltpu.sync_copy(h1p_hbm.at[pl.ds(base, ch)], rows_v)
            copies = []
            for k in range(TOP_K):
                pltpu.sync_copy(dest_hbm.at[pl.ds(k * t + base, ch)], idx_refs[k])
                copies.append(pltpu.async_copy(rows_v, xs_hbm.at[idx_refs[k]], sems[k]))
            for copy in copies:
                copy.wait()

    return scatter_rows(dest_km, h1p)


def _sc_gather(dest_km, y_rows):
    n = dest_km.shape[0]
    w = y_rows.shape[1]
    per_worker = n // V7X_SC_WORKERS
    ch = SC_GATHER_ROWS
    ways = SC_GATHER_WAYS

    @functools.partial(
        pl.kernel, mesh=_sc_mesh(),
        out_type=jax.ShapeDtypeStruct((n, w), y_rows.dtype),
        scratch_types=[pltpu.VMEM((ch,), I32)] * ways + [pltpu.VMEM((ch, w), y_rows.dtype)] * ways
        + [pltpu.SemaphoreType.DMA] * (2 * ways),
        name="sc_gather",
    )
    def gather_rows(dest_hbm, y_hbm, out_hbm, *scratch):
        idx_refs, row_refs = scratch[:ways], scratch[ways:2 * ways]
        gather_sems, store_sems = scratch[2 * ways:3 * ways], scratch[3 * ways:]
        first = _sc_worker_id() * per_worker

        @pl.loop(0, per_worker // (ch * ways))
        def _(j):
            bases = [first + (j * ways + u) * ch for u in range(ways)]
            gathers = []
            for u in range(ways):
                pltpu.sync_copy(dest_hbm.at[pl.ds(bases[u], ch)], idx_refs[u])
                gathers.append(pltpu.async_copy(y_hbm.at[idx_refs[u]], row_refs[u], gather_sems[u]))
            stores = []
            for u in range(ways):
                gathers[u].wait()
                stores.append(pltpu.async_copy(row_refs[u], out_hbm.at[pl.ds(bases[u], ch)], store_sems[u]))
            for store in stores:
                store.wait()

    return gather_rows(dest_km, y_rows)


def _expert_kernel(be_ref, nv_ref, xs_ref, wgu_f32_ref, bgu_ref, wdn_f32_ref, bdn_ref, y_ref,
                   wgu_slots, wdn_slots, slot_ref):
    i = pl.program_id(0)
    n_blocks = pl.num_programs(0) - 1
    nv = jnp.where(i > 0, nv_ref[jnp.maximum(i - 1, 0)], 0)
    half = MOE_BLOCK // 2

    @pl.when(i == 0)
    def _():
        slot_ref[0] = 0

    slot = slot_ref[0]
    wgu_ref = wgu_slots.at[slot]
    wdn_ref = wdn_slots.at[slot]

    def ffn(rows):
        x = _unpack_bf16_pairs(xs_ref[rows, :])
        rowid = rows.start + lax.broadcasted_iota(I32, x.shape, 0)
        x = jnp.where(rowid < nv, x, 0.0).astype(BF16)
        gu = _dot(x, wgu_ref[...]) + bgu_ref[...]
        f = gu.shape[1] // 2
        glu = jnp.minimum(gu[:, :f], SWIGLU_LIMIT)
        lin = jnp.clip(gu[:, f:], -SWIGLU_LIMIT, SWIGLU_LIMIT)
        act = glu * _sigmoid(SWIGLU_ALPHA * glu) * (lin + 1.0)
        y_ref[rows, :] = _pack_bf16_pairs(_dot(act.astype(BF16), wdn_ref[...]) + bdn_ref[...])

    @pl.when(nv == 0)
    def _():
        y_ref[...] = jnp.zeros(y_ref.shape, U32)

    @pl.when(jnp.logical_and(nv > 0, nv <= half))
    def _():
        ffn(slice(0, half))
        y_ref[half:, :] = jnp.zeros((MOE_BLOCK - half, y_ref.shape[1]), U32)

    @pl.when(nv > half)
    def _():
        ffn(slice(0, half))
        ffn(slice(half, MOE_BLOCK))

    new_expert = jnp.logical_or(i == 0, be_ref[jnp.minimum(i, n_blocks - 1)] != be_ref[jnp.maximum(i - 1, 0)])

    @pl.when(jnp.logical_and(i < n_blocks, new_expert))
    def _():
        wgu_slots[1 - slot] = wgu_f32_ref[...].astype(BF16)
        wdn_slots[1 - slot] = wdn_f32_ref[...].astype(BF16)
        slot_ref[0] = 1 - slot


def _experts(block_e, block_valid, xs, wgu, bgu, wdn, bdn):
    n_rows, w = xs.shape
    e, d, f2 = wgu.shape
    n_blocks = n_rows // MOE_BLOCK
    ahead = lambda i, be: be[jnp.minimum(i, n_blocks - 1)]
    behind = lambda i: jnp.maximum(i - 1, 0)
    grid_spec = pltpu.PrefetchScalarGridSpec(
        num_scalar_prefetch=2,
        grid=(n_blocks + 1,),
        in_specs=[
            pl.BlockSpec((MOE_BLOCK, w), lambda i, be, nv: (behind(i), 0)),
            pl.BlockSpec((None, d, f2), lambda i, be, nv: (ahead(i, be), 0, 0)),
            pl.BlockSpec((None, 1, f2), lambda i, be, nv: (be[behind(i)], 0, 0)),
            pl.BlockSpec((None, f2 // 2, d), lambda i, be, nv: (ahead(i, be), 0, 0)),
            pl.BlockSpec((None, 1, d), lambda i, be, nv: (be[behind(i)], 0, 0)),
        ],
        out_specs=pl.BlockSpec((MOE_BLOCK, d // 2), lambda i, be, nv: (behind(i), 0)),
        scratch_shapes=[pltpu.VMEM((2, d, f2), BF16), pltpu.VMEM((2, f2 // 2, d), BF16),
                        pltpu.SMEM((1,), I32)],
    )
    return pl.pallas_call(
        _expert_kernel,
        grid_spec=grid_spec,
        out_shape=jax.ShapeDtypeStruct((n_rows, d // 2), U32),
        compiler_params=pltpu.CompilerParams(dimension_semantics=("arbitrary",),
                                             vmem_limit_bytes=EXPERT_VMEM_LIMIT_BYTES),
        name="experts",
    )(block_e, block_valid, xs, wgu, bgu, wdn, bdn)


def _combine_kernel(dn_alpha, h1_ref, gate_ref, g_ref, b_ref, yg_ref, o_ref):
    gate = gate_ref[...]
    ffn = gate[:, 0:1] * _unpack_bf16_pairs(yg_ref[0])
    for k in range(1, TOP_K):
        ffn = ffn + gate[:, k:k + 1] * _unpack_bf16_pairs(yg_ref[k])
    o_ref[...] = _layer_norm_rows(dn_alpha * h1_ref[...] + ffn, g_ref[...], b_ref[...])


def _combine(dn_alpha, h1, gate, ln_g, ln_b, yg):
    t, d = h1.shape
    tc = COMBINE_ROWS
    row = lambda i: (i, 0)
    const = lambda i: (0, 0)
    return pl.pallas_call(
        functools.partial(_combine_kernel, dn_alpha),
        grid=(t // tc,),
        in_specs=[
            pl.BlockSpec((tc, d), row),
            pl.BlockSpec((tc, V7X_LANES), row),
            pl.BlockSpec((1, d), const),
            pl.BlockSpec((1, d), const),
            pl.BlockSpec((TOP_K, tc, d // 2), lambda i: (0, i, 0)),
        ],
        out_specs=pl.BlockSpec((tc, d), row),
        out_shape=jax.ShapeDtypeStruct((t, d), F32),
        compiler_params=_params(("parallel",)),
        name="combine_ln2",
    )(h1, gate, ln_g, ln_b, yg)


def _in_proj_columns(d):
    o = 0
    cols = {}
    for name, width in (("mqk", MQK_W), ("mv", M_W), ("mo", M_W), ("mi", M_HEADS), ("mf", M_HEADS),
                        ("fq", F_W), ("fk", F_W), ("fv", F_W), ("ff", F_HEADS), ("gm", d), ("gf", d)):
        cols[name] = (o, o + width)
        o += width
    return cols


def _pack_w_kernel(w_ref, main_ref, gate_ref):
    d = gate_ref.shape[1] // 2
    cols = _in_proj_columns(d)
    out = 0
    for name in ("mqk", "mv", "mo", "fq", "fk", "fv"):
        lo, hi = cols[name]
        main_ref[:, out:out + hi - lo] = w_ref[:, lo:hi].astype(BF16)
        out += hi - lo
    lane = lax.broadcasted_iota(I32, (w_ref.shape[0], GATE_COLS), 1)
    mi_lo, ff_lo = cols["mi"][0], cols["ff"][0] - 2 * M_HEADS
    assert mi_lo % V7X_LANES == 0 and ff_lo % V7X_LANES == 0 and cols["mf"][0] == mi_lo + M_HEADS
    gates = jnp.where(lane < 2 * M_HEADS, w_ref[:, mi_lo:mi_lo + GATE_COLS],
                      jnp.where(lane < 2 * M_HEADS + F_HEADS, w_ref[:, ff_lo:ff_lo + GATE_COLS], 0.0))
    main_ref[:, out:out + GATE_COLS] = gates.astype(BF16)
    gate_ref[...] = w_ref[:, cols["gm"][0]:cols["gf"][1]].astype(BF16)


def _pack_in_proj(w_in_layers, layer, b_in):
    _, d, n_cols = w_in_layers.shape
    cols = _in_proj_columns(d)
    tr = V7X_LANES
    w_main, w_gate = pl.pallas_call(
        _pack_w_kernel,
        grid=(d // tr,),
        in_specs=[pl.BlockSpec((None, tr, n_cols), lambda i: (layer, i, 0))],
        out_specs=(pl.BlockSpec((tr, _PACKED_COLS), lambda i: (i, 0)),
                   pl.BlockSpec((tr, 2 * d), lambda i: (i, 0))),
        out_shape=(jax.ShapeDtypeStruct((d, _PACKED_COLS), BF16),
                   jax.ShapeDtypeStruct((d, 2 * d), BF16)),
        compiler_params=_params(("parallel",)),
        name="pack_in_proj",
    )(w_in_layers)

    def take(names):
        return [b_in[cols[n][0]:cols[n][1]] for n in names]

    n_gate = 2 * M_HEADS + F_HEADS
    b_main = jnp.concatenate(take(("mqk", "mv", "mo", "fq", "fk", "fv", "mi", "mf", "ff"))
                             + [jnp.zeros((GATE_COLS - n_gate,), b_in.dtype)])
    b_gate = jnp.concatenate(take(("gm", "gf")))
    return w_main, b_main[None, :], w_gate, b_gate[None, :]


def _layer(h, depth, layer, w_in_layers, b_in, m_conv_w, m_conv_b, m_norm_g, w_bm, w_bf, w_o, ln1_g, ln1_b,
           w_router, b_router, w_gu, b_gu, w_dn, b_dn, ln2_g, ln2_b):
    bsz, s, d = h.shape
    t = bsz * s
    dn_alpha = (2.0 * depth) ** 0.25
    x2 = h.reshape(t, d)
    assert d == MQK_W and w_gu.shape[0] == N_EXPERTS, "kernels are written for this layer geometry"
    assert s % INPROJ_ROWS == 0 and s % MLSTM_CHUNK == 0 and s % FOX_Q_BLOCK == 0
    assert bsz % MLSTM_SEQS == 0
    assert t % MERGE_ROWS == 0 and t % ROUTE_ROWS == 0 and t % COMBINE_ROWS == 0
    assert t % (V7X_SC_WORKERS * SC_SCATTER_ROWS) == 0
    assert (t * TOP_K) % (V7X_SC_WORKERS * SC_GATHER_ROWS * SC_GATHER_WAYS) == 0

    w_main, b_main, w_gate, b_gate = _pack_in_proj(w_in_layers, layer, b_in)
    mq, mk, mv, mo, fq, fk, fv, gates = _inproj(x2, w_main, b_main, m_conv_w, m_conv_b[None, :], s)
    ccol, crel_k, crel_q = _fox_cumsum(gates, bsz, s)
    hm = _mlstm(mq, mk, mv, mo, gates, m_norm_g, bsz, s)
    hf = _fox_attention(fq, fk, fv, ccol, crel_k, crel_q, bsz, s)

    n_exp = w_router.shape[1]
    wr = jnp.zeros((d, V7X_LANES), F32).at[:, :n_exp].set(w_router)
    wr_hi = wr.astype(BF16)
    wr_lo = (wr - wr_hi.astype(F32)).astype(BF16)
    br = jnp.zeros((1, V7X_LANES), F32).at[0, :n_exp].set(b_router)
    h1, h1p, gate, tope, tile_counts = _merge(
        dn_alpha, hm, hf, x2, w_gate, b_gate, w_bm.astype(BF16), w_bf.astype(BF16), w_o.astype(BF16),
        ln1_g[None, :], ln1_b[None, :], wr_hi, wr_lo, br)

    n_blocks = -(-(t * TOP_K) // MOE_BLOCK) + N_EXPERTS
    dest, table = _routing(tile_counts, tope, n_blocks)
    dest_km = dest[:TOP_K].reshape(TOP_K * t)
    block_e, block_valid = table[:, 0], table[:, 1]
    xs = _sc_dispatch(dest_km, h1p, n_blocks * MOE_BLOCK)
    y_rows = _experts(block_e, block_valid, xs, w_gu, b_gu[:, None, :], w_dn, b_dn[:, None, :])
    yg = _sc_gather(dest_km, y_rows).reshape(TOP_K, t, d // 2)
    out = _combine(dn_alpha, h1, gate, ln2_g[None, :], ln2_b[None, :], yg)
    return out.reshape(bsz, s, d)


def kernel(x, w_in, b_in, m_conv_w, m_conv_b, m_norm_g, w_bm, w_bf, w_o, ln1_g, ln1_b,
           w_router, b_router, w_gu, b_gu, w_dn, b_dn, ln2_g, ln2_b):
    depth = w_in.shape[0]
    h = x
    for l in range(depth):
        h = _layer(h, depth, l, w_in, b_in[l], m_conv_w[l], m_conv_b[l], m_norm_g[l], w_bm[l], w_bf[l],
                   w_o[l], ln1_g[l], ln1_b[l], w_router[l], b_router[l], w_gu[l], b_gu[l], w_dn[l],
                   b_dn[l], ln2_g[l], ln2_b[l])
    return h
```

```python
import functools

import jax
import jax.numpy as jnp
from jax import lax
from jax.experimental import pallas as pl
from jax.experimental.pallas import tpu as pltpu
from jax.experimental.pallas import tpu_sc as plsc

F32 = jnp.float32
BF16 = jnp.bfloat16
I32 = jnp.int32
U32 = jnp.uint32

M_HEADS = 4
M_DQK = 128
M_DV = 128
CONV_W = 4
F_HEADS = 8
F_DH = 64
N_EXPERTS = 32
TOP_K = 4
SWIGLU_ALPHA = 1.702
SWIGLU_LIMIT = 7.0
LN_EPS = 1e-5

M_W = M_HEADS * M_DV
F_W = F_HEADS * F_DH
MQK_W = 2 * M_HEADS * M_DQK

V7X_LANES = 128
V7X_SUBLANES = 8
V7X_VMEM_BYTES = 64 * 1024 * 1024
VMEM_LIMIT_BYTES = (V7X_VMEM_BYTES * 3) // 4
EXPERT_VMEM_LIMIT_BYTES = (V7X_VMEM_BYTES * 7) // 8
V7X_SC_CORES = 2
V7X_SC_SUBCORES = 16
V7X_SC_WORKERS = V7X_SC_CORES * V7X_SC_SUBCORES

INPROJ_ROWS = 512
MLSTM_CHUNK = 256
MLSTM_SEQS = 4
FOX_Q_BLOCK = 512
FOX_K_BLOCK = 256
MERGE_ROWS = 512
MERGE_SUB_ROWS = 256
ROUTE_ROWS = 2048
ROUTE_SUB_ROWS = 256
MOE_BLOCK = 512
SC_SCATTER_ROWS = 128
SC_GATHER_ROWS = 64
SC_GATHER_WAYS = 2
COMBINE_ROWS = 1024

GATE_COLS = V7X_LANES
MI_LANE = 0
MF_LANE = M_HEADS
FF_LANE = 2 * M_HEADS


def _params(semantics):
    return pltpu.CompilerParams(dimension_semantics=semantics, vmem_limit_bytes=VMEM_LIMIT_BYTES)


def _log_sigmoid(x):
    return jnp.minimum(x, 0.0) - jnp.log1p(jnp.exp(-jnp.abs(x)))


def _sigmoid(x):
    return 0.5 * jnp.tanh(0.5 * x) + 0.5


def _dot(a, b):
    return jnp.dot(a, b, preferred_element_type=F32)


def _dot_nt(a, b):
    return lax.dot_general(a, b, (((1,), (1,)), ((), ())), preferred_element_type=F32)


def _split3(x):
    hi = x.astype(BF16)
    r1 = x - hi.astype(F32)
    mid = r1.astype(BF16)
    lo = (r1 - mid.astype(F32)).astype(BF16)
    return hi, mid, lo


def _dot_mask_f32(mask_bf16, x):
    hi, mid, lo = _split3(x)
    return (_dot(mask_bf16, lo) + _dot(mask_bf16, mid)) + _dot(mask_bf16, hi)


def _pack_bf16_pairs(x):
    half = x.shape[1] // 2
    bits = lax.bitcast_convert_type(x.astype(BF16).astype(F32), U32)
    return (bits[:, :half] >> 16) | bits[:, half:]


def _unpack_bf16_pairs(words):
    lo = lax.bitcast_convert_type(words << 16, F32)
    hi = lax.bitcast_convert_type(words & jnp.uint32(0xFFFF0000), F32)
    return jnp.concatenate([lo, hi], axis=1)


def _tril_mask(n, strict=False):
    r = lax.broadcasted_iota(I32, (n, n), 0)
    c = lax.broadcasted_iota(I32, (n, n), 1)
    return (r > c) if strict else (r >= c)


_OFF_MQK = 0
_OFF_MV = _OFF_MQK + MQK_W
_OFF_MO = _OFF_MV + M_W
_OFF_FQ = _OFF_MO + M_W
_OFF_FK = _OFF_FQ + F_W
_OFF_FV = _OFF_FK + F_W
_OFF_GATES = _OFF_FV + F_W
_PACKED_COLS = _OFF_GATES + GATE_COLS


def _inproj_kernel(tiles_per_seq, x_ref, w_ref, b_ref, cw_ref, cb_ref,
                   mq_ref, mk_ref, mv_ref, mo_ref, fq_ref, fk_ref, fv_ref, gates_ref, ext_ref):
    tm = x_ref.shape[0]
    pad = V7X_SUBLANES
    xb = x_ref[...].astype(BF16)

    def seg(lo, width):
        return _dot(xb, w_ref[:, lo:lo + width]) + b_ref[:, lo:lo + width]

    @pl.when(pl.program_id(0) % tiles_per_seq == 0)
    def _():
        ext_ref[0:pad, :] = jnp.zeros((pad, MQK_W), F32)

    def conv_silu(cols):
        ext_ref[pad:pad + tm, cols] = seg(_OFF_MQK + cols.start, cols.stop - cols.start)
        y = cb_ref[:, cols] + cw_ref[CONV_W - 1:CONV_W, cols] * ext_ref[pad:pad + tm, cols]
        for k in range(CONV_W - 1):
            shift = CONV_W - 1 - k
            y = y + cw_ref[k:k + 1, cols] * ext_ref[pad - shift:pad - shift + tm, cols]
        ext_ref[0:pad, cols] = ext_ref[tm:tm + pad, cols]
        return y * _sigmoid(y)

    half = MQK_W // 2
    group = half // 2
    mq_ref[:, :group] = conv_silu(slice(0, group)).astype(BF16)
    mv_ref[...] = seg(_OFF_MV, M_W).astype(BF16)
    mq_ref[:, group:] = conv_silu(slice(group, half)).astype(BF16)
    mo_ref[...] = seg(_OFF_MO, M_W)
    mk_ref[:, :group] = conv_silu(slice(half, half + group)) * (M_DQK ** -0.5)
    fq_ref[...] = (seg(_OFF_FQ, F_W) * (F_DH ** -0.5)).astype(BF16)
    mk_ref[:, group:] = conv_silu(slice(half + group, MQK_W)) * (M_DQK ** -0.5)
    fk_ref[...] = seg(_OFF_FK, F_W).astype(BF16)
    fv_ref[...] = seg(_OFF_FV, F_W).astype(BF16)
    gates_ref[...] = seg(_OFF_GATES, GATE_COLS)


def _inproj(x2, w_packed, b_packed, conv_w, conv_b, s):
    t, d = x2.shape
    tm = INPROJ_ROWS
    row = lambda i: (i, 0)
    const = lambda i: (0, 0)
    out_shapes = (
        jax.ShapeDtypeStruct((t, MQK_W // 2), BF16),
        jax.ShapeDtypeStruct((t, MQK_W // 2), F32),
        jax.ShapeDtypeStruct((t, M_W), BF16),
        jax.ShapeDtypeStruct((t, M_W), F32),
        jax.ShapeDtypeStruct((t, F_W), BF16),
        jax.ShapeDtypeStruct((t, F_W), BF16),
        jax.ShapeDtypeStruct((t, F_W), BF16),
        jax.ShapeDtypeStruct((t, GATE_COLS), F32),
    )
    return pl.pallas_call(
        functools.partial(_inproj_kernel, s // tm),
        grid=(t // tm,),
        in_specs=[
            pl.BlockSpec((tm, d), row),
            pl.BlockSpec((d, _PACKED_COLS), const),
            pl.BlockSpec((1, _PACKED_COLS), const),
            pl.BlockSpec((CONV_W, MQK_W), const),
            pl.BlockSpec((1, MQK_W), const),
        ],
        out_specs=tuple(pl.BlockSpec((tm, o.shape[1]), row) for o in out_shapes),
        out_shape=out_shapes,
        scratch_shapes=[pltpu.VMEM((tm + V7X_SUBLANES, MQK_W), F32)],
        compiler_params=_params(("arbitrary",)),
        name="inproj",
    )(x2, w_packed, b_packed, conv_w, conv_b)


def _fox_cumsum_kernel(g_ref, ccol_ref, crel_k_ref, crel_q_ref):
    s = g_ref.shape[0]
    cb = FOX_K_BLOCK
    per_q = FOX_Q_BLOCK // cb
    tri = _tril_mask(cb).astype(BF16)
    carry = jnp.zeros((1, GATE_COLS), F32)
    for j in range(s // cb):
        rows = slice(j * cb, (j + 1) * cb)
        if j % per_q == 0:
            q_carry = carry
        within = _dot_mask_f32(tri, _log_sigmoid(g_ref[rows, :]))
        crel_k_ref[rows, :] = within
        crel_q_ref[rows, :] = within + (carry - q_carry)
        ccol_ref[rows, :] = within + carry
        carry = carry + within[cb - 1:cb, :]


def _fox_cumsum(gates, bsz, s):
    t = gates.shape[0]
    spec = pl.BlockSpec((s, GATE_COLS), lambda b: (b, 0))
    shape = jax.ShapeDtypeStruct((t, GATE_COLS), F32)
    return pl.pallas_call(
        _fox_cumsum_kernel,
        grid=(bsz,),
        in_specs=[spec],
        out_specs=(spec, spec, spec),
        out_shape=(shape, shape, shape),
        compiler_params=_params(("parallel",)),
        name="fox_cumsum",
    )(gates)


def _mlstm_kernel(mq_ref, mk_ref, mv_ref, mo_ref, gates_ref, ng_ref, hm_ref, state_ref, m_ref):
    @pl.when(pl.program_id(1) == 0)
    def _():
        state_ref[...] = jnp.zeros(state_ref.shape, F32)
        m_ref[...] = jnp.zeros(m_ref.shape, F32)

    seqs = range(mq_ref.shape[0])
    states = [[state_ref[bb, h] for h in range(M_HEADS)] for bb in seqs]
    maxes = [[m_ref[bb, h][0:1, 0:1] for h in range(M_HEADS)] for bb in seqs]
    results = [_mlstm_chunk(mq_ref.at[bb], mk_ref.at[bb], mv_ref.at[bb], mo_ref.at[bb], gates_ref.at[bb],
                            ng_ref, states[bb], maxes[bb]) for bb in seqs]
    for bb in seqs:
        for h, (out_h, state_h, m_h) in enumerate(results[bb]):
            hm_ref[bb, :, h * M_DV:(h + 1) * M_DV] = out_h
            state_ref[bb, h] = state_h
            m_ref[bb, h] = jnp.broadcast_to(m_h, m_ref.shape[2:])


def _mlstm_chunk(mq_ref, mk_ref, mv_ref, mo_ref, gates_ref, ng_ref, states, maxes):
    L = MLSTM_CHUNK
    reps = L // V7X_LANES
    results = []
    gates = gates_ref[...]
    bfull = _dot_mask_f32(_tril_mask(L).astype(BF16), _log_sigmoid(gates))
    b_rows = bfull.T
    z_all = gates - pltpu.roll(bfull, shift=V7X_LANES - (MF_LANE - MI_LANE), axis=1)
    visible = (lax.broadcasted_iota(I32, (L, L), 0) <= lax.broadcasted_iota(I32, (L, L), 1))
    ones_rows = (lax.broadcasted_iota(I32, (M_DV, L), 0) == 0).astype(BF16)

    for h in range(M_HEADS):
        b_row = b_rows[MF_LANE + h:MF_LANE + h + 1, :]
        g_tot = b_row[:, L - 1:L]
        m_prev = maxes[h]
        z = jnp.broadcast_to(z_all[:, MI_LANE + h:MI_LANE + h + 1], (L, V7X_LANES))

        q_h = mq_ref[:, h * M_DQK:(h + 1) * M_DQK]
        k_f = mk_ref[:, h * M_DQK:(h + 1) * M_DQK]
        k_h = k_f.astype(BF16)
        v_t = mv_ref[:, h * M_DV:(h + 1) * M_DV].astype(F32).T.astype(BF16)
        cn_t = states[h]

        dlog = jnp.where(visible, b_row + jnp.tile(z, (1, reps)), -jnp.inf)
        inter_log = b_row + m_prev
        m_t = jnp.maximum(inter_log, jnp.max(dlog, axis=0, keepdims=True))
        w_inter = jnp.exp(inter_log - m_t)
        qkw = _dot_nt(k_h, q_h) * jnp.exp(dlog - m_t)
        qc = _dot_nt(cn_t.astype(BF16), q_h)
        num = w_inter * qc[:M_DV, :] + _dot(v_t, qkw.astype(BF16))
        den = w_inter * qc[M_DV:M_DV + 1, :] + jnp.sum(qkw, axis=0, keepdims=True)
        hh = num / jnp.maximum(jnp.abs(den), jnp.exp(-m_t))

        mu = jnp.mean(hh, axis=0, keepdims=True)
        dv = hh - mu
        var = jnp.mean(dv * dv, axis=0, keepdims=True)
        hn = (dv * lax.rsqrt(var + LN_EPS)) * jnp.tile(ng_ref[h * M_DV:(h + 1) * M_DV, :], (1, reps))
        out_h = (_sigmoid(mo_ref[:, h * M_DV:(h + 1) * M_DV]) * hn.T).astype(BF16)

        a = g_tot + z
        m_new = jnp.maximum(g_tot + m_prev, jnp.max(a, axis=0, keepdims=True)[:, 0:1])
        decay = jnp.exp(g_tot + m_prev - m_new)
        kw = (k_f * jnp.exp(a - m_new)).astype(BF16)
        v_aug = jnp.concatenate([v_t, ones_rows], axis=0)
        results.append((out_h, decay * cn_t + _dot(v_aug, kw), m_new))
    return results


def _mlstm(mq, mk, mv, mo, gates, norm_g, bsz, s):
    t = mq.shape[0]
    L = MLSTM_CHUNK
    nb = MLSTM_SEQS
    seq = lambda a: a.reshape(bsz, s, a.shape[1])
    blk = lambda width: pl.BlockSpec((nb, L, width), lambda g, n: (g, n, 0))
    const = lambda g, n: (0, 0)
    hm = pl.pallas_call(
        _mlstm_kernel,
        grid=(bsz // nb, s // L),
        in_specs=[
            blk(MQK_W // 2), blk(MQK_W // 2), blk(M_W), blk(M_W), blk(GATE_COLS),
            pl.BlockSpec((M_W, V7X_LANES), const),
        ],
        out_specs=blk(M_W),
        out_shape=jax.ShapeDtypeStruct((bsz, s, M_W), BF16),
        scratch_shapes=[
            pltpu.VMEM((nb, M_HEADS, 2 * M_DV, M_DQK), F32),
            pltpu.VMEM((nb, M_HEADS, V7X_SUBLANES, V7X_LANES), F32),
        ],
        compiler_params=_params(("parallel", "arbitrary")),
        name="mlstm",
    )(seq(mq), seq(mk), seq(mv), seq(mo), seq(gates),
      jnp.broadcast_to(norm_g[:, None], (M_W, V7X_LANES)))
    return hm.reshape(t, M_W)


_FOX_FEATURES = 3


def _fox_operands(x, c_tile, c_lane, key_side):
    rows = x.shape[0]
    nf = _FOX_FEATURES
    first = 0 if key_side else nf
    src = lax.broadcasted_iota(I32, (V7X_LANES, V7X_LANES), 0) - c_lane
    dst = lax.broadcasted_iota(I32, (V7X_LANES, V7X_LANES), 1)
    feats = jnp.zeros((rows, V7X_LANES), F32)
    for n, part in enumerate(_split3(c_tile)):
        place = jnp.logical_or(jnp.logical_and(src == 0, dst == F_DH + first + n),
                               jnp.logical_and(src == 1, dst == first + n))
        feats = feats + _dot(part, place.astype(BF16))
    lane = lax.broadcasted_iota(I32, (rows, V7X_LANES), 1)
    within = lane % F_DH
    const_lanes = jnp.logical_and(within >= nf - first, within < 2 * nf - first)
    feats = jnp.where(const_lanes, 1.0 if key_side else -1.0, feats).astype(BF16)
    low = lane < F_DH
    return jnp.where(low, x, feats), jnp.where(low, feats, x)


def _fox_kernel(q_ref, k_ref, v_ref, ccol_ref, crel_k_ref, crel_q_ref, o_ref, qaug_ref, kaug_ref, vt_ref,
                st_a0, st_a1, st_b0, st_b1, pe_a0, pe_a1, pe_b0, pe_b1, acc0, acc1):
    blk = FOX_K_BLOCK
    tq = FOX_Q_BLOCK
    assert tq == 2 * blk
    strips = tq // V7X_LANES
    st_a_refs, st_b_refs = (st_a0, st_a1), (st_b0, st_b1)
    pe_a_refs, pe_b_refs = (pe_a0, pe_a1), (pe_b0, pe_b1)
    acc_refs = (acc0, acc1)
    p = pl.program_id(1)
    i = pl.program_id(2)

    @pl.when(i == 0)
    def _():
        c_lane = FF_LANE + 2 * p
        qaug_ref[0], qaug_ref[1] = _fox_operands(q_ref[...], crel_q_ref[...], c_lane, False)
        kaug_ref[0], kaug_ref[1] = _fox_operands(k_ref[...], crel_k_ref[...], c_lane, True)
        v_t = v_ref[...].astype(F32).T
        for j in range(vt_ref.shape[0]):
            vt_ref[j] = v_t[:, j * blk:(j + 1) * blk].astype(BF16)

    q_start = pl.multiple_of(i * tq, tq)
    q_heads = [qaug_ref[hh, pl.ds(q_start, tq), :] for hh in range(2)]
    key_row = lax.broadcasted_iota(I32, (blk, V7X_LANES), 0)
    query_col = lax.broadcasted_iota(I32, (blk, V7X_LANES), 1)
    last_chunk = 2 * i + 1
    head_lane = lax.broadcasted_iota(I32, (1, GATE_COLS), 1) - (FF_LANE + 2 * p)

    def c_before(position, hh):
        row = ccol_ref[pl.ds(jnp.maximum(position - 1, 0), 1), :]
        keep = jnp.logical_and(head_lane == hh, position > 0)
        return jnp.sum(jnp.where(keep, row, 0.0), axis=-1, keepdims=True)

    c_query0 = [c_before(q_start, hh) for hh in range(2)]

    def scores(j, hh):
        start = pl.multiple_of(j * blk, blk)
        return _dot_nt(kaug_ref[hh, pl.ds(start, blk), :], q_heads[hh])

    def softmax_update(st_ref, pe_ref, m_old, l_old, base, key_minus_query=None):
        alphas, ms, ls = [], [], []
        for c in range(strips):
            cols = slice(c * V7X_LANES, (c + 1) * V7X_LANES)
            gap = None if key_minus_query is None else key_minus_query - c * V7X_LANES
            if gap is not None and gap - (V7X_LANES - 1) > 0:
                pe_ref[:, cols] = jnp.zeros((blk, V7X_LANES), BF16)
                alphas.append(jnp.ones((1, V7X_LANES), F32))
                ms.append(m_old[:, cols])
                ls.append(l_old[:, cols])
                continue
            st = st_ref[:, cols]
            if gap is not None and gap + (blk - 1) > 0:
                st = jnp.where(key_row + gap <= query_col, st, -jnp.inf)
            m_new = jnp.maximum(m_old[:, cols], jnp.max(st, axis=0, keepdims=True) + base)
            alpha = jnp.exp(m_old[:, cols] - m_new)
            pe = jnp.exp(st - (m_new - base))
            pe_ref[:, cols] = pe.astype(BF16)
            alphas.append(alpha)
            ms.append(m_new)
            ls.append(alpha * l_old[:, cols] + jnp.sum(pe, axis=0, keepdims=True))
        cat = lambda parts: jnp.concatenate(parts, axis=1)
        return cat(alphas), cat(ms), cat(ls)

    def pair(mi, carry, diagonal=False):
        a = 2 * mi
        b = a + 1
        v_prev = vt_ref[jnp.maximum(a - 1, 0)]
        v_a = vt_ref[a]
        partial = []
        for hh in range(2):
            alpha_prev = carry[hh][0]
            partial.append(alpha_prev * acc_refs[hh][...] + _dot(v_prev, pe_b_refs[hh][...]))
            st_b_refs[hh][...] = scores(b, hh)
        stats = []
        for hh in range(2):
            _, m_old, l_old = carry[hh]
            stats.append(softmax_update(st_a_refs[hh], pe_a_refs[hh], m_old, l_old,
                                        c_query0[hh] - c_before(a * blk, hh), 0 if diagonal else None))
        for hh in range(2):
            alpha_a = stats[hh][0]
            acc_refs[hh][...] = alpha_a * partial[hh] + _dot(v_a, pe_a_refs[hh][...])
            if not diagonal:
                st_a_refs[hh][...] = scores(a + 2, hh)
        return tuple(softmax_update(st_b_refs[hh], pe_b_refs[hh], stats[hh][1], stats[hh][2],
                                    c_query0[hh] - c_before(b * blk, hh), blk if diagonal else None)
                     for hh in range(2))

    for hh in range(2):
        st_a_refs[hh][...] = scores(0, hh)
        pe_b_refs[hh][...] = jnp.zeros((blk, tq), BF16)
        acc_refs[hh][...] = jnp.zeros((V7X_LANES, tq), F32)
    init = tuple((jnp.ones((1, tq), F32), jnp.full((1, tq), -jnp.inf, F32), jnp.zeros((1, tq), F32))
                 for _ in range(2))
    final = pair(i, lax.fori_loop(0, i, pair, init), diagonal=True)
    v_last = vt_ref[last_chunk]
    outs = []
    for hh in range(2):
        alpha, _, l_fin = final[hh]
        outs.append((alpha * acc_refs[hh][...] + _dot(v_last, pe_b_refs[hh][...])) / l_fin)
    row = lax.broadcasted_iota(I32, (V7X_LANES, tq), 0)
    o_t = jnp.where(row < F_DH, outs[0], outs[1])
    o_ref[...] = o_t.T.astype(BF16)


def _fox_attention(fq, fk, fv, ccol, crel_k, crel_q, bsz, s):
    t = fq.shape[0]
    blk = FOX_K_BLOCK
    tq = FOX_Q_BLOCK
    nq = s // tq
    pairs = F_HEADS // 2
    qmap = lambda b, p, i: (b * nq + i, p)
    kvmap = lambda b, p, i: (b, p)
    return pl.pallas_call(
        _fox_kernel,
        grid=(bsz, pairs, nq),
        in_specs=[
            pl.BlockSpec((s, V7X_LANES), kvmap),
            pl.BlockSpec((s, V7X_LANES), kvmap),
            pl.BlockSpec((s, V7X_LANES), kvmap),
            pl.BlockSpec((s, GATE_COLS), lambda b, p, i: (b, 0)),
            pl.BlockSpec((s, GATE_COLS), lambda b, p, i: (b, 0)),
            pl.BlockSpec((s, GATE_COLS), lambda b, p, i: (b, 0)),
        ],
        out_specs=pl.BlockSpec((tq, V7X_LANES), qmap),
        out_shape=jax.ShapeDtypeStruct((t, F_W), BF16),
        scratch_shapes=[
            pltpu.VMEM((2, s, V7X_LANES), BF16),
            pltpu.VMEM((2, s, V7X_LANES), BF16),
            pltpu.VMEM((s // blk, V7X_LANES, blk), BF16),
        ] + [pltpu.VMEM((blk, tq), F32)] * 4 + [pltpu.VMEM((blk, tq), BF16)] * 4
          + [pltpu.VMEM((V7X_LANES, tq), F32)] * 2,
        compiler_params=_params(("parallel", "parallel", "arbitrary")),
        name="fox_attention",
    )(fq, fk, fv, ccol, crel_k, crel_q)


def _layer_norm_rows(r, g, b):
    mu = jnp.mean(r, axis=-1, keepdims=True)
    d = r - mu
    var = jnp.mean(d * d, axis=-1, keepdims=True)
    return (d * lax.rsqrt(var + LN_EPS)) * g + b


def _merge_kernel(dn_alpha, hm_ref, hf_ref, x_ref, wg_ref, bg_ref, wbm_ref, wbf_ref, wo_ref,
                  g_ref, b_ref, wrh_ref, wrl_ref, br_ref, h1_ref, h1p_ref, gate_ref, tope_ref, cnt_ref,
                  resid_ref):
    i = pl.program_id(0)
    slot = i % 2

    @pl.when(i == 0)
    def _():
        resid_ref[1] = jnp.zeros(resid_ref.shape[1:], F32)

    subs = [slice(n * MERGE_SUB_ROWS, (n + 1) * MERGE_SUB_ROWS)
            for n in range(x_ref.shape[0] // MERGE_SUB_ROWS)]
    counts = jnp.zeros((1, V7X_LANES), F32)
    for rows in subs:
        counts = counts + _merge_tail(resid_ref[1 - slot, rows, :], rows, g_ref, b_ref, wrh_ref, wrl_ref,
                                      br_ref, h1_ref, h1p_ref, gate_ref, tope_ref)
    sub = lax.broadcasted_iota(I32, cnt_ref.shape, 0)
    cnt_ref[...] = jnp.where(sub == 0, counts, 0.0)

    for rows in subs:
        x = x_ref[rows, :]
        d = x.shape[1]
        gmf = _dot(x.astype(BF16), wg_ref[...]) + bg_ref[...]
        ym = _dot(hm_ref[rows, :], wbm_ref[...])
        yf = _dot(hf_ref[rows, :], wbf_ref[...])
        y = _sigmoid(gmf[:, :d]) * ym + _sigmoid(gmf[:, d:]) * yf
        resid_ref[slot, rows, :] = dn_alpha * x + _dot(y.astype(BF16), wo_ref[...])


def _merge_tail(resid, rows, g_ref, b_ref, wrh_ref, wrl_ref, br_ref, h1_ref, h1p_ref, gate_ref, tope_ref):
    h1 = _layer_norm_rows(resid, g_ref[...], b_ref[...])
    h1_ref[rows, :] = h1

    h1p_ref[rows, :] = _pack_bf16_pairs(h1)
    hb = h1.astype(BF16)

    lo = (h1 - hb.astype(F32)).astype(BF16)
    logits = (_dot(lo, wrh_ref[...]) + _dot(hb, wrl_ref[...])) + _dot(hb, wrh_ref[...]) + br_ref[...]
    tm = logits.shape[0]
    lane = lax.broadcasted_iota(I32, (tm, V7X_LANES), 1)
    vals = jnp.where(lane < N_EXPERTS, logits, -jnp.inf)
    top_v, top_i = [], []
    for _ in range(TOP_K):
        mx = jnp.max(vals, axis=-1, keepdims=True)
        idx = jnp.min(jnp.where(vals == mx, lane, V7X_LANES), axis=-1, keepdims=True)
        top_v.append(mx)
        top_i.append(idx)
        vals = jnp.where(lane == idx, -jnp.inf, vals)
    ex = [jnp.exp(v - top_v[0]) for v in top_v]
    tot = ex[0]
    for e in ex[1:]:
        tot = tot + e
    gate = jnp.zeros((tm, V7X_LANES), F32)
    tope = jnp.zeros((tm, V7X_LANES), I32)
    member = jnp.zeros((tm, V7X_LANES), F32)
    for k in range(TOP_K):
        gate = jnp.where(lane == k, ex[k] / tot, gate)
        tope = jnp.where(lane == k, top_i[k], tope)
        member = member + (lane == top_i[k]).astype(F32)
    gate_ref[rows, :] = gate
    tope_ref[rows, :] = tope
    return jnp.sum(member, axis=0, keepdims=True)


def _merge(dn_alpha, hm, hf, x2, wg, bg, wbm, wbf, wo, ln_g, ln_b, wr_hi, wr_lo, br):
    t, d = x2.shape
    tm = MERGE_ROWS
    nt = t // tm
    row = lambda i: (jnp.minimum(i, nt - 1), 0)
    out_row = lambda i: (jnp.maximum(i - 1, 0), 0)
    const = lambda i: (0, 0)
    full = lambda a: pl.BlockSpec(a.shape, const)
    return pl.pallas_call(
        functools.partial(_merge_kernel, dn_alpha),
        grid=(nt + 1,),
        in_specs=[
            pl.BlockSpec((tm, M_W), row),
            pl.BlockSpec((tm, F_W), row),
            pl.BlockSpec((tm, d), row),
            full(wg), full(bg), full(wbm), full(wbf), full(wo), full(ln_g), full(ln_b),
            full(wr_hi), full(wr_lo), full(br),
        ],
        out_specs=(
            pl.BlockSpec((tm, d), out_row),
            pl.BlockSpec((tm, d // 2), out_row),
            pl.BlockSpec((tm, V7X_LANES), out_row),
            pl.BlockSpec((tm, V7X_LANES), out_row),
            pl.BlockSpec((V7X_SUBLANES, V7X_LANES), out_row),
        ),
        out_shape=(
            jax.ShapeDtypeStruct((t, d), F32),
            jax.ShapeDtypeStruct((t, d // 2), U32),
            jax.ShapeDtypeStruct((t, V7X_LANES), F32),
            jax.ShapeDtypeStruct((t, V7X_LANES), I32),
            jax.ShapeDtypeStruct((t // tm * V7X_SUBLANES, V7X_LANES), F32),
        ),
        scratch_shapes=[pltpu.VMEM((2, tm, d), F32)],
        compiler_params=_params(("arbitrary",)),
        name="merge_ln1_router",
    )(hm, hf, x2, wg, bg, wbm, wbf, wo, ln_g, ln_b, wr_hi, wr_lo, br)


def _lane_cumsum(x):
    lane = lax.broadcasted_iota(I32, x.shape, 1)
    d = 1
    while d < V7X_LANES:
        x = x + jnp.where(lane >= d, pltpu.roll(x, shift=d, axis=1), 0.0)
        d *= 2
    return x


def _routing_kernel(cnt_ref, tope_ref, dest_ref, table_ref, run_ref, start_ref):
    sb = ROUTE_SUB_ROWS

    @pl.when(pl.program_id(0) == 0)
    def _():
        total = jnp.sum(cnt_ref[...], axis=0, keepdims=True)
        counts = jnp.broadcast_to(total, (V7X_SUBLANES, V7X_LANES))
        padded = jnp.ceil(counts * (1.0 / MOE_BLOCK)) * MOE_BLOCK
        pad_end = _lane_cumsum(padded)
        pad_start = pad_end - padded
        start_ref[...] = pad_start
        run_ref[...] = jnp.zeros(run_ref.shape, F32)
        nb = table_ref.shape[0]
        blk = lax.broadcasted_iota(I32, (nb, V7X_LANES), 0).astype(F32) * MOE_BLOCK
        ln = lax.broadcasted_iota(I32, (nb, V7X_LANES), 1)
        done = jnp.logical_and(pad_end[0:1, :] <= blk, ln < N_EXPERTS)
        be = jnp.minimum(jnp.sum(done.astype(F32), axis=-1, keepdims=True), N_EXPERTS - 1.0)
        onehot = ln == be.astype(I32)
        cnt_e = jnp.sum(jnp.where(onehot, counts[0:1, :], 0.0), axis=-1, keepdims=True)
        start_e = jnp.sum(jnp.where(onehot, pad_start[0:1, :], 0.0), axis=-1, keepdims=True)
        valid = jnp.clip(cnt_e - (blk[:, 0:1] - start_e), 0.0, float(MOE_BLOCK))
        table_ref[...] = jnp.where(ln == 0, be.astype(I32),
                                   jnp.where(ln == 1, valid.astype(I32), 0))

    earlier = _tril_mask(sb, strict=True).astype(BF16)
    lane = lax.broadcasted_iota(I32, (sb, V7X_LANES), 1)
    for j in range(tope_ref.shape[0] // sb):
        tope = tope_ref[j * sb:(j + 1) * sb, :]
        hit = [lane == tope[:, k:k + 1] for k in range(TOP_K)]
        member = jnp.zeros((sb, V7X_LANES), F32)
        for k in range(TOP_K):
            member = member + hit[k].astype(F32)
        base = _dot(earlier, member.astype(BF16)) + (run_ref[0:1, :] + start_ref[0:1, :])
        dest = jnp.zeros((sb, V7X_LANES), F32)
        for k in range(TOP_K):
            dk = jnp.sum(jnp.where(hit[k], base, 0.0), axis=-1, keepdims=True)
            dest = jnp.where(lane == k, dk, dest)
        dest_ref[:, j * sb:(j + 1) * sb] = dest.T[0:V7X_SUBLANES, :].astype(I32)
        run_ref[...] = run_ref[...] + jnp.sum(member, axis=0, keepdims=True)


def _routing(tile_counts, tope, n_blocks):
    t = tope.shape[0]
    tr = ROUTE_ROWS
    return pl.pallas_call(
        _routing_kernel,
        grid=(t // tr,),
        in_specs=[pl.BlockSpec(tile_counts.shape, lambda i: (0, 0)),
                  pl.BlockSpec((tr, V7X_LANES), lambda i: (i, 0))],
        out_specs=(
            pl.BlockSpec((V7X_SUBLANES, tr), lambda i: (0, i)),
            pl.BlockSpec((n_blocks, V7X_LANES), lambda i: (0, 0)),
        ),
        out_shape=(
            jax.ShapeDtypeStruct((V7X_SUBLANES, t), I32),
            jax.ShapeDtypeStruct((n_blocks, V7X_LANES), I32),
        ),
        scratch_shapes=[
            pltpu.VMEM((V7X_SUBLANES, V7X_LANES), F32),
            pltpu.VMEM((V7X_SUBLANES, V7X_LANES), F32),
        ],
        compiler_params=_params(("arbitrary",)),
        name="routing",
    )(tile_counts, tope)


def _sc_worker_id():
    return lax.axis_index("s") * V7X_SC_CORES + lax.axis_index("c")


def _sc_mesh():
    return plsc.VectorSubcoreMesh(core_axis_name="c", subcore_axis_name="s",
                                  num_cores=V7X_SC_CORES, num_subcores=V7X_SC_SUBCORES)


def _sc_dispatch(dest_km, h1p, n_rows):
    t, w = h1p.shape
    per_worker = t // V7X_SC_WORKERS
    ch = SC_SCATTER_ROWS

    @functools.partial(
        pl.kernel, mesh=_sc_mesh(),
        out_type=jax.ShapeDtypeStruct((n_rows, w), h1p.dtype),
        scratch_types=[pltpu.VMEM((ch, w), h1p.dtype)]
        + [pltpu.VMEM((ch,), I32)] * TOP_K + [pltpu.SemaphoreType.DMA] * TOP_K,
        name="sc_dispatch",
    )
    def scatter_rows(dest_hbm, h1p_hbm, xs_hbm, rows_v, *idx_and_sems):
        idx_refs, sems = idx_and_sems[:TOP_K], idx_and_sems[TOP_K:]
        first = _sc_worker_id() * per_worker

        @pl.loop(0, per_worker // ch)
        def _(j):
            base = first + j * ch
            pltpu.sync_copy(h1p_hbm.at[pl.ds(base, ch)], rows_v)
            copies = []
            for k in range(TOP_K):
                pltpu.sync_copy(dest_hbm.at[pl.ds(k * t + base, ch)], idx_refs[k])
                copies.append(pltpu.async_copy(rows_v, xs_hbm.at[idx_refs[k]], sems[k]))
            for copy in copies:
                copy.wait()

    return scatter_rows(dest_km, h1p)


def _sc_gather(dest_km, y_rows):
    n = dest_km.shape[0]
    w = y_rows.shape[1]
    per_worker = n // V7X_SC_WORKERS
    ch = SC_GATHER_ROWS
    ways = SC_GATHER_WAYS

    @functools.partial(
        pl.kernel, mesh=_sc_mesh(),
        out_type=jax.ShapeDtypeStruct((n, w), y_rows.dtype),
        scratch_types=[pltpu.VMEM((ch,), I32)] * ways + [pltpu.VMEM((ch, w), y_rows.dtype)] * ways
        + [pltpu.SemaphoreType.DMA] * (2 * ways),
        name="sc_gather",
    )
    def gather_rows(dest_hbm, y_hbm, out_hbm, *scratch):
        idx_refs, row_refs = scratch[:ways], scratch[ways:2 * ways]
        gather_sems, store_sems = scratch[2 * ways:3 * ways], scratch[3 * ways:]
        first = _sc_worker_id() * per_worker

        @pl.loop(0, per_worker // (ch * ways))
        def _(j):
            bases = [first + (j * ways + u) * ch for u in range(ways)]
            gathers = []
            for u in range(ways):
                pltpu.sync_copy(dest_hbm.at[pl.ds(bases[u], ch)], idx_refs[u])
                gathers.append(pltpu.async_copy(y_hbm.at[idx_refs[u]], row_refs[u], gather_sems[u]))
            stores = []
            for u in range(ways):
                gathers[u].wait()
                stores.append(pltpu.async_copy(row_refs[u], out_hbm.at[pl.ds(bases[u], ch)], store_sems[u]))
            for store in stores:
                store.wait()

    return gather_rows(dest_km, y_rows)


def _expert_kernel(be_ref, nv_ref, xs_ref, wgu_f32_ref, bgu_ref, wdn_f32_ref, bdn_ref, y_ref,
                   wgu_slots, wdn_slots, slot_ref):
    i = pl.program_id(0)
    n_blocks = pl.num_programs(0) - 1
    nv = jnp.where(i > 0, nv_ref[jnp.maximum(i - 1, 0)], 0)
    half = MOE_BLOCK // 2

    @pl.when(i == 0)
    def _():
        slot_ref[0] = 0

    slot = slot_ref[0]
    wgu_ref = wgu_slots.at[slot]
    wdn_ref = wdn_slots.at[slot]

    def ffn(rows):
        x = _unpack_bf16_pairs(xs_ref[rows, :])
        rowid = rows.start + lax.broadcasted_iota(I32, x.shape, 0)
        x = jnp.where(rowid < nv, x, 0.0).astype(BF16)
        gu = _dot(x, wgu_ref[...]) + bgu_ref[...]
        f = gu.shape[1] // 2
        glu = jnp.minimum(gu[:, :f], SWIGLU_LIMIT)
        lin = jnp.clip(gu[:, f:], -SWIGLU_LIMIT, SWIGLU_LIMIT)
        act = glu * _sigmoid(SWIGLU_ALPHA * glu) * (lin + 1.0)
        y_ref[rows, :] = _pack_bf16_pairs(_dot(act.astype(BF16), wdn_ref[...]) + bdn_ref[...])

    @pl.when(nv == 0)
    def _():
        y_ref[...] = jnp.zeros(y_ref.shape, U32)

    @pl.when(jnp.logical_and(nv > 0, nv <= half))
    def _():
        ffn(slice(0, half))
        y_ref[half:, :] = jnp.zeros((MOE_BLOCK - half, y_ref.shape[1]), U32)

    @pl.when(nv > half)
    def _():
        ffn(slice(0, half))
        ffn(slice(half, MOE_BLOCK))

    new_expert = jnp.logical_or(i == 0, be_ref[jnp.minimum(i, n_blocks - 1)] != be_ref[jnp.maximum(i - 1, 0)])

    @pl.when(jnp.logical_and(i < n_blocks, new_expert))
    def _():
        wgu_slots[1 - slot] = wgu_f32_ref[...].astype(BF16)
        wdn_slots[1 - slot] = wdn_f32_ref[...].astype(BF16)
        slot_ref[0] = 1 - slot


def _experts(block_e, block_valid, xs, wgu, bgu, wdn, bdn):
    n_rows, w = xs.shape
    e, d, f2 = wgu.shape
    n_blocks = n_rows // MOE_BLOCK
    ahead = lambda i, be: be[jnp.minimum(i, n_blocks - 1)]
    behind = lambda i: jnp.maximum(i - 1, 0)
    grid_spec = pltpu.PrefetchScalarGridSpec(
        num_scalar_prefetch=2,
        grid=(n_blocks + 1,),
        in_specs=[
            pl.BlockSpec((MOE_BLOCK, w), lambda i, be, nv: (behind(i), 0)),
            pl.BlockSpec((None, d, f2), lambda i, be, nv: (ahead(i, be), 0, 0)),
            pl.BlockSpec((None, 1, f2), lambda i, be, nv: (be[behind(i)], 0, 0)),
            pl.BlockSpec((None, f2 // 2, d), lambda i, be, nv: (ahead(i, be), 0, 0)),
            pl.BlockSpec((None, 1, d), lambda i, be, nv: (be[behind(i)], 0, 0)),
        ],
        out_specs=pl.BlockSpec((MOE_BLOCK, d // 2), lambda i, be, nv: (behind(i), 0)),
        scratch_shapes=[pltpu.VMEM((2, d, f2), BF16), pltpu.VMEM((2, f2 // 2, d), BF16),
                        pltpu.SMEM((1,), I32)],
    )
    return pl.pallas_call(
        _expert_kernel,
        grid_spec=grid_spec,
        out_shape=jax.ShapeDtypeStruct((n_rows, d // 2), U32),
        compiler_params=pltpu.CompilerParams(dimension_semantics=("arbitrary",),
                                             vmem_limit_bytes=EXPERT_VMEM_LIMIT_BYTES),
        name="experts",
    )(block_e, block_valid, xs, wgu, bgu, wdn, bdn)


def _combine_kernel(dn_alpha, h1_ref, gate_ref, g_ref, b_ref, yg_ref, o_ref):
    gate = gate_ref[...]
    ffn = gate[:, 0:1] * _unpack_bf16_pairs(yg_ref[0])
    for k in range(1, TOP_K):
        ffn = ffn + gate[:, k:k + 1] * _unpack_bf16_pairs(yg_ref[k])
    o_ref[...] = _layer_norm_rows(dn_alpha * h1_ref[...] + ffn, g_ref[...], b_ref[...])


def _combine(dn_alpha, h1, gate, ln_g, ln_b, yg):
    t, d = h1.shape
    tc = COMBINE_ROWS
    row = lambda i: (i, 0)
    const = lambda i: (0, 0)
    return pl.pallas_call(
        functools.partial(_combine_kernel, dn_alpha),
        grid=(t // tc,),
        in_specs=[
            pl.BlockSpec((tc, d), row),
            pl.BlockSpec((tc, V7X_LANES), row),
            pl.BlockSpec((1, d), const),
            pl.BlockSpec((1, d), const),
            pl.BlockSpec((TOP_K, tc, d // 2), lambda i: (0, i, 0)),
        ],
        out_specs=pl.BlockSpec((tc, d), row),
        out_shape=jax.ShapeDtypeStruct((t, d), F32),
        compiler_params=_params(("parallel",)),
        name="combine_ln2",
    )(h1, gate, ln_g, ln_b, yg)


def _in_proj_columns(d):
    o = 0
    cols = {}
    for name, width in (("mqk", MQK_W), ("mv", M_W), ("mo", M_W), ("mi", M_HEADS), ("mf", M_HEADS),
                        ("fq", F_W), ("fk", F_W), ("fv", F_W), ("ff", F_HEADS), ("gm", d), ("gf", d)):
        cols[name] = (o, o + width)
        o += width
    return cols


def _pack_w_kernel(w_ref, main_ref, gate_ref):
    d = gate_ref.shape[1] // 2
    cols = _in_proj_columns(d)
    out = 0
    for name in ("mqk", "mv", "mo", "fq", "fk", "fv"):
        lo, hi = cols[name]
        main_ref[:, out:out + hi - lo] = w_ref[:, lo:hi].astype(BF16)
        out += hi - lo
    lane = lax.broadcasted_iota(I32, (w_ref.shape[0], GATE_COLS), 1)
    mi_lo, ff_lo = cols["mi"][0], cols["ff"][0] - 2 * M_HEADS
    assert mi_lo % V7X_LANES == 0 and ff_lo % V7X_LANES == 0 and cols["mf"][0] == mi_lo + M_HEADS
    gates = jnp.where(lane < 2 * M_HEADS, w_ref[:, mi_lo:mi_lo + GATE_COLS],
                      jnp.where(lane < 2 * M_HEADS + F_HEADS, w_ref[:, ff_lo:ff_lo + GATE_COLS], 0.0))
    main_ref[:, out:out + GATE_COLS] = gates.astype(BF16)
    gate_ref[...] = w_ref[:, cols["gm"][0]:cols["gf"][1]].astype(BF16)


def _pack_in_proj(w_in_layers, layer, b_in):
    _, d, n_cols = w_in_layers.shape
    cols = _in_proj_columns(d)
    tr = V7X_LANES
    w_main, w_gate = pl.pallas_call(
        _pack_w_kernel,
        grid=(d // tr,),
        in_specs=[pl.BlockSpec((None, tr, n_cols), lambda i: (layer, i, 0))],
        out_specs=(pl.BlockSpec((tr, _PACKED_COLS), lambda i: (i, 0)),
                   pl.BlockSpec((tr, 2 * d), lambda i: (i, 0))),
        out_shape=(jax.ShapeDtypeStruct((d, _PACKED_COLS), BF16),
                   jax.ShapeDtypeStruct((d, 2 * d), BF16)),
        compiler_params=_params(("parallel",)),
        name="pack_in_proj",
    )(w_in_layers)

    def take(names):
        return [b_in[cols[n][0]:cols[n][1]] for n in names]

    n_gate = 2 * M_HEADS + F_HEADS
    b_main = jnp.concatenate(take(("mqk", "mv", "mo", "fq", "fk", "fv", "mi", "mf", "ff"))
                             + [jnp.zeros((GATE_COLS - n_gate,), b_in.dtype)])
    b_gate = jnp.concatenate(take(("gm", "gf")))
    return w_main, b_main[None, :], w_gate, b_gate[None, :]


def _layer(h, depth, layer, w_in_layers, b_in, m_conv_w, m_conv_b, m_norm_g, w_bm, w_bf, w_o, ln1_g, ln1_b,
           w_router, b_router, w_gu, b_gu, w_dn, b_dn, ln2_g, ln2_b):
    bsz, s, d = h.shape
    t = bsz * s
    dn_alpha = (2.0 * depth) ** 0.25
    x2 = h.reshape(t, d)
    assert d == MQK_W and w_gu.shape[0] == N_EXPERTS, "kernels are written for this layer geometry"
    assert s % INPROJ_ROWS == 0 and s % MLSTM_CHUNK == 0 and s % FOX_Q_BLOCK == 0
    assert bsz % MLSTM_SEQS == 0
    assert t % MERGE_ROWS == 0 and t % ROUTE_ROWS == 0 and t % COMBINE_ROWS == 0
    assert t % (V7X_SC_WORKERS * SC_SCATTER_ROWS) == 0
    assert (t * TOP_K) % (V7X_SC_WORKERS * SC_GATHER_ROWS * SC_GATHER_WAYS) == 0

    w_main, b_main, w_gate, b_gate = _pack_in_proj(w_in_layers, layer, b_in)
    mq, mk, mv, mo, fq, fk, fv, gates = _inproj(x2, w_main, b_main, m_conv_w, m_conv_b[None, :], s)
    ccol, crel_k, crel_q = _fox_cumsum(gates, bsz, s)
    hm = _mlstm(mq, mk, mv, mo, gates, m_norm_g, bsz, s)
    hf = _fox_attention(fq, fk, fv, ccol, crel_k, crel_q, bsz, s)

    n_exp = w_router.shape[1]
    wr = jnp.zeros((d, V7X_LANES), F32).at[:, :n_exp].set(w_router)
    wr_hi = wr.astype(BF16)
    wr_lo = (wr - wr_hi.astype(F32)).astype(BF16)
    br = jnp.zeros((1, V7X_LANES), F32).at[0, :n_exp].set(b_router)
    h1, h1p, gate, tope, tile_counts = _merge(
        dn_alpha, hm, hf, x2, w_gate, b_gate, w_bm.astype(BF16), w_bf.astype(BF16), w_o.astype(BF16),
        ln1_g[None, :], ln1_b[None, :], wr_hi, wr_lo, br)

    n_blocks = -(-(t * TOP_K) // MOE_BLOCK) + N_EXPERTS
    dest, table = _routing(tile_counts, tope, n_blocks)
    dest_km = dest[:TOP_K].reshape(TOP_K * t)
    block_e, block_valid = table[:, 0], table[:, 1]
    xs = _sc_dispatch(dest_km, h1p, n_blocks * MOE_BLOCK)
    y_rows = _experts(block_e, block_valid, xs, w_gu, b_gu[:, None, :], w_dn, b_dn[:, None, :])
    yg = _sc_gather(dest_km, y_rows).reshape(TOP_K, t, d // 2)
    out = _combine(dn_alpha, h1, gate, ln2_g[None, :], ln2_b[None, :], yg)
    return out.reshape(bsz, s, d)


def kernel(x, w_in, b_in, m_conv_w, m_conv_b, m_norm_g, w_bm, w_bf, w_o, ln1_g, ln1_b,
           w_router, b_router, w_gu, b_gu, w_dn, b_dn, ln2_g, ln2_b):
    depth = w_in.shape[0]
    h = x
    for l in range(depth):
        h = _layer(h, depth, l, w_in, b_in[l], m_conv_w[l], m_conv_b[l], m_norm_g[l], w_bm[l], w_bf[l],
                   w_o[l], ln1_g[l], ln1_b[l], w_router[l], b_router[l], w_gu[l], b_gu[l], w_dn[l],
                   b_dn[l], ln2_g[l], ln2_b[l])
    return h
```

```python
import functools

import jax
import jax.numpy as jnp
from jax import lax
from jax.experimental import pallas as pl
from jax.experimental.pallas import tpu as pltpu
from jax.experimental.pallas import tpu_sc as plsc

F32 = jnp.float32
BF16 = jnp.bfloat16
I32 = jnp.int32
U32 = jnp.uint32

M_HEADS = 4
M_DQK = 128
M_DV = 128
CONV_W = 4
F_HEADS = 8
F_DH = 64
N_EXPERTS = 32
TOP_K = 4
SWIGLU_ALPHA = 1.702
SWIGLU_LIMIT = 7.0
LN_EPS = 1e-5

M_W = M_HEADS * M_DV
F_W = F_HEADS * F_DH
MQK_W = 2 * M_HEADS * M_DQK

V7X_LANES = 128
V7X_SUBLANES = 8
V7X_VMEM_BYTES = 64 * 1024 * 1024
VMEM_LIMIT_BYTES = (V7X_VMEM_BYTES * 3) // 4
EXPERT_VMEM_LIMIT_BYTES = (V7X_VMEM_BYTES * 7) // 8
V7X_SC_CORES = 2
V7X_SC_SUBCORES = 16
V7X_SC_WORKERS = V7X_SC_CORES * V7X_SC_SUBCORES

INPROJ_ROWS = 512
MLSTM_CHUNK = 256
MLSTM_SEQS = 4
FOX_Q_BLOCK = 512
FOX_K_BLOCK = 256
MERGE_ROWS = 512
MERGE_SUB_ROWS = 256
ROUTE_ROWS = 2048
ROUTE_SUB_ROWS = 256
MOE_BLOCK = 512
SC_SCATTER_ROWS = 128
SC_GATHER_ROWS = 64
SC_GATHER_WAYS = 2
COMBINE_ROWS = 1024

GATE_COLS = V7X_LANES
MI_LANE = 0
MF_LANE = M_HEADS
FF_LANE = 2 * M_HEADS


def _params(semantics):
    return pltpu.CompilerParams(dimension_semantics=semantics, vmem_limit_bytes=VMEM_LIMIT_BYTES)


def _log_sigmoid(x):
    return jnp.minimum(x, 0.0) - jnp.log1p(jnp.exp(-jnp.abs(x)))


def _sigmoid(x):
    return 0.5 * jnp.tanh(0.5 * x) + 0.5


def _dot(a, b):
    return jnp.dot(a, b, preferred_element_type=F32)


def _dot_nt(a, b):
    return lax.dot_general(a, b, (((1,), (1,)), ((), ())), preferred_element_type=F32)


def _split3(x):
    hi = x.astype(BF16)
    r1 = x - hi.astype(F32)
    mid = r1.astype(BF16)
    lo = (r1 - mid.astype(F32)).astype(BF16)
    return hi, mid, lo


def _dot_mask_f32(mask_bf16, x):
    hi, mid, lo = _split3(x)
    return (_dot(mask_bf16, lo) + _dot(mask_bf16, mid)) + _dot(mask_bf16, hi)


def _pack_bf16_pairs(x):
    half = x.shape[1] // 2
    bits = lax.bitcast_convert_type(x.astype(BF16).astype(F32), U32)
    return (bits[:, :half] >> 16) | bits[:, half:]


def _unpack_bf16_pairs(words):
    lo = lax.bitcast_convert_type(words << 16, F32)
    hi = lax.bitcast_convert_type(words & jnp.uint32(0xFFFF0000), F32)
    return jnp.concatenate([lo, hi], axis=1)


def _tril_mask(n, strict=False):
    r = lax.broadcasted_iota(I32, (n, n), 0)
    c = lax.broadcasted_iota(I32, (n, n), 1)
    return (r > c) if strict else (r >= c)


_OFF_MQK = 0
_OFF_MV = _OFF_MQK + MQK_W
_OFF_MO = _OFF_MV + M_W
_OFF_FQ = _OFF_MO + M_W
_OFF_FK = _OFF_FQ + F_W
_OFF_FV = _OFF_FK + F_W
_OFF_GATES = _OFF_FV + F_W
_PACKED_COLS = _OFF_GATES + GATE_COLS


def _inproj_kernel(tiles_per_seq, x_ref, w_ref, b_ref, cw_ref, cb_ref,
                   mq_ref, mk_ref, mv_ref, mo_ref, fq_ref, fk_ref, fv_ref, gates_ref, ext_ref):
    tm = x_ref.shape[0]
    pad = V7X_SUBLANES
    xb = x_ref[...].astype(BF16)

    def seg(lo, width):
        return _dot(xb, w_ref[:, lo:lo + width]) + b_ref[:, lo:lo + width]

    @pl.when(pl.program_id(0) % tiles_per_seq == 0)
    def _():
        ext_ref[0:pad, :] = jnp.zeros((pad, MQK_W), F32)

    def conv_silu(cols):
        ext_ref[pad:pad + tm, cols] = seg(_OFF_MQK + cols.start, cols.stop - cols.start)
        y = cb_ref[:, cols] + cw_ref[CONV_W - 1:CONV_W, cols] * ext_ref[pad:pad + tm, cols]
        for k in range(CONV_W - 1):
            shift = CONV_W - 1 - k
            y = y + cw_ref[k:k + 1, cols] * ext_ref[pad - shift:pad - shift + tm, cols]
        ext_ref[0:pad, cols] = ext_ref[tm:tm + pad, cols]
        return y * _sigmoid(y)

    half = MQK_W // 2
    group = half // 2
    mq_ref[:, :group] = conv_silu(slice(0, group)).astype(BF16)
    mv_ref[...] = seg(_OFF_MV, M_W).astype(BF16)
    mq_ref[:, group:] = conv_silu(slice(group, half)).astype(BF16)
    mo_ref[...] = seg(_OFF_MO, M_W)
    mk_ref[:, :group] = conv_silu(slice(half, half + group)) * (M_DQK ** -0.5)
    fq_ref[...] = (seg(_OFF_FQ, F_W) * (F_DH ** -0.5)).astype(BF16)
    mk_ref[:, group:] = conv_silu(slice(half + group, MQK_W)) * (M_DQK ** -0.5)
    fk_ref[...] = seg(_OFF_FK, F_W).astype(BF16)
    fv_ref[...] = seg(_OFF_FV, F_W).astype(BF16)
    gates_ref[...] = seg(_OFF_GATES, GATE_COLS)


def _inproj(x2, w_packed, b_packed, conv_w, conv_b, s):
    t, d = x2.shape
    tm = INPROJ_ROWS
    row = lambda i: (i, 0)
    const = lambda i: (0, 0)
    out_shapes = (
        jax.ShapeDtypeStruct((t, MQK_W // 2), BF16),
        jax.ShapeDtypeStruct((t, MQK_W // 2), F32),
        jax.ShapeDtypeStruct((t, M_W), BF16),
        jax.ShapeDtypeStruct((t, M_W), F32),
        jax.ShapeDtypeStruct((t, F_W), BF16),
        jax.ShapeDtypeStruct((t, F_W), BF16),
        jax.ShapeDtypeStruct((t, F_W), BF16),
        jax.ShapeDtypeStruct((t, GATE_COLS), F32),
    )
    return pl.pallas_call(
        functools.partial(_inproj_kernel, s // tm),
        grid=(t // tm,),
        in_specs=[
            pl.BlockSpec((tm, d), row),
            pl.BlockSpec((d, _PACKED_COLS), const),
            pl.BlockSpec((1, _PACKED_COLS), const),
            pl.BlockSpec((CONV_W, MQK_W), const),
            pl.BlockSpec((1, MQK_W), const),
        ],
        out_specs=tuple(pl.BlockSpec((tm, o.shape[1]), row) for o in out_shapes),
        out_shape=out_shapes,
        scratch_shapes=[pltpu.VMEM((tm + V7X_SUBLANES, MQK_W), F32)],
        compiler_params=_params(("arbitrary",)),
        name="inproj",
    )(x2, w_packed, b_packed, conv_w, conv_b)


def _fox_cumsum_kernel(g_ref, ccol_ref, crel_k_ref, crel_q_ref):
    s = g_ref.shape[0]
    cb = FOX_K_BLOCK
    per_q = FOX_Q_BLOCK // cb
    tri = _tril_mask(cb).astype(BF16)
    carry = jnp.zeros((1, GATE_COLS), F32)
    for j in range(s // cb):
        rows = slice(j * cb, (j + 1) * cb)
        if j % per_q == 0:
            q_carry = carry
        within = _dot_mask_f32(tri, _log_sigmoid(g_ref[rows, :]))
        crel_k_ref[rows, :] = within
        crel_q_ref[rows, :] = within + (carry - q_carry)
        ccol_ref[rows, :] = within + carry
        carry = carry + within[cb - 1:cb, :]


def _fox_cumsum(gates, bsz, s):
    t = gates.shape[0]
    spec = pl.BlockSpec((s, GATE_COLS), lambda b: (b, 0))
    shape = jax.ShapeDtypeStruct((t, GATE_COLS), F32)
    return pl.pallas_call(
        _fox_cumsum_kernel,
        grid=(bsz,),
        in_specs=[spec],
        out_specs=(spec, spec, spec),
        out_shape=(shape, shape, shape),
        compiler_params=_params(("parallel",)),
        name="fox_cumsum",
    )(gates)


def _mlstm_kernel(mq_ref, mk_ref, mv_ref, mo_ref, gates_ref, ng_ref, hm_ref, state_ref, m_ref):
    @pl.when(pl.program_id(1) == 0)
    def _():
        state_ref[...] = jnp.zeros(state_ref.shape, F32)
        m_ref[...] = jnp.zeros(m_ref.shape, F32)

    seqs = range(mq_ref.shape[0])
    states = [[state_ref[bb, h] for h in range(M_HEADS)] for bb in seqs]
    maxes = [[m_ref[bb, h][0:1, 0:1] for h in range(M_HEADS)] for bb in seqs]
    results = [_mlstm_chunk(mq_ref.at[bb], mk_ref.at[bb], mv_ref.at[bb], mo_ref.at[bb], gates_ref.at[bb],
                            ng_ref, states[bb], maxes[bb]) for bb in seqs]
    for bb in seqs:
        for h, (out_h, state_h, m_h) in enumerate(results[bb]):
            hm_ref[bb, :, h * M_DV:(h + 1) * M_DV] = out_h
            state_ref[bb, h] = state_h
            m_ref[bb, h] = jnp.broadcast_to(m_h, m_ref.shape[2:])


def _mlstm_chunk(mq_ref, mk_ref, mv_ref, mo_ref, gates_ref, ng_ref, states, maxes):
    L = MLSTM_CHUNK
    reps = L // V7X_LANES
    results = []
    gates = gates_ref[...]
    bfull = _dot_mask_f32(_tril_mask(L).astype(BF16), _log_sigmoid(gates))
    b_rows = bfull.T
    z_all = gates - pltpu.roll(bfull, shift=V7X_LANES - (MF_LANE - MI_LANE), axis=1)
    visible = (lax.broadcasted_iota(I32, (L, L), 0) <= lax.broadcasted_iota(I32, (L, L), 1))
    ones_rows = (lax.broadcasted_iota(I32, (M_DV, L), 0) == 0).astype(BF16)

    for h in range(M_HEADS):
        b_row = b_rows[MF_LANE + h:MF_LANE + h + 1, :]
        g_tot = b_row[:, L - 1:L]
        m_prev = maxes[h]
        z = jnp.broadcast_to(z_all[:, MI_LANE + h:MI_LANE + h + 1], (L, V7X_LANES))

        q_h = mq_ref[:, h * M_DQK:(h + 1) * M_DQK]
        k_f = mk_ref[:, h * M_DQK:(h + 1) * M_DQK]
        k_h = k_f.astype(BF16)
        v_t = mv_ref[:, h * M_DV:(h + 1) * M_DV].astype(F32).T.astype(BF16)
        cn_t = states[h]

        dlog = jnp.where(visible, b_row + jnp.tile(z, (1, reps)), -jnp.inf)
        inter_log = b_row + m_prev
        m_t = jnp.maximum(inter_log, jnp.max(dlog, axis=0, keepdims=True))
        w_inter = jnp.exp(inter_log - m_t)
        qkw = _dot_nt(k_h, q_h) * jnp.exp(dlog - m_t)
        qc = _dot_nt(cn_t.astype(BF16), q_h)
        num = w_inter * qc[:M_DV, :] + _dot(v_t, qkw.astype(BF16))
        den = w_inter * qc[M_DV:M_DV + 1, :] + jnp.sum(qkw, axis=0, keepdims=True)
        hh = num / jnp.maximum(jnp.abs(den), jnp.exp(-m_t))

        mu = jnp.mean(hh, axis=0, keepdims=True)
        dv = hh - mu
        var = jnp.mean(dv * dv, axis=0, keepdims=True)
        hn = (dv * lax.rsqrt(var + LN_EPS)) * jnp.tile(ng_ref[h * M_DV:(h + 1) * M_DV, :], (1, reps))
        out_h = (_sigmoid(mo_ref[:, h * M_DV:(h + 1) * M_DV]) * hn.T).astype(BF16)

        a = g_tot + z
        m_new = jnp.maximum(g_tot + m_prev, jnp.max(a, axis=0, keepdims=True)[:, 0:1])
        decay = jnp.exp(g_tot + m_prev - m_new)
        kw = (k_f * jnp.exp(a - m_new)).astype(BF16)
        v_aug = jnp.concatenate([v_t, ones_rows], axis=0)
        results.append((out_h, decay * cn_t + _dot(v_aug, kw), m_new))
    return results


def _mlstm(mq, mk, mv, mo, gates, norm_g, bsz, s):
    t = mq.shape[0]
    L = MLSTM_CHUNK
    nb = MLSTM_SEQS
    seq = lambda a: a.reshape(bsz, s, a.shape[1])
    blk = lambda width: pl.BlockSpec((nb, L, width), lambda g, n: (g, n, 0))
    const = lambda g, n: (0, 0)
    hm = pl.pallas_call(
        _mlstm_kernel,
        grid=(bsz // nb, s // L),
        in_specs=[
            blk(MQK_W // 2), blk(MQK_W // 2), blk(M_W), blk(M_W), blk(GATE_COLS),
            pl.BlockSpec((M_W, V7X_LANES), const),
        ],
        out_specs=blk(M_W),
        out_shape=jax.ShapeDtypeStruct((bsz, s, M_W), BF16),
        scratch_shapes=[
            pltpu.VMEM((nb, M_HEADS, 2 * M_DV, M_DQK), F32),
            pltpu.VMEM((nb, M_HEADS, V7X_SUBLANES, V7X_LANES), F32),
        ],
        compiler_params=_params(("parallel", "arbitrary")),
        name="mlstm",
    )(seq(mq), seq(mk), seq(mv), seq(mo), seq(gates),
      jnp.broadcast_to(norm_g[:, None], (M_W, V7X_LANES)))
    return hm.reshape(t, M_W)


_FOX_FEATURES = 3


def _fox_operands(x, c_tile, c_lane, key_side):
    rows = x.shape[0]
    nf = _FOX_FEATURES
    first = 0 if key_side else nf
    src = lax.broadcasted_iota(I32, (V7X_LANES, V7X_LANES), 0) - c_lane
    dst = lax.broadcasted_iota(I32, (V7X_LANES, V7X_LANES), 1)
    feats = jnp.zeros((rows, V7X_LANES), F32)
    for n, part in enumerate(_split3(c_tile)):
        place = jnp.logical_or(jnp.logical_and(src == 0, dst == F_DH + first + n),
                               jnp.logical_and(src == 1, dst == first + n))
        feats = feats + _dot(part, place.astype(BF16))
    lane = lax.broadcasted_iota(I32, (rows, V7X_LANES), 1)
    within = lane % F_DH
    const_lanes = jnp.logical_and(within >= nf - first, within < 2 * nf - first)
    feats = jnp.where(const_lanes, 1.0 if key_side else -1.0, feats).astype(BF16)
    low = lane < F_DH
    return jnp.where(low, x, feats), jnp.where(low, feats, x)


def _fox_kernel(q_ref, k_ref, v_ref, ccol_ref, crel_k_ref, crel_q_ref, o_ref, qaug_ref, kaug_ref, vt_ref,
                st_a0, st_a1, st_b0, st_b1, pe_a0, pe_a1, pe_b0, pe_b1, acc0, acc1):
    blk = FOX_K_BLOCK
    tq = FOX_Q_BLOCK
    assert tq == 2 * blk
    strips = tq // V7X_LANES
    st_a_refs, st_b_refs = (st_a0, st_a1), (st_b0, st_b1)
    pe_a_refs, pe_b_refs = (pe_a0, pe_a1), (pe_b0, pe_b1)
    acc_refs = (acc0, acc1)
    p = pl.program_id(1)
    i = pl.program_id(2)

    @pl.when(i == 0)
    def _():
        c_lane = FF_LANE + 2 * p
        qaug_ref[0], qaug_ref[1] = _fox_operands(q_ref[...], crel_q_ref[...], c_lane, False)
        kaug_ref[0], kaug_ref[1] = _fox_operands(k_ref[...], crel_k_ref[...], c_lane, True)
        v_t = v_ref[...].astype(F32).T
        for j in range(vt_ref.shape[0]):
            vt_ref[j] = v_t[:, j * blk:(j + 1) * blk].astype(BF16)

    q_start = pl.multiple_of(i * tq, tq)
    q_heads = [qaug_ref[hh, pl.ds(q_start, tq), :] for hh in range(2)]
    key_row = lax.broadcasted_iota(I32, (blk, V7X_LANES), 0)
    query_col = lax.broadcasted_iota(I32, (blk, V7X_LANES), 1)
    last_chunk = 2 * i + 1
    head_lane = lax.broadcasted_iota(I32, (1, GATE_COLS), 1) - (FF_LANE + 2 * p)

    def c_before(position, hh):
        row = ccol_ref[pl.ds(jnp.maximum(position - 1, 0), 1), :]
        keep = jnp.logical_and(head_lane == hh, position > 0)
        return jnp.sum(jnp.where(keep, row, 0.0), axis=-1, keepdims=True)

    c_query0 = [c_before(q_start, hh) for hh in range(2)]

    def put_scores(st_ref, j, hh):
        start = pl.multiple_of(j * blk, blk)
        st = _dot_nt(kaug_ref[hh, pl.ds(start, blk), :], q_heads[hh])
        for c in range(strips):
            st_ref[c] = st[:, c * V7X_LANES:(c + 1) * V7X_LANES]

    def exponentials(pe_ref):
        return jnp.concatenate([pe_ref[c] for c in range(strips)], axis=1)

    def softmax_update(st_ref, pe_ref, m_old, l_old, base, key_minus_query=None):
        alphas, ms, ls = [], [], []
        for c in range(strips):
            cols = slice(c * V7X_LANES, (c + 1) * V7X_LANES)
            gap = None if key_minus_query is None else key_minus_query - c * V7X_LANES
            if gap is not None and gap - (V7X_LANES - 1) > 0:
                pe_ref[c] = jnp.zeros((blk, V7X_LANES), BF16)
                alphas.append(jnp.ones((1, V7X_LANES), F32))
                ms.append(m_old[:, cols])
                ls.append(l_old[:, cols])
                continue
            st = st_ref[c]
            if gap is not None and gap + (blk - 1) > 0:
                st = jnp.where(key_row + gap <= query_col, st, -jnp.inf)
            m_new = jnp.maximum(m_old[:, cols], jnp.max(st, axis=0, keepdims=True) + base)
            alpha = jnp.exp(m_old[:, cols] - m_new)
            pe = jnp.exp(st - (m_new - base))
            pe_ref[c] = pe.astype(BF16)
            alphas.append(alpha)
            ms.append(m_new)
            ls.append(alpha * l_old[:, cols] + jnp.sum(pe, axis=0, keepdims=True))
        cat = lambda parts: jnp.concatenate(parts, axis=1)
        return cat(alphas), cat(ms), cat(ls)

    def pair(mi, carry, diagonal=False):
        a = 2 * mi
        b = a + 1
        v_prev = vt_ref[jnp.maximum(a - 1, 0)]
        v_a = vt_ref[a]
        partial = []
        for hh in range(2):
            alpha_prev = carry[hh][0]
            partial.append(alpha_prev * acc_refs[hh][...] + _dot(v_prev, exponentials(pe_b_refs[hh])))
            put_scores(st_b_refs[hh], b, hh)
        stats = []
        for hh in range(2):
            _, m_old, l_old = carry[hh]
            stats.append(softmax_update(st_a_refs[hh], pe_a_refs[hh], m_old, l_old,
                                        c_query0[hh] - c_before(a * blk, hh), 0 if diagonal else None))
        for hh in range(2):
            alpha_a = stats[hh][0]
            acc_refs[hh][...] = alpha_a * partial[hh] + _dot(v_a, exponentials(pe_a_refs[hh]))
            if not diagonal:
                put_scores(st_a_refs[hh], a + 2, hh)
        return tuple(softmax_update(st_b_refs[hh], pe_b_refs[hh], stats[hh][1], stats[hh][2],
                                    c_query0[hh] - c_before(b * blk, hh), blk if diagonal else None)
                     for hh in range(2))

    for hh in range(2):
        put_scores(st_a_refs[hh], 0, hh)
        pe_b_refs[hh][...] = jnp.zeros(pe_b_refs[hh].shape, BF16)
        acc_refs[hh][...] = jnp.zeros((V7X_LANES, tq), F32)
    init = tuple((jnp.ones((1, tq), F32), jnp.full((1, tq), -jnp.inf, F32), jnp.zeros((1, tq), F32))
                 for _ in range(2))
    final = pair(i, lax.fori_loop(0, i, pair, init), diagonal=True)
    v_last = vt_ref[last_chunk]
    outs = []
    for hh in range(2):
        alpha, _, l_fin = final[hh]
        outs.append((alpha * acc_refs[hh][...] + _dot(v_last, exponentials(pe_b_refs[hh]))) / l_fin)
    row = lax.broadcasted_iota(I32, (V7X_LANES, tq), 0)
    o_t = jnp.where(row < F_DH, outs[0], outs[1])
    o_ref[...] = o_t.T.astype(BF16)


def _fox_attention(fq, fk, fv, ccol, crel_k, crel_q, bsz, s):
    t = fq.shape[0]
    blk = FOX_K_BLOCK
    tq = FOX_Q_BLOCK
    nq = s // tq
    pairs = F_HEADS // 2
    qmap = lambda b, p, i: (b * nq + i, p)
    kvmap = lambda b, p, i: (b, p)
    return pl.pallas_call(
        _fox_kernel,
        grid=(bsz, pairs, nq),
        in_specs=[
            pl.BlockSpec((s, V7X_LANES), kvmap),
            pl.BlockSpec((s, V7X_LANES), kvmap),
            pl.BlockSpec((s, V7X_LANES), kvmap),
            pl.BlockSpec((s, GATE_COLS), lambda b, p, i: (b, 0)),
            pl.BlockSpec((s, GATE_COLS), lambda b, p, i: (b, 0)),
            pl.BlockSpec((s, GATE_COLS), lambda b, p, i: (b, 0)),
        ],
        out_specs=pl.BlockSpec((tq, V7X_LANES), qmap),
        out_shape=jax.ShapeDtypeStruct((t, F_W), BF16),
        scratch_shapes=[
            pltpu.VMEM((2, s, V7X_LANES), BF16),
            pltpu.VMEM((2, s, V7X_LANES), BF16),
            pltpu.VMEM((s // blk, V7X_LANES, blk), BF16),
        ] + [pltpu.VMEM((tq // V7X_LANES, blk, V7X_LANES), F32)] * 4 + [pltpu.VMEM((tq // V7X_LANES, blk, V7X_LANES), BF16)] * 4
          + [pltpu.VMEM((V7X_LANES, tq), F32)] * 2,
        compiler_params=_params(("parallel", "parallel", "arbitrary")),
        name="fox_attention",
    )(fq, fk, fv, ccol, crel_k, crel_q)


def _layer_norm_rows(r, g, b):
    mu = jnp.mean(r, axis=-1, keepdims=True)
    d = r - mu
    var = jnp.mean(d * d, axis=-1, keepdims=True)
    return (d * lax.rsqrt(var + LN_EPS)) * g + b


def _merge_kernel(dn_alpha, hm_ref, hf_ref, x_ref, wg_ref, bg_ref, wbm_ref, wbf_ref, wo_ref,
                  g_ref, b_ref, wrh_ref, wrl_ref, br_ref, h1_ref, h1p_ref, gate_ref, tope_ref, cnt_ref,
                  resid_ref):
    i = pl.program_id(0)
    slot = i % 2

    @pl.when(i == 0)
    def _():
        resid_ref[1] = jnp.zeros(resid_ref.shape[1:], F32)

    subs = [slice(n * MERGE_SUB_ROWS, (n + 1) * MERGE_SUB_ROWS)
            for n in range(x_ref.shape[0] // MERGE_SUB_ROWS)]
    counts = jnp.zeros((1, V7X_LANES), F32)
    for rows in subs:
        counts = counts + _merge_tail(resid_ref[1 - slot, rows, :], rows, g_ref, b_ref, wrh_ref, wrl_ref,
                                      br_ref, h1_ref, h1p_ref, gate_ref, tope_ref)
    sub = lax.broadcasted_iota(I32, cnt_ref.shape, 0)
    cnt_ref[...] = jnp.where(sub == 0, counts, 0.0)

    for rows in subs:
        x = x_ref[rows, :]
        d = x.shape[1]
        gmf = _dot(x.astype(BF16), wg_ref[...]) + bg_ref[...]
        ym = _dot(hm_ref[rows, :], wbm_ref[...])
        yf = _dot(hf_ref[rows, :], wbf_ref[...])
        y = _sigmoid(gmf[:, :d]) * ym + _sigmoid(gmf[:, d:]) * yf
        resid_ref[slot, rows, :] = dn_alpha * x + _dot(y.astype(BF16), wo_ref[...])


def _merge_tail(resid, rows, g_ref, b_ref, wrh_ref, wrl_ref, br_ref, h1_ref, h1p_ref, gate_ref, tope_ref):
    h1 = _layer_norm_rows(resid, g_ref[...], b_ref[...])
    h1_ref[rows, :] = h1

    h1p_ref[rows, :] = _pack_bf16_pairs(h1)
    hb = h1.astype(BF16)

    lo = (h1 - hb.astype(F32)).astype(BF16)
    logits = (_dot(lo, wrh_ref[...]) + _dot(hb, wrl_ref[...])) + _dot(hb, wrh_ref[...]) + br_ref[...]
    tm = logits.shape[0]
    lane = lax.broadcasted_iota(I32, (tm, V7X_LANES), 1)
    vals = jnp.where(lane < N_EXPERTS, logits, -jnp.inf)
    top_v, top_i = [], []
    for _ in range(TOP_K):
        mx = jnp.max(vals, axis=-1, keepdims=True)
        idx = jnp.min(jnp.where(vals == mx, lane, V7X_LANES), axis=-1, keepdims=True)
        top_v.append(mx)
        top_i.append(idx)
        vals = jnp.where(lane == idx, -jnp.inf, vals)
    ex = [jnp.exp(v - top_v[0]) for v in top_v]
    tot = ex[0]
    for e in ex[1:]:
        tot = tot + e
    gate = jnp.zeros((tm, V7X_LANES), F32)
    tope = jnp.zeros((tm, V7X_LANES), I32)
    member = jnp.zeros((tm, V7X_LANES), F32)
    for k in range(TOP_K):
        gate = jnp.where(lane == k, ex[k] / tot, gate)
        tope = jnp.where(lane == k, top_i[k], tope)
        member = member + (lane == top_i[k]).astype(F32)
    gate_ref[rows, :] = gate
    tope_ref[rows, :] = tope
    return jnp.sum(member, axis=0, keepdims=True)


def _merge(dn_alpha, hm, hf, x2, wg, bg, wbm, wbf, wo, ln_g, ln_b, wr_hi, wr_lo, br):
    t, d = x2.shape
    tm = MERGE_ROWS
    nt = t // tm
    row = lambda i: (jnp.minimum(i, nt - 1), 0)
    out_row = lambda i: (jnp.maximum(i - 1, 0), 0)
    const = lambda i: (0, 0)
    full = lambda a: pl.BlockSpec(a.shape, const)
    return pl.pallas_call(
        functools.partial(_merge_kernel, dn_alpha),
        grid=(nt + 1,),
        in_specs=[
            pl.BlockSpec((tm, M_W), row),
            pl.BlockSpec((tm, F_W), row),
            pl.BlockSpec((tm, d), row),
            full(wg), full(bg), full(wbm), full(wbf), full(wo), full(ln_g), full(ln_b),
            full(wr_hi), full(wr_lo), full(br),
        ],
        out_specs=(
            pl.BlockSpec((tm, d), out_row),
            pl.BlockSpec((tm, d // 2), out_row),
            pl.BlockSpec((tm, V7X_LANES), out_row),
            pl.BlockSpec((tm, V7X_LANES), out_row),
            pl.BlockSpec((V7X_SUBLANES, V7X_LANES), out_row),
        ),
        out_shape=(
            jax.ShapeDtypeStruct((t, d), F32),
            jax.ShapeDtypeStruct((t, d // 2), U32),
            jax.ShapeDtypeStruct((t, V7X_LANES), F32),
            jax.ShapeDtypeStruct((t, V7X_LANES), I32),
            jax.ShapeDtypeStruct((t // tm * V7X_SUBLANES, V7X_LANES), F32),
        ),
        scratch_shapes=[pltpu.VMEM((2, tm, d), F32)],
        compiler_params=_params(("arbitrary",)),
        name="merge_ln1_router",
    )(hm, hf, x2, wg, bg, wbm, wbf, wo, ln_g, ln_b, wr_hi, wr_lo, br)


def _lane_cumsum(x):
    lane = lax.broadcasted_iota(I32, x.shape, 1)
    d = 1
    while d < V7X_LANES:
        x = x + jnp.where(lane >= d, pltpu.roll(x, shift=d, axis=1), 0.0)
        d *= 2
    return x


def _routing_kernel(cnt_ref, tope_ref, dest_ref, table_ref, run_ref, start_ref):
    sb = ROUTE_SUB_ROWS

    @pl.when(pl.program_id(0) == 0)
    def _():
        total = jnp.sum(cnt_ref[...], axis=0, keepdims=True)
        counts = jnp.broadcast_to(total, (V7X_SUBLANES, V7X_LANES))
        padded = jnp.ceil(counts * (1.0 / MOE_BLOCK)) * MOE_BLOCK
        pad_end = _lane_cumsum(padded)
        pad_start = pad_end - padded
        start_ref[...] = pad_start
        run_ref[...] = jnp.zeros(run_ref.shape, F32)
        nb = table_ref.shape[0]
        blk = lax.broadcasted_iota(I32, (nb, V7X_LANES), 0).astype(F32) * MOE_BLOCK
        ln = lax.broadcasted_iota(I32, (nb, V7X_LANES), 1)
        done = jnp.logical_and(pad_end[0:1, :] <= blk, ln < N_EXPERTS)
        be = jnp.minimum(jnp.sum(done.astype(F32), axis=-1, keepdims=True), N_EXPERTS - 1.0)
        onehot = ln == be.astype(I32)
        cnt_e = jnp.sum(jnp.where(onehot, counts[0:1, :], 0.0), axis=-1, keepdims=True)
        start_e = jnp.sum(jnp.where(onehot, pad_start[0:1, :], 0.0), axis=-1, keepdims=True)
        valid = jnp.clip(cnt_e - (blk[:, 0:1] - start_e), 0.0, float(MOE_BLOCK))
        table_ref[...] = jnp.where(ln == 0, be.astype(I32),
                                   jnp.where(ln == 1, valid.astype(I32), 0))

    earlier = _tril_mask(sb, strict=True).astype(BF16)
    lane = lax.broadcasted_iota(I32, (sb, V7X_LANES), 1)
    offset = run_ref[0:1, :] + start_ref[0:1, :]
    seen = jnp.zeros((1, V7X_LANES), F32)
    for j in range(tope_ref.shape[0] // sb):
        tope = tope_ref[j * sb:(j + 1) * sb, :]
        hit = [lane == tope[:, k:k + 1] for k in range(TOP_K)]
        member = jnp.zeros((sb, V7X_LANES), F32)
        for k in range(TOP_K):
            member = member + hit[k].astype(F32)
        base = _dot(earlier, member.astype(BF16)) + (offset + seen)
        dest = jnp.zeros((sb, V7X_LANES), F32)
        for k in range(TOP_K):
            dk = jnp.sum(jnp.where(hit[k], base, 0.0), axis=-1, keepdims=True)
            dest = jnp.where(lane == k, dk, dest)
        dest_ref[:, j * sb:(j + 1) * sb] = dest.T[0:V7X_SUBLANES, :].astype(I32)
        seen = seen + jnp.sum(member, axis=0, keepdims=True)
    run_ref[...] = run_ref[...] + seen


def _routing(tile_counts, tope, n_blocks):
    t = tope.shape[0]
    tr = ROUTE_ROWS
    return pl.pallas_call(
        _routing_kernel,
        grid=(t // tr,),
        in_specs=[pl.BlockSpec(tile_counts.shape, lambda i: (0, 0)),
                  pl.BlockSpec((tr, V7X_LANES), lambda i: (i, 0))],
        out_specs=(
            pl.BlockSpec((V7X_SUBLANES, tr), lambda i: (0, i)),
            pl.BlockSpec((n_blocks, V7X_LANES), lambda i: (0, 0)),
        ),
        out_shape=(
            jax.ShapeDtypeStruct((V7X_SUBLANES, t), I32),
            jax.ShapeDtypeStruct((n_blocks, V7X_LANES), I32),
        ),
        scratch_shapes=[
            pltpu.VMEM((V7X_SUBLANES, V7X_LANES), F32),
            pltpu.VMEM((V7X_SUBLANES, V7X_LANES), F32),
        ],
        compiler_params=_params(("arbitrary",)),
        name="routing",
    )(tile_counts, tope)


def _sc_worker_id():
    return lax.axis_index("s") * V7X_SC_CORES + lax.axis_index("c")


def _sc_mesh():
    return plsc.VectorSubcoreMesh(core_axis_name="c", subcore_axis_name="s",
                                  num_cores=V7X_SC_CORES, num_subcores=V7X_SC_SUBCORES)


def _sc_dispatch(dest_km, h1p, n_rows):
    t, w = h1p.shape
    per_worker = t // V7X_SC_WORKERS
    ch = SC_SCATTER_ROWS

    @functools.partial(
        pl.kernel, mesh=_sc_mesh(),
        out_type=jax.ShapeDtypeStruct((n_rows, w), h1p.dtype),
        scratch_types=[pltpu.VMEM((ch, w), h1p.dtype)]
        + [pltpu.VMEM((ch,), I32)] * TOP_K + [pltpu.SemaphoreType.DMA] * TOP_K,
        name="sc_dispatch",
    )
    def scatter_rows(dest_hbm, h1p_hbm, xs_hbm, rows_v, *idx_and_sems):
        idx_refs, sems = idx_and_sems[:TOP_K], idx_and_sems[TOP_K:]
        first = _sc_worker_id() * per_worker

        @pl.loop(0, per_worker // ch)
        def _(j):
            base = first + j * ch
            pltpu.sync_copy(h1p_hbm.at[pl.ds(base, ch)], rows_v)
            copies = []
            for k in range(TOP_K):
                pltpu.sync_copy(dest_hbm.at[pl.ds(k * t + base, ch)], idx_refs[k])
                copies.append(pltpu.async_copy(rows_v, xs_hbm.at[idx_refs[k]], sems[k]))
            for copy in copies:
                copy.wait()

    return scatter_rows(dest_km, h1p)


def _sc_gather(dest_km, y_rows):
    n = dest_km.shape[0]
    w = y_rows.shape[1]
    per_worker = n // V7X_SC_WORKERS
    ch = SC_GATHER_ROWS
    ways = SC_GATHER_WAYS

    @functools.partial(
        pl.kernel, mesh=_sc_mesh(),
        out_type=jax.ShapeDtypeStruct((n, w), y_rows.dtype),
        scratch_types=[pltpu.VMEM((ch,), I32)] * ways + [pltpu.VMEM((ch, w), y_rows.dtype)] * ways
        + [pltpu.SemaphoreType.DMA] * (2 * ways),
        name="sc_gather",
    )
    def gather_rows(dest_hbm, y_hbm, out_hbm, *scratch):
        idx_refs, row_refs = scratch[:ways], scratch[ways:2 * ways]
        gather_sems, store_sems = scratch[2 * ways:3 * ways], scratch[3 * ways:]
        first = _sc_worker_id() * per_worker

        @pl.loop(0, per_worker // (ch * ways))
        def _(j):
            bases = [first + (j * ways + u) * ch for u in range(ways)]
            gathers = []
            for u in range(ways):
                pltpu.sync_copy(dest_hbm.at[pl.ds(bases[u], ch)], idx_refs[u])
                gathers.append(pltpu.async_copy(y_hbm.at[idx_refs[u]], row_refs[u], gather_sems[u]))
            stores = []
            for u in range(ways):
                gathers[u].wait()
                stores.append(pltpu.async_copy(row_refs[u], out_hbm.at[pl.ds(bases[u], ch)], store_sems[u]))
            for store in stores:
                store.wait()

    return gather_rows(dest_km, y_rows)


def _expert_kernel(be_ref, nv_ref, xs_ref, wgu_f32_ref, bgu_ref, wdn_f32_ref, bdn_ref, y_ref,
                   wgu_slots, wdn_slots, slot_ref):
    i = pl.program_id(0)
    n_blocks = pl.num_programs(0) - 1
    nv = jnp.where(i > 0, nv_ref[jnp.maximum(i - 1, 0)], 0)
    half = MOE_BLOCK // 2

    @pl.when(i == 0)
    def _():
        slot_ref[0] = 0

    slot = slot_ref[0]
    wgu_ref = wgu_slots.at[slot]
    wdn_ref = wdn_slots.at[slot]

    def ffn(rows):
        x = _unpack_bf16_pairs(xs_ref[rows, :])
        rowid = rows.start + lax.broadcasted_iota(I32, x.shape, 0)
        x = jnp.where(rowid < nv, x, 0.0).astype(BF16)
        gu = _dot(x, wgu_ref[...]) + bgu_ref[...]
        f = gu.shape[1] // 2
        glu = jnp.minimum(gu[:, :f], SWIGLU_LIMIT)
        lin = jnp.clip(gu[:, f:], -SWIGLU_LIMIT, SWIGLU_LIMIT)
        act = glu * _sigmoid(SWIGLU_ALPHA * glu) * (lin + 1.0)
        y_ref[rows, :] = _pack_bf16_pairs(_dot(act.astype(BF16), wdn_ref[...]) + bdn_ref[...])

    @pl.when(nv == 0)
    def _():
        y_ref[...] = jnp.zeros(y_ref.shape, U32)

    @pl.when(jnp.logical_and(nv > 0, nv <= half))
    def _():
        ffn(slice(0, half))
        y_ref[half:, :] = jnp.zeros((MOE_BLOCK - half, y_ref.shape[1]), U32)

    @pl.when(nv > half)
    def _():
        ffn(slice(0, half))
        ffn(slice(half, MOE_BLOCK))

    new_expert = jnp.logical_or(i == 0, be_ref[jnp.minimum(i, n_blocks - 1)] != be_ref[jnp.maximum(i - 1, 0)])

    @pl.when(jnp.logical_and(i < n_blocks, new_expert))
    def _():
        wgu_slots[1 - slot] = wgu_f32_ref[...].astype(BF16)
        wdn_slots[1 - slot] = wdn_f32_ref[...].astype(BF16)
        slot_ref[0] = 1 - slot


def _experts(block_e, block_valid, xs, wgu, bgu, wdn, bdn):
    n_rows, w = xs.shape
    e, d, f2 = wgu.shape
    n_blocks = n_rows // MOE_BLOCK
    ahead = lambda i, be: be[jnp.minimum(i, n_blocks - 1)]
    behind = lambda i: jnp.maximum(i - 1, 0)
    grid_spec = pltpu.PrefetchScalarGridSpec(
        num_scalar_prefetch=2,
        grid=(n_blocks + 1,),
        in_specs=[
            pl.BlockSpec((MOE_BLOCK, w), lambda i, be, nv: (behind(i), 0)),
            pl.BlockSpec((None, d, f2), lambda i, be, nv: (ahead(i, be), 0, 0)),
            pl.BlockSpec((None, 1, f2), lambda i, be, nv: (be[behind(i)], 0, 0)),
            pl.BlockSpec((None, f2 // 2, d), lambda i, be, nv: (ahead(i, be), 0, 0)),
            pl.BlockSpec((None, 1, d), lambda i, be, nv: (be[behind(i)], 0, 0)),
        ],
        out_specs=pl.BlockSpec((MOE_BLOCK, d // 2), lambda i, be, nv: (behind(i), 0)),
        scratch_shapes=[pltpu.VMEM((2, d, f2), BF16), pltpu.VMEM((2, f2 // 2, d), BF16),
                        pltpu.SMEM((1,), I32)],
    )
    return pl.pallas_call(
        _expert_kernel,
        grid_spec=grid_spec,
        out_shape=jax.ShapeDtypeStruct((n_rows, d // 2), U32),
        compiler_params=pltpu.CompilerParams(dimension_semantics=("arbitrary",),
                                             vmem_limit_bytes=EXPERT_VMEM_LIMIT_BYTES),
        name="experts",
    )(block_e, block_valid, xs, wgu, bgu, wdn, bdn)


def _combine_kernel(dn_alpha, h1_ref, gate_ref, g_ref, b_ref, yg_ref, o_ref):
    gate = gate_ref[...]
    ffn = gate[:, 0:1] * _unpack_bf16_pairs(yg_ref[0])
    for k in range(1, TOP_K):
        ffn = ffn + gate[:, k:k + 1] * _unpack_bf16_pairs(yg_ref[k])
    o_ref[...] = _layer_norm_rows(dn_alpha * h1_ref[...] + ffn, g_ref[...], b_ref[...])


def _combine(dn_alpha, h1, gate, ln_g, ln_b, yg):
    t, d = h1.shape
    tc = COMBINE_ROWS
    row = lambda i: (i, 0)
    const = lambda i: (0, 0)
    return pl.pallas_call(
        functools.partial(_combine_kernel, dn_alpha),
        grid=(t // tc,),
        in_specs=[
            pl.BlockSpec((tc, d), row),
            pl.BlockSpec((tc, V7X_LANES), row),
            pl.BlockSpec((1, d), const),
            pl.BlockSpec((1, d), const),
            pl.BlockSpec((TOP_K, tc, d // 2), lambda i: (0, i, 0)),
        ],
        out_specs=pl.BlockSpec((tc, d), row),
        out_shape=jax.ShapeDtypeStruct((t, d), F32),
        compiler_params=_params(("parallel",)),
        name="combine_ln2",
    )(h1, gate, ln_g, ln_b, yg)


def _in_proj_columns(d):
    o = 0
    cols = {}
    for name, width in (("mqk", MQK_W), ("mv", M_W), ("mo", M_W), ("mi", M_HEADS), ("mf", M_HEADS),
                        ("fq", F_W), ("fk", F_W), ("fv", F_W), ("ff", F_HEADS), ("gm", d), ("gf", d)):
        cols[name] = (o, o + width)
        o += width
    return cols


def _pack_w_kernel(w_ref, main_ref, gate_ref):
    d = gate_ref.shape[1] // 2
    cols = _in_proj_columns(d)
    out = 0
    for name in ("mqk", "mv", "mo", "fq", "fk", "fv"):
        lo, hi = cols[name]
        main_ref[:, out:out + hi - lo] = w_ref[:, lo:hi].astype(BF16)
        out += hi - lo
    lane = lax.broadcasted_iota(I32, (w_ref.shape[0], GATE_COLS), 1)
    mi_lo, ff_lo = cols["mi"][0], cols["ff"][0] - 2 * M_HEADS
    assert mi_lo % V7X_LANES == 0 and ff_lo % V7X_LANES == 0 and cols["mf"][0] == mi_lo + M_HEADS
    gates = jnp.where(lane < 2 * M_HEADS, w_ref[:, mi_lo:mi_lo + GATE_COLS],
                      jnp.where(lane < 2 * M_HEADS + F_HEADS, w_ref[:, ff_lo:ff_lo + GATE_COLS], 0.0))
    main_ref[:, out:out + GATE_COLS] = gates.astype(BF16)
    gate_ref[...] = w_ref[:, cols["gm"][0]:cols["gf"][1]].astype(BF16)


def _pack_in_proj(w_in_layers, layer, b_in):
    _, d, n_cols = w_in_layers.shape
    cols = _in_proj_columns(d)
    tr = V7X_LANES
    w_main, w_gate = pl.pallas_call(
        _pack_w_kernel,
        grid=(d // tr,),
        in_specs=[pl.BlockSpec((None, tr, n_cols), lambda i: (layer, i, 0))],
        out_specs=(pl.BlockSpec((tr, _PACKED_COLS), lambda i: (i, 0)),
                   pl.BlockSpec((tr, 2 * d), lambda i: (i, 0))),
        out_shape=(jax.ShapeDtypeStruct((d, _PACKED_COLS), BF16),
                   jax.ShapeDtypeStruct((d, 2 * d), BF16)),
        compiler_params=_params(("parallel",)),
        name="pack_in_proj",
    )(w_in_layers)

    def take(names):
        return [b_in[cols[n][0]:cols[n][1]] for n in names]

    n_gate = 2 * M_HEADS + F_HEADS
    b_main = jnp.concatenate(take(("mqk", "mv", "mo", "fq", "fk", "fv", "mi", "mf", "ff"))
                             + [jnp.zeros((GATE_COLS - n_gate,), b_in.dtype)])
    b_gate = jnp.concatenate(take(("gm", "gf")))
    return w_main, b_main[None, :], w_gate, b_gate[None, :]


def _layer(h, depth, layer, w_in_layers, b_in, m_conv_w, m_conv_b, m_norm_g, w_bm, w_bf, w_o, ln1_g, ln1_b,
           w_router, b_router, w_gu, b_gu, w_dn, b_dn, ln2_g, ln2_b):
    bsz, s, d = h.shape
    t = bsz * s
    dn_alpha = (2.0 * depth) ** 0.25
    x2 = h.reshape(t, d)
    assert d == MQK_W and w_gu.shape[0] == N_EXPERTS, "kernels are written for this layer geometry"
    assert s % INPROJ_ROWS == 0 and s % MLSTM_CHUNK == 0 and s % FOX_Q_BLOCK == 0
    assert bsz % MLSTM_SEQS == 0
    assert t % MERGE_ROWS == 0 and t % ROUTE_ROWS == 0 and t % COMBINE_ROWS == 0
    assert t % (V7X_SC_WORKERS * SC_SCATTER_ROWS) == 0
    assert (t * TOP_K) % (V7X_SC_WORKERS * SC_GATHER_ROWS * SC_GATHER_WAYS) == 0

    w_main, b_main, w_gate, b_gate = _pack_in_proj(w_in_layers, layer, b_in)
    mq, mk, mv, mo, fq, fk, fv, gates = _inproj(x2, w_main, b_main, m_conv_w, m_conv_b[None, :], s)
    ccol, crel_k, crel_q = _fox_cumsum(gates, bsz, s)
    hm = _mlstm(mq, mk, mv, mo, gates, m_norm_g, bsz, s)
    hf = _fox_attention(fq, fk, fv, ccol, crel_k, crel_q, bsz, s)

    n_exp = w_router.shape[1]
    wr = jnp.zeros((d, V7X_LANES), F32).at[:, :n_exp].set(w_router)
    wr_hi = wr.astype(BF16)
    wr_lo = (wr - wr_hi.astype(F32)).astype(BF16)
    br = jnp.zeros((1, V7X_LANES), F32).at[0, :n_exp].set(b_router)
    h1, h1p, gate, tope, tile_counts = _merge(
        dn_alpha, hm, hf, x2, w_gate, b_gate, w_bm.astype(BF16), w_bf.astype(BF16), w_o.astype(BF16),
        ln1_g[None, :], ln1_b[None, :], wr_hi, wr_lo, br)

    n_blocks = -(-(t * TOP_K) // MOE_BLOCK) + N_EXPERTS
    dest, table = _routing(tile_counts, tope, n_blocks)
    dest_km = dest[:TOP_K].reshape(TOP_K * t)
    block_e, block_valid = table[:, 0], table[:, 1]
    xs = _sc_dispatch(dest_km, h1p, n_blocks * MOE_BLOCK)
    y_rows = _experts(block_e, block_valid, xs, w_gu, b_gu[:, None, :], w_dn, b_dn[:, None, :])
    yg = _sc_gather(dest_km, y_rows).reshape(TOP_K, t, d // 2)
    out = _combine(dn_alpha, h1, gate, ln2_g[None, :], ln2_b[None, :], yg)
    return out.reshape(bsz, s, d)


def kernel(x, w_in, b_in, m_conv_w, m_conv_b, m_norm_g, w_bm, w_bf, w_o, ln1_g, ln1_b,
           w_router, b_router, w_gu, b_gu, w_dn, b_dn, ln2_g, ln2_b):
    depth = w_in.shape[0]
    h = x
    for l in range(depth):
        h = _layer(h, depth, l, w_in, b_in[l], m_conv_w[l], m_conv_b[l], m_norm_g[l], w_bm[l], w_bf[l],
                   w_o[l], ln1_g[l], ln1_b[l], w_router[l], b_router[l], w_gu[l], b_gu[l], w_dn[l],
                   b_dn[l], ln2_g[l], ln2_b[l])
    return h
```

```python
import functools

import jax
import jax.numpy as jnp
from jax import lax
from jax.experimental import pallas as pl
from jax.experimental.pallas import tpu as pltpu
from jax.experimental.pallas import tpu_sc as plsc

F32 = jnp.float32
BF16 = jnp.bfloat16
I32 = jnp.int32
U32 = jnp.uint32

M_HEADS = 4
M_DQK = 128
M_DV = 128
CONV_W = 4
F_HEADS = 8
F_DH = 64
N_EXPERTS = 32
TOP_K = 4
SWIGLU_ALPHA = 1.702
SWIGLU_LIMIT = 7.0
LN_EPS = 1e-5
LOG2_E = 1.4426950408889634

M_W = M_HEADS * M_DV
F_W = F_HEADS * F_DH
MQK_W = 2 * M_HEADS * M_DQK

V7X_LANES = 128
V7X_SUBLANES = 8
V7X_VMEM_BYTES = 64 * 1024 * 1024
VMEM_LIMIT_BYTES = (V7X_VMEM_BYTES * 3) // 4
EXPERT_VMEM_LIMIT_BYTES = (V7X_VMEM_BYTES * 7) // 8
V7X_SC_CORES = 2
V7X_SC_SUBCORES = 16
V7X_SC_WORKERS = V7X_SC_CORES * V7X_SC_SUBCORES

INPROJ_ROWS = 512
MLSTM_CHUNK = 256
MLSTM_SEQS = 4
FOX_Q_BLOCK = 512
FOX_K_BLOCK = 256
MERGE_ROWS = 512
MERGE_SUB_ROWS = 256
ROUTE_ROWS = 2048
ROUTE_SUB_ROWS = 256
MOE_BLOCK = 512
SC_SCATTER_ROWS = 128
SC_GATHER_ROWS = 64
SC_GATHER_WAYS = 2
COMBINE_ROWS = 1024

GATE_COLS = V7X_LANES
MI_LANE = 0
MF_LANE = M_HEADS
FF_LANE = 2 * M_HEADS


def _params(semantics):
    return pltpu.CompilerParams(dimension_semantics=semantics, vmem_limit_bytes=VMEM_LIMIT_BYTES)


def _log_sigmoid(x):
    return jnp.minimum(x, 0.0) - jnp.log1p(jnp.exp(-jnp.abs(x)))


def _sigmoid(x):
    return 0.5 * jnp.tanh(0.5 * x) + 0.5


def _dot(a, b):
    return jnp.dot(a, b, preferred_element_type=F32)


def _dot_nt(a, b):
    return lax.dot_general(a, b, (((1,), (1,)), ((), ())), preferred_element_type=F32)


def _split3(x):
    hi = x.astype(BF16)
    r1 = x - hi.astype(F32)
    mid = r1.astype(BF16)
    lo = (r1 - mid.astype(F32)).astype(BF16)
    return hi, mid, lo


def _dot_mask_f32(mask_bf16, x):
    hi, mid, lo = _split3(x)
    return (_dot(mask_bf16, lo) + _dot(mask_bf16, mid)) + _dot(mask_bf16, hi)


def _pack_bf16_pairs(x):
    half = x.shape[1] // 2
    bits = lax.bitcast_convert_type(x.astype(BF16).astype(F32), U32)
    return (bits[:, :half] >> 16) | bits[:, half:]


def _unpack_bf16_pairs(words):
    lo = lax.bitcast_convert_type(words << 16, F32)
    hi = lax.bitcast_convert_type(words & jnp.uint32(0xFFFF0000), F32)
    return jnp.concatenate([lo, hi], axis=1)


def _tril_mask(n, strict=False):
    r = lax.broadcasted_iota(I32, (n, n), 0)
    c = lax.broadcasted_iota(I32, (n, n), 1)
    return (r > c) if strict else (r >= c)


_OFF_MQK = 0
_OFF_MV = _OFF_MQK + MQK_W
_OFF_MO = _OFF_MV + M_W
_OFF_FQ = _OFF_MO + M_W
_OFF_FK = _OFF_FQ + F_W
_OFF_FV = _OFF_FK + F_W
_OFF_GATES = _OFF_FV + F_W
_PACKED_COLS = _OFF_GATES + GATE_COLS


def _inproj_kernel(tiles_per_seq, x_ref, w_ref, b_ref, cw_ref, cb_ref,
                   mq_ref, mk_ref, mv_ref, mo_ref, fq_ref, fk_ref, fv_ref, gates_ref, ext_ref):
    tm = x_ref.shape[0]
    pad = V7X_SUBLANES
    xb = x_ref[...].astype(BF16)

    def seg(lo, width):
        return _dot(xb, w_ref[:, lo:lo + width]) + b_ref[:, lo:lo + width]

    @pl.when(pl.program_id(0) % tiles_per_seq == 0)
    def _():
        ext_ref[0:pad, :] = jnp.zeros((pad, MQK_W), F32)

    def conv_silu(cols):
        ext_ref[pad:pad + tm, cols] = seg(_OFF_MQK + cols.start, cols.stop - cols.start)
        y = cb_ref[:, cols] + cw_ref[CONV_W - 1:CONV_W, cols] * ext_ref[pad:pad + tm, cols]
        for k in range(CONV_W - 1):
            shift = CONV_W - 1 - k
            y = y + cw_ref[k:k + 1, cols] * ext_ref[pad - shift:pad - shift + tm, cols]
        ext_ref[0:pad, cols] = ext_ref[tm:tm + pad, cols]
        return y * _sigmoid(y)

    half = MQK_W // 2
    group = half // 2
    mq_ref[:, :group] = conv_silu(slice(0, group)).astype(BF16)
    mv_ref[...] = seg(_OFF_MV, M_W).astype(BF16)
    mq_ref[:, group:] = conv_silu(slice(group, half)).astype(BF16)
    mo_ref[...] = seg(_OFF_MO, M_W)
    mk_ref[:, :group] = conv_silu(slice(half, half + group)) * (M_DQK ** -0.5)
    fq_ref[...] = (seg(_OFF_FQ, F_W) * (F_DH ** -0.5 * LOG2_E)).astype(BF16)
    mk_ref[:, group:] = conv_silu(slice(half + group, MQK_W)) * (M_DQK ** -0.5)
    fk_ref[...] = seg(_OFF_FK, F_W).astype(BF16)
    fv_ref[...] = seg(_OFF_FV, F_W).astype(BF16)
    gates_ref[...] = seg(_OFF_GATES, GATE_COLS)


def _inproj(x2, w_packed, b_packed, conv_w, conv_b, s):
    t, d = x2.shape
    tm = INPROJ_ROWS
    row = lambda i: (i, 0)
    const = lambda i: (0, 0)
    out_shapes = (
        jax.ShapeDtypeStruct((t, MQK_W // 2), BF16),
        jax.ShapeDtypeStruct((t, MQK_W // 2), F32),
        jax.ShapeDtypeStruct((t, M_W), BF16),
        jax.ShapeDtypeStruct((t, M_W), F32),
        jax.ShapeDtypeStruct((t, F_W), BF16),
        jax.ShapeDtypeStruct((t, F_W), BF16),
        jax.ShapeDtypeStruct((t, F_W), BF16),
        jax.ShapeDtypeStruct((t, GATE_COLS), F32),
    )
    return pl.pallas_call(
        functools.partial(_inproj_kernel, s // tm),
        grid=(t // tm,),
        in_specs=[
            pl.BlockSpec((tm, d), row),
            pl.BlockSpec((d, _PACKED_COLS), const),
            pl.BlockSpec((1, _PACKED_COLS), const),
            pl.BlockSpec((CONV_W, MQK_W), const),
            pl.BlockSpec((1, MQK_W), const),
        ],
        out_specs=tuple(pl.BlockSpec((tm, o.shape[1]), row) for o in out_shapes),
        out_shape=out_shapes,
        scratch_shapes=[pltpu.VMEM((tm + V7X_SUBLANES, MQK_W), F32)],
        compiler_params=_params(("arbitrary",)),
        name="inproj",
    )(x2, w_packed, b_packed, conv_w, conv_b)


def _fox_cumsum_kernel(g_ref, ccol_ref, crel_k_ref, crel_q_ref):
    s = g_ref.shape[0]
    cb = FOX_K_BLOCK
    per_q = FOX_Q_BLOCK // cb
    tri = _tril_mask(cb).astype(BF16)
    carry = jnp.zeros((1, GATE_COLS), F32)
    for j in range(s // cb):
        rows = slice(j * cb, (j + 1) * cb)
        if j % per_q == 0:
            q_carry = carry
        within = _dot_mask_f32(tri, _log_sigmoid(g_ref[rows, :])) * LOG2_E
        crel_k_ref[rows, :] = within
        crel_q_ref[rows, :] = within + (carry - q_carry)
        ccol_ref[rows, :] = within + carry
        carry = carry + within[cb - 1:cb, :]


def _fox_cumsum(gates, bsz, s):
    t = gates.shape[0]
    spec = pl.BlockSpec((s, GATE_COLS), lambda b: (b, 0))
    shape = jax.ShapeDtypeStruct((t, GATE_COLS), F32)
    return pl.pallas_call(
        _fox_cumsum_kernel,
        grid=(bsz,),
        in_specs=[spec],
        out_specs=(spec, spec, spec),
        out_shape=(shape, shape, shape),
        compiler_params=_params(("parallel",)),
        name="fox_cumsum",
    )(gates)


def _mlstm_kernel(mq_ref, mk_ref, mv_ref, mo_ref, gates_ref, ng_ref, hm_ref, state_ref, m_ref):
    @pl.when(pl.program_id(1) == 0)
    def _():
        state_ref[...] = jnp.zeros(state_ref.shape, F32)
        m_ref[...] = jnp.zeros(m_ref.shape, F32)

    seqs = range(mq_ref.shape[0])
    states = [[state_ref[bb, h] for h in range(M_HEADS)] for bb in seqs]
    maxes = [[m_ref[bb, h][0:1, 0:1] for h in range(M_HEADS)] for bb in seqs]
    results = [_mlstm_chunk(mq_ref.at[bb], mk_ref.at[bb], mv_ref.at[bb], mo_ref.at[bb], gates_ref.at[bb],
                            ng_ref, states[bb], maxes[bb]) for bb in seqs]
    for bb in seqs:
        for h, (out_h, state_h, m_h) in enumerate(results[bb]):
            hm_ref[bb, :, h * M_DV:(h + 1) * M_DV] = out_h
            state_ref[bb, h] = state_h
            m_ref[bb, h] = jnp.broadcast_to(m_h, m_ref.shape[2:])


def _mlstm_chunk(mq_ref, mk_ref, mv_ref, mo_ref, gates_ref, ng_ref, states, maxes):
    L = MLSTM_CHUNK
    reps = L // V7X_LANES
    results = []
    gates = gates_ref[...]
    bfull = _dot_mask_f32(_tril_mask(L).astype(BF16), _log_sigmoid(gates))
    b_rows = bfull.T
    z_all = gates - pltpu.roll(bfull, shift=V7X_LANES - (MF_LANE - MI_LANE), axis=1)
    visible = (lax.broadcasted_iota(I32, (L, L), 0) <= lax.broadcasted_iota(I32, (L, L), 1))
    ones_rows = (lax.broadcasted_iota(I32, (M_DV, L), 0) == 0).astype(BF16)

    for h in range(M_HEADS):
        b_row = b_rows[MF_LANE + h:MF_LANE + h + 1, :]
        g_tot = b_row[:, L - 1:L]
        m_prev = maxes[h]
        z = jnp.broadcast_to(z_all[:, MI_LANE + h:MI_LANE + h + 1], (L, V7X_LANES))

        q_h = mq_ref[:, h * M_DQK:(h + 1) * M_DQK]
        k_f = mk_ref[:, h * M_DQK:(h + 1) * M_DQK]
        k_h = k_f.astype(BF16)
        v_t = mv_ref[:, h * M_DV:(h + 1) * M_DV].astype(F32).T.astype(BF16)
        cn_t = states[h]

        dlog = jnp.where(visible, b_row + jnp.tile(z, (1, reps)), -jnp.inf)
        inter_log = b_row + m_prev
        m_t = jnp.maximum(inter_log, jnp.max(dlog, axis=0, keepdims=True))
        w_inter = jnp.exp(inter_log - m_t)
        qkw = _dot_nt(k_h, q_h) * jnp.exp(dlog - m_t)
        qc = _dot_nt(cn_t.astype(BF16), q_h)
        num = w_inter * qc[:M_DV, :] + _dot(v_t, qkw.astype(BF16))
        den = w_inter * qc[M_DV:M_DV + 1, :] + jnp.sum(qkw, axis=0, keepdims=True)
        hh = num / jnp.maximum(jnp.abs(den), jnp.exp(-m_t))

        mu = jnp.mean(hh, axis=0, keepdims=True)
        dv = hh - mu
        var = jnp.mean(dv * dv, axis=0, keepdims=True)
        hn = (dv * lax.rsqrt(var + LN_EPS)) * jnp.tile(ng_ref[h * M_DV:(h + 1) * M_DV, :], (1, reps))
        out_h = (_sigmoid(mo_ref[:, h * M_DV:(h + 1) * M_DV]) * hn.T).astype(BF16)

        a = g_tot + z
        m_new = jnp.maximum(g_tot + m_prev, jnp.max(a, axis=0, keepdims=True)[:, 0:1])
        decay = jnp.exp(g_tot + m_prev - m_new)
        kw = (k_f * jnp.exp(a - m_new)).astype(BF16)
        v_aug = jnp.concatenate([v_t, ones_rows], axis=0)
        results.append((out_h, decay * cn_t + _dot(v_aug, kw), m_new))
    return results


def _mlstm(mq, mk, mv, mo, gates, norm_g, bsz, s):
    t = mq.shape[0]
    L = MLSTM_CHUNK
    nb = MLSTM_SEQS
    seq = lambda a: a.reshape(bsz, s, a.shape[1])
    blk = lambda width: pl.BlockSpec((nb, L, width), lambda g, n: (g, n, 0))
    const = lambda g, n: (0, 0)
    hm = pl.pallas_call(
        _mlstm_kernel,
        grid=(bsz // nb, s // L),
        in_specs=[
            blk(MQK_W // 2), blk(MQK_W // 2), blk(M_W), blk(M_W), blk(GATE_COLS),
            pl.BlockSpec((M_W, V7X_LANES), const),
        ],
        out_specs=blk(M_W),
        out_shape=jax.ShapeDtypeStruct((bsz, s, M_W), BF16),
        scratch_shapes=[
            pltpu.VMEM((nb, M_HEADS, 2 * M_DV, M_DQK), F32),
            pltpu.VMEM((nb, M_HEADS, V7X_SUBLANES, V7X_LANES), F32),
        ],
        compiler_params=_params(("parallel", "arbitrary")),
        name="mlstm",
    )(seq(mq), seq(mk), seq(mv), seq(mo), seq(gates),
      jnp.broadcast_to(norm_g[:, None], (M_W, V7X_LANES)))
    return hm.reshape(t, M_W)


_FOX_FEATURES = 3


def _fox_operands(x, c_tile, c_lane, key_side):
    rows = x.shape[0]
    nf = _FOX_FEATURES
    first = 0 if key_side else nf
    src = lax.broadcasted_iota(I32, (V7X_LANES, V7X_LANES), 0) - c_lane
    dst = lax.broadcasted_iota(I32, (V7X_LANES, V7X_LANES), 1)
    feats = jnp.zeros((rows, V7X_LANES), F32)
    for n, part in enumerate(_split3(c_tile)):
        place = jnp.logical_or(jnp.logical_and(src == 0, dst == F_DH + first + n),
                               jnp.logical_and(src == 1, dst == first + n))
        feats = feats + _dot(part, place.astype(BF16))
    lane = lax.broadcasted_iota(I32, (rows, V7X_LANES), 1)
    within = lane % F_DH
    const_lanes = jnp.logical_and(within >= nf - first, within < 2 * nf - first)
    feats = jnp.where(const_lanes, 1.0 if key_side else -1.0, feats).astype(BF16)
    low = lane < F_DH
    return jnp.where(low, x, feats), jnp.where(low, feats, x)


def _fox_kernel(q_ref, k_ref, v_ref, ccol_ref, crel_k_ref, crel_q_ref, o_ref, qaug_ref, kaug_ref, vt_ref,
                st_a0, st_a1, st_b0, st_b1, pe_a0, pe_a1, pe_b0, pe_b1, acc0, acc1):
    blk = FOX_K_BLOCK
    tq = FOX_Q_BLOCK
    assert tq == 2 * blk
    strips = tq // V7X_LANES
    st_a_refs, st_b_refs = (st_a0, st_a1), (st_b0, st_b1)
    pe_a_refs, pe_b_refs = (pe_a0, pe_a1), (pe_b0, pe_b1)
    acc_refs = (acc0, acc1)
    p = pl.program_id(1)
    i = pl.program_id(2)

    @pl.when(i == 0)
    def _():
        c_lane = FF_LANE + 2 * p
        qaug_ref[0], qaug_ref[1] = _fox_operands(q_ref[...], crel_q_ref[...], c_lane, False)
        kaug_ref[0], kaug_ref[1] = _fox_operands(k_ref[...], crel_k_ref[...], c_lane, True)
        v_t = v_ref[...].astype(F32).T
        for j in range(vt_ref.shape[0]):
            vt_ref[j] = v_t[:, j * blk:(j + 1) * blk].astype(BF16)

    q_start = pl.multiple_of(i * tq, tq)
    q_heads = [qaug_ref[hh, pl.ds(q_start, tq), :] for hh in range(2)]
    key_row = lax.broadcasted_iota(I32, (blk, V7X_LANES), 0)
    query_col = lax.broadcasted_iota(I32, (blk, V7X_LANES), 1)
    last_chunk = 2 * i + 1
    head_lane = lax.broadcasted_iota(I32, (1, GATE_COLS), 1) - (FF_LANE + 2 * p)

    def c_before(position, hh):
        row = ccol_ref[pl.ds(jnp.maximum(position - 1, 0), 1), :]
        keep = jnp.logical_and(head_lane == hh, position > 0)
        return jnp.sum(jnp.where(keep, row, 0.0), axis=-1, keepdims=True)

    c_query0 = [c_before(q_start, hh) for hh in range(2)]

    def put_scores(st_ref, j, hh):
        start = pl.multiple_of(j * blk, blk)
        st = _dot_nt(kaug_ref[hh, pl.ds(start, blk), :], q_heads[hh])
        for c in range(strips):
            st_ref[c] = st[:, c * V7X_LANES:(c + 1) * V7X_LANES]

    def exponentials(pe_ref):
        return jnp.concatenate([pe_ref[c] for c in range(strips)], axis=1)

    def softmax_update(st_ref, pe_ref, m_old, l_old, base, key_minus_query=None):
        alphas, ms, ls = [], [], []
        for c in range(strips):
            cols = slice(c * V7X_LANES, (c + 1) * V7X_LANES)
            gap = None if key_minus_query is None else key_minus_query - c * V7X_LANES
            if gap is not None and gap - (V7X_LANES - 1) > 0:
                pe_ref[c] = jnp.zeros((blk, V7X_LANES), BF16)
                alphas.append(jnp.ones((1, V7X_LANES), F32))
                ms.append(m_old[:, cols])
                ls.append(l_old[:, cols])
                continue
            st = st_ref[c]
            if gap is not None and gap + (blk - 1) > 0:
                st = jnp.where(key_row + gap <= query_col, st, -jnp.inf)
            m_new = jnp.maximum(m_old[:, cols], jnp.max(st, axis=0, keepdims=True) + base)
            alpha = jnp.exp2(m_old[:, cols] - m_new)
            pe = jnp.exp2(st - (m_new - base))
            pe_ref[c] = pe.astype(BF16)
            alphas.append(alpha)
            ms.append(m_new)
            ls.append(alpha * l_old[:, cols] + jnp.sum(pe, axis=0, keepdims=True))
        cat = lambda parts: jnp.concatenate(parts, axis=1)
        return cat(alphas), cat(ms), cat(ls)

    def pair(mi, carry, diagonal=False):
        a = 2 * mi
        b = a + 1
        v_prev = vt_ref[jnp.maximum(a - 1, 0)]
        v_a = vt_ref[a]
        partial = []
        for hh in range(2):
            alpha_prev = carry[hh][0]
            partial.append(alpha_prev * acc_refs[hh][...] + _dot(v_prev, exponentials(pe_b_refs[hh])))
            put_scores(st_b_refs[hh], b, hh)
        stats = []
        for hh in range(2):
            _, m_old, l_old = carry[hh]
            stats.append(softmax_update(st_a_refs[hh], pe_a_refs[hh], m_old, l_old,
                                        c_query0[hh] - c_before(a * blk, hh), 0 if diagonal else None))
        for hh in range(2):
            alpha_a = stats[hh][0]
            acc_refs[hh][...] = alpha_a * partial[hh] + _dot(v_a, exponentials(pe_a_refs[hh]))
            if not diagonal:
                put_scores(st_a_refs[hh], a + 2, hh)
        return tuple(softmax_update(st_b_refs[hh], pe_b_refs[hh], stats[hh][1], stats[hh][2],
                                    c_query0[hh] - c_before(b * blk, hh), blk if diagonal else None)
                     for hh in range(2))

    for hh in range(2):
        put_scores(st_a_refs[hh], 0, hh)
        pe_b_refs[hh][...] = jnp.zeros(pe_b_refs[hh].shape, BF16)
        acc_refs[hh][...] = jnp.zeros((V7X_LANES, tq), F32)
    init = tuple((jnp.ones((1, tq), F32), jnp.full((1, tq), -jnp.inf, F32), jnp.zeros((1, tq), F32))
                 for _ in range(2))
    final = pair(i, lax.fori_loop(0, i, pair, init), diagonal=True)
    v_last = vt_ref[last_chunk]
    outs = []
    for hh in range(2):
        alpha, _, l_fin = final[hh]
        outs.append((alpha * acc_refs[hh][...] + _dot(v_last, exponentials(pe_b_refs[hh]))) / l_fin)
    row = lax.broadcasted_iota(I32, (V7X_LANES, tq), 0)
    o_t = jnp.where(row < F_DH, outs[0], outs[1])
    o_ref[...] = o_t.T.astype(BF16)


def _fox_attention(fq, fk, fv, ccol, crel_k, crel_q, bsz, s):
    t = fq.shape[0]
    blk = FOX_K_BLOCK
    tq = FOX_Q_BLOCK
    nq = s // tq
    pairs = F_HEADS // 2
    qmap = lambda b, p, i: (b * nq + i, p)
    kvmap = lambda b, p, i: (b, p)
    return pl.pallas_call(
        _fox_kernel,
        grid=(bsz, pairs, nq),
        in_specs=[
            pl.BlockSpec((s, V7X_LANES), kvmap),
            pl.BlockSpec((s, V7X_LANES), kvmap),
            pl.BlockSpec((s, V7X_LANES), kvmap),
            pl.BlockSpec((s, GATE_COLS), lambda b, p, i: (b, 0)),
            pl.BlockSpec((s, GATE_COLS), lambda b, p, i: (b, 0)),
            pl.BlockSpec((s, GATE_COLS), lambda b, p, i: (b, 0)),
        ],
        out_specs=pl.BlockSpec((tq, V7X_LANES), qmap),
        out_shape=jax.ShapeDtypeStruct((t, F_W), BF16),
        scratch_shapes=[
            pltpu.VMEM((2, s, V7X_LANES), BF16),
            pltpu.VMEM((2, s, V7X_LANES), BF16),
            pltpu.VMEM((s // blk, V7X_LANES, blk), BF16),
        ] + [pltpu.VMEM((tq // V7X_LANES, blk, V7X_LANES), F32)] * 4 + [pltpu.VMEM((tq // V7X_LANES, blk, V7X_LANES), BF16)] * 4
          + [pltpu.VMEM((V7X_LANES, tq), F32)] * 2,
        compiler_params=_params(("parallel", "parallel", "arbitrary")),
        name="fox_attention",
    )(fq, fk, fv, ccol, crel_k, crel_q)


def _layer_norm_rows(r, g, b):
    mu = jnp.mean(r, axis=-1, keepdims=True)
    d = r - mu
    var = jnp.mean(d * d, axis=-1, keepdims=True)
    return (d * lax.rsqrt(var + LN_EPS)) * g + b


def _merge_kernel(dn_alpha, hm_ref, hf_ref, x_ref, wg_ref, bg_ref, wbm_ref, wbf_ref, wo_ref,
                  g_ref, b_ref, wrh_ref, wrl_ref, br_ref, h1_ref, h1p_ref, gate_ref, tope_ref, cnt_ref,
                  resid_ref):
    i = pl.program_id(0)
    slot = i % 2

    @pl.when(i == 0)
    def _():
        resid_ref[1] = jnp.zeros(resid_ref.shape[1:], F32)

    subs = [slice(n * MERGE_SUB_ROWS, (n + 1) * MERGE_SUB_ROWS)
            for n in range(x_ref.shape[0] // MERGE_SUB_ROWS)]
    counts = jnp.zeros((1, V7X_LANES), F32)
    for rows in subs:
        counts = counts + _merge_tail(resid_ref[1 - slot, rows, :], rows, g_ref, b_ref, wrh_ref, wrl_ref,
                                      br_ref, h1_ref, h1p_ref, gate_ref, tope_ref)
    sub = lax.broadcasted_iota(I32, cnt_ref.shape, 0)
    cnt_ref[...] = jnp.where(sub == 0, counts, 0.0)

    for rows in subs:
        x = x_ref[rows, :]
        d = x.shape[1]
        gmf = _dot(x.astype(BF16), wg_ref[...]) + bg_ref[...]
        ym = _dot(hm_ref[rows, :], wbm_ref[...])
        yf = _dot(hf_ref[rows, :], wbf_ref[...])
        y = _sigmoid(gmf[:, :d]) * ym + _sigmoid(gmf[:, d:]) * yf
        resid_ref[slot, rows, :] = dn_alpha * x + _dot(y.astype(BF16), wo_ref[...])


def _merge_tail(resid, rows, g_ref, b_ref, wrh_ref, wrl_ref, br_ref, h1_ref, h1p_ref, gate_ref, tope_ref):
    h1 = _layer_norm_rows(resid, g_ref[...], b_ref[...])
    h1_ref[rows, :] = h1

    h1p_ref[rows, :] = _pack_bf16_pairs(h1)
    hb = h1.astype(BF16)

    lo = (h1 - hb.astype(F32)).astype(BF16)
    logits = (_dot(lo, wrh_ref[...]) + _dot(hb, wrl_ref[...])) + _dot(hb, wrh_ref[...]) + br_ref[...]
    tm = logits.shape[0]
    lane = lax.broadcasted_iota(I32, (tm, V7X_LANES), 1)
    vals = jnp.where(lane < N_EXPERTS, logits, -jnp.inf)
    top_v, top_i = [], []
    for _ in range(TOP_K):
        mx = jnp.max(vals, axis=-1, keepdims=True)
        idx = jnp.min(jnp.where(vals == mx, lane, V7X_LANES), axis=-1, keepdims=True)
        top_v.append(mx)
        top_i.append(idx)
        vals = jnp.where(lane == idx, -jnp.inf, vals)
    ex = [jnp.exp(v - top_v[0]) for v in top_v]
    tot = ex[0]
    for e in ex[1:]:
        tot = tot + e
    gate = jnp.zeros((tm, V7X_LANES), F32)
    tope = jnp.zeros((tm, V7X_LANES), I32)
    member = jnp.zeros((tm, V7X_LANES), F32)
    for k in range(TOP_K):
        gate = jnp.where(lane == k, ex[k] / tot, gate)
        tope = jnp.where(lane == k, top_i[k], tope)
        member = member + (lane == top_i[k]).astype(F32)
    gate_ref[rows, :] = gate
    tope_ref[rows, :] = tope
    return jnp.sum(member, axis=0, keepdims=True)


def _merge(dn_alpha, hm, hf, x2, wg, bg, wbm, wbf, wo, ln_g, ln_b, wr_hi, wr_lo, br):
    t, d = x2.shape
    tm = MERGE_ROWS
    nt = t // tm
    row = lambda i: (jnp.minimum(i, nt - 1), 0)
    out_row = lambda i: (jnp.maximum(i - 1, 0), 0)
    const = lambda i: (0, 0)
    full = lambda a: pl.BlockSpec(a.shape, const)
    return pl.pallas_call(
        functools.partial(_merge_kernel, dn_alpha),
        grid=(nt + 1,),
        in_specs=[
            pl.BlockSpec((tm, M_W), row),
            pl.BlockSpec((tm, F_W), row),
            pl.BlockSpec((tm, d), row),
            full(wg), full(bg), full(wbm), full(wbf), full(wo), full(ln_g), full(ln_b),
            full(wr_hi), full(wr_lo), full(br),
        ],
        out_specs=(
            pl.BlockSpec((tm, d), out_row),
            pl.BlockSpec((tm, d // 2), out_row),
            pl.BlockSpec((tm, V7X_LANES), out_row),
            pl.BlockSpec((tm, V7X_LANES), out_row),
            pl.BlockSpec((V7X_SUBLANES, V7X_LANES), out_row),
        ),
        out_shape=(
            jax.ShapeDtypeStruct((t, d), F32),
            jax.ShapeDtypeStruct((t, d // 2), U32),
            jax.ShapeDtypeStruct((t, V7X_LANES), F32),
            jax.ShapeDtypeStruct((t, V7X_LANES), I32),
            jax.ShapeDtypeStruct((t // tm * V7X_SUBLANES, V7X_LANES), F32),
        ),
        scratch_shapes=[pltpu.VMEM((2, tm, d), F32)],
        compiler_params=_params(("arbitrary",)),
        name="merge_ln1_router",
    )(hm, hf, x2, wg, bg, wbm, wbf, wo, ln_g, ln_b, wr_hi, wr_lo, br)


def _lane_cumsum(x):
    lane = lax.broadcasted_iota(I32, x.shape, 1)
    d = 1
    while d < V7X_LANES:
        x = x + jnp.where(lane >= d, pltpu.roll(x, shift=d, axis=1), 0.0)
        d *= 2
    return x


def _routing_kernel(cnt_ref, tope_ref, dest_ref, table_ref, run_ref, start_ref):
    sb = ROUTE_SUB_ROWS

    @pl.when(pl.program_id(0) == 0)
    def _():
        total = jnp.sum(cnt_ref[...], axis=0, keepdims=True)
        counts = jnp.broadcast_to(total, (V7X_SUBLANES, V7X_LANES))
        padded = jnp.ceil(counts * (1.0 / MOE_BLOCK)) * MOE_BLOCK
        pad_end = _lane_cumsum(padded)
        pad_start = pad_end - padded
        start_ref[...] = pad_start
        run_ref[...] = jnp.zeros(run_ref.shape, F32)
        nb = table_ref.shape[0]
        blk = lax.broadcasted_iota(I32, (nb, V7X_LANES), 0).astype(F32) * MOE_BLOCK
        ln = lax.broadcasted_iota(I32, (nb, V7X_LANES), 1)
        done = jnp.logical_and(pad_end[0:1, :] <= blk, ln < N_EXPERTS)
        be = jnp.minimum(jnp.sum(done.astype(F32), axis=-1, keepdims=True), N_EXPERTS - 1.0)
        onehot = ln == be.astype(I32)
        cnt_e = jnp.sum(jnp.where(onehot, counts[0:1, :], 0.0), axis=-1, keepdims=True)
        start_e = jnp.sum(jnp.where(onehot, pad_start[0:1, :], 0.0), axis=-1, keepdims=True)
        valid = jnp.clip(cnt_e - (blk[:, 0:1] - start_e), 0.0, float(MOE_BLOCK))
        table_ref[...] = jnp.where(ln == 0, be.astype(I32),
                                   jnp.where(ln == 1, valid.astype(I32), 0))

    earlier = _tril_mask(sb, strict=True).astype(BF16)
    lane = lax.broadcasted_iota(I32, (sb, V7X_LANES), 1)
    offset = run_ref[0:1, :] + start_ref[0:1, :]
    seen = jnp.zeros((1, V7X_LANES), F32)
    for j in range(tope_ref.shape[0] // sb):
        tope = tope_ref[j * sb:(j + 1) * sb, :]
        hit = [lane == tope[:, k:k + 1] for k in range(TOP_K)]
        member = jnp.zeros((sb, V7X_LANES), F32)
        for k in range(TOP_K):
            member = member + hit[k].astype(F32)
        base = _dot(earlier, member.astype(BF16)) + (offset + seen)
        dest = jnp.zeros((sb, V7X_LANES), F32)
        for k in range(TOP_K):
            dk = jnp.sum(jnp.where(hit[k], base, 0.0), axis=-1, keepdims=True)
            dest = jnp.where(lane == k, dk, dest)
        dest_ref[:, j * sb:(j + 1) * sb] = dest.T[0:V7X_SUBLANES, :].astype(I32)
        seen = seen + jnp.sum(member, axis=0, keepdims=True)
    run_ref[...] = run_ref[...] + seen


def _routing(tile_counts, tope, n_blocks):
    t = tope.shape[0]
    tr = ROUTE_ROWS
    return pl.pallas_call(
        _routing_kernel,
        grid=(t // tr,),
        in_specs=[pl.BlockSpec(tile_counts.shape, lambda i: (0, 0)),
                  pl.BlockSpec((tr, V7X_LANES), lambda i: (i, 0))],
        out_specs=(
            pl.BlockSpec((V7X_SUBLANES, tr), lambda i: (0, i)),
            pl.BlockSpec((n_blocks, V7X_LANES), lambda i: (0, 0)),
        ),
        out_shape=(
            jax.ShapeDtypeStruct((V7X_SUBLANES, t), I32),
            jax.ShapeDtypeStruct((n_blocks, V7X_LANES), I32),
        ),
        scratch_shapes=[
            pltpu.VMEM((V7X_SUBLANES, V7X_LANES), F32),
            pltpu.VMEM((V7X_SUBLANES, V7X_LANES), F32),
        ],
        compiler_params=_params(("arbitrary",)),
        name="routing",
    )(tile_counts, tope)


def _sc_worker_id():
    return lax.axis_index("s") * V7X_SC_CORES + lax.axis_index("c")


def _sc_mesh():
    return plsc.VectorSubcoreMesh(core_axis_name="c", subcore_axis_name="s",
                                  num_cores=V7X_SC_CORES, num_subcores=V7X_SC_SUBCORES)


def _sc_dispatch(dest_km, h1p, n_rows):
    t, w = h1p.shape
    per_worker = t // V7X_SC_WORKERS
    ch = SC_SCATTER_ROWS

    @functools.partial(
        pl.kernel, mesh=_sc_mesh(),
        out_type=jax.ShapeDtypeStruct((n_rows, w), h1p.dtype),
        scratch_types=[pltpu.VMEM((ch, w), h1p.dtype)]
        + [pltpu.VMEM((ch,), I32)] * TOP_K + [pltpu.SemaphoreType.DMA] * TOP_K,
        name="sc_dispatch",
    )
    def scatter_rows(dest_hbm, h1p_hbm, xs_hbm, rows_v, *idx_and_sems):
        idx_refs, sems = idx_and_sems[:TOP_K], idx_and_sems[TOP_K:]
        first = _sc_worker_id() * per_worker

        @pl.loop(0, per_worker // ch)
        def _(j):
            base = first + j * ch
            pltpu.sync_copy(h1p_hbm.at[pl.ds(base, ch)], rows_v)
            copies = []
            for k in range(TOP_K):
                pltpu.sync_copy(dest_hbm.at[pl.ds(k * t + base, ch)], idx_refs[k])
                copies.append(pltpu.async_copy(rows_v, xs_hbm.at[idx_refs[k]], sems[k]))
            for copy in copies:
                copy.wait()

    return scatter_rows(dest_km, h1p)


def _sc_gather(dest_km, y_rows):
    n = dest_km.shape[0]
    w = y_rows.shape[1]
    per_worker = n // V7X_SC_WORKERS
    ch = SC_GATHER_ROWS
    ways = SC_GATHER_WAYS

    @functools.partial(
        pl.kernel, mesh=_sc_mesh(),
        out_type=jax.ShapeDtypeStruct((n, w), y_rows.dtype),
        scratch_types=[pltpu.VMEM((ch,), I32)] * ways + [pltpu.VMEM((ch, w), y_rows.dtype)] * ways
        + [pltpu.SemaphoreType.DMA] * (2 * ways),
        name="sc_gather",
    )
    def gather_rows(dest_hbm, y_hbm, out_hbm, *scratch):
        idx_refs, row_refs = scratch[:ways], scratch[ways:2 * ways]
        gather_sems, store_sems = scratch[2 * ways:3 * ways], scratch[3 * ways:]
        first = _sc_worker_id() * per_worker

        @pl.loop(0, per_worker // (ch * ways))
        def _(j):
            bases = [first + (j * ways + u) * ch for u in range(ways)]
            gathers = []
            for u in range(ways):
                pltpu.sync_copy(dest_hbm.at[pl.ds(bases[u], ch)], idx_refs[u])
                gathers.append(pltpu.async_copy(y_hbm.at[idx_refs[u]], row_refs[u], gather_sems[u]))
            stores = []
            for u in range(ways):
                gathers[u].wait()
                stores.append(pltpu.async_copy(row_refs[u], out_hbm.at[pl.ds(bases[u], ch)], store_sems[u]))
            for store in stores:
                store.wait()

    return gather_rows(dest_km, y_rows)


def _expert_kernel(be_ref, nv_ref, xs_ref, wgu_f32_ref, bgu_ref, wdn_f32_ref, bdn_ref, y_ref,
                   wgu_slots, wdn_slots, slot_ref):
    i = pl.program_id(0)
    n_blocks = pl.num_programs(0) - 1
    nv = jnp.where(i > 0, nv_ref[jnp.maximum(i - 1, 0)], 0)
    half = MOE_BLOCK // 2

    @pl.when(i == 0)
    def _():
        slot_ref[0] = 0

    slot = slot_ref[0]
    wgu_ref = wgu_slots.at[slot]
    wdn_ref = wdn_slots.at[slot]

    def ffn(rows):
        x = _unpack_bf16_pairs(xs_ref[rows, :])
        rowid = rows.start + lax.broadcasted_iota(I32, x.shape, 0)
        x = jnp.where(rowid < nv, x, 0.0).astype(BF16)
        gu = _dot(x, wgu_ref[...]) + bgu_ref[...]
        f = gu.shape[1] // 2
        glu = jnp.minimum(gu[:, :f], SWIGLU_LIMIT)
        lin = jnp.clip(gu[:, f:], -SWIGLU_LIMIT, SWIGLU_LIMIT)
        act = glu * _sigmoid(SWIGLU_ALPHA * glu) * (lin + 1.0)
        y_ref[rows, :] = _pack_bf16_pairs(_dot(act.astype(BF16), wdn_ref[...]) + bdn_ref[...])

    @pl.when(nv == 0)
    def _():
        y_ref[...] = jnp.zeros(y_ref.shape, U32)

    @pl.when(jnp.logical_and(nv > 0, nv <= half))
    def _():
        ffn(slice(0, half))
        y_ref[half:, :] = jnp.zeros((MOE_BLOCK - half, y_ref.shape[1]), U32)

    @pl.when(nv > half)
    def _():
        ffn(slice(0, half))
        ffn(slice(half, MOE_BLOCK))

    new_expert = jnp.logical_or(i == 0, be_ref[jnp.minimum(i, n_blocks - 1)] != be_ref[jnp.maximum(i - 1, 0)])

    @pl.when(jnp.logical_and(i < n_blocks, new_expert))
    def _():
        wgu_slots[1 - slot] = wgu_f32_ref[...].astype(BF16)
        wdn_slots[1 - slot] = wdn_f32_ref[...].astype(BF16)
        slot_ref[0] = 1 - slot


def _experts(block_e, block_valid, xs, wgu, bgu, wdn, bdn):
    n_rows, w = xs.shape
    e, d, f2 = wgu.shape
    n_blocks = n_rows // MOE_BLOCK
    ahead = lambda i, be: be[jnp.minimum(i, n_blocks - 1)]
    behind = lambda i: jnp.maximum(i - 1, 0)
    grid_spec = pltpu.PrefetchScalarGridSpec(
        num_scalar_prefetch=2,
        grid=(n_blocks + 1,),
        in_specs=[
            pl.BlockSpec((MOE_BLOCK, w), lambda i, be, nv: (behind(i), 0)),
            pl.BlockSpec((None, d, f2), lambda i, be, nv: (ahead(i, be), 0, 0)),
            pl.BlockSpec((None, 1, f2), lambda i, be, nv: (be[behind(i)], 0, 0)),
            pl.BlockSpec((None, f2 // 2, d), lambda i, be, nv: (ahead(i, be), 0, 0)),
            pl.BlockSpec((None, 1, d), lambda i, be, nv: (be[behind(i)], 0, 0)),
        ],
        out_specs=pl.BlockSpec((MOE_BLOCK, d // 2), lambda i, be, nv: (behind(i), 0)),
        scratch_shapes=[pltpu.VMEM((2, d, f2), BF16), pltpu.VMEM((2, f2 // 2, d), BF16),
                        pltpu.SMEM((1,), I32)],
    )
    return pl.pallas_call(
        _expert_kernel,
        grid_spec=grid_spec,
        out_shape=jax.ShapeDtypeStruct((n_rows, d // 2), U32),
        compiler_params=pltpu.CompilerParams(dimension_semantics=("arbitrary",),
                                             vmem_limit_bytes=EXPERT_VMEM_LIMIT_BYTES),
        name="experts",
    )(block_e, block_valid, xs, wgu, bgu, wdn, bdn)


def _combine_kernel(dn_alpha, h1_ref, gate_ref, g_ref, b_ref, yg_ref, o_ref):
    gate = gate_ref[...]
    ffn = gate[:, 0:1] * _unpack_bf16_pairs(yg_ref[0])
    for k in range(1, TOP_K):
        ffn = ffn + gate[:, k:k + 1] * _unpack_bf16_pairs(yg_ref[k])
    o_ref[...] = _layer_norm_rows(dn_alpha * h1_ref[...] + ffn, g_ref[...], b_ref[...])


def _combine(dn_alpha, h1, gate, ln_g, ln_b, yg):
    t, d = h1.shape
    tc = COMBINE_ROWS
    row = lambda i: (i, 0)
    const = lambda i: (0, 0)
    return pl.pallas_call(
        functools.partial(_combine_kernel, dn_alpha),
        grid=(t // tc,),
        in_specs=[
            pl.BlockSpec((tc, d), row),
            pl.BlockSpec((tc, V7X_LANES), row),
            pl.BlockSpec((1, d), const),
            pl.BlockSpec((1, d), const),
            pl.BlockSpec((TOP_K, tc, d // 2), lambda i: (0, i, 0)),
        ],
        out_specs=pl.BlockSpec((tc, d), row),
        out_shape=jax.ShapeDtypeStruct((t, d), F32),
        compiler_params=_params(("parallel",)),
        name="combine_ln2",
    )(h1, gate, ln_g, ln_b, yg)


def _in_proj_columns(d):
    o = 0
    cols = {}
    for name, width in (("mqk", MQK_W), ("mv", M_W), ("mo", M_W), ("mi", M_HEADS), ("mf", M_HEADS),
                        ("fq", F_W), ("fk", F_W), ("fv", F_W), ("ff", F_HEADS), ("gm", d), ("gf", d)):
        cols[name] = (o, o + width)
        o += width
    return cols


def _pack_w_kernel(w_ref, main_ref, gate_ref):
    d = gate_ref.shape[1] // 2
    cols = _in_proj_columns(d)
    out = 0
    for name in ("mqk", "mv", "mo", "fq", "fk", "fv"):
        lo, hi = cols[name]
        main_ref[:, out:out + hi - lo] = w_ref[:, lo:hi].astype(BF16)
        out += hi - lo
    lane = lax.broadcasted_iota(I32, (w_ref.shape[0], GATE_COLS), 1)
    mi_lo, ff_lo = cols["mi"][0], cols["ff"][0] - 2 * M_HEADS
    assert mi_lo % V7X_LANES == 0 and ff_lo % V7X_LANES == 0 and cols["mf"][0] == mi_lo + M_HEADS
    gates = jnp.where(lane < 2 * M_HEADS, w_ref[:, mi_lo:mi_lo + GATE_COLS],
                      jnp.where(lane < 2 * M_HEADS + F_HEADS, w_ref[:, ff_lo:ff_lo + GATE_COLS], 0.0))
    main_ref[:, out:out + GATE_COLS] = gates.astype(BF16)
    gate_ref[...] = w_ref[:, cols["gm"][0]:cols["gf"][1]].astype(BF16)


def _pack_in_proj(w_in_layers, layer, b_in):
    _, d, n_cols = w_in_layers.shape
    cols = _in_proj_columns(d)
    tr = V7X_LANES
    w_main, w_gate = pl.pallas_call(
        _pack_w_kernel,
        grid=(d // tr,),
        in_specs=[pl.BlockSpec((None, tr, n_cols), lambda i: (layer, i, 0))],
        out_specs=(pl.BlockSpec((tr, _PACKED_COLS), lambda i: (i, 0)),
                   pl.BlockSpec((tr, 2 * d), lambda i: (i, 0))),
        out_shape=(jax.ShapeDtypeStruct((d, _PACKED_COLS), BF16),
                   jax.ShapeDtypeStruct((d, 2 * d), BF16)),
        compiler_params=_params(("parallel",)),
        name="pack_in_proj",
    )(w_in_layers)

    def take(names):
        return [b_in[cols[n][0]:cols[n][1]] for n in names]

    n_gate = 2 * M_HEADS + F_HEADS
    b_main = jnp.concatenate(take(("mqk", "mv", "mo", "fq", "fk", "fv", "mi", "mf", "ff"))
                             + [jnp.zeros((GATE_COLS - n_gate,), b_in.dtype)])
    b_gate = jnp.concatenate(take(("gm", "gf")))
    return w_main, b_main[None, :], w_gate, b_gate[None, :]


def _layer(h, depth, layer, w_in_layers, b_in, m_conv_w, m_conv_b, m_norm_g, w_bm, w_bf, w_o, ln1_g, ln1_b,
           w_router, b_router, w_gu, b_gu, w_dn, b_dn, ln2_g, ln2_b):
    bsz, s, d = h.shape
    t = bsz * s
    dn_alpha = (2.0 * depth) ** 0.25
    x2 = h.reshape(t, d)
    assert d == MQK_W and w_gu.shape[0] == N_EXPERTS, "kernels are written for this layer geometry"
    assert s % INPROJ_ROWS == 0 and s % MLSTM_CHUNK == 0 and s % FOX_Q_BLOCK == 0
    assert bsz % MLSTM_SEQS == 0
    assert t % MERGE_ROWS == 0 and t % ROUTE_ROWS == 0 and t % COMBINE_ROWS == 0
    assert t % (V7X_SC_WORKERS * SC_SCATTER_ROWS) == 0
    assert (t * TOP_K) % (V7X_SC_WORKERS * SC_GATHER_ROWS * SC_GATHER_WAYS) == 0

    w_main, b_main, w_gate, b_gate = _pack_in_proj(w_in_layers, layer, b_in)
    mq, mk, mv, mo, fq, fk, fv, gates = _inproj(x2, w_main, b_main, m_conv_w, m_conv_b[None, :], s)
    ccol, crel_k, crel_q = _fox_cumsum(gates, bsz, s)
    hm = _mlstm(mq, mk, mv, mo, gates, m_norm_g, bsz, s)
    hf = _fox_attention(fq, fk, fv, ccol, crel_k, crel_q, bsz, s)

    n_exp = w_router.shape[1]
    wr = jnp.zeros((d, V7X_LANES), F32).at[:, :n_exp].set(w_router)
    wr_hi = wr.astype(BF16)
    wr_lo = (wr - wr_hi.astype(F32)).astype(BF16)
    br = jnp.zeros((1, V7X_LANES), F32).at[0, :n_exp].set(b_router)
    h1, h1p, gate, tope, tile_counts = _merge(
        dn_alpha, hm, hf, x2, w_gate, b_gate, w_bm.astype(BF16), w_bf.astype(BF16), w_o.astype(BF16),
        ln1_g[None, :], ln1_b[None, :], wr_hi, wr_lo, br)

    n_blocks = -(-(t * TOP_K) // MOE_BLOCK) + N_EXPERTS
    dest, table = _routing(tile_counts, tope, n_blocks)
    dest_km = dest[:TOP_K].reshape(TOP_K * t)
    block_e, block_valid = table[:, 0], table[:, 1]
    xs = _sc_dispatch(dest_km, h1p, n_blocks * MOE_BLOCK)
    y_rows = _experts(block_e, block_valid, xs, w_gu, b_gu[:, None, :], w_dn, b_dn[:, None, :])
    yg = _sc_gather(dest_km, y_rows).reshape(TOP_K, t, d // 2)
    out = _combine(dn_alpha, h1, gate, ln2_g[None, :], ln2_b[None, :], yg)
    return out.reshape(bsz, s, d)


def kernel(x, w_in, b_in, m_conv_w, m_conv_b, m_norm_g, w_bm, w_bf, w_o, ln1_g, ln1_b,
           w_router, b_router, w_gu, b_gu, w_dn, b_dn, ln2_g, ln2_b):
    depth = w_in.shape[0]
    h = x
    for l in range(depth):
        h = _layer(h, depth, l, w_in, b_in[l], m_conv_w[l], m_conv_b[l], m_norm_g[l], w_bm[l], w_bf[l],
                   w_o[l], ln1_g[l], ln1_b[l], w_router[l], b_router[l], w_gu[l], b_gu[l], w_dn[l],
                   b_dn[l], ln2_g[l], ln2_b[l])
    return h
```

```python
import functools

import jax
import jax.numpy as jnp
from jax import lax
from jax.experimental import pallas as pl
from jax.experimental.pallas import tpu as pltpu
from jax.experimental.pallas import tpu_sc as plsc

F32 = jnp.float32
BF16 = jnp.bfloat16
I32 = jnp.int32
U32 = jnp.uint32

M_HEADS = 4
M_DQK = 128
M_DV = 128
CONV_W = 4
F_HEADS = 8
F_DH = 64
N_EXPERTS = 32
TOP_K = 4
SWIGLU_ALPHA = 1.702
SWIGLU_LIMIT = 7.0
LN_EPS = 1e-5
LOG2_E = 1.4426950408889634

M_W = M_HEADS * M_DV
F_W = F_HEADS * F_DH
MQK_W = 2 * M_HEADS * M_DQK

V7X_LANES = 128
V7X_SUBLANES = 8
V7X_VMEM_BYTES = 64 * 1024 * 1024
VMEM_LIMIT_BYTES = (V7X_VMEM_BYTES * 3) // 4
EXPERT_VMEM_LIMIT_BYTES = (V7X_VMEM_BYTES * 7) // 8
V7X_SC_CORES = 2
V7X_SC_SUBCORES = 16
V7X_SC_WORKERS = V7X_SC_CORES * V7X_SC_SUBCORES

INPROJ_ROWS = 512
MLSTM_CHUNK = 256
MLSTM_SEQS = 4
FOX_Q_BLOCK = 512
FOX_K_BLOCK = 256
MERGE_ROWS = 512
MERGE_SUB_ROWS = 256
ROUTE_ROWS = 2048
ROUTE_SUB_ROWS = 256
MOE_BLOCK = 512
SC_SCATTER_ROWS = 128
SC_GATHER_ROWS = 64
SC_GATHER_WAYS = 2
COMBINE_ROWS = 1024

GATE_COLS = V7X_LANES
MI_LANE = 0
MF_LANE = M_HEADS
FF_LANE = 2 * M_HEADS


def _params(semantics):
    return pltpu.CompilerParams(dimension_semantics=semantics, vmem_limit_bytes=VMEM_LIMIT_BYTES)


def _log_sigmoid(x):
    return jnp.minimum(x, 0.0) - jnp.log1p(jnp.exp(-jnp.abs(x)))


def _sigmoid(x):
    return 0.5 * jnp.tanh(0.5 * x) + 0.5


def _dot(a, b):
    return jnp.dot(a, b, preferred_element_type=F32)


def _dot_nt(a, b):
    return lax.dot_general(a, b, (((1,), (1,)), ((), ())), preferred_element_type=F32)


def _split3(x):
    hi = x.astype(BF16)
    r1 = x - hi.astype(F32)
    mid = r1.astype(BF16)
    lo = (r1 - mid.astype(F32)).astype(BF16)
    return hi, mid, lo


def _dot_mask_f32(mask_bf16, x):
    hi, mid, lo = _split3(x)
    return (_dot(mask_bf16, lo) + _dot(mask_bf16, mid)) + _dot(mask_bf16, hi)


def _pack_bf16_pairs(x):
    half = x.shape[1] // 2
    bits = lax.bitcast_convert_type(x.astype(BF16).astype(F32), U32)
    return (bits[:, :half] >> 16) | bits[:, half:]


def _unpack_bf16_pairs(words):
    lo = lax.bitcast_convert_type(words << 16, F32)
    hi = lax.bitcast_convert_type(words & jnp.uint32(0xFFFF0000), F32)
    return jnp.concatenate([lo, hi], axis=1)


def _tril_mask(n, strict=False):
    r = lax.broadcasted_iota(I32, (n, n), 0)
    c = lax.broadcasted_iota(I32, (n, n), 1)
    return (r > c) if strict else (r >= c)


_OFF_MQK = 0
_OFF_MV = _OFF_MQK + MQK_W
_OFF_MO = _OFF_MV + M_W
_OFF_FQ = _OFF_MO + M_W
_OFF_FK = _OFF_FQ + F_W
_OFF_FV = _OFF_FK + F_W
_OFF_GATES = _OFF_FV + F_W
_PACKED_COLS = _OFF_GATES + GATE_COLS


def _inproj_kernel(tiles_per_seq, x_ref, w_ref, b_ref, cw_ref, cb_ref,
                   mq_ref, mk_ref, mv_ref, mo_ref, fq_ref, fk_ref, fv_ref, gates_ref, ext_ref):
    tm = x_ref.shape[0]
    pad = V7X_SUBLANES
    xb = x_ref[...].astype(BF16)

    def seg(lo, width):
        return _dot(xb, w_ref[:, lo:lo + width]) + b_ref[:, lo:lo + width]

    @pl.when(pl.program_id(0) % tiles_per_seq == 0)
    def _():
        ext_ref[0:pad, :] = jnp.zeros((pad, MQK_W), F32)

    def conv_silu(cols):
        ext_ref[pad:pad + tm, cols] = seg(_OFF_MQK + cols.start, cols.stop - cols.start)
        y = cb_ref[:, cols] + cw_ref[CONV_W - 1:CONV_W, cols] * ext_ref[pad:pad + tm, cols]
        for k in range(CONV_W - 1):
            shift = CONV_W - 1 - k
            y = y + cw_ref[k:k + 1, cols] * ext_ref[pad - shift:pad - shift + tm, cols]
        ext_ref[0:pad, cols] = ext_ref[tm:tm + pad, cols]
        return y * _sigmoid(y)

    half = MQK_W // 2
    group = half // 2
    mq_ref[:, :group] = conv_silu(slice(0, group)).astype(BF16)
    mv_ref[...] = seg(_OFF_MV, M_W).astype(BF16)
    mq_ref[:, group:] = conv_silu(slice(group, half)).astype(BF16)
    mo_ref[...] = seg(_OFF_MO, M_W)
    mk_ref[:, :group] = conv_silu(slice(half, half + group)) * (M_DQK ** -0.5)
    fq_ref[...] = (seg(_OFF_FQ, F_W) * (F_DH ** -0.5 * LOG2_E)).astype(BF16)
    mk_ref[:, group:] = conv_silu(slice(half + group, MQK_W)) * (M_DQK ** -0.5)
    fk_ref[...] = seg(_OFF_FK, F_W).astype(BF16)
    fv_ref[...] = seg(_OFF_FV, F_W).astype(BF16)
    gates_ref[...] = seg(_OFF_GATES, GATE_COLS)


def _inproj(x2, w_packed, b_packed, conv_w, conv_b, s):
    t, d = x2.shape
    tm = INPROJ_ROWS
    row = lambda i: (i, 0)
    const = lambda i: (0, 0)
    out_shapes = (
        jax.ShapeDtypeStruct((t, MQK_W // 2), BF16),
        jax.ShapeDtypeStruct((t, MQK_W // 2), F32),
        jax.ShapeDtypeStruct((t, M_W), BF16),
        jax.ShapeDtypeStruct((t, M_W), F32),
        jax.ShapeDtypeStruct((t, F_W), BF16),
        jax.ShapeDtypeStruct((t, F_W), BF16),
        jax.ShapeDtypeStruct((t, F_W), BF16),
        jax.ShapeDtypeStruct((t, GATE_COLS), F32),
    )
    return pl.pallas_call(
        functools.partial(_inproj_kernel, s // tm),
        grid=(t // tm,),
        in_specs=[
            pl.BlockSpec((tm, d), row),
            pl.BlockSpec((d, _PACKED_COLS), const),
            pl.BlockSpec((1, _PACKED_COLS), const),
            pl.BlockSpec((CONV_W, MQK_W), const),
            pl.BlockSpec((1, MQK_W), const),
        ],
        out_specs=tuple(pl.BlockSpec((tm, o.shape[1]), row) for o in out_shapes),
        out_shape=out_shapes,
        scratch_shapes=[pltpu.VMEM((tm + V7X_SUBLANES, MQK_W), F32)],
        compiler_params=_params(("arbitrary",)),
        name="inproj",
    )(x2, w_packed, b_packed, conv_w, conv_b)


def _fox_cumsum_kernel(g_ref, ccol_ref, crel_k_ref, crel_q_ref):
    s = g_ref.shape[0]
    cb = FOX_K_BLOCK
    per_q = FOX_Q_BLOCK // cb
    tri = _tril_mask(cb).astype(BF16)
    carry = jnp.zeros((1, GATE_COLS), F32)
    for j in range(s // cb):
        rows = slice(j * cb, (j + 1) * cb)
        if j % per_q == 0:
            q_carry = carry
        within = _dot_mask_f32(tri, _log_sigmoid(g_ref[rows, :])) * LOG2_E
        crel_k_ref[rows, :] = within
        crel_q_ref[rows, :] = within + (carry - q_carry)
        ccol_ref[rows, :] = within + carry
        carry = carry + within[cb - 1:cb, :]


def _fox_cumsum(gates, bsz, s):
    t = gates.shape[0]
    spec = pl.BlockSpec((s, GATE_COLS), lambda b: (b, 0))
    shape = jax.ShapeDtypeStruct((t, GATE_COLS), F32)
    return pl.pallas_call(
        _fox_cumsum_kernel,
        grid=(bsz,),
        in_specs=[spec],
        out_specs=(spec, spec, spec),
        out_shape=(shape, shape, shape),
        compiler_params=_params(("parallel",)),
        name="fox_cumsum",
    )(gates)


def _mlstm_kernel(mq_ref, mk_ref, mv_ref, mo_ref, gates_ref, ng_ref, hm_ref, state_ref, m_ref):
    @pl.when(pl.program_id(1) == 0)
    def _():
        state_ref[...] = jnp.zeros(state_ref.shape, F32)
        m_ref[...] = jnp.zeros(m_ref.shape, F32)

    seqs = range(mq_ref.shape[0])
    states = [[state_ref[bb, h] for h in range(M_HEADS)] for bb in seqs]
    maxes = [[m_ref[bb, h][0:1, 0:1] for h in range(M_HEADS)] for bb in seqs]
    results = [_mlstm_chunk(mq_ref.at[bb], mk_ref.at[bb], mv_ref.at[bb], mo_ref.at[bb], gates_ref.at[bb],
                            ng_ref, states[bb], maxes[bb]) for bb in seqs]
    for bb in seqs:
        for h, (out_h, state_h, m_h) in enumerate(results[bb]):
            hm_ref[bb, :, h * M_DV:(h + 1) * M_DV] = out_h
            state_ref[bb, h] = state_h
            m_ref[bb, h] = jnp.broadcast_to(m_h, m_ref.shape[2:])


def _mlstm_chunk(mq_ref, mk_ref, mv_ref, mo_ref, gates_ref, ng_ref, states, maxes):
    L = MLSTM_CHUNK
    reps = L // V7X_LANES
    results = []
    gates = gates_ref[...]
    bfull = _dot_mask_f32(_tril_mask(L).astype(BF16), _log_sigmoid(gates))
    b_rows = bfull.T
    z_all = gates - pltpu.roll(bfull, shift=V7X_LANES - (MF_LANE - MI_LANE), axis=1)
    visible = (lax.broadcasted_iota(I32, (L, L), 0) <= lax.broadcasted_iota(I32, (L, L), 1))
    ones_rows = (lax.broadcasted_iota(I32, (M_DV, L), 0) == 0).astype(BF16)

    for h in range(M_HEADS):
        b_row = b_rows[MF_LANE + h:MF_LANE + h + 1, :]
        g_tot = b_row[:, L - 1:L]
        m_prev = maxes[h]
        z = jnp.broadcast_to(z_all[:, MI_LANE + h:MI_LANE + h + 1], (L, V7X_LANES))

        q_h = mq_ref[:, h * M_DQK:(h + 1) * M_DQK]
        k_f = mk_ref[:, h * M_DQK:(h + 1) * M_DQK]
        k_h = k_f.astype(BF16)
        v_t = mv_ref[:, h * M_DV:(h + 1) * M_DV].astype(F32).T.astype(BF16)
        cn_t = states[h]

        dlog = jnp.where(visible, b_row + jnp.tile(z, (1, reps)), -jnp.inf)
        inter_log = b_row + m_prev
        m_t = jnp.maximum(inter_log, jnp.max(dlog, axis=0, keepdims=True))
        w_inter = jnp.exp(inter_log - m_t)
        qkw = _dot_nt(k_h, q_h) * jnp.exp(dlog - m_t)
        qc = _dot_nt(cn_t.astype(BF16), q_h)
        num = w_inter * qc[:M_DV, :] + _dot(v_t, qkw.astype(BF16))
        den = w_inter * qc[M_DV:M_DV + 1, :] + jnp.sum(qkw, axis=0, keepdims=True)
        hh = num / jnp.maximum(jnp.abs(den), jnp.exp(-m_t))

        mu = jnp.mean(hh, axis=0, keepdims=True)
        dv = hh - mu
        var = jnp.mean(dv * dv, axis=0, keepdims=True)
        hn = (dv * lax.rsqrt(var + LN_EPS)) * jnp.tile(ng_ref[h * M_DV:(h + 1) * M_DV, :], (1, reps))
        out_h = (_sigmoid(mo_ref[:, h * M_DV:(h + 1) * M_DV]) * hn.T).astype(BF16)

        a = g_tot + z
        m_new = jnp.maximum(g_tot + m_prev, jnp.max(a, axis=0, keepdims=True)[:, 0:1])
        decay = jnp.exp(g_tot + m_prev - m_new)
        kw = (k_f * jnp.exp(a - m_new)).astype(BF16)
        v_aug = jnp.concatenate([v_t, ones_rows], axis=0)
        results.append((out_h, decay * cn_t + _dot(v_aug, kw), m_new))
    return results


def _mlstm(mq, mk, mv, mo, gates, norm_g, bsz, s):
    t = mq.shape[0]
    L = MLSTM_CHUNK
    nb = MLSTM_SEQS
    seq = lambda a: a.reshape(bsz, s, a.shape[1])
    blk = lambda width: pl.BlockSpec((nb, L, width), lambda g, n: (g, n, 0))
    const = lambda g, n: (0, 0)
    hm = pl.pallas_call(
        _mlstm_kernel,
        grid=(bsz // nb, s // L),
        in_specs=[
            blk(MQK_W // 2), blk(MQK_W // 2), blk(M_W), blk(M_W), blk(GATE_COLS),
            pl.BlockSpec((M_W, V7X_LANES), const),
        ],
        out_specs=blk(M_W),
        out_shape=jax.ShapeDtypeStruct((bsz, s, M_W), BF16),
        scratch_shapes=[
            pltpu.VMEM((nb, M_HEADS, 2 * M_DV, M_DQK), F32),
            pltpu.VMEM((nb, M_HEADS, V7X_SUBLANES, V7X_LANES), F32),
        ],
        compiler_params=_params(("parallel", "arbitrary")),
        name="mlstm",
    )(seq(mq), seq(mk), seq(mv), seq(mo), seq(gates),
      jnp.broadcast_to(norm_g[:, None], (M_W, V7X_LANES)))
    return hm.reshape(t, M_W)


_FOX_FEATURES = 3


def _fox_operands(x, c_tile, c_lane, key_side):
    rows = x.shape[0]
    nf = _FOX_FEATURES
    first = 0 if key_side else nf
    src = lax.broadcasted_iota(I32, (V7X_LANES, V7X_LANES), 0) - c_lane
    dst = lax.broadcasted_iota(I32, (V7X_LANES, V7X_LANES), 1)
    feats = jnp.zeros((rows, V7X_LANES), F32)
    for n, part in enumerate(_split3(c_tile)):
        place = jnp.logical_or(jnp.logical_and(src == 0, dst == F_DH + first + n),
                               jnp.logical_and(src == 1, dst == first + n))
        feats = feats + _dot(part, place.astype(BF16))
    lane = lax.broadcasted_iota(I32, (rows, V7X_LANES), 1)
    within = lane % F_DH
    const_lanes = jnp.logical_and(within >= nf - first, within < 2 * nf - first)
    feats = jnp.where(const_lanes, 1.0 if key_side else -1.0, feats).astype(BF16)
    low = lane < F_DH
    return jnp.where(low, x, feats), jnp.where(low, feats, x)


def _fox_kernel(q_ref, k_ref, v_ref, ccol_ref, crel_k_ref, crel_q_ref, o_ref, qaug_ref, kaug_ref, vt_ref,
                *scratch):
    blk = FOX_K_BLOCK
    tq = FOX_Q_BLOCK
    p = pl.program_id(1)
    c_lane = FF_LANE + 2 * p
    qaug_ref[0], qaug_ref[1] = _fox_operands(q_ref[...], crel_q_ref[...], c_lane, False)
    kaug_ref[0], kaug_ref[1] = _fox_operands(k_ref[...], crel_k_ref[...], c_lane, True)
    v_t = v_ref[...].astype(F32).T
    for j in range(vt_ref.shape[0]):
        vt_ref[j] = v_t[:, j * blk:(j + 1) * blk].astype(BF16)
    for i in range(o_ref.shape[0] // tq):
        _fox_query_block(i, p, ccol_ref, o_ref.at[pl.ds(i * tq, tq)], qaug_ref, kaug_ref, vt_ref, *scratch)


def _fox_query_block(i, p, ccol_ref, o_ref, qaug_ref, kaug_ref, vt_ref,
                     st_a0, st_a1, st_b0, st_b1, pe_a0, pe_a1, pe_b0, pe_b1, acc0, acc1):
    blk = FOX_K_BLOCK
    tq = FOX_Q_BLOCK
    assert tq == 2 * blk
    strips = tq // V7X_LANES
    st_a_refs, st_b_refs = (st_a0, st_a1), (st_b0, st_b1)
    pe_a_refs, pe_b_refs = (pe_a0, pe_a1), (pe_b0, pe_b1)
    acc_refs = (acc0, acc1)

    q_start = i * tq
    q_heads = [qaug_ref[hh, pl.ds(q_start, tq), :] for hh in range(2)]
    key_row = lax.broadcasted_iota(I32, (blk, V7X_LANES), 0)
    query_col = lax.broadcasted_iota(I32, (blk, V7X_LANES), 1)
    last_chunk = 2 * i + 1
    head_lane = lax.broadcasted_iota(I32, (1, GATE_COLS), 1) - (FF_LANE + 2 * p)

    def c_before(position, hh):
        row = ccol_ref[pl.ds(jnp.maximum(position - 1, 0), 1), :]
        keep = jnp.logical_and(head_lane == hh, position > 0)
        return jnp.sum(jnp.where(keep, row, 0.0), axis=-1, keepdims=True)

    c_query0 = [c_before(q_start, hh) for hh in range(2)]

    def put_scores(st_ref, j, hh):
        start = j * blk if isinstance(j, int) else pl.multiple_of(j * blk, blk)
        st = _dot_nt(kaug_ref[hh, pl.ds(start, blk), :], q_heads[hh])
        for c in range(strips):
            st_ref[c] = st[:, c * V7X_LANES:(c + 1) * V7X_LANES]

    def exponentials(pe_ref):
        return jnp.concatenate([pe_ref[c] for c in range(strips)], axis=1)

    def softmax_update(st_ref, pe_ref, m_old, l_old, base, key_minus_query=None):
        alphas, ms, ls = [], [], []
        for c in range(strips):
            cols = slice(c * V7X_LANES, (c + 1) * V7X_LANES)
            gap = None if key_minus_query is None else key_minus_query - c * V7X_LANES
            if gap is not None and gap - (V7X_LANES - 1) > 0:
                pe_ref[c] = jnp.zeros((blk, V7X_LANES), BF16)
                alphas.append(jnp.ones((1, V7X_LANES), F32))
                ms.append(m_old[:, cols])
                ls.append(l_old[:, cols])
                continue
            st = st_ref[c]
            if gap is not None and gap + (blk - 1) > 0:
                st = jnp.where(key_row + gap <= query_col, st, -jnp.inf)
            m_new = jnp.maximum(m_old[:, cols], jnp.max(st, axis=0, keepdims=True) + base)
            alpha = jnp.exp2(m_old[:, cols] - m_new)
            pe = jnp.exp2(st - (m_new - base))
            pe_ref[c] = pe.astype(BF16)
            alphas.append(alpha)
            ms.append(m_new)
            ls.append(alpha * l_old[:, cols] + jnp.sum(pe, axis=0, keepdims=True))
        cat = lambda parts: jnp.concatenate(parts, axis=1)
        return cat(alphas), cat(ms), cat(ls)

    def pair(mi, carry, diagonal=False):
        a = 2 * mi
        b = a + 1
        v_prev = vt_ref[jnp.maximum(a - 1, 0)]
        v_a = vt_ref[a]
        partial = []
        for hh in range(2):
            alpha_prev = carry[hh][0]
            partial.append(alpha_prev * acc_refs[hh][...] + _dot(v_prev, exponentials(pe_b_refs[hh])))
            put_scores(st_b_refs[hh], b, hh)
        stats = []
        for hh in range(2):
            _, m_old, l_old = carry[hh]
            stats.append(softmax_update(st_a_refs[hh], pe_a_refs[hh], m_old, l_old,
                                        c_query0[hh] - c_before(a * blk, hh), 0 if diagonal else None))
        for hh in range(2):
            alpha_a = stats[hh][0]
            acc_refs[hh][...] = alpha_a * partial[hh] + _dot(v_a, exponentials(pe_a_refs[hh]))
            if not diagonal:
                put_scores(st_a_refs[hh], a + 2, hh)
        return tuple(softmax_update(st_b_refs[hh], pe_b_refs[hh], stats[hh][1], stats[hh][2],
                                    c_query0[hh] - c_before(b * blk, hh), blk if diagonal else None)
                     for hh in range(2))

    for hh in range(2):
        put_scores(st_a_refs[hh], 0, hh)
        pe_b_refs[hh][...] = jnp.zeros(pe_b_refs[hh].shape, BF16)
        acc_refs[hh][...] = jnp.zeros((V7X_LANES, tq), F32)
    init = tuple((jnp.ones((1, tq), F32), jnp.full((1, tq), -jnp.inf, F32), jnp.zeros((1, tq), F32))
                 for _ in range(2))
    final = pair(i, lax.fori_loop(0, i, pair, init), diagonal=True)
    v_last = vt_ref[last_chunk]
    outs = []
    for hh in range(2):
        alpha, _, l_fin = final[hh]
        outs.append((alpha * acc_refs[hh][...] + _dot(v_last, exponentials(pe_b_refs[hh]))) / l_fin)
    row = lax.broadcasted_iota(I32, (V7X_LANES, tq), 0)
    o_t = jnp.where(row < F_DH, outs[0], outs[1])
    o_ref[...] = o_t.T.astype(BF16)


def _fox_attention(fq, fk, fv, ccol, crel_k, crel_q, bsz, s):
    t = fq.shape[0]
    blk = FOX_K_BLOCK
    tq = FOX_Q_BLOCK
    pairs = F_HEADS // 2
    kvmap = lambda b, p: (b, p)
    return pl.pallas_call(
        _fox_kernel,
        grid=(bsz, pairs),
        in_specs=[
            pl.BlockSpec((s, V7X_LANES), kvmap),
            pl.BlockSpec((s, V7X_LANES), kvmap),
            pl.BlockSpec((s, V7X_LANES), kvmap),
            pl.BlockSpec((s, GATE_COLS), lambda b, p: (b, 0)),
            pl.BlockSpec((s, GATE_COLS), lambda b, p: (b, 0)),
            pl.BlockSpec((s, GATE_COLS), lambda b, p: (b, 0)),
        ],
        out_specs=pl.BlockSpec((s, V7X_LANES), kvmap),
        out_shape=jax.ShapeDtypeStruct((t, F_W), BF16),
        scratch_shapes=[
            pltpu.VMEM((2, s, V7X_LANES), BF16),
            pltpu.VMEM((2, s, V7X_LANES), BF16),
            pltpu.VMEM((s // blk, V7X_LANES, blk), BF16),
        ] + [pltpu.VMEM((tq // V7X_LANES, blk, V7X_LANES), F32)] * 4 + [pltpu.VMEM((tq // V7X_LANES, blk, V7X_LANES), BF16)] * 4
          + [pltpu.VMEM((V7X_LANES, tq), F32)] * 2,
        compiler_params=_params(("parallel", "parallel")),
        name="fox_attention",
    )(fq, fk, fv, ccol, crel_k, crel_q)


def _layer_norm_rows(r, g, b):
    mu = jnp.mean(r, axis=-1, keepdims=True)
    d = r - mu
    var = jnp.mean(d * d, axis=-1, keepdims=True)
    return (d * lax.rsqrt(var + LN_EPS)) * g + b


def _merge_kernel(dn_alpha, hm_ref, hf_ref, x_ref, wg_ref, bg_ref, wbm_ref, wbf_ref, wo_ref,
                  g_ref, b_ref, wrh_ref, wrl_ref, br_ref, h1_ref, h1p_ref, gate_ref, tope_ref, cnt_ref,
                  resid_ref):
    i = pl.program_id(0)
    slot = i % 2

    @pl.when(i == 0)
    def _():
        resid_ref[1] = jnp.zeros(resid_ref.shape[1:], F32)

    subs = [slice(n * MERGE_SUB_ROWS, (n + 1) * MERGE_SUB_ROWS)
            for n in range(x_ref.shape[0] // MERGE_SUB_ROWS)]
    counts = jnp.zeros((1, V7X_LANES), F32)
    for rows in subs:
        counts = counts + _merge_tail(resid_ref[1 - slot, rows, :], rows, g_ref, b_ref, wrh_ref, wrl_ref,
                                      br_ref, h1_ref, h1p_ref, gate_ref, tope_ref)
    sub = lax.broadcasted_iota(I32, cnt_ref.shape, 0)
    cnt_ref[...] = jnp.where(sub == 0, counts, 0.0)

    for rows in subs:
        x = x_ref[rows, :]
        d = x.shape[1]
        gmf = _dot(x.astype(BF16), wg_ref[...]) + bg_ref[...]
        ym = _dot(hm_ref[rows, :], wbm_ref[...])
        yf = _dot(hf_ref[rows, :], wbf_ref[...])
        y = _sigmoid(gmf[:, :d]) * ym + _sigmoid(gmf[:, d:]) * yf
        resid_ref[slot, rows, :] = dn_alpha * x + _dot(y.astype(BF16), wo_ref[...])


def _merge_tail(resid, rows, g_ref, b_ref, wrh_ref, wrl_ref, br_ref, h1_ref, h1p_ref, gate_ref, tope_ref):
    h1 = _layer_norm_rows(resid, g_ref[...], b_ref[...])
    h1_ref[rows, :] = h1

    h1p_ref[rows, :] = _pack_bf16_pairs(h1)
    hb = h1.astype(BF16)

    lo = (h1 - hb.astype(F32)).astype(BF16)
    logits = (_dot(lo, wrh_ref[...]) + _dot(hb, wrl_ref[...])) + _dot(hb, wrh_ref[...]) + br_ref[...]
    tm = logits.shape[0]
    lane = lax.broadcasted_iota(I32, (tm, V7X_LANES), 1)
    vals = jnp.where(lane < N_EXPERTS, logits, -jnp.inf)
    top_v, top_i = [], []
    for _ in range(TOP_K):
        mx = jnp.max(vals, axis=-1, keepdims=True)
        idx = jnp.min(jnp.where(vals == mx, lane, V7X_LANES), axis=-1, keepdims=True)
        top_v.append(mx)
        top_i.append(idx)
        vals = jnp.where(lane == idx, -jnp.inf, vals)
    ex = [jnp.exp(v - top_v[0]) for v in top_v]
    tot = ex[0]
    for e in ex[1:]:
        tot = tot + e
    gate = jnp.zeros((tm, V7X_LANES), F32)
    tope = jnp.zeros((tm, V7X_LANES), I32)
    member = jnp.zeros((tm, V7X_LANES), F32)
    for k in range(TOP_K):
        gate = jnp.where(lane == k, ex[k] / tot, gate)
        tope = jnp.where(lane == k, top_i[k], tope)
        member = member + (lane == top_i[k]).astype(F32)
    gate_ref[rows, :] = gate
    tope_ref[rows, :] = tope
    return jnp.sum(member, axis=0, keepdims=True)


def _merge(dn_alpha, hm, hf, x2, wg, bg, wbm, wbf, wo, ln_g, ln_b, wr_hi, wr_lo, br):
    t, d = x2.shape
    tm = MERGE_ROWS
    nt = t // tm
    row = lambda i: (jnp.minimum(i, nt - 1), 0)
    out_row = lambda i: (jnp.maximum(i - 1, 0), 0)
    const = lambda i: (0, 0)
    full = lambda a: pl.BlockSpec(a.shape, const)
    return pl.pallas_call(
        functools.partial(_merge_kernel, dn_alpha),
        grid=(nt + 1,),
        in_specs=[
            pl.BlockSpec((tm, M_W), row),
            pl.BlockSpec((tm, F_W), row),
            pl.BlockSpec((tm, d), row),
            full(wg), full(bg), full(wbm), full(wbf), full(wo), full(ln_g), full(ln_b),
            full(wr_hi), full(wr_lo), full(br),
        ],
        out_specs=(
            pl.BlockSpec((tm, d), out_row),
            pl.BlockSpec((tm, d // 2), out_row),
            pl.BlockSpec((tm, V7X_LANES), out_row),
            pl.BlockSpec((tm, V7X_LANES), out_row),
            pl.BlockSpec((V7X_SUBLANES, V7X_LANES), out_row),
        ),
        out_shape=(
            jax.ShapeDtypeStruct((t, d), F32),
            jax.ShapeDtypeStruct((t, d // 2), U32),
            jax.ShapeDtypeStruct((t, V7X_LANES), F32),
            jax.ShapeDtypeStruct((t, V7X_LANES), I32),
            jax.ShapeDtypeStruct((t // tm * V7X_SUBLANES, V7X_LANES), F32),
        ),
        scratch_shapes=[pltpu.VMEM((2, tm, d), F32)],
        compiler_params=_params(("arbitrary",)),
        name="merge_ln1_router",
    )(hm, hf, x2, wg, bg, wbm, wbf, wo, ln_g, ln_b, wr_hi, wr_lo, br)


def _lane_cumsum(x):
    lane = lax.broadcasted_iota(I32, x.shape, 1)
    d = 1
    while d < V7X_LANES:
        x = x + jnp.where(lane >= d, pltpu.roll(x, shift=d, axis=1), 0.0)
        d *= 2
    return x


def _routing_kernel(cnt_ref, tope_ref, dest_ref, table_ref, run_ref, start_ref):
    sb = ROUTE_SUB_ROWS

    @pl.when(pl.program_id(0) == 0)
    def _():
        total = jnp.sum(cnt_ref[...], axis=0, keepdims=True)
        counts = jnp.broadcast_to(total, (V7X_SUBLANES, V7X_LANES))
        padded = jnp.ceil(counts * (1.0 / MOE_BLOCK)) * MOE_BLOCK
        pad_end = _lane_cumsum(padded)
        pad_start = pad_end - padded
        start_ref[...] = pad_start
        run_ref[...] = jnp.zeros(run_ref.shape, F32)
        nb = table_ref.shape[0]
        blk = lax.broadcasted_iota(I32, (nb, V7X_LANES), 0).astype(F32) * MOE_BLOCK
        ln = lax.broadcasted_iota(I32, (nb, V7X_LANES), 1)
        done = jnp.logical_and(pad_end[0:1, :] <= blk, ln < N_EXPERTS)
        be = jnp.minimum(jnp.sum(done.astype(F32), axis=-1, keepdims=True), N_EXPERTS - 1.0)
        onehot = ln == be.astype(I32)
        cnt_e = jnp.sum(jnp.where(onehot, counts[0:1, :], 0.0), axis=-1, keepdims=True)
        start_e = jnp.sum(jnp.where(onehot, pad_start[0:1, :], 0.0), axis=-1, keepdims=True)
        valid = jnp.clip(cnt_e - (blk[:, 0:1] - start_e), 0.0, float(MOE_BLOCK))
        table_ref[...] = jnp.where(ln == 0, be.astype(I32),
                                   jnp.where(ln == 1, valid.astype(I32), 0))

    earlier = _tril_mask(sb, strict=True).astype(BF16)
    lane = lax.broadcasted_iota(I32, (sb, V7X_LANES), 1)
    offset = run_ref[0:1, :] + start_ref[0:1, :]
    seen = jnp.zeros((1, V7X_LANES), F32)
    for j in range(tope_ref.shape[0] // sb):
        tope = tope_ref[j * sb:(j + 1) * sb, :]
        hit = [lane == tope[:, k:k + 1] for k in range(TOP_K)]
        member = jnp.zeros((sb, V7X_LANES), F32)
        for k in range(TOP_K):
            member = member + hit[k].astype(F32)
        base = _dot(earlier, member.astype(BF16)) + (offset + seen)
        dest = jnp.zeros((sb, V7X_LANES), F32)
        for k in range(TOP_K):
            dk = jnp.sum(jnp.where(hit[k], base, 0.0), axis=-1, keepdims=True)
            dest = jnp.where(lane == k, dk, dest)
        dest_ref[:, j * sb:(j + 1) * sb] = dest.T[0:V7X_SUBLANES, :].astype(I32)
        seen = seen + jnp.sum(member, axis=0, keepdims=True)
    run_ref[...] = run_ref[...] + seen


def _routing(tile_counts, tope, n_blocks):
    t = tope.shape[0]
    tr = ROUTE_ROWS
    return pl.pallas_call(
        _routing_kernel,
        grid=(t // tr,),
        in_specs=[pl.BlockSpec(tile_counts.shape, lambda i: (0, 0)),
                  pl.BlockSpec((tr, V7X_LANES), lambda i: (i, 0))],
        out_specs=(
            pl.BlockSpec((V7X_SUBLANES, tr), lambda i: (0, i)),
            pl.BlockSpec((n_blocks, V7X_LANES), lambda i: (0, 0)),
        ),
        out_shape=(
            jax.ShapeDtypeStruct((V7X_SUBLANES, t), I32),
            jax.ShapeDtypeStruct((n_blocks, V7X_LANES), I32),
        ),
        scratch_shapes=[
            pltpu.VMEM((V7X_SUBLANES, V7X_LANES), F32),
            pltpu.VMEM((V7X_SUBLANES, V7X_LANES), F32),
        ],
        compiler_params=_params(("arbitrary",)),
        name="routing",
    )(tile_counts, tope)


def _sc_worker_id():
    return lax.axis_index("s") * V7X_SC_CORES + lax.axis_index("c")


def _sc_mesh():
    return plsc.VectorSubcoreMesh(core_axis_name="c", subcore_axis_name="s",
                                  num_cores=V7X_SC_CORES, num_subcores=V7X_SC_SUBCORES)


def _sc_dispatch(dest_km, h1p, n_rows):
    t, w = h1p.shape
    per_worker = t // V7X_SC_WORKERS
    ch = SC_SCATTER_ROWS

    @functools.partial(
        pl.kernel, mesh=_sc_mesh(),
        out_type=jax.ShapeDtypeStruct((n_rows, w), h1p.dtype),
        scratch_types=[pltpu.VMEM((ch, w), h1p.dtype)]
        + [pltpu.VMEM((ch,), I32)] * TOP_K + [pltpu.SemaphoreType.DMA] * TOP_K,
        name="sc_dispatch",
    )
    def scatter_rows(dest_hbm, h1p_hbm, xs_hbm, rows_v, *idx_and_sems):
        idx_refs, sems = idx_and_sems[:TOP_K], idx_and_sems[TOP_K:]
        first = _sc_worker_id() * per_worker

        @pl.loop(0, per_worker // ch)
        def _(j):
            base = first + j * ch
            pltpu.sync_copy(h1p_hbm.at[pl.ds(base, ch)], rows_v)
            copies = []
            for k in range(TOP_K):
                pltpu.sync_copy(dest_hbm.at[pl.ds(k * t + base, ch)], idx_refs[k])
                copies.append(pltpu.async_copy(rows_v, xs_hbm.at[idx_refs[k]], sems[k]))
            for copy in copies:
                copy.wait()

    return scatter_rows(dest_km, h1p)


def _sc_gather(dest_km, y_rows):
    n = dest_km.shape[0]
    w = y_rows.shape[1]
    per_worker = n // V7X_SC_WORKERS
    ch = SC_GATHER_ROWS
    ways = SC_GATHER_WAYS

    @functools.partial(
        pl.kernel, mesh=_sc_mesh(),
        out_type=jax.ShapeDtypeStruct((n, w), y_rows.dtype),
        scratch_types=[pltpu.VMEM((ch,), I32)] * ways + [pltpu.VMEM((ch, w), y_rows.dtype)] * ways
        + [pltpu.SemaphoreType.DMA] * (2 * ways),
        name="sc_gather",
    )
    def gather_rows(dest_hbm, y_hbm, out_hbm, *scratch):
        idx_refs, row_refs = scratch[:ways], scratch[ways:2 * ways]
        gather_sems, store_sems = scratch[2 * ways:3 * ways], scratch[3 * ways:]
        first = _sc_worker_id() * per_worker

        @pl.loop(0, per_worker // (ch * ways))
        def _(j):
            bases = [first + (j * ways + u) * ch for u in range(ways)]
            gathers = []
            for u in range(ways):
                pltpu.sync_copy(dest_hbm.at[pl.ds(bases[u], ch)], idx_refs[u])
                gathers.append(pltpu.async_copy(y_hbm.at[idx_refs[u]], row_refs[u], gather_sems[u]))
            stores = []
            for u in range(ways):
                gathers[u].wait()
                stores.append(pltpu.async_copy(row_refs[u], out_hbm.at[pl.ds(bases[u], ch)], store_sems[u]))
            for store in stores:
                store.wait()

    return gather_rows(dest_km, y_rows)


def _expert_kernel(be_ref, nv_ref, xs_ref, wgu_f32_ref, bgu_ref, wdn_f32_ref, bdn_ref, y_ref,
                   wgu_slots, wdn_slots, slot_ref):
    i = pl.program_id(0)
    n_blocks = pl.num_programs(0) - 1
    nv = jnp.where(i > 0, nv_ref[jnp.maximum(i - 1, 0)], 0)
    half = MOE_BLOCK // 2

    @pl.when(i == 0)
    def _():
        slot_ref[0] = 0

    slot = slot_ref[0]
    wgu_ref = wgu_slots.at[slot]
    wdn_ref = wdn_slots.at[slot]

    def ffn(rows):
        x = _unpack_bf16_pairs(xs_ref[rows, :])
        rowid = rows.start + lax.broadcasted_iota(I32, x.shape, 0)
        x = jnp.where(rowid < nv, x, 0.0).astype(BF16)
        gu = _dot(x, wgu_ref[...]) + bgu_ref[...]
        f = gu.shape[1] // 2
        glu = jnp.minimum(gu[:, :f], SWIGLU_LIMIT)
        lin = jnp.clip(gu[:, f:], -SWIGLU_LIMIT, SWIGLU_LIMIT)
        act = glu * _sigmoid(SWIGLU_ALPHA * glu) * (lin + 1.0)
        y_ref[rows, :] = _pack_bf16_pairs(_dot(act.astype(BF16), wdn_ref[...]) + bdn_ref[...])

    @pl.when(nv == 0)
    def _():
        y_ref[...] = jnp.zeros(y_ref.shape, U32)

    @pl.when(jnp.logical_and(nv > 0, nv <= half))
    def _():
        ffn(slice(0, half))
        y_ref[half:, :] = jnp.zeros((MOE_BLOCK - half, y_ref.shape[1]), U32)

    @pl.when(nv > half)
    def _():
        ffn(slice(0, half))
        ffn(slice(half, MOE_BLOCK))

    new_expert = jnp.logical_or(i == 0, be_ref[jnp.minimum(i, n_blocks - 1)] != be_ref[jnp.maximum(i - 1, 0)])

    @pl.when(jnp.logical_and(i < n_blocks, new_expert))
    def _():
        wgu_slots[1 - slot] = wgu_f32_ref[...].astype(BF16)
        wdn_slots[1 - slot] = wdn_f32_ref[...].astype(BF16)
        slot_ref[0] = 1 - slot


def _experts(block_e, block_valid, xs, wgu, bgu, wdn, bdn):
    n_rows, w = xs.shape
    e, d, f2 = wgu.shape
    n_blocks = n_rows // MOE_BLOCK
    ahead = lambda i, be: be[jnp.minimum(i, n_blocks - 1)]
    behind = lambda i: jnp.maximum(i - 1, 0)
    grid_spec = pltpu.PrefetchScalarGridSpec(
        num_scalar_prefetch=2,
        grid=(n_blocks + 1,),
        in_specs=[
            pl.BlockSpec((MOE_BLOCK, w), lambda i, be, nv: (behind(i), 0)),
            pl.BlockSpec((None, d, f2), lambda i, be, nv: (ahead(i, be), 0, 0)),
            pl.BlockSpec((None, 1, f2), lambda i, be, nv: (be[behind(i)], 0, 0)),
            pl.BlockSpec((None, f2 // 2, d), lambda i, be, nv: (ahead(i, be), 0, 0)),
            pl.BlockSpec((None, 1, d), lambda i, be, nv: (be[behind(i)], 0, 0)),
        ],
        out_specs=pl.BlockSpec((MOE_BLOCK, d // 2), lambda i, be, nv: (behind(i), 0)),
        scratch_shapes=[pltpu.VMEM((2, d, f2), BF16), pltpu.VMEM((2, f2 // 2, d), BF16),
                        pltpu.SMEM((1,), I32)],
    )
    return pl.pallas_call(
        _expert_kernel,
        grid_spec=grid_spec,
        out_shape=jax.ShapeDtypeStruct((n_rows, d // 2), U32),
        compiler_params=pltpu.CompilerParams(dimension_semantics=("arbitrary",),
                                             vmem_limit_bytes=EXPERT_VMEM_LIMIT_BYTES),
        name="experts",
    )(block_e, block_valid, xs, wgu, bgu, wdn, bdn)


def _combine_kernel(dn_alpha, h1_ref, gate_ref, g_ref, b_ref, yg_ref, o_ref):
    gate = gate_ref[...]
    ffn = gate[:, 0:1] * _unpack_bf16_pairs(yg_ref[0])
    for k in range(1, TOP_K):
        ffn = ffn + gate[:, k:k + 1] * _unpack_bf16_pairs(yg_ref[k])
    o_ref[...] = _layer_norm_rows(dn_alpha * h1_ref[...] + ffn, g_ref[...], b_ref[...])


def _combine(dn_alpha, h1, gate, ln_g, ln_b, yg):
    t, d = h1.shape
    tc = COMBINE_ROWS
    row = lambda i: (i, 0)
    const = lambda i: (0, 0)
    return pl.pallas_call(
        functools.partial(_combine_kernel, dn_alpha),
        grid=(t // tc,),
        in_specs=[
            pl.BlockSpec((tc, d), row),
            pl.BlockSpec((tc, V7X_LANES), row),
            pl.BlockSpec((1, d), const),
            pl.BlockSpec((1, d), const),
            pl.BlockSpec((TOP_K, tc, d // 2), lambda i: (0, i, 0)),
        ],
        out_specs=pl.BlockSpec((tc, d), row),
        out_shape=jax.ShapeDtypeStruct((t, d), F32),
        compiler_params=_params(("parallel",)),
        name="combine_ln2",
    )(h1, gate, ln_g, ln_b, yg)


def _in_proj_columns(d):
    o = 0
    cols = {}
    for name, width in (("mqk", MQK_W), ("mv", M_W), ("mo", M_W), ("mi", M_HEADS), ("mf", M_HEADS),
                        ("fq", F_W), ("fk", F_W), ("fv", F_W), ("ff", F_HEADS), ("gm", d), ("gf", d)):
        cols[name] = (o, o + width)
        o += width
    return cols


def _pack_w_kernel(w_ref, main_ref, gate_ref):
    d = gate_ref.shape[1] // 2
    cols = _in_proj_columns(d)
    out = 0
    for name in ("mqk", "mv", "mo", "fq", "fk", "fv"):
        lo, hi = cols[name]
        main_ref[:, out:out + hi - lo] = w_ref[:, lo:hi].astype(BF16)
        out += hi - lo
    lane = lax.broadcasted_iota(I32, (w_ref.shape[0], GATE_COLS), 1)
    mi_lo, ff_lo = cols["mi"][0], cols["ff"][0] - 2 * M_HEADS
    assert mi_lo % V7X_LANES == 0 and ff_lo % V7X_LANES == 0 and cols["mf"][0] == mi_lo + M_HEADS
    gates = jnp.where(lane < 2 * M_HEADS, w_ref[:, mi_lo:mi_lo + GATE_COLS],
                      jnp.where(lane < 2 * M_HEADS + F_HEADS, w_ref[:, ff_lo:ff_lo + GATE_COLS], 0.0))
    main_ref[:, out:out + GATE_COLS] = gates.astype(BF16)
    gate_ref[...] = w_ref[:, cols["gm"][0]:cols["gf"][1]].astype(BF16)


def _pack_in_proj(w_in_layers, layer, b_in):
    _, d, n_cols = w_in_layers.shape
    cols = _in_proj_columns(d)
    tr = V7X_LANES
    w_main, w_gate = pl.pallas_call(
        _pack_w_kernel,
        grid=(d // tr,),
        in_specs=[pl.BlockSpec((None, tr, n_cols), lambda i: (layer, i, 0))],
        out_specs=(pl.BlockSpec((tr, _PACKED_COLS), lambda i: (i, 0)),
                   pl.BlockSpec((tr, 2 * d), lambda i: (i, 0))),
        out_shape=(jax.ShapeDtypeStruct((d, _PACKED_COLS), BF16),
                   jax.ShapeDtypeStruct((d, 2 * d), BF16)),
        compiler_params=_params(("parallel",)),
        name="pack_in_proj",
    )(w_in_layers)

    def take(names):
        return [b_in[cols[n][0]:cols[n][1]] for n in names]

    n_gate = 2 * M_HEADS + F_HEADS
    b_main = jnp.concatenate(take(("mqk", "mv", "mo", "fq", "fk", "fv", "mi", "mf", "ff"))
                             + [jnp.zeros((GATE_COLS - n_gate,), b_in.dtype)])
    b_gate = jnp.concatenate(take(("gm", "gf")))
    return w_main, b_main[None, :], w_gate, b_gate[None, :]


def _layer(h, depth, layer, w_in_layers, b_in, m_conv_w, m_conv_b, m_norm_g, w_bm, w_bf, w_o, ln1_g, ln1_b,
           w_router, b_router, w_gu, b_gu, w_dn, b_dn, ln2_g, ln2_b):
    bsz, s, d = h.shape
    t = bsz * s
    dn_alpha = (2.0 * depth) ** 0.25
    x2 = h.reshape(t, d)
    assert d == MQK_W and w_gu.shape[0] == N_EXPERTS, "kernels are written for this layer geometry"
    assert s % INPROJ_ROWS == 0 and s % MLSTM_CHUNK == 0 and s % FOX_Q_BLOCK == 0
    assert bsz % MLSTM_SEQS == 0
    assert t % MERGE_ROWS == 0 and t % ROUTE_ROWS == 0 and t % COMBINE_ROWS == 0
    assert t % (V7X_SC_WORKERS * SC_SCATTER_ROWS) == 0
    assert (t * TOP_K) % (V7X_SC_WORKERS * SC_GATHER_ROWS * SC_GATHER_WAYS) == 0

    w_main, b_main, w_gate, b_gate = _pack_in_proj(w_in_layers, layer, b_in)
    mq, mk, mv, mo, fq, fk, fv, gates = _inproj(x2, w_main, b_main, m_conv_w, m_conv_b[None, :], s)
    ccol, crel_k, crel_q = _fox_cumsum(gates, bsz, s)
    hm = _mlstm(mq, mk, mv, mo, gates, m_norm_g, bsz, s)
    hf = _fox_attention(fq, fk, fv, ccol, crel_k, crel_q, bsz, s)

    n_exp = w_router.shape[1]
    wr = jnp.zeros((d, V7X_LANES), F32).at[:, :n_exp].set(w_router)
    wr_hi = wr.astype(BF16)
    wr_lo = (wr - wr_hi.astype(F32)).astype(BF16)
    br = jnp.zeros((1, V7X_LANES), F32).at[0, :n_exp].set(b_router)
    h1, h1p, gate, tope, tile_counts = _merge(
        dn_alpha, hm, hf, x2, w_gate, b_gate, w_bm.astype(BF16), w_bf.astype(BF16), w_o.astype(BF16),
        ln1_g[None, :], ln1_b[None, :], wr_hi, wr_lo, br)

    n_blocks = -(-(t * TOP_K) // MOE_BLOCK) + N_EXPERTS
    dest, table = _routing(tile_counts, tope, n_blocks)
    dest_km = dest[:TOP_K].reshape(TOP_K * t)
    block_e, block_valid = table[:, 0], table[:, 1]
    xs = _sc_dispatch(dest_km, h1p, n_blocks * MOE_BLOCK)
    y_rows = _experts(block_e, block_valid, xs, w_gu, b_gu[:, None, :], w_dn, b_dn[:, None, :])
    yg = _sc_gather(dest_km, y_rows).reshape(TOP_K, t, d // 2)
    out = _combine(dn_alpha, h1, gate, ln2_g[None, :], ln2_b[None, :], yg)
    return out.reshape(bsz, s, d)


def kernel(x, w_in, b_in, m_conv_w, m_conv_b, m_norm_g, w_bm, w_bf, w_o, ln1_g, ln1_b,
           w_router, b_router, w_gu, b_gu, w_dn, b_dn, ln2_g, ln2_b):
    depth = w_in.shape[0]
    h = x
    for l in range(depth):
        h = _layer(h, depth, l, w_in, b_in[l], m_conv_w[l], m_conv_b[l], m_norm_g[l], w_bm[l], w_bf[l],
                   w_o[l], ln1_g[l], ln1_b[l], w_router[l], b_router[l], w_gu[l], b_gu[l], w_dn[l],
                   b_dn[l], ln2_g[l], ln2_b[l])
    return h
```

```python
import functools

import jax
import jax.numpy as jnp
from jax import lax
from jax.experimental import pallas as pl
from jax.experimental.pallas import tpu as pltpu
from jax.experimental.pallas import tpu_sc as plsc

F32 = jnp.float32
BF16 = jnp.bfloat16
I32 = jnp.int32
U32 = jnp.uint32

M_HEADS = 4
M_DQK = 128
M_DV = 128
CONV_W = 4
F_HEADS = 8
F_DH = 64
N_EXPERTS = 32
TOP_K = 4
SWIGLU_ALPHA = 1.702
SWIGLU_LIMIT = 7.0
LN_EPS = 1e-5
LOG2_E = 1.4426950408889634

M_W = M_HEADS * M_DV
F_W = F_HEADS * F_DH
MQK_W = 2 * M_HEADS * M_DQK

V7X_LANES = 128
V7X_SUBLANES = 8
V7X_VMEM_BYTES = 64 * 1024 * 1024
VMEM_LIMIT_BYTES = (V7X_VMEM_BYTES * 3) // 4
EXPERT_VMEM_LIMIT_BYTES = (V7X_VMEM_BYTES * 7) // 8
V7X_SC_CORES = 2
V7X_SC_SUBCORES = 16
V7X_SC_WORKERS = V7X_SC_CORES * V7X_SC_SUBCORES

INPROJ_ROWS = 512
MLSTM_CHUNK = 256
MLSTM_SEQS = 4
FOX_Q_BLOCK = 512
FOX_K_BLOCK = 256
MERGE_ROWS = 512
MERGE_SUB_ROWS = 256
ROUTE_ROWS = 2048
ROUTE_SUB_ROWS = 256
MOE_BLOCK = 512
SC_SCATTER_ROWS = 128
SC_GATHER_ROWS = 64
SC_GATHER_WAYS = 2
COMBINE_ROWS = 1024

GATE_COLS = V7X_LANES
MI_LANE = 0
MF_LANE = M_HEADS
FF_LANE = 2 * M_HEADS


def _params(semantics):
    return pltpu.CompilerParams(dimension_semantics=semantics, vmem_limit_bytes=VMEM_LIMIT_BYTES)


def _log_sigmoid(x):
    return jnp.minimum(x, 0.0) - jnp.log1p(jnp.exp(-jnp.abs(x)))


def _sigmoid(x):
    return 0.5 * jnp.tanh(0.5 * x) + 0.5


def _dot(a, b):
    return jnp.dot(a, b, preferred_element_type=F32)


def _dot_nt(a, b):
    return lax.dot_general(a, b, (((1,), (1,)), ((), ())), preferred_element_type=F32)


def _split3(x):
    hi = x.astype(BF16)
    r1 = x - hi.astype(F32)
    mid = r1.astype(BF16)
    lo = (r1 - mid.astype(F32)).astype(BF16)
    return hi, mid, lo


def _dot_mask_f32(mask_bf16, x):
    hi, mid, lo = _split3(x)
    return (_dot(mask_bf16, lo) + _dot(mask_bf16, mid)) + _dot(mask_bf16, hi)


def _pack_bf16_pairs(x):
    half = x.shape[1] // 2
    bits = lax.bitcast_convert_type(x.astype(BF16).astype(F32), U32)
    return (bits[:, :half] >> 16) | bits[:, half:]


def _unpack_bf16_pairs(words):
    lo = lax.bitcast_convert_type(words << 16, F32)
    hi = lax.bitcast_convert_type(words & jnp.uint32(0xFFFF0000), F32)
    return jnp.concatenate([lo, hi], axis=1)


def _tril_mask(n, strict=False):
    r = lax.broadcasted_iota(I32, (n, n), 0)
    c = lax.broadcasted_iota(I32, (n, n), 1)
    return (r > c) if strict else (r >= c)


_OFF_MQK = 0
_OFF_MV = _OFF_MQK + MQK_W
_OFF_MO = _OFF_MV + M_W
_OFF_FQ = _OFF_MO + M_W
_OFF_FK = _OFF_FQ + F_W
_OFF_FV = _OFF_FK + F_W
_OFF_GATES = _OFF_FV + F_W
_PACKED_COLS = _OFF_GATES + GATE_COLS


def _inproj_kernel(tiles_per_seq, x_ref, w_ref, b_ref, cw_ref, cb_ref,
                   mq_ref, mk_ref, mv_ref, mo_ref, fq_ref, fk_ref, fv_ref, gates_ref, ext_ref):
    tm = x_ref.shape[0]
    pad = V7X_SUBLANES
    xb = x_ref[...].astype(BF16)

    def seg(lo, width):
        return _dot(xb, w_ref[:, lo:lo + width]) + b_ref[:, lo:lo + width]

    @pl.when(pl.program_id(0) % tiles_per_seq == 0)
    def _():
        ext_ref[0:pad, :] = jnp.zeros((pad, MQK_W), F32)

    def conv_silu(cols):
        ext_ref[pad:pad + tm, cols] = seg(_OFF_MQK + cols.start, cols.stop - cols.start)
        y = cb_ref[:, cols] + cw_ref[CONV_W - 1:CONV_W, cols] * ext_ref[pad:pad + tm, cols]
        for k in range(CONV_W - 1):
            shift = CONV_W - 1 - k
            y = y + cw_ref[k:k + 1, cols] * ext_ref[pad - shift:pad - shift + tm, cols]
        ext_ref[0:pad, cols] = ext_ref[tm:tm + pad, cols]
        return y * _sigmoid(y)

    half = MQK_W // 2
    group = half // 2
    mq_ref[:, :group] = conv_silu(slice(0, group)).astype(BF16)
    mv_ref[...] = seg(_OFF_MV, M_W).astype(BF16)
    mq_ref[:, group:] = conv_silu(slice(group, half)).astype(BF16)
    mo_ref[...] = seg(_OFF_MO, M_W)
    mk_ref[:, :group] = conv_silu(slice(half, half + group)) * (M_DQK ** -0.5)
    fq_ref[...] = (seg(_OFF_FQ, F_W) * (F_DH ** -0.5 * LOG2_E)).astype(BF16)
    mk_ref[:, group:] = conv_silu(slice(half + group, MQK_W)) * (M_DQK ** -0.5)
    fk_ref[...] = seg(_OFF_FK, F_W).astype(BF16)
    fv_ref[...] = seg(_OFF_FV, F_W).astype(BF16)
    gates_ref[...] = seg(_OFF_GATES, GATE_COLS)


def _inproj(x2, w_packed, b_packed, conv_w, conv_b, s):
    t, d = x2.shape
    tm = INPROJ_ROWS
    row = lambda i: (i, 0)
    const = lambda i: (0, 0)
    out_shapes = (
        jax.ShapeDtypeStruct((t, MQK_W // 2), BF16),
        jax.ShapeDtypeStruct((t, MQK_W // 2), F32),
        jax.ShapeDtypeStruct((t, M_W), BF16),
        jax.ShapeDtypeStruct((t, M_W), F32),
        jax.ShapeDtypeStruct((t, F_W), BF16),
        jax.ShapeDtypeStruct((t, F_W), BF16),
        jax.ShapeDtypeStruct((t, F_W), BF16),
        jax.ShapeDtypeStruct((t, GATE_COLS), F32),
    )
    return pl.pallas_call(
        functools.partial(_inproj_kernel, s // tm),
        grid=(t // tm,),
        in_specs=[
            pl.BlockSpec((tm, d), row),
            pl.BlockSpec((d, _PACKED_COLS), const),
            pl.BlockSpec((1, _PACKED_COLS), const),
            pl.BlockSpec((CONV_W, MQK_W), const),
            pl.BlockSpec((1, MQK_W), const),
        ],
        out_specs=tuple(pl.BlockSpec((tm, o.shape[1]), row) for o in out_shapes),
        out_shape=out_shapes,
        scratch_shapes=[pltpu.VMEM((tm + V7X_SUBLANES, MQK_W), F32)],
        compiler_params=_params(("arbitrary",)),
        name="inproj",
    )(x2, w_packed, b_packed, conv_w, conv_b)


def _fox_cumsum_kernel(g_ref, ccol_ref, crel_k_ref, crel_q_ref):
    s = g_ref.shape[0]
    cb = FOX_K_BLOCK
    per_q = FOX_Q_BLOCK // cb
    tri = _tril_mask(cb).astype(BF16)
    carry = jnp.zeros((1, GATE_COLS), F32)
    for j in range(s // cb):
        rows = slice(j * cb, (j + 1) * cb)
        if j % per_q == 0:
            q_carry = carry
        within = _dot_mask_f32(tri, _log_sigmoid(g_ref[rows, :])) * LOG2_E
        crel_k_ref[rows, :] = within
        crel_q_ref[rows, :] = within + (carry - q_carry)
        ccol_ref[rows, :] = within + carry
        carry = carry + within[cb - 1:cb, :]


def _fox_cumsum(gates, bsz, s):
    t = gates.shape[0]
    spec = pl.BlockSpec((s, GATE_COLS), lambda b: (b, 0))
    shape = jax.ShapeDtypeStruct((t, GATE_COLS), F32)
    return pl.pallas_call(
        _fox_cumsum_kernel,
        grid=(bsz,),
        in_specs=[spec],
        out_specs=(spec, spec, spec),
        out_shape=(shape, shape, shape),
        compiler_params=_params(("parallel",)),
        name="fox_cumsum",
    )(gates)


def _mlstm_kernel(mq_ref, mk_ref, mv_ref, mo_ref, gates_ref, ng_ref, hm_ref, state_ref, m_ref):
    @pl.when(pl.program_id(1) == 0)
    def _():
        state_ref[...] = jnp.zeros(state_ref.shape, F32)
        m_ref[...] = jnp.zeros(m_ref.shape, F32)

    seqs = range(mq_ref.shape[0])
    states = [[state_ref[bb, h] for h in range(M_HEADS)] for bb in seqs]
    maxes = [[m_ref[bb, h][0:1, 0:1] for h in range(M_HEADS)] for bb in seqs]
    results = [_mlstm_chunk(mq_ref.at[bb], mk_ref.at[bb], mv_ref.at[bb], mo_ref.at[bb], gates_ref.at[bb],
                            ng_ref, states[bb], maxes[bb]) for bb in seqs]
    for bb in seqs:
        for h, (out_h, state_h, m_h) in enumerate(results[bb]):
            hm_ref[bb, :, h * M_DV:(h + 1) * M_DV] = out_h
            state_ref[bb, h] = state_h
            m_ref[bb, h] = jnp.broadcast_to(m_h, m_ref.shape[2:])


def _mlstm_chunk(mq_ref, mk_ref, mv_ref, mo_ref, gates_ref, ng_ref, states, maxes):
    L = MLSTM_CHUNK
    reps = L // V7X_LANES
    results = []
    gates = gates_ref[...]
    bfull = _dot_mask_f32(_tril_mask(L).astype(BF16), _log_sigmoid(gates))
    b_rows = bfull.T
    z_all = gates - pltpu.roll(bfull, shift=V7X_LANES - (MF_LANE - MI_LANE), axis=1)
    visible = (lax.broadcasted_iota(I32, (L, L), 0) <= lax.broadcasted_iota(I32, (L, L), 1))
    ones_rows = (lax.broadcasted_iota(I32, (M_DV, L), 0) == 0).astype(BF16)

    for h in range(M_HEADS):
        b_row = b_rows[MF_LANE + h:MF_LANE + h + 1, :]
        g_tot = b_row[:, L - 1:L]
        m_prev = maxes[h]
        z = jnp.broadcast_to(z_all[:, MI_LANE + h:MI_LANE + h + 1], (L, V7X_LANES))

        q_h = mq_ref[:, h * M_DQK:(h + 1) * M_DQK]
        k_f = mk_ref[:, h * M_DQK:(h + 1) * M_DQK]
        k_h = k_f.astype(BF16)
        v_t = mv_ref[:, h * M_DV:(h + 1) * M_DV].astype(F32).T.astype(BF16)
        cn_t = states[h]

        dlog = jnp.where(visible, b_row + jnp.tile(z, (1, reps)), -jnp.inf)
        inter_log = b_row + m_prev
        m_t = jnp.maximum(inter_log, jnp.max(dlog, axis=0, keepdims=True))
        w_inter = jnp.exp(inter_log - m_t)
        qkw = _dot_nt(k_h, q_h) * jnp.exp(dlog - m_t)
        qc = _dot_nt(cn_t.astype(BF16), q_h)
        num = w_inter * qc[:M_DV, :] + _dot(v_t, qkw.astype(BF16))
        den = w_inter * qc[M_DV:M_DV + 1, :] + jnp.sum(qkw, axis=0, keepdims=True)
        hh = num / jnp.maximum(jnp.abs(den), jnp.exp(-m_t))

        mu = jnp.mean(hh, axis=0, keepdims=True)
        dv = hh - mu
        var = jnp.mean(dv * dv, axis=0, keepdims=True)
        hn = (dv * lax.rsqrt(var + LN_EPS)) * jnp.tile(ng_ref[h * M_DV:(h + 1) * M_DV, :], (1, reps))
        out_h = (_sigmoid(mo_ref[:, h * M_DV:(h + 1) * M_DV]) * hn.T).astype(BF16)

        a = g_tot + z
        m_new = jnp.maximum(g_tot + m_prev, jnp.max(a, axis=0, keepdims=True)[:, 0:1])
        decay = jnp.exp(g_tot + m_prev - m_new)
        kw = (k_f * jnp.exp(a - m_new)).astype(BF16)
        v_aug = jnp.concatenate([v_t, ones_rows], axis=0)
        results.append((out_h, decay * cn_t + _dot(v_aug, kw), m_new))
    return results


def _mlstm(mq, mk, mv, mo, gates, norm_g, bsz, s):
    t = mq.shape[0]
    L = MLSTM_CHUNK
    nb = MLSTM_SEQS
    seq = lambda a: a.reshape(bsz, s, a.shape[1])
    blk = lambda width: pl.BlockSpec((nb, L, width), lambda g, n: (g, n, 0))
    const = lambda g, n: (0, 0)
    hm = pl.pallas_call(
        _mlstm_kernel,
        grid=(bsz // nb, s // L),
        in_specs=[
            blk(MQK_W // 2), blk(MQK_W // 2), blk(M_W), blk(M_W), blk(GATE_COLS),
            pl.BlockSpec((M_W, V7X_LANES), const),
        ],
        out_specs=blk(M_W),
        out_shape=jax.ShapeDtypeStruct((bsz, s, M_W), BF16),
        scratch_shapes=[
            pltpu.VMEM((nb, M_HEADS, 2 * M_DV, M_DQK), F32),
            pltpu.VMEM((nb, M_HEADS, V7X_SUBLANES, V7X_LANES), F32),
        ],
        compiler_params=_params(("parallel", "arbitrary")),
        name="mlstm",
    )(seq(mq), seq(mk), seq(mv), seq(mo), seq(gates),
      jnp.broadcast_to(norm_g[:, None], (M_W, V7X_LANES)))
    return hm.reshape(t, M_W)


_FOX_FEATURES = 3


def _fox_operands(x, c_tile, c_lane, key_side):
    rows = x.shape[0]
    nf = _FOX_FEATURES
    first = 0 if key_side else nf
    src = lax.broadcasted_iota(I32, (V7X_LANES, V7X_LANES), 0) - c_lane
    dst = lax.broadcasted_iota(I32, (V7X_LANES, V7X_LANES), 1)
    feats = jnp.zeros((rows, V7X_LANES), F32)
    for n, part in enumerate(_split3(c_tile)):
        place = jnp.logical_or(jnp.logical_and(src == 0, dst == F_DH + first + n),
                               jnp.logical_and(src == 1, dst == first + n))
        feats = feats + _dot(part, place.astype(BF16))
    lane = lax.broadcasted_iota(I32, (rows, V7X_LANES), 1)
    within = lane % F_DH
    const_lanes = jnp.logical_and(within >= nf - first, within < 2 * nf - first)
    feats = jnp.where(const_lanes, 1.0 if key_side else -1.0, feats).astype(BF16)
    low = lane < F_DH
    return jnp.where(low, x, feats), jnp.where(low, feats, x)


def _fox_kernel(q_ref, k_ref, v_ref, ccol_ref, crel_k_ref, crel_q_ref, o_ref, qaug_ref, kaug_ref, vt_ref,
                *scratch):
    blk = FOX_K_BLOCK
    tq = FOX_Q_BLOCK
    p = pl.program_id(1)
    c_lane = FF_LANE + 2 * p
    qaug_ref[0], qaug_ref[1] = _fox_operands(q_ref[...], crel_q_ref[...], c_lane, False)
    kaug_ref[0], kaug_ref[1] = _fox_operands(k_ref[...], crel_k_ref[...], c_lane, True)
    v_t = v_ref[...].astype(F32).T
    for j in range(vt_ref.shape[0]):
        vt_ref[j] = v_t[:, j * blk:(j + 1) * blk].astype(BF16)
    for i in range(o_ref.shape[0] // tq):
        _fox_query_block(i, p, ccol_ref, o_ref.at[pl.ds(i * tq, tq)], qaug_ref, kaug_ref, vt_ref, *scratch)


def _fox_query_block(i, p, ccol_ref, o_ref, qaug_ref, kaug_ref, vt_ref,
                     st_a0, st_a1, st_b0, st_b1, pe_a0, pe_a1, pe_b0, pe_b1, acc0, acc1):
    blk = FOX_K_BLOCK
    tq = FOX_Q_BLOCK
    assert tq == 2 * blk
    strips = tq // V7X_LANES
    st_a_refs, st_b_refs = (st_a0, st_a1), (st_b0, st_b1)
    pe_a_refs, pe_b_refs = (pe_a0, pe_a1), (pe_b0, pe_b1)
    acc_refs = (acc0, acc1)

    q_start = i * tq
    q_heads = [qaug_ref[hh, pl.ds(q_start, tq), :] for hh in range(2)]
    key_row = lax.broadcasted_iota(I32, (blk, V7X_LANES), 0)
    query_col = lax.broadcasted_iota(I32, (blk, V7X_LANES), 1)
    last_chunk = 2 * i + 1
    head_lane = lax.broadcasted_iota(I32, (1, GATE_COLS), 1) - (FF_LANE + 2 * p)

    def c_before(position, hh):
        row = ccol_ref[pl.ds(jnp.maximum(position - 1, 0), 1), :]
        keep = jnp.logical_and(head_lane == hh, position > 0)
        return jnp.sum(jnp.where(keep, row, 0.0), axis=-1, keepdims=True)

    c_query0 = [c_before(q_start, hh) for hh in range(2)]

    def put_scores(st_ref, j, hh):
        start = j * blk if isinstance(j, int) else pl.multiple_of(j * blk, blk)
        st = _dot_nt(kaug_ref[hh, pl.ds(start, blk), :], q_heads[hh])
        for c in range(strips):
            st_ref[c] = st[:, c * V7X_LANES:(c + 1) * V7X_LANES]

    def exponentials(pe_ref):
        return jnp.concatenate([pe_ref[c] for c in range(strips)], axis=1)

    def softmax_update(st_ref, pe_ref, m_old, l_old, base, key_minus_query=None):
        alphas, ms, ls = [], [], []
        for c in range(strips):
            cols = slice(c * V7X_LANES, (c + 1) * V7X_LANES)
            gap = None if key_minus_query is None else key_minus_query - c * V7X_LANES
            if gap is not None and gap - (V7X_LANES - 1) > 0:
                pe_ref[c] = jnp.zeros((blk, V7X_LANES), BF16)
                alphas.append(jnp.ones((1, V7X_LANES), F32))
                ms.append(m_old[:, cols])
                ls.append(l_old[:, cols])
                continue
            st = st_ref[c]
            if gap is not None and gap + (blk - 1) > 0:
                st = jnp.where(key_row + gap <= query_col, st, -jnp.inf)
            m_new = jnp.maximum(m_old[:, cols], jnp.max(st, axis=0, keepdims=True) + base)
            alpha = jnp.exp2(m_old[:, cols] - m_new)
            pe = jnp.exp2(st - (m_new - base))
            pe_ref[c] = pe.astype(BF16)
            alphas.append(alpha)
            ms.append(m_new)
            ls.append(alpha * l_old[:, cols] + jnp.sum(pe, axis=0, keepdims=True))
        cat = lambda parts: jnp.concatenate(parts, axis=1)
        return cat(alphas), cat(ms), cat(ls)

    def pair(mi, carry, diagonal=False):
        a = 2 * mi
        b = a + 1
        v_prev = vt_ref[jnp.maximum(a - 1, 0)]
        v_a = vt_ref[a]
        partial = []
        for hh in range(2):
            alpha_prev = carry[hh][0]
            partial.append(alpha_prev * acc_refs[hh][...] + _dot(v_prev, exponentials(pe_b_refs[hh])))
            put_scores(st_b_refs[hh], b, hh)
        stats = []
        for hh in range(2):
            _, m_old, l_old = carry[hh]
            stats.append(softmax_update(st_a_refs[hh], pe_a_refs[hh], m_old, l_old,
                                        c_query0[hh] - c_before(a * blk, hh), 0 if diagonal else None))
        for hh in range(2):
            alpha_a = stats[hh][0]
            acc_refs[hh][...] = alpha_a * partial[hh] + _dot(v_a, exponentials(pe_a_refs[hh]))
            if not diagonal:
                put_scores(st_a_refs[hh], a + 2, hh)
        return tuple(softmax_update(st_b_refs[hh], pe_b_refs[hh], stats[hh][1], stats[hh][2],
                                    c_query0[hh] - c_before(b * blk, hh), blk if diagonal else None)
                     for hh in range(2))

    for hh in range(2):
        put_scores(st_a_refs[hh], 0, hh)
        pe_b_refs[hh][...] = jnp.zeros(pe_b_refs[hh].shape, BF16)
        acc_refs[hh][...] = jnp.zeros((V7X_LANES, tq), F32)
    init = tuple((jnp.ones((1, tq), F32), jnp.full((1, tq), -jnp.inf, F32), jnp.zeros((1, tq), F32))
                 for _ in range(2))
    carry = init
    for mi in range(i):
        carry = pair(mi, carry)
    final = pair(i, carry, diagonal=True)
    v_last = vt_ref[last_chunk]
    outs = []
    for hh in range(2):
        alpha, _, l_fin = final[hh]
        outs.append((alpha * acc_refs[hh][...] + _dot(v_last, exponentials(pe_b_refs[hh]))) / l_fin)
    row = lax.broadcasted_iota(I32, (V7X_LANES, tq), 0)
    o_t = jnp.where(row < F_DH, outs[0], outs[1])
    o_ref[...] = o_t.T.astype(BF16)


def _fox_attention(fq, fk, fv, ccol, crel_k, crel_q, bsz, s):
    t = fq.shape[0]
    blk = FOX_K_BLOCK
    tq = FOX_Q_BLOCK
    pairs = F_HEADS // 2
    kvmap = lambda b, p: (b, p)
    return pl.pallas_call(
        _fox_kernel,
        grid=(bsz, pairs),
        in_specs=[
            pl.BlockSpec((s, V7X_LANES), kvmap),
            pl.BlockSpec((s, V7X_LANES), kvmap),
            pl.BlockSpec((s, V7X_LANES), kvmap),
            pl.BlockSpec((s, GATE_COLS), lambda b, p: (b, 0)),
            pl.BlockSpec((s, GATE_COLS), lambda b, p: (b, 0)),
            pl.BlockSpec((s, GATE_COLS), lambda b, p: (b, 0)),
        ],
        out_specs=pl.BlockSpec((s, V7X_LANES), kvmap),
        out_shape=jax.ShapeDtypeStruct((t, F_W), BF16),
        scratch_shapes=[
            pltpu.VMEM((2, s, V7X_LANES), BF16),
            pltpu.VMEM((2, s, V7X_LANES), BF16),
            pltpu.VMEM((s // blk, V7X_LANES, blk), BF16),
        ] + [pltpu.VMEM((tq // V7X_LANES, blk, V7X_LANES), F32)] * 4 + [pltpu.VMEM((tq // V7X_LANES, blk, V7X_LANES), BF16)] * 4
          + [pltpu.VMEM((V7X_LANES, tq), F32)] * 2,
        compiler_params=_params(("parallel", "parallel")),
        name="fox_attention",
    )(fq, fk, fv, ccol, crel_k, crel_q)


def _layer_norm_rows(r, g, b):
    mu = jnp.mean(r, axis=-1, keepdims=True)
    d = r - mu
    var = jnp.mean(d * d, axis=-1, keepdims=True)
    return (d * lax.rsqrt(var + LN_EPS)) * g + b


def _merge_kernel(dn_alpha, hm_ref, hf_ref, x_ref, wg_ref, bg_ref, wbm_ref, wbf_ref, wo_ref,
                  g_ref, b_ref, wrh_ref, wrl_ref, br_ref, h1_ref, h1p_ref, gate_ref, tope_ref, cnt_ref,
                  resid_ref):
    i = pl.program_id(0)
    slot = i % 2

    @pl.when(i == 0)
    def _():
        resid_ref[1] = jnp.zeros(resid_ref.shape[1:], F32)

    subs = [slice(n * MERGE_SUB_ROWS, (n + 1) * MERGE_SUB_ROWS)
            for n in range(x_ref.shape[0] // MERGE_SUB_ROWS)]
    counts = jnp.zeros((1, V7X_LANES), F32)
    for rows in subs:
        counts = counts + _merge_tail(resid_ref[1 - slot, rows, :], rows, g_ref, b_ref, wrh_ref, wrl_ref,
                                      br_ref, h1_ref, h1p_ref, gate_ref, tope_ref)
    sub = lax.broadcasted_iota(I32, cnt_ref.shape, 0)
    cnt_ref[...] = jnp.where(sub == 0, counts, 0.0)

    for rows in subs:
        x = x_ref[rows, :]
        d = x.shape[1]
        gmf = _dot(x.astype(BF16), wg_ref[...]) + bg_ref[...]
        ym = _dot(hm_ref[rows, :], wbm_ref[...])
        yf = _dot(hf_ref[rows, :], wbf_ref[...])
        y = _sigmoid(gmf[:, :d]) * ym + _sigmoid(gmf[:, d:]) * yf
        resid_ref[slot, rows, :] = dn_alpha * x + _dot(y.astype(BF16), wo_ref[...])


def _merge_tail(resid, rows, g_ref, b_ref, wrh_ref, wrl_ref, br_ref, h1_ref, h1p_ref, gate_ref, tope_ref):
    h1 = _layer_norm_rows(resid, g_ref[...], b_ref[...])
    h1_ref[rows, :] = h1

    h1p_ref[rows, :] = _pack_bf16_pairs(h1)
    hb = h1.astype(BF16)

    lo = (h1 - hb.astype(F32)).astype(BF16)
    logits = (_dot(lo, wrh_ref[...]) + _dot(hb, wrl_ref[...])) + _dot(hb, wrh_ref[...]) + br_ref[...]
    tm = logits.shape[0]
    lane = lax.broadcasted_iota(I32, (tm, V7X_LANES), 1)
    vals = jnp.where(lane < N_EXPERTS, logits, -jnp.inf)
    top_v, top_i = [], []
    for _ in range(TOP_K):
        mx = jnp.max(vals, axis=-1, keepdims=True)
        idx = jnp.min(jnp.where(vals == mx, lane, V7X_LANES), axis=-1, keepdims=True)
        top_v.append(mx)
        top_i.append(idx)
        vals = jnp.where(lane == idx, -jnp.inf, vals)
    ex = [jnp.exp(v - top_v[0]) for v in top_v]
    tot = ex[0]
    for e in ex[1:]:
        tot = tot + e
    gate = jnp.zeros((tm, V7X_LANES), F32)
    tope = jnp.zeros((tm, V7X_LANES), I32)
    member = jnp.zeros((tm, V7X_LANES), F32)
    for k in range(TOP_K):
        gate = jnp.where(lane == k, ex[k] / tot, gate)
        tope = jnp.where(lane == k, top_i[k], tope)
        member = member + (lane == top_i[k]).astype(F32)
    gate_ref[rows, :] = gate
    tope_ref[rows, :] = tope
    return jnp.sum(member, axis=0, keepdims=True)


def _merge(dn_alpha, hm, hf, x2, wg, bg, wbm, wbf, wo, ln_g, ln_b, wr_hi, wr_lo, br):
    t, d = x2.shape
    tm = MERGE_ROWS
    nt = t // tm
    row = lambda i: (jnp.minimum(i, nt - 1), 0)
    out_row = lambda i: (jnp.maximum(i - 1, 0), 0)
    const = lambda i: (0, 0)
    full = lambda a: pl.BlockSpec(a.shape, const)
    return pl.pallas_call(
        functools.partial(_merge_kernel, dn_alpha),
        grid=(nt + 1,),
        in_specs=[
            pl.BlockSpec((tm, M_W), row),
            pl.BlockSpec((tm, F_W), row),
            pl.BlockSpec((tm, d), row),
            full(wg), full(bg), full(wbm), full(wbf), full(wo), full(ln_g), full(ln_b),
            full(wr_hi), full(wr_lo), full(br),
        ],
        out_specs=(
            pl.BlockSpec((tm, d), out_row),
            pl.BlockSpec((tm, d // 2), out_row),
            pl.BlockSpec((tm, V7X_LANES), out_row),
            pl.BlockSpec((tm, V7X_LANES), out_row),
            pl.BlockSpec((V7X_SUBLANES, V7X_LANES), out_row),
        ),
        out_shape=(
            jax.ShapeDtypeStruct((t, d), F32),
            jax.ShapeDtypeStruct((t, d // 2), U32),
            jax.ShapeDtypeStruct((t, V7X_LANES), F32),
            jax.ShapeDtypeStruct((t, V7X_LANES), I32),
            jax.ShapeDtypeStruct((t // tm * V7X_SUBLANES, V7X_LANES), F32),
        ),
        scratch_shapes=[pltpu.VMEM((2, tm, d), F32)],
        compiler_params=_params(("arbitrary",)),
        name="merge_ln1_router",
    )(hm, hf, x2, wg, bg, wbm, wbf, wo, ln_g, ln_b, wr_hi, wr_lo, br)


def _lane_cumsum(x):
    lane = lax.broadcasted_iota(I32, x.shape, 1)
    d = 1
    while d < V7X_LANES:
        x = x + jnp.where(lane >= d, pltpu.roll(x, shift=d, axis=1), 0.0)
        d *= 2
    return x


def _routing_kernel(cnt_ref, tope_ref, dest_ref, table_ref, run_ref, start_ref):
    sb = ROUTE_SUB_ROWS

    @pl.when(pl.program_id(0) == 0)
    def _():
        total = jnp.sum(cnt_ref[...], axis=0, keepdims=True)
        counts = jnp.broadcast_to(total, (V7X_SUBLANES, V7X_LANES))
        padded = jnp.ceil(counts * (1.0 / MOE_BLOCK)) * MOE_BLOCK
        pad_end = _lane_cumsum(padded)
        pad_start = pad_end - padded
        start_ref[...] = pad_start
        run_ref[...] = jnp.zeros(run_ref.shape, F32)
        nb = table_ref.shape[0]
        blk = lax.broadcasted_iota(I32, (nb, V7X_LANES), 0).astype(F32) * MOE_BLOCK
        ln = lax.broadcasted_iota(I32, (nb, V7X_LANES), 1)
        done = jnp.logical_and(pad_end[0:1, :] <= blk, ln < N_EXPERTS)
        be = jnp.minimum(jnp.sum(done.astype(F32), axis=-1, keepdims=True), N_EXPERTS - 1.0)
        onehot = ln == be.astype(I32)
        cnt_e = jnp.sum(jnp.where(onehot, counts[0:1, :], 0.0), axis=-1, keepdims=True)
        start_e = jnp.sum(jnp.where(onehot, pad_start[0:1, :], 0.0), axis=-1, keepdims=True)
        valid = jnp.clip(cnt_e - (blk[:, 0:1] - start_e), 0.0, float(MOE_BLOCK))
        table_ref[...] = jnp.where(ln == 0, be.astype(I32),
                                   jnp.where(ln == 1, valid.astype(I32), 0))

    earlier = _tril_mask(sb, strict=True).astype(BF16)
    lane = lax.broadcasted_iota(I32, (sb, V7X_LANES), 1)
    offset = run_ref[0:1, :] + start_ref[0:1, :]
    seen = jnp.zeros((1, V7X_LANES), F32)
    for j in range(tope_ref.shape[0] // sb):
        tope = tope_ref[j * sb:(j + 1) * sb, :]
        hit = [lane == tope[:, k:k + 1] for k in range(TOP_K)]
        member = jnp.zeros((sb, V7X_LANES), F32)
        for k in range(TOP_K):
            member = member + hit[k].astype(F32)
        base = _dot(earlier, member.astype(BF16)) + (offset + seen)
        dest = jnp.zeros((sb, V7X_LANES), F32)
        for k in range(TOP_K):
            dk = jnp.sum(jnp.where(hit[k], base, 0.0), axis=-1, keepdims=True)
            dest = jnp.where(lane == k, dk, dest)
        dest_ref[:, j * sb:(j + 1) * sb] = dest.T[0:V7X_SUBLANES, :].astype(I32)
        seen = seen + jnp.sum(member, axis=0, keepdims=True)
    run_ref[...] = run_ref[...] + seen


def _routing(tile_counts, tope, n_blocks):
    t = tope.shape[0]
    tr = ROUTE_ROWS
    return pl.pallas_call(
        _routing_kernel,
        grid=(t // tr,),
        in_specs=[pl.BlockSpec(tile_counts.shape, lambda i: (0, 0)),
                  pl.BlockSpec((tr, V7X_LANES), lambda i: (i, 0))],
        out_specs=(
            pl.BlockSpec((V7X_SUBLANES, tr), lambda i: (0, i)),
            pl.BlockSpec((n_blocks, V7X_LANES), lambda i: (0, 0)),
        ),
        out_shape=(
            jax.ShapeDtypeStruct((V7X_SUBLANES, t), I32),
            jax.ShapeDtypeStruct((n_blocks, V7X_LANES), I32),
        ),
        scratch_shapes=[
            pltpu.VMEM((V7X_SUBLANES, V7X_LANES), F32),
            pltpu.VMEM((V7X_SUBLANES, V7X_LANES), F32),
        ],
        compiler_params=_params(("arbitrary",)),
        name="routing",
    )(tile_counts, tope)


def _sc_worker_id():
    return lax.axis_index("s") * V7X_SC_CORES + lax.axis_index("c")


def _sc_mesh():
    return plsc.VectorSubcoreMesh(core_axis_name="c", subcore_axis_name="s",
                                  num_cores=V7X_SC_CORES, num_subcores=V7X_SC_SUBCORES)


def _sc_dispatch(dest_km, h1p, n_rows):
    t, w = h1p.shape
    per_worker = t // V7X_SC_WORKERS
    ch = SC_SCATTER_ROWS

    @functools.partial(
        pl.kernel, mesh=_sc_mesh(),
        out_type=jax.ShapeDtypeStruct((n_rows, w), h1p.dtype),
        scratch_types=[pltpu.VMEM((ch, w), h1p.dtype)]
        + [pltpu.VMEM((ch,), I32)] * TOP_K + [pltpu.SemaphoreType.DMA] * TOP_K,
        name="sc_dispatch",
    )
    def scatter_rows(dest_hbm, h1p_hbm, xs_hbm, rows_v, *idx_and_sems):
        idx_refs, sems = idx_and_sems[:TOP_K], idx_and_sems[TOP_K:]
        first = _sc_worker_id() * per_worker

        @pl.loop(0, per_worker // ch)
        def _(j):
            base = first + j * ch
            pltpu.sync_copy(h1p_hbm.at[pl.ds(base, ch)], rows_v)
            copies = []
            for k in range(TOP_K):
                pltpu.sync_copy(dest_hbm.at[pl.ds(k * t + base, ch)], idx_refs[k])
                copies.append(pltpu.async_copy(rows_v, xs_hbm.at[idx_refs[k]], sems[k]))
            for copy in copies:
                copy.wait()

    return scatter_rows(dest_km, h1p)


def _sc_gather(dest_km, y_rows):
    n = dest_km.shape[0]
    w = y_rows.shape[1]
    per_worker = n // V7X_SC_WORKERS
    ch = SC_GATHER_ROWS
    ways = SC_GATHER_WAYS

    @functools.partial(
        pl.kernel, mesh=_sc_mesh(),
        out_type=jax.ShapeDtypeStruct((n, w), y_rows.dtype),
        scratch_types=[pltpu.VMEM((ch,), I32)] * ways + [pltpu.VMEM((ch, w), y_rows.dtype)] * ways
        + [pltpu.SemaphoreType.DMA] * (2 * ways),
        name="sc_gather",
    )
    def gather_rows(dest_hbm, y_hbm, out_hbm, *scratch):
        idx_refs, row_refs = scratch[:ways], scratch[ways:2 * ways]
        gather_sems, store_sems = scratch[2 * ways:3 * ways], scratch[3 * ways:]
        first = _sc_worker_id() * per_worker

        @pl.loop(0, per_worker // (ch * ways))
        def _(j):
            bases = [first + (j * ways + u) * ch for u in range(ways)]
            gathers = []
            for u in range(ways):
                pltpu.sync_copy(dest_hbm.at[pl.ds(bases[u], ch)], idx_refs[u])
                gathers.append(pltpu.async_copy(y_hbm.at[idx_refs[u]], row_refs[u], gather_sems[u]))
            stores = []
            for u in range(ways):
                gathers[u].wait()
                stores.append(pltpu.async_copy(row_refs[u], out_hbm.at[pl.ds(bases[u], ch)], store_sems[u]))
            for store in stores:
                store.wait()

    return gather_rows(dest_km, y_rows)


def _expert_kernel(be_ref, nv_ref, xs_ref, wgu_f32_ref, bgu_ref, wdn_f32_ref, bdn_ref, y_ref,
                   wgu_slots, wdn_slots, slot_ref):
    i = pl.program_id(0)
    n_blocks = pl.num_programs(0) - 1
    nv = jnp.where(i > 0, nv_ref[jnp.maximum(i - 1, 0)], 0)
    half = MOE_BLOCK // 2

    @pl.when(i == 0)
    def _():
        slot_ref[0] = 0

    slot = slot_ref[0]
    wgu_ref = wgu_slots.at[slot]
    wdn_ref = wdn_slots.at[slot]

    def ffn(rows):
        x = _unpack_bf16_pairs(xs_ref[rows, :])
        rowid = rows.start + lax.broadcasted_iota(I32, x.shape, 0)
        x = jnp.where(rowid < nv, x, 0.0).astype(BF16)
        gu = _dot(x, wgu_ref[...]) + bgu_ref[...]
        f = gu.shape[1] // 2
        glu = jnp.minimum(gu[:, :f], SWIGLU_LIMIT)
        lin = jnp.clip(gu[:, f:], -SWIGLU_LIMIT, SWIGLU_LIMIT)
        act = glu * _sigmoid(SWIGLU_ALPHA * glu) * (lin + 1.0)
        y_ref[rows, :] = _pack_bf16_pairs(_dot(act.astype(BF16), wdn_ref[...]) + bdn_ref[...])

    @pl.when(nv == 0)
    def _():
        y_ref[...] = jnp.zeros(y_ref.shape, U32)

    @pl.when(jnp.logical_and(nv > 0, nv <= half))
    def _():
        ffn(slice(0, half))
        y_ref[half:, :] = jnp.zeros((MOE_BLOCK - half, y_ref.shape[1]), U32)

    @pl.when(nv > half)
    def _():
        ffn(slice(0, half))
        ffn(slice(half, MOE_BLOCK))

    new_expert = jnp.logical_or(i == 0, be_ref[jnp.minimum(i, n_blocks - 1)] != be_ref[jnp.maximum(i - 1, 0)])

    @pl.when(jnp.logical_and(i < n_blocks, new_expert))
    def _():
        wgu_slots[1 - slot] = wgu_f32_ref[...].astype(BF16)
        wdn_slots[1 - slot] = wdn_f32_ref[...].astype(BF16)
        slot_ref[0] = 1 - slot


def _experts(block_e, block_valid, xs, wgu, bgu, wdn, bdn):
    n_rows, w = xs.shape
    e, d, f2 = wgu.shape
    n_blocks = n_rows // MOE_BLOCK
    ahead = lambda i, be: be[jnp.minimum(i, n_blocks - 1)]
    behind = lambda i: jnp.maximum(i - 1, 0)
    grid_spec = pltpu.PrefetchScalarGridSpec(
        num_scalar_prefetch=2,
        grid=(n_blocks + 1,),
        in_specs=[
            pl.BlockSpec((MOE_BLOCK, w), lambda i, be, nv: (behind(i), 0)),
            pl.BlockSpec((None, d, f2), lambda i, be, nv: (ahead(i, be), 0, 0)),
            pl.BlockSpec((None, 1, f2), lambda i, be, nv: (be[behind(i)], 0, 0)),
            pl.BlockSpec((None, f2 // 2, d), lambda i, be, nv: (ahead(i, be), 0, 0)),
            pl.BlockSpec((None, 1, d), lambda i, be, nv: (be[behind(i)], 0, 0)),
        ],
        out_specs=pl.BlockSpec((MOE_BLOCK, d // 2), lambda i, be, nv: (behind(i), 0)),
        scratch_shapes=[pltpu.VMEM((2, d, f2), BF16), pltpu.VMEM((2, f2 // 2, d), BF16),
                        pltpu.SMEM((1,), I32)],
    )
    return pl.pallas_call(
        _expert_kernel,
        grid_spec=grid_spec,
        out_shape=jax.ShapeDtypeStruct((n_rows, d // 2), U32),
        compiler_params=pltpu.CompilerParams(dimension_semantics=("arbitrary",),
                                             vmem_limit_bytes=EXPERT_VMEM_LIMIT_BYTES),
        name="experts",
    )(block_e, block_valid, xs, wgu, bgu, wdn, bdn)


def _combine_kernel(dn_alpha, h1_ref, gate_ref, g_ref, b_ref, yg_ref, o_ref):
    gate = gate_ref[...]
    ffn = gate[:, 0:1] * _unpack_bf16_pairs(yg_ref[0])
    for k in range(1, TOP_K):
        ffn = ffn + gate[:, k:k + 1] * _unpack_bf16_pairs(yg_ref[k])
    o_ref[...] = _layer_norm_rows(dn_alpha * h1_ref[...] + ffn, g_ref[...], b_ref[...])


def _combine(dn_alpha, h1, gate, ln_g, ln_b, yg):
    t, d = h1.shape
    tc = COMBINE_ROWS
    row = lambda i: (i, 0)
    const = lambda i: (0, 0)
    return pl.pallas_call(
        functools.partial(_combine_kernel, dn_alpha),
        grid=(t // tc,),
        in_specs=[
            pl.BlockSpec((tc, d), row),
            pl.BlockSpec((tc, V7X_LANES), row),
            pl.BlockSpec((1, d), const),
            pl.BlockSpec((1, d), const),
            pl.BlockSpec((TOP_K, tc, d // 2), lambda i: (0, i, 0)),
        ],
        out_specs=pl.BlockSpec((tc, d), row),
        out_shape=jax.ShapeDtypeStruct((t, d), F32),
        compiler_params=_params(("parallel",)),
        name="combine_ln2",
    )(h1, gate, ln_g, ln_b, yg)


def _in_proj_columns(d):
    o = 0
    cols = {}
    for name, width in (("mqk", MQK_W), ("mv", M_W), ("mo", M_W), ("mi", M_HEADS), ("mf", M_HEADS),
                        ("fq", F_W), ("fk", F_W), ("fv", F_W), ("ff", F_HEADS), ("gm", d), ("gf", d)):
        cols[name] = (o, o + width)
        o += width
    return cols


def _pack_w_kernel(w_ref, main_ref, gate_ref):
    d = gate_ref.shape[1] // 2
    cols = _in_proj_columns(d)
    out = 0
    for name in ("mqk", "mv", "mo", "fq", "fk", "fv"):
        lo, hi = cols[name]
        main_ref[:, out:out + hi - lo] = w_ref[:, lo:hi].astype(BF16)
        out += hi - lo
    lane = lax.broadcasted_iota(I32, (w_ref.shape[0], GATE_COLS), 1)
    mi_lo, ff_lo = cols["mi"][0], cols["ff"][0] - 2 * M_HEADS
    assert mi_lo % V7X_LANES == 0 and ff_lo % V7X_LANES == 0 and cols["mf"][0] == mi_lo + M_HEADS
    gates = jnp.where(lane < 2 * M_HEADS, w_ref[:, mi_lo:mi_lo + GATE_COLS],
                      jnp.where(lane < 2 * M_HEADS + F_HEADS, w_ref[:, ff_lo:ff_lo + GATE_COLS], 0.0))
    main_ref[:, out:out + GATE_COLS] = gates.astype(BF16)
    gate_ref[...] = w_ref[:, cols["gm"][0]:cols["gf"][1]].astype(BF16)


def _pack_in_proj(w_in_layers, layer, b_in):
    _, d, n_cols = w_in_layers.shape
    cols = _in_proj_columns(d)
    tr = V7X_LANES
    w_main, w_gate = pl.pallas_call(
        _pack_w_kernel,
        grid=(d // tr,),
        in_specs=[pl.BlockSpec((None, tr, n_cols), lambda i: (layer, i, 0))],
        out_specs=(pl.BlockSpec((tr, _PACKED_COLS), lambda i: (i, 0)),
                   pl.BlockSpec((tr, 2 * d), lambda i: (i, 0))),
        out_shape=(jax.ShapeDtypeStruct((d, _PACKED_COLS), BF16),
                   jax.ShapeDtypeStruct((d, 2 * d), BF16)),
        compiler_params=_params(("parallel",)),
        name="pack_in_proj",
    )(w_in_layers)

    def take(names):
        return [b_in[cols[n][0]:cols[n][1]] for n in names]

    n_gate = 2 * M_HEADS + F_HEADS
    b_main = jnp.concatenate(take(("mqk", "mv", "mo", "fq", "fk", "fv", "mi", "mf", "ff"))
                             + [jnp.zeros((GATE_COLS - n_gate,), b_in.dtype)])
    b_gate = jnp.concatenate(take(("gm", "gf")))
    return w_main, b_main[None, :], w_gate, b_gate[None, :]


def _layer(h, depth, layer, w_in_layers, b_in, m_conv_w, m_conv_b, m_norm_g, w_bm, w_bf, w_o, ln1_g, ln1_b,
           w_router, b_router, w_gu, b_gu, w_dn, b_dn, ln2_g, ln2_b):
    bsz, s, d = h.shape
    t = bsz * s
    dn_alpha = (2.0 * depth) ** 0.25
    x2 = h.reshape(t, d)
    assert d == MQK_W and w_gu.shape[0] == N_EXPERTS, "kernels are written for this layer geometry"
    assert s % INPROJ_ROWS == 0 and s % MLSTM_CHUNK == 0 and s % FOX_Q_BLOCK == 0
    assert bsz % MLSTM_SEQS == 0
    assert t % MERGE_ROWS == 0 and t % ROUTE_ROWS == 0 and t % COMBINE_ROWS == 0
    assert t % (V7X_SC_WORKERS * SC_SCATTER_ROWS) == 0
    assert (t * TOP_K) % (V7X_SC_WORKERS * SC_GATHER_ROWS * SC_GATHER_WAYS) == 0

    w_main, b_main, w_gate, b_gate = _pack_in_proj(w_in_layers, layer, b_in)
    mq, mk, mv, mo, fq, fk, fv, gates = _inproj(x2, w_main, b_main, m_conv_w, m_conv_b[None, :], s)
    ccol, crel_k, crel_q = _fox_cumsum(gates, bsz, s)
    hm = _mlstm(mq, mk, mv, mo, gates, m_norm_g, bsz, s)
    hf = _fox_attention(fq, fk, fv, ccol, crel_k, crel_q, bsz, s)

    n_exp = w_router.shape[1]
    wr = jnp.zeros((d, V7X_LANES), F32).at[:, :n_exp].set(w_router)
    wr_hi = wr.astype(BF16)
    wr_lo = (wr - wr_hi.astype(F32)).astype(BF16)
    br = jnp.zeros((1, V7X_LANES), F32).at[0, :n_exp].set(b_router)
    h1, h1p, gate, tope, tile_counts = _merge(
        dn_alpha, hm, hf, x2, w_gate, b_gate, w_bm.astype(BF16), w_bf.astype(BF16), w_o.astype(BF16),
        ln1_g[None, :], ln1_b[None, :], wr_hi, wr_lo, br)

    n_blocks = -(-(t * TOP_K) // MOE_BLOCK) + N_EXPERTS
    dest, table = _routing(tile_counts, tope, n_blocks)
    dest_km = dest[:TOP_K].reshape(TOP_K * t)
    block_e, block_valid = table[:, 0], table[:, 1]
    xs = _sc_dispatch(dest_km, h1p, n_blocks * MOE_BLOCK)
    y_rows = _experts(block_e, block_valid, xs, w_gu, b_gu[:, None, :], w_dn, b_dn[:, None, :])
    yg = _sc_gather(dest_km, y_rows).reshape(TOP_K, t, d // 2)
    out = _combine(dn_alpha, h1, gate, ln2_g[None, :], ln2_b[None, :], yg)
    return out.reshape(bsz, s, d)


def kernel(x, w_in, b_in, m_conv_w, m_conv_b, m_norm_g, w_bm, w_bf, w_o, ln1_g, ln1_b,
           w_router, b_router, w_gu, b_gu, w_dn, b_dn, ln2_g, ln2_b):
    depth = w_in.shape[0]
    h = x
    for l in range(depth):
        h = _layer(h, depth, l, w_in, b_in[l], m_conv_w[l], m_conv_b[l], m_norm_g[l], w_bm[l], w_bf[l],
                   w_o[l], ln1_g[l], ln1_b[l], w_router[l], b_router[l], w_gu[l], b_gu[l], w_dn[l],
                   b_dn[l], ln2_g[l], ln2_b[l])
    return h
```

```python
import functools

import jax
import jax.numpy as jnp
from jax import lax
from jax.experimental import pallas as pl
from jax.experimental.pallas import tpu as pltpu
from jax.experimental.pallas import tpu_sc as plsc

F32 = jnp.float32
BF16 = jnp.bfloat16
I32 = jnp.int32
U32 = jnp.uint32

M_HEADS = 4
M_DQK = 128
M_DV = 128
CONV_W = 4
F_HEADS = 8
F_DH = 64
N_EXPERTS = 32
TOP_K = 4
SWIGLU_ALPHA = 1.702
SWIGLU_LIMIT = 7.0
LN_EPS = 1e-5
LOG2_E = 1.4426950408889634

M_W = M_HEADS * M_DV
F_W = F_HEADS * F_DH
MQK_W = 2 * M_HEADS * M_DQK

V7X_LANES = 128
V7X_SUBLANES = 8
V7X_VMEM_BYTES = 64 * 1024 * 1024
VMEM_LIMIT_BYTES = (V7X_VMEM_BYTES * 3) // 4
EXPERT_VMEM_LIMIT_BYTES = (V7X_VMEM_BYTES * 7) // 8
V7X_SC_CORES = 2
V7X_SC_SUBCORES = 16
V7X_SC_WORKERS = V7X_SC_CORES * V7X_SC_SUBCORES

INPROJ_ROWS = 512
MLSTM_CHUNK = 256
MLSTM_SEQS = 4
MLSTM_STEP_CHUNKS = 2
FOX_Q_BLOCK = 512
FOX_K_BLOCK = 256
MERGE_ROWS = 512
MERGE_SUB_ROWS = 256
ROUTE_ROWS = 2048
ROUTE_SUB_ROWS = 256
MOE_BLOCK = 1024
MOE_SUB_ROWS = 256
SC_SCATTER_ROWS = 128
SC_GATHER_ROWS = 64
SC_GATHER_WAYS = 2
COMBINE_ROWS = 1024

GATE_COLS = V7X_LANES
MI_LANE = 0
MF_LANE = M_HEADS
FF_LANE = 2 * M_HEADS


def _params(semantics):
    return pltpu.CompilerParams(dimension_semantics=semantics, vmem_limit_bytes=VMEM_LIMIT_BYTES)


def _log_sigmoid(x):
    return jnp.minimum(x, 0.0) - jnp.log1p(jnp.exp(-jnp.abs(x)))


def _sigmoid(x):
    return 0.5 * jnp.tanh(0.5 * x) + 0.5


def _dot(a, b):
    return jnp.dot(a, b, preferred_element_type=F32)


def _dot_nt(a, b):
    return lax.dot_general(a, b, (((1,), (1,)), ((), ())), preferred_element_type=F32)


def _split3(x):
    hi = x.astype(BF16)
    r1 = x - hi.astype(F32)
    mid = r1.astype(BF16)
    lo = (r1 - mid.astype(F32)).astype(BF16)
    return hi, mid, lo


def _dot_mask_f32(mask_bf16, x):
    hi, mid, lo = _split3(x)
    return (_dot(mask_bf16, lo) + _dot(mask_bf16, mid)) + _dot(mask_bf16, hi)


def _pack_bf16_pairs(x):
    half = x.shape[1] // 2
    bits = lax.bitcast_convert_type(x.astype(BF16).astype(F32), U32)
    return (bits[:, :half] >> 16) | bits[:, half:]


def _unpack_bf16_pairs(words):
    lo = lax.bitcast_convert_type(words << 16, F32)
    hi = lax.bitcast_convert_type(words & jnp.uint32(0xFFFF0000), F32)
    return jnp.concatenate([lo, hi], axis=1)


def _tril_mask(n, strict=False):
    r = lax.broadcasted_iota(I32, (n, n), 0)
    c = lax.broadcasted_iota(I32, (n, n), 1)
    return (r > c) if strict else (r >= c)


_OFF_MQK = 0
_OFF_MV = _OFF_MQK + MQK_W
_OFF_MO = _OFF_MV + M_W
_OFF_FQ = _OFF_MO + M_W
_OFF_FK = _OFF_FQ + F_W
_OFF_FV = _OFF_FK + F_W
_OFF_GATES = _OFF_FV + F_W
_PACKED_COLS = _OFF_GATES + GATE_COLS


def _inproj_kernel(tiles_per_seq, x_ref, w_ref, b_ref, cw_ref, cb_ref,
                   mq_ref, mk_ref, mv_ref, mo_ref, fq_ref, fk_ref, fv_ref, gates_ref, ext_ref):
    tm = x_ref.shape[0]
    pad = V7X_SUBLANES
    xb = x_ref[...].astype(BF16)

    def seg(lo, width):
        return _dot(xb, w_ref[:, lo:lo + width]) + b_ref[:, lo:lo + width]

    @pl.when(pl.program_id(0) % tiles_per_seq == 0)
    def _():
        ext_ref[0:pad, :] = jnp.zeros((pad, MQK_W), F32)

    def conv_silu(cols):
        ext_ref[pad:pad + tm, cols] = seg(_OFF_MQK + cols.start, cols.stop - cols.start)
        y = cb_ref[:, cols] + cw_ref[CONV_W - 1:CONV_W, cols] * ext_ref[pad:pad + tm, cols]
        for k in range(CONV_W - 1):
            shift = CONV_W - 1 - k
            y = y + cw_ref[k:k + 1, cols] * ext_ref[pad - shift:pad - shift + tm, cols]
        ext_ref[0:pad, cols] = ext_ref[tm:tm + pad, cols]
        return y * _sigmoid(y)

    half = MQK_W // 2
    group = half // 2
    mq_ref[:, :group] = conv_silu(slice(0, group)).astype(BF16)
    mv_ref[...] = seg(_OFF_MV, M_W).astype(BF16)
    mq_ref[:, group:] = conv_silu(slice(group, half)).astype(BF16)
    mo_ref[...] = seg(_OFF_MO, M_W)
    mk_ref[:, :group] = conv_silu(slice(half, half + group)) * (M_DQK ** -0.5)
    fq_ref[...] = (seg(_OFF_FQ, F_W) * (F_DH ** -0.5 * LOG2_E)).astype(BF16)
    mk_ref[:, group:] = conv_silu(slice(half + group, MQK_W)) * (M_DQK ** -0.5)
    fk_ref[...] = seg(_OFF_FK, F_W).astype(BF16)
    fv_ref[...] = seg(_OFF_FV, F_W).astype(BF16)
    gates_ref[...] = seg(_OFF_GATES, GATE_COLS)


def _inproj(x2, w_packed, b_packed, conv_w, conv_b, s):
    t, d = x2.shape
    tm = INPROJ_ROWS
    row = lambda i: (i, 0)
    const = lambda i: (0, 0)
    out_shapes = (
        jax.ShapeDtypeStruct((t, MQK_W // 2), BF16),
        jax.ShapeDtypeStruct((t, MQK_W // 2), F32),
        jax.ShapeDtypeStruct((t, M_W), BF16),
        jax.ShapeDtypeStruct((t, M_W), F32),
        jax.ShapeDtypeStruct((t, F_W), BF16),
        jax.ShapeDtypeStruct((t, F_W), BF16),
        jax.ShapeDtypeStruct((t, F_W), BF16),
        jax.ShapeDtypeStruct((t, GATE_COLS), F32),
    )
    return pl.pallas_call(
        functools.partial(_inproj_kernel, s // tm),
        grid=(t // tm,),
        in_specs=[
            pl.BlockSpec((tm, d), row),
            pl.BlockSpec((d, _PACKED_COLS), const),
            pl.BlockSpec((1, _PACKED_COLS), const),
            pl.BlockSpec((CONV_W, MQK_W), const),
            pl.BlockSpec((1, MQK_W), const),
        ],
        out_specs=tuple(pl.BlockSpec((tm, o.shape[1]), row) for o in out_shapes),
        out_shape=out_shapes,
        scratch_shapes=[pltpu.VMEM((tm + V7X_SUBLANES, MQK_W), F32)],
        compiler_params=_params(("arbitrary",)),
        name="inproj",
    )(x2, w_packed, b_packed, conv_w, conv_b)


def _fox_cumsum_kernel(g_ref, ccol_ref, crel_k_ref, crel_q_ref):
    s = g_ref.shape[0]
    cb = FOX_K_BLOCK
    per_q = FOX_Q_BLOCK // cb
    tri = _tril_mask(cb).astype(BF16)
    carry = jnp.zeros((1, GATE_COLS), F32)
    for j in range(s // cb):
        rows = slice(j * cb, (j + 1) * cb)
        if j % per_q == 0:
            q_carry = carry
        within = _dot_mask_f32(tri, _log_sigmoid(g_ref[rows, :])) * LOG2_E
        crel_k_ref[rows, :] = within
        crel_q_ref[rows, :] = within + (carry - q_carry)
        ccol_ref[rows, :] = within + carry
        carry = carry + within[cb - 1:cb, :]


def _fox_cumsum(gates, bsz, s):
    t = gates.shape[0]
    spec = pl.BlockSpec((s, GATE_COLS), lambda b: (b, 0))
    shape = jax.ShapeDtypeStruct((t, GATE_COLS), F32)
    return pl.pallas_call(
        _fox_cumsum_kernel,
        grid=(bsz,),
        in_specs=[spec],
        out_specs=(spec, spec, spec),
        out_shape=(shape, shape, shape),
        compiler_params=_params(("parallel",)),
        name="fox_cumsum",
    )(gates)


def _mlstm_kernel(mq_ref, mk_ref, mv_ref, mo_ref, gates_ref, ng_ref, hm_ref, state_ref, m_ref):
    @pl.when(pl.program_id(1) == 0)
    def _():
        state_ref[...] = jnp.zeros(state_ref.shape, F32)
        m_ref[...] = jnp.zeros(m_ref.shape, F32)

    L = MLSTM_CHUNK
    seqs = range(mq_ref.shape[0])
    states = [[state_ref[bb, h] for h in range(M_HEADS)] for bb in seqs]
    maxes = [[m_ref[bb, h][0:1, 0:1] for h in range(M_HEADS)] for bb in seqs]
    for c in range(mq_ref.shape[1] // L):
        rows = pl.ds(c * L, L)
        results = [_mlstm_chunk(mq_ref.at[bb, rows], mk_ref.at[bb, rows], mv_ref.at[bb, rows],
                                mo_ref.at[bb, rows], gates_ref.at[bb, rows], ng_ref, states[bb], maxes[bb])
                   for bb in seqs]
        for bb in seqs:
            for h, (out_h, _, _) in enumerate(results[bb]):
                hm_ref[bb, c * L:(c + 1) * L, h * M_DV:(h + 1) * M_DV] = out_h
        states = [[res[1] for res in results[bb]] for bb in seqs]
        maxes = [[res[2] for res in results[bb]] for bb in seqs]
    for bb in seqs:
        for h in range(M_HEADS):
            state_ref[bb, h] = states[bb][h]
            m_ref[bb, h] = jnp.broadcast_to(maxes[bb][h], m_ref.shape[2:])


def _mlstm_chunk(mq_ref, mk_ref, mv_ref, mo_ref, gates_ref, ng_ref, states, maxes):
    L = MLSTM_CHUNK
    reps = L // V7X_LANES
    results = []
    gates = gates_ref[...]
    bfull = _dot_mask_f32(_tril_mask(L).astype(BF16), _log_sigmoid(gates))
    b_rows = bfull.T
    z_all = gates - pltpu.roll(bfull, shift=V7X_LANES - (MF_LANE - MI_LANE), axis=1)
    visible = (lax.broadcasted_iota(I32, (L, L), 0) <= lax.broadcasted_iota(I32, (L, L), 1))
    ones_rows = (lax.broadcasted_iota(I32, (M_DV, L), 0) == 0).astype(BF16)

    for h in range(M_HEADS):
        b_row = b_rows[MF_LANE + h:MF_LANE + h + 1, :]
        g_tot = b_row[:, L - 1:L]
        m_prev = maxes[h]
        z = jnp.broadcast_to(z_all[:, MI_LANE + h:MI_LANE + h + 1], (L, V7X_LANES))

        q_h = mq_ref[:, h * M_DQK:(h + 1) * M_DQK]
        k_f = mk_ref[:, h * M_DQK:(h + 1) * M_DQK]
        k_h = k_f.astype(BF16)
        v_t = mv_ref[:, h * M_DV:(h + 1) * M_DV].astype(F32).T.astype(BF16)
        cn_t = states[h]

        dlog = jnp.where(visible, b_row + jnp.tile(z, (1, reps)), -jnp.inf)
        inter_log = b_row + m_prev
        m_t = jnp.maximum(inter_log, jnp.max(dlog, axis=0, keepdims=True))
        w_inter = jnp.exp(inter_log - m_t)
        qkw = _dot_nt(k_h, q_h) * jnp.exp(dlog - m_t)
        qc = _dot_nt(cn_t.astype(BF16), q_h)
        num = w_inter * qc[:M_DV, :] + _dot(v_t, qkw.astype(BF16))
        den = w_inter * qc[M_DV:M_DV + 1, :] + jnp.sum(qkw, axis=0, keepdims=True)
        hh = num / jnp.maximum(jnp.abs(den), jnp.exp(-m_t))

        mu = jnp.mean(hh, axis=0, keepdims=True)
        dv = hh - mu
        var = jnp.mean(dv * dv, axis=0, keepdims=True)
        hn = (dv * lax.rsqrt(var + LN_EPS)) * jnp.tile(ng_ref[h * M_DV:(h + 1) * M_DV, :], (1, reps))
        out_h = (_sigmoid(mo_ref[:, h * M_DV:(h + 1) * M_DV]) * hn.T).astype(BF16)

        a = g_tot + z
        m_new = jnp.maximum(g_tot + m_prev, jnp.max(a, axis=0, keepdims=True)[:, 0:1])
        decay = jnp.exp(g_tot + m_prev - m_new)
        kw = (k_f * jnp.exp(a - m_new)).astype(BF16)
        v_aug = jnp.concatenate([v_t, ones_rows], axis=0)
        results.append((out_h, decay * cn_t + _dot(v_aug, kw), m_new))
    return results


def _mlstm(mq, mk, mv, mo, gates, norm_g, bsz, s):
    t = mq.shape[0]
    rows = MLSTM_CHUNK * MLSTM_STEP_CHUNKS
    nb = MLSTM_SEQS
    seq = lambda a: a.reshape(bsz, s, a.shape[1])
    blk = lambda width: pl.BlockSpec((nb, rows, width), lambda g, n: (g, n, 0))
    const = lambda g, n: (0, 0)
    hm = pl.pallas_call(
        _mlstm_kernel,
        grid=(bsz // nb, s // rows),
        in_specs=[
            blk(MQK_W // 2), blk(MQK_W // 2), blk(M_W), blk(M_W), blk(GATE_COLS),
            pl.BlockSpec((M_W, V7X_LANES), const),
        ],
        out_specs=blk(M_W),
        out_shape=jax.ShapeDtypeStruct((bsz, s, M_W), BF16),
        scratch_shapes=[
            pltpu.VMEM((nb, M_HEADS, 2 * M_DV, M_DQK), F32),
            pltpu.VMEM((nb, M_HEADS, V7X_SUBLANES, V7X_LANES), F32),
        ],
        compiler_params=_params(("parallel", "arbitrary")),
        name="mlstm",
    )(seq(mq), seq(mk), seq(mv), seq(mo), seq(gates),
      jnp.broadcast_to(norm_g[:, None], (M_W, V7X_LANES)))
    return hm.reshape(t, M_W)


_FOX_FEATURES = 3


def _fox_operands(x, c_tile, c_lane, key_side):
    rows = x.shape[0]
    nf = _FOX_FEATURES
    first = 0 if key_side else nf
    src = lax.broadcasted_iota(I32, (V7X_LANES, V7X_LANES), 0) - c_lane
    dst = lax.broadcasted_iota(I32, (V7X_LANES, V7X_LANES), 1)
    feats = jnp.zeros((rows, V7X_LANES), F32)
    for n, part in enumerate(_split3(c_tile)):
        place = jnp.logical_or(jnp.logical_and(src == 0, dst == F_DH + first + n),
                               jnp.logical_and(src == 1, dst == first + n))
        feats = feats + _dot(part, place.astype(BF16))
    lane = lax.broadcasted_iota(I32, (rows, V7X_LANES), 1)
    within = lane % F_DH
    const_lanes = jnp.logical_and(within >= nf - first, within < 2 * nf - first)
    feats = jnp.where(const_lanes, 1.0 if key_side else -1.0, feats).astype(BF16)
    low = lane < F_DH
    return jnp.where(low, x, feats), jnp.where(low, feats, x)


def _fox_kernel(q_ref, k_ref, v_ref, ccol_ref, crel_k_ref, crel_q_ref, o_ref, qaug_ref, kaug_ref, vt_ref,
                *scratch):
    blk = FOX_K_BLOCK
    tq = FOX_Q_BLOCK
    p = pl.program_id(1)
    c_lane = FF_LANE + 2 * p
    qaug_ref[0], qaug_ref[1] = _fox_operands(q_ref[...], crel_q_ref[...], c_lane, False)
    kaug_ref[0], kaug_ref[1] = _fox_operands(k_ref[...], crel_k_ref[...], c_lane, True)
    v_t = v_ref[...].astype(F32).T
    for j in range(vt_ref.shape[0]):
        vt_ref[j] = v_t[:, j * blk:(j + 1) * blk].astype(BF16)
    for i in range(o_ref.shape[0] // tq):
        _fox_query_block(i, p, ccol_ref, o_ref.at[pl.ds(i * tq, tq)], qaug_ref, kaug_ref, vt_ref, *scratch)


def _fox_query_block(i, p, ccol_ref, o_ref, qaug_ref, kaug_ref, vt_ref,
                     st_a0, st_a1, st_b0, st_b1, pe_a0, pe_a1, pe_b0, pe_b1, acc0, acc1):
    blk = FOX_K_BLOCK
    tq = FOX_Q_BLOCK
    assert tq == 2 * blk
    strips = tq // V7X_LANES
    st_a_refs, st_b_refs = (st_a0, st_a1), (st_b0, st_b1)
    pe_a_refs, pe_b_refs = (pe_a0, pe_a1), (pe_b0, pe_b1)
    acc_refs = (acc0, acc1)

    q_start = i * tq
    q_heads = [qaug_ref[hh, pl.ds(q_start, tq), :] for hh in range(2)]
    key_row = lax.broadcasted_iota(I32, (blk, V7X_LANES), 0)
    query_col = lax.broadcasted_iota(I32, (blk, V7X_LANES), 1)
    last_chunk = 2 * i + 1
    head_lane = lax.broadcasted_iota(I32, (1, GATE_COLS), 1) - (FF_LANE + 2 * p)

    def c_before(position, hh):
        row = ccol_ref[pl.ds(jnp.maximum(position - 1, 0), 1), :]
        keep = jnp.logical_and(head_lane == hh, position > 0)
        return jnp.sum(jnp.where(keep, row, 0.0), axis=-1, keepdims=True)

    c_query0 = [c_before(q_start, hh) for hh in range(2)]

    def put_scores(st_ref, j, hh, first_strip=0):
        start = j * blk if isinstance(j, int) else pl.multiple_of(j * blk, blk)
        queries = q_heads[hh][first_strip * V7X_LANES:, :]
        st = _dot_nt(kaug_ref[hh, pl.ds(start, blk), :], queries)
        for c in range(first_strip, strips):
            st_ref[c] = st[:, (c - first_strip) * V7X_LANES:(c - first_strip + 1) * V7X_LANES]

    def exponentials(pe_ref, first_strip=0):
        return jnp.concatenate([pe_ref[c] for c in range(first_strip, strips)], axis=1)

    hidden_strips = blk // V7X_LANES

    def softmax_update(st_ref, pe_ref, m_old, l_old, base, key_minus_query=None):
        alphas, ms, ls = [], [], []
        for c in range(strips):
            cols = slice(c * V7X_LANES, (c + 1) * V7X_LANES)
            gap = None if key_minus_query is None else key_minus_query - c * V7X_LANES
            if gap is not None and gap - (V7X_LANES - 1) > 0:
                pe_ref[c] = jnp.zeros((blk, V7X_LANES), BF16)
                alphas.append(jnp.ones((1, V7X_LANES), F32))
                ms.append(m_old[:, cols])
                ls.append(l_old[:, cols])
                continue
            st = st_ref[c]
            if gap is not None and gap + (blk - 1) > 0:
                st = jnp.where(key_row + gap <= query_col, st, -jnp.inf)
            m_new = jnp.maximum(m_old[:, cols], jnp.max(st, axis=0, keepdims=True) + base)
            alpha = jnp.exp2(m_old[:, cols] - m_new)
            pe = jnp.exp2(st - (m_new - base))
            pe_ref[c] = pe.astype(BF16)
            alphas.append(alpha)
            ms.append(m_new)
            ls.append(alpha * l_old[:, cols] + jnp.sum(pe, axis=0, keepdims=True))
        cat = lambda parts: jnp.concatenate(parts, axis=1)
        return cat(alphas), cat(ms), cat(ls)

    def pair(mi, carry, diagonal=False):
        a = 2 * mi
        b = a + 1
        v_prev = vt_ref[jnp.maximum(a - 1, 0)]
        v_a = vt_ref[a]
        partial = []
        for hh in range(2):
            alpha_prev = carry[hh][0]
            partial.append(alpha_prev * acc_refs[hh][...] + _dot(v_prev, exponentials(pe_b_refs[hh])))
            put_scores(st_b_refs[hh], b, hh, hidden_strips if diagonal else 0)
        stats = []
        for hh in range(2):
            _, m_old, l_old = carry[hh]
            stats.append(softmax_update(st_a_refs[hh], pe_a_refs[hh], m_old, l_old,
                                        c_query0[hh] - c_before(a * blk, hh), 0 if diagonal else None))
        for hh in range(2):
            alpha_a = stats[hh][0]
            acc_refs[hh][...] = alpha_a * partial[hh] + _dot(v_a, exponentials(pe_a_refs[hh]))
            if not diagonal:
                put_scores(st_a_refs[hh], a + 2, hh)
        return tuple(softmax_update(st_b_refs[hh], pe_b_refs[hh], stats[hh][1], stats[hh][2],
                                    c_query0[hh] - c_before(b * blk, hh), blk if diagonal else None)
                     for hh in range(2))

    for hh in range(2):
        put_scores(st_a_refs[hh], 0, hh)
        pe_b_refs[hh][...] = jnp.zeros(pe_b_refs[hh].shape, BF16)
        acc_refs[hh][...] = jnp.zeros((V7X_LANES, tq), F32)
    init = tuple((jnp.ones((1, tq), F32), jnp.full((1, tq), -jnp.inf, F32), jnp.zeros((1, tq), F32))
                 for _ in range(2))
    carry = init
    for mi in range(i):
        carry = pair(mi, carry)
    final = pair(i, carry, diagonal=True)
    v_last = vt_ref[last_chunk]
    outs = []
    for hh in range(2):
        alpha, _, l_fin = final[hh]
        tail = _dot(v_last, exponentials(pe_b_refs[hh], hidden_strips))
        tail = jnp.concatenate([jnp.zeros((V7X_LANES, hidden_strips * V7X_LANES), F32), tail], axis=1)
        outs.append((alpha * acc_refs[hh][...] + tail) / l_fin)
    row = lax.broadcasted_iota(I32, (V7X_LANES, tq), 0)
    o_t = jnp.where(row < F_DH, outs[0], outs[1])
    o_ref[...] = o_t.T.astype(BF16)


def _fox_attention(fq, fk, fv, ccol, crel_k, crel_q, bsz, s):
    t = fq.shape[0]
    blk = FOX_K_BLOCK
    tq = FOX_Q_BLOCK
    pairs = F_HEADS // 2
    kvmap = lambda b, p: (b, p)
    return pl.pallas_call(
        _fox_kernel,
        grid=(bsz, pairs),
        in_specs=[
            pl.BlockSpec((s, V7X_LANES), kvmap),
            pl.BlockSpec((s, V7X_LANES), kvmap),
            pl.BlockSpec((s, V7X_LANES), kvmap),
            pl.BlockSpec((s, GATE_COLS), lambda b, p: (b, 0)),
            pl.BlockSpec((s, GATE_COLS), lambda b, p: (b, 0)),
            pl.BlockSpec((s, GATE_COLS), lambda b, p: (b, 0)),
        ],
        out_specs=pl.BlockSpec((s, V7X_LANES), kvmap),
        out_shape=jax.ShapeDtypeStruct((t, F_W), BF16),
        scratch_shapes=[
            pltpu.VMEM((2, s, V7X_LANES), BF16),
            pltpu.VMEM((2, s, V7X_LANES), BF16),
            pltpu.VMEM((s // blk, V7X_LANES, blk), BF16),
        ] + [pltpu.VMEM((tq // V7X_LANES, blk, V7X_LANES), F32)] * 4 + [pltpu.VMEM((tq // V7X_LANES, blk, V7X_LANES), BF16)] * 4
          + [pltpu.VMEM((V7X_LANES, tq), F32)] * 2,
        compiler_params=_params(("parallel", "parallel")),
        name="fox_attention",
    )(fq, fk, fv, ccol, crel_k, crel_q)


def _layer_norm_rows(r, g, b):
    mu = jnp.mean(r, axis=-1, keepdims=True)
    d = r - mu
    var = jnp.mean(d * d, axis=-1, keepdims=True)
    return (d * lax.rsqrt(var + LN_EPS)) * g + b


def _merge_kernel(dn_alpha, hm_ref, hf_ref, x_ref, wg_ref, bg_ref, wbm_ref, wbf_ref, wo_ref,
                  g_ref, b_ref, wrh_ref, wrl_ref, br_ref, h1_ref, h1p_ref, gate_ref, tope_ref, cnt_ref,
                  resid_ref):
    i = pl.program_id(0)
    slot = i % 2

    @pl.when(i == 0)
    def _():
        resid_ref[1] = jnp.zeros(resid_ref.shape[1:], F32)

    subs = [slice(n * MERGE_SUB_ROWS, (n + 1) * MERGE_SUB_ROWS)
            for n in range(x_ref.shape[0] // MERGE_SUB_ROWS)]
    counts = jnp.zeros((1, V7X_LANES), F32)
    for rows in subs:
        counts = counts + _merge_tail(resid_ref[1 - slot, rows, :], rows, g_ref, b_ref, wrh_ref, wrl_ref,
                                      br_ref, h1_ref, h1p_ref, gate_ref, tope_ref)
    sub = lax.broadcasted_iota(I32, cnt_ref.shape, 0)
    cnt_ref[...] = jnp.where(sub == 0, counts, 0.0)

    for rows in subs:
        x = x_ref[rows, :]
        d = x.shape[1]
        gmf = _dot(x.astype(BF16), wg_ref[...]) + bg_ref[...]
        ym = _dot(hm_ref[rows, :], wbm_ref[...])
        yf = _dot(hf_ref[rows, :], wbf_ref[...])
        y = _sigmoid(gmf[:, :d]) * ym + _sigmoid(gmf[:, d:]) * yf
        resid_ref[slot, rows, :] = dn_alpha * x + _dot(y.astype(BF16), wo_ref[...])


def _merge_tail(resid, rows, g_ref, b_ref, wrh_ref, wrl_ref, br_ref, h1_ref, h1p_ref, gate_ref, tope_ref):
    h1 = _layer_norm_rows(resid, g_ref[...], b_ref[...])
    h1_ref[rows, :] = h1

    h1p_ref[rows, :] = _pack_bf16_pairs(h1)
    hb = h1.astype(BF16)

    lo = (h1 - hb.astype(F32)).astype(BF16)
    logits = (_dot(lo, wrh_ref[...]) + _dot(hb, wrl_ref[...])) + _dot(hb, wrh_ref[...]) + br_ref[...]
    tm = logits.shape[0]
    lane = lax.broadcasted_iota(I32, (tm, V7X_LANES), 1)
    vals = jnp.where(lane < N_EXPERTS, logits, -jnp.inf)
    top_v, top_i = [], []
    for _ in range(TOP_K):
        mx = jnp.max(vals, axis=-1, keepdims=True)
        idx = jnp.min(jnp.where(vals == mx, lane, V7X_LANES), axis=-1, keepdims=True)
        top_v.append(mx)
        top_i.append(idx)
        vals = jnp.where(lane == idx, -jnp.inf, vals)
    ex = [jnp.exp(v - top_v[0]) for v in top_v]
    tot = ex[0]
    for e in ex[1:]:
        tot = tot + e
    gate = jnp.zeros((tm, V7X_LANES), F32)
    tope = jnp.zeros((tm, V7X_LANES), I32)
    member = jnp.zeros((tm, V7X_LANES), F32)
    for k in range(TOP_K):
        gate = jnp.where(lane == k, ex[k] / tot, gate)
        tope = jnp.where(lane == k, top_i[k], tope)
        member = member + (lane == top_i[k]).astype(F32)
    gate_ref[rows, :] = gate
    tope_ref[rows, :] = tope
    return jnp.sum(member, axis=0, keepdims=True)


def _merge(dn_alpha, hm, hf, x2, wg, bg, wbm, wbf, wo, ln_g, ln_b, wr_hi, wr_lo, br):
    t, d = x2.shape
    tm = MERGE_ROWS
    nt = t // tm
    row = lambda i: (jnp.minimum(i, nt - 1), 0)
    out_row = lambda i: (jnp.maximum(i - 1, 0), 0)
    const = lambda i: (0, 0)
    full = lambda a: pl.BlockSpec(a.shape, const)
    return pl.pallas_call(
        functools.partial(_merge_kernel, dn_alpha),
        grid=(nt + 1,),
        in_specs=[
            pl.BlockSpec((tm, M_W), row),
            pl.BlockSpec((tm, F_W), row),
            pl.BlockSpec((tm, d), row),
            full(wg), full(bg), full(wbm), full(wbf), full(wo), full(ln_g), full(ln_b),
            full(wr_hi), full(wr_lo), full(br),
        ],
        out_specs=(
            pl.BlockSpec((tm, d), out_row),
            pl.BlockSpec((tm, d // 2), out_row),
            pl.BlockSpec((tm, V7X_LANES), out_row),
            pl.BlockSpec((tm, V7X_LANES), out_row),
            pl.BlockSpec((V7X_SUBLANES, V7X_LANES), out_row),
        ),
        out_shape=(
            jax.ShapeDtypeStruct((t, d), F32),
            jax.ShapeDtypeStruct((t, d // 2), U32),
            jax.ShapeDtypeStruct((t, V7X_LANES), F32),
            jax.ShapeDtypeStruct((t, V7X_LANES), I32),
            jax.ShapeDtypeStruct((t // tm * V7X_SUBLANES, V7X_LANES), F32),
        ),
        scratch_shapes=[pltpu.VMEM((2, tm, d), F32)],
        compiler_params=_params(("arbitrary",)),
        name="merge_ln1_router",
    )(hm, hf, x2, wg, bg, wbm, wbf, wo, ln_g, ln_b, wr_hi, wr_lo, br)


def _lane_cumsum(x):
    lane = lax.broadcasted_iota(I32, x.shape, 1)
    d = 1
    while d < V7X_LANES:
        x = x + jnp.where(lane >= d, pltpu.roll(x, shift=d, axis=1), 0.0)
        d *= 2
    return x


def _routing_kernel(cnt_ref, tope_ref, dest_ref, table_ref, run_ref, start_ref):
    sb = ROUTE_SUB_ROWS

    @pl.when(pl.program_id(0) == 0)
    def _():
        total = jnp.sum(cnt_ref[...], axis=0, keepdims=True)
        counts = jnp.broadcast_to(total, (V7X_SUBLANES, V7X_LANES))
        padded = jnp.ceil(counts * (1.0 / MOE_BLOCK)) * MOE_BLOCK
        pad_end = _lane_cumsum(padded)
        pad_start = pad_end - padded
        start_ref[...] = pad_start
        run_ref[...] = jnp.zeros(run_ref.shape, F32)
        nb = table_ref.shape[0]
        blk = lax.broadcasted_iota(I32, (nb, V7X_LANES), 0).astype(F32) * MOE_BLOCK
        ln = lax.broadcasted_iota(I32, (nb, V7X_LANES), 1)
        done = jnp.logical_and(pad_end[0:1, :] <= blk, ln < N_EXPERTS)
        be = jnp.minimum(jnp.sum(done.astype(F32), axis=-1, keepdims=True), N_EXPERTS - 1.0)
        onehot = ln == be.astype(I32)
        cnt_e = jnp.sum(jnp.where(onehot, counts[0:1, :], 0.0), axis=-1, keepdims=True)
        start_e = jnp.sum(jnp.where(onehot, pad_start[0:1, :], 0.0), axis=-1, keepdims=True)
        valid = jnp.clip(cnt_e - (blk[:, 0:1] - start_e), 0.0, float(MOE_BLOCK))
        table_ref[...] = jnp.where(ln == 0, be.astype(I32),
                                   jnp.where(ln == 1, valid.astype(I32), 0))

    earlier = _tril_mask(sb, strict=True).astype(BF16)
    lane = lax.broadcasted_iota(I32, (sb, V7X_LANES), 1)
    offset = run_ref[0:1, :] + start_ref[0:1, :]
    seen = jnp.zeros((1, V7X_LANES), F32)
    for j in range(tope_ref.shape[0] // sb):
        tope = tope_ref[j * sb:(j + 1) * sb, :]
        hit = [lane == tope[:, k:k + 1] for k in range(TOP_K)]
        member = jnp.zeros((sb, V7X_LANES), F32)
        for k in range(TOP_K):
            member = member + hit[k].astype(F32)
        base = _dot(earlier, member.astype(BF16)) + (offset + seen)
        dest = jnp.zeros((sb, V7X_LANES), F32)
        for k in range(TOP_K):
            dk = jnp.sum(jnp.where(hit[k], base, 0.0), axis=-1, keepdims=True)
            dest = jnp.where(lane == k, dk, dest)
        dest_ref[:, j * sb:(j + 1) * sb] = dest.T[0:V7X_SUBLANES, :].astype(I32)
        seen = seen + jnp.sum(member, axis=0, keepdims=True)
    run_ref[...] = run_ref[...] + seen


def _routing(tile_counts, tope, n_blocks):
    t = tope.shape[0]
    tr = ROUTE_ROWS
    return pl.pallas_call(
        _routing_kernel,
        grid=(t // tr,),
        in_specs=[pl.BlockSpec(tile_counts.shape, lambda i: (0, 0)),
                  pl.BlockSpec((tr, V7X_LANES), lambda i: (i, 0))],
        out_specs=(
            pl.BlockSpec((V7X_SUBLANES, tr), lambda i: (0, i)),
            pl.BlockSpec((n_blocks, V7X_LANES), lambda i: (0, 0)),
        ),
        out_shape=(
            jax.ShapeDtypeStruct((V7X_SUBLANES, t), I32),
            jax.ShapeDtypeStruct((n_blocks, V7X_LANES), I32),
        ),
        scratch_shapes=[
            pltpu.VMEM((V7X_SUBLANES, V7X_LANES), F32),
            pltpu.VMEM((V7X_SUBLANES, V7X_LANES), F32),
        ],
        compiler_params=_params(("arbitrary",)),
        name="routing",
    )(tile_counts, tope)


def _sc_worker_id():
    return lax.axis_index("s") * V7X_SC_CORES + lax.axis_index("c")


def _sc_mesh():
    return plsc.VectorSubcoreMesh(core_axis_name="c", subcore_axis_name="s",
                                  num_cores=V7X_SC_CORES, num_subcores=V7X_SC_SUBCORES)


def _sc_dispatch(dest_km, h1p, n_rows):
    t, w = h1p.shape
    per_worker = t // V7X_SC_WORKERS
    ch = SC_SCATTER_ROWS

    @functools.partial(
        pl.kernel, mesh=_sc_mesh(),
        out_type=jax.ShapeDtypeStruct((n_rows, w), h1p.dtype),
        scratch_types=[pltpu.VMEM((ch, w), h1p.dtype)]
        + [pltpu.VMEM((ch,), I32)] * TOP_K + [pltpu.SemaphoreType.DMA] * TOP_K,
        name="sc_dispatch",
    )
    def scatter_rows(dest_hbm, h1p_hbm, xs_hbm, rows_v, *idx_and_sems):
        idx_refs, sems = idx_and_sems[:TOP_K], idx_and_sems[TOP_K:]
        first = _sc_worker_id() * per_worker

        @pl.loop(0, per_worker // ch)
        def _(j):
            base = first + j * ch
            pltpu.sync_copy(h1p_hbm.at[pl.ds(base, ch)], rows_v)
            copies = []
            for k in range(TOP_K):
                pltpu.sync_copy(dest_hbm.at[pl.ds(k * t + base, ch)], idx_refs[k])
                copies.append(pltpu.async_copy(rows_v, xs_hbm.at[idx_refs[k]], sems[k]))
            for copy in copies:
                copy.wait()

    return scatter_rows(dest_km, h1p)


def _sc_gather(dest_km, y_rows):
    n = dest_km.shape[0]
    w = y_rows.shape[1]
    per_worker = n // V7X_SC_WORKERS
    ch = SC_GATHER_ROWS
    ways = SC_GATHER_WAYS

    @functools.partial(
        pl.kernel, mesh=_sc_mesh(),
        out_type=jax.ShapeDtypeStruct((n, w), y_rows.dtype),
        scratch_types=[pltpu.VMEM((ch,), I32)] * ways + [pltpu.VMEM((ch, w), y_rows.dtype)] * ways
        + [pltpu.SemaphoreType.DMA] * (2 * ways),
        name="sc_gather",
    )
    def gather_rows(dest_hbm, y_hbm, out_hbm, *scratch):
        idx_refs, row_refs = scratch[:ways], scratch[ways:2 * ways]
        gather_sems, store_sems = scratch[2 * ways:3 * ways], scratch[3 * ways:]
        first = _sc_worker_id() * per_worker

        @pl.loop(0, per_worker // (ch * ways))
        def _(j):
            bases = [first + (j * ways + u) * ch for u in range(ways)]
            gathers = []
            for u in range(ways):
                pltpu.sync_copy(dest_hbm.at[pl.ds(bases[u], ch)], idx_refs[u])
                gathers.append(pltpu.async_copy(y_hbm.at[idx_refs[u]], row_refs[u], gather_sems[u]))
            stores = []
            for u in range(ways):
                gathers[u].wait()
                stores.append(pltpu.async_copy(row_refs[u], out_hbm.at[pl.ds(bases[u], ch)], store_sems[u]))
            for store in stores:
                store.wait()

    return gather_rows(dest_km, y_rows)


def _expert_kernel(be_ref, nv_ref, xs_ref, wgu_f32_ref, bgu_ref, wdn_f32_ref, bdn_ref, y_ref,
                   wgu_slots, wdn_slots, slot_ref):
    i = pl.program_id(0)
    n_blocks = pl.num_programs(0) - 1
    nv = jnp.where(i > 0, nv_ref[jnp.maximum(i - 1, 0)], 0)

    @pl.when(i == 0)
    def _():
        slot_ref[0] = 0

    slot = slot_ref[0]
    wgu_ref = wgu_slots.at[slot]
    wdn_ref = wdn_slots.at[slot]

    def ffn(rows):
        x = _unpack_bf16_pairs(xs_ref[rows, :])
        rowid = rows.start + lax.broadcasted_iota(I32, x.shape, 0)
        x = jnp.where(rowid < nv, x, 0.0).astype(BF16)
        gu = _dot(x, wgu_ref[...]) + bgu_ref[...]
        f = gu.shape[1] // 2
        glu = jnp.minimum(gu[:, :f], SWIGLU_LIMIT)
        lin = jnp.clip(gu[:, f:], -SWIGLU_LIMIT, SWIGLU_LIMIT)
        act = glu * _sigmoid(SWIGLU_ALPHA * glu) * (lin + 1.0)
        y_ref[rows, :] = _pack_bf16_pairs(_dot(act.astype(BF16), wdn_ref[...]) + bdn_ref[...])

    @pl.when(nv == 0)
    def _():
        y_ref[...] = jnp.zeros(y_ref.shape, U32)

    n_sub = MOE_BLOCK // MOE_SUB_ROWS
    for used in range(1, n_sub + 1):
        lower = nv > (used - 1) * MOE_SUB_ROWS
        cond = lower if used == n_sub else jnp.logical_and(lower, nv <= used * MOE_SUB_ROWS)

        @pl.when(cond)
        def _(used=used):
            for j in range(used):
                ffn(slice(j * MOE_SUB_ROWS, (j + 1) * MOE_SUB_ROWS))
            if used < n_sub:
                y_ref[used * MOE_SUB_ROWS:, :] = jnp.zeros((MOE_BLOCK - used * MOE_SUB_ROWS, y_ref.shape[1]), U32)

    new_expert = jnp.logical_or(i == 0, be_ref[jnp.minimum(i, n_blocks - 1)] != be_ref[jnp.maximum(i - 1, 0)])

    @pl.when(jnp.logical_and(i < n_blocks, new_expert))
    def _():
        wgu_slots[1 - slot] = wgu_f32_ref[...].astype(BF16)
        wdn_slots[1 - slot] = wdn_f32_ref[...].astype(BF16)
        slot_ref[0] = 1 - slot


def _experts(block_e, block_valid, xs, wgu, bgu, wdn, bdn):
    n_rows, w = xs.shape
    e, d, f2 = wgu.shape
    n_blocks = n_rows // MOE_BLOCK
    ahead = lambda i, be: be[jnp.minimum(i, n_blocks - 1)]
    behind = lambda i: jnp.maximum(i - 1, 0)
    grid_spec = pltpu.PrefetchScalarGridSpec(
        num_scalar_prefetch=2,
        grid=(n_blocks + 1,),
        in_specs=[
            pl.BlockSpec((MOE_BLOCK, w), lambda i, be, nv: (behind(i), 0)),
            pl.BlockSpec((None, d, f2), lambda i, be, nv: (ahead(i, be), 0, 0)),
            pl.BlockSpec((None, 1, f2), lambda i, be, nv: (be[behind(i)], 0, 0)),
            pl.BlockSpec((None, f2 // 2, d), lambda i, be, nv: (ahead(i, be), 0, 0)),
            pl.BlockSpec((None, 1, d), lambda i, be, nv: (be[behind(i)], 0, 0)),
        ],
        out_specs=pl.BlockSpec((MOE_BLOCK, d // 2), lambda i, be, nv: (behind(i), 0)),
        scratch_shapes=[pltpu.VMEM((2, d, f2), BF16), pltpu.VMEM((2, f2 // 2, d), BF16),
                        pltpu.SMEM((1,), I32)],
    )
    return pl.pallas_call(
        _expert_kernel,
        grid_spec=grid_spec,
        out_shape=jax.ShapeDtypeStruct((n_rows, d // 2), U32),
        compiler_params=pltpu.CompilerParams(dimension_semantics=("arbitrary",),
                                             vmem_limit_bytes=EXPERT_VMEM_LIMIT_BYTES),
        name="experts",
    )(block_e, block_valid, xs, wgu, bgu, wdn, bdn)


def _combine_kernel(dn_alpha, h1_ref, gate_ref, g_ref, b_ref, yg_ref, o_ref):
    gate = gate_ref[...]
    ffn = gate[:, 0:1] * _unpack_bf16_pairs(yg_ref[0])
    for k in range(1, TOP_K):
        ffn = ffn + gate[:, k:k + 1] * _unpack_bf16_pairs(yg_ref[k])
    o_ref[...] = _layer_norm_rows(dn_alpha * h1_ref[...] + ffn, g_ref[...], b_ref[...])


def _combine(dn_alpha, h1, gate, ln_g, ln_b, yg):
    t, d = h1.shape
    tc = COMBINE_ROWS
    row = lambda i: (i, 0)
    const = lambda i: (0, 0)
    return pl.pallas_call(
        functools.partial(_combine_kernel, dn_alpha),
        grid=(t // tc,),
        in_specs=[
            pl.BlockSpec((tc, d), row),
            pl.BlockSpec((tc, V7X_LANES), row),
            pl.BlockSpec((1, d), const),
            pl.BlockSpec((1, d), const),
            pl.BlockSpec((TOP_K, tc, d // 2), lambda i: (0, i, 0)),
        ],
        out_specs=pl.BlockSpec((tc, d), row),
        out_shape=jax.ShapeDtypeStruct((t, d), F32),
        compiler_params=_params(("parallel",)),
        name="combine_ln2",
    )(h1, gate, ln_g, ln_b, yg)


def _in_proj_columns(d):
    o = 0
    cols = {}
    for name, width in (("mqk", MQK_W), ("mv", M_W), ("mo", M_W), ("mi", M_HEADS), ("mf", M_HEADS),
                        ("fq", F_W), ("fk", F_W), ("fv", F_W), ("ff", F_HEADS), ("gm", d), ("gf", d)):
        cols[name] = (o, o + width)
        o += width
    return cols


def _pack_w_kernel(w_ref, main_ref, gate_ref):
    d = gate_ref.shape[1] // 2
    cols = _in_proj_columns(d)
    out = 0
    for name in ("mqk", "mv", "mo", "fq", "fk", "fv"):
        lo, hi = cols[name]
        main_ref[:, out:out + hi - lo] = w_ref[:, lo:hi].astype(BF16)
        out += hi - lo
    lane = lax.broadcasted_iota(I32, (w_ref.shape[0], GATE_COLS), 1)
    mi_lo, ff_lo = cols["mi"][0], cols["ff"][0] - 2 * M_HEADS
    assert mi_lo % V7X_LANES == 0 and ff_lo % V7X_LANES == 0 and cols["mf"][0] == mi_lo + M_HEADS
    gates = jnp.where(lane < 2 * M_HEADS, w_ref[:, mi_lo:mi_lo + GATE_COLS],
                      jnp.where(lane < 2 * M_HEADS + F_HEADS, w_ref[:, ff_lo:ff_lo + GATE_COLS], 0.0))
    main_ref[:, out:out + GATE_COLS] = gates.astype(BF16)
    gate_ref[...] = w_ref[:, cols["gm"][0]:cols["gf"][1]].astype(BF16)


def _pack_in_proj(w_in_layers, layer, b_in):
    _, d, n_cols = w_in_layers.shape
    cols = _in_proj_columns(d)
    tr = V7X_LANES
    w_main, w_gate = pl.pallas_call(
        _pack_w_kernel,
        grid=(d // tr,),
        in_specs=[pl.BlockSpec((None, tr, n_cols), lambda i: (layer, i, 0))],
        out_specs=(pl.BlockSpec((tr, _PACKED_COLS), lambda i: (i, 0)),
                   pl.BlockSpec((tr, 2 * d), lambda i: (i, 0))),
        out_shape=(jax.ShapeDtypeStruct((d, _PACKED_COLS), BF16),
                   jax.ShapeDtypeStruct((d, 2 * d), BF16)),
        compiler_params=_params(("parallel",)),
        name="pack_in_proj",
    )(w_in_layers)

    def take(names):
        return [b_in[cols[n][0]:cols[n][1]] for n in names]

    n_gate = 2 * M_HEADS + F_HEADS
    b_main = jnp.concatenate(take(("mqk", "mv", "mo", "fq", "fk", "fv", "mi", "mf", "ff"))
                             + [jnp.zeros((GATE_COLS - n_gate,), b_in.dtype)])
    b_gate = jnp.concatenate(take(("gm", "gf")))
    return w_main, b_main[None, :], w_gate, b_gate[None, :]


def _layer(h, depth, layer, w_in_layers, b_in, m_conv_w, m_conv_b, m_norm_g, w_bm, w_bf, w_o, ln1_g, ln1_b,
           w_router, b_router, w_gu, b_gu, w_dn, b_dn, ln2_g, ln2_b):
    bsz, s, d = h.shape
    t = bsz * s
    dn_alpha = (2.0 * depth) ** 0.25
    x2 = h.reshape(t, d)
    assert d == MQK_W and w_gu.shape[0] == N_EXPERTS, "kernels are written for this layer geometry"
    assert s % INPROJ_ROWS == 0 and s % (MLSTM_CHUNK * MLSTM_STEP_CHUNKS) == 0 and s % FOX_Q_BLOCK == 0
    assert bsz % MLSTM_SEQS == 0
    assert t % MERGE_ROWS == 0 and t % ROUTE_ROWS == 0 and t % COMBINE_ROWS == 0
    assert t % (V7X_SC_WORKERS * SC_SCATTER_ROWS) == 0
    assert (t * TOP_K) % (V7X_SC_WORKERS * SC_GATHER_ROWS * SC_GATHER_WAYS) == 0

    w_main, b_main, w_gate, b_gate = _pack_in_proj(w_in_layers, layer, b_in)
    mq, mk, mv, mo, fq, fk, fv, gates = _inproj(x2, w_main, b_main, m_conv_w, m_conv_b[None, :], s)
    ccol, crel_k, crel_q = _fox_cumsum(gates, bsz, s)
    hm = _mlstm(mq, mk, mv, mo, gates, m_norm_g, bsz, s)
    hf = _fox_attention(fq, fk, fv, ccol, crel_k, crel_q, bsz, s)

    n_exp = w_router.shape[1]
    wr = jnp.zeros((d, V7X_LANES), F32).at[:, :n_exp].set(w_router)
    wr_hi = wr.astype(BF16)
    wr_lo = (wr - wr_hi.astype(F32)).astype(BF16)
    br = jnp.zeros((1, V7X_LANES), F32).at[0, :n_exp].set(b_router)
    h1, h1p, gate, tope, tile_counts = _merge(
        dn_alpha, hm, hf, x2, w_gate, b_gate, w_bm.astype(BF16), w_bf.astype(BF16), w_o.astype(BF16),
        ln1_g[None, :], ln1_b[None, :], wr_hi, wr_lo, br)

    n_blocks = -(-(t * TOP_K) // MOE_BLOCK) + N_EXPERTS
    dest, table = _routing(tile_counts, tope, n_blocks)
    dest_km = dest[:TOP_K].reshape(TOP_K * t)
    block_e, block_valid = table[:, 0], table[:, 1]
    xs = _sc_dispatch(dest_km, h1p, n_blocks * MOE_BLOCK)
    y_rows = _experts(block_e, block_valid, xs, w_gu, b_gu[:, None, :], w_dn, b_dn[:, None, :])
    yg = _sc_gather(dest_km, y_rows).reshape(TOP_K, t, d // 2)
    out = _combine(dn_alpha, h1, gate, ln2_g[None, :], ln2_b[None, :], yg)
    return out.reshape(bsz, s, d)


def kernel(x, w_in, b_in, m_conv_w, m_conv_b, m_norm_g, w_bm, w_bf, w_o, ln1_g, ln1_b,
           w_router, b_router, w_gu, b_gu, w_dn, b_dn, ln2_g, ln2_b):
    depth = w_in.shape[0]
    h = x
    for l in range(depth):
        h = _layer(h, depth, l, w_in, b_in[l], m_conv_w[l], m_conv_b[l], m_norm_g[l], w_bm[l], w_bf[l],
                   w_o[l], ln1_g[l], ln1_b[l], w_router[l], b_router[l], w_gu[l], b_gu[l], w_dn[l],
                   b_dn[l], ln2_g[l], ln2_b[l])
    return h
```
